```python
import jax, jax.numpy as jnp
from jax import lax
import numpy as np

D_MODEL = 1024
BATCH = 16
SEQ = 2048
DEPTH = 4

N_MIXERS = 2
WINDOWS = (128, 512, 2048)
DILATIONS = (1, 4, 16)
N_GROUPS = len(WINDOWS)
N_HEADS = 16
HEAD_DIM = D_MODEL // N_HEADS
ROPE_THETA = 10000.0
POOL_SIZES = (2, 4, 8, 16)
N_POOL_GROUPS = len(POOL_SIZES)
POOL_CH = D_MODEL // N_POOL_GROUPS
D_FF = ((8 * D_MODEL // 3 + 255) // 256) * 256
N_EXPERTS = 8
TOP_K = 2
D_FF_EXPERT = 7 * D_MODEL // 2
MOE_BLOCK = 256
RMS_EPS = 1e-6
MASK_VALUE = -1e30

kernel_name = 'hybrid_dilated_attn_pool_moe_trunk'


def rmsnorm(x, g):
    xf = x.astype(jnp.float32)
    y = xf * lax.rsqrt(jnp.mean(xf * xf, axis=-1, keepdims=True) + RMS_EPS)
    return (y * g.astype(jnp.float32)).astype(x.dtype)


def rope_tables(seq):
    pos = jnp.arange(seq, dtype=jnp.float32)
    inv_freq = ROPE_THETA ** (-jnp.arange(0, HEAD_DIM, 2, dtype=jnp.float32) / HEAD_DIM)
    ang = pos[:, None] * inv_freq[None, :]
    ang = jnp.concatenate([ang, ang], axis=-1)
    return jnp.cos(ang), jnp.sin(ang)


def apply_rope(t, cos, sin):
    half = HEAD_DIM // 2
    rot = jnp.concatenate([-t[..., half:], t[..., :half]], axis=-1)
    return t * cos[None, :, None, :] + rot * sin[None, :, None, :]


def dilated_window_attention(q, k, v, dilation, sub_window):
    B, S, H, Dh = q.shape
    d, W = dilation, sub_window
    L = S // d
    nb = -(-L // W)
    Lp = nb * W

    def to_sub(t):
        return t.reshape(B, L, d, H, Dh).transpose(0, 2, 3, 1, 4)

    qs = jnp.pad(to_sub(q), ((0, 0), (0, 0), (0, 0), (0, Lp - L), (0, 0)))
    qs = qs.reshape(B, d, H, nb, W, Dh)

    def key_blocks(t):
        tp = jnp.pad(to_sub(t), ((0, 0), (0, 0), (0, 0), (W, Lp - L), (0, 0)))
        tp = tp.reshape(B, d, H, nb + 1, W, Dh)
        return jnp.concatenate([tp[:, :, :, :-1], tp[:, :, :, 1:]], axis=4)

    kb = key_blocks(k)
    vb = key_blocks(v)
    scores = jnp.einsum('bdhnqe,bdhnke->bdhnqk', qs, kb)

    qi = jnp.arange(W)[None, :, None]
    kj = jnp.arange(2 * W)[None, None, :]
    blk = jnp.arange(nb)[:, None, None]
    dist = qi + W - kj
    valid = (dist >= 0) & (dist <= W) & (blk * W + kj - W >= 0)
    scores = jnp.where(valid, scores, MASK_VALUE)

    lse = jax.nn.logsumexp(scores, axis=-1)
    p = jnp.exp(scores - lse[..., None])
    o = jnp.einsum('bdhnqk,bdhnke->bdhnqe', p, vb)

    o = o.reshape(B, d, H, Lp, Dh)[:, :, :, :L]
    lse = lse.reshape(B, d, H, Lp)[:, :, :, :L]
    o = o.transpose(0, 3, 1, 2, 4).reshape(B, S, H, Dh)
    lse = lse.transpose(0, 3, 1, 2).reshape(B, S, H)
    return o, lse


def dilated_attention_mixer(h, w_qkv, w_o, cos, sin):
    B, S, D = h.shape
    qkv = (h @ w_qkv).astype(jnp.float32).reshape(B, S, N_GROUPS, 3, N_HEADS, HEAD_DIM)
    outs, lses = [], []
    for g in range(N_GROUPS):
        q = apply_rope(qkv[:, :, g, 0], cos, sin) * (HEAD_DIM ** -0.5)
        k = apply_rope(qkv[:, :, g, 1], cos, sin)
        v = qkv[:, :, g, 2]
        o, lse = dilated_window_attention(q, k, v, DILATIONS[g], WINDOWS[g] // DILATIONS[g])
        outs.append(o)
        lses.append(lse)
    wts = jax.nn.softmax(jnp.stack(lses, axis=0), axis=0)
    o = jnp.sum(wts[..., None] * jnp.stack(outs, axis=0), axis=0)
    return o.reshape(B, S, D).astype(h.dtype) @ w_o


def multiscale_pool_mixer(h, w_group, scale):
    B, S, D = h.shape
    hf = h.astype(jnp.float32)
    csum = jnp.concatenate([jnp.zeros((B, 1, D), jnp.float32), jnp.cumsum(hf, axis=1)], axis=1)
    t = jnp.arange(S)
    outs = []
    for g, P in enumerate(POOL_SIZES):
        sl = slice(g * POOL_CH, (g + 1) * POOL_CH)
        lo = jnp.maximum(t + 1 - P, 0)
        cnt = (t + 1 - lo).astype(jnp.float32)
        c_g = csum[:, :, sl]
        win_sum = c_g[:, 1:] - jnp.take(c_g, lo, axis=1)
        outs.append(win_sum / cnt[None, :, None] - hf[..., sl])
    y = jnp.stack(outs, axis=2)
    y = jnp.einsum('bsgc,gcd->bsgd', y, w_group.astype(jnp.float32)).reshape(B, S, D)
    return (y * scale.astype(jnp.float32)).astype(h.dtype)


def swiglu(h, w_gate, w_up, w_down):
    return (jax.nn.silu(h @ w_gate) * (h @ w_up)) @ w_down


def moe_swiglu(h, w_router, w_gate, w_up, w_down):
    B, S, D = h.shape
    N = B * S
    NK = N * TOP_K
    hf = h.reshape(N, D)
    logits = (hf @ w_router).astype(jnp.float32)
    top_val, top_idx = lax.top_k(logits, TOP_K)
    gates = jax.nn.softmax(top_val, axis=-1)

    flat_e = top_idx.reshape(NK).astype(jnp.int32)
    flat_t = jnp.repeat(jnp.arange(N, dtype=jnp.int32), TOP_K)
    flat_g = gates.reshape(NK)
    order = jnp.argsort(flat_e)
    e_s, t_s, g_s = flat_e[order], flat_t[order], flat_g[order]

    counts = jnp.bincount(flat_e, length=N_EXPERTS)
    starts = jnp.cumsum(counts) - counts
    padded = ((counts + MOE_BLOCK - 1) // MOE_BLOCK) * MOE_BLOCK
    pends = jnp.cumsum(padded)
    pstarts = pends - padded
    dest = pstarts[e_s] + (jnp.arange(NK, dtype=jnp.int32) - starts[e_s])

    n_blocks = -(-NK // MOE_BLOCK) + N_EXPERTS
    cap = n_blocks * MOE_BLOCK
    buf_tok = jnp.zeros((cap,), jnp.int32).at[dest].set(t_s)
    buf_gate = jnp.zeros((cap,), jnp.float32).at[dest].set(g_s)
    block_e = jnp.minimum(
        jnp.searchsorted(pends, jnp.arange(n_blocks) * MOE_BLOCK, side='right'),
        N_EXPERTS - 1).astype(jnp.int32)

    def run_block(args):
        tok, gate, e = args
        y = swiglu(hf[tok], w_gate[e], w_up[e], w_down[e])
        return y * gate[:, None].astype(y.dtype)

    y = lax.map(run_block, (buf_tok.reshape(n_blocks, MOE_BLOCK),
                            buf_gate.reshape(n_blocks, MOE_BLOCK), block_e))
    out = jnp.zeros((N, D), h.dtype).at[buf_tok].add(y.reshape(cap, D).astype(h.dtype))
    return out.reshape(B, S, D)


def setup_inputs(seed: int = 0) -> dict:
    key = jax.random.key(seed)
    ks = jax.random.split(key, 16)
    n_even = (DEPTH + 1) // 2
    n_odd = DEPTH // 2
    f32 = jnp.float32

    def nrm(k, shape, fan_in):
        return jax.random.normal(k, shape, f32) * (fan_in ** -0.5)

    def gain(k, shape):
        return 1.0 + 0.1 * jax.random.normal(k, shape, f32)

    return {
        'x': jax.random.normal(ks[0], (BATCH, SEQ, D_MODEL), f32),
        'mix_norm': gain(ks[1], (DEPTH, D_MODEL)),
        'ffn_norm': gain(ks[2], (DEPTH, D_MODEL)),
        'attn_w_qkv': nrm(ks[3], (n_even, D_MODEL, N_GROUPS * 3 * D_MODEL), D_MODEL),
        'attn_w_o': nrm(ks[4], (n_even, D_MODEL, D_MODEL), D_MODEL),
        'pool_w': nrm(ks[5], (n_odd, N_POOL_GROUPS, POOL_CH, POOL_CH), POOL_CH),
        'pool_scale': gain(ks[6], (n_odd, D_MODEL)),
        'dense_w_gate': nrm(ks[7], (n_even, D_MODEL, D_FF), D_MODEL),
        'dense_w_up': nrm(ks[8], (n_even, D_MODEL, D_FF), D_MODEL),
        'dense_w_down': nrm(ks[9], (n_even, D_FF, D_MODEL), D_FF),
        'moe_router': nrm(ks[10], (n_odd, D_MODEL, N_EXPERTS), D_MODEL),
        'moe_w_gate': nrm(ks[11], (n_odd, N_EXPERTS, D_MODEL, D_FF_EXPERT), D_MODEL),
        'moe_w_up': nrm(ks[12], (n_odd, N_EXPERTS, D_MODEL, D_FF_EXPERT), D_MODEL),
        'moe_w_down': nrm(ks[13], (n_odd, N_EXPERTS, D_FF_EXPERT, D_MODEL), D_FF_EXPERT),
        'final_norm': gain(ks[14], (D_MODEL,)),
    }


def reference(x, mix_norm, ffn_norm, attn_w_qkv, attn_w_o, pool_w, pool_scale,
              dense_w_gate, dense_w_up, dense_w_down, moe_router, moe_w_gate,
              moe_w_up, moe_w_down, final_norm):
    cos, sin = rope_tables(x.shape[1])
    for i in range(DEPTH):
        h = rmsnorm(x, mix_norm[i])
        j = i // N_MIXERS
        if i % N_MIXERS == 0:
            x = x + dilated_attention_mixer(h, attn_w_qkv[j], attn_w_o[j], cos, sin)
        else:
            x = x + multiscale_pool_mixer(h, pool_w[j], pool_scale[j])
        h = rmsnorm(x, ffn_norm[i])
        f = i // 2
        if i % 2 == 0:
            x = x + swiglu(h, dense_w_gate[f], dense_w_up[f], dense_w_down[f])
        else:
            x = x + moe_swiglu(h, moe_router[f], moe_w_gate[f], moe_w_up[f], moe_w_down[f])
    return rmsnorm(x, final_norm)
```

```python
import functools

import jax
import jax.numpy as jnp
from jax import lax
from jax.experimental import pallas as pl
from jax.experimental.pallas import tpu as pltpu

WINDOWS = (128, 512, 2048)
DILATIONS = (1, 4, 16)
N_GROUPS = len(WINDOWS)
N_HEADS = 16
HEAD_DIM = 64
ROPE_THETA = 10000.0
POOL_SIZES = (2, 4, 8, 16)
N_EXPERTS = 8
TOP_K = 2
RMS_EPS = 1e-6
MASK_VALUE = -1e30

LANES = 128
MXU_DIM = 256
VMEM_LIMIT_BYTES = 56 * 1024 * 1024

F32 = jnp.float32
BF16 = jnp.bfloat16


def _params(*sem):
    return pltpu.CompilerParams(dimension_semantics=sem, vmem_limit_bytes=VMEM_LIMIT_BYTES)


def _rms(x, g):
    ms = jnp.mean(x * x, axis=-1, keepdims=True)
    return x * lax.rsqrt(ms + RMS_EPS) * g


def _silu(g):
    return g / (1.0 + jnp.exp(-g))


def _qkv_kernel(x_ref, g_ref, w_ref, cos_ref, sina_ref, sinb_ref, o_ref, h_ref):
    j = pl.program_id(1)

    @pl.when(j == 0)
    def _():
        h_ref[...] = _rms(x_ref[...], g_ref[...]).astype(BF16)

    acc = jnp.dot(h_ref[...], w_ref[...], preferred_element_type=F32)
    kind = j % 3

    @pl.when(kind == 2)
    def _():
        o_ref[...] = acc.astype(BF16)

    @pl.when(kind != 2)
    def _():
        scale = jnp.where(kind == 0, HEAD_DIM ** -0.5, 1.0).astype(F32)
        cos = cos_ref[...]
        sina = sina_ref[...]
        sinb = sinb_ref[...]
        for c in range(acc.shape[1] // LANES):
            t = acc[:, c * LANES:(c + 1) * LANES]
            r = (t * cos + pltpu.roll(t, LANES - HEAD_DIM // 2, 1) * sina
                 + pltpu.roll(t, HEAD_DIM // 2, 1) * sinb)
            o_ref[:, c * LANES:(c + 1) * LANES] = (r * scale).astype(BF16)


def _qkv_rope(x, gain, w_qkv, cos, sina, sinb, seq):
    n, d = x.shape
    tm = 1024
    nj = w_qkv.shape[1] // d
    tiles_per_seq = seq // tm
    tab = pl.BlockSpec((tm, LANES), lambda i, j: (i % tiles_per_seq, 0))
    return pl.pallas_call(
        _qkv_kernel,
        grid=(n // tm, nj),
        in_specs=[
            pl.BlockSpec((tm, d), lambda i, j: (i, 0)),
            pl.BlockSpec((1, d), lambda i, j: (0, 0)),
            pl.BlockSpec((d, d), lambda i, j: (0, j)),
            tab, tab, tab,
        ],
        out_specs=pl.BlockSpec((tm, d), lambda i, j: (i, j)),
        out_shape=jax.ShapeDtypeStruct((n, w_qkv.shape[1]), BF16),
        scratch_shapes=[pltpu.VMEM((tm, d), BF16)],
        compiler_params=_params("parallel", "arbitrary"),
        name="qkv_rope",
    )(x, gain, w_qkv, cos, sina, sinb)


def _attn_kernel(*refs, has_prev, merge):
    it = iter(refs)
    q_ref, kc_ref, vc_ref = next(it), next(it), next(it)
    kp_ref = vp_ref = op_ref = lp_ref = None
    if has_prev:
        kp_ref, vp_ref = next(it), next(it)
    if merge:
        op_ref, lp_ref = next(it), next(it)
    o_ref, l_ref = next(it), next(it)

    n = pl.program_id(2)
    w = q_ref.shape[0]
    q = q_ref[...]
    if has_prev:
        keys = jnp.concatenate([kp_ref[...], kc_ref[...]], axis=0)
        vals = jnp.concatenate([vp_ref[...], vc_ref[...]], axis=0)
    else:
        keys = kc_ref[...]
        vals = vc_ref[...]
    nk = keys.shape[0]
    row = lax.broadcasted_iota(jnp.int32, (w, nk), 0)
    col = lax.broadcasted_iota(jnp.int32, (w, nk), 1)
    if has_prev:
        dist = row + w - col
        valid = (dist >= 0) & (dist <= w) & ((col >= w) | (n > 0))
    else:
        valid = row >= col
    lane = lax.broadcasted_iota(jnp.int32, (1, LANES), 1)
    first_head = lane < HEAD_DIM
    lse_tile = jnp.zeros((w, LANES), F32)

    for hp in range(q.shape[1] // LANES):
        sl = slice(hp * LANES, (hp + 1) * LANES)
        qp, kp, vp = q[:, sl], keys[:, sl], vals[:, sl]
        if merge:
            o_prev = op_ref[:, sl].astype(F32)
            lse_prev = lp_ref[...]
        outs = []
        for sub in range(2):
            head = 2 * hp + sub
            qm = jnp.where(first_head if sub == 0 else jnp.logical_not(first_head), qp, jnp.zeros_like(qp))
            s = lax.dot_general(qm, kp, (((1,), (1,)), ((), ())), preferred_element_type=F32)
            s = jnp.where(valid, s, MASK_VALUE)
            mx = jnp.max(s, axis=1, keepdims=True)
            p = jnp.exp(s - mx)
            den = jnp.sum(p, axis=1, keepdims=True)
            o = jnp.dot(p.astype(BF16), vp, preferred_element_type=F32) / den
            lse = mx + jnp.log(den)
            if merge:
                lp = lse_prev[:, head:head + 1]
                top = jnp.maximum(lp, lse)
                new = top + jnp.log(jnp.exp(lp - top) + jnp.exp(lse - top))
                o = jnp.exp(lp - new) * o_prev + jnp.exp(lse - new) * o
                lse = new
            outs.append(o)
            lse_tile = jnp.where(lane == head, lse, lse_tile)
        o_ref[:, sl] = jnp.where(first_head, outs[0], outs[1]).astype(BF16)
    l_ref[...] = lse_tile


def _attn_group(qkv, g, prev, batch, seq):
    d_model = N_HEADS * HEAD_DIM
    dil = DILATIONS[g]
    w = WINDOWS[g] // dil
    sub_len = seq // dil
    nb = sub_len // w
    ncol = qkv.shape[1] // d_model
    has_prev = nb > 1
    merge = prev is not None

    qkv_v = qkv.reshape(batch, sub_len, dil * qkv.shape[1])

    def col_spec(which, shift):
        def imap(b, r, n):
            return (b, jnp.maximum(n - shift, 0), r * ncol + g * 3 + which)
        return pl.BlockSpec((None, w, d_model), imap)

    in_specs = [col_spec(0, 0), col_spec(1, 0), col_spec(2, 0)]
    args = [qkv_v, qkv_v, qkv_v]
    if has_prev:
        in_specs += [col_spec(1, 1), col_spec(2, 1)]
        args += [qkv_v, qkv_v]
    o_spec = pl.BlockSpec((None, w, d_model), lambda b, r, n: (b, n, r))
    l_spec = pl.BlockSpec((None, w, LANES), lambda b, r, n: (b, n, r))
    if merge:
        in_specs += [o_spec, l_spec]
        args += [prev[0].reshape(batch, sub_len, dil * d_model), prev[1].reshape(batch, sub_len, dil * LANES)]
    o, lse = pl.pallas_call(
        functools.partial(_attn_kernel, has_prev=has_prev, merge=merge),
        grid=(batch, dil, nb),
        in_specs=in_specs,
        out_specs=[o_spec, l_spec],
        out_shape=[jax.ShapeDtypeStruct((batch, sub_len, dil * d_model), BF16),
                   jax.ShapeDtypeStruct((batch, sub_len, dil * LANES), F32)],
        compiler_params=_params("parallel", "parallel", "arbitrary"),
        name=f"attn_g{g}",
    )(*args)
    return o.reshape(batch * seq, d_model), lse.reshape(batch * seq, LANES)


def _proj_kernel(x_ref, o_ref, w_ref, y_ref):
    y_ref[...] = x_ref[...] + jnp.dot(o_ref[...], w_ref[...], preferred_element_type=F32)


def _out_proj(x, o, w_o):
    n, d = x.shape
    tm = 1024
    row = pl.BlockSpec((tm, d), lambda i: (i, 0))
    return pl.pallas_call(
        _proj_kernel,
        grid=(n // tm,),
        in_specs=[row, row, pl.BlockSpec((d, d), lambda i: (0, 0))],
        out_specs=row,
        out_shape=jax.ShapeDtypeStruct((n, d), F32),
        compiler_params=_params("parallel"),
        name="out_proj",
    )(x, o, w_o)


def _dense_kernel(x_ref, g_ref, wg_ref, wu_ref, wd_ref, y_ref):
    x = x_ref[...]
    h = _rms(x, g_ref[...]).astype(BF16)
    gate = jnp.dot(h, wg_ref[...], preferred_element_type=F32)
    up = jnp.dot(h, wu_ref[...], preferred_element_type=F32)
    a = (_silu(gate) * up).astype(BF16)
    y_ref[...] = x + jnp.dot(a, wd_ref[...], preferred_element_type=F32)


def _dense_ffn(x, gain, w_gate, w_up, w_down):
    n, d = x.shape
    ff = w_gate.shape[1]
    tm = 256
    row = pl.BlockSpec((tm, d), lambda i: (i, 0))
    const = lambda shape: pl.BlockSpec(shape, lambda i: (0, 0), pipeline_mode=pl.Buffered(1))
    return pl.pallas_call(
        _dense_kernel,
        grid=(n // tm,),
        in_specs=[row, const((1, d)), const((d, ff)), const((d, ff)), const((ff, d))],
        out_specs=row,
        out_shape=jax.ShapeDtypeStruct((n, d), F32),
        compiler_params=_params("parallel"),
        name="dense_ffn",
    )(x, gain, w_gate, w_up, w_down)


POOL_HALO = max(POOL_SIZES)


def _pool_kernel(x_ref, halo_ref, g_ref, w_ref, sc_ref, y_ref):
    i = pl.program_id(1)
    ts = x_ref.shape[0]
    g = g_ref[...]
    x = x_ref[...]
    h = _rms(x, g)
    hh = jnp.where(i > 0, _rms(halo_ref[...], g), 0.0)
    ext = jnp.concatenate([hh, h], axis=0)
    pos = i * ts + lax.broadcasted_iota(jnp.int32, (ts, 1), 0)
    pc = w_ref.shape[1]
    for grp, size in enumerate(POOL_SIZES):
        sl = slice(grp * pc, (grp + 1) * pc)
        s = ext[:, sl]
        step = 1
        while step < size:
            s = s + pltpu.roll(s, step, 0)
            step *= 2
        cnt = jnp.minimum(pos + 1, size).astype(F32)
        y = s[POOL_HALO:, :] / cnt - h[:, sl]
        z = jnp.dot(y.astype(BF16), w_ref[grp], preferred_element_type=F32)
        y_ref[:, sl] = x[:, sl] + z * sc_ref[:, sl]


def _pool_mixer(x, gain, pool_w, scale, batch, seq):
    n, d = x.shape
    ts = 512
    x3 = x.reshape(batch, seq, d)
    hb = ts // POOL_HALO
    out = pl.pallas_call(
        _pool_kernel,
        grid=(batch, seq // ts),
        in_specs=[
            pl.BlockSpec((None, ts, d), lambda b, i: (b, i, 0)),
            pl.BlockSpec((None, POOL_HALO, d), lambda b, i: (b, jnp.maximum(i * hb - 1, 0), 0)),
            pl.BlockSpec((1, d), lambda b, i: (0, 0)),
            pl.BlockSpec(pool_w.shape, lambda b, i: (0, 0, 0)),
            pl.BlockSpec((1, d), lambda b, i: (0, 0)),
        ],
        out_specs=pl.BlockSpec((None, ts, d), lambda b, i: (b, i, 0)),
        out_shape=jax.ShapeDtypeStruct((batch, seq, d), F32),
        compiler_params=_params("parallel", "arbitrary"),
        name="pool_mixer",
    )(x3, x3, gain, pool_w, scale)
    return out.reshape(n, d)


ROUTE_COLS = 8


def _router_kernel(x_ref, g_ref, wr_ref, tril_ref, ri_ref, rg_ref, cnt_ref, run_ref):
    i = pl.program_id(0)

    @pl.when(i == 0)
    def _():
        run_ref[...] = jnp.zeros_like(run_ref)

    h = _rms(x_ref[...], g_ref[...])
    logits = jnp.dot(h, wr_ref[...], preferred_element_type=F32, precision=lax.Precision.HIGHEST)
    tm, ne = logits.shape
    lane = lax.broadcasted_iota(jnp.int32, (tm, ne), 1)
    v1 = jnp.max(logits, axis=1, keepdims=True)
    i1 = jnp.min(jnp.where(logits == v1, lane, ne), axis=1, keepdims=True)
    rest = jnp.where(lane == i1, -jnp.inf, logits)
    v2 = jnp.max(rest, axis=1, keepdims=True)
    i2 = jnp.min(jnp.where(rest == v2, lane, ne), axis=1, keepdims=True)
    e = jnp.exp(v2 - v1)
    g1 = 1.0 / (1.0 + e)
    g2 = e / (1.0 + e)
    oh1 = (lane == i1).astype(F32)
    oh2 = (lane == i2).astype(F32)
    both = oh1 + oh2
    before = jnp.dot(tril_ref[...], both.astype(BF16), preferred_element_type=F32) + run_ref[...]
    r1 = jnp.sum(before * oh1, axis=1, keepdims=True).astype(jnp.int32)
    r2 = jnp.sum(before * oh2, axis=1, keepdims=True).astype(jnp.int32)
    run_ref[...] += jnp.sum(both, axis=0, keepdims=True)
    ri_ref[...] = jnp.where(lane == 0, i1, jnp.where(lane == 1, i2, jnp.where(lane == 2, r1, r2)))
    rg_ref[...] = jnp.where(lane == 0, g1, g2)
    cnt_ref[...] = run_ref[...]


def _router(x, gain, w_router):
    n, d = x.shape
    ne = w_router.shape[1]
    tm = 512
    tril = (jnp.arange(tm)[:, None] > jnp.arange(tm)[None, :]).astype(BF16)
    return pl.pallas_call(
        _router_kernel,
        grid=(n // tm,),
        in_specs=[
            pl.BlockSpec((tm, d), lambda i: (i, 0)),
            pl.BlockSpec((1, d), lambda i: (0, 0)),
            pl.BlockSpec((d, ne), lambda i: (0, 0)),
            pl.BlockSpec((tm, tm), lambda i: (0, 0)),
        ],
        out_specs=[
            pl.BlockSpec((tm, ROUTE_COLS), lambda i: (i, 0)),
            pl.BlockSpec((tm, ROUTE_COLS), lambda i: (i, 0)),
            pl.BlockSpec((1, ne), lambda i: (0, 0)),
        ],
        out_shape=[
            jax.ShapeDtypeStruct((n, ROUTE_COLS), jnp.int32),
            jax.ShapeDtypeStruct((n, ROUTE_COLS), F32),
            jax.ShapeDtypeStruct((1, ne), F32),
        ],
        scratch_shapes=[pltpu.VMEM((1, ne), F32)],
        compiler_params=_params("arbitrary"),
        name="moe_router",
    )(x, gain, w_router, tril)


def _dispatch_kernel(dest_ref, x_ref, g_ref, xs_in_ref, xs_ref, h_ref, sem):
    del xs_in_ref
    i = pl.program_id(0)
    tm = h_ref.shape[0]
    h_ref[...] = _rms(x_ref[...], g_ref[...])

    def row_copy(t, d):
        return pltpu.make_async_copy(h_ref.at[pl.ds(t, 1)], xs_ref.at[pl.ds(d, 1)], sem)

    def start(t, c):
        base = (i * tm + t) * TOP_K
        for k in range(TOP_K):
            row_copy(t, dest_ref[base + k]).start()
        return c

    lax.fori_loop(0, tm, start, 0)

    def wait(t, c):
        for k in range(TOP_K):
            row_copy(0, 0).wait()
        return c

    lax.fori_loop(0, tm, wait, 0)


def _dispatch(x, gain, dest_flat, cap):
    n, d = x.shape
    tm = 256
    xs_init = jnp.zeros((cap, d), F32)
    return pl.pallas_call(
        _dispatch_kernel,
        grid_spec=pltpu.PrefetchScalarGridSpec(
            num_scalar_prefetch=1,
            grid=(n // tm,),
            in_specs=[
                pl.BlockSpec((tm, d), lambda i, dest: (i, 0)),
                pl.BlockSpec((1, d), lambda i, dest: (0, 0)),
                pl.BlockSpec(memory_space=pl.ANY),
            ],
            out_specs=pl.BlockSpec(memory_space=pl.ANY),
            scratch_shapes=[pltpu.VMEM((tm, d), F32), pltpu.SemaphoreType.DMA],
        ),
        out_shape=jax.ShapeDtypeStruct((cap, d), F32),
        input_output_aliases={3: 0},
        compiler_params=_params("arbitrary"),
        name="moe_dispatch",
    )(dest_flat, x, gain, xs_init)


def _expert_kernel(be_ref, nv_ref, xs_ref, wg_ref, wu_ref, wd_ref, ys_ref, xb_ref):
    del be_ref
    b = pl.program_id(0)
    j = pl.program_id(1)

    @pl.when(b < nv_ref[0])
    def _():
        @pl.when(j == 0)
        def _():
            xb_ref[...] = xs_ref[...].astype(BF16)

        x = xb_ref[...]
        gate = jnp.dot(x, wg_ref[...], preferred_element_type=F32)
        up = jnp.dot(x, wu_ref[...], preferred_element_type=F32)
        a = (_silu(gate) * up).astype(BF16)
        part = jnp.dot(a, wd_ref[...], preferred_element_type=F32)

        @pl.when(j == 0)
        def _():
            ys_ref[...] = part

        @pl.when(j > 0)
        def _():
            ys_ref[...] += part

    @pl.when((b >= nv_ref[0]) & (j == 0))
    def _():
        ys_ref[...] = jnp.zeros_like(ys_ref)


def _experts(xs, block_e, n_valid, w_gate, w_up, w_down, tm, tf):
    cap, d = xs.shape
    ff = w_gate.shape[2]
    nj = ff // tf

    def blk(b, nv):
        return jnp.minimum(b, nv[0] - 1)

    def ffi(b, j, nv):
        return jnp.where(b < nv[0], j, nj - 1)

    return pl.pallas_call(
        _expert_kernel,
        grid_spec=pltpu.PrefetchScalarGridSpec(
            num_scalar_prefetch=2,
            grid=(cap // tm, nj),
            in_specs=[
                pl.BlockSpec((tm, d), lambda b, j, be, nv: (blk(b, nv), 0)),
                pl.BlockSpec((None, d, tf), lambda b, j, be, nv: (be[blk(b, nv)], 0, ffi(b, j, nv))),
                pl.BlockSpec((None, d, tf), lambda b, j, be, nv: (be[blk(b, nv)], 0, ffi(b, j, nv))),
                pl.BlockSpec((None, tf, d), lambda b, j, be, nv: (be[blk(b, nv)], ffi(b, j, nv), 0)),
            ],
            out_specs=pl.BlockSpec((tm, d), lambda b, j, be, nv: (b, 0)),
            scratch_shapes=[pltpu.VMEM((tm, d), BF16)],
        ),
        out_shape=jax.ShapeDtypeStruct((cap, d), F32),
        compiler_params=_params("arbitrary", "arbitrary"),
        name="moe_experts",
    )(block_e, n_valid, xs, w_gate, w_up, w_down)


def _combine_kernel(dest_ref, x_ref, rg_ref, gf_ref, ys_ref, o_ref, y0_ref, y1_ref, sem, *, final_norm):
    i = pl.program_id(0)
    tm = x_ref.shape[0]
    bufs = (y0_ref, y1_ref)

    def row_copy(t, d, k):
        return pltpu.make_async_copy(ys_ref.at[pl.ds(d, 1)], bufs[k].at[pl.ds(t, 1)], sem)

    def start(t, c):
        base = (i * tm + t) * TOP_K
        for k in range(TOP_K):
            row_copy(t, dest_ref[base + k], k).start()
        return c

    lax.fori_loop(0, tm, start, 0)

    def wait(t, c):
        for k in range(TOP_K):
            row_copy(0, 0, k).wait()
        return c

    lax.fori_loop(0, tm, wait, 0)
    rg = rg_ref[...]
    out = x_ref[...] + (rg[:, 0:1] * y0_ref[...] + rg[:, 1:2] * y1_ref[...])
    if final_norm:
        out = _rms(out, gf_ref[...])
    o_ref[...] = out


def _combine(x, route_g, ys, dest_flat, final_gain, final_norm):
    n, d = x.shape
    tm = 256
    return pl.pallas_call(
        functools.partial(_combine_kernel, final_norm=final_norm),
        grid_spec=pltpu.PrefetchScalarGridSpec(
            num_scalar_prefetch=1,
            grid=(n // tm,),
            in_specs=[
                pl.BlockSpec((tm, d), lambda i, dest: (i, 0)),
                pl.BlockSpec((tm, ROUTE_COLS), lambda i, dest: (i, 0)),
                pl.BlockSpec((1, d), lambda i, dest: (0, 0)),
                pl.BlockSpec(memory_space=pl.ANY),
            ],
            out_specs=pl.BlockSpec((tm, d), lambda i, dest: (i, 0)),
            scratch_shapes=[pltpu.VMEM((tm, d), F32), pltpu.VMEM((tm, d), F32), pltpu.SemaphoreType.DMA],
        ),
        out_shape=jax.ShapeDtypeStruct((n, d), F32),
        compiler_params=_params("arbitrary"),
        name="moe_combine",
    )(dest_flat, x, route_g, final_gain, ys)


def _moe_layer(x, gain, w_router, w_gate, w_up, w_down, final_gain, final_norm):
    n, d = x.shape
    tm = 1024 if n * TOP_K >= 8 * 1024 else 256
    tf = 2 * MXU_DIM
    route_i, route_g, counts = _router(x, gain, w_router)

    counts = counts[0].astype(jnp.int32)
    padded = ((counts + tm - 1) // tm) * tm
    ends = jnp.cumsum(padded)
    starts = ends - padded
    dest = starts[route_i[:, :TOP_K]] + route_i[:, TOP_K:2 * TOP_K]
    dest_flat = dest.reshape(n * TOP_K)
    n_blocks = (n * TOP_K) // tm + N_EXPERTS
    block_e = jnp.minimum(
        jnp.searchsorted(ends, jnp.arange(n_blocks, dtype=jnp.int32) * tm, side="right"),
        N_EXPERTS - 1).astype(jnp.int32)
    n_valid = (ends[-1:] // tm).astype(jnp.int32)

    xs = _dispatch(x, gain, dest_flat, n_blocks * tm)
    ys = _experts(xs, block_e, n_valid, w_gate, w_up, w_down, tm, tf)
    return _combine(x, route_g, ys, dest_flat, final_gain, final_norm)


def _rope_tables(seq):
    pos = jnp.arange(seq, dtype=F32)
    inv_freq = ROPE_THETA ** (-jnp.arange(0, HEAD_DIM, 2, dtype=F32) / HEAD_DIM)
    ang = pos[:, None] * inv_freq[None, :]
    reps = LANES // (HEAD_DIM // 2)
    cos = jnp.tile(jnp.cos(ang), (1, reps))
    sin = jnp.tile(jnp.sin(ang), (1, reps))
    first_half = (jnp.arange(LANES) % HEAD_DIM) < HEAD_DIM // 2
    return cos, jnp.where(first_half, -sin, 0.0), jnp.where(first_half, 0.0, sin)


def kernel(x, mix_norm, ffn_norm, attn_w_qkv, attn_w_o, pool_w, pool_scale, dense_w_gate, dense_w_up,
           dense_w_down, moe_router, moe_w_gate, moe_w_up, moe_w_down, final_norm):
    batch, seq, d = x.shape
    depth = mix_norm.shape[0]
    assert depth % 2 == 0 and d == N_HEADS * HEAD_DIM
    cos, sina, sinb = _rope_tables(seq)
    xf = x.reshape(batch * seq, d)
    final_gain = final_norm.reshape(1, d)
    for i in range(depth):
        j = i // 2
        mix_gain = mix_norm[i].reshape(1, d)
        ffn_gain = ffn_norm[i].reshape(1, d)
        if i % 2 == 0:
            qkv = _qkv_rope(xf, mix_gain, attn_w_qkv[j].astype(BF16), cos, sina, sinb, seq)
            acc = None
            for g in range(N_GROUPS):
                acc = _attn_group(qkv, g, acc, batch, seq)
            xf = _out_proj(xf, acc[0], attn_w_o[j].astype(BF16))
            xf = _dense_ffn(xf, ffn_gain, dense_w_gate[j].astype(BF16), dense_w_up[j].astype(BF16),
                            dense_w_down[j].astype(BF16))
        else:
            xf = _pool_mixer(xf, mix_gain, pool_w[j].astype(BF16), pool_scale[j].reshape(1, d), batch, seq)
            xf = _moe_layer(xf, ffn_gain, moe_router[j], moe_w_gate[j].astype(BF16), moe_w_up[j].astype(BF16),
                            moe_w_down[j].astype(BF16), final_gain, i == depth - 1)
    return xf.reshape(batch, seq, d)
```

```python
import functools

import jax
import jax.numpy as jnp
from jax import lax
from jax.experimental import pallas as pl
from jax.experimental.pallas import tpu as pltpu

WINDOWS = (128, 512, 2048)
DILATIONS = (1, 4, 16)
N_GROUPS = len(WINDOWS)
N_HEADS = 16
HEAD_DIM = 64
ROPE_THETA = 10000.0
POOL_SIZES = (2, 4, 8, 16)
N_EXPERTS = 8
TOP_K = 2
RMS_EPS = 1e-6
MASK_VALUE = -1e30
ATTN_UNROLL = 4

LANES = 128
MXU_DIM = 256
VMEM_LIMIT_BYTES = 56 * 1024 * 1024

F32 = jnp.float32
BF16 = jnp.bfloat16


def _params(*sem):
    return pltpu.CompilerParams(dimension_semantics=sem, vmem_limit_bytes=VMEM_LIMIT_BYTES)


def _rms(x, g):
    ms = jnp.mean(x * x, axis=-1, keepdims=True)
    return x * lax.rsqrt(ms + RMS_EPS) * g


def _silu(g):
    return g / (1.0 + jnp.exp(-g))


def _qkv_kernel(x_ref, g_ref, w0_ref, w1_ref, w2_ref, cos_ref, sina_ref, sinb_ref,
                o0_ref, o1_ref, o2_ref, acc_ref):
    kind = pl.program_id(0)
    tm = x_ref.shape[0]
    h = _rms(x_ref[...], g_ref[...]).astype(BF16)
    scale = jnp.where(kind == 0, HEAD_DIM ** -0.5, 1.0).astype(F32)
    n_chunks = acc_ref.shape[0]
    for w_ref, o_ref, dil in zip((w0_ref, w1_ref, w2_ref), (o0_ref, o1_ref, o2_ref), DILATIONS):
        acc = jnp.dot(h, w_ref[...], preferred_element_type=F32)
        for c in range(n_chunks):
            acc_ref[c] = acc[:, c * LANES:(c + 1) * LANES]
        rows = tm // dil

        def strided(ref, r, dil=dil, rows=rows):
            if dil == 1:
                return ref[...]
            return ref[pl.ds(r, rows, stride=dil), :]

        @pl.when(kind == 2)
        def _(o_ref=o_ref, dil=dil, strided=strided):
            for r in range(dil):
                for c in range(n_chunks):
                    o_ref[r, :, c * LANES:(c + 1) * LANES] = strided(acc_ref.at[c], r).astype(BF16)

        @pl.when(kind != 2)
        def _(o_ref=o_ref, dil=dil, strided=strided):
            for r in range(dil):
                cos = strided(cos_ref, r)
                sina = strided(sina_ref, r)
                sinb = strided(sinb_ref, r)
                for c in range(n_chunks):
                    t = strided(acc_ref.at[c], r)
                    rot = (t * cos + pltpu.roll(t, LANES - HEAD_DIM // 2, 1) * sina
                           + pltpu.roll(t, HEAD_DIM // 2, 1) * sinb)
                    o_ref[r, :, c * LANES:(c + 1) * LANES] = (rot * scale).astype(BF16)


def _qkv_rope(x, gain, w_qkv, cos, sina, sinb, batch, seq):
    n, d = x.shape
    tm = 512
    tiles_per_seq = seq // tm
    tab = pl.BlockSpec((tm, LANES), lambda k, i: (i % tiles_per_seq, 0))

    def w_spec(g):
        return pl.BlockSpec((d, d), lambda k, i: (0, g * 3 + k))

    def o_spec(dil):
        return pl.BlockSpec((None, None, dil, tm // dil, d),
                            lambda k, i: (k, i // tiles_per_seq, 0, i % tiles_per_seq, 0))

    outs = pl.pallas_call(
        _qkv_kernel,
        grid=(3, n // tm),
        in_specs=[
            pl.BlockSpec((tm, d), lambda k, i: (i, 0)),
            pl.BlockSpec((1, d), lambda k, i: (0, 0)),
            w_spec(0), w_spec(1), w_spec(2),
            tab, tab, tab,
        ],
        out_specs=[o_spec(dil) for dil in DILATIONS],
        out_shape=[jax.ShapeDtypeStruct((3, batch, dil, seq // dil, d), BF16) for dil in DILATIONS],
        scratch_shapes=[pltpu.VMEM((d // LANES, tm, LANES), F32)],
        compiler_params=_params("parallel", "parallel"),
        name="qkv_rope",
    )(x, gain, w_qkv, w_qkv, w_qkv, cos, sina, sinb)
    return [o.reshape(3, batch, seq, d) for o in outs]


def _attn_kernel(q0, k0, v0, q1, k1, v1, q2, k2, v2, o_ref, num_ref, max_ref, den_ref, kt_ref):
    seq = o_ref.shape[0]
    w = WINDOWS[0] // DILATIONS[0]
    lane = lax.broadcasted_iota(jnp.int32, (1, LANES), 1)
    first_head = lane < HEAD_DIM
    row2 = lax.broadcasted_iota(jnp.int32, (2 * w, 2 * w), 0) % w
    col2 = lax.broadcasted_iota(jnp.int32, (2 * w, 2 * w), 1)
    dist = row2 + w - col2
    band = (dist >= 0) & (dist <= w)
    in_current = col2 >= w
    causal = (lax.broadcasted_iota(jnp.int32, (2 * w, w), 0) % w
              >= lax.broadcasted_iota(jnp.int32, (2 * w, w), 1))

    def tile(q_ref, v_ref, qoff, n, has_prev):
        q = q_ref[pl.ds(qoff, w), :]
        zero = jnp.zeros_like(q)
        qq = jnp.concatenate([jnp.where(first_head, q, zero), jnp.where(first_head, zero, q)], axis=0)
        if has_prev:
            poff = pl.multiple_of(jnp.maximum(qoff - w, 0), w)
            keys_t = jnp.concatenate([kt_ref[:, pl.ds(poff, w)], kt_ref[:, pl.ds(qoff, w)]], axis=1)
            vals = jnp.concatenate([v_ref[pl.ds(poff, w), :], v_ref[pl.ds(qoff, w), :]], axis=0)
            valid = band & (in_current | (n > 0))
        else:
            keys_t = kt_ref[:, pl.ds(qoff, w)]
            vals = v_ref[pl.ds(qoff, w), :]
            valid = causal
        s = jnp.dot(qq, keys_t, preferred_element_type=F32)
        s = jnp.where(valid, s, MASK_VALUE)
        mx = jnp.max(s, axis=1, keepdims=True)
        p = jnp.exp(s - mx)
        den = jnp.sum(p, axis=1, keepdims=True)
        pv = jnp.dot(p.astype(BF16), vals, preferred_element_type=F32)
        return tuple(jnp.where(first_head, a[:w], a[w:]) for a in (pv, mx, den))

    order = sorted(range(N_GROUPS), key=lambda g: -DILATIONS[g])
    for g in order:
        q_ref, k_ref, v_ref = ((q0, k0, v0), (q1, k1, v1), (q2, k2, v2))[g]
        dil = DILATIONS[g]
        sub_len = seq // dil
        nb = sub_len // w

        kt_ref[...] = k_ref[...].T

        def body(t, carry, q_ref=q_ref, v_ref=v_ref, g=g, dil=dil, sub_len=sub_len, nb=nb):
            r = t // nb
            n = t % nb
            qoff = pl.multiple_of(r * sub_len + n * w, w)
            pv, mx, den = tile(q_ref, v_ref, qoff, n, nb > 1)
            if dil == 1:
                rows = pl.ds(qoff, w)
            else:
                rows = pl.ds(r + dil * n * w, w, stride=dil)
            if g == order[0]:
                num_ref[rows, :] = pv
                max_ref[rows, :] = mx
                den_ref[rows, :] = den
                return carry
            old_max = max_ref[rows, :]
            new_max = jnp.maximum(old_max, mx)
            a = jnp.exp(old_max - new_max)
            b = jnp.exp(mx - new_max)
            num = a * num_ref[rows, :] + b * pv
            den = a * den_ref[rows, :] + b * den
            if g == order[-1]:
                o_ref[rows, :] = (num / den).astype(BF16)
            else:
                num_ref[rows, :] = num
                den_ref[rows, :] = den
                max_ref[rows, :] = new_max
            return carry

        lax.fori_loop(0, dil * nb, body, 0, unroll=ATTN_UNROLL)


def _attention(qkv_groups, batch, seq):
    d_model = N_HEADS * HEAD_DIM
    in_specs, args = [], []
    for arr in qkv_groups:
        for kind in range(3):
            in_specs.append(pl.BlockSpec((None, None, seq, LANES), lambda b, hp, kind=kind: (kind, b, 0, hp)))
            args.append(arr)
    return pl.pallas_call(
        _attn_kernel,
        grid=(batch, d_model // LANES),
        in_specs=in_specs,
        out_specs=pl.BlockSpec((None, seq, LANES), lambda b, hp: (b, 0, hp)),
        out_shape=jax.ShapeDtypeStruct((batch, seq, d_model), BF16),
        scratch_shapes=[pltpu.VMEM((seq, LANES), F32)] * 3 + [pltpu.VMEM((LANES, seq), BF16)],
        compiler_params=_params("parallel", "parallel"),
        name="attention",
    )(*args).reshape(batch * seq, d_model)


def _proj_kernel(x_ref, o_ref, w_ref, y_ref):
    y_ref[...] = x_ref[...] + jnp.dot(o_ref[...], w_ref[...], preferred_element_type=F32)


def _out_proj(x, o, w_o):
    n, d = x.shape
    tm = 1024
    row = pl.BlockSpec((tm, d), lambda i: (i, 0))
    return pl.pallas_call(
        _proj_kernel,
        grid=(n // tm,),
        in_specs=[row, row, pl.BlockSpec((d, d), lambda i: (0, 0))],
        out_specs=row,
        out_shape=jax.ShapeDtypeStruct((n, d), F32),
        compiler_params=_params("parallel"),
        name="out_proj",
    )(x, o, w_o)


def _dense_kernel(x_ref, g_ref, wg_ref, wu_ref, wd_ref, y_ref):
    x = x_ref[...]
    h = _rms(x, g_ref[...]).astype(BF16)
    gate = jnp.dot(h, wg_ref[...], preferred_element_type=F32)
    up = jnp.dot(h, wu_ref[...], preferred_element_type=F32)
    a = (_silu(gate) * up).astype(BF16)
    y_ref[...] = x + jnp.dot(a, wd_ref[...], preferred_element_type=F32)


def _dense_ffn(x, gain, w_gate, w_up, w_down):
    n, d = x.shape
    ff = w_gate.shape[1]
    tm = 256
    row = pl.BlockSpec((tm, d), lambda i: (i, 0))
    const = lambda shape: pl.BlockSpec(shape, lambda i: (0, 0), pipeline_mode=pl.Buffered(1))
    return pl.pallas_call(
        _dense_kernel,
        grid=(n // tm,),
        in_specs=[row, const((1, d)), const((d, ff)), const((d, ff)), const((ff, d))],
        out_specs=row,
        out_shape=jax.ShapeDtypeStruct((n, d), F32),
        compiler_params=_params("parallel"),
        name="dense_ffn",
    )(x, gain, w_gate, w_up, w_down)


POOL_HALO = max(POOL_SIZES)


def _pool_kernel(x_ref, halo_ref, g_ref, w_ref, sc_ref, y_ref):
    i = pl.program_id(1)
    ts = x_ref.shape[0]
    g = g_ref[...]
    x = x_ref[...]
    h = _rms(x, g)
    hh = jnp.where(i > 0, _rms(halo_ref[...], g), 0.0)
    ext = jnp.concatenate([hh, h], axis=0)
    pos = i * ts + lax.broadcasted_iota(jnp.int32, (ts, 1), 0)
    pc = w_ref.shape[1]
    for grp, size in enumerate(POOL_SIZES):
        sl = slice(grp * pc, (grp + 1) * pc)
        s = ext[:, sl]
        step = 1
        while step < size:
            s = s + pltpu.roll(s, step, 0)
            step *= 2
        cnt = jnp.minimum(pos + 1, size).astype(F32)
        y = s[POOL_HALO:, :] / cnt - h[:, sl]
        z = jnp.dot(y.astype(BF16), w_ref[grp], preferred_element_type=F32)
        y_ref[:, sl] = x[:, sl] + z * sc_ref[:, sl]


def _pool_mixer(x, gain, pool_w, scale, batch, seq):
    n, d = x.shape
    ts = 512
    x3 = x.reshape(batch, seq, d)
    hb = ts // POOL_HALO
    out = pl.pallas_call(
        _pool_kernel,
        grid=(batch, seq // ts),
        in_specs=[
            pl.BlockSpec((None, ts, d), lambda b, i: (b, i, 0)),
            pl.BlockSpec((None, POOL_HALO, d), lambda b, i: (b, jnp.maximum(i * hb - 1, 0), 0)),
            pl.BlockSpec((1, d), lambda b, i: (0, 0)),
            pl.BlockSpec(pool_w.shape, lambda b, i: (0, 0, 0)),
            pl.BlockSpec((1, d), lambda b, i: (0, 0)),
        ],
        out_specs=pl.BlockSpec((None, ts, d), lambda b, i: (b, i, 0)),
        out_shape=jax.ShapeDtypeStruct((batch, seq, d), F32),
        compiler_params=_params("parallel", "arbitrary"),
        name="pool_mixer",
    )(x3, x3, gain, pool_w, scale)
    return out.reshape(n, d)


ROUTE_COLS = 8


def _router_kernel(x_ref, g_ref, wr_ref, tril_ref, ri_ref, rg_ref, cnt_ref, run_ref):
    i = pl.program_id(0)

    @pl.when(i == 0)
    def _():
        run_ref[...] = jnp.zeros_like(run_ref)

    h = _rms(x_ref[...], g_ref[...])
    logits = jnp.dot(h, wr_ref[...], preferred_element_type=F32, precision=lax.Precision.HIGHEST)
    tm, ne = logits.shape
    lane = lax.broadcasted_iota(jnp.int32, (tm, ne), 1)
    v1 = jnp.max(logits, axis=1, keepdims=True)
    i1 = jnp.min(jnp.where(logits == v1, lane, ne), axis=1, keepdims=True)
    rest = jnp.where(lane == i1, -jnp.inf, logits)
    v2 = jnp.max(rest, axis=1, keepdims=True)
    i2 = jnp.min(jnp.where(rest == v2, lane, ne), axis=1, keepdims=True)
    e = jnp.exp(v2 - v1)
    g1 = 1.0 / (1.0 + e)
    g2 = e / (1.0 + e)
    oh1 = (lane == i1).astype(F32)
    oh2 = (lane == i2).astype(F32)
    both = oh1 + oh2
    before = jnp.dot(tril_ref[...], both.astype(BF16), preferred_element_type=F32) + run_ref[...]
    r1 = jnp.sum(before * oh1, axis=1, keepdims=True).astype(jnp.int32)
    r2 = jnp.sum(before * oh2, axis=1, keepdims=True).astype(jnp.int32)
    run_ref[...] += jnp.sum(both, axis=0, keepdims=True)
    ri_ref[...] = jnp.where(lane == 0, i1, jnp.where(lane == 1, i2, jnp.where(lane == 2, r1, r2)))
    rg_ref[...] = jnp.where(lane == 0, g1, g2)
    cnt_ref[...] = run_ref[...]


def _router(x, gain, w_router):
    n, d = x.shape
    ne = w_router.shape[1]
    tm = 512
    tril = (jnp.arange(tm)[:, None] > jnp.arange(tm)[None, :]).astype(BF16)
    return pl.pallas_call(
        _router_kernel,
        grid=(n // tm,),
        in_specs=[
            pl.BlockSpec((tm, d), lambda i: (i, 0)),
            pl.BlockSpec((1, d), lambda i: (0, 0)),
            pl.BlockSpec((d, ne), lambda i: (0, 0)),
            pl.BlockSpec((tm, tm), lambda i: (0, 0)),
        ],
        out_specs=[
            pl.BlockSpec((tm, ROUTE_COLS), lambda i: (i, 0)),
            pl.BlockSpec((tm, ROUTE_COLS), lambda i: (i, 0)),
            pl.BlockSpec((1, ne), lambda i: (0, 0)),
        ],
        out_shape=[
            jax.ShapeDtypeStruct((n, ROUTE_COLS), jnp.int32),
            jax.ShapeDtypeStruct((n, ROUTE_COLS), F32),
            jax.ShapeDtypeStruct((1, ne), F32),
        ],
        scratch_shapes=[pltpu.VMEM((1, ne), F32)],
        compiler_params=_params("arbitrary"),
        name="moe_router",
    )(x, gain, w_router, tril)


def _dispatch_kernel(dest_ref, x_ref, g_ref, xs_in_ref, xs_ref, h_ref, sem):
    del xs_in_ref
    i = pl.program_id(0)
    tm = h_ref.shape[0]
    h_ref[...] = _rms(x_ref[...], g_ref[...])

    def row_copy(t, d):
        return pltpu.make_async_copy(h_ref.at[pl.ds(t, 1)], xs_ref.at[pl.ds(d, 1)], sem)

    def start(t, c):
        base = (i * tm + t) * TOP_K
        for k in range(TOP_K):
            row_copy(t, dest_ref[base + k]).start()
        return c

    lax.fori_loop(0, tm, start, 0)

    def wait(t, c):
        for k in range(TOP_K):
            row_copy(0, 0).wait()
        return c

    lax.fori_loop(0, tm, wait, 0)


def _dispatch(x, gain, dest_flat, cap):
    n, d = x.shape
    tm = 256
    xs_init = jnp.zeros((cap, d), F32)
    return pl.pallas_call(
        _dispatch_kernel,
        grid_spec=pltpu.PrefetchScalarGridSpec(
            num_scalar_prefetch=1,
            grid=(n // tm,),
            in_specs=[
                pl.BlockSpec((tm, d), lambda i, dest: (i, 0)),
                pl.BlockSpec((1, d), lambda i, dest: (0, 0)),
                pl.BlockSpec(memory_space=pl.ANY),
            ],
            out_specs=pl.BlockSpec(memory_space=pl.ANY),
            scratch_shapes=[pltpu.VMEM((tm, d), F32), pltpu.SemaphoreType.DMA],
        ),
        out_shape=jax.ShapeDtypeStruct((cap, d), F32),
        input_output_aliases={3: 0},
        compiler_params=_params("arbitrary"),
        name="moe_dispatch",
    )(dest_flat, x, gain, xs_init)


def _expert_kernel(be_ref, nv_ref, xs_ref, wg_ref, wu_ref, wd_ref, ys_ref, xb_ref):
    del be_ref
    b = pl.program_id(0)
    j = pl.program_id(1)

    @pl.when(b < nv_ref[0])
    def _():
        @pl.when(j == 0)
        def _():
            xb_ref[...] = xs_ref[...].astype(BF16)

        x = xb_ref[...]
        gate = jnp.dot(x, wg_ref[...], preferred_element_type=F32)
        up = jnp.dot(x, wu_ref[...], preferred_element_type=F32)
        a = (_silu(gate) * up).astype(BF16)
        part = jnp.dot(a, wd_ref[...], preferred_element_type=F32)

        @pl.when(j == 0)
        def _():
            ys_ref[...] = part

        @pl.when(j > 0)
        def _():
            ys_ref[...] += part

    @pl.when((b >= nv_ref[0]) & (j == 0))
    def _():
        ys_ref[...] = jnp.zeros_like(ys_ref)


def _experts(xs, block_e, n_valid, w_gate, w_up, w_down, tm, tf):
    cap, d = xs.shape
    ff = w_gate.shape[2]
    nj = ff // tf

    def blk(b, nv):
        return jnp.minimum(b, nv[0] - 1)

    def ffi(b, j, nv):
        return jnp.where(b < nv[0], j, nj - 1)

    return pl.pallas_call(
        _expert_kernel,
        grid_spec=pltpu.PrefetchScalarGridSpec(
            num_scalar_prefetch=2,
            grid=(cap // tm, nj),
            in_specs=[
                pl.BlockSpec((tm, d), lambda b, j, be, nv: (blk(b, nv), 0)),
                pl.BlockSpec((None, d, tf), lambda b, j, be, nv: (be[blk(b, nv)], 0, ffi(b, j, nv))),
                pl.BlockSpec((None, d, tf), lambda b, j, be, nv: (be[blk(b, nv)], 0, ffi(b, j, nv))),
                pl.BlockSpec((None, tf, d), lambda b, j, be, nv: (be[blk(b, nv)], ffi(b, j, nv), 0)),
            ],
            out_specs=pl.BlockSpec((tm, d), lambda b, j, be, nv: (b, 0)),
            scratch_shapes=[pltpu.VMEM((tm, d), BF16)],
        ),
        out_shape=jax.ShapeDtypeStruct((cap, d), F32),
        compiler_params=_params("arbitrary", "arbitrary"),
        name="moe_experts",
    )(block_e, n_valid, xs, w_gate, w_up, w_down)


def _combine_kernel(dest_ref, x_ref, rg_ref, gf_ref, ys_ref, o_ref, y0_ref, y1_ref, sem, *, final_norm):
    i = pl.program_id(0)
    tm = x_ref.shape[0]
    bufs = (y0_ref, y1_ref)

    def row_copy(t, d, k):
        return pltpu.make_async_copy(ys_ref.at[pl.ds(d, 1)], bufs[k].at[pl.ds(t, 1)], sem)

    def start(t, c):
        base = (i * tm + t) * TOP_K
        for k in range(TOP_K):
            row_copy(t, dest_ref[base + k], k).start()
        return c

    lax.fori_loop(0, tm, start, 0)

    def wait(t, c):
        for k in range(TOP_K):
            row_copy(0, 0, k).wait()
        return c

    lax.fori_loop(0, tm, wait, 0)
    rg = rg_ref[...]
    out = x_ref[...] + (rg[:, 0:1] * y0_ref[...] + rg[:, 1:2] * y1_ref[...])
    if final_norm:
        out = _rms(out, gf_ref[...])
    o_ref[...] = out


def _combine(x, route_g, ys, dest_flat, final_gain, final_norm):
    n, d = x.shape
    tm = 256
    return pl.pallas_call(
        functools.partial(_combine_kernel, final_norm=final_norm),
        grid_spec=pltpu.PrefetchScalarGridSpec(
            num_scalar_prefetch=1,
            grid=(n // tm,),
            in_specs=[
                pl.BlockSpec((tm, d), lambda i, dest: (i, 0)),
                pl.BlockSpec((tm, ROUTE_COLS), lambda i, dest: (i, 0)),
                pl.BlockSpec((1, d), lambda i, dest: (0, 0)),
                pl.BlockSpec(memory_space=pl.ANY),
            ],
            out_specs=pl.BlockSpec((tm, d), lambda i, dest: (i, 0)),
            scratch_shapes=[pltpu.VMEM((tm, d), F32), pltpu.VMEM((tm, d), F32), pltpu.SemaphoreType.DMA],
        ),
        out_shape=jax.ShapeDtypeStruct((n, d), F32),
        compiler_params=_params("arbitrary"),
        name="moe_combine",
    )(dest_flat, x, route_g, final_gain, ys)


def _moe_layer(x, gain, w_router, w_gate, w_up, w_down, final_gain, final_norm):
    n, d = x.shape
    tm = 1024 if n * TOP_K >= 8 * 1024 else 256
    tf = 2 * MXU_DIM
    route_i, route_g, counts = _router(x, gain, w_router)

    counts = counts[0].astype(jnp.int32)
    padded = ((counts + tm - 1) // tm) * tm
    ends = jnp.cumsum(padded)
    starts = ends - padded
    dest = starts[route_i[:, :TOP_K]] + route_i[:, TOP_K:2 * TOP_K]
    dest_flat = dest.reshape(n * TOP_K)
    n_blocks = (n * TOP_K) // tm + N_EXPERTS
    block_e = jnp.minimum(
        jnp.searchsorted(ends, jnp.arange(n_blocks, dtype=jnp.int32) * tm, side="right"),
        N_EXPERTS - 1).astype(jnp.int32)
    n_valid = (ends[-1:] // tm).astype(jnp.int32)

    xs = _dispatch(x, gain, dest_flat, n_blocks * tm)
    ys = _experts(xs, block_e, n_valid, w_gate, w_up, w_down, tm, tf)
    return _combine(x, route_g, ys, dest_flat, final_gain, final_norm)


def _rope_tables(seq):
    pos = jnp.arange(seq, dtype=F32)
    inv_freq = ROPE_THETA ** (-jnp.arange(0, HEAD_DIM, 2, dtype=F32) / HEAD_DIM)
    ang = pos[:, None] * inv_freq[None, :]
    reps = LANES // (HEAD_DIM // 2)
    cos = jnp.tile(jnp.cos(ang), (1, reps))
    sin = jnp.tile(jnp.sin(ang), (1, reps))
    first_half = (jnp.arange(LANES) % HEAD_DIM) < HEAD_DIM // 2
    return cos, jnp.where(first_half, -sin, 0.0), jnp.where(first_half, 0.0, sin)


def kernel(x, mix_norm, ffn_norm, attn_w_qkv, attn_w_o, pool_w, pool_scale, dense_w_gate, dense_w_up,
           dense_w_down, moe_router, moe_w_gate, moe_w_up, moe_w_down, final_norm):
    batch, seq, d = x.shape
    depth = mix_norm.shape[0]
    assert depth % 2 == 0 and d == N_HEADS * HEAD_DIM
    cos, sina, sinb = _rope_tables(seq)
    xf = x.reshape(batch * seq, d)
    final_gain = final_norm.reshape(1, d)
    for i in range(depth):
        j = i // 2
        mix_gain = mix_norm[i].reshape(1, d)
        ffn_gain = ffn_norm[i].reshape(1, d)
        if i % 2 == 0:
            qkv = _qkv_rope(xf, mix_gain, attn_w_qkv[j].astype(BF16), cos, sina, sinb, batch, seq)
            o = _attention(qkv, batch, seq)
            xf = _out_proj(xf, o, attn_w_o[j].astype(BF16))
            xf = _dense_ffn(xf, ffn_gain, dense_w_gate[j].astype(BF16), dense_w_up[j].astype(BF16),
                            dense_w_down[j].astype(BF16))
        else:
            xf = _pool_mixer(xf, mix_gain, pool_w[j].astype(BF16), pool_scale[j].reshape(1, d), batch, seq)
            xf = _moe_layer(xf, ffn_gain, moe_router[j], moe_w_gate[j].astype(BF16), moe_w_up[j].astype(BF16),
                            moe_w_down[j].astype(BF16), final_gain, i == depth - 1)
    return xf.reshape(batch, seq, d)
```

```python
import functools

import jax
import jax.numpy as jnp
from jax import lax
from jax.experimental import pallas as pl
from jax.experimental.pallas import tpu as pltpu

WINDOWS = (128, 512, 2048)
DILATIONS = (1, 4, 16)
N_GROUPS = len(WINDOWS)
N_HEADS = 16
HEAD_DIM = 64
ROPE_THETA = 10000.0
POOL_SIZES = (2, 4, 8, 16)
N_EXPERTS = 8
TOP_K = 2
RMS_EPS = 1e-6
MASK_VALUE = -1e30
ATTN_UNROLL = 8

LANES = 128
MXU_DIM = 256
VMEM_LIMIT_BYTES = 56 * 1024 * 1024

F32 = jnp.float32
BF16 = jnp.bfloat16


def _params(*sem):
    return pltpu.CompilerParams(dimension_semantics=sem, vmem_limit_bytes=VMEM_LIMIT_BYTES)


def _rms(x, g):
    ms = jnp.mean(x * x, axis=-1, keepdims=True)
    return x * lax.rsqrt(ms + RMS_EPS) * g


def _silu(g):
    return g / (1.0 + jnp.exp(-g))


def _qkv_kernel(x_ref, g_ref, w0_ref, w1_ref, w2_ref, cos_ref, sina_ref, sinb_ref,
                o0_ref, o1_ref, o2_ref, acc_ref):
    is_v = pl.program_id(0) == 2
    tm = x_ref.shape[0]
    h = _rms(x_ref[...], g_ref[...]).astype(BF16)
    n_chunks = acc_ref.shape[1]
    for g, (w_ref, o_ref, dil) in enumerate(zip((w0_ref, w1_ref, w2_ref), (o0_ref, o1_ref, o2_ref), DILATIONS)):
        acc = jnp.dot(h, w_ref[...], preferred_element_type=F32)
        for c in range(n_chunks):
            acc_ref[g, c] = acc[:, c * LANES:(c + 1) * LANES]
        rows = tm // dil

        def strided(ref, r, dil=dil, rows=rows):
            if dil == 1:
                return ref[...]
            return ref[pl.ds(r, rows, stride=dil), :]

        for r in range(dil):
            cos = strided(cos_ref, r)
            sina = strided(sina_ref, r)
            sinb = strided(sinb_ref, r)
            for c in range(n_chunks):
                t = strided(acc_ref.at[g, c], r)
                rot = (t * cos + pltpu.roll(t, LANES - HEAD_DIM // 2, 1) * sina
                       + pltpu.roll(t, HEAD_DIM // 2, 1) * sinb)
                o_ref[r, :, c * LANES:(c + 1) * LANES] = jnp.where(is_v, t, rot).astype(BF16)


def _qkv_rope(x, gain, w_qkv, cos, sina, sinb, batch, seq):
    n, d = x.shape
    tm = 512
    tiles_per_seq = seq // tm
    tab = pl.BlockSpec((None, tm, LANES), lambda k, i: (jnp.minimum(k, 1), i % tiles_per_seq, 0))

    def w_spec(g):
        return pl.BlockSpec((d, d), lambda k, i: (0, g * 3 + k))

    def o_spec(dil):
        return pl.BlockSpec((None, None, dil, tm // dil, d),
                            lambda k, i: (k, i // tiles_per_seq, 0, i % tiles_per_seq, 0))

    outs = pl.pallas_call(
        _qkv_kernel,
        grid=(3, n // tm),
        in_specs=[
            pl.BlockSpec((tm, d), lambda k, i: (i, 0)),
            pl.BlockSpec((1, d), lambda k, i: (0, 0)),
            w_spec(0), w_spec(1), w_spec(2),
            tab, tab, tab,
        ],
        out_specs=[o_spec(dil) for dil in DILATIONS],
        out_shape=[jax.ShapeDtypeStruct((3, batch, dil, seq // dil, d), BF16) for dil in DILATIONS],
        scratch_shapes=[pltpu.VMEM((N_GROUPS, d // LANES, tm, LANES), F32)],
        compiler_params=_params("parallel", "parallel"),
        name="qkv_rope",
    )(x, gain, w_qkv, w_qkv, w_qkv, cos, sina, sinb)
    return [o.reshape(3, batch, seq, d) for o in outs]


def _attn_kernel(q0, k0, v0, q1, k1, v1, q2, k2, v2, o_ref, num_ref, max_ref, den_ref, kt_ref):
    seq = o_ref.shape[0]
    w = WINDOWS[0] // DILATIONS[0]
    lane = lax.broadcasted_iota(jnp.int32, (1, LANES), 1)
    first_head = lane < HEAD_DIM
    row2 = lax.broadcasted_iota(jnp.int32, (2 * w, 2 * w), 0) % w
    col2 = lax.broadcasted_iota(jnp.int32, (2 * w, 2 * w), 1)
    dist = row2 + w - col2
    band = (dist >= 0) & (dist <= w)
    in_current = col2 >= w
    causal = (lax.broadcasted_iota(jnp.int32, (2 * w, w), 0) % w
              >= lax.broadcasted_iota(jnp.int32, (2 * w, w), 1))

    def tile(q_ref, v_ref, qoff, n, has_prev):
        q = q_ref[pl.ds(qoff, w), :]
        zero = jnp.zeros_like(q)
        qq = jnp.concatenate([jnp.where(first_head, q, zero), jnp.where(first_head, zero, q)], axis=0)
        if has_prev:
            poff = pl.multiple_of(jnp.maximum(qoff - w, 0), w)
            keys_t = jnp.concatenate([kt_ref[:, pl.ds(poff, w)], kt_ref[:, pl.ds(qoff, w)]], axis=1)
            vals = jnp.concatenate([v_ref[pl.ds(poff, w), :], v_ref[pl.ds(qoff, w), :]], axis=0)
            valid = band & (in_current | (n > 0))
        else:
            keys_t = kt_ref[:, pl.ds(qoff, w)]
            vals = v_ref[pl.ds(qoff, w), :]
            valid = causal
        s = jnp.dot(qq, keys_t, preferred_element_type=F32)
        s = jnp.where(valid, s, MASK_VALUE)
        mx = jnp.max(s, axis=1, keepdims=True)
        p = jnp.exp2(s - mx)
        den = jnp.sum(p, axis=1, keepdims=True)
        pv = jnp.dot(p.astype(BF16), vals, preferred_element_type=F32)
        return tuple(jnp.where(first_head, a[:w], a[w:]) for a in (pv, mx, den))

    order = sorted(range(N_GROUPS), key=lambda g: -DILATIONS[g])
    for g in order:
        q_ref, k_ref, v_ref = ((q0, k0, v0), (q1, k1, v1), (q2, k2, v2))[g]
        dil = DILATIONS[g]
        sub_len = seq // dil
        nb = sub_len // w

        kt_ref[...] = k_ref[...].T

        def body(t, carry, q_ref=q_ref, v_ref=v_ref, g=g, dil=dil, sub_len=sub_len, nb=nb):
            r = t // nb
            n = t % nb
            qoff = pl.multiple_of(r * sub_len + n * w, w)
            pv, mx, den = tile(q_ref, v_ref, qoff, n, nb > 1)
            if dil == 1:
                rows = pl.ds(qoff, w)
            else:
                rows = pl.ds(r + dil * n * w, w, stride=dil)
            if g == order[0]:
                num_ref[rows, :] = pv
                max_ref[rows, :] = mx
                den_ref[rows, :] = den
                return carry
            old_max = max_ref[rows, :]
            new_max = jnp.maximum(old_max, mx)
            a = jnp.exp2(old_max - new_max)
            b = jnp.exp2(mx - new_max)
            num = a * num_ref[rows, :] + b * pv
            den = a * den_ref[rows, :] + b * den
            if g == order[-1]:
                o_ref[rows, :] = (num / den).astype(BF16)
            else:
                num_ref[rows, :] = num
                den_ref[rows, :] = den
                max_ref[rows, :] = new_max
            return carry

        lax.fori_loop(0, dil * nb, body, 0, unroll=ATTN_UNROLL)


def _attention(qkv_groups, batch, seq):
    d_model = N_HEADS * HEAD_DIM
    in_specs, args = [], []
    for arr in qkv_groups:
        for kind in range(3):
            in_specs.append(pl.BlockSpec((None, None, seq, LANES), lambda b, hp, kind=kind: (kind, b, 0, hp)))
            args.append(arr)
    return pl.pallas_call(
        _attn_kernel,
        grid=(batch, d_model // LANES),
        in_specs=in_specs,
        out_specs=pl.BlockSpec((None, seq, LANES), lambda b, hp: (b, 0, hp)),
        out_shape=jax.ShapeDtypeStruct((batch, seq, d_model), BF16),
        scratch_shapes=[pltpu.VMEM((seq, LANES), F32)] * 3 + [pltpu.VMEM((LANES, seq), BF16)],
        compiler_params=_params("parallel", "parallel"),
        name="attention",
    )(*args).reshape(batch * seq, d_model)


def _proj_kernel(x_ref, o_ref, w_ref, y_ref):
    y_ref[...] = x_ref[...] + jnp.dot(o_ref[...], w_ref[...], preferred_element_type=F32)


def _out_proj(x, o, w_o):
    n, d = x.shape
    tm = 1024
    row = pl.BlockSpec((tm, d), lambda i: (i, 0))
    return pl.pallas_call(
        _proj_kernel,
        grid=(n // tm,),
        in_specs=[row, row, pl.BlockSpec((d, d), lambda i: (0, 0))],
        out_specs=row,
        out_shape=jax.ShapeDtypeStruct((n, d), F32),
        compiler_params=_params("parallel"),
        name="out_proj",
    )(x, o, w_o)


def _dense_kernel(x_ref, g_ref, wg_ref, wu_ref, wd_ref, y_ref):
    x = x_ref[...]
    h = _rms(x, g_ref[...]).astype(BF16)
    gate = jnp.dot(h, wg_ref[...], preferred_element_type=F32)
    up = jnp.dot(h, wu_ref[...], preferred_element_type=F32)
    a = (_silu(gate) * up).astype(BF16)
    y_ref[...] = x + jnp.dot(a, wd_ref[...], preferred_element_type=F32)


def _dense_ffn(x, gain, w_gate, w_up, w_down):
    n, d = x.shape
    ff = w_gate.shape[1]
    tm = 256
    row = pl.BlockSpec((tm, d), lambda i: (i, 0))
    const = lambda shape: pl.BlockSpec(shape, lambda i: (0, 0), pipeline_mode=pl.Buffered(1))
    return pl.pallas_call(
        _dense_kernel,
        grid=(n // tm,),
        in_specs=[row, const((1, d)), const((d, ff)), const((d, ff)), const((ff, d))],
        out_specs=row,
        out_shape=jax.ShapeDtypeStruct((n, d), F32),
        compiler_params=_params("parallel"),
        name="dense_ffn",
    )(x, gain, w_gate, w_up, w_down)


POOL_HALO = max(POOL_SIZES)


def _pool_kernel(x_ref, halo_ref, g_ref, w_ref, sc_ref, y_ref):
    i = pl.program_id(1)
    ts = x_ref.shape[0]
    g = g_ref[...]
    x = x_ref[...]
    h = _rms(x, g)
    hh = jnp.where(i > 0, _rms(halo_ref[...], g), 0.0)
    ext = jnp.concatenate([hh, h], axis=0)
    pos = i * ts + lax.broadcasted_iota(jnp.int32, (ts, 1), 0)
    pc = w_ref.shape[1]
    for grp, size in enumerate(POOL_SIZES):
        sl = slice(grp * pc, (grp + 1) * pc)
        s = ext[:, sl]
        step = 1
        while step < size:
            s = s + pltpu.roll(s, step, 0)
            step *= 2
        cnt = jnp.minimum(pos + 1, size).astype(F32)
        y = s[POOL_HALO:, :] / cnt - h[:, sl]
        z = jnp.dot(y.astype(BF16), w_ref[grp], preferred_element_type=F32)
        y_ref[:, sl] = x[:, sl] + z * sc_ref[:, sl]


def _pool_mixer(x, gain, pool_w, scale, batch, seq):
    n, d = x.shape
    ts = 512
    x3 = x.reshape(batch, seq, d)
    hb = ts // POOL_HALO
    out = pl.pallas_call(
        _pool_kernel,
        grid=(batch, seq // ts),
        in_specs=[
            pl.BlockSpec((None, ts, d), lambda b, i: (b, i, 0)),
            pl.BlockSpec((None, POOL_HALO, d), lambda b, i: (b, jnp.maximum(i * hb - 1, 0), 0)),
            pl.BlockSpec((1, d), lambda b, i: (0, 0)),
            pl.BlockSpec(pool_w.shape, lambda b, i: (0, 0, 0)),
            pl.BlockSpec((1, d), lambda b, i: (0, 0)),
        ],
        out_specs=pl.BlockSpec((None, ts, d), lambda b, i: (b, i, 0)),
        out_shape=jax.ShapeDtypeStruct((batch, seq, d), F32),
        compiler_params=_params("parallel", "arbitrary"),
        name="pool_mixer",
    )(x3, x3, gain, pool_w, scale)
    return out.reshape(n, d)


ROUTE_COLS = 8


def _router_kernel(x_ref, g_ref, wr_ref, tril_ref, ri_ref, rg_ref, cnt_ref, run_ref):
    i = pl.program_id(0)

    @pl.when(i == 0)
    def _():
        run_ref[...] = jnp.zeros_like(run_ref)

    h = _rms(x_ref[...], g_ref[...])
    logits = jnp.dot(h, wr_ref[...], preferred_element_type=F32, precision=lax.Precision.HIGHEST)
    tm, ne = logits.shape
    lane = lax.broadcasted_iota(jnp.int32, (tm, ne), 1)
    v1 = jnp.max(logits, axis=1, keepdims=True)
    i1 = jnp.min(jnp.where(logits == v1, lane, ne), axis=1, keepdims=True)
    rest = jnp.where(lane == i1, -jnp.inf, logits)
    v2 = jnp.max(rest, axis=1, keepdims=True)
    i2 = jnp.min(jnp.where(rest == v2, lane, ne), axis=1, keepdims=True)
    e = jnp.exp(v2 - v1)
    g1 = 1.0 / (1.0 + e)
    g2 = e / (1.0 + e)
    oh1 = (lane == i1).astype(F32)
    oh2 = (lane == i2).astype(F32)
    both = oh1 + oh2
    before = jnp.dot(tril_ref[...], both.astype(BF16), preferred_element_type=F32) + run_ref[...]
    r1 = jnp.sum(before * oh1, axis=1, keepdims=True).astype(jnp.int32)
    r2 = jnp.sum(before * oh2, axis=1, keepdims=True).astype(jnp.int32)
    run_ref[...] += jnp.sum(both, axis=0, keepdims=True)
    ri_ref[...] = jnp.where(lane == 0, i1, jnp.where(lane == 1, i2, jnp.where(lane == 2, r1, r2)))
    rg_ref[...] = jnp.where(lane == 0, g1, g2)
    cnt_ref[...] = run_ref[...]


def _router(x, gain, w_router):
    n, d = x.shape
    ne = w_router.shape[1]
    tm = 512
    tril = (jnp.arange(tm)[:, None] > jnp.arange(tm)[None, :]).astype(BF16)
    return pl.pallas_call(
        _router_kernel,
        grid=(n // tm,),
        in_specs=[
            pl.BlockSpec((tm, d), lambda i: (i, 0)),
            pl.BlockSpec((1, d), lambda i: (0, 0)),
            pl.BlockSpec((d, ne), lambda i: (0, 0)),
            pl.BlockSpec((tm, tm), lambda i: (0, 0)),
        ],
        out_specs=[
            pl.BlockSpec((tm, ROUTE_COLS), lambda i: (i, 0)),
            pl.BlockSpec((tm, ROUTE_COLS), lambda i: (i, 0)),
            pl.BlockSpec((1, ne), lambda i: (0, 0)),
        ],
        out_shape=[
            jax.ShapeDtypeStruct((n, ROUTE_COLS), jnp.int32),
            jax.ShapeDtypeStruct((n, ROUTE_COLS), F32),
            jax.ShapeDtypeStruct((1, ne), F32),
        ],
        scratch_shapes=[pltpu.VMEM((1, ne), F32)],
        compiler_params=_params("arbitrary"),
        name="moe_router",
    )(x, gain, w_router, tril)


def _dispatch_kernel(dest_ref, x_ref, g_ref, xs_in_ref, xs_ref, h_ref, sem):
    del xs_in_ref
    i = pl.program_id(0)
    tm = h_ref.shape[0]
    h_ref[...] = _rms(x_ref[...], g_ref[...])

    def row_copy(t, d):
        return pltpu.make_async_copy(h_ref.at[pl.ds(t, 1)], xs_ref.at[pl.ds(d, 1)], sem)

    def start(t, c):
        base = (i * tm + t) * TOP_K
        for k in range(TOP_K):
            row_copy(t, dest_ref[base + k]).start()
        return c

    lax.fori_loop(0, tm, start, 0)

    def wait(t, c):
        for k in range(TOP_K):
            row_copy(0, 0).wait()
        return c

    lax.fori_loop(0, tm, wait, 0)


def _dispatch(x, gain, dest_flat, cap):
    n, d = x.shape
    tm = 256
    xs_init = jnp.zeros((cap, d), F32)
    return pl.pallas_call(
        _dispatch_kernel,
        grid_spec=pltpu.PrefetchScalarGridSpec(
            num_scalar_prefetch=1,
            grid=(n // tm,),
            in_specs=[
                pl.BlockSpec((tm, d), lambda i, dest: (i, 0)),
                pl.BlockSpec((1, d), lambda i, dest: (0, 0)),
                pl.BlockSpec(memory_space=pl.ANY),
            ],
            out_specs=pl.BlockSpec(memory_space=pl.ANY),
            scratch_shapes=[pltpu.VMEM((tm, d), F32), pltpu.SemaphoreType.DMA],
        ),
        out_shape=jax.ShapeDtypeStruct((cap, d), F32),
        input_output_aliases={3: 0},
        compiler_params=_params("arbitrary"),
        name="moe_dispatch",
    )(dest_flat, x, gain, xs_init)


def _expert_kernel(be_ref, nv_ref, xs_ref, wg_ref, wu_ref, wd_ref, ys_ref, xb_ref):
    del be_ref
    b = pl.program_id(0)
    j = pl.program_id(1)

    @pl.when(b < nv_ref[0])
    def _():
        @pl.when(j == 0)
        def _():
            xb_ref[...] = xs_ref[...].astype(BF16)

        x = xb_ref[...]
        gate = jnp.dot(x, wg_ref[...], preferred_element_type=F32)
        up = jnp.dot(x, wu_ref[...], preferred_element_type=F32)
        a = (_silu(gate) * up).astype(BF16)
        part = jnp.dot(a, wd_ref[...], preferred_element_type=F32)

        @pl.when(j == 0)
        def _():
            ys_ref[...] = part

        @pl.when(j > 0)
        def _():
            ys_ref[...] += part

    @pl.when((b >= nv_ref[0]) & (j == 0))
    def _():
        ys_ref[...] = jnp.zeros_like(ys_ref)


def _experts(xs, block_e, n_valid, w_gate, w_up, w_down, tm, tf):
    cap, d = xs.shape
    ff = w_gate.shape[2]
    nj = ff // tf

    def blk(b, nv):
        return jnp.minimum(b, nv[0] - 1)

    def ffi(b, j, nv):
        return jnp.where(b < nv[0], j, nj - 1)

    return pl.pallas_call(
        _expert_kernel,
        grid_spec=pltpu.PrefetchScalarGridSpec(
            num_scalar_prefetch=2,
            grid=(cap // tm, nj),
            in_specs=[
                pl.BlockSpec((tm, d), lambda b, j, be, nv: (blk(b, nv), 0)),
                pl.BlockSpec((None, d, tf), lambda b, j, be, nv: (be[blk(b, nv)], 0, ffi(b, j, nv))),
                pl.BlockSpec((None, d, tf), lambda b, j, be, nv: (be[blk(b, nv)], 0, ffi(b, j, nv))),
                pl.BlockSpec((None, tf, d), lambda b, j, be, nv: (be[blk(b, nv)], ffi(b, j, nv), 0)),
            ],
            out_specs=pl.BlockSpec((tm, d), lambda b, j, be, nv: (b, 0)),
            scratch_shapes=[pltpu.VMEM((tm, d), BF16)],
        ),
        out_shape=jax.ShapeDtypeStruct((cap, d), F32),
        compiler_params=_params("arbitrary", "arbitrary"),
        name="moe_experts",
    )(block_e, n_valid, xs, w_gate, w_up, w_down)


def _combine_kernel(dest_ref, x_ref, rg_ref, gf_ref, ys_ref, o_ref, y0_ref, y1_ref, sem, *, final_norm):
    i = pl.program_id(0)
    tm = x_ref.shape[0]
    bufs = (y0_ref, y1_ref)

    def row_copy(t, d, k):
        return pltpu.make_async_copy(ys_ref.at[pl.ds(d, 1)], bufs[k].at[pl.ds(t, 1)], sem)

    def start(t, c):
        base = (i * tm + t) * TOP_K
        for k in range(TOP_K):
            row_copy(t, dest_ref[base + k], k).start()
        return c

    lax.fori_loop(0, tm, start, 0)

    def wait(t, c):
        for k in range(TOP_K):
            row_copy(0, 0, k).wait()
        return c

    lax.fori_loop(0, tm, wait, 0)
    rg = rg_ref[...]
    out = x_ref[...] + (rg[:, 0:1] * y0_ref[...] + rg[:, 1:2] * y1_ref[...])
    if final_norm:
        out = _rms(out, gf_ref[...])
    o_ref[...] = out


def _combine(x, route_g, ys, dest_flat, final_gain, final_norm):
    n, d = x.shape
    tm = 256
    return pl.pallas_call(
        functools.partial(_combine_kernel, final_norm=final_norm),
        grid_spec=pltpu.PrefetchScalarGridSpec(
            num_scalar_prefetch=1,
            grid=(n // tm,),
            in_specs=[
                pl.BlockSpec((tm, d), lambda i, dest: (i, 0)),
                pl.BlockSpec((tm, ROUTE_COLS), lambda i, dest: (i, 0)),
                pl.BlockSpec((1, d), lambda i, dest: (0, 0)),
                pl.BlockSpec(memory_space=pl.ANY),
            ],
            out_specs=pl.BlockSpec((tm, d), lambda i, dest: (i, 0)),
            scratch_shapes=[pltpu.VMEM((tm, d), F32), pltpu.VMEM((tm, d), F32), pltpu.SemaphoreType.DMA],
        ),
        out_shape=jax.ShapeDtypeStruct((n, d), F32),
        compiler_params=_params("arbitrary"),
        name="moe_combine",
    )(dest_flat, x, route_g, final_gain, ys)


def _moe_layer(x, gain, w_router, w_gate, w_up, w_down, final_gain, final_norm):
    n, d = x.shape
    tm = 1024 if n * TOP_K >= 8 * 1024 else 256
    tf = 2 * MXU_DIM
    route_i, route_g, counts = _router(x, gain, w_router)

    counts = counts[0].astype(jnp.int32)
    padded = ((counts + tm - 1) // tm) * tm
    ends = jnp.cumsum(padded)
    starts = ends - padded
    dest = starts[route_i[:, :TOP_K]] + route_i[:, TOP_K:2 * TOP_K]
    dest_flat = dest.reshape(n * TOP_K)
    n_blocks = (n * TOP_K) // tm + N_EXPERTS
    block_e = jnp.minimum(
        jnp.searchsorted(ends, jnp.arange(n_blocks, dtype=jnp.int32) * tm, side="right"),
        N_EXPERTS - 1).astype(jnp.int32)
    n_valid = (ends[-1:] // tm).astype(jnp.int32)

    xs = _dispatch(x, gain, dest_flat, n_blocks * tm)
    ys = _experts(xs, block_e, n_valid, w_gate, w_up, w_down, tm, tf)
    return _combine(x, route_g, ys, dest_flat, final_gain, final_norm)


def _rope_tables(seq):
    pos = jnp.arange(seq, dtype=F32)
    inv_freq = ROPE_THETA ** (-jnp.arange(0, HEAD_DIM, 2, dtype=F32) / HEAD_DIM)
    ang = pos[:, None] * inv_freq[None, :]
    reps = LANES // (HEAD_DIM // 2)
    cos = jnp.tile(jnp.cos(ang), (1, reps))
    sin = jnp.tile(jnp.sin(ang), (1, reps))
    first_half = (jnp.arange(LANES) % HEAD_DIM) < HEAD_DIM // 2
    tabs = (cos, jnp.where(first_half, -sin, 0.0), jnp.where(first_half, 0.0, sin))
    q_scale = HEAD_DIM ** -0.5 * 1.4426950408889634
    return tuple(jnp.stack([t * q_scale, t]) for t in tabs)


def kernel(x, mix_norm, ffn_norm, attn_w_qkv, attn_w_o, pool_w, pool_scale, dense_w_gate, dense_w_up,
           dense_w_down, moe_router, moe_w_gate, moe_w_up, moe_w_down, final_norm):
    batch, seq, d = x.shape
    depth = mix_norm.shape[0]
    assert depth % 2 == 0 and d == N_HEADS * HEAD_DIM
    cos, sina, sinb = _rope_tables(seq)
    xf = x.reshape(batch * seq, d)
    final_gain = final_norm.reshape(1, d)
    for i in range(depth):
        j = i // 2
        mix_gain = mix_norm[i].reshape(1, d)
        ffn_gain = ffn_norm[i].reshape(1, d)
        if i % 2 == 0:
            qkv = _qkv_rope(xf, mix_gain, attn_w_qkv[j].astype(BF16), cos, sina, sinb, batch, seq)
            o = _attention(qkv, batch, seq)
            xf = _out_proj(xf, o, attn_w_o[j].astype(BF16))
            xf = _dense_ffn(xf, ffn_gain, dense_w_gate[j].astype(BF16), dense_w_up[j].astype(BF16),
                            dense_w_down[j].astype(BF16))
        else:
            xf = _pool_mixer(xf, mix_gain, pool_w[j].astype(BF16), pool_scale[j].reshape(1, d), batch, seq)
            xf = _moe_layer(xf, ffn_gain, moe_router[j], moe_w_gate[j].astype(BF16), moe_w_up[j].astype(BF16),
                            moe_w_down[j].astype(BF16), final_gain, i == depth - 1)
    return xf.reshape(batch, seq, d)
```

```python
import functools

import jax
import jax.numpy as jnp
from jax import lax
from jax.experimental import pallas as pl
from jax.experimental.pallas import tpu as pltpu
from jax.experimental.pallas import tpu_sc as plsc

WINDOWS = (128, 512, 2048)
DILATIONS = (1, 4, 16)
N_GROUPS = len(WINDOWS)
N_HEADS = 16
HEAD_DIM = 64
ROPE_THETA = 10000.0
POOL_SIZES = (2, 4, 8, 16)
N_EXPERTS = 8
TOP_K = 2
RMS_EPS = 1e-6
MASK_VALUE = -1e30
ATTN_UNROLL = 8

LANES = 128
MXU_DIM = 256
VMEM_LIMIT_BYTES = 56 * 1024 * 1024

F32 = jnp.float32
BF16 = jnp.bfloat16


def _params(*sem):
    return pltpu.CompilerParams(dimension_semantics=sem, vmem_limit_bytes=VMEM_LIMIT_BYTES)


def _rms(x, g):
    ms = jnp.mean(x * x, axis=-1, keepdims=True)
    return x * lax.rsqrt(ms + RMS_EPS) * g


def _silu(g):
    return g / (1.0 + jnp.exp(-g))


def _qkv_kernel(x_ref, g_ref, w0_ref, w1_ref, w2_ref, cos_ref, sina_ref, sinb_ref,
                o0_ref, o1_ref, o2_ref, acc_ref):
    is_v = pl.program_id(0) == 2
    tm = x_ref.shape[0]
    h = _rms(x_ref[...], g_ref[...]).astype(BF16)
    n_chunks = acc_ref.shape[1]
    for g, (w_ref, o_ref, dil) in enumerate(zip((w0_ref, w1_ref, w2_ref), (o0_ref, o1_ref, o2_ref), DILATIONS)):
        acc = jnp.dot(h, w_ref[...], preferred_element_type=F32)
        for c in range(n_chunks):
            acc_ref[g, c] = acc[:, c * LANES:(c + 1) * LANES]
        rows = tm // dil

        def strided(ref, r, dil=dil, rows=rows):
            if dil == 1:
                return ref[...]
            return ref[pl.ds(r, rows, stride=dil), :]

        for r in range(dil):
            cos = strided(cos_ref, r)
            sina = strided(sina_ref, r)
            sinb = strided(sinb_ref, r)
            for c in range(n_chunks):
                t = strided(acc_ref.at[g, c], r)
                rot = (t * cos + pltpu.roll(t, LANES - HEAD_DIM // 2, 1) * sina
                       + pltpu.roll(t, HEAD_DIM // 2, 1) * sinb)
                o_ref[r, :, c * LANES:(c + 1) * LANES] = jnp.where(is_v, t, rot).astype(BF16)


def _qkv_rope(x, gain, w_qkv, cos, sina, sinb, batch, seq):
    n, d = x.shape
    tm = 512
    tiles_per_seq = seq // tm
    tab = pl.BlockSpec((None, tm, LANES), lambda k, i: (jnp.minimum(k, 1), i % tiles_per_seq, 0))

    def w_spec(g):
        return pl.BlockSpec((d, d), lambda k, i: (0, g * 3 + k))

    def o_spec(dil):
        return pl.BlockSpec((None, None, dil, tm // dil, d),
                            lambda k, i: (k, i // tiles_per_seq, 0, i % tiles_per_seq, 0))

    outs = pl.pallas_call(
        _qkv_kernel,
        grid=(3, n // tm),
        in_specs=[
            pl.BlockSpec((tm, d), lambda k, i: (i, 0)),
            pl.BlockSpec((1, d), lambda k, i: (0, 0)),
            w_spec(0), w_spec(1), w_spec(2),
            tab, tab, tab,
        ],
        out_specs=[o_spec(dil) for dil in DILATIONS],
        out_shape=[jax.ShapeDtypeStruct((3, batch, dil, seq // dil, d), BF16) for dil in DILATIONS],
        scratch_shapes=[pltpu.VMEM((N_GROUPS, d // LANES, tm, LANES), F32)],
        compiler_params=_params("parallel", "parallel"),
        name="qkv_rope",
    )(x, gain, w_qkv, w_qkv, w_qkv, cos, sina, sinb)
    return [o.reshape(3, batch, seq, d) for o in outs]


def _attn_kernel(q0, k0, v0, q1, k1, v1, q2, k2, v2, o_ref, num_ref, max_ref, den_ref, kt_ref):
    seq = o_ref.shape[0]
    w = WINDOWS[0] // DILATIONS[0]
    lane = lax.broadcasted_iota(jnp.int32, (1, LANES), 1)
    first_head = lane < HEAD_DIM
    row2 = lax.broadcasted_iota(jnp.int32, (2 * w, 2 * w), 0) % w
    col2 = lax.broadcasted_iota(jnp.int32, (2 * w, 2 * w), 1)
    dist = row2 + w - col2
    band = (dist >= 0) & (dist <= w)
    in_current = col2 >= w
    causal = (lax.broadcasted_iota(jnp.int32, (2 * w, w), 0) % w
              >= lax.broadcasted_iota(jnp.int32, (2 * w, w), 1))

    def tile(q_ref, v_ref, qoff, n, has_prev):
        q = q_ref[pl.ds(qoff, w), :]
        zero = jnp.zeros_like(q)
        qq = jnp.concatenate([jnp.where(first_head, q, zero), jnp.where(first_head, zero, q)], axis=0)
        if has_prev:
            poff = pl.multiple_of(jnp.maximum(qoff - w, 0), w)
            keys_t = jnp.concatenate([kt_ref[:, pl.ds(poff, w)], kt_ref[:, pl.ds(qoff, w)]], axis=1)
            vals = jnp.concatenate([v_ref[pl.ds(poff, w), :], v_ref[pl.ds(qoff, w), :]], axis=0)
            valid = band & (in_current | (n > 0))
        else:
            keys_t = kt_ref[:, pl.ds(qoff, w)]
            vals = v_ref[pl.ds(qoff, w), :]
            valid = causal
        s = jnp.dot(qq, keys_t, preferred_element_type=F32)
        s = jnp.where(valid, s, MASK_VALUE)
        mx = jnp.max(s, axis=1, keepdims=True)
        p = jnp.exp2(s - mx)
        den = jnp.sum(p, axis=1, keepdims=True)
        pv = jnp.dot(p.astype(BF16), vals, preferred_element_type=F32)
        return tuple(jnp.where(first_head, a[:w], a[w:]) for a in (pv, mx, den))

    order = sorted(range(N_GROUPS), key=lambda g: -DILATIONS[g])
    for g in order:
        q_ref, k_ref, v_ref = ((q0, k0, v0), (q1, k1, v1), (q2, k2, v2))[g]
        dil = DILATIONS[g]
        sub_len = seq // dil
        nb = sub_len // w

        kt_ref[...] = k_ref[...].T

        def body(t, carry, q_ref=q_ref, v_ref=v_ref, g=g, dil=dil, sub_len=sub_len, nb=nb):
            r = t // nb
            n = t % nb
            qoff = pl.multiple_of(r * sub_len + n * w, w)
            pv, mx, den = tile(q_ref, v_ref, qoff, n, nb > 1)
            if dil == 1:
                rows = pl.ds(qoff, w)
            else:
                rows = pl.ds(r + dil * n * w, w, stride=dil)
            if g == order[0]:
                num_ref[rows, :] = pv
                max_ref[rows, :] = mx
                den_ref[rows, :] = den
                return carry
            old_max = max_ref[rows, :]
            new_max = jnp.maximum(old_max, mx)
            a = jnp.exp2(old_max - new_max)
            b = jnp.exp2(mx - new_max)
            num = a * num_ref[rows, :] + b * pv
            den = a * den_ref[rows, :] + b * den
            if g == order[-1]:
                o_ref[rows, :] = (num / den).astype(BF16)
            else:
                num_ref[rows, :] = num
                den_ref[rows, :] = den
                max_ref[rows, :] = new_max
            return carry

        lax.fori_loop(0, dil * nb, body, 0, unroll=ATTN_UNROLL)


def _attention(qkv_groups, batch, seq):
    d_model = N_HEADS * HEAD_DIM
    in_specs, args = [], []
    for arr in qkv_groups:
        for kind in range(3):
            in_specs.append(pl.BlockSpec((None, None, seq, LANES), lambda b, hp, kind=kind: (kind, b, 0, hp)))
            args.append(arr)
    return pl.pallas_call(
        _attn_kernel,
        grid=(batch, d_model // LANES),
        in_specs=in_specs,
        out_specs=pl.BlockSpec((None, seq, LANES), lambda b, hp: (b, 0, hp)),
        out_shape=jax.ShapeDtypeStruct((batch, seq, d_model), BF16),
        scratch_shapes=[pltpu.VMEM((seq, LANES), F32)] * 3 + [pltpu.VMEM((LANES, seq), BF16)],
        compiler_params=_params("parallel", "parallel"),
        name="attention",
    )(*args).reshape(batch * seq, d_model)


def _proj_kernel(x_ref, o_ref, w_ref, y_ref):
    y_ref[...] = x_ref[...] + jnp.dot(o_ref[...], w_ref[...], preferred_element_type=F32)


def _out_proj(x, o, w_o):
    n, d = x.shape
    tm = 1024
    row = pl.BlockSpec((tm, d), lambda i: (i, 0))
    return pl.pallas_call(
        _proj_kernel,
        grid=(n // tm,),
        in_specs=[row, row, pl.BlockSpec((d, d), lambda i: (0, 0))],
        out_specs=row,
        out_shape=jax.ShapeDtypeStruct((n, d), F32),
        compiler_params=_params("parallel"),
        name="out_proj",
    )(x, o, w_o)


def _dense_kernel(x_ref, g_ref, wg_ref, wu_ref, wd_ref, y_ref):
    x = x_ref[...]
    h = _rms(x, g_ref[...]).astype(BF16)
    gate = jnp.dot(h, wg_ref[...], preferred_element_type=F32)
    up = jnp.dot(h, wu_ref[...], preferred_element_type=F32)
    a = (_silu(gate) * up).astype(BF16)
    y_ref[...] = x + jnp.dot(a, wd_ref[...], preferred_element_type=F32)


def _dense_ffn(x, gain, w_gate, w_up, w_down):
    n, d = x.shape
    ff = w_gate.shape[1]
    tm = 256
    row = pl.BlockSpec((tm, d), lambda i: (i, 0))
    const = lambda shape: pl.BlockSpec(shape, lambda i: (0, 0), pipeline_mode=pl.Buffered(1))
    return pl.pallas_call(
        _dense_kernel,
        grid=(n // tm,),
        in_specs=[row, const((1, d)), const((d, ff)), const((d, ff)), const((ff, d))],
        out_specs=row,
        out_shape=jax.ShapeDtypeStruct((n, d), F32),
        compiler_params=_params("parallel"),
        name="dense_ffn",
    )(x, gain, w_gate, w_up, w_down)


POOL_HALO = max(POOL_SIZES)


def _pool_kernel(x_ref, halo_ref, g_ref, w_ref, sc_ref, y_ref):
    i = pl.program_id(1)
    ts = x_ref.shape[0]
    g = g_ref[...]
    x = x_ref[...]
    h = _rms(x, g)
    hh = jnp.where(i > 0, _rms(halo_ref[...], g), 0.0)
    ext = jnp.concatenate([hh, h], axis=0)
    pos = i * ts + lax.broadcasted_iota(jnp.int32, (ts, 1), 0)
    pc = w_ref.shape[1]
    for grp, size in enumerate(POOL_SIZES):
        sl = slice(grp * pc, (grp + 1) * pc)
        s = ext[:, sl]
        step = 1
        while step < size:
            s = s + pltpu.roll(s, step, 0)
            step *= 2
        cnt = jnp.minimum(pos + 1, size).astype(F32)
        y = s[POOL_HALO:, :] / cnt - h[:, sl]
        z = jnp.dot(y.astype(BF16), w_ref[grp], preferred_element_type=F32)
        y_ref[:, sl] = x[:, sl] + z * sc_ref[:, sl]


def _pool_mixer(x, gain, pool_w, scale, batch, seq):
    n, d = x.shape
    ts = 512
    x3 = x.reshape(batch, seq, d)
    hb = ts // POOL_HALO
    out = pl.pallas_call(
        _pool_kernel,
        grid=(batch, seq // ts),
        in_specs=[
            pl.BlockSpec((None, ts, d), lambda b, i: (b, i, 0)),
            pl.BlockSpec((None, POOL_HALO, d), lambda b, i: (b, jnp.maximum(i * hb - 1, 0), 0)),
            pl.BlockSpec((1, d), lambda b, i: (0, 0)),
            pl.BlockSpec(pool_w.shape, lambda b, i: (0, 0, 0)),
            pl.BlockSpec((1, d), lambda b, i: (0, 0)),
        ],
        out_specs=pl.BlockSpec((None, ts, d), lambda b, i: (b, i, 0)),
        out_shape=jax.ShapeDtypeStruct((batch, seq, d), F32),
        compiler_params=_params("parallel", "arbitrary"),
        name="pool_mixer",
    )(x3, x3, gain, pool_w, scale)
    return out.reshape(n, d)


ROUTE_COLS = 8


def _router_kernel(x_ref, g_ref, wr_ref, tril_ref, ri_ref, rg_ref, cnt_ref, run_ref):
    i = pl.program_id(0)

    @pl.when(i == 0)
    def _():
        run_ref[...] = jnp.zeros_like(run_ref)

    h = _rms(x_ref[...], g_ref[...])
    logits = jnp.dot(h, wr_ref[...], preferred_element_type=F32, precision=lax.Precision.HIGHEST)
    tm, ne = logits.shape
    lane = lax.broadcasted_iota(jnp.int32, (tm, ne), 1)
    v1 = jnp.max(logits, axis=1, keepdims=True)
    i1 = jnp.min(jnp.where(logits == v1, lane, ne), axis=1, keepdims=True)
    rest = jnp.where(lane == i1, -jnp.inf, logits)
    v2 = jnp.max(rest, axis=1, keepdims=True)
    i2 = jnp.min(jnp.where(rest == v2, lane, ne), axis=1, keepdims=True)
    e = jnp.exp(v2 - v1)
    g1 = 1.0 / (1.0 + e)
    g2 = e / (1.0 + e)
    oh1 = (lane == i1).astype(F32)
    oh2 = (lane == i2).astype(F32)
    both = oh1 + oh2
    before = jnp.dot(tril_ref[...], both.astype(BF16), preferred_element_type=F32) + run_ref[...]
    r1 = jnp.sum(before * oh1, axis=1, keepdims=True).astype(jnp.int32)
    r2 = jnp.sum(before * oh2, axis=1, keepdims=True).astype(jnp.int32)
    run_ref[...] += jnp.sum(both, axis=0, keepdims=True)
    ri_ref[...] = jnp.where(lane == 0, i1, jnp.where(lane == 1, i2, jnp.where(lane == 2, r1, r2)))
    rg_ref[...] = jnp.where(lane == 0, g1, g2)
    cnt_ref[...] = run_ref[...]


def _router(x, gain, w_router):
    n, d = x.shape
    ne = w_router.shape[1]
    tm = 512
    tril = (jnp.arange(tm)[:, None] > jnp.arange(tm)[None, :]).astype(BF16)
    return pl.pallas_call(
        _router_kernel,
        grid=(n // tm,),
        in_specs=[
            pl.BlockSpec((tm, d), lambda i: (i, 0)),
            pl.BlockSpec((1, d), lambda i: (0, 0)),
            pl.BlockSpec((d, ne), lambda i: (0, 0)),
            pl.BlockSpec((tm, tm), lambda i: (0, 0)),
        ],
        out_specs=[
            pl.BlockSpec((tm, ROUTE_COLS), lambda i: (i, 0)),
            pl.BlockSpec((tm, ROUTE_COLS), lambda i: (i, 0)),
            pl.BlockSpec((1, ne), lambda i: (0, 0)),
        ],
        out_shape=[
            jax.ShapeDtypeStruct((n, ROUTE_COLS), jnp.int32),
            jax.ShapeDtypeStruct((n, ROUTE_COLS), F32),
            jax.ShapeDtypeStruct((1, ne), F32),
        ],
        scratch_shapes=[pltpu.VMEM((1, ne), F32)],
        compiler_params=_params("arbitrary"),
        name="moe_router",
    )(x, gain, w_router, tril)


SC_WINDOW = 64


def _sc_workers():
    info = plsc.get_sparse_core_info()
    mesh = plsc.VectorSubcoreMesh(core_axis_name="core", subcore_axis_name="subcore")
    return mesh, info.num_cores, info.num_cores * info.num_subcores


def _dispatch(x, dest, cap):
    n, d = x.shape
    mesh, n_cores, n_workers = _sc_workers()
    per_worker = n // n_workers

    @functools.partial(
        pl.kernel, mesh=mesh, out_type=jax.ShapeDtypeStruct((cap, d), x.dtype),
        scratch_types=[pltpu.VMEM((SC_WINDOW,), jnp.int32)] * TOP_K
        + [pltpu.VMEM((SC_WINDOW, d), x.dtype), pltpu.SemaphoreType.DMA],
        name="moe_dispatch")
    def scatter(x_hbm, *rest):
        idx_hbm, (xs_hbm, *idx_vmem, rows, sem) = rest[:TOP_K], rest[TOP_K:]
        wid = lax.axis_index("subcore") * n_cores + lax.axis_index("core")

        @pl.loop(0, per_worker // SC_WINDOW)
        def _(c):
            base = wid * per_worker + c * SC_WINDOW
            for k in range(TOP_K):
                pltpu.sync_copy(idx_hbm[k].at[pl.ds(base, SC_WINDOW)], idx_vmem[k])
            pltpu.sync_copy(x_hbm.at[pl.ds(base, SC_WINDOW)], rows)
            for k in range(TOP_K):
                pltpu.async_copy(rows, xs_hbm.at[idx_vmem[k]], sem).wait()

    return scatter(x, *[dest[k] for k in range(TOP_K)])


def _gather_rows(table, idx):
    m = idx.shape[0]
    d = table.shape[1]
    mesh, n_cores, n_workers = _sc_workers()
    per_worker = m // n_workers

    @functools.partial(
        pl.kernel, mesh=mesh, out_type=jax.ShapeDtypeStruct((m, d), table.dtype),
        scratch_types=[pltpu.VMEM((SC_WINDOW,), jnp.int32), pltpu.VMEM((SC_WINDOW, d), table.dtype),
                       pltpu.SemaphoreType.DMA],
        name="moe_gather")
    def gather(t_hbm, i_hbm, o_hbm, idx_vmem, rows, sem):
        wid = lax.axis_index("subcore") * n_cores + lax.axis_index("core")

        @pl.loop(0, per_worker // SC_WINDOW)
        def _(c):
            base = wid * per_worker + c * SC_WINDOW
            pltpu.sync_copy(i_hbm.at[pl.ds(base, SC_WINDOW)], idx_vmem)
            pltpu.async_copy(t_hbm.at[idx_vmem], rows, sem).wait()
            pltpu.sync_copy(rows, o_hbm.at[pl.ds(base, SC_WINDOW)])

    return gather(table, idx)


def _expert_kernel(be_ref, nv_ref, xs_ref, g_ref, wg_ref, wu_ref, wd_ref, ys_ref, xb_ref):
    del be_ref
    b = pl.program_id(0)
    j = pl.program_id(1)

    @pl.when(b < nv_ref[0])
    def _():
        @pl.when(j == 0)
        def _():
            xb_ref[...] = _rms(xs_ref[...], g_ref[...]).astype(BF16)

        x = xb_ref[...]
        gate = jnp.dot(x, wg_ref[...], preferred_element_type=F32)
        up = jnp.dot(x, wu_ref[...], preferred_element_type=F32)
        a = (_silu(gate) * up).astype(BF16)
        part = jnp.dot(a, wd_ref[...], preferred_element_type=F32)

        @pl.when(j == 0)
        def _():
            ys_ref[...] = part

        @pl.when(j > 0)
        def _():
            ys_ref[...] += part

    @pl.when((b >= nv_ref[0]) & (j == 0))
    def _():
        ys_ref[...] = jnp.zeros_like(ys_ref)


def _experts(xs, gain, block_e, n_valid, w_gate, w_up, w_down, tm, tf):
    cap, d = xs.shape
    ff = w_gate.shape[2]
    nj = ff // tf

    def blk(b, nv):
        return jnp.minimum(b, nv[0] - 1)

    def ffi(b, j, nv):
        return jnp.where(b < nv[0], j, nj - 1)

    return pl.pallas_call(
        _expert_kernel,
        grid_spec=pltpu.PrefetchScalarGridSpec(
            num_scalar_prefetch=2,
            grid=(cap // tm, nj),
            in_specs=[
                pl.BlockSpec((tm, d), lambda b, j, be, nv: (blk(b, nv), 0)),
                pl.BlockSpec((1, d), lambda b, j, be, nv: (0, 0)),
                pl.BlockSpec((None, d, tf), lambda b, j, be, nv: (be[blk(b, nv)], 0, ffi(b, j, nv))),
                pl.BlockSpec((None, d, tf), lambda b, j, be, nv: (be[blk(b, nv)], 0, ffi(b, j, nv))),
                pl.BlockSpec((None, tf, d), lambda b, j, be, nv: (be[blk(b, nv)], ffi(b, j, nv), 0)),
            ],
            out_specs=pl.BlockSpec((tm, d), lambda b, j, be, nv: (b, 0)),
            scratch_shapes=[pltpu.VMEM((tm, d), BF16)],
        ),
        out_shape=jax.ShapeDtypeStruct((cap, d), F32),
        compiler_params=_params("arbitrary", "arbitrary"),
        name="moe_experts",
    )(block_e, n_valid, xs, gain, w_gate, w_up, w_down)


def _combine_kernel(x_ref, rg_ref, gf_ref, y0_ref, y1_ref, o_ref, *, final_norm):
    rg = rg_ref[...]
    out = x_ref[...] + (rg[:, 0:1] * y0_ref[...] + rg[:, 1:2] * y1_ref[...])
    if final_norm:
        out = _rms(out, gf_ref[...])
    o_ref[...] = out


def _combine(x, route_g, yg, final_gain, final_norm):
    n, d = x.shape
    tm = 512
    return pl.pallas_call(
        functools.partial(_combine_kernel, final_norm=final_norm),
        grid=(n // tm,),
        in_specs=[
            pl.BlockSpec((tm, d), lambda i: (i, 0)),
            pl.BlockSpec((tm, ROUTE_COLS), lambda i: (i, 0)),
            pl.BlockSpec((1, d), lambda i: (0, 0)),
            pl.BlockSpec((None, tm, d), lambda i: (0, i, 0)),
            pl.BlockSpec((None, tm, d), lambda i: (1, i, 0)),
        ],
        out_specs=pl.BlockSpec((tm, d), lambda i: (i, 0)),
        out_shape=jax.ShapeDtypeStruct((n, d), F32),
        compiler_params=_params("parallel"),
        name="moe_combine",
    )(x, route_g, final_gain, yg, yg)


def _moe_layer(x, gain, w_router, w_gate, w_up, w_down, final_gain, final_norm):
    n, d = x.shape
    tm = 1024 if n * TOP_K >= 8 * 1024 else 256
    tf = 2 * MXU_DIM
    route_i, route_g, counts = _router(x, gain, w_router)

    counts = counts[0].astype(jnp.int32)
    padded = ((counts + tm - 1) // tm) * tm
    ends = jnp.cumsum(padded)
    starts = ends - padded
    dest = (starts[route_i[:, :TOP_K]] + route_i[:, TOP_K:2 * TOP_K]).T
    n_blocks = (n * TOP_K) // tm + N_EXPERTS
    block_e = jnp.minimum(
        jnp.searchsorted(ends, jnp.arange(n_blocks, dtype=jnp.int32) * tm, side="right"),
        N_EXPERTS - 1).astype(jnp.int32)
    n_valid = (ends[-1:] // tm).astype(jnp.int32)

    xs = _dispatch(x, dest, n_blocks * tm)
    ys = _experts(xs, gain, block_e, n_valid, w_gate, w_up, w_down, tm, tf)
    yg = _gather_rows(ys, dest.reshape(TOP_K * n)).reshape(TOP_K, n, d)
    return _combine(x, route_g, yg, final_gain, final_norm)


def _rope_tables(seq):
    pos = jnp.arange(seq, dtype=F32)
    inv_freq = ROPE_THETA ** (-jnp.arange(0, HEAD_DIM, 2, dtype=F32) / HEAD_DIM)
    ang = pos[:, None] * inv_freq[None, :]
    reps = LANES // (HEAD_DIM // 2)
    cos = jnp.tile(jnp.cos(ang), (1, reps))
    sin = jnp.tile(jnp.sin(ang), (1, reps))
    first_half = (jnp.arange(LANES) % HEAD_DIM) < HEAD_DIM // 2
    tabs = (cos, jnp.where(first_half, -sin, 0.0), jnp.where(first_half, 0.0, sin))
    q_scale = HEAD_DIM ** -0.5 * 1.4426950408889634
    return tuple(jnp.stack([t * q_scale, t]) for t in tabs)


def kernel(x, mix_norm, ffn_norm, attn_w_qkv, attn_w_o, pool_w, pool_scale, dense_w_gate, dense_w_up,
           dense_w_down, moe_router, moe_w_gate, moe_w_up, moe_w_down, final_norm):
    batch, seq, d = x.shape
    depth = mix_norm.shape[0]
    assert depth % 2 == 0 and d == N_HEADS * HEAD_DIM
    cos, sina, sinb = _rope_tables(seq)
    xf = x.reshape(batch * seq, d)
    final_gain = final_norm.reshape(1, d)
    for i in range(depth):
        j = i // 2
        mix_gain = mix_norm[i].reshape(1, d)
        ffn_gain = ffn_norm[i].reshape(1, d)
        if i % 2 == 0:
            qkv = _qkv_rope(xf, mix_gain, attn_w_qkv[j].astype(BF16), cos, sina, sinb, batch, seq)
            o = _attention(qkv, batch, seq)
            xf = _out_proj(xf, o, attn_w_o[j].astype(BF16))
            xf = _dense_ffn(xf, ffn_gain, dense_w_gate[j].astype(BF16), dense_w_up[j].astype(BF16),
                            dense_w_down[j].astype(BF16))
        else:
            xf = _pool_mixer(xf, mix_gain, pool_w[j].astype(BF16), pool_scale[j].reshape(1, d), batch, seq)
            xf = _moe_layer(xf, ffn_gain, moe_router[j], moe_w_gate[j].astype(BF16), moe_w_up[j].astype(BF16),
                            moe_w_down[j].astype(BF16), final_gain, i == depth - 1)
    return xf.reshape(batch, seq, d)
```

```python
import functools

import jax
import jax.numpy as jnp
from jax import lax
from jax.experimental import pallas as pl
from jax.experimental.pallas import tpu as pltpu
from jax.experimental.pallas import tpu_sc as plsc

WINDOWS = (128, 512, 2048)
DILATIONS = (1, 4, 16)
N_GROUPS = len(WINDOWS)
N_HEADS = 16
HEAD_DIM = 64
ROPE_THETA = 10000.0
POOL_SIZES = (2, 4, 8, 16)
N_EXPERTS = 8
TOP_K = 2
RMS_EPS = 1e-6
MASK_VALUE = -1e30
ATTN_UNROLL = 8

LANES = 128
MXU_DIM = 256
VMEM_LIMIT_BYTES = 56 * 1024 * 1024

F32 = jnp.float32
BF16 = jnp.bfloat16


def _params(*sem):
    return pltpu.CompilerParams(dimension_semantics=sem, vmem_limit_bytes=VMEM_LIMIT_BYTES)


def _rms(x, g):
    ms = jnp.mean(x * x, axis=-1, keepdims=True)
    return x * lax.rsqrt(ms + RMS_EPS) * g


def _silu(g):
    return g / (1.0 + jnp.exp(-g))


def _qkv_kernel(x_ref, g_ref, w0_ref, w1_ref, w2_ref, cos_ref, sina_ref, sinb_ref,
                o0_ref, o1_ref, o2_ref, acc_ref):
    is_v = pl.program_id(0) == 2
    tm = x_ref.shape[0]
    h = _rms(x_ref[...], g_ref[...]).astype(BF16)
    n_chunks = acc_ref.shape[1]
    for g, (w_ref, o_ref, dil) in enumerate(zip((w0_ref, w1_ref, w2_ref), (o0_ref, o1_ref, o2_ref), DILATIONS)):
        acc = jnp.dot(h, w_ref[...], preferred_element_type=F32)
        for c in range(n_chunks):
            acc_ref[g, c] = acc[:, c * LANES:(c + 1) * LANES]
        rows = tm // dil

        def strided(ref, r, dil=dil, rows=rows):
            if dil == 1:
                return ref[...]
            return ref[pl.ds(r, rows, stride=dil), :]

        for r in range(dil):
            cos = strided(cos_ref, r)
            sina = strided(sina_ref, r)
            sinb = strided(sinb_ref, r)
            for c in range(n_chunks):
                t = strided(acc_ref.at[g, c], r)
                rot = (t * cos + pltpu.roll(t, LANES - HEAD_DIM // 2, 1) * sina
                       + pltpu.roll(t, HEAD_DIM // 2, 1) * sinb)
                o_ref[r, :, c * LANES:(c + 1) * LANES] = jnp.where(is_v, t, rot).astype(BF16)


def _qkv_rope(x, gain, w_qkv, cos, sina, sinb, batch, seq):
    n, d = x.shape
    tm = 512
    tiles_per_seq = seq // tm
    tab = pl.BlockSpec((None, tm, LANES), lambda k, i: (jnp.minimum(k, 1), i % tiles_per_seq, 0))

    def w_spec(g):
        return pl.BlockSpec((d, d), lambda k, i: (0, g * 3 + k))

    def o_spec(dil):
        return pl.BlockSpec((None, None, dil, tm // dil, d),
                            lambda k, i: (k, i // tiles_per_seq, 0, i % tiles_per_seq, 0))

    outs = pl.pallas_call(
        _qkv_kernel,
        grid=(3, n // tm),
        in_specs=[
            pl.BlockSpec((tm, d), lambda k, i: (i, 0)),
            pl.BlockSpec((1, d), lambda k, i: (0, 0)),
            w_spec(0), w_spec(1), w_spec(2),
            tab, tab, tab,
        ],
        out_specs=[o_spec(dil) for dil in DILATIONS],
        out_shape=[jax.ShapeDtypeStruct((3, batch, dil, seq // dil, d), BF16) for dil in DILATIONS],
        scratch_shapes=[pltpu.VMEM((N_GROUPS, d // LANES, tm, LANES), F32)],
        compiler_params=_params("parallel", "parallel"),
        name="qkv_rope",
    )(x, gain, w_qkv, w_qkv, w_qkv, cos, sina, sinb)
    return [o.reshape(3, batch, seq, d) for o in outs]


def _attn_kernel(q0, k0, v0, q1, k1, v1, q2, k2, v2, o_ref, num_ref, max_ref, den_ref, kt_ref):
    seq = o_ref.shape[0]
    w = WINDOWS[0] // DILATIONS[0]
    lane = lax.broadcasted_iota(jnp.int32, (1, LANES), 1)
    first_head = lane < HEAD_DIM
    row2 = lax.broadcasted_iota(jnp.int32, (2 * w, 2 * w), 0) % w
    col2 = lax.broadcasted_iota(jnp.int32, (2 * w, 2 * w), 1)
    dist = row2 + w - col2
    band = (dist >= 0) & (dist <= w)
    in_current = col2 >= w
    causal = (lax.broadcasted_iota(jnp.int32, (2 * w, w), 0) % w
              >= lax.broadcasted_iota(jnp.int32, (2 * w, w), 1))

    def tile(q_ref, v_ref, qoff, n, has_prev):
        q = q_ref[pl.ds(qoff, w), :]
        zero = jnp.zeros_like(q)
        qq = jnp.concatenate([jnp.where(first_head, q, zero), jnp.where(first_head, zero, q)], axis=0)
        if has_prev:
            poff = pl.multiple_of(jnp.maximum(qoff - w, 0), w)
            keys_t = jnp.concatenate([kt_ref[:, pl.ds(poff, w)], kt_ref[:, pl.ds(qoff, w)]], axis=1)
            vals = jnp.concatenate([v_ref[pl.ds(poff, w), :], v_ref[pl.ds(qoff, w), :]], axis=0)
            valid = band & (in_current | (n > 0))
        else:
            keys_t = kt_ref[:, pl.ds(qoff, w)]
            vals = v_ref[pl.ds(qoff, w), :]
            valid = causal
        s = jnp.dot(qq, keys_t, preferred_element_type=F32)
        s = jnp.where(valid, s, MASK_VALUE)
        mx = jnp.max(s, axis=1, keepdims=True)
        p = jnp.exp2(s - mx)
        den = jnp.sum(p, axis=1, keepdims=True)
        pv = jnp.dot(p.astype(BF16), vals, preferred_element_type=F32)
        return tuple(jnp.where(first_head, a[:w], a[w:]) for a in (pv, mx, den))

    order = sorted(range(N_GROUPS), key=lambda g: -DILATIONS[g])
    for g in order:
        q_ref, k_ref, v_ref = ((q0, k0, v0), (q1, k1, v1), (q2, k2, v2))[g]
        dil = DILATIONS[g]
        sub_len = seq // dil
        nb = sub_len // w

        kt_ref[...] = k_ref[...].T

        def body(t, carry, q_ref=q_ref, v_ref=v_ref, g=g, dil=dil, sub_len=sub_len, nb=nb):
            r = t // nb
            n = t % nb
            qoff = pl.multiple_of(r * sub_len + n * w, w)
            pv, mx, den = tile(q_ref, v_ref, qoff, n, nb > 1)
            if dil == 1:
                rows = pl.ds(qoff, w)
            else:
                rows = pl.ds(r + dil * n * w, w, stride=dil)
            if g == order[0]:
                num_ref[rows, :] = pv
                max_ref[rows, :] = mx
                den_ref[rows, :] = den
                return carry
            old_max = max_ref[rows, :]
            new_max = jnp.maximum(old_max, mx)
            a = jnp.exp2(old_max - new_max)
            b = jnp.exp2(mx - new_max)
            num = a * num_ref[rows, :] + b * pv
            den = a * den_ref[rows, :] + b * den
            if g == order[-1]:
                o_ref[rows, :] = (num / den).astype(BF16)
            else:
                num_ref[rows, :] = num
                den_ref[rows, :] = den
                max_ref[rows, :] = new_max
            return carry

        lax.fori_loop(0, dil * nb, body, 0, unroll=ATTN_UNROLL)


def _attention(qkv_groups, batch, seq):
    d_model = N_HEADS * HEAD_DIM
    in_specs, args = [], []
    for arr in qkv_groups:
        for kind in range(3):
            in_specs.append(pl.BlockSpec((None, None, seq, LANES), lambda b, hp, kind=kind: (kind, b, 0, hp)))
            args.append(arr)
    return pl.pallas_call(
        _attn_kernel,
        grid=(batch, d_model // LANES),
        in_specs=in_specs,
        out_specs=pl.BlockSpec((None, seq, LANES), lambda b, hp: (b, 0, hp)),
        out_shape=jax.ShapeDtypeStruct((batch, seq, d_model), BF16),
        scratch_shapes=[pltpu.VMEM((seq, LANES), F32)] * 3 + [pltpu.VMEM((LANES, seq), BF16)],
        compiler_params=_params("parallel", "parallel"),
        name="attention",
    )(*args).reshape(batch * seq, d_model)


def _proj_dense_kernel(x_ref, o_ref, wo_ref, g_ref, wg_ref, wu_ref, wd_ref, y_ref):
    x = x_ref[...] + jnp.dot(o_ref[...], wo_ref[...], preferred_element_type=F32)
    h = _rms(x, g_ref[...]).astype(BF16)
    gate = jnp.dot(h, wg_ref[...], preferred_element_type=F32)
    up = jnp.dot(h, wu_ref[...], preferred_element_type=F32)
    a = (_silu(gate) * up).astype(BF16)
    y_ref[...] = x + jnp.dot(a, wd_ref[...], preferred_element_type=F32)


def _proj_dense_ffn(x, o, w_o, gain, w_gate, w_up, w_down):
    n, d = x.shape
    ff = w_gate.shape[1]
    tm = 256
    row = pl.BlockSpec((tm, d), lambda i: (i, 0))
    const = lambda shape: pl.BlockSpec(shape, lambda i: (0, 0), pipeline_mode=pl.Buffered(1))
    return pl.pallas_call(
        _proj_dense_kernel,
        grid=(n // tm,),
        in_specs=[row, row, const((d, d)), const((1, d)), const((d, ff)), const((d, ff)), const((ff, d))],
        out_specs=row,
        out_shape=jax.ShapeDtypeStruct((n, d), F32),
        compiler_params=_params("parallel"),
        name="proj_dense_ffn",
    )(x, o, w_o, gain, w_gate, w_up, w_down)


POOL_HALO = max(POOL_SIZES)


def _pool_kernel(x_ref, halo_ref, g_ref, w_ref, sc_ref, y_ref):
    i = pl.program_id(1)
    ts = x_ref.shape[0]
    g = g_ref[...]
    x = x_ref[...]
    h = _rms(x, g)
    hh = jnp.where(i > 0, _rms(halo_ref[...], g), 0.0)
    ext = jnp.concatenate([hh, h], axis=0)
    pos = i * ts + lax.broadcasted_iota(jnp.int32, (ts, 1), 0)
    pc = w_ref.shape[1]
    for grp, size in enumerate(POOL_SIZES):
        sl = slice(grp * pc, (grp + 1) * pc)
        s = ext[:, sl]
        step = 1
        while step < size:
            s = s + pltpu.roll(s, step, 0)
            step *= 2
        cnt = jnp.minimum(pos + 1, size).astype(F32)
        y = s[POOL_HALO:, :] / cnt - h[:, sl]
        z = jnp.dot(y.astype(BF16), w_ref[grp], preferred_element_type=F32)
        y_ref[:, sl] = x[:, sl] + z * sc_ref[:, sl]


def _pool_mixer(x, gain, pool_w, scale, batch, seq):
    n, d = x.shape
    ts = 512
    x3 = x.reshape(batch, seq, d)
    hb = ts // POOL_HALO
    out = pl.pallas_call(
        _pool_kernel,
        grid=(batch, seq // ts),
        in_specs=[
            pl.BlockSpec((None, ts, d), lambda b, i: (b, i, 0)),
            pl.BlockSpec((None, POOL_HALO, d), lambda b, i: (b, jnp.maximum(i * hb - 1, 0), 0)),
            pl.BlockSpec((1, d), lambda b, i: (0, 0)),
            pl.BlockSpec(pool_w.shape, lambda b, i: (0, 0, 0)),
            pl.BlockSpec((1, d), lambda b, i: (0, 0)),
        ],
        out_specs=pl.BlockSpec((None, ts, d), lambda b, i: (b, i, 0)),
        out_shape=jax.ShapeDtypeStruct((batch, seq, d), F32),
        compiler_params=_params("parallel", "arbitrary"),
        name="pool_mixer",
    )(x3, x3, gain, pool_w, scale)
    return out.reshape(n, d)


ROUTE_COLS = 8


def _router_kernel(x_ref, g_ref, wr_ref, tril_ref, ri_ref, rg_ref, cnt_ref, run_ref):
    i = pl.program_id(0)

    @pl.when(i == 0)
    def _():
        run_ref[...] = jnp.zeros_like(run_ref)

    h = _rms(x_ref[...], g_ref[...])
    wr = wr_ref[...]
    ne = wr.shape[1]
    h_hi = h.astype(BF16)
    h_lo = (h - h_hi.astype(F32)).astype(BF16)
    w_hi = wr.astype(BF16)
    w_lo = (wr - w_hi.astype(F32)).astype(BF16)
    hi_part = jnp.dot(h_hi, jnp.concatenate([w_hi, w_lo], axis=1), preferred_element_type=F32)
    logits = hi_part[:, :ne] + hi_part[:, ne:] + jnp.dot(h_lo, w_hi, preferred_element_type=F32)
    tm = logits.shape[0]
    lane = lax.broadcasted_iota(jnp.int32, (tm, ne), 1)
    v1 = jnp.max(logits, axis=1, keepdims=True)
    i1 = jnp.min(jnp.where(logits == v1, lane, ne), axis=1, keepdims=True)
    rest = jnp.where(lane == i1, -jnp.inf, logits)
    v2 = jnp.max(rest, axis=1, keepdims=True)
    i2 = jnp.min(jnp.where(rest == v2, lane, ne), axis=1, keepdims=True)
    e = jnp.exp(v2 - v1)
    g1 = 1.0 / (1.0 + e)
    g2 = e / (1.0 + e)
    oh1 = (lane == i1).astype(F32)
    oh2 = (lane == i2).astype(F32)
    both = oh1 + oh2
    before = jnp.dot(tril_ref[...], both.astype(BF16), preferred_element_type=F32) + run_ref[...]
    r1 = jnp.sum(before * oh1, axis=1, keepdims=True).astype(jnp.int32)
    r2 = jnp.sum(before * oh2, axis=1, keepdims=True).astype(jnp.int32)
    run_ref[...] += jnp.sum(both, axis=0, keepdims=True)
    ri_ref[...] = jnp.where(lane == 0, i1, jnp.where(lane == 1, i2, jnp.where(lane == 2, r1, r2)))
    rg_ref[...] = jnp.where(lane == 0, g1, g2)
    cnt_ref[...] = run_ref[...]


def _router(x, gain, w_router):
    n, d = x.shape
    ne = w_router.shape[1]
    tm = 512
    tril = (jnp.arange(tm)[:, None] > jnp.arange(tm)[None, :]).astype(BF16)
    return pl.pallas_call(
        _router_kernel,
        grid=(n // tm,),
        in_specs=[
            pl.BlockSpec((tm, d), lambda i: (i, 0)),
            pl.BlockSpec((1, d), lambda i: (0, 0)),
            pl.BlockSpec((d, ne), lambda i: (0, 0)),
            pl.BlockSpec((tm, tm), lambda i: (0, 0)),
        ],
        out_specs=[
            pl.BlockSpec((tm, ROUTE_COLS), lambda i: (i, 0)),
            pl.BlockSpec((tm, ROUTE_COLS), lambda i: (i, 0)),
            pl.BlockSpec((1, ne), lambda i: (0, 0)),
        ],
        out_shape=[
            jax.ShapeDtypeStruct((n, ROUTE_COLS), jnp.int32),
            jax.ShapeDtypeStruct((n, ROUTE_COLS), F32),
            jax.ShapeDtypeStruct((1, ne), F32),
        ],
        scratch_shapes=[pltpu.VMEM((1, ne), F32)],
        compiler_params=_params("arbitrary"),
        name="moe_router",
    )(x, gain, w_router, tril)


SC_WINDOW = 64


def _sc_workers():
    info = plsc.get_sparse_core_info()
    mesh = plsc.VectorSubcoreMesh(core_axis_name="core", subcore_axis_name="subcore")
    return mesh, info.num_cores, info.num_cores * info.num_subcores


def _dispatch(x, dest, cap):
    n, d = x.shape
    mesh, n_cores, n_workers = _sc_workers()
    per_worker = n // n_workers

    @functools.partial(
        pl.kernel, mesh=mesh, out_type=jax.ShapeDtypeStruct((cap, d), x.dtype),
        scratch_types=[pltpu.VMEM((SC_WINDOW,), jnp.int32)] * TOP_K
        + [pltpu.VMEM((SC_WINDOW, d), x.dtype), pltpu.SemaphoreType.DMA],
        name="moe_dispatch")
    def scatter(x_hbm, *rest):
        idx_hbm, (xs_hbm, *idx_vmem, rows, sem) = rest[:TOP_K], rest[TOP_K:]
        wid = lax.axis_index("subcore") * n_cores + lax.axis_index("core")

        @pl.loop(0, per_worker // SC_WINDOW)
        def _(c):
            base = wid * per_worker + c * SC_WINDOW
            for k in range(TOP_K):
                pltpu.sync_copy(idx_hbm[k].at[pl.ds(base, SC_WINDOW)], idx_vmem[k])
            pltpu.sync_copy(x_hbm.at[pl.ds(base, SC_WINDOW)], rows)
            for k in range(TOP_K):
                pltpu.async_copy(rows, xs_hbm.at[idx_vmem[k]], sem).wait()

    return scatter(x, *[dest[k] for k in range(TOP_K)])


def _gather_rows(table, idx):
    m = idx.shape[0]
    d = table.shape[1]
    mesh, n_cores, n_workers = _sc_workers()
    per_worker = m // n_workers

    @functools.partial(
        pl.kernel, mesh=mesh, out_type=jax.ShapeDtypeStruct((m, d), table.dtype),
        scratch_types=[pltpu.VMEM((SC_WINDOW,), jnp.int32), pltpu.VMEM((SC_WINDOW, d), table.dtype),
                       pltpu.SemaphoreType.DMA],
        name="moe_gather")
    def gather(t_hbm, i_hbm, o_hbm, idx_vmem, rows, sem):
        wid = lax.axis_index("subcore") * n_cores + lax.axis_index("core")

        @pl.loop(0, per_worker // SC_WINDOW)
        def _(c):
            base = wid * per_worker + c * SC_WINDOW
            pltpu.sync_copy(i_hbm.at[pl.ds(base, SC_WINDOW)], idx_vmem)
            pltpu.async_copy(t_hbm.at[idx_vmem], rows, sem).wait()
            pltpu.sync_copy(rows, o_hbm.at[pl.ds(base, SC_WINDOW)])

    return gather(table, idx)


def _expert_kernel(be_ref, nv_ref, xs_ref, g_ref, wg_ref, wu_ref, wd_ref, ys_ref, xb_ref):
    del be_ref
    b = pl.program_id(0)
    j = pl.program_id(1)

    @pl.when(b < nv_ref[0])
    def _():
        @pl.when(j == 0)
        def _():
            xb_ref[...] = _rms(xs_ref[...], g_ref[...]).astype(BF16)

        x = xb_ref[...]
        gate = jnp.dot(x, wg_ref[...].astype(BF16), preferred_element_type=F32)
        up = jnp.dot(x, wu_ref[...].astype(BF16), preferred_element_type=F32)
        a = (_silu(gate) * up).astype(BF16)
        part = jnp.dot(a, wd_ref[...].astype(BF16), preferred_element_type=F32)

        @pl.when(j == 0)
        def _():
            ys_ref[...] = part

        @pl.when(j > 0)
        def _():
            ys_ref[...] += part

    @pl.when((b >= nv_ref[0]) & (j == 0))
    def _():
        ys_ref[...] = jnp.zeros_like(ys_ref)


def _experts(xs, gain, block_e, n_valid, w_gate, w_up, w_down, layer, tm, tf):
    cap, d = xs.shape
    ff = w_gate.shape[3]
    nj = ff // tf

    def blk(b, nv):
        return jnp.minimum(b, nv[0] - 1)

    def ffi(b, j, nv):
        return jnp.where(b < nv[0], j, nj - 1)

    return pl.pallas_call(
        _expert_kernel,
        grid_spec=pltpu.PrefetchScalarGridSpec(
            num_scalar_prefetch=2,
            grid=(cap // tm, nj),
            in_specs=[
                pl.BlockSpec((tm, d), lambda b, j, be, nv: (blk(b, nv), 0)),
                pl.BlockSpec((1, d), lambda b, j, be, nv: (0, 0)),
                pl.BlockSpec((None, None, d, tf), lambda b, j, be, nv: (layer, be[blk(b, nv)], 0, ffi(b, j, nv))),
                pl.BlockSpec((None, None, d, tf), lambda b, j, be, nv: (layer, be[blk(b, nv)], 0, ffi(b, j, nv))),
                pl.BlockSpec((None, None, tf, d), lambda b, j, be, nv: (layer, be[blk(b, nv)], ffi(b, j, nv), 0)),
            ],
            out_specs=pl.BlockSpec((tm, d), lambda b, j, be, nv: (b, 0)),
            scratch_shapes=[pltpu.VMEM((tm, d), BF16)],
        ),
        out_shape=jax.ShapeDtypeStruct((cap, d), F32),
        compiler_params=_params("arbitrary", "arbitrary"),
        name="moe_experts",
    )(block_e, n_valid, xs, gain, w_gate, w_up, w_down)


def _combine_kernel(x_ref, rg_ref, gf_ref, y0_ref, y1_ref, o_ref, *, final_norm):
    rg = rg_ref[...]
    out = x_ref[...] + (rg[:, 0:1] * y0_ref[...] + rg[:, 1:2] * y1_ref[...])
    if final_norm:
        out = _rms(out, gf_ref[...])
    o_ref[...] = out


def _combine(x, route_g, yg, final_gain, final_norm):
    n, d = x.shape
    tm = 512
    return pl.pallas_call(
        functools.partial(_combine_kernel, final_norm=final_norm),
        grid=(n // tm,),
        in_specs=[
            pl.BlockSpec((tm, d), lambda i: (i, 0)),
            pl.BlockSpec((tm, ROUTE_COLS), lambda i: (i, 0)),
            pl.BlockSpec((1, d), lambda i: (0, 0)),
            pl.BlockSpec((None, tm, d), lambda i: (0, i, 0)),
            pl.BlockSpec((None, tm, d), lambda i: (1, i, 0)),
        ],
        out_specs=pl.BlockSpec((tm, d), lambda i: (i, 0)),
        out_shape=jax.ShapeDtypeStruct((n, d), F32),
        compiler_params=_params("parallel"),
        name="moe_combine",
    )(x, route_g, final_gain, yg, yg)


def _moe_layer(x, gain, w_router, w_gate, w_up, w_down, layer, final_gain, final_norm):
    n, d = x.shape
    tm = 1024 if n * TOP_K >= 8 * 1024 else 256
    tf = 2 * MXU_DIM
    route_i, route_g, counts = _router(x, gain, w_router)

    counts = counts[0].astype(jnp.int32)
    padded = ((counts + tm - 1) // tm) * tm
    ends = jnp.cumsum(padded)
    starts = ends - padded
    dest = (starts[route_i[:, :TOP_K]] + route_i[:, TOP_K:2 * TOP_K]).T
    n_blocks = (n * TOP_K) // tm + N_EXPERTS
    block_e = jnp.minimum(
        jnp.searchsorted(ends, jnp.arange(n_blocks, dtype=jnp.int32) * tm, side="right"),
        N_EXPERTS - 1).astype(jnp.int32)
    n_valid = (ends[-1:] // tm).astype(jnp.int32)

    xs = _dispatch(x, dest, n_blocks * tm)
    ys = _experts(xs, gain, block_e, n_valid, w_gate, w_up, w_down, layer, tm, tf)
    yg = _gather_rows(ys, dest.reshape(TOP_K * n)).reshape(TOP_K, n, d)
    return _combine(x, route_g, yg, final_gain, final_norm)


def _rope_tables(seq):
    pos = jnp.arange(seq, dtype=F32)
    inv_freq = ROPE_THETA ** (-jnp.arange(0, HEAD_DIM, 2, dtype=F32) / HEAD_DIM)
    ang = pos[:, None] * inv_freq[None, :]
    reps = LANES // (HEAD_DIM // 2)
    cos = jnp.tile(jnp.cos(ang), (1, reps))
    sin = jnp.tile(jnp.sin(ang), (1, reps))
    first_half = (jnp.arange(LANES) % HEAD_DIM) < HEAD_DIM // 2
    tabs = (cos, jnp.where(first_half, -sin, 0.0), jnp.where(first_half, 0.0, sin))
    q_scale = HEAD_DIM ** -0.5 * 1.4426950408889634
    return tuple(jnp.stack([t * q_scale, t]) for t in tabs)


def kernel(x, mix_norm, ffn_norm, attn_w_qkv, attn_w_o, pool_w, pool_scale, dense_w_gate, dense_w_up,
           dense_w_down, moe_router, moe_w_gate, moe_w_up, moe_w_down, final_norm):
    batch, seq, d = x.shape
    depth = mix_norm.shape[0]
    assert depth % 2 == 0 and d == N_HEADS * HEAD_DIM
    cos, sina, sinb = _rope_tables(seq)
    xf = x.reshape(batch * seq, d)
    final_gain = final_norm.reshape(1, d)
    for i in range(depth):
        j = i // 2
        mix_gain = mix_norm[i].reshape(1, d)
        ffn_gain = ffn_norm[i].reshape(1, d)
        if i % 2 == 0:
            qkv = _qkv_rope(xf, mix_gain, attn_w_qkv[j].astype(BF16), cos, sina, sinb, batch, seq)
            o = _attention(qkv, batch, seq)
            xf = _proj_dense_ffn(xf, o, attn_w_o[j].astype(BF16), ffn_gain, dense_w_gate[j].astype(BF16),
                                 dense_w_up[j].astype(BF16), dense_w_down[j].astype(BF16))
        else:
            xf = _pool_mixer(xf, mix_gain, pool_w[j].astype(BF16), pool_scale[j].reshape(1, d), batch, seq)
            xf = _moe_layer(xf, ffn_gain, moe_router[j], moe_w_gate, moe_w_up, moe_w_down, j,
                            final_gain, i == depth - 1)
    return xf.reshape(batch, seq, d)
```

```python
import functools

import jax
import jax.numpy as jnp
from jax import lax
from jax.experimental import pallas as pl
from jax.experimental.pallas import tpu as pltpu
from jax.experimental.pallas import tpu_sc as plsc

WINDOWS = (128, 512, 2048)
DILATIONS = (1, 4, 16)
N_GROUPS = len(WINDOWS)
N_HEADS = 16
HEAD_DIM = 64
ROPE_THETA = 10000.0
POOL_SIZES = (2, 4, 8, 16)
N_EXPERTS = 8
TOP_K = 2
RMS_EPS = 1e-6
MASK_VALUE = -1e30
ATTN_UNROLL = 8

LANES = 128
MXU_DIM = 256
VMEM_LIMIT_BYTES = 56 * 1024 * 1024

F32 = jnp.float32
BF16 = jnp.bfloat16


def _params(*sem):
    return pltpu.CompilerParams(dimension_semantics=sem, vmem_limit_bytes=VMEM_LIMIT_BYTES)


def _rms(x, g):
    ms = jnp.mean(x * x, axis=-1, keepdims=True)
    return x * lax.rsqrt(ms + RMS_EPS) * g


def _silu(g):
    return g / (1.0 + jnp.exp(-g))


def _qkv_kernel(x_ref, g_ref, w0_ref, w1_ref, w2_ref, cos_ref, sina_ref, sinb_ref,
                o0_ref, o1_ref, o2_ref, acc_ref):
    is_v = pl.program_id(0) == 2
    tm = x_ref.shape[0]
    h = _rms(x_ref[...], g_ref[...]).astype(BF16)
    n_chunks = acc_ref.shape[1]
    for g, (w_ref, o_ref, dil) in enumerate(zip((w0_ref, w1_ref, w2_ref), (o0_ref, o1_ref, o2_ref), DILATIONS)):
        acc = jnp.dot(h, w_ref[...], preferred_element_type=F32)
        for c in range(n_chunks):
            acc_ref[g, c] = acc[:, c * LANES:(c + 1) * LANES]
        rows = tm // dil

        def strided(ref, r, dil=dil, rows=rows):
            if dil == 1:
                return ref[...]
            return ref[pl.ds(r, rows, stride=dil), :]

        for r in range(dil):
            cos = strided(cos_ref, r)
            sina = strided(sina_ref, r)
            sinb = strided(sinb_ref, r)
            for c in range(n_chunks):
                t = strided(acc_ref.at[g, c], r)
                rot = (t * cos + pltpu.roll(t, LANES - HEAD_DIM // 2, 1) * sina
                       + pltpu.roll(t, HEAD_DIM // 2, 1) * sinb)
                o_ref[r, :, c * LANES:(c + 1) * LANES] = jnp.where(is_v, t, rot).astype(BF16)


def _qkv_rope(x, gain, w_qkv, cos, sina, sinb, batch, seq):
    n, d = x.shape
    tm = 512
    tiles_per_seq = seq // tm
    tab = pl.BlockSpec((None, tm, LANES), lambda k, i: (jnp.minimum(k, 1), i % tiles_per_seq, 0))

    def w_spec(g):
        return pl.BlockSpec((d, d), lambda k, i: (0, g * 3 + k))

    def o_spec(dil):
        return pl.BlockSpec((None, None, dil, tm // dil, d),
                            lambda k, i: (k, i // tiles_per_seq, 0, i % tiles_per_seq, 0))

    outs = pl.pallas_call(
        _qkv_kernel,
        grid=(3, n // tm),
        in_specs=[
            pl.BlockSpec((tm, d), lambda k, i: (i, 0)),
            pl.BlockSpec((1, d), lambda k, i: (0, 0)),
            w_spec(0), w_spec(1), w_spec(2),
            tab, tab, tab,
        ],
        out_specs=[o_spec(dil) for dil in DILATIONS],
        out_shape=[jax.ShapeDtypeStruct((3, batch, dil, seq // dil, d), BF16) for dil in DILATIONS],
        scratch_shapes=[pltpu.VMEM((N_GROUPS, d // LANES, tm, LANES), F32)],
        compiler_params=_params("parallel", "parallel"),
        name="qkv_rope",
    )(x, gain, w_qkv, w_qkv, w_qkv, cos, sina, sinb)
    return [o.reshape(3, batch, seq, d) for o in outs]


def _attn_kernel(q0, k0, v0, q1, k1, v1, q2, k2, v2, o_ref, num_ref, max_ref, den_ref, kt_ref):
    seq = o_ref.shape[0]
    w = WINDOWS[0] // DILATIONS[0]
    lane = lax.broadcasted_iota(jnp.int32, (1, LANES), 1)
    first_head = lane < HEAD_DIM
    row2 = lax.broadcasted_iota(jnp.int32, (2 * w, 2 * w), 0) % w
    col2 = lax.broadcasted_iota(jnp.int32, (2 * w, 2 * w), 1)
    dist = row2 + w - col2
    band = (dist >= 0) & (dist <= w)
    in_current = col2 >= w
    causal = (lax.broadcasted_iota(jnp.int32, (2 * w, w), 0) % w
              >= lax.broadcasted_iota(jnp.int32, (2 * w, w), 1))

    def tile(q_ref, v_ref, qoff, n, has_prev):
        q = q_ref[pl.ds(qoff, w), :]
        zero = jnp.zeros_like(q)
        qq = jnp.concatenate([jnp.where(first_head, q, zero), jnp.where(first_head, zero, q)], axis=0)
        if has_prev:
            poff = pl.multiple_of(jnp.maximum(qoff - w, 0), w)
            keys_t = jnp.concatenate([kt_ref[:, pl.ds(poff, w)], kt_ref[:, pl.ds(qoff, w)]], axis=1)
            vals = jnp.concatenate([v_ref[pl.ds(poff, w), :], v_ref[pl.ds(qoff, w), :]], axis=0)
            valid = band & (in_current | (n > 0))
        else:
            keys_t = kt_ref[:, pl.ds(qoff, w)]
            vals = v_ref[pl.ds(qoff, w), :]
            valid = causal
        s = jnp.dot(qq, keys_t, preferred_element_type=F32)
        s = jnp.where(valid, s, MASK_VALUE)
        mx = jnp.max(s, axis=1, keepdims=True)
        p = jnp.exp2(s - mx)
        den = jnp.sum(p, axis=1, keepdims=True)
        pv = jnp.dot(p.astype(BF16), vals, preferred_element_type=F32)
        return tuple(jnp.where(first_head, a[:w], a[w:]) for a in (pv, mx, den))

    order = sorted(range(N_GROUPS), key=lambda g: -DILATIONS[g])
    for g in order:
        q_ref, k_ref, v_ref = ((q0, k0, v0), (q1, k1, v1), (q2, k2, v2))[g]
        dil = DILATIONS[g]
        sub_len = seq // dil
        nb = sub_len // w

        kt_ref[...] = k_ref[...].T

        def body(t, carry, q_ref=q_ref, v_ref=v_ref, g=g, dil=dil, sub_len=sub_len, nb=nb):
            r = t // nb
            n = t % nb
            qoff = pl.multiple_of(r * sub_len + n * w, w)
            pv, mx, den = tile(q_ref, v_ref, qoff, n, nb > 1)
            if dil == 1:
                rows = pl.ds(qoff, w)
            else:
                rows = pl.ds(r + dil * n * w, w, stride=dil)
            if g == order[0]:
                num_ref[rows, :] = pv
                max_ref[rows, :] = mx
                den_ref[rows, :] = den
                return carry
            old_max = max_ref[rows, :]
            new_max = jnp.maximum(old_max, mx)
            a = jnp.exp2(old_max - new_max)
            b = jnp.exp2(mx - new_max)
            num = a * num_ref[rows, :] + b * pv
            den = a * den_ref[rows, :] + b * den
            if g == order[-1]:
                o_ref[rows, :] = (num / den).astype(BF16)
            else:
                num_ref[rows, :] = num
                den_ref[rows, :] = den
                max_ref[rows, :] = new_max
            return carry

        lax.fori_loop(0, dil * nb, body, 0, unroll=ATTN_UNROLL)


def _attention(qkv_groups, batch, seq):
    d_model = N_HEADS * HEAD_DIM
    in_specs, args = [], []
    for arr in qkv_groups:
        for kind in range(3):
            in_specs.append(pl.BlockSpec((None, None, seq, LANES), lambda b, hp, kind=kind: (kind, b, 0, hp)))
            args.append(arr)
    return pl.pallas_call(
        _attn_kernel,
        grid=(batch, d_model // LANES),
        in_specs=in_specs,
        out_specs=pl.BlockSpec((None, seq, LANES), lambda b, hp: (b, 0, hp)),
        out_shape=jax.ShapeDtypeStruct((batch, seq, d_model), BF16),
        scratch_shapes=[pltpu.VMEM((seq, LANES), F32)] * 3 + [pltpu.VMEM((LANES, seq), BF16)],
        compiler_params=_params("parallel", "parallel"),
        name="attention",
    )(*args).reshape(batch * seq, d_model)


def _proj_dense_kernel(x_ref, o_ref, wo_ref, g_ref, wg_ref, wu_ref, wd_ref, y_ref):
    x = x_ref[...] + jnp.dot(o_ref[...], wo_ref[...], preferred_element_type=F32)
    h = _rms(x, g_ref[...]).astype(BF16)
    gate = jnp.dot(h, wg_ref[...], preferred_element_type=F32)
    up = jnp.dot(h, wu_ref[...], preferred_element_type=F32)
    a = (_silu(gate) * up).astype(BF16)
    y_ref[...] = x + jnp.dot(a, wd_ref[...], preferred_element_type=F32)


def _proj_dense_ffn(x, o, w_o, gain, w_gate, w_up, w_down):
    n, d = x.shape
    ff = w_gate.shape[1]
    tm = 256
    row = pl.BlockSpec((tm, d), lambda i: (i, 0))
    const = lambda shape: pl.BlockSpec(shape, lambda i: (0, 0), pipeline_mode=pl.Buffered(1))
    return pl.pallas_call(
        _proj_dense_kernel,
        grid=(n // tm,),
        in_specs=[row, row, const((d, d)), const((1, d)), const((d, ff)), const((d, ff)), const((ff, d))],
        out_specs=row,
        out_shape=jax.ShapeDtypeStruct((n, d), F32),
        compiler_params=_params("parallel"),
        name="proj_dense_ffn",
    )(x, o, w_o, gain, w_gate, w_up, w_down)


POOL_HALO = max(POOL_SIZES)


def _pool_kernel(x_ref, halo_ref, g_ref, w_ref, sc_ref, y_ref):
    i = pl.program_id(1)
    ts = x_ref.shape[0]
    g = g_ref[...]
    x = x_ref[...]
    h = _rms(x, g)
    hh = jnp.where(i > 0, _rms(halo_ref[...], g), 0.0)
    ext = jnp.concatenate([hh, h], axis=0)
    pos = i * ts + lax.broadcasted_iota(jnp.int32, (ts, 1), 0)
    pc = w_ref.shape[1]
    for grp, size in enumerate(POOL_SIZES):
        sl = slice(grp * pc, (grp + 1) * pc)
        s = ext[:, sl]
        step = 1
        while step < size:
            s = s + pltpu.roll(s, step, 0)
            step *= 2
        cnt = jnp.minimum(pos + 1, size).astype(F32)
        y = s[POOL_HALO:, :] / cnt - h[:, sl]
        z = jnp.dot(y.astype(BF16), w_ref[grp], preferred_element_type=F32)
        y_ref[:, sl] = x[:, sl] + z * sc_ref[:, sl]


def _pool_mixer(x, gain, pool_w, scale, batch, seq):
    n, d = x.shape
    ts = 512
    x3 = x.reshape(batch, seq, d)
    hb = ts // POOL_HALO
    out = pl.pallas_call(
        _pool_kernel,
        grid=(batch, seq // ts),
        in_specs=[
            pl.BlockSpec((None, ts, d), lambda b, i: (b, i, 0)),
            pl.BlockSpec((None, POOL_HALO, d), lambda b, i: (b, jnp.maximum(i * hb - 1, 0), 0)),
            pl.BlockSpec((1, d), lambda b, i: (0, 0)),
            pl.BlockSpec(pool_w.shape, lambda b, i: (0, 0, 0)),
            pl.BlockSpec((1, d), lambda b, i: (0, 0)),
        ],
        out_specs=pl.BlockSpec((None, ts, d), lambda b, i: (b, i, 0)),
        out_shape=jax.ShapeDtypeStruct((batch, seq, d), F32),
        compiler_params=_params("parallel", "arbitrary"),
        name="pool_mixer",
    )(x3, x3, gain, pool_w, scale)
    return out.reshape(n, d)


ROUTE_COLS = 8


def _router_kernel(x_ref, g_ref, wr_ref, tril_ref, ri_ref, rg_ref, cnt_ref, run_ref):
    i = pl.program_id(0)

    @pl.when(i == 0)
    def _():
        run_ref[...] = jnp.zeros_like(run_ref)

    h = _rms(x_ref[...], g_ref[...])
    wr = wr_ref[...]
    ne = wr.shape[1]
    h_hi = h.astype(BF16)
    h_lo = (h - h_hi.astype(F32)).astype(BF16)
    w_hi = wr.astype(BF16)
    w_lo = (wr - w_hi.astype(F32)).astype(BF16)
    hi_part = jnp.dot(h_hi, jnp.concatenate([w_hi, w_lo], axis=1), preferred_element_type=F32)
    logits = hi_part[:, :ne] + hi_part[:, ne:] + jnp.dot(h_lo, w_hi, preferred_element_type=F32)
    tm = logits.shape[0]
    lane = lax.broadcasted_iota(jnp.int32, (tm, ne), 1)
    v1 = jnp.max(logits, axis=1, keepdims=True)
    i1 = jnp.min(jnp.where(logits == v1, lane, ne), axis=1, keepdims=True)
    rest = jnp.where(lane == i1, -jnp.inf, logits)
    v2 = jnp.max(rest, axis=1, keepdims=True)
    i2 = jnp.min(jnp.where(rest == v2, lane, ne), axis=1, keepdims=True)
    e = jnp.exp(v2 - v1)
    g1 = 1.0 / (1.0 + e)
    g2 = e / (1.0 + e)
    oh1 = (lane == i1).astype(F32)
    oh2 = (lane == i2).astype(F32)
    both = oh1 + oh2
    before = jnp.dot(tril_ref[...], both.astype(BF16), preferred_element_type=F32) + run_ref[...]
    r1 = jnp.sum(before * oh1, axis=1, keepdims=True).astype(jnp.int32)
    r2 = jnp.sum(before * oh2, axis=1, keepdims=True).astype(jnp.int32)
    run_ref[...] += jnp.sum(both, axis=0, keepdims=True)
    ri_ref[...] = jnp.where(lane == 0, i1, jnp.where(lane == 1, i2, jnp.where(lane == 2, r1, r2)))
    rg_ref[...] = jnp.where(lane == 0, g1, g2)
    cnt_ref[...] = run_ref[...]


def _router(x, gain, w_router):
    n, d = x.shape
    ne = w_router.shape[1]
    tm = 512
    tril = (jnp.arange(tm)[:, None] > jnp.arange(tm)[None, :]).astype(BF16)
    return pl.pallas_call(
        _router_kernel,
        grid=(n // tm,),
        in_specs=[
            pl.BlockSpec((tm, d), lambda i: (i, 0)),
            pl.BlockSpec((1, d), lambda i: (0, 0)),
            pl.BlockSpec((d, ne), lambda i: (0, 0)),
            pl.BlockSpec((tm, tm), lambda i: (0, 0)),
        ],
        out_specs=[
            pl.BlockSpec((tm, ROUTE_COLS), lambda i: (i, 0)),
            pl.BlockSpec((tm, ROUTE_COLS), lambda i: (i, 0)),
            pl.BlockSpec((1, ne), lambda i: (0, 0)),
        ],
        out_shape=[
            jax.ShapeDtypeStruct((n, ROUTE_COLS), jnp.int32),
            jax.ShapeDtypeStruct((n, ROUTE_COLS), F32),
            jax.ShapeDtypeStruct((1, ne), F32),
        ],
        scratch_shapes=[pltpu.VMEM((1, ne), F32)],
        compiler_params=_params("arbitrary"),
        name="moe_router",
    )(x, gain, w_router, tril)


SC_WINDOW = 64


def _sc_workers():
    info = plsc.get_sparse_core_info()
    mesh = plsc.VectorSubcoreMesh(core_axis_name="core", subcore_axis_name="subcore")
    return mesh, info.num_cores, info.num_cores * info.num_subcores


def _dispatch(x, dest, cap):
    n, d = x.shape
    mesh, n_cores, n_workers = _sc_workers()
    per_worker = n // n_workers

    @functools.partial(
        pl.kernel, mesh=mesh, out_type=jax.ShapeDtypeStruct((cap, d), x.dtype),
        scratch_types=[pltpu.VMEM((SC_WINDOW,), jnp.int32)] * TOP_K
        + [pltpu.VMEM((SC_WINDOW, d), x.dtype), pltpu.SemaphoreType.DMA],
        name="moe_dispatch")
    def scatter(x_hbm, *rest):
        idx_hbm, (xs_hbm, *idx_vmem, rows, sem) = rest[:TOP_K], rest[TOP_K:]
        wid = lax.axis_index("subcore") * n_cores + lax.axis_index("core")

        @pl.loop(0, per_worker // SC_WINDOW)
        def _(c):
            base = wid * per_worker + c * SC_WINDOW
            for k in range(TOP_K):
                pltpu.sync_copy(idx_hbm[k].at[pl.ds(base, SC_WINDOW)], idx_vmem[k])
            pltpu.sync_copy(x_hbm.at[pl.ds(base, SC_WINDOW)], rows)
            for k in range(TOP_K):
                pltpu.async_copy(rows, xs_hbm.at[idx_vmem[k]], sem).wait()

    return scatter(x, *[dest[k] for k in range(TOP_K)])


def _gather_rows(table, idx):
    m = idx.shape[0]
    d = table.shape[1]
    mesh, n_cores, n_workers = _sc_workers()
    per_worker = m // n_workers

    @functools.partial(
        pl.kernel, mesh=mesh, out_type=jax.ShapeDtypeStruct((m, d), table.dtype),
        scratch_types=[pltpu.VMEM((SC_WINDOW,), jnp.int32), pltpu.VMEM((SC_WINDOW, d), table.dtype),
                       pltpu.SemaphoreType.DMA],
        name="moe_gather")
    def gather(t_hbm, i_hbm, o_hbm, idx_vmem, rows, sem):
        wid = lax.axis_index("subcore") * n_cores + lax.axis_index("core")

        @pl.loop(0, per_worker // SC_WINDOW)
        def _(c):
            base = wid * per_worker + c * SC_WINDOW
            pltpu.sync_copy(i_hbm.at[pl.ds(base, SC_WINDOW)], idx_vmem)
            pltpu.async_copy(t_hbm.at[idx_vmem], rows, sem).wait()
            pltpu.sync_copy(rows, o_hbm.at[pl.ds(base, SC_WINDOW)])

    return gather(table, idx)


def _expert_kernel(be_ref, nv_ref, xs_ref, g_ref, wg_ref, wu_ref, wd_ref, ys_ref, xb_ref):
    del be_ref
    b = pl.program_id(0)
    j = pl.program_id(1)

    @pl.when(j == 0)
    def _():
        ys_ref[...] = jnp.zeros_like(ys_ref)

    @pl.when(b < nv_ref[0])
    def _():
        @pl.when(j == 0)
        def _():
            xb_ref[...] = _rms(xs_ref[...], g_ref[...]).astype(BF16)

        x = xb_ref[...]
        acts = []
        for c in range(wg_ref.shape[1] // MXU_DIM):
            cols = slice(c * MXU_DIM, (c + 1) * MXU_DIM)
            gate = jnp.dot(x, wg_ref[:, cols].astype(BF16), preferred_element_type=F32)
            up = jnp.dot(x, wu_ref[:, cols].astype(BF16), preferred_element_type=F32)
            acts.append((_silu(gate) * up).astype(BF16))
        a = jnp.concatenate(acts, axis=1)
        ys_ref[...] += jnp.dot(a, wd_ref[...].astype(BF16), preferred_element_type=F32)


def _experts(xs, gain, block_e, n_valid, w_gate, w_up, w_down, layer, tm, tf):
    cap, d = xs.shape
    ff = w_gate.shape[3]
    nj = ff // tf

    def blk(b, nv):
        return jnp.minimum(b, nv[0] - 1)

    def ffi(b, j, nv):
        return jnp.where(b < nv[0], j, nj - 1)

    return pl.pallas_call(
        _expert_kernel,
        grid_spec=pltpu.PrefetchScalarGridSpec(
            num_scalar_prefetch=2,
            grid=(cap // tm, nj),
            in_specs=[
                pl.BlockSpec((tm, d), lambda b, j, be, nv: (blk(b, nv), 0)),
                pl.BlockSpec((1, d), lambda b, j, be, nv: (0, 0)),
                pl.BlockSpec((None, None, d, tf), lambda b, j, be, nv: (layer, be[blk(b, nv)], 0, ffi(b, j, nv))),
                pl.BlockSpec((None, None, d, tf), lambda b, j, be, nv: (layer, be[blk(b, nv)], 0, ffi(b, j, nv))),
                pl.BlockSpec((None, None, tf, d), lambda b, j, be, nv: (layer, be[blk(b, nv)], ffi(b, j, nv), 0)),
            ],
            out_specs=pl.BlockSpec((tm, d), lambda b, j, be, nv: (b, 0)),
            scratch_shapes=[pltpu.VMEM((tm, d), BF16)],
        ),
        out_shape=jax.ShapeDtypeStruct((cap, d), F32),
        compiler_params=_params("arbitrary", "arbitrary"),
        name="moe_experts",
    )(block_e, n_valid, xs, gain, w_gate, w_up, w_down)


def _combine_kernel(x_ref, rg_ref, gf_ref, y0_ref, y1_ref, o_ref, *, final_norm):
    rg = rg_ref[...]
    out = x_ref[...] + (rg[:, 0:1] * y0_ref[...] + rg[:, 1:2] * y1_ref[...])
    if final_norm:
        out = _rms(out, gf_ref[...])
    o_ref[...] = out


def _combine(x, route_g, yg, final_gain, final_norm):
    n, d = x.shape
    tm = 512
    return pl.pallas_call(
        functools.partial(_combine_kernel, final_norm=final_norm),
        grid=(n // tm,),
        in_specs=[
            pl.BlockSpec((tm, d), lambda i: (i, 0)),
            pl.BlockSpec((tm, ROUTE_COLS), lambda i: (i, 0)),
            pl.BlockSpec((1, d), lambda i: (0, 0)),
            pl.BlockSpec((None, tm, d), lambda i: (0, i, 0)),
            pl.BlockSpec((None, tm, d), lambda i: (1, i, 0)),
        ],
        out_specs=pl.BlockSpec((tm, d), lambda i: (i, 0)),
        out_shape=jax.ShapeDtypeStruct((n, d), F32),
        compiler_params=_params("parallel"),
        name="moe_combine",
    )(x, route_g, final_gain, yg, yg)


def _moe_layer(x, gain, w_router, w_gate, w_up, w_down, layer, final_gain, final_norm):
    n, d = x.shape
    tm = 1024 if n * TOP_K >= 8 * 1024 else 256
    tf = 2 * MXU_DIM
    route_i, route_g, counts = _router(x, gain, w_router)

    counts = counts[0].astype(jnp.int32)
    padded = ((counts + tm - 1) // tm) * tm
    ends = jnp.cumsum(padded)
    starts = ends - padded
    dest = (starts[route_i[:, :TOP_K]] + route_i[:, TOP_K:2 * TOP_K]).T
    n_blocks = (n * TOP_K) // tm + N_EXPERTS
    block_e = jnp.minimum(
        jnp.searchsorted(ends, jnp.arange(n_blocks, dtype=jnp.int32) * tm, side="right"),
        N_EXPERTS - 1).astype(jnp.int32)
    n_valid = (ends[-1:] // tm).astype(jnp.int32)

    xs = _dispatch(x, dest, n_blocks * tm)
    ys = _experts(xs, gain, block_e, n_valid, w_gate, w_up, w_down, layer, tm, tf)
    yg = _gather_rows(ys, dest.reshape(TOP_K * n)).reshape(TOP_K, n, d)
    return _combine(x, route_g, yg, final_gain, final_norm)


def _rope_tables(seq):
    pos = jnp.arange(seq, dtype=F32)
    inv_freq = ROPE_THETA ** (-jnp.arange(0, HEAD_DIM, 2, dtype=F32) / HEAD_DIM)
    ang = pos[:, None] * inv_freq[None, :]
    reps = LANES // (HEAD_DIM // 2)
    cos = jnp.tile(jnp.cos(ang), (1, reps))
    sin = jnp.tile(jnp.sin(ang), (1, reps))
    first_half = (jnp.arange(LANES) % HEAD_DIM) < HEAD_DIM // 2
    tabs = (cos, jnp.where(first_half, -sin, 0.0), jnp.where(first_half, 0.0, sin))
    q_scale = HEAD_DIM ** -0.5 * 1.4426950408889634
    return tuple(jnp.stack([t * q_scale, t]) for t in tabs)


def kernel(x, mix_norm, ffn_norm, attn_w_qkv, attn_w_o, pool_w, pool_scale, dense_w_gate, dense_w_up,
           dense_w_down, moe_router, moe_w_gate, moe_w_up, moe_w_down, final_norm):
    batch, seq, d = x.shape
    depth = mix_norm.shape[0]
    assert depth % 2 == 0 and d == N_HEADS * HEAD_DIM
    cos, sina, sinb = _rope_tables(seq)
    xf = x.reshape(batch * seq, d)
    final_gain = final_norm.reshape(1, d)
    for i in range(depth):
        j = i // 2
        mix_gain = mix_norm[i].reshape(1, d)
        ffn_gain = ffn_norm[i].reshape(1, d)
        if i % 2 == 0:
            qkv = _qkv_rope(xf, mix_gain, attn_w_qkv[j].astype(BF16), cos, sina, sinb, batch, seq)
            o = _attention(qkv, batch, seq)
            xf = _proj_dense_ffn(xf, o, attn_w_o[j].astype(BF16), ffn_gain, dense_w_gate[j].astype(BF16),
                                 dense_w_up[j].astype(BF16), dense_w_down[j].astype(BF16))
        else:
            xf = _pool_mixer(xf, mix_gain, pool_w[j].astype(BF16), pool_scale[j].reshape(1, d), batch, seq)
            xf = _moe_layer(xf, ffn_gain, moe_router[j], moe_w_gate, moe_w_up, moe_w_down, j,
                            final_gain, i == depth - 1)
    return xf.reshape(batch, seq, d)
```

```python
import functools

import jax
import jax.numpy as jnp
from jax import lax
from jax.experimental import pallas as pl
from jax.experimental.pallas import tpu as pltpu
from jax.experimental.pallas import tpu_sc as plsc

WINDOWS = (128, 512, 2048)
DILATIONS = (1, 4, 16)
N_GROUPS = len(WINDOWS)
N_HEADS = 16
HEAD_DIM = 64
ROPE_THETA = 10000.0
POOL_SIZES = (2, 4, 8, 16)
N_EXPERTS = 8
TOP_K = 2
RMS_EPS = 1e-6
MASK_VALUE = -1e30
QKV_SUB_ROWS = 512
ATTN_UNROLL = 8

LANES = 128
MXU_DIM = 256
VMEM_LIMIT_BYTES = 56 * 1024 * 1024

F32 = jnp.float32
BF16 = jnp.bfloat16


def _params(*sem):
    return pltpu.CompilerParams(dimension_semantics=sem, vmem_limit_bytes=VMEM_LIMIT_BYTES)


def _rms(x, g):
    ms = jnp.mean(x * x, axis=-1, keepdims=True)
    return x * lax.rsqrt(ms + RMS_EPS) * g


def _silu(g):
    return g / (1.0 + jnp.exp(-g))


def _qkv_kernel(x_ref, g_ref, w0_ref, w1_ref, w2_ref, *rest, rope):
    if rope:
        cos_ref, sin_ref, *rest = rest
    o_refs, acc_ref = rest[:N_GROUPS], rest[N_GROUPS]
    tm = x_ref.shape[0]
    n_chunks = acc_ref.shape[1]
    lane = lax.broadcasted_iota(jnp.int32, (1, LANES), 1)
    first_half = (lane % HEAD_DIM) < HEAD_DIM // 2
    for sub in range(tm // QKV_SUB_ROWS):
        row0 = sub * QKV_SUB_ROWS
        h = _rms(x_ref[row0:row0 + QKV_SUB_ROWS, :], g_ref[...]).astype(BF16)
        for g, (w_ref, o_ref, dil) in enumerate(zip((w0_ref, w1_ref, w2_ref), o_refs, DILATIONS)):
            acc = jnp.dot(h, w_ref[...], preferred_element_type=F32)
            for c in range(n_chunks):
                acc_ref[g, c, row0:row0 + QKV_SUB_ROWS, :] = acc[:, c * LANES:(c + 1) * LANES]
            rows = QKV_SUB_ROWS // dil

            def strided(ref, r, dil=dil, rows=rows, row0=row0):
                if dil == 1:
                    return ref[row0:row0 + rows, :]
                return ref[pl.ds(row0 + r, rows, stride=dil), :]

            for r in range(dil):
                if rope:
                    cos = strided(cos_ref, r)
                    sin = strided(sin_ref, r)
                for c in range(n_chunks):
                    t = strided(acc_ref.at[g, c], r)
                    if rope:
                        partner = jnp.where(first_half, pltpu.roll(t, LANES - HEAD_DIM // 2, 1),
                                            pltpu.roll(t, HEAD_DIM // 2, 1))
                        t = t * cos + partner * sin
                    o_ref[r, sub * rows:(sub + 1) * rows, c * LANES:(c + 1) * LANES] = t.astype(BF16)


def _qkv_project(x, gain, w_qkv, tables, kinds, batch, seq):
    n, d = x.shape
    tm = 2 * QKV_SUB_ROWS
    tiles_per_seq = seq // tm
    rope = tables is not None
    tab = pl.BlockSpec((None, tm, LANES), lambda k, i: (k, i % tiles_per_seq, 0))

    def w_spec(g):
        return pl.BlockSpec((d, d), lambda k, i: (0, g * 3 + kinds[0] + k))

    def o_spec(dil):
        return pl.BlockSpec((None, None, dil, tm // dil, d),
                            lambda k, i: (k, i // tiles_per_seq, 0, i % tiles_per_seq, 0))

    outs = pl.pallas_call(
        functools.partial(_qkv_kernel, rope=rope),
        grid=(len(kinds), n // tm),
        in_specs=[
            pl.BlockSpec((tm, d), lambda k, i: (i, 0)),
            pl.BlockSpec((1, d), lambda k, i: (0, 0)),
            w_spec(0), w_spec(1), w_spec(2),
        ] + ([tab, tab] if rope else []),
        out_specs=[o_spec(dil) for dil in DILATIONS],
        out_shape=[jax.ShapeDtypeStruct((len(kinds), batch, dil, seq // dil, d), BF16) for dil in DILATIONS],
        scratch_shapes=[pltpu.VMEM((N_GROUPS, d // LANES, tm, LANES), F32)],
        compiler_params=_params("parallel", "parallel"),
        name="qk_rope" if rope else "v_proj",
    )(x, gain, w_qkv, w_qkv, w_qkv, *(tables if rope else ()))
    return [o.reshape(len(kinds), batch, seq, d) for o in outs]


def _attn_kernel(q0, k0, v0, q1, k1, v1, q2, k2, v2, o_ref, num_ref, max_ref, den_ref, kt_ref):
    seq = o_ref.shape[0]
    w = WINDOWS[0] // DILATIONS[0]
    lane = lax.broadcasted_iota(jnp.int32, (1, LANES), 1)
    first_head = lane < HEAD_DIM
    row2 = lax.broadcasted_iota(jnp.int32, (2 * w, 2 * w), 0) % w
    col2 = lax.broadcasted_iota(jnp.int32, (2 * w, 2 * w), 1)
    dist = row2 + w - col2
    band = (dist >= 0) & (dist <= w)
    in_current = col2 >= w
    causal = (lax.broadcasted_iota(jnp.int32, (2 * w, w), 0) % w
              >= lax.broadcasted_iota(jnp.int32, (2 * w, w), 1))

    def tile(q_ref, v_ref, qoff, n, has_prev):
        q = q_ref[pl.ds(qoff, w), :]
        zero = jnp.zeros_like(q)
        qq = jnp.concatenate([jnp.where(first_head, q, zero), jnp.where(first_head, zero, q)], axis=0)
        if has_prev:
            poff = pl.multiple_of(jnp.maximum(qoff - w, 0), w)
            keys_t = jnp.concatenate([kt_ref[:, pl.ds(poff, w)], kt_ref[:, pl.ds(qoff, w)]], axis=1)
            vals = jnp.concatenate([v_ref[pl.ds(poff, w), :], v_ref[pl.ds(qoff, w), :]], axis=0)
            valid = band & (in_current | (n > 0))
        else:
            keys_t = kt_ref[:, pl.ds(qoff, w)]
            vals = v_ref[pl.ds(qoff, w), :]
            valid = causal
        s = jnp.dot(qq, keys_t, preferred_element_type=F32)
        s = jnp.where(valid, s, MASK_VALUE)
        mx = jnp.max(s, axis=1, keepdims=True)
        p = jnp.exp2(s - mx)
        den = jnp.sum(p, axis=1, keepdims=True)
        pv = jnp.dot(p.astype(BF16), vals, preferred_element_type=F32)
        return tuple(jnp.where(first_head, a[:w], a[w:]) for a in (pv, mx, den))

    order = sorted(range(N_GROUPS), key=lambda g: -DILATIONS[g])
    for g in order:
        q_ref, k_ref, v_ref = ((q0, k0, v0), (q1, k1, v1), (q2, k2, v2))[g]
        dil = DILATIONS[g]
        sub_len = seq // dil
        nb = sub_len // w

        kt_ref[...] = k_ref[...].T

        def body(t, carry, q_ref=q_ref, v_ref=v_ref, g=g, dil=dil, sub_len=sub_len, nb=nb):
            r = t // nb
            n = t % nb
            qoff = pl.multiple_of(r * sub_len + n * w, w)
            pv, mx, den = tile(q_ref, v_ref, qoff, n, nb > 1)
            if dil == 1:
                rows = pl.ds(qoff, w)
            else:
                rows = pl.ds(r + dil * n * w, w, stride=dil)
            if g == order[0]:
                num_ref[rows, :] = pv
                max_ref[rows, :] = mx
                den_ref[rows, :] = den
                return carry
            old_max = max_ref[rows, :]
            new_max = jnp.maximum(old_max, mx)
            a = jnp.exp2(old_max - new_max)
            b = jnp.exp2(mx - new_max)
            num = a * num_ref[rows, :] + b * pv
            den = a * den_ref[rows, :] + b * den
            if g == order[-1]:
                o_ref[rows, :] = (num / den).astype(BF16)
            else:
                num_ref[rows, :] = num
                den_ref[rows, :] = den
                max_ref[rows, :] = new_max
            return carry

        lax.fori_loop(0, dil * nb, body, 0, unroll=ATTN_UNROLL)


def _attention(qk_groups, v_groups, batch, seq):
    d_model = N_HEADS * HEAD_DIM
    in_specs, args = [], []
    for qk, v in zip(qk_groups, v_groups):
        for arr, kind in ((qk, 0), (qk, 1), (v, 0)):
            in_specs.append(pl.BlockSpec((None, None, seq, LANES), lambda b, hp, kind=kind: (kind, b, 0, hp)))
            args.append(arr)
    return pl.pallas_call(
        _attn_kernel,
        grid=(batch, d_model // LANES),
        in_specs=in_specs,
        out_specs=pl.BlockSpec((None, seq, LANES), lambda b, hp: (b, 0, hp)),
        out_shape=jax.ShapeDtypeStruct((batch, seq, d_model), BF16),
        scratch_shapes=[pltpu.VMEM((seq, LANES), F32)] * 3 + [pltpu.VMEM((LANES, seq), BF16)],
        compiler_params=_params("parallel", "parallel"),
        name="attention",
    )(*args).reshape(batch * seq, d_model)


def _proj_dense_kernel(x_ref, o_ref, wo_ref, g_ref, wg_ref, wu_ref, wd_ref, y_ref):
    x = x_ref[...] + jnp.dot(o_ref[...], wo_ref[...], preferred_element_type=F32)
    h = _rms(x, g_ref[...]).astype(BF16)
    gate = jnp.dot(h, wg_ref[...], preferred_element_type=F32)
    up = jnp.dot(h, wu_ref[...], preferred_element_type=F32)
    a = (_silu(gate) * up).astype(BF16)
    y_ref[...] = x + jnp.dot(a, wd_ref[...], preferred_element_type=F32)


def _proj_dense_ffn(x, o, w_o, gain, w_gate, w_up, w_down):
    n, d = x.shape
    ff = w_gate.shape[1]
    tm = 256
    row = pl.BlockSpec((tm, d), lambda i: (i, 0))
    const = lambda shape: pl.BlockSpec(shape, lambda i: (0, 0), pipeline_mode=pl.Buffered(1))
    return pl.pallas_call(
        _proj_dense_kernel,
        grid=(n // tm,),
        in_specs=[row, row, const((d, d)), const((1, d)), const((d, ff)), const((d, ff)), const((ff, d))],
        out_specs=row,
        out_shape=jax.ShapeDtypeStruct((n, d), F32),
        compiler_params=_params("parallel"),
        name="proj_dense_ffn",
    )(x, o, w_o, gain, w_gate, w_up, w_down)


POOL_HALO = max(POOL_SIZES)


def _pool_kernel(x_ref, halo_ref, g_ref, w_ref, sc_ref, y_ref):
    i = pl.program_id(1)
    ts = x_ref.shape[0]
    g = g_ref[...]
    x = x_ref[...]
    h = _rms(x, g)
    hh = jnp.where(i > 0, _rms(halo_ref[...], g), 0.0)
    ext = jnp.concatenate([hh, h], axis=0)
    pos = i * ts + lax.broadcasted_iota(jnp.int32, (ts, 1), 0)
    pc = w_ref.shape[1]
    for grp, size in enumerate(POOL_SIZES):
        sl = slice(grp * pc, (grp + 1) * pc)
        s = ext[:, sl]
        step = 1
        while step < size:
            s = s + pltpu.roll(s, step, 0)
            step *= 2
        cnt = jnp.minimum(pos + 1, size).astype(F32)
        y = s[POOL_HALO:, :] / cnt - h[:, sl]
        z = jnp.dot(y.astype(BF16), w_ref[grp], preferred_element_type=F32)
        y_ref[:, sl] = x[:, sl] + z * sc_ref[:, sl]


def _pool_mixer(x, gain, pool_w, scale, batch, seq):
    n, d = x.shape
    ts = 512
    x3 = x.reshape(batch, seq, d)
    hb = ts // POOL_HALO
    out = pl.pallas_call(
        _pool_kernel,
        grid=(batch, seq // ts),
        in_specs=[
            pl.BlockSpec((None, ts, d), lambda b, i: (b, i, 0)),
            pl.BlockSpec((None, POOL_HALO, d), lambda b, i: (b, jnp.maximum(i * hb - 1, 0), 0)),
            pl.BlockSpec((1, d), lambda b, i: (0, 0)),
            pl.BlockSpec(pool_w.shape, lambda b, i: (0, 0, 0)),
            pl.BlockSpec((1, d), lambda b, i: (0, 0)),
        ],
        out_specs=pl.BlockSpec((None, ts, d), lambda b, i: (b, i, 0)),
        out_shape=jax.ShapeDtypeStruct((batch, seq, d), F32),
        compiler_params=_params("parallel", "arbitrary"),
        name="pool_mixer",
    )(x3, x3, gain, pool_w, scale)
    return out.reshape(n, d)


ROUTE_COLS = 8


def _router_kernel(x_ref, g_ref, wr_ref, tril_ref, ri_ref, rg_ref, cnt_ref, run_ref):
    i = pl.program_id(0)

    @pl.when(i == 0)
    def _():
        run_ref[...] = jnp.zeros_like(run_ref)

    h = _rms(x_ref[...], g_ref[...])
    wr = wr_ref[...]
    ne = wr.shape[1]
    h_hi = h.astype(BF16)
    h_lo = (h - h_hi.astype(F32)).astype(BF16)
    w_hi = wr.astype(BF16)
    w_lo = (wr - w_hi.astype(F32)).astype(BF16)
    hi_part = jnp.dot(h_hi, jnp.concatenate([w_hi, w_lo], axis=1), preferred_element_type=F32)
    logits = hi_part[:, :ne] + hi_part[:, ne:] + jnp.dot(h_lo, w_hi, preferred_element_type=F32)
    tm = logits.shape[0]
    lane = lax.broadcasted_iota(jnp.int32, (tm, ne), 1)
    v1 = jnp.max(logits, axis=1, keepdims=True)
    i1 = jnp.min(jnp.where(logits == v1, lane, ne), axis=1, keepdims=True)
    rest = jnp.where(lane == i1, -jnp.inf, logits)
    v2 = jnp.max(rest, axis=1, keepdims=True)
    i2 = jnp.min(jnp.where(rest == v2, lane, ne), axis=1, keepdims=True)
    e = jnp.exp(v2 - v1)
    g1 = 1.0 / (1.0 + e)
    g2 = e / (1.0 + e)
    oh1 = (lane == i1).astype(F32)
    oh2 = (lane == i2).astype(F32)
    both = oh1 + oh2
    before = jnp.dot(tril_ref[...], both.astype(BF16), preferred_element_type=F32) + run_ref[...]
    r1 = jnp.sum(before * oh1, axis=1, keepdims=True).astype(jnp.int32)
    r2 = jnp.sum(before * oh2, axis=1, keepdims=True).astype(jnp.int32)
    run_ref[...] += jnp.sum(both, axis=0, keepdims=True)
    ri_ref[...] = jnp.where(lane == 0, i1, jnp.where(lane == 1, i2, jnp.where(lane == 2, r1, r2)))
    rg_ref[...] = jnp.where(lane == 0, g1, g2)
    cnt_ref[...] = run_ref[...]


def _router(x, gain, w_router):
    n, d = x.shape
    ne = w_router.shape[1]
    tm = 512
    tril = (jnp.arange(tm)[:, None] > jnp.arange(tm)[None, :]).astype(BF16)
    return pl.pallas_call(
        _router_kernel,
        grid=(n // tm,),
        in_specs=[
            pl.BlockSpec((tm, d), lambda i: (i, 0)),
            pl.BlockSpec((1, d), lambda i: (0, 0)),
            pl.BlockSpec((d, ne), lambda i: (0, 0)),
            pl.BlockSpec((tm, tm), lambda i: (0, 0)),
        ],
        out_specs=[
            pl.BlockSpec((tm, ROUTE_COLS), lambda i: (i, 0)),
            pl.BlockSpec((tm, ROUTE_COLS), lambda i: (i, 0)),
            pl.BlockSpec((1, ne), lambda i: (0, 0)),
        ],
        out_shape=[
            jax.ShapeDtypeStruct((n, ROUTE_COLS), jnp.int32),
            jax.ShapeDtypeStruct((n, ROUTE_COLS), F32),
            jax.ShapeDtypeStruct((1, ne), F32),
        ],
        scratch_shapes=[pltpu.VMEM((1, ne), F32)],
        compiler_params=_params("arbitrary"),
        name="moe_router",
    )(x, gain, w_router, tril)


SC_WINDOW = 64


def _sc_workers():
    info = plsc.get_sparse_core_info()
    mesh = plsc.VectorSubcoreMesh(core_axis_name="core", subcore_axis_name="subcore")
    return mesh, info.num_cores, info.num_cores * info.num_subcores


def _dispatch(x, dest, cap):
    n, d = x.shape
    mesh, n_cores, n_workers = _sc_workers()
    per_worker = n // n_workers

    @functools.partial(
        pl.kernel, mesh=mesh, out_type=jax.ShapeDtypeStruct((cap, d), x.dtype),
        scratch_types=[pltpu.VMEM((SC_WINDOW,), jnp.int32)] * TOP_K
        + [pltpu.VMEM((SC_WINDOW, d), x.dtype), pltpu.SemaphoreType.DMA],
        name="moe_dispatch")
    def scatter(x_hbm, *rest):
        idx_hbm, (xs_hbm, *idx_vmem, rows, sem) = rest[:TOP_K], rest[TOP_K:]
        wid = lax.axis_index("subcore") * n_cores + lax.axis_index("core")

        @pl.loop(0, per_worker // SC_WINDOW)
        def _(c):
            base = wid * per_worker + c * SC_WINDOW
            for k in range(TOP_K):
                pltpu.sync_copy(idx_hbm[k].at[pl.ds(base, SC_WINDOW)], idx_vmem[k])
            pltpu.sync_copy(x_hbm.at[pl.ds(base, SC_WINDOW)], rows)
            for k in range(TOP_K):
                pltpu.async_copy(rows, xs_hbm.at[idx_vmem[k]], sem).wait()

    return scatter(x, *[dest[k] for k in range(TOP_K)])


def _gather_rows(table, idx):
    m = idx.shape[0]
    d = table.shape[1]
    mesh, n_cores, n_workers = _sc_workers()
    per_worker = m // n_workers

    @functools.partial(
        pl.kernel, mesh=mesh, out_type=jax.ShapeDtypeStruct((m, d), table.dtype),
        scratch_types=[pltpu.VMEM((SC_WINDOW,), jnp.int32), pltpu.VMEM((SC_WINDOW, d), table.dtype),
                       pltpu.SemaphoreType.DMA],
        name="moe_gather")
    def gather(t_hbm, i_hbm, o_hbm, idx_vmem, rows, sem):
        wid = lax.axis_index("subcore") * n_cores + lax.axis_index("core")

        @pl.loop(0, per_worker // SC_WINDOW)
        def _(c):
            base = wid * per_worker + c * SC_WINDOW
            pltpu.sync_copy(i_hbm.at[pl.ds(base, SC_WINDOW)], idx_vmem)
            pltpu.async_copy(t_hbm.at[idx_vmem], rows, sem).wait()
            pltpu.sync_copy(rows, o_hbm.at[pl.ds(base, SC_WINDOW)])

    return gather(table, idx)


def _expert_kernel(be_ref, nv_ref, xs_ref, g_ref, wg_ref, wu_ref, wd_ref, ys_ref, xb_ref):
    del be_ref
    b = pl.program_id(0)
    j = pl.program_id(1)

    @pl.when(j == 0)
    def _():
        ys_ref[...] = jnp.zeros_like(ys_ref)

    @pl.when(b < nv_ref[0])
    def _():
        @pl.when(j == 0)
        def _():
            xb_ref[...] = _rms(xs_ref[...], g_ref[...]).astype(BF16)

        x = xb_ref[...]
        acts = []
        for c in range(wg_ref.shape[1] // MXU_DIM):
            cols = slice(c * MXU_DIM, (c + 1) * MXU_DIM)
            gate = jnp.dot(x, wg_ref[:, cols].astype(BF16), preferred_element_type=F32)
            up = jnp.dot(x, wu_ref[:, cols].astype(BF16), preferred_element_type=F32)
            acts.append((_silu(gate) * up).astype(BF16))
        a = jnp.concatenate(acts, axis=1)
        ys_ref[...] += jnp.dot(a, wd_ref[...].astype(BF16), preferred_element_type=F32)


def _experts(xs, gain, block_e, n_valid, w_gate, w_up, w_down, layer, tm, tf):
    cap, d = xs.shape
    ff = w_gate.shape[3]
    nj = ff // tf

    def blk(b, nv):
        return jnp.minimum(b, nv[0] - 1)

    def ffi(b, j, nv):
        return jnp.where(b < nv[0], j, nj - 1)

    return pl.pallas_call(
        _expert_kernel,
        grid_spec=pltpu.PrefetchScalarGridSpec(
            num_scalar_prefetch=2,
            grid=(cap // tm, nj),
            in_specs=[
                pl.BlockSpec((tm, d), lambda b, j, be, nv: (blk(b, nv), 0)),
                pl.BlockSpec((1, d), lambda b, j, be, nv: (0, 0)),
                pl.BlockSpec((None, None, d, tf), lambda b, j, be, nv: (layer, be[blk(b, nv)], 0, ffi(b, j, nv))),
                pl.BlockSpec((None, None, d, tf), lambda b, j, be, nv: (layer, be[blk(b, nv)], 0, ffi(b, j, nv))),
                pl.BlockSpec((None, None, tf, d), lambda b, j, be, nv: (layer, be[blk(b, nv)], ffi(b, j, nv), 0)),
            ],
            out_specs=pl.BlockSpec((tm, d), lambda b, j, be, nv: (b, 0)),
            scratch_shapes=[pltpu.VMEM((tm, d), BF16)],
        ),
        out_shape=jax.ShapeDtypeStruct((cap, d), F32),
        compiler_params=_params("arbitrary", "arbitrary"),
        name="moe_experts",
    )(block_e, n_valid, xs, gain, w_gate, w_up, w_down)


def _combine_kernel(x_ref, rg_ref, gf_ref, y0_ref, y1_ref, o_ref, *, final_norm):
    rg = rg_ref[...]
    out = x_ref[...] + (rg[:, 0:1] * y0_ref[...] + rg[:, 1:2] * y1_ref[...])
    if final_norm:
        out = _rms(out, gf_ref[...])
    o_ref[...] = out


def _combine(x, route_g, yg, final_gain, final_norm):
    n, d = x.shape
    tm = 512
    return pl.pallas_call(
        functools.partial(_combine_kernel, final_norm=final_norm),
        grid=(n // tm,),
        in_specs=[
            pl.BlockSpec((tm, d), lambda i: (i, 0)),
            pl.BlockSpec((tm, ROUTE_COLS), lambda i: (i, 0)),
            pl.BlockSpec((1, d), lambda i: (0, 0)),
            pl.BlockSpec((None, tm, d), lambda i: (0, i, 0)),
            pl.BlockSpec((None, tm, d), lambda i: (1, i, 0)),
        ],
        out_specs=pl.BlockSpec((tm, d), lambda i: (i, 0)),
        out_shape=jax.ShapeDtypeStruct((n, d), F32),
        compiler_params=_params("parallel"),
        name="moe_combine",
    )(x, route_g, final_gain, yg, yg)


def _moe_layer(x, gain, w_router, w_gate, w_up, w_down, layer, final_gain, final_norm):
    n, d = x.shape
    tm = 1024 if n * TOP_K >= 8 * 1024 else 256
    tf = 2 * MXU_DIM
    route_i, route_g, counts = _router(x, gain, w_router)

    counts = counts[0].astype(jnp.int32)
    padded = ((counts + tm - 1) // tm) * tm
    ends = jnp.cumsum(padded)
    starts = ends - padded
    dest = (starts[route_i[:, :TOP_K]] + route_i[:, TOP_K:2 * TOP_K]).T
    n_blocks = (n * TOP_K) // tm + N_EXPERTS
    block_e = jnp.minimum(
        jnp.searchsorted(ends, jnp.arange(n_blocks, dtype=jnp.int32) * tm, side="right"),
        N_EXPERTS - 1).astype(jnp.int32)
    n_valid = (ends[-1:] // tm).astype(jnp.int32)

    xs = _dispatch(x, dest, n_blocks * tm)
    ys = _experts(xs, gain, block_e, n_valid, w_gate, w_up, w_down, layer, tm, tf)
    yg = _gather_rows(ys, dest.reshape(TOP_K * n)).reshape(TOP_K, n, d)
    return _combine(x, route_g, yg, final_gain, final_norm)


def _rope_tables(seq):
    pos = jnp.arange(seq, dtype=F32)
    inv_freq = ROPE_THETA ** (-jnp.arange(0, HEAD_DIM, 2, dtype=F32) / HEAD_DIM)
    ang = pos[:, None] * inv_freq[None, :]
    reps = LANES // (HEAD_DIM // 2)
    cos = jnp.tile(jnp.cos(ang), (1, reps))
    sin = jnp.tile(jnp.sin(ang), (1, reps))
    first_half = (jnp.arange(LANES) % HEAD_DIM) < HEAD_DIM // 2
    tabs = (cos, jnp.where(first_half, -sin, sin))
    q_scale = HEAD_DIM ** -0.5 * 1.4426950408889634
    return tuple(jnp.stack([t * q_scale, t]) for t in tabs)


def kernel(x, mix_norm, ffn_norm, attn_w_qkv, attn_w_o, pool_w, pool_scale, dense_w_gate, dense_w_up,
           dense_w_down, moe_router, moe_w_gate, moe_w_up, moe_w_down, final_norm):
    batch, seq, d = x.shape
    depth = mix_norm.shape[0]
    assert depth % 2 == 0 and d == N_HEADS * HEAD_DIM
    rope_tables = _rope_tables(seq)
    xf = x.reshape(batch * seq, d)
    final_gain = final_norm.reshape(1, d)
    for i in range(depth):
        j = i // 2
        mix_gain = mix_norm[i].reshape(1, d)
        ffn_gain = ffn_norm[i].reshape(1, d)
        if i % 2 == 0:
            w_qkv = attn_w_qkv[j].astype(BF16)
            qk = _qkv_project(xf, mix_gain, w_qkv, rope_tables, (0, 1), batch, seq)
            v = _qkv_project(xf, mix_gain, w_qkv, None, (2,), batch, seq)
            o = _attention(qk, v, batch, seq)
            xf = _proj_dense_ffn(xf, o, attn_w_o[j].astype(BF16), ffn_gain, dense_w_gate[j].astype(BF16),
                                 dense_w_up[j].astype(BF16), dense_w_down[j].astype(BF16))
        else:
            xf = _pool_mixer(xf, mix_gain, pool_w[j].astype(BF16), pool_scale[j].reshape(1, d), batch, seq)
            xf = _moe_layer(xf, ffn_gain, moe_router[j], moe_w_gate, moe_w_up, moe_w_down, j,
                            final_gain, i == depth - 1)
    return xf.reshape(batch, seq, d)
```

```python
import functools

import jax
import jax.numpy as jnp
from jax import lax
from jax.experimental import pallas as pl
from jax.experimental.pallas import tpu as pltpu
from jax.experimental.pallas import tpu_sc as plsc

WINDOWS = (128, 512, 2048)
DILATIONS = (1, 4, 16)
N_GROUPS = len(WINDOWS)
N_HEADS = 16
HEAD_DIM = 64
ROPE_THETA = 10000.0
POOL_SIZES = (2, 4, 8, 16)
N_EXPERTS = 8
TOP_K = 2
RMS_EPS = 1e-6
MASK_VALUE = -1e30
QKV_SUB_ROWS = 512
ATTN_UNROLL = 16

LANES = 128
MXU_DIM = 256
VMEM_LIMIT_BYTES = 56 * 1024 * 1024

F32 = jnp.float32
BF16 = jnp.bfloat16


def _params(*sem):
    return pltpu.CompilerParams(dimension_semantics=sem, vmem_limit_bytes=VMEM_LIMIT_BYTES)


def _rms(x, g):
    ms = jnp.mean(x * x, axis=-1, keepdims=True)
    return x * lax.rsqrt(ms + RMS_EPS) * g


def _silu(g):
    return g / (1.0 + jnp.exp(-g))


def _qkv_kernel(x_ref, g_ref, w0_ref, w1_ref, w2_ref, *rest, rope):
    if rope:
        cos_ref, sin_ref, *rest = rest
    o_refs, acc_ref = rest[:N_GROUPS], rest[N_GROUPS]
    tm = x_ref.shape[0]
    n_chunks = acc_ref.shape[1]
    lane = lax.broadcasted_iota(jnp.int32, (1, LANES), 1)
    first_half = (lane % HEAD_DIM) < HEAD_DIM // 2
    for sub in range(tm // QKV_SUB_ROWS):
        row0 = sub * QKV_SUB_ROWS
        h = _rms(x_ref[row0:row0 + QKV_SUB_ROWS, :], g_ref[...]).astype(BF16)
        for g, (w_ref, o_ref, dil) in enumerate(zip((w0_ref, w1_ref, w2_ref), o_refs, DILATIONS)):
            acc = jnp.dot(h, w_ref[...], preferred_element_type=F32)
            for c in range(n_chunks):
                acc_ref[g, c, row0:row0 + QKV_SUB_ROWS, :] = acc[:, c * LANES:(c + 1) * LANES]
            rows = QKV_SUB_ROWS // dil

            def strided(ref, r, dil=dil, rows=rows, row0=row0):
                if dil == 1:
                    return ref[row0:row0 + rows, :]
                return ref[pl.ds(row0 + r, rows, stride=dil), :]

            for r in range(dil):
                if rope:
                    cos = strided(cos_ref, r)
                    sin = strided(sin_ref, r)
                for c in range(n_chunks):
                    t = strided(acc_ref.at[g, c], r)
                    if rope:
                        partner = jnp.where(first_half, pltpu.roll(t, LANES - HEAD_DIM // 2, 1),
                                            pltpu.roll(t, HEAD_DIM // 2, 1))
                        t = t * cos + partner * sin
                    o_ref[r, sub * rows:(sub + 1) * rows, c * LANES:(c + 1) * LANES] = t.astype(BF16)


def _qkv_project(x, gain, w_qkv, tables, kinds, batch, seq):
    n, d = x.shape
    tm = 2 * QKV_SUB_ROWS
    tiles_per_seq = seq // tm
    rope = tables is not None
    tab = pl.BlockSpec((None, tm, LANES), lambda k, i: (k, i % tiles_per_seq, 0))

    def w_spec(g):
        return pl.BlockSpec((d, d), lambda k, i: (0, g * 3 + kinds[0] + k))

    def o_spec(dil):
        return pl.BlockSpec((None, None, dil, tm // dil, d),
                            lambda k, i: (k, i // tiles_per_seq, 0, i % tiles_per_seq, 0))

    outs = pl.pallas_call(
        functools.partial(_qkv_kernel, rope=rope),
        grid=(len(kinds), n // tm),
        in_specs=[
            pl.BlockSpec((tm, d), lambda k, i: (i, 0)),
            pl.BlockSpec((1, d), lambda k, i: (0, 0)),
            w_spec(0), w_spec(1), w_spec(2),
        ] + ([tab, tab] if rope else []),
        out_specs=[o_spec(dil) for dil in DILATIONS],
        out_shape=[jax.ShapeDtypeStruct((len(kinds), batch, dil, seq // dil, d), BF16) for dil in DILATIONS],
        scratch_shapes=[pltpu.VMEM((N_GROUPS, d // LANES, tm, LANES), F32)],
        compiler_params=_params("parallel", "parallel"),
        name="qk_rope" if rope else "v_proj",
    )(x, gain, w_qkv, w_qkv, w_qkv, *(tables if rope else ()))
    return [o.reshape(len(kinds), batch, seq, d) for o in outs]


def _attn_kernel(q0, k0, v0, q1, k1, v1, q2, k2, v2, o_ref, num_ref, max_ref, den_ref, kt_ref):
    seq = o_ref.shape[0]
    w = WINDOWS[0] // DILATIONS[0]
    lane = lax.broadcasted_iota(jnp.int32, (1, LANES), 1)
    first_head = lane < HEAD_DIM
    row2 = lax.broadcasted_iota(jnp.int32, (2 * w, 2 * w), 0) % w
    col2 = lax.broadcasted_iota(jnp.int32, (2 * w, 2 * w), 1)
    dist = row2 + w - col2
    band = (dist >= 0) & (dist <= w)
    in_current = col2 >= w
    causal = (lax.broadcasted_iota(jnp.int32, (2 * w, w), 0) % w
              >= lax.broadcasted_iota(jnp.int32, (2 * w, w), 1))

    def tile(q_ref, v_ref, qoff, n, has_prev):
        q = q_ref[pl.ds(qoff, w), :]
        zero = jnp.zeros_like(q)
        qq = jnp.concatenate([jnp.where(first_head, q, zero), jnp.where(first_head, zero, q)], axis=0)
        if has_prev:
            poff = pl.multiple_of(jnp.maximum(qoff - w, 0), w)
            keys_t = jnp.concatenate([kt_ref[:, pl.ds(poff, w)], kt_ref[:, pl.ds(qoff, w)]], axis=1)
            vals = jnp.concatenate([v_ref[pl.ds(poff, w), :], v_ref[pl.ds(qoff, w), :]], axis=0)
            valid = band & (in_current | (n > 0))
        else:
            keys_t = kt_ref[:, pl.ds(qoff, w)]
            vals = v_ref[pl.ds(qoff, w), :]
            valid = causal
        s = jnp.dot(qq, keys_t, preferred_element_type=F32)
        s = jnp.where(valid, s, MASK_VALUE)
        mx = jnp.max(s, axis=1, keepdims=True)
        p = jnp.exp2(s - mx)
        den = jnp.sum(p, axis=1, keepdims=True)
        pv = jnp.dot(p.astype(BF16), vals, preferred_element_type=F32)
        return tuple(jnp.where(first_head, a[:w], a[w:]) for a in (pv, mx, den))

    order = sorted(range(N_GROUPS), key=lambda g: -DILATIONS[g])
    for g in order:
        q_ref, k_ref, v_ref = ((q0, k0, v0), (q1, k1, v1), (q2, k2, v2))[g]
        dil = DILATIONS[g]
        sub_len = seq // dil
        nb = sub_len // w

        kt_ref[...] = k_ref[...].T

        def body(t, carry, q_ref=q_ref, v_ref=v_ref, g=g, dil=dil, sub_len=sub_len, nb=nb):
            r = t // nb
            n = t % nb
            qoff = pl.multiple_of(r * sub_len + n * w, w)
            pv, mx, den = tile(q_ref, v_ref, qoff, n, nb > 1)
            if dil == 1:
                rows = pl.ds(qoff, w)
            else:
                rows = pl.ds(r + dil * n * w, w, stride=dil)
            if g == order[0]:
                num_ref[rows, :] = pv
                max_ref[rows, :] = mx
                den_ref[rows, :] = den
                return carry
            old_max = max_ref[rows, :]
            new_max = jnp.maximum(old_max, mx)
            a = jnp.exp2(old_max - new_max)
            b = jnp.exp2(mx - new_max)
            num = a * num_ref[rows, :] + b * pv
            den = a * den_ref[rows, :] + b * den
            if g == order[-1]:
                o_ref[rows, :] = (num / den).astype(BF16)
            else:
                num_ref[rows, :] = num
                den_ref[rows, :] = den
                max_ref[rows, :] = new_max
            return carry

        lax.fori_loop(0, dil * nb, body, 0, unroll=ATTN_UNROLL)


def _attention(qk_groups, v_groups, batch, seq):
    d_model = N_HEADS * HEAD_DIM
    in_specs, args = [], []
    for qk, v in zip(qk_groups, v_groups):
        for arr, kind in ((qk, 0), (qk, 1), (v, 0)):
            in_specs.append(pl.BlockSpec((None, None, seq, LANES), lambda b, hp, kind=kind: (kind, b, 0, hp)))
            args.append(arr)
    return pl.pallas_call(
        _attn_kernel,
        grid=(batch, d_model // LANES),
        in_specs=in_specs,
        out_specs=pl.BlockSpec((None, seq, LANES), lambda b, hp: (b, 0, hp)),
        out_shape=jax.ShapeDtypeStruct((batch, seq, d_model), BF16),
        scratch_shapes=[pltpu.VMEM((seq, LANES), F32)] * 3 + [pltpu.VMEM((LANES, seq), BF16)],
        compiler_params=_params("parallel", "parallel"),
        name="attention",
    )(*args).reshape(batch * seq, d_model)


def _proj_dense_kernel(x_ref, o_ref, wo_ref, g_ref, wg_ref, wu_ref, wd_ref, y_ref):
    x = x_ref[...] + jnp.dot(o_ref[...], wo_ref[...], preferred_element_type=F32)
    h = _rms(x, g_ref[...]).astype(BF16)
    gate = jnp.dot(h, wg_ref[...], preferred_element_type=F32)
    up = jnp.dot(h, wu_ref[...], preferred_element_type=F32)
    a = (_silu(gate) * up).astype(BF16)
    y_ref[...] = x + jnp.dot(a, wd_ref[...], preferred_element_type=F32)


def _proj_dense_ffn(x, o, w_o, gain, w_gate, w_up, w_down):
    n, d = x.shape
    ff = w_gate.shape[1]
    tm = 512
    row = pl.BlockSpec((tm, d), lambda i: (i, 0))
    const = lambda shape: pl.BlockSpec(shape, lambda i: (0, 0), pipeline_mode=pl.Buffered(1))
    return pl.pallas_call(
        _proj_dense_kernel,
        grid=(n // tm,),
        in_specs=[row, row, const((d, d)), const((1, d)), const((d, ff)), const((d, ff)), const((ff, d))],
        out_specs=row,
        out_shape=jax.ShapeDtypeStruct((n, d), F32),
        compiler_params=_params("parallel"),
        name="proj_dense_ffn",
    )(x, o, w_o, gain, w_gate, w_up, w_down)


POOL_HALO = max(POOL_SIZES)


def _pool_kernel(x_ref, halo_ref, g_ref, w_ref, sc_ref, y_ref):
    i = pl.program_id(1)
    ts = x_ref.shape[0]
    g = g_ref[...]
    x = x_ref[...]
    h = _rms(x, g)
    hh = jnp.where(i > 0, _rms(halo_ref[...], g), 0.0)
    ext = jnp.concatenate([hh, h], axis=0)
    pos = i * ts + lax.broadcasted_iota(jnp.int32, (ts, 1), 0)
    pc = w_ref.shape[1]
    for grp, size in enumerate(POOL_SIZES):
        sl = slice(grp * pc, (grp + 1) * pc)
        s = ext[:, sl]
        step = 1
        while step < size:
            s = s + pltpu.roll(s, step, 0)
            step *= 2
        cnt = jnp.minimum(pos + 1, size).astype(F32)
        y = s[POOL_HALO:, :] / cnt - h[:, sl]
        z = jnp.dot(y.astype(BF16), w_ref[grp], preferred_element_type=F32)
        y_ref[:, sl] = x[:, sl] + z * sc_ref[:, sl]


def _pool_mixer(x, gain, pool_w, scale, batch, seq):
    n, d = x.shape
    ts = 512
    x3 = x.reshape(batch, seq, d)
    hb = ts // POOL_HALO
    out = pl.pallas_call(
        _pool_kernel,
        grid=(batch, seq // ts),
        in_specs=[
            pl.BlockSpec((None, ts, d), lambda b, i: (b, i, 0)),
            pl.BlockSpec((None, POOL_HALO, d), lambda b, i: (b, jnp.maximum(i * hb - 1, 0), 0)),
            pl.BlockSpec((1, d), lambda b, i: (0, 0)),
            pl.BlockSpec(pool_w.shape, lambda b, i: (0, 0, 0)),
            pl.BlockSpec((1, d), lambda b, i: (0, 0)),
        ],
        out_specs=pl.BlockSpec((None, ts, d), lambda b, i: (b, i, 0)),
        out_shape=jax.ShapeDtypeStruct((batch, seq, d), F32),
        compiler_params=_params("parallel", "arbitrary"),
        name="pool_mixer",
    )(x3, x3, gain, pool_w, scale)
    return out.reshape(n, d)


ROUTE_COLS = 8


def _pack_bf16_pairs(hr):
    c = hr.shape[1] // 2
    bits = pltpu.bitcast(hr, jnp.int32)
    return (bits[:, c:] & jnp.int32(-65536)) | lax.shift_right_logical(bits[:, :c], 16)


def _unpack_bf16_pairs(packed):
    lo = pltpu.bitcast(lax.shift_left(packed, 16), F32)
    hi = pltpu.bitcast(packed & jnp.int32(-65536), F32)
    return jnp.concatenate([lo.astype(BF16), hi.astype(BF16)], axis=1)


def _router_kernel(x_ref, g_ref, wr_ref, tril_ref, ri_ref, rg_ref, cnt_ref, hp_ref, run_ref):
    i = pl.program_id(0)

    @pl.when(i == 0)
    def _():
        run_ref[...] = jnp.zeros_like(run_ref)

    h = _rms(x_ref[...], g_ref[...])
    wr = wr_ref[...]
    ne = wr.shape[1]
    h_hi = h.astype(BF16)
    hp_ref[...] = _pack_bf16_pairs(h_hi.astype(F32))
    h_lo = (h - h_hi.astype(F32)).astype(BF16)
    w_hi = wr.astype(BF16)
    w_lo = (wr - w_hi.astype(F32)).astype(BF16)
    hi_part = jnp.dot(h_hi, jnp.concatenate([w_hi, w_lo], axis=1), preferred_element_type=F32)
    logits = hi_part[:, :ne] + hi_part[:, ne:] + jnp.dot(h_lo, w_hi, preferred_element_type=F32)
    tm = logits.shape[0]
    lane = lax.broadcasted_iota(jnp.int32, (tm, ne), 1)
    v1 = jnp.max(logits, axis=1, keepdims=True)
    i1 = jnp.min(jnp.where(logits == v1, lane, ne), axis=1, keepdims=True)
    rest = jnp.where(lane == i1, -jnp.inf, logits)
    v2 = jnp.max(rest, axis=1, keepdims=True)
    i2 = jnp.min(jnp.where(rest == v2, lane, ne), axis=1, keepdims=True)
    e = jnp.exp(v2 - v1)
    g1 = 1.0 / (1.0 + e)
    g2 = e / (1.0 + e)
    oh1 = (lane == i1).astype(F32)
    oh2 = (lane == i2).astype(F32)
    both = oh1 + oh2
    before = jnp.dot(tril_ref[...], both.astype(BF16), preferred_element_type=F32) + run_ref[...]
    r1 = jnp.sum(before * oh1, axis=1, keepdims=True).astype(jnp.int32)
    r2 = jnp.sum(before * oh2, axis=1, keepdims=True).astype(jnp.int32)
    run_ref[...] += jnp.sum(both, axis=0, keepdims=True)
    ri_ref[...] = jnp.where(lane == 0, i1, jnp.where(lane == 1, i2, jnp.where(lane == 2, r1, r2)))
    rg_ref[...] = jnp.where(lane == 0, g1, g2)
    cnt_ref[...] = run_ref[...]


def _router(x, gain, w_router):
    n, d = x.shape
    ne = w_router.shape[1]
    tm = 512
    tril = (jnp.arange(tm)[:, None] > jnp.arange(tm)[None, :]).astype(BF16)
    return pl.pallas_call(
        _router_kernel,
        grid=(n // tm,),
        in_specs=[
            pl.BlockSpec((tm, d), lambda i: (i, 0)),
            pl.BlockSpec((1, d), lambda i: (0, 0)),
            pl.BlockSpec((d, ne), lambda i: (0, 0)),
            pl.BlockSpec((tm, tm), lambda i: (0, 0)),
        ],
        out_specs=[
            pl.BlockSpec((tm, ROUTE_COLS), lambda i: (i, 0)),
            pl.BlockSpec((tm, ROUTE_COLS), lambda i: (i, 0)),
            pl.BlockSpec((1, ne), lambda i: (0, 0)),
            pl.BlockSpec((tm, d // 2), lambda i: (i, 0)),
        ],
        out_shape=[
            jax.ShapeDtypeStruct((n, ROUTE_COLS), jnp.int32),
            jax.ShapeDtypeStruct((n, ROUTE_COLS), F32),
            jax.ShapeDtypeStruct((1, ne), F32),
            jax.ShapeDtypeStruct((n, d // 2), jnp.int32),
        ],
        scratch_shapes=[pltpu.VMEM((1, ne), F32)],
        compiler_params=_params("arbitrary"),
        name="moe_router",
    )(x, gain, w_router, tril)


SC_WINDOW = 64


def _sc_workers():
    info = plsc.get_sparse_core_info()
    mesh = plsc.VectorSubcoreMesh(core_axis_name="core", subcore_axis_name="subcore")
    return mesh, info.num_cores, info.num_cores * info.num_subcores


def _dispatch(x, dest, cap):
    n, d = x.shape
    mesh, n_cores, n_workers = _sc_workers()
    per_worker = n // n_workers

    @functools.partial(
        pl.kernel, mesh=mesh, out_type=jax.ShapeDtypeStruct((cap, d), x.dtype),
        scratch_types=[pltpu.VMEM((SC_WINDOW,), jnp.int32)] * TOP_K
        + [pltpu.VMEM((SC_WINDOW, d), x.dtype), pltpu.SemaphoreType.DMA],
        name="moe_dispatch")
    def scatter(x_hbm, *rest):
        idx_hbm, (xs_hbm, *idx_vmem, rows, sem) = rest[:TOP_K], rest[TOP_K:]
        wid = lax.axis_index("subcore") * n_cores + lax.axis_index("core")

        @pl.loop(0, per_worker // SC_WINDOW)
        def _(c):
            base = wid * per_worker + c * SC_WINDOW
            for k in range(TOP_K):
                pltpu.sync_copy(idx_hbm[k].at[pl.ds(base, SC_WINDOW)], idx_vmem[k])
            pltpu.sync_copy(x_hbm.at[pl.ds(base, SC_WINDOW)], rows)
            for k in range(TOP_K):
                pltpu.async_copy(rows, xs_hbm.at[idx_vmem[k]], sem).wait()

    return scatter(x, *[dest[k] for k in range(TOP_K)])


def _gather_rows(table, idx):
    m = idx.shape[0]
    d = table.shape[1]
    mesh, n_cores, n_workers = _sc_workers()
    per_worker = m // n_workers

    @functools.partial(
        pl.kernel, mesh=mesh, out_type=jax.ShapeDtypeStruct((m, d), table.dtype),
        scratch_types=[pltpu.VMEM((SC_WINDOW,), jnp.int32), pltpu.VMEM((SC_WINDOW, d), table.dtype),
                       pltpu.SemaphoreType.DMA],
        name="moe_gather")
    def gather(t_hbm, i_hbm, o_hbm, idx_vmem, rows, sem):
        wid = lax.axis_index("subcore") * n_cores + lax.axis_index("core")

        @pl.loop(0, per_worker // SC_WINDOW)
        def _(c):
            base = wid * per_worker + c * SC_WINDOW
            pltpu.sync_copy(i_hbm.at[pl.ds(base, SC_WINDOW)], idx_vmem)
            pltpu.async_copy(t_hbm.at[idx_vmem], rows, sem).wait()
            pltpu.sync_copy(rows, o_hbm.at[pl.ds(base, SC_WINDOW)])

    return gather(table, idx)


def _expert_kernel(be_ref, nv_ref, xs_ref, wg_ref, wu_ref, wd_ref, ys_ref, xb_ref):
    del be_ref
    b = pl.program_id(0)
    j = pl.program_id(1)

    @pl.when(j == 0)
    def _():
        ys_ref[...] = jnp.zeros_like(ys_ref)

    @pl.when(b < nv_ref[0])
    def _():
        @pl.when(j == 0)
        def _():
            xb_ref[...] = _unpack_bf16_pairs(xs_ref[...])

        x = xb_ref[...]
        acts = []
        for c in range(wg_ref.shape[1] // MXU_DIM):
            cols = slice(c * MXU_DIM, (c + 1) * MXU_DIM)
            gate = jnp.dot(x, wg_ref[:, cols].astype(BF16), preferred_element_type=F32)
            up = jnp.dot(x, wu_ref[:, cols].astype(BF16), preferred_element_type=F32)
            acts.append((_silu(gate) * up).astype(BF16))
        a = jnp.concatenate(acts, axis=1)
        ys_ref[...] += jnp.dot(a, wd_ref[...].astype(BF16), preferred_element_type=F32)


def _experts(xs, block_e, n_valid, w_gate, w_up, w_down, layer, tm, tf):
    cap = xs.shape[0]
    d = 2 * xs.shape[1]
    ff = w_gate.shape[3]
    nj = ff // tf

    def blk(b, nv):
        return jnp.minimum(b, nv[0] - 1)

    def ffi(b, j, nv):
        return jnp.where(b < nv[0], j, nj - 1)

    return pl.pallas_call(
        _expert_kernel,
        grid_spec=pltpu.PrefetchScalarGridSpec(
            num_scalar_prefetch=2,
            grid=(cap // tm, nj),
            in_specs=[
                pl.BlockSpec((tm, d // 2), lambda b, j, be, nv: (blk(b, nv), 0)),
                pl.BlockSpec((None, None, d, tf), lambda b, j, be, nv: (layer, be[blk(b, nv)], 0, ffi(b, j, nv))),
                pl.BlockSpec((None, None, d, tf), lambda b, j, be, nv: (layer, be[blk(b, nv)], 0, ffi(b, j, nv))),
                pl.BlockSpec((None, None, tf, d), lambda b, j, be, nv: (layer, be[blk(b, nv)], ffi(b, j, nv), 0)),
            ],
            out_specs=pl.BlockSpec((tm, d), lambda b, j, be, nv: (b, 0)),
            scratch_shapes=[pltpu.VMEM((tm, d), BF16)],
        ),
        out_shape=jax.ShapeDtypeStruct((cap, d), F32),
        compiler_params=_params("arbitrary", "arbitrary"),
        name="moe_experts",
    )(block_e, n_valid, xs, w_gate, w_up, w_down)


def _combine_kernel(x_ref, rg_ref, gf_ref, y0_ref, y1_ref, o_ref, *, final_norm):
    rg = rg_ref[...]
    out = x_ref[...] + (rg[:, 0:1] * y0_ref[...] + rg[:, 1:2] * y1_ref[...])
    if final_norm:
        out = _rms(out, gf_ref[...])
    o_ref[...] = out


def _combine(x, route_g, yg, final_gain, final_norm):
    n, d = x.shape
    tm = 512
    return pl.pallas_call(
        functools.partial(_combine_kernel, final_norm=final_norm),
        grid=(n // tm,),
        in_specs=[
            pl.BlockSpec((tm, d), lambda i: (i, 0)),
            pl.BlockSpec((tm, ROUTE_COLS), lambda i: (i, 0)),
            pl.BlockSpec((1, d), lambda i: (0, 0)),
            pl.BlockSpec((None, tm, d), lambda i: (0, i, 0)),
            pl.BlockSpec((None, tm, d), lambda i: (1, i, 0)),
        ],
        out_specs=pl.BlockSpec((tm, d), lambda i: (i, 0)),
        out_shape=jax.ShapeDtypeStruct((n, d), F32),
        compiler_params=_params("parallel"),
        name="moe_combine",
    )(x, route_g, final_gain, yg, yg)


def _moe_layer(x, gain, w_router, w_gate, w_up, w_down, layer, final_gain, final_norm):
    n, d = x.shape
    tm = 1024 if n * TOP_K >= 8 * 1024 else 256
    tf = 2 * MXU_DIM
    route_i, route_g, counts, h_packed = _router(x, gain, w_router)

    counts = counts[0].astype(jnp.int32)
    padded = ((counts + tm - 1) // tm) * tm
    ends = jnp.cumsum(padded)
    starts = ends - padded
    dest = (starts[route_i[:, :TOP_K]] + route_i[:, TOP_K:2 * TOP_K]).T
    n_blocks = (n * TOP_K) // tm + N_EXPERTS
    block_e = jnp.minimum(
        jnp.searchsorted(ends, jnp.arange(n_blocks, dtype=jnp.int32) * tm, side="right"),
        N_EXPERTS - 1).astype(jnp.int32)
    n_valid = (ends[-1:] // tm).astype(jnp.int32)

    xs = _dispatch(h_packed, dest, n_blocks * tm)
    ys = _experts(xs, block_e, n_valid, w_gate, w_up, w_down, layer, tm, tf)
    yg = _gather_rows(ys, dest.reshape(TOP_K * n)).reshape(TOP_K, n, d)
    return _combine(x, route_g, yg, final_gain, final_norm)


def _rope_tables(seq):
    pos = jnp.arange(seq, dtype=F32)
    inv_freq = ROPE_THETA ** (-jnp.arange(0, HEAD_DIM, 2, dtype=F32) / HEAD_DIM)
    ang = pos[:, None] * inv_freq[None, :]
    reps = LANES // (HEAD_DIM // 2)
    cos = jnp.tile(jnp.cos(ang), (1, reps))
    sin = jnp.tile(jnp.sin(ang), (1, reps))
    first_half = (jnp.arange(LANES) % HEAD_DIM) < HEAD_DIM // 2
    tabs = (cos, jnp.where(first_half, -sin, sin))
    q_scale = HEAD_DIM ** -0.5 * 1.4426950408889634
    return tuple(jnp.stack([t * q_scale, t]) for t in tabs)


def kernel(x, mix_norm, ffn_norm, attn_w_qkv, attn_w_o, pool_w, pool_scale, dense_w_gate, dense_w_up,
           dense_w_down, moe_router, moe_w_gate, moe_w_up, moe_w_down, final_norm):
    batch, seq, d = x.shape
    depth = mix_norm.shape[0]
    assert depth % 2 == 0 and d == N_HEADS * HEAD_DIM
    rope_tables = _rope_tables(seq)
    xf = x.reshape(batch * seq, d)
    final_gain = final_norm.reshape(1, d)
    for i in range(depth):
        j = i // 2
        mix_gain = mix_norm[i].reshape(1, d)
        ffn_gain = ffn_norm[i].reshape(1, d)
        if i % 2 == 0:
            w_qkv = attn_w_qkv[j].astype(BF16)
            qk = _qkv_project(xf, mix_gain, w_qkv, rope_tables, (0, 1), batch, seq)
            v = _qkv_project(xf, mix_gain, w_qkv, None, (2,), batch, seq)
            o = _attention(qk, v, batch, seq)
            xf = _proj_dense_ffn(xf, o, attn_w_o[j].astype(BF16), ffn_gain, dense_w_gate[j].astype(BF16),
                                 dense_w_up[j].astype(BF16), dense_w_down[j].astype(BF16))
        else:
            xf = _pool_mixer(xf, mix_gain, pool_w[j].astype(BF16), pool_scale[j].reshape(1, d), batch, seq)
            xf = _moe_layer(xf, ffn_gain, moe_router[j], moe_w_gate, moe_w_up, moe_w_down, j,
                            final_gain, i == depth - 1)
    return xf.reshape(batch, seq, d)
```

```python
import functools

import jax
import jax.numpy as jnp
from jax import lax
from jax.experimental import pallas as pl
from jax.experimental.pallas import tpu as pltpu
from jax.experimental.pallas import tpu_sc as plsc

WINDOWS = (128, 512, 2048)
DILATIONS = (1, 4, 16)
N_GROUPS = len(WINDOWS)
N_HEADS = 16
HEAD_DIM = 64
ROPE_THETA = 10000.0
POOL_SIZES = (2, 4, 8, 16)
N_EXPERTS = 8
TOP_K = 2
RMS_EPS = 1e-6
MASK_VALUE = -1e30
QKV_ROWS = 512
ATTN_UNROLL = 16

LANES = 128
MXU_DIM = 256
VMEM_LIMIT_BYTES = 56 * 1024 * 1024

F32 = jnp.float32
BF16 = jnp.bfloat16


def _params(*sem):
    return pltpu.CompilerParams(dimension_semantics=sem, vmem_limit_bytes=VMEM_LIMIT_BYTES)


def _rms(x, g):
    ms = jnp.mean(x * x, axis=-1, keepdims=True)
    return x * lax.rsqrt(ms + RMS_EPS) * g


def _silu(g):
    return g / (1.0 + jnp.exp(-g))


def _qkv_kernel(x_ref, g_ref, w_ref, cos_ref, sin_ref, o0_ref, o1_ref, o2_ref, acc_ref):
    tm, d = x_ref.shape
    n_chunks = acc_ref.shape[1]
    lane = lax.broadcasted_iota(jnp.int32, (1, LANES), 1)
    first_half = (lane % HEAD_DIM) < HEAD_DIM // 2
    h = _rms(x_ref[...], g_ref[...]).astype(BF16)
    for g, (o_ref, dil) in enumerate(zip((o0_ref, o1_ref, o2_ref), DILATIONS)):
        rows = tm // dil

        def strided(ref, r, dil=dil, rows=rows):
            if dil == 1:
                return ref[...]
            return ref[pl.ds(r, rows, stride=dil), :]

        for kind in range(3):
            col0 = (g * 3 + kind) * d
            acc = jnp.dot(h, w_ref[:, col0:col0 + d], preferred_element_type=F32)
            for c in range(n_chunks):
                acc_ref[kind, c] = acc[:, c * LANES:(c + 1) * LANES]
            for r in range(dil):
                if kind < 2:
                    cos = strided(cos_ref.at[kind], r)
                    sin = strided(sin_ref.at[kind], r)
                for c in range(n_chunks):
                    t = strided(acc_ref.at[kind, c], r)
                    if kind < 2:
                        partner = jnp.where(first_half, pltpu.roll(t, LANES - HEAD_DIM // 2, 1),
                                            pltpu.roll(t, HEAD_DIM // 2, 1))
                        t = t * cos + partner * sin
                    o_ref[kind, r, :, c * LANES:(c + 1) * LANES] = t.astype(BF16)


def _qkv_project(x, gain, w_qkv, tables, batch, seq):
    n, d = x.shape
    tm = QKV_ROWS
    tiles_per_seq = seq // tm
    tab = pl.BlockSpec((2, tm, LANES), lambda i: (0, i % tiles_per_seq, 0))

    def o_spec(dil):
        return pl.BlockSpec((3, None, dil, tm // dil, d),
                            lambda i: (0, i // tiles_per_seq, 0, i % tiles_per_seq, 0))

    outs = pl.pallas_call(
        _qkv_kernel,
        grid=(n // tm,),
        in_specs=[
            pl.BlockSpec((tm, d), lambda i: (i, 0)),
            pl.BlockSpec((1, d), lambda i: (0, 0)),
            pl.BlockSpec(w_qkv.shape, lambda i: (0, 0), pipeline_mode=pl.Buffered(1)),
            tab, tab,
        ],
        out_specs=[o_spec(dil) for dil in DILATIONS],
        out_shape=[jax.ShapeDtypeStruct((3, batch, dil, seq // dil, d), BF16) for dil in DILATIONS],
        scratch_shapes=[pltpu.VMEM((3, d // LANES, tm, LANES), F32)],
        compiler_params=_params("parallel"),
        name="qkv_rope",
    )(x, gain, w_qkv, *tables)
    return [o.reshape(3, batch, seq, d) for o in outs]


def _attn_kernel(q0, k0, v0, q1, k1, v1, q2, k2, v2, o_ref, num_ref, max_ref, den_ref, kt_ref):
    seq = o_ref.shape[0]
    w = WINDOWS[0] // DILATIONS[0]
    lane = lax.broadcasted_iota(jnp.int32, (1, LANES), 1)
    first_head = lane < HEAD_DIM
    row2 = lax.broadcasted_iota(jnp.int32, (2 * w, 2 * w), 0) % w
    col2 = lax.broadcasted_iota(jnp.int32, (2 * w, 2 * w), 1)
    dist = row2 + w - col2
    band = (dist >= 0) & (dist <= w)
    in_current = col2 >= w
    causal = (lax.broadcasted_iota(jnp.int32, (2 * w, w), 0) % w
              >= lax.broadcasted_iota(jnp.int32, (2 * w, w), 1))

    def tile(q_ref, v_ref, qoff, n, has_prev):
        q = q_ref[pl.ds(qoff, w), :]
        zero = jnp.zeros_like(q)
        qq = jnp.concatenate([jnp.where(first_head, q, zero), jnp.where(first_head, zero, q)], axis=0)
        if has_prev:
            poff = pl.multiple_of(jnp.maximum(qoff - w, 0), w)
            keys_t = jnp.concatenate([kt_ref[:, pl.ds(poff, w)], kt_ref[:, pl.ds(qoff, w)]], axis=1)
            vals = jnp.concatenate([v_ref[pl.ds(poff, w), :], v_ref[pl.ds(qoff, w), :]], axis=0)
            valid = band & (in_current | (n > 0))
        else:
            keys_t = kt_ref[:, pl.ds(qoff, w)]
            vals = v_ref[pl.ds(qoff, w), :]
            valid = causal
        s = jnp.dot(qq, keys_t, preferred_element_type=F32)
        s = jnp.where(valid, s, MASK_VALUE)
        mx = jnp.max(s, axis=1, keepdims=True)
        p = jnp.exp2(s - mx)
        den = jnp.sum(p, axis=1, keepdims=True)
        pv = jnp.dot(p.astype(BF16), vals, preferred_element_type=F32)
        return tuple(jnp.where(first_head, a[:w], a[w:]) for a in (pv, mx, den))

    order = sorted(range(N_GROUPS), key=lambda g: -DILATIONS[g])
    for g in order:
        q_ref, k_ref, v_ref = ((q0, k0, v0), (q1, k1, v1), (q2, k2, v2))[g]
        dil = DILATIONS[g]
        sub_len = seq // dil
        nb = sub_len // w

        kt_ref[...] = k_ref[...].T

        def body(t, carry, q_ref=q_ref, v_ref=v_ref, g=g, dil=dil, sub_len=sub_len, nb=nb):
            r = t // nb
            n = t % nb
            qoff = pl.multiple_of(r * sub_len + n * w, w)
            pv, mx, den = tile(q_ref, v_ref, qoff, n, nb > 1)
            if dil == 1:
                rows = pl.ds(qoff, w)
            else:
                rows = pl.ds(r + dil * n * w, w, stride=dil)
            if g == order[0]:
                num_ref[rows, :] = pv
                max_ref[rows, :] = mx
                den_ref[rows, :] = den
                return carry
            old_max = max_ref[rows, :]
            new_max = jnp.maximum(old_max, mx)
            a = jnp.exp2(old_max - new_max)
            b = jnp.exp2(mx - new_max)
            num = a * num_ref[rows, :] + b * pv
            den = a * den_ref[rows, :] + b * den
            if g == order[-1]:
                o_ref[rows, :] = (num / den).astype(BF16)
            else:
                num_ref[rows, :] = num
                den_ref[rows, :] = den
                max_ref[rows, :] = new_max
            return carry

        lax.fori_loop(0, dil * nb, body, 0, unroll=ATTN_UNROLL)


def _attention(qkv_groups, batch, seq):
    d_model = N_HEADS * HEAD_DIM
    in_specs, args = [], []
    for arr in qkv_groups:
        for kind in range(3):
            in_specs.append(pl.BlockSpec((None, None, seq, LANES), lambda b, hp, kind=kind: (kind, b, 0, hp)))
            args.append(arr)
    return pl.pallas_call(
        _attn_kernel,
        grid=(batch, d_model // LANES),
        in_specs=in_specs,
        out_specs=pl.BlockSpec((None, seq, LANES), lambda b, hp: (b, 0, hp)),
        out_shape=jax.ShapeDtypeStruct((batch, seq, d_model), BF16),
        scratch_shapes=[pltpu.VMEM((seq, LANES), F32)] * 3 + [pltpu.VMEM((LANES, seq), BF16)],
        compiler_params=_params("parallel", "parallel"),
        name="attention",
    )(*args).reshape(batch * seq, d_model)


def _proj_dense_kernel(x_ref, o_ref, wo_ref, g_ref, wg_ref, wu_ref, wd_ref, y_ref):
    x = x_ref[...] + jnp.dot(o_ref[...], wo_ref[...], preferred_element_type=F32)
    h = _rms(x, g_ref[...]).astype(BF16)
    gate = jnp.dot(h, wg_ref[...], preferred_element_type=F32)
    up = jnp.dot(h, wu_ref[...], preferred_element_type=F32)
    a = (_silu(gate) * up).astype(BF16)
    y_ref[...] = x + jnp.dot(a, wd_ref[...], preferred_element_type=F32)


def _proj_dense_ffn(x, o, w_o, gain, w_gate, w_up, w_down):
    n, d = x.shape
    ff = w_gate.shape[1]
    tm = 512
    row = pl.BlockSpec((tm, d), lambda i: (i, 0))
    const = lambda shape: pl.BlockSpec(shape, lambda i: (0, 0), pipeline_mode=pl.Buffered(1))
    return pl.pallas_call(
        _proj_dense_kernel,
        grid=(n // tm,),
        in_specs=[row, row, const((d, d)), const((1, d)), const((d, ff)), const((d, ff)), const((ff, d))],
        out_specs=row,
        out_shape=jax.ShapeDtypeStruct((n, d), F32),
        compiler_params=_params("parallel"),
        name="proj_dense_ffn",
    )(x, o, w_o, gain, w_gate, w_up, w_down)


POOL_HALO = max(POOL_SIZES)


def _pool_kernel(x_ref, halo_ref, g_ref, w_ref, sc_ref, y_ref):
    i = pl.program_id(1)
    ts = x_ref.shape[0]
    g = g_ref[...]
    x = x_ref[...]
    h = _rms(x, g)
    hh = jnp.where(i > 0, _rms(halo_ref[...], g), 0.0)
    ext = jnp.concatenate([hh, h], axis=0)
    pos = i * ts + lax.broadcasted_iota(jnp.int32, (ts, 1), 0)
    pc = w_ref.shape[1]
    for grp, size in enumerate(POOL_SIZES):
        sl = slice(grp * pc, (grp + 1) * pc)
        s = ext[:, sl]
        step = 1
        while step < size:
            s = s + pltpu.roll(s, step, 0)
            step *= 2
        cnt = jnp.minimum(pos + 1, size).astype(F32)
        y = s[POOL_HALO:, :] / cnt - h[:, sl]
        z = jnp.dot(y.astype(BF16), w_ref[grp], preferred_element_type=F32)
        y_ref[:, sl] = x[:, sl] + z * sc_ref[:, sl]


def _pool_mixer(x, gain, pool_w, scale, batch, seq):
    n, d = x.shape
    ts = 512
    x3 = x.reshape(batch, seq, d)
    hb = ts // POOL_HALO
    out = pl.pallas_call(
        _pool_kernel,
        grid=(batch, seq // ts),
        in_specs=[
            pl.BlockSpec((None, ts, d), lambda b, i: (b, i, 0)),
            pl.BlockSpec((None, POOL_HALO, d), lambda b, i: (b, jnp.maximum(i * hb - 1, 0), 0)),
            pl.BlockSpec((1, d), lambda b, i: (0, 0)),
            pl.BlockSpec(pool_w.shape, lambda b, i: (0, 0, 0)),
            pl.BlockSpec((1, d), lambda b, i: (0, 0)),
        ],
        out_specs=pl.BlockSpec((None, ts, d), lambda b, i: (b, i, 0)),
        out_shape=jax.ShapeDtypeStruct((batch, seq, d), F32),
        compiler_params=_params("parallel", "arbitrary"),
        name="pool_mixer",
    )(x3, x3, gain, pool_w, scale)
    return out.reshape(n, d)


ROUTE_COLS = 8


def _pack_bf16_pairs(hr):
    c = hr.shape[1] // 2
    bits = pltpu.bitcast(hr, jnp.int32)
    return (bits[:, c:] & jnp.int32(-65536)) | lax.shift_right_logical(bits[:, :c], 16)


def _unpack_bf16_pairs(packed):
    lo = pltpu.bitcast(lax.shift_left(packed, 16), F32)
    hi = pltpu.bitcast(packed & jnp.int32(-65536), F32)
    return jnp.concatenate([lo.astype(BF16), hi.astype(BF16)], axis=1)


def _router_kernel(x_ref, g_ref, wr_ref, tril_ref, ri_ref, rg_ref, cnt_ref, hp_ref, run_ref):
    i = pl.program_id(0)

    @pl.when(i == 0)
    def _():
        run_ref[...] = jnp.zeros_like(run_ref)

    h = _rms(x_ref[...], g_ref[...])
    wr = wr_ref[...]
    ne = wr.shape[1]
    h_hi = h.astype(BF16)
    hp_ref[...] = _pack_bf16_pairs(h_hi.astype(F32))
    h_lo = (h - h_hi.astype(F32)).astype(BF16)
    w_hi = wr.astype(BF16)
    w_lo = (wr - w_hi.astype(F32)).astype(BF16)
    hi_part = jnp.dot(h_hi, jnp.concatenate([w_hi, w_lo], axis=1), preferred_element_type=F32)
    logits = hi_part[:, :ne] + hi_part[:, ne:] + jnp.dot(h_lo, w_hi, preferred_element_type=F32)
    tm = logits.shape[0]
    lane = lax.broadcasted_iota(jnp.int32, (tm, ne), 1)
    v1 = jnp.max(logits, axis=1, keepdims=True)
    i1 = jnp.min(jnp.where(logits == v1, lane, ne), axis=1, keepdims=True)
    rest = jnp.where(lane == i1, -jnp.inf, logits)
    v2 = jnp.max(rest, axis=1, keepdims=True)
    i2 = jnp.min(jnp.where(rest == v2, lane, ne), axis=1, keepdims=True)
    e = jnp.exp(v2 - v1)
    g1 = 1.0 / (1.0 + e)
    g2 = e / (1.0 + e)
    oh1 = (lane == i1).astype(F32)
    oh2 = (lane == i2).astype(F32)
    both = oh1 + oh2
    before = jnp.dot(tril_ref[...], both.astype(BF16), preferred_element_type=F32) + run_ref[...]
    r1 = jnp.sum(before * oh1, axis=1, keepdims=True).astype(jnp.int32)
    r2 = jnp.sum(before * oh2, axis=1, keepdims=True).astype(jnp.int32)
    run_ref[...] += jnp.sum(both, axis=0, keepdims=True)
    ri_ref[...] = jnp.where(lane == 0, i1, jnp.where(lane == 1, i2, jnp.where(lane == 2, r1, r2)))
    rg_ref[...] = jnp.where(lane == 0, g1, g2)
    cnt_ref[...] = run_ref[...]


def _router(x, gain, w_router):
    n, d = x.shape
    ne = w_router.shape[1]
    tm = 512
    tril = (jnp.arange(tm)[:, None] > jnp.arange(tm)[None, :]).astype(BF16)
    return pl.pallas_call(
        _router_kernel,
        grid=(n // tm,),
        in_specs=[
            pl.BlockSpec((tm, d), lambda i: (i, 0)),
            pl.BlockSpec((1, d), lambda i: (0, 0)),
            pl.BlockSpec((d, ne), lambda i: (0, 0)),
            pl.BlockSpec((tm, tm), lambda i: (0, 0)),
        ],
        out_specs=[
            pl.BlockSpec((tm, ROUTE_COLS), lambda i: (i, 0)),
            pl.BlockSpec((tm, ROUTE_COLS), lambda i: (i, 0)),
            pl.BlockSpec((1, ne), lambda i: (0, 0)),
            pl.BlockSpec((tm, d // 2), lambda i: (i, 0)),
        ],
        out_shape=[
            jax.ShapeDtypeStruct((n, ROUTE_COLS), jnp.int32),
            jax.ShapeDtypeStruct((n, ROUTE_COLS), F32),
            jax.ShapeDtypeStruct((1, ne), F32),
            jax.ShapeDtypeStruct((n, d // 2), jnp.int32),
        ],
        scratch_shapes=[pltpu.VMEM((1, ne), F32)],
        compiler_params=_params("arbitrary"),
        name="moe_router",
    )(x, gain, w_router, tril)


SC_WINDOW = 64


def _sc_workers():
    info = plsc.get_sparse_core_info()
    mesh = plsc.VectorSubcoreMesh(core_axis_name="core", subcore_axis_name="subcore")
    return mesh, info.num_cores, info.num_cores * info.num_subcores


def _dispatch(x, dest, cap):
    n, d = x.shape
    mesh, n_cores, n_workers = _sc_workers()
    per_worker = n // n_workers

    @functools.partial(
        pl.kernel, mesh=mesh, out_type=jax.ShapeDtypeStruct((cap, d), x.dtype),
        scratch_types=[pltpu.VMEM((SC_WINDOW,), jnp.int32)] * TOP_K
        + [pltpu.VMEM((SC_WINDOW, d), x.dtype), pltpu.SemaphoreType.DMA],
        name="moe_dispatch")
    def scatter(x_hbm, *rest):
        idx_hbm, (xs_hbm, *idx_vmem, rows, sem) = rest[:TOP_K], rest[TOP_K:]
        wid = lax.axis_index("subcore") * n_cores + lax.axis_index("core")

        @pl.loop(0, per_worker // SC_WINDOW)
        def _(c):
            base = wid * per_worker + c * SC_WINDOW
            for k in range(TOP_K):
                pltpu.sync_copy(idx_hbm[k].at[pl.ds(base, SC_WINDOW)], idx_vmem[k])
            pltpu.sync_copy(x_hbm.at[pl.ds(base, SC_WINDOW)], rows)
            for k in range(TOP_K):
                pltpu.async_copy(rows, xs_hbm.at[idx_vmem[k]], sem).wait()

    return scatter(x, *[dest[k] for k in range(TOP_K)])


def _gather_rows(table, idx):
    m = idx.shape[0]
    d = table.shape[1]
    mesh, n_cores, n_workers = _sc_workers()
    per_worker = m // n_workers

    @functools.partial(
        pl.kernel, mesh=mesh, out_type=jax.ShapeDtypeStruct((m, d), table.dtype),
        scratch_types=[pltpu.VMEM((SC_WINDOW,), jnp.int32), pltpu.VMEM((SC_WINDOW, d), table.dtype),
                       pltpu.SemaphoreType.DMA],
        name="moe_gather")
    def gather(t_hbm, i_hbm, o_hbm, idx_vmem, rows, sem):
        wid = lax.axis_index("subcore") * n_cores + lax.axis_index("core")

        @pl.loop(0, per_worker // SC_WINDOW)
        def _(c):
            base = wid * per_worker + c * SC_WINDOW
            pltpu.sync_copy(i_hbm.at[pl.ds(base, SC_WINDOW)], idx_vmem)
            pltpu.async_copy(t_hbm.at[idx_vmem], rows, sem).wait()
            pltpu.sync_copy(rows, o_hbm.at[pl.ds(base, SC_WINDOW)])

    return gather(table, idx)


def _expert_kernel(be_ref, nv_ref, xs_ref, wg_ref, wu_ref, wd_ref, ys_ref, xb_ref):
    del be_ref
    b = pl.program_id(0)
    j = pl.program_id(1)

    @pl.when(j == 0)
    def _():
        ys_ref[...] = jnp.zeros_like(ys_ref)

    @pl.when(b < nv_ref[0])
    def _():
        @pl.when(j == 0)
        def _():
            xb_ref[...] = _unpack_bf16_pairs(xs_ref[...])

        x = xb_ref[...]
        acts = []
        for c in range(wg_ref.shape[1] // MXU_DIM):
            cols = slice(c * MXU_DIM, (c + 1) * MXU_DIM)
            gate = jnp.dot(x, wg_ref[:, cols].astype(BF16), preferred_element_type=F32)
            up = jnp.dot(x, wu_ref[:, cols].astype(BF16), preferred_element_type=F32)
            acts.append((_silu(gate) * up).astype(BF16))
        a = jnp.concatenate(acts, axis=1)
        ys_ref[...] += jnp.dot(a, wd_ref[...].astype(BF16), preferred_element_type=F32)


def _experts(xs, block_e, n_valid, w_gate, w_up, w_down, layer, tm, tf):
    cap = xs.shape[0]
    d = 2 * xs.shape[1]
    ff = w_gate.shape[3]
    nj = ff // tf

    def blk(b, nv):
        return jnp.minimum(b, nv[0] - 1)

    def ffi(b, j, nv):
        return jnp.where(b < nv[0], j, nj - 1)

    return pl.pallas_call(
        _expert_kernel,
        grid_spec=pltpu.PrefetchScalarGridSpec(
            num_scalar_prefetch=2,
            grid=(cap // tm, nj),
            in_specs=[
                pl.BlockSpec((tm, d // 2), lambda b, j, be, nv: (blk(b, nv), 0)),
                pl.BlockSpec((None, None, d, tf), lambda b, j, be, nv: (layer, be[blk(b, nv)], 0, ffi(b, j, nv))),
                pl.BlockSpec((None, None, d, tf), lambda b, j, be, nv: (layer, be[blk(b, nv)], 0, ffi(b, j, nv))),
                pl.BlockSpec((None, None, tf, d), lambda b, j, be, nv: (layer, be[blk(b, nv)], ffi(b, j, nv), 0)),
            ],
            out_specs=pl.BlockSpec((tm, d), lambda b, j, be, nv: (b, 0)),
            scratch_shapes=[pltpu.VMEM((tm, d), BF16)],
        ),
        out_shape=jax.ShapeDtypeStruct((cap, d), F32),
        compiler_params=_params("arbitrary", "arbitrary"),
        name="moe_experts",
    )(block_e, n_valid, xs, w_gate, w_up, w_down)


def _combine_kernel(x_ref, rg_ref, gf_ref, y0_ref, y1_ref, o_ref, *, final_norm):
    rg = rg_ref[...]
    out = x_ref[...] + (rg[:, 0:1] * y0_ref[...] + rg[:, 1:2] * y1_ref[...])
    if final_norm:
        out = _rms(out, gf_ref[...])
    o_ref[...] = out


def _combine(x, route_g, yg, final_gain, final_norm):
    n, d = x.shape
    tm = 512
    return pl.pallas_call(
        functools.partial(_combine_kernel, final_norm=final_norm),
        grid=(n // tm,),
        in_specs=[
            pl.BlockSpec((tm, d), lambda i: (i, 0)),
            pl.BlockSpec((tm, ROUTE_COLS), lambda i: (i, 0)),
            pl.BlockSpec((1, d), lambda i: (0, 0)),
            pl.BlockSpec((None, tm, d), lambda i: (0, i, 0)),
            pl.BlockSpec((None, tm, d), lambda i: (1, i, 0)),
        ],
        out_specs=pl.BlockSpec((tm, d), lambda i: (i, 0)),
        out_shape=jax.ShapeDtypeStruct((n, d), F32),
        compiler_params=_params("parallel"),
        name="moe_combine",
    )(x, route_g, final_gain, yg, yg)


def _moe_layer(x, gain, w_router, w_gate, w_up, w_down, layer, final_gain, final_norm):
    n, d = x.shape
    tm = 1024 if n * TOP_K >= 8 * 1024 else 256
    tf = 2 * MXU_DIM
    route_i, route_g, counts, h_packed = _router(x, gain, w_router)

    counts = counts[0].astype(jnp.int32)
    padded = ((counts + tm - 1) // tm) * tm
    ends = jnp.cumsum(padded)
    starts = ends - padded
    dest = (starts[route_i[:, :TOP_K]] + route_i[:, TOP_K:2 * TOP_K]).T
    n_blocks = (n * TOP_K) // tm + N_EXPERTS
    block_e = jnp.minimum(
        jnp.searchsorted(ends, jnp.arange(n_blocks, dtype=jnp.int32) * tm, side="right"),
        N_EXPERTS - 1).astype(jnp.int32)
    n_valid = (ends[-1:] // tm).astype(jnp.int32)

    xs = _dispatch(h_packed, dest, n_blocks * tm)
    ys = _experts(xs, block_e, n_valid, w_gate, w_up, w_down, layer, tm, tf)
    yg = _gather_rows(ys, dest.reshape(TOP_K * n)).reshape(TOP_K, n, d)
    return _combine(x, route_g, yg, final_gain, final_norm)


def _rope_tables(seq):
    pos = jnp.arange(seq, dtype=F32)
    inv_freq = ROPE_THETA ** (-jnp.arange(0, HEAD_DIM, 2, dtype=F32) / HEAD_DIM)
    ang = pos[:, None] * inv_freq[None, :]
    reps = LANES // (HEAD_DIM // 2)
    cos = jnp.tile(jnp.cos(ang), (1, reps))
    sin = jnp.tile(jnp.sin(ang), (1, reps))
    first_half = (jnp.arange(LANES) % HEAD_DIM) < HEAD_DIM // 2
    tabs = (cos, jnp.where(first_half, -sin, sin))
    q_scale = HEAD_DIM ** -0.5 * 1.4426950408889634
    return tuple(jnp.stack([t * q_scale, t]) for t in tabs)


def kernel(x, mix_norm, ffn_norm, attn_w_qkv, attn_w_o, pool_w, pool_scale, dense_w_gate, dense_w_up,
           dense_w_down, moe_router, moe_w_gate, moe_w_up, moe_w_down, final_norm):
    batch, seq, d = x.shape
    depth = mix_norm.shape[0]
    assert depth % 2 == 0 and d == N_HEADS * HEAD_DIM
    rope_tables = _rope_tables(seq)
    xf = x.reshape(batch * seq, d)
    final_gain = final_norm.reshape(1, d)
    for i in range(depth):
        j = i // 2
        mix_gain = mix_norm[i].reshape(1, d)
        ffn_gain = ffn_norm[i].reshape(1, d)
        if i % 2 == 0:
            w_qkv = attn_w_qkv[j].astype(BF16)
            o = _attention(_qkv_project(xf, mix_gain, w_qkv, rope_tables, batch, seq), batch, seq)
            xf = _proj_dense_ffn(xf, o, attn_w_o[j].astype(BF16), ffn_gain, dense_w_gate[j].astype(BF16),
                                 dense_w_up[j].astype(BF16), dense_w_down[j].astype(BF16))
        else:
            xf = _pool_mixer(xf, mix_gain, pool_w[j].astype(BF16), pool_scale[j].reshape(1, d), batch, seq)
            xf = _moe_layer(xf, ffn_gain, moe_router[j], moe_w_gate, moe_w_up, moe_w_down, j,
                            final_gain, i == depth - 1)
    return xf.reshape(batch, seq, d)
```

```python
import functools

import jax
import jax.numpy as jnp
from jax import lax
from jax.experimental import pallas as pl
from jax.experimental.pallas import tpu as pltpu
from jax.experimental.pallas import tpu_sc as plsc

WINDOWS = (128, 512, 2048)
DILATIONS = (1, 4, 16)
N_GROUPS = len(WINDOWS)
N_HEADS = 16
HEAD_DIM = 64
ROPE_THETA = 10000.0
POOL_SIZES = (2, 4, 8, 16)
N_EXPERTS = 8
TOP_K = 2
RMS_EPS = 1e-6
MASK_VALUE = -1e30
QKV_ROWS = 512
ATTN_UNROLL = 16

LANES = 128
MXU_DIM = 256
VMEM_LIMIT_BYTES = 56 * 1024 * 1024

F32 = jnp.float32
BF16 = jnp.bfloat16


def _params(*sem):
    return pltpu.CompilerParams(dimension_semantics=sem, vmem_limit_bytes=VMEM_LIMIT_BYTES)


def _rms(x, g):
    ms = jnp.mean(x * x, axis=-1, keepdims=True)
    return x * lax.rsqrt(ms + RMS_EPS) * g


def _silu(g):
    return g / (1.0 + jnp.exp(-g))


def _qkv_kernel(x_ref, g_ref, w_ref, cos_ref, sin_ref, o0_ref, o1_ref, o2_ref, acc_ref):
    tm, d = x_ref.shape
    n_chunks = acc_ref.shape[1]
    lane = lax.broadcasted_iota(jnp.int32, (1, LANES), 1)
    first_half = (lane % HEAD_DIM) < HEAD_DIM // 2
    h = _rms(x_ref[...], g_ref[...]).astype(BF16)
    for g, (o_ref, dil) in enumerate(zip((o0_ref, o1_ref, o2_ref), DILATIONS)):
        rows = tm // dil

        def strided(ref, r, dil=dil, rows=rows):
            if dil == 1:
                return ref[...]
            return ref[pl.ds(r, rows, stride=dil), :]

        for kind in range(3):
            col0 = (g * 3 + kind) * d
            acc = jnp.dot(h, w_ref[:, col0:col0 + d], preferred_element_type=F32)
            for c in range(n_chunks):
                acc_ref[kind, c] = acc[:, c * LANES:(c + 1) * LANES]
            for r in range(dil):
                if kind < 2:
                    cos = strided(cos_ref.at[kind], r)
                    sin = strided(sin_ref.at[kind], r)
                for c in range(n_chunks):
                    t = strided(acc_ref.at[kind, c], r)
                    if kind < 2:
                        partner = jnp.where(first_half, pltpu.roll(t, LANES - HEAD_DIM // 2, 1),
                                            pltpu.roll(t, HEAD_DIM // 2, 1))
                        t = t * cos + partner * sin
                    o_ref[kind, r, :, c * LANES:(c + 1) * LANES] = t.astype(BF16)


def _qkv_project(x, gain, w_qkv, tables, batch, seq):
    n, d = x.shape
    tm = QKV_ROWS
    tiles_per_seq = seq // tm
    tab = pl.BlockSpec((2, tm, LANES), lambda i: (0, i % tiles_per_seq, 0))

    def o_spec(dil):
        return pl.BlockSpec((3, None, dil, tm // dil, d),
                            lambda i: (0, i // tiles_per_seq, 0, i % tiles_per_seq, 0))

    outs = pl.pallas_call(
        _qkv_kernel,
        grid=(n // tm,),
        in_specs=[
            pl.BlockSpec((tm, d), lambda i: (i, 0)),
            pl.BlockSpec((1, d), lambda i: (0, 0)),
            pl.BlockSpec(w_qkv.shape, lambda i: (0, 0), pipeline_mode=pl.Buffered(1)),
            tab, tab,
        ],
        out_specs=[o_spec(dil) for dil in DILATIONS],
        out_shape=[jax.ShapeDtypeStruct((3, batch, dil, seq // dil, d), BF16) for dil in DILATIONS],
        scratch_shapes=[pltpu.VMEM((3, d // LANES, tm, LANES), F32)],
        compiler_params=_params("parallel"),
        name="qkv_rope",
    )(x, gain, w_qkv, *tables)
    return [o.reshape(3, batch, seq, d) for o in outs]


def _attn_kernel(q0, k0, v0, q1, k1, v1, q2, k2, v2, o_ref, num_ref, max_ref, den_ref, kt_ref):
    seq = o_ref.shape[0]
    w = WINDOWS[0] // DILATIONS[0]
    lane = lax.broadcasted_iota(jnp.int32, (1, LANES), 1)
    first_head = lane < HEAD_DIM
    row2 = lax.broadcasted_iota(jnp.int32, (2 * w, 2 * w), 0) % w
    col2 = lax.broadcasted_iota(jnp.int32, (2 * w, 2 * w), 1)
    dist = row2 + w - col2
    band = (dist >= 0) & (dist <= w)
    in_current = col2 >= w
    causal = (lax.broadcasted_iota(jnp.int32, (2 * w, w), 0) % w
              >= lax.broadcasted_iota(jnp.int32, (2 * w, w), 1))

    def tile(q_ref, v_ref, qoff, n, has_prev):
        q = q_ref[pl.ds(qoff, w), :]
        zero = jnp.zeros_like(q)
        qq = jnp.concatenate([jnp.where(first_head, q, zero), jnp.where(first_head, zero, q)], axis=0)
        if has_prev:
            poff = pl.multiple_of(jnp.maximum(qoff - w, 0), w)
            keys_t = jnp.concatenate([kt_ref[:, pl.ds(poff, w)], kt_ref[:, pl.ds(qoff, w)]], axis=1)
            vals = jnp.concatenate([v_ref[pl.ds(poff, w), :], v_ref[pl.ds(qoff, w), :]], axis=0)
            valid = band & (in_current | (n > 0))
        else:
            keys_t = kt_ref[:, pl.ds(qoff, w)]
            vals = v_ref[pl.ds(qoff, w), :]
            valid = causal
        s = jnp.dot(qq, keys_t, preferred_element_type=F32)
        s = jnp.where(valid, s, MASK_VALUE)
        mx = jnp.max(s, axis=1, keepdims=True)
        p = jnp.exp2(s - mx)
        den = jnp.sum(p, axis=1, keepdims=True)
        pv = jnp.dot(p.astype(BF16), vals, preferred_element_type=F32)
        return tuple(jnp.where(first_head, a[:w], a[w:]) for a in (pv, mx, den))

    order = sorted(range(N_GROUPS), key=lambda g: -DILATIONS[g])
    for g in order:
        q_ref, k_ref, v_ref = ((q0, k0, v0), (q1, k1, v1), (q2, k2, v2))[g]
        dil = DILATIONS[g]
        sub_len = seq // dil
        nb = sub_len // w

        kt_ref[...] = k_ref[...].T

        def body(t, carry, q_ref=q_ref, v_ref=v_ref, g=g, dil=dil, sub_len=sub_len, nb=nb):
            r = t // nb
            n = t % nb
            qoff = pl.multiple_of(r * sub_len + n * w, w)
            pv, mx, den = tile(q_ref, v_ref, qoff, n, nb > 1)
            if dil == 1:
                rows = pl.ds(qoff, w)
            else:
                rows = pl.ds(r + dil * n * w, w, stride=dil)
            if g == order[0]:
                num_ref[rows, :] = pv
                max_ref[rows, :] = mx
                den_ref[rows, :] = den
                return carry
            old_max = max_ref[rows, :]
            new_max = jnp.maximum(old_max, mx)
            a = jnp.exp2(old_max - new_max)
            b = jnp.exp2(mx - new_max)
            num = a * num_ref[rows, :] + b * pv
            den = a * den_ref[rows, :] + b * den
            if g == order[-1]:
                o_ref[rows, :] = (num / den).astype(BF16)
            else:
                num_ref[rows, :] = num
                den_ref[rows, :] = den
                max_ref[rows, :] = new_max
            return carry

        lax.fori_loop(0, dil * nb, body, 0, unroll=ATTN_UNROLL)


def _attention(qkv_groups, batch, seq):
    d_model = N_HEADS * HEAD_DIM
    in_specs, args = [], []
    for arr in qkv_groups:
        for kind in range(3):
            in_specs.append(pl.BlockSpec((None, None, seq, LANES), lambda b, hp, kind=kind: (kind, b, 0, hp)))
            args.append(arr)
    return pl.pallas_call(
        _attn_kernel,
        grid=(batch, d_model // LANES),
        in_specs=in_specs,
        out_specs=pl.BlockSpec((None, seq, LANES), lambda b, hp: (b, 0, hp)),
        out_shape=jax.ShapeDtypeStruct((batch, seq, d_model), BF16),
        scratch_shapes=[pltpu.VMEM((seq, LANES), F32)] * 3 + [pltpu.VMEM((LANES, seq), BF16)],
        compiler_params=_params("parallel", "parallel"),
        name="attention",
    )(*args).reshape(batch * seq, d_model)


def _proj_dense_kernel(x_ref, o_ref, wo_ref, g_ref, wg_ref, wu_ref, wd_ref, y_ref):
    x = x_ref[...] + jnp.dot(o_ref[...], wo_ref[...], preferred_element_type=F32)
    h = _rms(x, g_ref[...]).astype(BF16)
    gate = jnp.dot(h, wg_ref[...], preferred_element_type=F32)
    up = jnp.dot(h, wu_ref[...], preferred_element_type=F32)
    a = (_silu(gate) * up).astype(BF16)
    y_ref[...] = x + jnp.dot(a, wd_ref[...], preferred_element_type=F32)


def _proj_dense_ffn(x, o, w_o, gain, w_gate, w_up, w_down):
    n, d = x.shape
    ff = w_gate.shape[1]
    tm = 512
    row = pl.BlockSpec((tm, d), lambda i: (i, 0))
    const = lambda shape: pl.BlockSpec(shape, lambda i: (0, 0), pipeline_mode=pl.Buffered(1))
    return pl.pallas_call(
        _proj_dense_kernel,
        grid=(n // tm,),
        in_specs=[row, row, const((d, d)), const((1, d)), const((d, ff)), const((d, ff)), const((ff, d))],
        out_specs=row,
        out_shape=jax.ShapeDtypeStruct((n, d), F32),
        compiler_params=_params("parallel"),
        name="proj_dense_ffn",
    )(x, o, w_o, gain, w_gate, w_up, w_down)


POOL_HALO = max(POOL_SIZES)


def _pool_tile(i, x_ref, halo_ref, g_ref, w_ref, sc_ref, y_ref):
    ts = x_ref.shape[0]
    g = g_ref[...]
    x = x_ref[...]
    h = _rms(x, g)
    hh = jnp.where(i > 0, _rms(halo_ref[...], g), 0.0)
    ext = jnp.concatenate([hh, h], axis=0)
    pos = i * ts + lax.broadcasted_iota(jnp.int32, (ts, 1), 0)
    pc = w_ref.shape[1]
    for grp, size in enumerate(POOL_SIZES):
        sl = slice(grp * pc, (grp + 1) * pc)
        s = ext[:, sl]
        step = 1
        while step < size:
            s = s + pltpu.roll(s, step, 0)
            step *= 2
        cnt = jnp.minimum(pos + 1, size).astype(F32)
        y = s[POOL_HALO:, :] / cnt - h[:, sl]
        z = jnp.dot(y.astype(BF16), w_ref[grp], preferred_element_type=F32)
        y_ref[:, sl] = x[:, sl] + z * sc_ref[:, sl]


def _pool_router_kernel(x_ref, halo_ref, g_ref, w_ref, sc_ref, g2_ref, wr_ref, tril_ref,
                        y_ref, ri_ref, rg_ref, cnt_ref, hp_ref, run_ref):
    b = pl.program_id(0)
    i = pl.program_id(1)
    _pool_tile(i, x_ref, halo_ref, g_ref, w_ref, sc_ref, y_ref)
    _route_tile((b == 0) & (i == 0), y_ref[...], g2_ref, wr_ref, tril_ref, ri_ref, rg_ref, cnt_ref, hp_ref, run_ref)


def _pool_mixer_and_router(x, gain, pool_w, scale, ffn_gain, w_router, batch, seq):
    n, d = x.shape
    ne = w_router.shape[1]
    ts = 512
    tiles = seq // ts
    x3 = x.reshape(batch, seq, d)
    hb = ts // POOL_HALO
    tril = (jnp.arange(ts)[:, None] > jnp.arange(ts)[None, :]).astype(BF16)
    const = lambda shape: pl.BlockSpec(shape, lambda b, i: (0,) * len(shape))
    tok = lambda cols: pl.BlockSpec((ts, cols), lambda b, i: (b * tiles + i, 0))
    out, route_i, route_g, counts, h_packed = pl.pallas_call(
        _pool_router_kernel,
        grid=(batch, tiles),
        in_specs=[
            pl.BlockSpec((None, ts, d), lambda b, i: (b, i, 0)),
            pl.BlockSpec((None, POOL_HALO, d), lambda b, i: (b, jnp.maximum(i * hb - 1, 0), 0)),
            const((1, d)), const(pool_w.shape), const((1, d)), const((1, d)), const((d, ne)), const((ts, ts)),
        ],
        out_specs=[
            pl.BlockSpec((None, ts, d), lambda b, i: (b, i, 0)),
            tok(ROUTE_COLS), tok(ROUTE_COLS), const((1, ne)), tok(d // 2),
        ],
        out_shape=[
            jax.ShapeDtypeStruct((batch, seq, d), F32),
            jax.ShapeDtypeStruct((n, ROUTE_COLS), jnp.int32),
            jax.ShapeDtypeStruct((n, ROUTE_COLS), F32),
            jax.ShapeDtypeStruct((1, ne), F32),
            jax.ShapeDtypeStruct((n, d // 2), jnp.int32),
        ],
        scratch_shapes=[pltpu.VMEM((1, ne), F32)],
        compiler_params=_params("arbitrary", "arbitrary"),
        name="pool_router",
    )(x3, x3, gain, pool_w, scale, ffn_gain, w_router, tril)
    return out.reshape(n, d), route_i, route_g, counts, h_packed


ROUTE_COLS = 8


def _pack_bf16_pairs(hr):
    c = hr.shape[1] // 2
    bits = pltpu.bitcast(hr, jnp.int32)
    return (bits[:, c:] & jnp.int32(-65536)) | lax.shift_right_logical(bits[:, :c], 16)


def _unpack_bf16_pairs(packed):
    lo = pltpu.bitcast(lax.shift_left(packed, 16), F32)
    hi = pltpu.bitcast(packed & jnp.int32(-65536), F32)
    return jnp.concatenate([lo.astype(BF16), hi.astype(BF16)], axis=1)


def _route_tile(first, x, g_ref, wr_ref, tril_ref, ri_ref, rg_ref, cnt_ref, hp_ref, run_ref):
    @pl.when(first)
    def _():
        run_ref[...] = jnp.zeros_like(run_ref)

    h = _rms(x, g_ref[...])
    wr = wr_ref[...]
    ne = wr.shape[1]
    h_hi = h.astype(BF16)
    hp_ref[...] = _pack_bf16_pairs(h_hi.astype(F32))
    h_lo = (h - h_hi.astype(F32)).astype(BF16)
    w_hi = wr.astype(BF16)
    w_lo = (wr - w_hi.astype(F32)).astype(BF16)
    hi_part = jnp.dot(h_hi, jnp.concatenate([w_hi, w_lo], axis=1), preferred_element_type=F32)
    logits = hi_part[:, :ne] + hi_part[:, ne:] + jnp.dot(h_lo, w_hi, preferred_element_type=F32)
    tm = logits.shape[0]
    lane = lax.broadcasted_iota(jnp.int32, (tm, ne), 1)
    v1 = jnp.max(logits, axis=1, keepdims=True)
    i1 = jnp.min(jnp.where(logits == v1, lane, ne), axis=1, keepdims=True)
    rest = jnp.where(lane == i1, -jnp.inf, logits)
    v2 = jnp.max(rest, axis=1, keepdims=True)
    i2 = jnp.min(jnp.where(rest == v2, lane, ne), axis=1, keepdims=True)
    e = jnp.exp(v2 - v1)
    g1 = 1.0 / (1.0 + e)
    g2 = e / (1.0 + e)
    oh1 = (lane == i1).astype(F32)
    oh2 = (lane == i2).astype(F32)
    both = oh1 + oh2
    before = jnp.dot(tril_ref[...], both.astype(BF16), preferred_element_type=F32) + run_ref[...]
    r1 = jnp.sum(before * oh1, axis=1, keepdims=True).astype(jnp.int32)
    r2 = jnp.sum(before * oh2, axis=1, keepdims=True).astype(jnp.int32)
    run_ref[...] += jnp.sum(both, axis=0, keepdims=True)
    ri_ref[...] = jnp.where(lane == 0, i1, jnp.where(lane == 1, i2, jnp.where(lane == 2, r1, r2)))
    rg_ref[...] = jnp.where(lane == 0, g1, g2)
    cnt_ref[...] = run_ref[...]


SC_WINDOW = 64


def _sc_workers():
    info = plsc.get_sparse_core_info()
    mesh = plsc.VectorSubcoreMesh(core_axis_name="core", subcore_axis_name="subcore")
    return mesh, info.num_cores, info.num_cores * info.num_subcores


def _dispatch(x, dest, cap):
    n, d = x.shape
    mesh, n_cores, n_workers = _sc_workers()
    per_worker = n // n_workers

    @functools.partial(
        pl.kernel, mesh=mesh, out_type=jax.ShapeDtypeStruct((cap, d), x.dtype),
        scratch_types=[pltpu.VMEM((SC_WINDOW,), jnp.int32)] * TOP_K
        + [pltpu.VMEM((SC_WINDOW, d), x.dtype), pltpu.SemaphoreType.DMA],
        name="moe_dispatch")
    def scatter(x_hbm, *rest):
        idx_hbm, (xs_hbm, *idx_vmem, rows, sem) = rest[:TOP_K], rest[TOP_K:]
        wid = lax.axis_index("subcore") * n_cores + lax.axis_index("core")

        @pl.loop(0, per_worker // SC_WINDOW)
        def _(c):
            base = wid * per_worker + c * SC_WINDOW
            for k in range(TOP_K):
                pltpu.sync_copy(idx_hbm[k].at[pl.ds(base, SC_WINDOW)], idx_vmem[k])
            pltpu.sync_copy(x_hbm.at[pl.ds(base, SC_WINDOW)], rows)
            for k in range(TOP_K):
                pltpu.async_copy(rows, xs_hbm.at[idx_vmem[k]], sem).wait()

    return scatter(x, *[dest[k] for k in range(TOP_K)])


def _gather_rows(table, idx):
    m = idx.shape[0]
    d = table.shape[1]
    mesh, n_cores, n_workers = _sc_workers()
    per_worker = m // n_workers

    @functools.partial(
        pl.kernel, mesh=mesh, out_type=jax.ShapeDtypeStruct((m, d), table.dtype),
        scratch_types=[pltpu.VMEM((SC_WINDOW,), jnp.int32), pltpu.VMEM((SC_WINDOW, d), table.dtype),
                       pltpu.SemaphoreType.DMA],
        name="moe_gather")
    def gather(t_hbm, i_hbm, o_hbm, idx_vmem, rows, sem):
        wid = lax.axis_index("subcore") * n_cores + lax.axis_index("core")

        @pl.loop(0, per_worker // SC_WINDOW)
        def _(c):
            base = wid * per_worker + c * SC_WINDOW
            pltpu.sync_copy(i_hbm.at[pl.ds(base, SC_WINDOW)], idx_vmem)
            pltpu.async_copy(t_hbm.at[idx_vmem], rows, sem).wait()
            pltpu.sync_copy(rows, o_hbm.at[pl.ds(base, SC_WINDOW)])

    return gather(table, idx)


def _expert_kernel(be_ref, nv_ref, xs_ref, wg_ref, wu_ref, wd_ref, ys_ref, xb_ref):
    del be_ref
    b = pl.program_id(0)
    j = pl.program_id(1)

    @pl.when(j == 0)
    def _():
        ys_ref[...] = jnp.zeros_like(ys_ref)

    @pl.when(b < nv_ref[0])
    def _():
        @pl.when(j == 0)
        def _():
            xb_ref[...] = _unpack_bf16_pairs(xs_ref[...])

        x = xb_ref[...]
        acts = []
        for c in range(wg_ref.shape[1] // MXU_DIM):
            cols = slice(c * MXU_DIM, (c + 1) * MXU_DIM)
            gate = jnp.dot(x, wg_ref[:, cols].astype(BF16), preferred_element_type=F32)
            up = jnp.dot(x, wu_ref[:, cols].astype(BF16), preferred_element_type=F32)
            acts.append((_silu(gate) * up).astype(BF16))
        a = jnp.concatenate(acts, axis=1)
        ys_ref[...] += jnp.dot(a, wd_ref[...].astype(BF16), preferred_element_type=F32)


def _experts(xs, block_e, n_valid, w_gate, w_up, w_down, layer, tm, tf):
    cap = xs.shape[0]
    d = 2 * xs.shape[1]
    ff = w_gate.shape[3]
    nj = ff // tf

    def blk(b, nv):
        return jnp.minimum(b, nv[0] - 1)

    def ffi(b, j, nv):
        return jnp.where(b < nv[0], j, nj - 1)

    return pl.pallas_call(
        _expert_kernel,
        grid_spec=pltpu.PrefetchScalarGridSpec(
            num_scalar_prefetch=2,
            grid=(cap // tm, nj),
            in_specs=[
                pl.BlockSpec((tm, d // 2), lambda b, j, be, nv: (blk(b, nv), 0)),
                pl.BlockSpec((None, None, d, tf), lambda b, j, be, nv: (layer, be[blk(b, nv)], 0, ffi(b, j, nv))),
                pl.BlockSpec((None, None, d, tf), lambda b, j, be, nv: (layer, be[blk(b, nv)], 0, ffi(b, j, nv))),
                pl.BlockSpec((None, None, tf, d), lambda b, j, be, nv: (layer, be[blk(b, nv)], ffi(b, j, nv), 0)),
            ],
            out_specs=pl.BlockSpec((tm, d), lambda b, j, be, nv: (b, 0)),
            scratch_shapes=[pltpu.VMEM((tm, d), BF16)],
        ),
        out_shape=jax.ShapeDtypeStruct((cap, d), F32),
        compiler_params=_params("arbitrary", "arbitrary"),
        name="moe_experts",
    )(block_e, n_valid, xs, w_gate, w_up, w_down)


def _combine_kernel(x_ref, rg_ref, gf_ref, y0_ref, y1_ref, o_ref, *, final_norm):
    rg = rg_ref[...]
    out = x_ref[...] + (rg[:, 0:1] * y0_ref[...] + rg[:, 1:2] * y1_ref[...])
    if final_norm:
        out = _rms(out, gf_ref[...])
    o_ref[...] = out


def _combine(x, route_g, yg, final_gain, final_norm):
    n, d = x.shape
    tm = 512
    return pl.pallas_call(
        functools.partial(_combine_kernel, final_norm=final_norm),
        grid=(n // tm,),
        in_specs=[
            pl.BlockSpec((tm, d), lambda i: (i, 0)),
            pl.BlockSpec((tm, ROUTE_COLS), lambda i: (i, 0)),
            pl.BlockSpec((1, d), lambda i: (0, 0)),
            pl.BlockSpec((None, tm, d), lambda i: (0, i, 0)),
            pl.BlockSpec((None, tm, d), lambda i: (1, i, 0)),
        ],
        out_specs=pl.BlockSpec((tm, d), lambda i: (i, 0)),
        out_shape=jax.ShapeDtypeStruct((n, d), F32),
        compiler_params=_params("parallel"),
        name="moe_combine",
    )(x, route_g, final_gain, yg, yg)


def _moe_layer(x, routing, w_gate, w_up, w_down, layer, final_gain, final_norm):
    n, d = x.shape
    tm = 1024 if n * TOP_K >= 8 * 1024 else 256
    tf = 2 * MXU_DIM
    route_i, route_g, counts, h_packed = routing

    counts = counts[0].astype(jnp.int32)
    padded = ((counts + tm - 1) // tm) * tm
    ends = jnp.cumsum(padded)
    starts = ends - padded
    dest = (starts[route_i[:, :TOP_K]] + route_i[:, TOP_K:2 * TOP_K]).T
    n_blocks = (n * TOP_K) // tm + N_EXPERTS
    block_e = jnp.minimum(
        jnp.searchsorted(ends, jnp.arange(n_blocks, dtype=jnp.int32) * tm, side="right"),
        N_EXPERTS - 1).astype(jnp.int32)
    n_valid = (ends[-1:] // tm).astype(jnp.int32)

    xs = _dispatch(h_packed, dest, n_blocks * tm)
    ys = _experts(xs, block_e, n_valid, w_gate, w_up, w_down, layer, tm, tf)
    yg = _gather_rows(ys, dest.reshape(TOP_K * n)).reshape(TOP_K, n, d)
    return _combine(x, route_g, yg, final_gain, final_norm)


def _rope_tables(seq):
    pos = jnp.arange(seq, dtype=F32)
    inv_freq = ROPE_THETA ** (-jnp.arange(0, HEAD_DIM, 2, dtype=F32) / HEAD_DIM)
    ang = pos[:, None] * inv_freq[None, :]
    reps = LANES // (HEAD_DIM // 2)
    cos = jnp.tile(jnp.cos(ang), (1, reps))
    sin = jnp.tile(jnp.sin(ang), (1, reps))
    first_half = (jnp.arange(LANES) % HEAD_DIM) < HEAD_DIM // 2
    tabs = (cos, jnp.where(first_half, -sin, sin))
    q_scale = HEAD_DIM ** -0.5 * 1.4426950408889634
    return tuple(jnp.stack([t * q_scale, t]) for t in tabs)


def kernel(x, mix_norm, ffn_norm, attn_w_qkv, attn_w_o, pool_w, pool_scale, dense_w_gate, dense_w_up,
           dense_w_down, moe_router, moe_w_gate, moe_w_up, moe_w_down, final_norm):
    batch, seq, d = x.shape
    depth = mix_norm.shape[0]
    assert depth % 2 == 0 and d == N_HEADS * HEAD_DIM
    rope_tables = _rope_tables(seq)
    xf = x.reshape(batch * seq, d)
    final_gain = final_norm.reshape(1, d)
    for i in range(depth):
        j = i // 2
        mix_gain = mix_norm[i].reshape(1, d)
        ffn_gain = ffn_norm[i].reshape(1, d)
        if i % 2 == 0:
            w_qkv = attn_w_qkv[j].astype(BF16)
            o = _attention(_qkv_project(xf, mix_gain, w_qkv, rope_tables, batch, seq), batch, seq)
            xf = _proj_dense_ffn(xf, o, attn_w_o[j].astype(BF16), ffn_gain, dense_w_gate[j].astype(BF16),
                                 dense_w_up[j].astype(BF16), dense_w_down[j].astype(BF16))
        else:
            xf, *routing = _pool_mixer_and_router(xf, mix_gain, pool_w[j].astype(BF16), pool_scale[j].reshape(1, d),
                                                  ffn_gain, moe_router[j], batch, seq)
            xf = _moe_layer(xf, routing, moe_w_gate, moe_w_up, moe_w_down, j, final_gain, i == depth - 1)
    return xf.reshape(batch, seq, d)
```

```python
import functools

import jax
import jax.numpy as jnp
from jax import lax
from jax.experimental import pallas as pl
from jax.experimental.pallas import tpu as pltpu
from jax.experimental.pallas import tpu_sc as plsc

WINDOWS = (128, 512, 2048)
DILATIONS = (1, 4, 16)
N_GROUPS = len(WINDOWS)
N_HEADS = 16
HEAD_DIM = 64
ROPE_THETA = 10000.0
POOL_SIZES = (2, 4, 8, 16)
N_EXPERTS = 8
TOP_K = 2
RMS_EPS = 1e-6
MASK_VALUE = -1e30
QKV_ROWS = 512
ATTN_UNROLL = 16

LANES = 128
MXU_DIM = 256
VMEM_LIMIT_BYTES = 56 * 1024 * 1024

F32 = jnp.float32
BF16 = jnp.bfloat16


def _params(*sem):
    return pltpu.CompilerParams(dimension_semantics=sem, vmem_limit_bytes=VMEM_LIMIT_BYTES)


def _rms(x, g):
    ms = jnp.mean(x * x, axis=-1, keepdims=True)
    return x * lax.rsqrt(ms + RMS_EPS) * g


def _silu(g):
    return g / (1.0 + jnp.exp(-g))


def _qkv_kernel(x_ref, g_ref, w_ref, cos_ref, sin_ref, o0_ref, o1_ref, o2_ref, acc_ref):
    tm, d = x_ref.shape
    n_chunks = acc_ref.shape[1]
    lane = lax.broadcasted_iota(jnp.int32, (1, LANES), 1)
    first_half = (lane % HEAD_DIM) < HEAD_DIM // 2
    h = _rms(x_ref[...], g_ref[...]).astype(BF16)
    for g, (o_ref, dil) in enumerate(zip((o0_ref, o1_ref, o2_ref), DILATIONS)):
        rows = tm // dil

        def strided(ref, r, dil=dil, rows=rows):
            if dil == 1:
                return ref[...]
            return ref[pl.ds(r, rows, stride=dil), :]

        for kind in range(3):
            col0 = (g * 3 + kind) * d
            acc = jnp.dot(h, w_ref[:, col0:col0 + d], preferred_element_type=F32)
            for c in range(n_chunks):
                acc_ref[kind, c] = acc[:, c * LANES:(c + 1) * LANES]
            for r in range(dil):
                if kind < 2:
                    cos = strided(cos_ref.at[kind], r)
                    sin = strided(sin_ref.at[kind], r)
                for c in range(n_chunks):
                    t = strided(acc_ref.at[kind, c], r)
                    if kind < 2:
                        partner = jnp.where(first_half, pltpu.roll(t, LANES - HEAD_DIM // 2, 1),
                                            pltpu.roll(t, HEAD_DIM // 2, 1))
                        t = t * cos + partner * sin
                    o_ref[kind, r, :, c * LANES:(c + 1) * LANES] = t.astype(BF16)


def _qkv_project(x, gain, w_qkv, tables, batch, seq):
    n, d = x.shape
    tm = QKV_ROWS
    tiles_per_seq = seq // tm
    tab = pl.BlockSpec((2, tm, LANES), lambda i: (0, i % tiles_per_seq, 0))

    def o_spec(dil):
        return pl.BlockSpec((3, None, dil, tm // dil, d),
                            lambda i: (0, i // tiles_per_seq, 0, i % tiles_per_seq, 0))

    outs = pl.pallas_call(
        _qkv_kernel,
        grid=(n // tm,),
        in_specs=[
            pl.BlockSpec((tm, d), lambda i: (i, 0)),
            pl.BlockSpec((1, d), lambda i: (0, 0)),
            pl.BlockSpec(w_qkv.shape, lambda i: (0, 0), pipeline_mode=pl.Buffered(1)),
            tab, tab,
        ],
        out_specs=[o_spec(dil) for dil in DILATIONS],
        out_shape=[jax.ShapeDtypeStruct((3, batch, dil, seq // dil, d), BF16) for dil in DILATIONS],
        scratch_shapes=[pltpu.VMEM((3, d // LANES, tm, LANES), F32)],
        compiler_params=_params("parallel"),
        name="qkv_rope",
    )(x, gain, w_qkv, *tables)
    return [o.reshape(3, batch, seq, d) for o in outs]


def _attn_kernel(q0, k0, v0, q1, k1, v1, q2, k2, v2, o_ref, num_ref, max_ref, den_ref, kt_ref):
    seq = o_ref.shape[0]
    w = WINDOWS[0] // DILATIONS[0]
    lane = lax.broadcasted_iota(jnp.int32, (1, LANES), 1)
    first_head = lane < HEAD_DIM
    row2 = lax.broadcasted_iota(jnp.int32, (2 * w, 2 * w), 0) % w
    col2 = lax.broadcasted_iota(jnp.int32, (2 * w, 2 * w), 1)
    dist = row2 + w - col2
    band = (dist >= 0) & (dist <= w)
    in_current = col2 >= w
    causal = (lax.broadcasted_iota(jnp.int32, (2 * w, w), 0) % w
              >= lax.broadcasted_iota(jnp.int32, (2 * w, w), 1))

    def tile(q_ref, v_ref, qoff, n, has_prev):
        q = q_ref[pl.ds(qoff, w), :]
        zero = jnp.zeros_like(q)
        qq = jnp.concatenate([jnp.where(first_head, q, zero), jnp.where(first_head, zero, q)], axis=0)
        if has_prev:
            poff = pl.multiple_of(jnp.maximum(qoff - w, 0), w)
            keys_t = jnp.concatenate([kt_ref[:, pl.ds(poff, w)], kt_ref[:, pl.ds(qoff, w)]], axis=1)
            vals = jnp.concatenate([v_ref[pl.ds(poff, w), :], v_ref[pl.ds(qoff, w), :]], axis=0)
            valid = band & (in_current | (n > 0))
        else:
            keys_t = kt_ref[:, pl.ds(qoff, w)]
            vals = v_ref[pl.ds(qoff, w), :]
            valid = causal
        s = jnp.dot(qq, keys_t, preferred_element_type=F32)
        s = jnp.where(valid, s, MASK_VALUE)
        mx = jnp.max(s, axis=1, keepdims=True)
        p = jnp.exp2(s - mx)
        den = jnp.sum(p, axis=1, keepdims=True)
        pv = jnp.dot(p.astype(BF16), vals, preferred_element_type=F32)
        return tuple(jnp.where(first_head, a[:w], a[w:]) for a in (pv, mx, den))

    order = sorted(range(N_GROUPS), key=lambda g: -DILATIONS[g])
    for g in order:
        q_ref, k_ref, v_ref = ((q0, k0, v0), (q1, k1, v1), (q2, k2, v2))[g]
        dil = DILATIONS[g]
        sub_len = seq // dil
        nb = sub_len // w

        kt_ref[...] = k_ref[...].T

        def body(t, carry, q_ref=q_ref, v_ref=v_ref, g=g, dil=dil, sub_len=sub_len, nb=nb):
            r = t // nb
            n = t % nb
            qoff = pl.multiple_of(r * sub_len + n * w, w)
            pv, mx, den = tile(q_ref, v_ref, qoff, n, nb > 1)
            if dil == 1:
                rows = pl.ds(qoff, w)
            else:
                rows = pl.ds(r + dil * n * w, w, stride=dil)
            if g == order[0]:
                num_ref[rows, :] = pv
                max_ref[rows, :] = mx
                den_ref[rows, :] = den
                return carry
            old_max = max_ref[rows, :]
            new_max = jnp.maximum(old_max, mx)
            a = jnp.exp2(old_max - new_max)
            b = jnp.exp2(mx - new_max)
            num = a * num_ref[rows, :] + b * pv
            den = a * den_ref[rows, :] + b * den
            if g == order[-1]:
                o_ref[rows, :] = (num / den).astype(BF16)
            else:
                num_ref[rows, :] = num
                den_ref[rows, :] = den
                max_ref[rows, :] = new_max
            return carry

        lax.fori_loop(0, dil * nb, body, 0, unroll=ATTN_UNROLL)


def _attention(qkv_groups, batch, seq):
    d_model = N_HEADS * HEAD_DIM
    in_specs, args = [], []
    for arr in qkv_groups:
        for kind in range(3):
            in_specs.append(pl.BlockSpec((None, None, seq, LANES), lambda b, hp, kind=kind: (kind, b, 0, hp)))
            args.append(arr)
    return pl.pallas_call(
        _attn_kernel,
        grid=(batch, d_model // LANES),
        in_specs=in_specs,
        out_specs=pl.BlockSpec((None, seq, LANES), lambda b, hp: (b, 0, hp)),
        out_shape=jax.ShapeDtypeStruct((batch, seq, d_model), BF16),
        scratch_shapes=[pltpu.VMEM((seq, LANES), F32)] * 3 + [pltpu.VMEM((LANES, seq), BF16)],
        compiler_params=_params("parallel", "parallel"),
        name="attention",
    )(*args).reshape(batch * seq, d_model)


def _proj_dense_kernel(x_ref, o_ref, wo_ref, g_ref, wg_ref, wu_ref, wd_ref, y_ref):
    x = x_ref[...] + jnp.dot(o_ref[...], wo_ref[...], preferred_element_type=F32)
    h = _rms(x, g_ref[...]).astype(BF16)
    gate = jnp.dot(h, wg_ref[...], preferred_element_type=F32)
    up = jnp.dot(h, wu_ref[...], preferred_element_type=F32)
    a = (_silu(gate) * up).astype(BF16)
    y_ref[...] = x + jnp.dot(a, wd_ref[...], preferred_element_type=F32)


def _proj_dense_ffn(x, o, w_o, gain, w_gate, w_up, w_down):
    n, d = x.shape
    ff = w_gate.shape[1]
    tm = 512
    row = pl.BlockSpec((tm, d), lambda i: (i, 0))
    const = lambda shape: pl.BlockSpec(shape, lambda i: (0, 0), pipeline_mode=pl.Buffered(1))
    return pl.pallas_call(
        _proj_dense_kernel,
        grid=(n // tm,),
        in_specs=[row, row, const((d, d)), const((1, d)), const((d, ff)), const((d, ff)), const((ff, d))],
        out_specs=row,
        out_shape=jax.ShapeDtypeStruct((n, d), F32),
        compiler_params=_params("parallel"),
        name="proj_dense_ffn",
    )(x, o, w_o, gain, w_gate, w_up, w_down)


POOL_HALO = max(POOL_SIZES)


def _pool_tile(i, x_ref, halo_ref, g_ref, w_ref, sc_ref, y_ref):
    ts = x_ref.shape[0]
    g = g_ref[...]
    x = x_ref[...]
    h = _rms(x, g)
    hh = jnp.where(i > 0, _rms(halo_ref[...], g), 0.0)
    ext = jnp.concatenate([hh, h], axis=0)
    pos = i * ts + lax.broadcasted_iota(jnp.int32, (ts, 1), 0)
    pc = w_ref.shape[1]
    for grp, size in enumerate(POOL_SIZES):
        sl = slice(grp * pc, (grp + 1) * pc)
        s = ext[:, sl]
        step = 1
        while step < size:
            s = s + pltpu.roll(s, step, 0)
            step *= 2
        cnt = jnp.minimum(pos + 1, size).astype(F32)
        y = s[POOL_HALO:, :] / cnt - h[:, sl]
        z = jnp.dot(y.astype(BF16), w_ref[grp], preferred_element_type=F32)
        y_ref[:, sl] = x[:, sl] + z * sc_ref[:, sl]


def _pool_router_kernel(x_ref, halo_ref, g_ref, w_ref, sc_ref, g2_ref, wr_ref, tril_ref,
                        y_ref, ri_ref, rg_ref, cnt_ref, hp_ref, run_ref):
    b = pl.program_id(0)
    i = pl.program_id(1)
    _pool_tile(i, x_ref, halo_ref, g_ref, w_ref, sc_ref, y_ref)
    _route_tile((b == 0) & (i == 0), y_ref[...], g2_ref, wr_ref, tril_ref, ri_ref, rg_ref, cnt_ref, hp_ref, run_ref)


def _pool_mixer_and_router(x, gain, pool_w, scale, ffn_gain, w_router, batch, seq):
    n, d = x.shape
    ne = w_router.shape[1]
    ts = 512
    tiles = seq // ts
    x3 = x.reshape(batch, seq, d)
    hb = ts // POOL_HALO
    tril = (jnp.arange(ts)[:, None] > jnp.arange(ts)[None, :]).astype(BF16)
    const = lambda shape: pl.BlockSpec(shape, lambda b, i: (0,) * len(shape))
    tok = lambda cols: pl.BlockSpec((ts, cols), lambda b, i: (b * tiles + i, 0))
    out, route_i, route_g, counts, h_packed = pl.pallas_call(
        _pool_router_kernel,
        grid=(batch, tiles),
        in_specs=[
            pl.BlockSpec((None, ts, d), lambda b, i: (b, i, 0)),
            pl.BlockSpec((None, POOL_HALO, d), lambda b, i: (b, jnp.maximum(i * hb - 1, 0), 0)),
            const((1, d)), const(pool_w.shape), const((1, d)), const((1, d)), const((d, ne)), const((ts, ts)),
        ],
        out_specs=[
            pl.BlockSpec((None, ts, d), lambda b, i: (b, i, 0)),
            tok(ROUTE_COLS), tok(ROUTE_COLS), const((1, ne)), tok(d // 2),
        ],
        out_shape=[
            jax.ShapeDtypeStruct((batch, seq, d), F32),
            jax.ShapeDtypeStruct((n, ROUTE_COLS), jnp.int32),
            jax.ShapeDtypeStruct((n, ROUTE_COLS), F32),
            jax.ShapeDtypeStruct((1, ne), F32),
            jax.ShapeDtypeStruct((n, d // 2), jnp.int32),
        ],
        scratch_shapes=[pltpu.VMEM((1, ne), F32)],
        compiler_params=_params("arbitrary", "arbitrary"),
        name="pool_router",
    )(x3, x3, gain, pool_w, scale, ffn_gain, w_router, tril)
    return out.reshape(n, d), route_i, route_g, counts, h_packed


ROUTE_COLS = 8


def _pack_bf16_pairs(hr):
    c = hr.shape[1] // 2
    bits = pltpu.bitcast(hr, jnp.int32)
    return (bits[:, c:] & jnp.int32(-65536)) | lax.shift_right_logical(bits[:, :c], 16)


def _unpack_bf16_pairs(packed):
    lo = pltpu.bitcast(lax.shift_left(packed, 16), F32)
    hi = pltpu.bitcast(packed & jnp.int32(-65536), F32)
    return jnp.concatenate([lo.astype(BF16), hi.astype(BF16)], axis=1)


def _route_tile(first, x, g_ref, wr_ref, tril_ref, ri_ref, rg_ref, cnt_ref, hp_ref, run_ref):
    @pl.when(first)
    def _():
        run_ref[...] = jnp.zeros_like(run_ref)

    h = _rms(x, g_ref[...])
    wr = wr_ref[...]
    ne = wr.shape[1]
    h_hi = h.astype(BF16)
    hp_ref[...] = _pack_bf16_pairs(h_hi.astype(F32))
    h_lo = (h - h_hi.astype(F32)).astype(BF16)
    w_hi = wr.astype(BF16)
    w_lo = (wr - w_hi.astype(F32)).astype(BF16)
    hi_part = jnp.dot(h_hi, jnp.concatenate([w_hi, w_lo], axis=1), preferred_element_type=F32)
    logits = hi_part[:, :ne] + hi_part[:, ne:] + jnp.dot(h_lo, w_hi, preferred_element_type=F32)
    tm = logits.shape[0]
    lane = lax.broadcasted_iota(jnp.int32, (tm, ne), 1)
    v1 = jnp.max(logits, axis=1, keepdims=True)
    i1 = jnp.min(jnp.where(logits == v1, lane, ne), axis=1, keepdims=True)
    rest = jnp.where(lane == i1, -jnp.inf, logits)
    v2 = jnp.max(rest, axis=1, keepdims=True)
    i2 = jnp.min(jnp.where(rest == v2, lane, ne), axis=1, keepdims=True)
    e = jnp.exp(v2 - v1)
    g1 = 1.0 / (1.0 + e)
    g2 = e / (1.0 + e)
    oh1 = (lane == i1).astype(F32)
    oh2 = (lane == i2).astype(F32)
    both = oh1 + oh2
    before = jnp.dot(tril_ref[...], both.astype(BF16), preferred_element_type=F32) + run_ref[...]
    r1 = jnp.sum(before * oh1, axis=1, keepdims=True).astype(jnp.int32)
    r2 = jnp.sum(before * oh2, axis=1, keepdims=True).astype(jnp.int32)
    run_ref[...] += jnp.sum(both, axis=0, keepdims=True)
    ri_ref[...] = jnp.where(lane == 0, i1, jnp.where(lane == 1, i2, jnp.where(lane == 2, r1, r2)))
    rg_ref[...] = jnp.where(lane == 0, g1, g2)
    cnt_ref[...] = run_ref[...]


SC_WINDOW = 64


def _sc_workers():
    info = plsc.get_sparse_core_info()
    mesh = plsc.VectorSubcoreMesh(core_axis_name="core", subcore_axis_name="subcore")
    return mesh, info.num_cores, info.num_cores * info.num_subcores


def _dispatch(x, dest, cap):
    n, d = x.shape
    mesh, n_cores, n_workers = _sc_workers()
    per_worker = n // n_workers

    @functools.partial(
        pl.kernel, mesh=mesh, out_type=jax.ShapeDtypeStruct((cap, d), x.dtype),
        scratch_types=[pltpu.VMEM((SC_WINDOW,), jnp.int32)] * TOP_K
        + [pltpu.VMEM((SC_WINDOW, d), x.dtype), pltpu.SemaphoreType.DMA],
        name="moe_dispatch")
    def scatter(x_hbm, *rest):
        idx_hbm, (xs_hbm, *idx_vmem, rows, sem) = rest[:TOP_K], rest[TOP_K:]
        wid = lax.axis_index("subcore") * n_cores + lax.axis_index("core")

        @pl.loop(0, per_worker // SC_WINDOW)
        def _(c):
            base = wid * per_worker + c * SC_WINDOW
            for k in range(TOP_K):
                pltpu.sync_copy(idx_hbm[k].at[pl.ds(base, SC_WINDOW)], idx_vmem[k])
            pltpu.sync_copy(x_hbm.at[pl.ds(base, SC_WINDOW)], rows)
            for k in range(TOP_K):
                pltpu.async_copy(rows, xs_hbm.at[idx_vmem[k]], sem).wait()

    return scatter(x, *[dest[k] for k in range(TOP_K)])


def _gather_rows(table, idx):
    m = idx.shape[0]
    d = table.shape[1]
    mesh, n_cores, n_workers = _sc_workers()
    per_worker = m // n_workers

    @functools.partial(
        pl.kernel, mesh=mesh, out_type=jax.ShapeDtypeStruct((m, d), table.dtype),
        scratch_types=[pltpu.VMEM((SC_WINDOW,), jnp.int32), pltpu.VMEM((SC_WINDOW, d), table.dtype),
                       pltpu.SemaphoreType.DMA],
        name="moe_gather")
    def gather(t_hbm, i_hbm, o_hbm, idx_vmem, rows, sem):
        wid = lax.axis_index("subcore") * n_cores + lax.axis_index("core")

        @pl.loop(0, per_worker // SC_WINDOW)
        def _(c):
            base = wid * per_worker + c * SC_WINDOW
            pltpu.sync_copy(i_hbm.at[pl.ds(base, SC_WINDOW)], idx_vmem)
            pltpu.async_copy(t_hbm.at[idx_vmem], rows, sem).wait()
            pltpu.sync_copy(rows, o_hbm.at[pl.ds(base, SC_WINDOW)])

    return gather(table, idx)


def _expert_kernel(be_ref, nv_ref, xs_ref, wg_hbm, wu_hbm, wd_hbm, ys_ref, wg_buf, wu_buf, wd_buf, sems,
                   *, layer, tf):
    b = pl.program_id(0)
    n_valid = nv_ref[0]
    nj = wg_hbm.shape[3] // tf

    def tile_copies(expert, j, slot):
        cols = pl.ds(j * tf, tf)
        return (
            pltpu.make_async_copy(wg_hbm.at[layer, expert, :, cols], wg_buf.at[slot], sems.at[slot, 0]),
            pltpu.make_async_copy(wu_hbm.at[layer, expert, :, cols], wu_buf.at[slot], sems.at[slot, 1]),
            pltpu.make_async_copy(wd_hbm.at[layer, expert, cols, :], wd_buf.at[slot], sems.at[slot, 2]),
        )

    def start(expert, j, slot):
        for cp in tile_copies(expert, j, slot):
            cp.start()

    @pl.when(b == 0)
    def _():
        start(be_ref[0], 0, 0)

    @pl.when(b >= n_valid)
    def _():
        ys_ref[...] = jnp.zeros_like(ys_ref)

    @pl.when(b < n_valid)
    def _():
        expert = be_ref[b]
        parity = (b * nj) % 2
        x = _unpack_bf16_pairs(xs_ref[...])
        acc = None
        for j in range(nj):
            slot = (parity + j) % 2
            if j + 1 < nj:
                start(expert, j + 1, 1 - slot)
            else:
                @pl.when(b + 1 < n_valid)
                def _(slot=slot):
                    start(be_ref[b + 1], 0, 1 - slot)
            for cp in tile_copies(expert, j, slot):
                cp.wait()
            acts = []
            for c in range(tf // MXU_DIM):
                cols = slice(c * MXU_DIM, (c + 1) * MXU_DIM)
                gate = jnp.dot(x, wg_buf[slot, :, cols].astype(BF16), preferred_element_type=F32)
                up = jnp.dot(x, wu_buf[slot, :, cols].astype(BF16), preferred_element_type=F32)
                acts.append((_silu(gate) * up).astype(BF16))
            part = jnp.dot(jnp.concatenate(acts, axis=1), wd_buf[slot].astype(BF16), preferred_element_type=F32)
            acc = part if acc is None else acc + part
        ys_ref[...] = acc


def _experts(xs, block_e, n_valid, w_gate, w_up, w_down, layer, tm, tf):
    cap = xs.shape[0]
    d = 2 * xs.shape[1]
    n_slots = 2
    hbm = pl.BlockSpec(memory_space=pl.ANY)
    return pl.pallas_call(
        functools.partial(_expert_kernel, layer=layer, tf=tf),
        grid_spec=pltpu.PrefetchScalarGridSpec(
            num_scalar_prefetch=2,
            grid=(cap // tm,),
            in_specs=[
                pl.BlockSpec((tm, d // 2), lambda b, be, nv: (jnp.minimum(b, nv[0] - 1), 0)),
                hbm, hbm, hbm,
            ],
            out_specs=pl.BlockSpec((tm, d), lambda b, be, nv: (b, 0)),
            scratch_shapes=[
                pltpu.VMEM((n_slots, d, tf), w_gate.dtype),
                pltpu.VMEM((n_slots, d, tf), w_up.dtype),
                pltpu.VMEM((n_slots, tf, d), w_down.dtype),
                pltpu.SemaphoreType.DMA((n_slots, 3)),
            ],
        ),
        out_shape=jax.ShapeDtypeStruct((cap, d), F32),
        compiler_params=_params("arbitrary"),
        name="moe_experts",
    )(block_e, n_valid, xs, w_gate, w_up, w_down)


def _combine_kernel(x_ref, rg_ref, gf_ref, y0_ref, y1_ref, o_ref, *, final_norm):
    rg = rg_ref[...]
    out = x_ref[...] + (rg[:, 0:1] * y0_ref[...] + rg[:, 1:2] * y1_ref[...])
    if final_norm:
        out = _rms(out, gf_ref[...])
    o_ref[...] = out


def _combine(x, route_g, yg, final_gain, final_norm):
    n, d = x.shape
    tm = 512
    return pl.pallas_call(
        functools.partial(_combine_kernel, final_norm=final_norm),
        grid=(n // tm,),
        in_specs=[
            pl.BlockSpec((tm, d), lambda i: (i, 0)),
            pl.BlockSpec((tm, ROUTE_COLS), lambda i: (i, 0)),
            pl.BlockSpec((1, d), lambda i: (0, 0)),
            pl.BlockSpec((None, tm, d), lambda i: (0, i, 0)),
            pl.BlockSpec((None, tm, d), lambda i: (1, i, 0)),
        ],
        out_specs=pl.BlockSpec((tm, d), lambda i: (i, 0)),
        out_shape=jax.ShapeDtypeStruct((n, d), F32),
        compiler_params=_params("parallel"),
        name="moe_combine",
    )(x, route_g, final_gain, yg, yg)


def _moe_layer(x, routing, w_gate, w_up, w_down, layer, final_gain, final_norm):
    n, d = x.shape
    tm = 1024 if n * TOP_K >= 8 * 1024 else 256
    tf = 2 * MXU_DIM
    route_i, route_g, counts, h_packed = routing

    counts = counts[0].astype(jnp.int32)
    padded = ((counts + tm - 1) // tm) * tm
    ends = jnp.cumsum(padded)
    starts = ends - padded
    dest = (starts[route_i[:, :TOP_K]] + route_i[:, TOP_K:2 * TOP_K]).T
    n_blocks = (n * TOP_K) // tm + N_EXPERTS
    block_e = jnp.minimum(
        jnp.searchsorted(ends, jnp.arange(n_blocks, dtype=jnp.int32) * tm, side="right"),
        N_EXPERTS - 1).astype(jnp.int32)
    n_valid = (ends[-1:] // tm).astype(jnp.int32)

    xs = _dispatch(h_packed, dest, n_blocks * tm)
    ys = _experts(xs, block_e, n_valid, w_gate, w_up, w_down, layer, tm, tf)
    yg = _gather_rows(ys, dest.reshape(TOP_K * n)).reshape(TOP_K, n, d)
    return _combine(x, route_g, yg, final_gain, final_norm)


def _rope_tables(seq):
    pos = jnp.arange(seq, dtype=F32)
    inv_freq = ROPE_THETA ** (-jnp.arange(0, HEAD_DIM, 2, dtype=F32) / HEAD_DIM)
    ang = pos[:, None] * inv_freq[None, :]
    reps = LANES // (HEAD_DIM // 2)
    cos = jnp.tile(jnp.cos(ang), (1, reps))
    sin = jnp.tile(jnp.sin(ang), (1, reps))
    first_half = (jnp.arange(LANES) % HEAD_DIM) < HEAD_DIM // 2
    tabs = (cos, jnp.where(first_half, -sin, sin))
    q_scale = HEAD_DIM ** -0.5 * 1.4426950408889634
    return tuple(jnp.stack([t * q_scale, t]) for t in tabs)


def kernel(x, mix_norm, ffn_norm, attn_w_qkv, attn_w_o, pool_w, pool_scale, dense_w_gate, dense_w_up,
           dense_w_down, moe_router, moe_w_gate, moe_w_up, moe_w_down, final_norm):
    batch, seq, d = x.shape
    depth = mix_norm.shape[0]
    assert depth % 2 == 0 and d == N_HEADS * HEAD_DIM
    rope_tables = _rope_tables(seq)
    xf = x.reshape(batch * seq, d)
    final_gain = final_norm.reshape(1, d)
    for i in range(depth):
        j = i // 2
        mix_gain = mix_norm[i].reshape(1, d)
        ffn_gain = ffn_norm[i].reshape(1, d)
        if i % 2 == 0:
            w_qkv = attn_w_qkv[j].astype(BF16)
            o = _attention(_qkv_project(xf, mix_gain, w_qkv, rope_tables, batch, seq), batch, seq)
            xf = _proj_dense_ffn(xf, o, attn_w_o[j].astype(BF16), ffn_gain, dense_w_gate[j].astype(BF16),
                                 dense_w_up[j].astype(BF16), dense_w_down[j].astype(BF16))
        else:
            xf, *routing = _pool_mixer_and_router(xf, mix_gain, pool_w[j].astype(BF16), pool_scale[j].reshape(1, d),
                                                  ffn_gain, moe_router[j], batch, seq)
            xf = _moe_layer(xf, routing, moe_w_gate, moe_w_up, moe_w_down, j, final_gain, i == depth - 1)
    return xf.reshape(batch, seq, d)
```

```python
import functools

import jax
import jax.numpy as jnp
from jax import lax
from jax.experimental import pallas as pl
from jax.experimental.pallas import tpu as pltpu
from jax.experimental.pallas import tpu_sc as plsc

WINDOWS = (128, 512, 2048)
DILATIONS = (1, 4, 16)
N_GROUPS = len(WINDOWS)
N_HEADS = 16
HEAD_DIM = 64
ROPE_THETA = 10000.0
POOL_SIZES = (2, 4, 8, 16)
N_EXPERTS = 8
TOP_K = 2
RMS_EPS = 1e-6
MASK_VALUE = -1e30
QKV_ROWS = 512

LANES = 128
MXU_DIM = 256
VMEM_LIMIT_BYTES = 56 * 1024 * 1024

F32 = jnp.float32
BF16 = jnp.bfloat16


def _params(*sem):
    return pltpu.CompilerParams(dimension_semantics=sem, vmem_limit_bytes=VMEM_LIMIT_BYTES)


def _rms(x, g):
    ms = jnp.mean(x * x, axis=-1, keepdims=True)
    return x * lax.rsqrt(ms + RMS_EPS) * g


def _silu(g):
    return g / (1.0 + jnp.exp(-g))


def _qkv_kernel(x_ref, g_ref, w_ref, cos_ref, sin_ref, o0_ref, o1_ref, o2_ref, acc_ref):
    tm, d = x_ref.shape
    n_chunks = acc_ref.shape[1]
    lane = lax.broadcasted_iota(jnp.int32, (1, LANES), 1)
    first_half = (lane % HEAD_DIM) < HEAD_DIM // 2
    h = _rms(x_ref[...], g_ref[...]).astype(BF16)
    for g, (o_ref, dil) in enumerate(zip((o0_ref, o1_ref, o2_ref), DILATIONS)):
        rows = tm // dil

        def strided(ref, r, dil=dil, rows=rows):
            if dil == 1:
                return ref[...]
            return ref[pl.ds(r, rows, stride=dil), :]

        for kind in range(3):
            col0 = (g * 3 + kind) * d
            acc = jnp.dot(h, w_ref[:, col0:col0 + d], preferred_element_type=F32)
            for c in range(n_chunks):
                acc_ref[kind, c] = acc[:, c * LANES:(c + 1) * LANES]
            for r in range(dil):
                if kind < 2:
                    cos = strided(cos_ref.at[kind], r)
                    sin = strided(sin_ref.at[kind], r)
                for c in range(n_chunks):
                    t = strided(acc_ref.at[kind, c], r)
                    if kind < 2:
                        partner = jnp.where(first_half, pltpu.roll(t, LANES - HEAD_DIM // 2, 1),
                                            pltpu.roll(t, HEAD_DIM // 2, 1))
                        t = t * cos + partner * sin
                    o_ref[kind, r, :, c * LANES:(c + 1) * LANES] = t.astype(BF16)


def _qkv_project(x, gain, w_qkv, tables, batch, seq):
    n, d = x.shape
    tm = QKV_ROWS
    tiles_per_seq = seq // tm
    tab = pl.BlockSpec((2, tm, LANES), lambda i: (0, i % tiles_per_seq, 0))

    def o_spec(dil):
        return pl.BlockSpec((3, None, dil, tm // dil, d),
                            lambda i: (0, i // tiles_per_seq, 0, i % tiles_per_seq, 0))

    outs = pl.pallas_call(
        _qkv_kernel,
        grid=(n // tm,),
        in_specs=[
            pl.BlockSpec((tm, d), lambda i: (i, 0)),
            pl.BlockSpec((1, d), lambda i: (0, 0)),
            pl.BlockSpec(w_qkv.shape, lambda i: (0, 0), pipeline_mode=pl.Buffered(1)),
            tab, tab,
        ],
        out_specs=[o_spec(dil) for dil in DILATIONS],
        out_shape=[jax.ShapeDtypeStruct((3, batch, dil, seq // dil, d), BF16) for dil in DILATIONS],
        scratch_shapes=[pltpu.VMEM((3, d // LANES, tm, LANES), F32)],
        compiler_params=_params("parallel"),
        name="qkv_rope",
    )(x, gain, w_qkv, *tables)
    return [o.reshape(3, batch, seq, d) for o in outs]


def _attn_kernel(q0, k0, v0, q1, k1, v1, q2, k2, v2, band_ref, causal_ref, o_ref,
                 num_ref, max_ref, den_ref, kt_ref):
    seq = o_ref.shape[0]
    w = WINDOWS[0] // DILATIONS[0]
    lane = lax.broadcasted_iota(jnp.int32, (1, LANES), 1)
    first_head = lane < HEAD_DIM

    def tile(q_ref, v_ref, qoff, has_prev):
        q = q_ref[qoff:qoff + w, :]
        zero = jnp.zeros_like(q)
        qq = jnp.concatenate([jnp.where(first_head, q, zero), jnp.where(first_head, zero, q)], axis=0)
        k_lo = qoff - w if has_prev else qoff
        s = jnp.dot(qq, kt_ref[:, k_lo:qoff + w], preferred_element_type=F32)
        s = s + (band_ref[...] if has_prev else causal_ref[...])
        mx = jnp.max(s, axis=1, keepdims=True)
        p = jnp.exp2(s - mx)
        den = jnp.sum(p, axis=1, keepdims=True)
        pv = jnp.dot(p.astype(BF16), v_ref[k_lo:qoff + w, :], preferred_element_type=F32)
        return tuple(jnp.where(first_head, a[:w], a[w:]) for a in (pv, mx, den))

    order = sorted(range(N_GROUPS), key=lambda g: -DILATIONS[g])
    for g in order:
        q_ref, k_ref, v_ref = ((q0, k0, v0), (q1, k1, v1), (q2, k2, v2))[g]
        dil = DILATIONS[g]
        sub_len = seq // dil
        nb = sub_len // w
        kt_ref[...] = k_ref[...].T
        for t in range(dil * nb):
            r, n = divmod(t, nb)
            qoff = r * sub_len + n * w
            pv, mx, den = tile(q_ref, v_ref, qoff, n > 0)
            rows = pl.ds(qoff, w) if dil == 1 else pl.ds(r + dil * n * w, w, stride=dil)
            if g == order[0]:
                num_ref[rows, :] = pv
                max_ref[rows, :] = mx
                den_ref[rows, :] = den
                continue
            old_max = max_ref[rows, :]
            new_max = jnp.maximum(old_max, mx)
            a = jnp.exp2(old_max - new_max)
            b = jnp.exp2(mx - new_max)
            num = a * num_ref[rows, :] + b * pv
            den = a * den_ref[rows, :] + b * den
            if g == order[-1]:
                o_ref[rows, :] = (num / den).astype(BF16)
            else:
                num_ref[rows, :] = num
                den_ref[rows, :] = den
                max_ref[rows, :] = new_max


def _attention_biases(w):
    row = jnp.arange(2 * w)[:, None] % w
    col = jnp.arange(2 * w)[None, :]
    dist = row + w - col
    band = jnp.where((dist >= 0) & (dist <= w), 0.0, MASK_VALUE).astype(F32)
    causal = jnp.where(row >= col[:, :w], 0.0, MASK_VALUE).astype(F32)
    return band, causal


def _attention(qkv_groups, batch, seq):
    d_model = N_HEADS * HEAD_DIM
    in_specs, args = [], []
    for arr in qkv_groups:
        for kind in range(3):
            in_specs.append(pl.BlockSpec((None, None, seq, LANES), lambda b, hp, kind=kind: (kind, b, 0, hp)))
            args.append(arr)
    for bias in _attention_biases(WINDOWS[0] // DILATIONS[0]):
        in_specs.append(pl.BlockSpec(bias.shape, lambda b, hp: (0, 0)))
        args.append(bias)
    return pl.pallas_call(
        _attn_kernel,
        grid=(batch, d_model // LANES),
        in_specs=in_specs,
        out_specs=pl.BlockSpec((None, seq, LANES), lambda b, hp: (b, 0, hp)),
        out_shape=jax.ShapeDtypeStruct((batch, seq, d_model), BF16),
        scratch_shapes=[pltpu.VMEM((seq, LANES), F32)] * 3 + [pltpu.VMEM((LANES, seq), BF16)],
        compiler_params=_params("parallel", "parallel"),
        name="attention",
    )(*args).reshape(batch * seq, d_model)


def _proj_dense_kernel(x_ref, o_ref, wo_ref, g_ref, wg_ref, wu_ref, wd_ref, y_ref):
    x = x_ref[...] + jnp.dot(o_ref[...], wo_ref[...], preferred_element_type=F32)
    h = _rms(x, g_ref[...]).astype(BF16)
    gate = jnp.dot(h, wg_ref[...], preferred_element_type=F32)
    up = jnp.dot(h, wu_ref[...], preferred_element_type=F32)
    a = (_silu(gate) * up).astype(BF16)
    y_ref[...] = x + jnp.dot(a, wd_ref[...], preferred_element_type=F32)


def _proj_dense_ffn(x, o, w_o, gain, w_gate, w_up, w_down):
    n, d = x.shape
    ff = w_gate.shape[1]
    tm = 512
    row = pl.BlockSpec((tm, d), lambda i: (i, 0))
    const = lambda shape: pl.BlockSpec(shape, lambda i: (0, 0), pipeline_mode=pl.Buffered(1))
    return pl.pallas_call(
        _proj_dense_kernel,
        grid=(n // tm,),
        in_specs=[row, row, const((d, d)), const((1, d)), const((d, ff)), const((d, ff)), const((ff, d))],
        out_specs=row,
        out_shape=jax.ShapeDtypeStruct((n, d), F32),
        compiler_params=_params("parallel"),
        name="proj_dense_ffn",
    )(x, o, w_o, gain, w_gate, w_up, w_down)


POOL_HALO = max(POOL_SIZES)


def _pool_tile(i, x_ref, halo_ref, g_ref, w_ref, sc_ref, y_ref):
    ts = x_ref.shape[0]
    g = g_ref[...]
    x = x_ref[...]
    h = _rms(x, g)
    hh = jnp.where(i > 0, _rms(halo_ref[...], g), 0.0)
    ext = jnp.concatenate([hh, h], axis=0)
    pos = i * ts + lax.broadcasted_iota(jnp.int32, (ts, 1), 0)
    pc = w_ref.shape[1]
    for grp, size in enumerate(POOL_SIZES):
        sl = slice(grp * pc, (grp + 1) * pc)
        s = ext[:, sl]
        step = 1
        while step < size:
            s = s + pltpu.roll(s, step, 0)
            step *= 2
        cnt = jnp.minimum(pos + 1, size).astype(F32)
        y = s[POOL_HALO:, :] / cnt - h[:, sl]
        z = jnp.dot(y.astype(BF16), w_ref[grp], preferred_element_type=F32)
        y_ref[:, sl] = x[:, sl] + z * sc_ref[:, sl]


def _pool_router_kernel(x_ref, halo_ref, g_ref, w_ref, sc_ref, g2_ref, wr_ref, tril_ref,
                        y_ref, ri_ref, rg_ref, cnt_ref, hp_ref, run_ref):
    b = pl.program_id(0)
    i = pl.program_id(1)
    _pool_tile(i, x_ref, halo_ref, g_ref, w_ref, sc_ref, y_ref)
    _route_tile((b == 0) & (i == 0), y_ref[...], g2_ref, wr_ref, tril_ref, ri_ref, rg_ref, cnt_ref, hp_ref, run_ref)


def _pool_mixer_and_router(x, gain, pool_w, scale, ffn_gain, w_router, batch, seq):
    n, d = x.shape
    ne = w_router.shape[1]
    ts = 512
    tiles = seq // ts
    x3 = x.reshape(batch, seq, d)
    hb = ts // POOL_HALO
    tril = (jnp.arange(ts)[:, None] > jnp.arange(ts)[None, :]).astype(BF16)
    const = lambda shape: pl.BlockSpec(shape, lambda b, i: (0,) * len(shape))
    tok = lambda cols: pl.BlockSpec((ts, cols), lambda b, i: (b * tiles + i, 0))
    out, route_i, route_g, counts, h_packed = pl.pallas_call(
        _pool_router_kernel,
        grid=(batch, tiles),
        in_specs=[
            pl.BlockSpec((None, ts, d), lambda b, i: (b, i, 0)),
            pl.BlockSpec((None, POOL_HALO, d), lambda b, i: (b, jnp.maximum(i * hb - 1, 0), 0)),
            const((1, d)), const(pool_w.shape), const((1, d)), const((1, d)), const((d, ne)), const((ts, ts)),
        ],
        out_specs=[
            pl.BlockSpec((None, ts, d), lambda b, i: (b, i, 0)),
            tok(ROUTE_COLS), tok(ROUTE_COLS), const((1, ne)), tok(d // 2),
        ],
        out_shape=[
            jax.ShapeDtypeStruct((batch, seq, d), F32),
            jax.ShapeDtypeStruct((n, ROUTE_COLS), jnp.int32),
            jax.ShapeDtypeStruct((n, ROUTE_COLS), F32),
            jax.ShapeDtypeStruct((1, ne), F32),
            jax.ShapeDtypeStruct((n, d // 2), jnp.int32),
        ],
        scratch_shapes=[pltpu.VMEM((1, ne), F32)],
        compiler_params=_params("arbitrary", "arbitrary"),
        name="pool_router",
    )(x3, x3, gain, pool_w, scale, ffn_gain, w_router, tril)
    return out.reshape(n, d), route_i, route_g, counts, h_packed


ROUTE_COLS = 8


def _pack_bf16_pairs(hr):
    c = hr.shape[1] // 2
    bits = pltpu.bitcast(hr, jnp.int32)
    return (bits[:, c:] & jnp.int32(-65536)) | lax.shift_right_logical(bits[:, :c], 16)


def _unpack_bf16_pairs(packed):
    lo = pltpu.bitcast(lax.shift_left(packed, 16), F32)
    hi = pltpu.bitcast(packed & jnp.int32(-65536), F32)
    return jnp.concatenate([lo.astype(BF16), hi.astype(BF16)], axis=1)


def _route_tile(first, x, g_ref, wr_ref, tril_ref, ri_ref, rg_ref, cnt_ref, hp_ref, run_ref):
    @pl.when(first)
    def _():
        run_ref[...] = jnp.zeros_like(run_ref)

    h = _rms(x, g_ref[...])
    wr = wr_ref[...]
    ne = wr.shape[1]
    h_hi = h.astype(BF16)
    hp_ref[...] = _pack_bf16_pairs(h_hi.astype(F32))
    h_lo = (h - h_hi.astype(F32)).astype(BF16)
    w_hi = wr.astype(BF16)
    w_lo = (wr - w_hi.astype(F32)).astype(BF16)
    hi_part = jnp.dot(h_hi, jnp.concatenate([w_hi, w_lo], axis=1), preferred_element_type=F32)
    logits = hi_part[:, :ne] + hi_part[:, ne:] + jnp.dot(h_lo, w_hi, preferred_element_type=F32)
    tm = logits.shape[0]
    lane = lax.broadcasted_iota(jnp.int32, (tm, ne), 1)
    v1 = jnp.max(logits, axis=1, keepdims=True)
    i1 = jnp.min(jnp.where(logits == v1, lane, ne), axis=1, keepdims=True)
    rest = jnp.where(lane == i1, -jnp.inf, logits)
    v2 = jnp.max(rest, axis=1, keepdims=True)
    i2 = jnp.min(jnp.where(rest == v2, lane, ne), axis=1, keepdims=True)
    e = jnp.exp(v2 - v1)
    g1 = 1.0 / (1.0 + e)
    g2 = e / (1.0 + e)
    oh1 = (lane == i1).astype(F32)
    oh2 = (lane == i2).astype(F32)
    both = oh1 + oh2
    before = jnp.dot(tril_ref[...], both.astype(BF16), preferred_element_type=F32) + run_ref[...]
    r1 = jnp.sum(before * oh1, axis=1, keepdims=True).astype(jnp.int32)
    r2 = jnp.sum(before * oh2, axis=1, keepdims=True).astype(jnp.int32)
    run_ref[...] += jnp.sum(both, axis=0, keepdims=True)
    ri_ref[...] = jnp.where(lane == 0, i1, jnp.where(lane == 1, i2, jnp.where(lane == 2, r1, r2)))
    rg_ref[...] = jnp.where(lane == 0, g1, g2)
    cnt_ref[...] = run_ref[...]


SC_WINDOW = 64


def _sc_workers():
    info = plsc.get_sparse_core_info()
    mesh = plsc.VectorSubcoreMesh(core_axis_name="core", subcore_axis_name="subcore")
    return mesh, info.num_cores, info.num_cores * info.num_subcores


def _dispatch(x, dest, cap):
    n, d = x.shape
    mesh, n_cores, n_workers = _sc_workers()
    per_worker = n // n_workers

    @functools.partial(
        pl.kernel, mesh=mesh, out_type=jax.ShapeDtypeStruct((cap, d), x.dtype),
        scratch_types=[pltpu.VMEM((SC_WINDOW,), jnp.int32)] * TOP_K
        + [pltpu.VMEM((SC_WINDOW, d), x.dtype), pltpu.SemaphoreType.DMA],
        name="moe_dispatch")
    def scatter(x_hbm, *rest):
        idx_hbm, (xs_hbm, *idx_vmem, rows, sem) = rest[:TOP_K], rest[TOP_K:]
        wid = lax.axis_index("subcore") * n_cores + lax.axis_index("core")

        @pl.loop(0, per_worker // SC_WINDOW)
        def _(c):
            base = wid * per_worker + c * SC_WINDOW
            for k in range(TOP_K):
                pltpu.sync_copy(idx_hbm[k].at[pl.ds(base, SC_WINDOW)], idx_vmem[k])
            pltpu.sync_copy(x_hbm.at[pl.ds(base, SC_WINDOW)], rows)
            for k in range(TOP_K):
                pltpu.async_copy(rows, xs_hbm.at[idx_vmem[k]], sem).wait()

    return scatter(x, *[dest[k] for k in range(TOP_K)])


def _gather_rows(table, idx):
    m = idx.shape[0]
    d = table.shape[1]
    mesh, n_cores, n_workers = _sc_workers()
    per_worker = m // n_workers

    @functools.partial(
        pl.kernel, mesh=mesh, out_type=jax.ShapeDtypeStruct((m, d), table.dtype),
        scratch_types=[pltpu.VMEM((SC_WINDOW,), jnp.int32), pltpu.VMEM((SC_WINDOW, d), table.dtype),
                       pltpu.SemaphoreType.DMA],
        name="moe_gather")
    def gather(t_hbm, i_hbm, o_hbm, idx_vmem, rows, sem):
        wid = lax.axis_index("subcore") * n_cores + lax.axis_index("core")

        @pl.loop(0, per_worker // SC_WINDOW)
        def _(c):
            base = wid * per_worker + c * SC_WINDOW
            pltpu.sync_copy(i_hbm.at[pl.ds(base, SC_WINDOW)], idx_vmem)
            pltpu.async_copy(t_hbm.at[idx_vmem], rows, sem).wait()
            pltpu.sync_copy(rows, o_hbm.at[pl.ds(base, SC_WINDOW)])

    return gather(table, idx)


def _expert_kernel(be_ref, nv_ref, xs_ref, wg_ref, wu_ref, wd_ref, ys_ref, xb_ref):
    del be_ref
    b = pl.program_id(0)
    j = pl.program_id(1)

    @pl.when(j == 0)
    def _():
        ys_ref[...] = jnp.zeros_like(ys_ref)

    @pl.when(b < nv_ref[0])
    def _():
        @pl.when(j == 0)
        def _():
            xb_ref[...] = _unpack_bf16_pairs(xs_ref[...])

        x = xb_ref[...]
        acts = []
        for c in range(wg_ref.shape[1] // MXU_DIM):
            cols = slice(c * MXU_DIM, (c + 1) * MXU_DIM)
            gate = jnp.dot(x, wg_ref[:, cols].astype(BF16), preferred_element_type=F32)
            up = jnp.dot(x, wu_ref[:, cols].astype(BF16), preferred_element_type=F32)
            acts.append((_silu(gate) * up).astype(BF16))
        a = jnp.concatenate(acts, axis=1)
        ys_ref[...] += jnp.dot(a, wd_ref[...].astype(BF16), preferred_element_type=F32)


def _experts(xs, block_e, n_valid, w_gate, w_up, w_down, layer, tm, tf):
    cap = xs.shape[0]
    d = 2 * xs.shape[1]
    ff = w_gate.shape[3]
    nj = ff // tf

    def blk(b, nv):
        return jnp.minimum(b, nv[0] - 1)

    def ffi(b, j, nv):
        return jnp.where(b < nv[0], j, nj - 1)

    return pl.pallas_call(
        _expert_kernel,
        grid_spec=pltpu.PrefetchScalarGridSpec(
            num_scalar_prefetch=2,
            grid=(cap // tm, nj),
            in_specs=[
                pl.BlockSpec((tm, d // 2), lambda b, j, be, nv: (blk(b, nv), 0)),
                pl.BlockSpec((None, None, d, tf), lambda b, j, be, nv: (layer, be[blk(b, nv)], 0, ffi(b, j, nv))),
                pl.BlockSpec((None, None, d, tf), lambda b, j, be, nv: (layer, be[blk(b, nv)], 0, ffi(b, j, nv))),
                pl.BlockSpec((None, None, tf, d), lambda b, j, be, nv: (layer, be[blk(b, nv)], ffi(b, j, nv), 0)),
            ],
            out_specs=pl.BlockSpec((tm, d), lambda b, j, be, nv: (b, 0)),
            scratch_shapes=[pltpu.VMEM((tm, d), BF16)],
        ),
        out_shape=jax.ShapeDtypeStruct((cap, d), F32),
        compiler_params=_params("arbitrary", "arbitrary"),
        name="moe_experts",
    )(block_e, n_valid, xs, w_gate, w_up, w_down)


def _combine_kernel(x_ref, rg_ref, gf_ref, y0_ref, y1_ref, o_ref, *, final_norm):
    rg = rg_ref[...]
    out = x_ref[...] + (rg[:, 0:1] * y0_ref[...] + rg[:, 1:2] * y1_ref[...])
    if final_norm:
        out = _rms(out, gf_ref[...])
    o_ref[...] = out


def _combine(x, route_g, yg, final_gain, final_norm):
    n, d = x.shape
    tm = 512
    return pl.pallas_call(
        functools.partial(_combine_kernel, final_norm=final_norm),
        grid=(n // tm,),
        in_specs=[
            pl.BlockSpec((tm, d), lambda i: (i, 0)),
            pl.BlockSpec((tm, ROUTE_COLS), lambda i: (i, 0)),
            pl.BlockSpec((1, d), lambda i: (0, 0)),
            pl.BlockSpec((None, tm, d), lambda i: (0, i, 0)),
            pl.BlockSpec((None, tm, d), lambda i: (1, i, 0)),
        ],
        out_specs=pl.BlockSpec((tm, d), lambda i: (i, 0)),
        out_shape=jax.ShapeDtypeStruct((n, d), F32),
        compiler_params=_params("parallel"),
        name="moe_combine",
    )(x, route_g, final_gain, yg, yg)


def _moe_layer(x, routing, w_gate, w_up, w_down, layer, final_gain, final_norm):
    n, d = x.shape
    tm = 1024 if n * TOP_K >= 8 * 1024 else 256
    tf = 2 * MXU_DIM
    route_i, route_g, counts, h_packed = routing

    counts = counts[0].astype(jnp.int32)
    padded = ((counts + tm - 1) // tm) * tm
    ends = jnp.cumsum(padded)
    starts = ends - padded
    dest = (starts[route_i[:, :TOP_K]] + route_i[:, TOP_K:2 * TOP_K]).T
    n_blocks = (n * TOP_K) // tm + N_EXPERTS
    block_e = jnp.minimum(
        jnp.searchsorted(ends, jnp.arange(n_blocks, dtype=jnp.int32) * tm, side="right"),
        N_EXPERTS - 1).astype(jnp.int32)
    n_valid = (ends[-1:] // tm).astype(jnp.int32)

    xs = _dispatch(h_packed, dest, n_blocks * tm)
    ys = _experts(xs, block_e, n_valid, w_gate, w_up, w_down, layer, tm, tf)
    yg = _gather_rows(ys, dest.reshape(TOP_K * n)).reshape(TOP_K, n, d)
    return _combine(x, route_g, yg, final_gain, final_norm)


def _rope_tables(seq):
    pos = jnp.arange(seq, dtype=F32)
    inv_freq = ROPE_THETA ** (-jnp.arange(0, HEAD_DIM, 2, dtype=F32) / HEAD_DIM)
    ang = pos[:, None] * inv_freq[None, :]
    reps = LANES // (HEAD_DIM // 2)
    cos = jnp.tile(jnp.cos(ang), (1, reps))
    sin = jnp.tile(jnp.sin(ang), (1, reps))
    first_half = (jnp.arange(LANES) % HEAD_DIM) < HEAD_DIM // 2
    tabs = (cos, jnp.where(first_half, -sin, sin))
    q_scale = HEAD_DIM ** -0.5 * 1.4426950408889634
    return tuple(jnp.stack([t * q_scale, t]) for t in tabs)


def kernel(x, mix_norm, ffn_norm, attn_w_qkv, attn_w_o, pool_w, pool_scale, dense_w_gate, dense_w_up,
           dense_w_down, moe_router, moe_w_gate, moe_w_up, moe_w_down, final_norm):
    batch, seq, d = x.shape
    depth = mix_norm.shape[0]
    assert depth % 2 == 0 and d == N_HEADS * HEAD_DIM
    rope_tables = _rope_tables(seq)
    xf = x.reshape(batch * seq, d)
    final_gain = final_norm.reshape(1, d)
    for i in range(depth):
        j = i // 2
        mix_gain = mix_norm[i].reshape(1, d)
        ffn_gain = ffn_norm[i].reshape(1, d)
        if i % 2 == 0:
            w_qkv = attn_w_qkv[j].astype(BF16)
            o = _attention(_qkv_project(xf, mix_gain, w_qkv, rope_tables, batch, seq), batch, seq)
            xf = _proj_dense_ffn(xf, o, attn_w_o[j].astype(BF16), ffn_gain, dense_w_gate[j].astype(BF16),
                                 dense_w_up[j].astype(BF16), dense_w_down[j].astype(BF16))
        else:
            xf, *routing = _pool_mixer_and_router(xf, mix_gain, pool_w[j].astype(BF16), pool_scale[j].reshape(1, d),
                                                  ffn_gain, moe_router[j], batch, seq)
            xf = _moe_layer(xf, routing, moe_w_gate, moe_w_up, moe_w_down, j, final_gain, i == depth - 1)
    return xf.reshape(batch, seq, d)
```

```python
import functools

import jax
import jax.numpy as jnp
from jax import lax
from jax.experimental import pallas as pl
from jax.experimental.pallas import tpu as pltpu
from jax.experimental.pallas import tpu_sc as plsc

WINDOWS = (128, 512, 2048)
DILATIONS = (1, 4, 16)
N_GROUPS = len(WINDOWS)
N_HEADS = 16
HEAD_DIM = 64
ROPE_THETA = 10000.0
POOL_SIZES = (2, 4, 8, 16)
N_EXPERTS = 8
TOP_K = 2
RMS_EPS = 1e-6
MASK_VALUE = -1e30
QKV_ROWS = 512

LANES = 128
MXU_DIM = 256
VMEM_LIMIT_BYTES = 56 * 1024 * 1024

F32 = jnp.float32
BF16 = jnp.bfloat16


def _params(*sem):
    return pltpu.CompilerParams(dimension_semantics=sem, vmem_limit_bytes=VMEM_LIMIT_BYTES)


def _rms(x, g):
    ms = jnp.mean(x * x, axis=-1, keepdims=True)
    return x * lax.rsqrt(ms + RMS_EPS) * g


def _silu(g):
    return g / (1.0 + jnp.exp(-g))


def _qkv_kernel(x_ref, g_ref, w_ref, cos_ref, sin_ref, o0_ref, o1_ref, o2_ref, acc_ref):
    tm, d = x_ref.shape
    n_chunks = acc_ref.shape[1]
    lane = lax.broadcasted_iota(jnp.int32, (1, LANES), 1)
    first_half = (lane % HEAD_DIM) < HEAD_DIM // 2
    h = _rms(x_ref[...], g_ref[...]).astype(BF16)
    for g, (o_ref, dil) in enumerate(zip((o0_ref, o1_ref, o2_ref), DILATIONS)):
        rows = tm // dil

        def strided(ref, r, dil=dil, rows=rows):
            if dil == 1:
                return ref[...]
            return ref[pl.ds(r, rows, stride=dil), :]

        for kind in range(3):
            col0 = (g * 3 + kind) * d
            acc = jnp.dot(h, w_ref[:, col0:col0 + d], preferred_element_type=F32)
            for c in range(n_chunks):
                acc_ref[kind, c] = acc[:, c * LANES:(c + 1) * LANES]
            for r in range(dil):
                if kind < 2:
                    cos = strided(cos_ref.at[kind], r)
                    sin = strided(sin_ref.at[kind], r)
                for c in range(n_chunks):
                    t = strided(acc_ref.at[kind, c], r)
                    if kind < 2:
                        partner = jnp.where(first_half, pltpu.roll(t, LANES - HEAD_DIM // 2, 1),
                                            pltpu.roll(t, HEAD_DIM // 2, 1))
                        t = t * cos + partner * sin
                    o_ref[kind, r, :, c * LANES:(c + 1) * LANES] = t.astype(BF16)


def _qkv_project(x, gain, w_qkv, tables, batch, seq):
    n, d = x.shape
    tm = QKV_ROWS
    tiles_per_seq = seq // tm
    tab = pl.BlockSpec((2, tm, LANES), lambda i: (0, i % tiles_per_seq, 0))

    def o_spec(dil):
        return pl.BlockSpec((3, None, dil, tm // dil, d),
                            lambda i: (0, i // tiles_per_seq, 0, i % tiles_per_seq, 0))

    outs = pl.pallas_call(
        _qkv_kernel,
        grid=(n // tm,),
        in_specs=[
            pl.BlockSpec((tm, d), lambda i: (i, 0)),
            pl.BlockSpec((1, d), lambda i: (0, 0)),
            pl.BlockSpec(w_qkv.shape, lambda i: (0, 0), pipeline_mode=pl.Buffered(1)),
            tab, tab,
        ],
        out_specs=[o_spec(dil) for dil in DILATIONS],
        out_shape=[jax.ShapeDtypeStruct((3, batch, dil, seq // dil, d), BF16) for dil in DILATIONS],
        scratch_shapes=[pltpu.VMEM((3, d // LANES, tm, LANES), F32)],
        compiler_params=_params("parallel"),
        name="qkv_rope",
    )(x, gain, w_qkv, *tables)
    return [o.reshape(3, batch, seq, d) for o in outs]


ATTN_TILES_PER_FFN_PIECE = 3


def _attn_dense_kernel(q0, k0, v0, q1, k1, v1, q2, k2, v2, band_ref, causal_ref,
                       x_ref, wo_ref, g_ref, wg_ref, wu_ref, wd_ref, y_ref,
                       o_scr, num_ref, max_ref, den_ref, kt_ref):
    b = pl.program_id(0)
    hp = pl.program_id(1)
    n_pairs, seq = o_scr.shape[1], o_scr.shape[2]
    w = WINDOWS[0] // DILATIONS[0]
    lane = lax.broadcasted_iota(jnp.int32, (1, LANES), 1)
    first_head = lane < HEAD_DIM
    o_out = o_scr.at[b % 2, hp]
    o_in = o_scr.at[(b + 1) % 2]

    @pl.when((b == 0) & (hp == 0))
    def _():
        o_scr[1] = jnp.zeros(o_scr.shape[1:], o_scr.dtype)

    def ffn_pieces():
        rows = pl.ds(pl.multiple_of(hp * x_ref.shape[0], x_ref.shape[0]), x_ref.shape[0])
        o = jnp.concatenate([o_in[p, rows, :] for p in range(n_pairs)], axis=1)
        x = x_ref[...] + jnp.dot(o, wo_ref[...], preferred_element_type=F32)
        h = _rms(x, g_ref[...]).astype(BF16)
        yield
        acts = []
        n_chunks = wg_ref.shape[1] // MXU_DIM
        for c in range(n_chunks):
            cols = slice(c * MXU_DIM, (c + 1) * MXU_DIM)
            gate = jnp.dot(h, wg_ref[:, cols], preferred_element_type=F32)
            up = jnp.dot(h, wu_ref[:, cols], preferred_element_type=F32)
            acts.append((_silu(gate) * up).astype(BF16))
            yield
        half = n_chunks // 2
        acc = jnp.dot(jnp.concatenate(acts[:half], axis=1), wd_ref[:half * MXU_DIM, :], preferred_element_type=F32)
        yield
        acc = acc + jnp.dot(jnp.concatenate(acts[half:], axis=1), wd_ref[half * MXU_DIM:, :],
                            preferred_element_type=F32)
        y_ref[...] = x + acc
        yield

    def tile(q_ref, v_ref, qoff, has_prev):
        q = q_ref[qoff:qoff + w, :]
        zero = jnp.zeros_like(q)
        qq = jnp.concatenate([jnp.where(first_head, q, zero), jnp.where(first_head, zero, q)], axis=0)
        k_lo = qoff - w if has_prev else qoff
        s = jnp.dot(qq, kt_ref[:, k_lo:qoff + w], preferred_element_type=F32)
        s = s + (band_ref[...] if has_prev else causal_ref[...])
        mx = jnp.max(s, axis=1, keepdims=True)
        p = jnp.exp2(s - mx)
        den = jnp.sum(p, axis=1, keepdims=True)
        pv = jnp.dot(p.astype(BF16), v_ref[k_lo:qoff + w, :], preferred_element_type=F32)
        return tuple(jnp.where(first_head, a[:w], a[w:]) for a in (pv, mx, den))

    pieces = ffn_pieces()
    tiles_done = 0
    order = sorted(range(N_GROUPS), key=lambda g: -DILATIONS[g])
    for g in order:
        q_ref, k_ref, v_ref = ((q0, k0, v0), (q1, k1, v1), (q2, k2, v2))[g]
        dil = DILATIONS[g]
        sub_len = seq // dil
        nb = sub_len // w
        kt_ref[...] = k_ref[...].T
        for t in range(dil * nb):
            r, n = divmod(t, nb)
            qoff = r * sub_len + n * w
            pv, mx, den = tile(q_ref, v_ref, qoff, n > 0)
            tiles_done += 1
            if tiles_done % ATTN_TILES_PER_FFN_PIECE == 0:
                next(pieces, None)
            rows = pl.ds(qoff, w) if dil == 1 else pl.ds(r + dil * n * w, w, stride=dil)
            if g == order[0]:
                num_ref[rows, :] = pv
                max_ref[rows, :] = mx
                den_ref[rows, :] = den
                continue
            old_max = max_ref[rows, :]
            new_max = jnp.maximum(old_max, mx)
            a = jnp.exp2(old_max - new_max)
            c = jnp.exp2(mx - new_max)
            num = a * num_ref[rows, :] + c * pv
            den = a * den_ref[rows, :] + c * den
            if g == order[-1]:
                o_out[rows, :] = (num / den).astype(BF16)
            else:
                num_ref[rows, :] = num
                den_ref[rows, :] = den
                max_ref[rows, :] = new_max
    for _ in pieces:
        pass


def _attention_biases(w):
    row = jnp.arange(2 * w)[:, None] % w
    col = jnp.arange(2 * w)[None, :]
    dist = row + w - col
    band = jnp.where((dist >= 0) & (dist <= w), 0.0, MASK_VALUE).astype(F32)
    causal = jnp.where(row >= col[:, :w], 0.0, MASK_VALUE).astype(F32)
    return band, causal


def _attention_and_dense_ffn(qkv_groups, x, w_o, gain, w_gate, w_up, w_down, batch, seq):
    n, d = x.shape
    ff = w_gate.shape[1]
    n_pairs = d // LANES
    rows = seq // n_pairs
    last = batch - 1
    in_specs, args = [], []
    for arr in qkv_groups:
        for kind in range(3):
            in_specs.append(pl.BlockSpec((None, None, seq, LANES),
                                         lambda b, hp, kind=kind: (kind, jnp.minimum(b, last), 0, hp)))
            args.append(arr)
    const = lambda shape: pl.BlockSpec(shape, lambda b, hp: (0, 0), pipeline_mode=pl.Buffered(1))
    ffn_in = pl.BlockSpec((rows, d), lambda b, hp: (jnp.maximum(b - 1, 0) * n_pairs + hp, 0))
    ffn_out = pl.BlockSpec((rows, d), lambda b, hp: (jnp.where(b == 0, batch, b - 1) * n_pairs + hp, 0))
    biases = _attention_biases(WINDOWS[0] // DILATIONS[0])
    in_specs += [const(t.shape) for t in biases]
    in_specs += [ffn_in, const((d, d)), const((1, d)), const((d, ff)), const((d, ff)), const((ff, d))]
    args += [*biases, x, w_o, gain, w_gate, w_up, w_down]
    return pl.pallas_call(
        _attn_dense_kernel,
        grid=(batch + 1, n_pairs),
        in_specs=in_specs,
        out_specs=ffn_out,
        out_shape=jax.ShapeDtypeStruct((n + seq, d), F32),
        scratch_shapes=[pltpu.VMEM((2, n_pairs, seq, LANES), BF16)]
        + [pltpu.VMEM((seq, LANES), F32)] * 3 + [pltpu.VMEM((LANES, seq), BF16)],
        compiler_params=_params("arbitrary", "arbitrary"),
        name="attention_ffn",
    )(*args)


POOL_HALO = max(POOL_SIZES)


def _pool_tile(i, x_ref, halo_ref, g_ref, w_ref, sc_ref, y_ref):
    ts = x_ref.shape[0]
    g = g_ref[...]
    x = x_ref[...]
    h = _rms(x, g)
    hh = jnp.where(i > 0, _rms(halo_ref[...], g), 0.0)
    ext = jnp.concatenate([hh, h], axis=0)
    pos = i * ts + lax.broadcasted_iota(jnp.int32, (ts, 1), 0)
    pc = w_ref.shape[1]
    for grp, size in enumerate(POOL_SIZES):
        sl = slice(grp * pc, (grp + 1) * pc)
        s = ext[:, sl]
        step = 1
        while step < size:
            s = s + pltpu.roll(s, step, 0)
            step *= 2
        cnt = jnp.minimum(pos + 1, size).astype(F32)
        y = s[POOL_HALO:, :] / cnt - h[:, sl]
        z = jnp.dot(y.astype(BF16), w_ref[grp], preferred_element_type=F32)
        y_ref[:, sl] = x[:, sl] + z * sc_ref[:, sl]


def _pool_router_kernel(x_ref, halo_ref, g_ref, w_ref, sc_ref, g2_ref, wr_ref, tril_ref,
                        y_ref, ri_ref, rg_ref, cnt_ref, hp_ref, run_ref):
    b = pl.program_id(0)
    i = pl.program_id(1)
    _pool_tile(i, x_ref, halo_ref, g_ref, w_ref, sc_ref, y_ref)
    _route_tile((b == 0) & (i == 0), y_ref[...], g2_ref, wr_ref, tril_ref, ri_ref, rg_ref, cnt_ref, hp_ref, run_ref)


def _pool_mixer_and_router(x, gain, pool_w, scale, ffn_gain, w_router, batch, seq):
    d = x.shape[1]
    n = batch * seq
    ne = w_router.shape[1]
    ts = 512
    tiles = seq // ts
    x3 = x.reshape(-1, seq, d)
    hb = ts // POOL_HALO
    tril = (jnp.arange(ts)[:, None] > jnp.arange(ts)[None, :]).astype(BF16)
    const = lambda shape: pl.BlockSpec(shape, lambda b, i: (0,) * len(shape))
    tok = lambda cols: pl.BlockSpec((ts, cols), lambda b, i: (b * tiles + i, 0))
    out, route_i, route_g, counts, h_packed = pl.pallas_call(
        _pool_router_kernel,
        grid=(batch, tiles),
        in_specs=[
            pl.BlockSpec((None, ts, d), lambda b, i: (b, i, 0)),
            pl.BlockSpec((None, POOL_HALO, d), lambda b, i: (b, jnp.maximum(i * hb - 1, 0), 0)),
            const((1, d)), const(pool_w.shape), const((1, d)), const((1, d)), const((d, ne)), const((ts, ts)),
        ],
        out_specs=[
            pl.BlockSpec((None, ts, d), lambda b, i: (b, i, 0)),
            tok(ROUTE_COLS), tok(ROUTE_COLS), const((1, ne)), tok(d // 2),
        ],
        out_shape=[
            jax.ShapeDtypeStruct((batch, seq, d), F32),
            jax.ShapeDtypeStruct((n, ROUTE_COLS), jnp.int32),
            jax.ShapeDtypeStruct((n, ROUTE_COLS), F32),
            jax.ShapeDtypeStruct((1, ne), F32),
            jax.ShapeDtypeStruct((n, d // 2), jnp.int32),
        ],
        scratch_shapes=[pltpu.VMEM((1, ne), F32)],
        compiler_params=_params("arbitrary", "arbitrary"),
        name="pool_router",
    )(x3, x3, gain, pool_w, scale, ffn_gain, w_router, tril)
    return out.reshape(n, d), route_i, route_g, counts, h_packed


ROUTE_COLS = 8


def _pack_bf16_pairs(hr):
    c = hr.shape[1] // 2
    bits = pltpu.bitcast(hr, jnp.int32)
    return (bits[:, c:] & jnp.int32(-65536)) | lax.shift_right_logical(bits[:, :c], 16)


def _unpack_bf16_pairs(packed):
    lo = pltpu.bitcast(lax.shift_left(packed, 16), F32)
    hi = pltpu.bitcast(packed & jnp.int32(-65536), F32)
    return jnp.concatenate([lo.astype(BF16), hi.astype(BF16)], axis=1)


def _route_tile(first, x, g_ref, wr_ref, tril_ref, ri_ref, rg_ref, cnt_ref, hp_ref, run_ref):
    @pl.when(first)
    def _():
        run_ref[...] = jnp.zeros_like(run_ref)

    h = _rms(x, g_ref[...])
    wr = wr_ref[...]
    ne = wr.shape[1]
    h_hi = h.astype(BF16)
    hp_ref[...] = _pack_bf16_pairs(h_hi.astype(F32))
    h_lo = (h - h_hi.astype(F32)).astype(BF16)
    w_hi = wr.astype(BF16)
    w_lo = (wr - w_hi.astype(F32)).astype(BF16)
    hi_part = jnp.dot(h_hi, jnp.concatenate([w_hi, w_lo], axis=1), preferred_element_type=F32)
    logits = hi_part[:, :ne] + hi_part[:, ne:] + jnp.dot(h_lo, w_hi, preferred_element_type=F32)
    tm = logits.shape[0]
    lane = lax.broadcasted_iota(jnp.int32, (tm, ne), 1)
    v1 = jnp.max(logits, axis=1, keepdims=True)
    i1 = jnp.min(jnp.where(logits == v1, lane, ne), axis=1, keepdims=True)
    rest = jnp.where(lane == i1, -jnp.inf, logits)
    v2 = jnp.max(rest, axis=1, keepdims=True)
    i2 = jnp.min(jnp.where(rest == v2, lane, ne), axis=1, keepdims=True)
    e = jnp.exp(v2 - v1)
    g1 = 1.0 / (1.0 + e)
    g2 = e / (1.0 + e)
    oh1 = (lane == i1).astype(F32)
    oh2 = (lane == i2).astype(F32)
    both = oh1 + oh2
    before = jnp.dot(tril_ref[...], both.astype(BF16), preferred_element_type=F32) + run_ref[...]
    r1 = jnp.sum(before * oh1, axis=1, keepdims=True).astype(jnp.int32)
    r2 = jnp.sum(before * oh2, axis=1, keepdims=True).astype(jnp.int32)
    run_ref[...] += jnp.sum(both, axis=0, keepdims=True)
    ri_ref[...] = jnp.where(lane == 0, i1, jnp.where(lane == 1, i2, jnp.where(lane == 2, r1, r2)))
    rg_ref[...] = jnp.where(lane == 0, g1, g2)
    cnt_ref[...] = run_ref[...]


SC_WINDOW = 64


def _sc_workers():
    info = plsc.get_sparse_core_info()
    mesh = plsc.VectorSubcoreMesh(core_axis_name="core", subcore_axis_name="subcore")
    return mesh, info.num_cores, info.num_cores * info.num_subcores


def _dispatch(x, dest, cap):
    n, d = x.shape
    mesh, n_cores, n_workers = _sc_workers()
    per_worker = n // n_workers

    @functools.partial(
        pl.kernel, mesh=mesh, out_type=jax.ShapeDtypeStruct((cap, d), x.dtype),
        scratch_types=[pltpu.VMEM((SC_WINDOW,), jnp.int32)] * TOP_K
        + [pltpu.VMEM((SC_WINDOW, d), x.dtype), pltpu.SemaphoreType.DMA],
        name="moe_dispatch")
    def scatter(x_hbm, *rest):
        idx_hbm, (xs_hbm, *idx_vmem, rows, sem) = rest[:TOP_K], rest[TOP_K:]
        wid = lax.axis_index("subcore") * n_cores + lax.axis_index("core")

        @pl.loop(0, per_worker // SC_WINDOW)
        def _(c):
            base = wid * per_worker + c * SC_WINDOW
            for k in range(TOP_K):
                pltpu.sync_copy(idx_hbm[k].at[pl.ds(base, SC_WINDOW)], idx_vmem[k])
            pltpu.sync_copy(x_hbm.at[pl.ds(base, SC_WINDOW)], rows)
            for k in range(TOP_K):
                pltpu.async_copy(rows, xs_hbm.at[idx_vmem[k]], sem).wait()

    return scatter(x, *[dest[k] for k in range(TOP_K)])


def _gather_rows(table, idx):
    m = idx.shape[0]
    d = table.shape[1]
    mesh, n_cores, n_workers = _sc_workers()
    per_worker = m // n_workers

    @functools.partial(
        pl.kernel, mesh=mesh, out_type=jax.ShapeDtypeStruct((m, d), table.dtype),
        scratch_types=[pltpu.VMEM((SC_WINDOW,), jnp.int32), pltpu.VMEM((SC_WINDOW, d), table.dtype),
                       pltpu.SemaphoreType.DMA],
        name="moe_gather")
    def gather(t_hbm, i_hbm, o_hbm, idx_vmem, rows, sem):
        wid = lax.axis_index("subcore") * n_cores + lax.axis_index("core")

        @pl.loop(0, per_worker // SC_WINDOW)
        def _(c):
            base = wid * per_worker + c * SC_WINDOW
            pltpu.sync_copy(i_hbm.at[pl.ds(base, SC_WINDOW)], idx_vmem)
            pltpu.async_copy(t_hbm.at[idx_vmem], rows, sem).wait()
            pltpu.sync_copy(rows, o_hbm.at[pl.ds(base, SC_WINDOW)])

    return gather(table, idx)


def _expert_kernel(be_ref, nv_ref, xs_ref, wg_ref, wu_ref, wd_ref, ys_ref, xb_ref):
    del be_ref
    b = pl.program_id(0)
    j = pl.program_id(1)

    @pl.when(j == 0)
    def _():
        ys_ref[...] = jnp.zeros_like(ys_ref)

    @pl.when(b < nv_ref[0])
    def _():
        @pl.when(j == 0)
        def _():
            xb_ref[...] = _unpack_bf16_pairs(xs_ref[...])

        x = xb_ref[...]
        acts = []
        for c in range(wg_ref.shape[1] // MXU_DIM):
            cols = slice(c * MXU_DIM, (c + 1) * MXU_DIM)
            gate = jnp.dot(x, wg_ref[:, cols].astype(BF16), preferred_element_type=F32)
            up = jnp.dot(x, wu_ref[:, cols].astype(BF16), preferred_element_type=F32)
            acts.append((_silu(gate) * up).astype(BF16))
        a = jnp.concatenate(acts, axis=1)
        ys_ref[...] += jnp.dot(a, wd_ref[...].astype(BF16), preferred_element_type=F32)


def _experts(xs, block_e, n_valid, w_gate, w_up, w_down, layer, tm, tf):
    cap = xs.shape[0]
    d = 2 * xs.shape[1]
    ff = w_gate.shape[3]
    nj = ff // tf

    def blk(b, nv):
        return jnp.minimum(b, nv[0] - 1)

    def ffi(b, j, nv):
        return jnp.where(b < nv[0], j, nj - 1)

    return pl.pallas_call(
        _expert_kernel,
        grid_spec=pltpu.PrefetchScalarGridSpec(
            num_scalar_prefetch=2,
            grid=(cap // tm, nj),
            in_specs=[
                pl.BlockSpec((tm, d // 2), lambda b, j, be, nv: (blk(b, nv), 0)),
                pl.BlockSpec((None, None, d, tf), lambda b, j, be, nv: (layer, be[blk(b, nv)], 0, ffi(b, j, nv))),
                pl.BlockSpec((None, None, d, tf), lambda b, j, be, nv: (layer, be[blk(b, nv)], 0, ffi(b, j, nv))),
                pl.BlockSpec((None, None, tf, d), lambda b, j, be, nv: (layer, be[blk(b, nv)], ffi(b, j, nv), 0)),
            ],
            out_specs=pl.BlockSpec((tm, d), lambda b, j, be, nv: (b, 0)),
            scratch_shapes=[pltpu.VMEM((tm, d), BF16)],
        ),
        out_shape=jax.ShapeDtypeStruct((cap, d), F32),
        compiler_params=_params("arbitrary", "arbitrary"),
        name="moe_experts",
    )(block_e, n_valid, xs, w_gate, w_up, w_down)


def _combine_kernel(x_ref, rg_ref, gf_ref, y0_ref, y1_ref, o_ref, *, final_norm):
    rg = rg_ref[...]
    out = x_ref[...] + (rg[:, 0:1] * y0_ref[...] + rg[:, 1:2] * y1_ref[...])
    if final_norm:
        out = _rms(out, gf_ref[...])
    o_ref[...] = out


def _combine(x, route_g, yg, final_gain, final_norm):
    n, d = x.shape
    tm = 512
    return pl.pallas_call(
        functools.partial(_combine_kernel, final_norm=final_norm),
        grid=(n // tm,),
        in_specs=[
            pl.BlockSpec((tm, d), lambda i: (i, 0)),
            pl.BlockSpec((tm, ROUTE_COLS), lambda i: (i, 0)),
            pl.BlockSpec((1, d), lambda i: (0, 0)),
            pl.BlockSpec((None, tm, d), lambda i: (0, i, 0)),
            pl.BlockSpec((None, tm, d), lambda i: (1, i, 0)),
        ],
        out_specs=pl.BlockSpec((tm, d), lambda i: (i, 0)),
        out_shape=jax.ShapeDtypeStruct((n, d), F32),
        compiler_params=_params("parallel"),
        name="moe_combine",
    )(x, route_g, final_gain, yg, yg)


def _moe_layer(x, routing, w_gate, w_up, w_down, layer, final_gain, final_norm):
    n, d = x.shape
    tm = 1024 if n * TOP_K >= 8 * 1024 else 256
    tf = 2 * MXU_DIM
    route_i, route_g, counts, h_packed = routing

    counts = counts[0].astype(jnp.int32)
    padded = ((counts + tm - 1) // tm) * tm
    ends = jnp.cumsum(padded)
    starts = ends - padded
    dest = (starts[route_i[:, :TOP_K]] + route_i[:, TOP_K:2 * TOP_K]).T
    n_blocks = (n * TOP_K) // tm + N_EXPERTS
    block_e = jnp.minimum(
        jnp.searchsorted(ends, jnp.arange(n_blocks, dtype=jnp.int32) * tm, side="right"),
        N_EXPERTS - 1).astype(jnp.int32)
    n_valid = (ends[-1:] // tm).astype(jnp.int32)

    xs = _dispatch(h_packed, dest, n_blocks * tm)
    ys = _experts(xs, block_e, n_valid, w_gate, w_up, w_down, layer, tm, tf)
    yg = _gather_rows(ys, dest.reshape(TOP_K * n)).reshape(TOP_K, n, d)
    return _combine(x, route_g, yg, final_gain, final_norm)


def _rope_tables(seq):
    pos = jnp.arange(seq, dtype=F32)
    inv_freq = ROPE_THETA ** (-jnp.arange(0, HEAD_DIM, 2, dtype=F32) / HEAD_DIM)
    ang = pos[:, None] * inv_freq[None, :]
    reps = LANES // (HEAD_DIM // 2)
    cos = jnp.tile(jnp.cos(ang), (1, reps))
    sin = jnp.tile(jnp.sin(ang), (1, reps))
    first_half = (jnp.arange(LANES) % HEAD_DIM) < HEAD_DIM // 2
    tabs = (cos, jnp.where(first_half, -sin, sin))
    q_scale = HEAD_DIM ** -0.5 * 1.4426950408889634
    return tuple(jnp.stack([t * q_scale, t]) for t in tabs)


def kernel(x, mix_norm, ffn_norm, attn_w_qkv, attn_w_o, pool_w, pool_scale, dense_w_gate, dense_w_up,
           dense_w_down, moe_router, moe_w_gate, moe_w_up, moe_w_down, final_norm):
    batch, seq, d = x.shape
    depth = mix_norm.shape[0]
    assert depth % 2 == 0 and d == N_HEADS * HEAD_DIM
    rope_tables = _rope_tables(seq)
    xf = x.reshape(batch * seq, d)
    final_gain = final_norm.reshape(1, d)
    for i in range(depth):
        j = i // 2
        mix_gain = mix_norm[i].reshape(1, d)
        ffn_gain = ffn_norm[i].reshape(1, d)
        if i % 2 == 0:
            w_qkv = attn_w_qkv[j].astype(BF16)
            qkv = _qkv_project(xf, mix_gain, w_qkv, rope_tables, batch, seq)
            xf = _attention_and_dense_ffn(qkv, xf, attn_w_o[j].astype(BF16), ffn_gain, dense_w_gate[j].astype(BF16),
                                          dense_w_up[j].astype(BF16), dense_w_down[j].astype(BF16), batch, seq)
        else:
            xf, *routing = _pool_mixer_and_router(xf, mix_gain, pool_w[j].astype(BF16), pool_scale[j].reshape(1, d),
                                                  ffn_gain, moe_router[j], batch, seq)
            xf = _moe_layer(xf, routing, moe_w_gate, moe_w_up, moe_w_down, j, final_gain, i == depth - 1)
    return xf.reshape(batch, seq, d)
```

```python
import functools

import jax
import jax.numpy as jnp
from jax import lax
from jax.experimental import pallas as pl
from jax.experimental.pallas import tpu as pltpu
from jax.experimental.pallas import tpu_sc as plsc

WINDOWS = (128, 512, 2048)
DILATIONS = (1, 4, 16)
N_GROUPS = len(WINDOWS)
N_HEADS = 16
HEAD_DIM = 64
ROPE_THETA = 10000.0
POOL_SIZES = (2, 4, 8, 16)
N_EXPERTS = 8
TOP_K = 2
RMS_EPS = 1e-6
MASK_VALUE = -1e30
QKV_ROWS = 512

LANES = 128
MXU_DIM = 256
VMEM_LIMIT_BYTES = 56 * 1024 * 1024

F32 = jnp.float32
BF16 = jnp.bfloat16


def _params(*sem):
    return pltpu.CompilerParams(dimension_semantics=sem, vmem_limit_bytes=VMEM_LIMIT_BYTES)


def _rms(x, g):
    ms = jnp.mean(x * x, axis=-1, keepdims=True)
    return x * lax.rsqrt(ms + RMS_EPS) * g


def _silu(g):
    return g / (1.0 + jnp.exp(-g))


def _qkv_kernel(x_ref, g_ref, w_ref, cos_ref, sin_ref, o0_ref, o1_ref, o2_ref, acc_ref):
    tm, d = x_ref.shape
    n_chunks = acc_ref.shape[1]
    lane = lax.broadcasted_iota(jnp.int32, (1, LANES), 1)
    first_half = (lane % HEAD_DIM) < HEAD_DIM // 2
    h = _rms(x_ref[...], g_ref[...]).astype(BF16)
    for g, (o_ref, dil) in enumerate(zip((o0_ref, o1_ref, o2_ref), DILATIONS)):
        rows = tm // dil

        def strided(ref, r, dil=dil, rows=rows):
            if dil == 1:
                return ref[...]
            return ref[pl.ds(r, rows, stride=dil), :]

        for kind in range(3):
            col0 = (g * 3 + kind) * d
            acc = jnp.dot(h, w_ref[:, col0:col0 + d], preferred_element_type=F32)
            for c in range(n_chunks):
                acc_ref[kind, c] = acc[:, c * LANES:(c + 1) * LANES]
            for r in range(dil):
                if kind < 2:
                    cos = strided(cos_ref.at[kind], r)
                    sin = strided(sin_ref.at[kind], r)
                for c in range(n_chunks):
                    t = strided(acc_ref.at[kind, c], r)
                    if kind < 2:
                        partner = jnp.where(first_half, pltpu.roll(t, LANES - HEAD_DIM // 2, 1),
                                            pltpu.roll(t, HEAD_DIM // 2, 1))
                        t = t * cos + partner * sin
                    o_ref[kind, r, :, c * LANES:(c + 1) * LANES] = t.astype(BF16)


def _qkv_project(x, gain, w_qkv, tables, batch, seq):
    n, d = x.shape
    tm = QKV_ROWS
    tiles_per_seq = seq // tm
    tab = pl.BlockSpec((2, tm, LANES), lambda i: (0, i % tiles_per_seq, 0))

    def o_spec(dil):
        return pl.BlockSpec((3, None, dil, tm // dil, d),
                            lambda i: (0, i // tiles_per_seq, 0, i % tiles_per_seq, 0))

    outs = pl.pallas_call(
        _qkv_kernel,
        grid=(n // tm,),
        in_specs=[
            pl.BlockSpec((tm, d), lambda i: (i, 0)),
            pl.BlockSpec((1, d), lambda i: (0, 0)),
            pl.BlockSpec(w_qkv.shape, lambda i: (0, 0), pipeline_mode=pl.Buffered(1)),
            tab, tab,
        ],
        out_specs=[o_spec(dil) for dil in DILATIONS],
        out_shape=[jax.ShapeDtypeStruct((3, batch, dil, seq // dil, d), BF16) for dil in DILATIONS],
        scratch_shapes=[pltpu.VMEM((3, d // LANES, tm, LANES), F32)],
        compiler_params=_params("parallel"),
        name="qkv_rope",
    )(x, gain, w_qkv, *tables)
    return [o.reshape(3, batch, seq, d) for o in outs]


ATTN_TILES_PER_FFN_PIECE = 3


def _attn_dense_kernel(q0, k0, v0, q1, k1, v1, q2, k2, v2, band_ref, causal_ref,
                       x_ref, wo_ref, g_ref, wg_ref, wu_ref, wd_ref, y_ref,
                       o_scr, num_ref, max_ref, den_ref, kt_ref):
    b = pl.program_id(0)
    hp = pl.program_id(1)
    n_pairs, seq = o_scr.shape[1], o_scr.shape[2]
    w = WINDOWS[0] // DILATIONS[0]
    lane = lax.broadcasted_iota(jnp.int32, (1, LANES), 1)
    first_head = lane < HEAD_DIM
    o_out = o_scr.at[b % 2, hp]
    o_in = o_scr.at[(b + 1) % 2]

    @pl.when((b == 0) & (hp == 0))
    def _():
        o_scr[1] = jnp.zeros(o_scr.shape[1:], o_scr.dtype)

    def ffn_pieces():
        rows = pl.ds(pl.multiple_of(hp * x_ref.shape[0], x_ref.shape[0]), x_ref.shape[0])
        o = jnp.concatenate([o_in[p, rows, :] for p in range(n_pairs)], axis=1)
        x = x_ref[...] + jnp.dot(o, wo_ref[...], preferred_element_type=F32)
        h = _rms(x, g_ref[...]).astype(BF16)
        yield
        acts = []
        n_chunks = wg_ref.shape[1] // MXU_DIM
        for c in range(n_chunks):
            cols = slice(c * MXU_DIM, (c + 1) * MXU_DIM)
            gate = jnp.dot(h, wg_ref[:, cols], preferred_element_type=F32)
            up = jnp.dot(h, wu_ref[:, cols], preferred_element_type=F32)
            acts.append((_silu(gate) * up).astype(BF16))
            yield
        half = n_chunks // 2
        acc = jnp.dot(jnp.concatenate(acts[:half], axis=1), wd_ref[:half * MXU_DIM, :], preferred_element_type=F32)
        yield
        acc = acc + jnp.dot(jnp.concatenate(acts[half:], axis=1), wd_ref[half * MXU_DIM:, :],
                            preferred_element_type=F32)
        y_ref[...] = x + acc
        yield

    def tile(q_ref, v_ref, qoff, has_prev):
        q = q_ref[qoff:qoff + w, :]
        zero = jnp.zeros_like(q)
        qq = jnp.concatenate([jnp.where(first_head, q, zero), jnp.where(first_head, zero, q)], axis=0)
        k_lo = qoff - w if has_prev else qoff
        s = jnp.dot(qq, kt_ref[:, k_lo:qoff + w], preferred_element_type=F32)
        s = s + (band_ref[...] if has_prev else causal_ref[...])
        mx = jnp.max(s, axis=1, keepdims=True)
        p = jnp.exp2(s - mx)
        den = jnp.sum(p, axis=1, keepdims=True)
        pv = jnp.dot(p.astype(BF16), v_ref[k_lo:qoff + w, :], preferred_element_type=F32)
        return tuple(jnp.where(first_head, a[:w], a[w:]) for a in (pv, mx, den))

    pieces = ffn_pieces()
    tiles_done = 0
    order = sorted(range(N_GROUPS), key=lambda g: -DILATIONS[g])
    for g in order:
        q_ref, k_ref, v_ref = ((q0, k0, v0), (q1, k1, v1), (q2, k2, v2))[g]
        dil = DILATIONS[g]
        sub_len = seq // dil
        nb = sub_len // w
        kt_ref[...] = k_ref[...].T
        for t in range(dil * nb):
            r, n = divmod(t, nb)
            qoff = r * sub_len + n * w
            pv, mx, den = tile(q_ref, v_ref, qoff, n > 0)
            tiles_done += 1
            if tiles_done % ATTN_TILES_PER_FFN_PIECE == 0:
                next(pieces, None)
            rows = pl.ds(qoff, w) if dil == 1 else pl.ds(r + dil * n * w, w, stride=dil)
            if g == order[0]:
                num_ref[rows, :] = pv
                max_ref[rows, :] = mx
                den_ref[rows, :] = den
                continue
            old_max = max_ref[rows, :]
            new_max = jnp.maximum(old_max, mx)
            a = jnp.exp2(old_max - new_max)
            c = jnp.exp2(mx - new_max)
            num = a * num_ref[rows, :] + c * pv
            den = a * den_ref[rows, :] + c * den
            if g == order[-1]:
                o_out[rows, :] = (num / den).astype(BF16)
            else:
                num_ref[rows, :] = num
                den_ref[rows, :] = den
                max_ref[rows, :] = new_max
    for _ in pieces:
        pass


def _attention_biases(w):
    row = jnp.arange(2 * w)[:, None] % w
    col = jnp.arange(2 * w)[None, :]
    dist = row + w - col
    band = jnp.where((dist >= 0) & (dist <= w), 0.0, MASK_VALUE).astype(F32)
    causal = jnp.where(row >= col[:, :w], 0.0, MASK_VALUE).astype(F32)
    return band, causal


def _attention_and_dense_ffn(qkv_groups, x, w_o, gain, w_gate, w_up, w_down, batch, seq):
    n, d = x.shape
    ff = w_gate.shape[1]
    n_pairs = d // LANES
    rows = seq // n_pairs
    last = batch - 1
    in_specs, args = [], []
    for arr in qkv_groups:
        for kind in range(3):
            in_specs.append(pl.BlockSpec((None, None, seq, LANES),
                                         lambda b, hp, kind=kind: (kind, jnp.minimum(b, last), 0, hp)))
            args.append(arr)
    const = lambda shape: pl.BlockSpec(shape, lambda b, hp: (0, 0), pipeline_mode=pl.Buffered(1))
    ffn_in = pl.BlockSpec((rows, d), lambda b, hp: (jnp.maximum(b - 1, 0) * n_pairs + hp, 0))
    ffn_out = pl.BlockSpec((rows, d), lambda b, hp: (jnp.where(b == 0, batch, b - 1) * n_pairs + hp, 0))
    biases = _attention_biases(WINDOWS[0] // DILATIONS[0])
    in_specs += [const(t.shape) for t in biases]
    in_specs += [ffn_in, const((d, d)), const((1, d)), const((d, ff)), const((d, ff)), const((ff, d))]
    args += [*biases, x, w_o, gain, w_gate, w_up, w_down]
    return pl.pallas_call(
        _attn_dense_kernel,
        grid=(batch + 1, n_pairs),
        in_specs=in_specs,
        out_specs=ffn_out,
        out_shape=jax.ShapeDtypeStruct((n + seq, d), F32),
        scratch_shapes=[pltpu.VMEM((2, n_pairs, seq, LANES), BF16)]
        + [pltpu.VMEM((seq, LANES), F32)] * 3 + [pltpu.VMEM((LANES, seq), BF16)],
        compiler_params=_params("arbitrary", "arbitrary"),
        name="attention_ffn",
    )(*args)


POOL_HALO = max(POOL_SIZES)


def _pool_tile(i, x_ref, halo_ref, g_ref, w_ref, sc_ref, y_ref):
    ts = x_ref.shape[0]
    g = g_ref[...]
    x = x_ref[...]
    h = _rms(x, g)
    hh = jnp.where(i > 0, _rms(halo_ref[...], g), 0.0)
    ext = jnp.concatenate([hh, h], axis=0)
    pos = i * ts + lax.broadcasted_iota(jnp.int32, (ts, 1), 0)
    pc = w_ref.shape[1]
    for grp, size in enumerate(POOL_SIZES):
        sl = slice(grp * pc, (grp + 1) * pc)
        s = ext[:, sl]
        step = 1
        while step < size:
            s = s + pltpu.roll(s, step, 0)
            step *= 2
        cnt = jnp.minimum(pos + 1, size).astype(F32)
        y = s[POOL_HALO:, :] / cnt - h[:, sl]
        z = jnp.dot(y.astype(BF16), w_ref[grp], preferred_element_type=F32)
        y_ref[:, sl] = x[:, sl] + z * sc_ref[:, sl]


def _pool_router_kernel(x_ref, halo_ref, g_ref, w_ref, sc_ref, g2_ref, wr_ref, tril_ref,
                        y_ref, ri_ref, rg_ref, cnt_ref, hp_ref, run_ref):
    b = pl.program_id(0)
    i = pl.program_id(1)
    _pool_tile(i, x_ref, halo_ref, g_ref, w_ref, sc_ref, y_ref)
    _route_tile((b == 0) & (i == 0), y_ref[...], g2_ref, wr_ref, tril_ref, ri_ref, rg_ref, cnt_ref, hp_ref, run_ref)


def _pool_mixer_and_router(x, gain, pool_w, scale, ffn_gain, w_router, batch, seq):
    d = x.shape[1]
    n = batch * seq
    ne = w_router.shape[1]
    ts = 512
    tiles = seq // ts
    x3 = x.reshape(-1, seq, d)
    hb = ts // POOL_HALO
    tril = (jnp.arange(ts)[:, None] > jnp.arange(ts)[None, :]).astype(BF16)
    const = lambda shape: pl.BlockSpec(shape, lambda b, i: (0,) * len(shape))
    tok = lambda cols: pl.BlockSpec((ts, cols), lambda b, i: (b * tiles + i, 0))
    out, route_i, route_g, counts, h_packed = pl.pallas_call(
        _pool_router_kernel,
        grid=(batch, tiles),
        in_specs=[
            pl.BlockSpec((None, ts, d), lambda b, i: (b, i, 0)),
            pl.BlockSpec((None, POOL_HALO, d), lambda b, i: (b, jnp.maximum(i * hb - 1, 0), 0)),
            const((1, d)), const(pool_w.shape), const((1, d)), const((1, d)), const((d, ne)), const((ts, ts)),
        ],
        out_specs=[
            pl.BlockSpec((None, ts, d), lambda b, i: (b, i, 0)),
            tok(ROUTE_COLS), tok(ROUTE_COLS), const((1, ne)), tok(d // 2),
        ],
        out_shape=[
            jax.ShapeDtypeStruct((batch, seq, d), F32),
            jax.ShapeDtypeStruct((n, ROUTE_COLS), jnp.int32),
            jax.ShapeDtypeStruct((n, ROUTE_COLS), F32),
            jax.ShapeDtypeStruct((1, ne), F32),
            jax.ShapeDtypeStruct((n, d // 2), jnp.int32),
        ],
        scratch_shapes=[pltpu.VMEM((1, ne), F32)],
        compiler_params=_params("arbitrary", "arbitrary"),
        name="pool_router",
    )(x3, x3, gain, pool_w, scale, ffn_gain, w_router, tril)
    return out.reshape(n, d), route_i, route_g, counts, h_packed


ROUTE_COLS = 8


def _pack_bf16_pairs(hr):
    c = hr.shape[1] // 2
    bits = pltpu.bitcast(hr, jnp.int32)
    return (bits[:, c:] & jnp.int32(-65536)) | lax.shift_right_logical(bits[:, :c], 16)


def _unpack_bf16_pairs(packed):
    lo = pltpu.bitcast(lax.shift_left(packed, 16), F32)
    hi = pltpu.bitcast(packed & jnp.int32(-65536), F32)
    return jnp.concatenate([lo.astype(BF16), hi.astype(BF16)], axis=1)


def _route_tile(first, x, g_ref, wr_ref, tril_ref, ri_ref, rg_ref, cnt_ref, hp_ref, run_ref):
    @pl.when(first)
    def _():
        run_ref[...] = jnp.zeros_like(run_ref)

    h = _rms(x, g_ref[...])
    wr = wr_ref[...]
    ne = wr.shape[1]
    h_hi = h.astype(BF16)
    hp_ref[...] = _pack_bf16_pairs(h_hi.astype(F32))
    h_lo = (h - h_hi.astype(F32)).astype(BF16)
    w_hi = wr.astype(BF16)
    w_lo = (wr - w_hi.astype(F32)).astype(BF16)
    hi_part = jnp.dot(h_hi, jnp.concatenate([w_hi, w_lo], axis=1), preferred_element_type=F32)
    logits = hi_part[:, :ne] + hi_part[:, ne:] + jnp.dot(h_lo, w_hi, preferred_element_type=F32)
    tm = logits.shape[0]
    lane = lax.broadcasted_iota(jnp.int32, (tm, ne), 1)
    v1 = jnp.max(logits, axis=1, keepdims=True)
    i1 = jnp.min(jnp.where(logits == v1, lane, ne), axis=1, keepdims=True)
    rest = jnp.where(lane == i1, -jnp.inf, logits)
    v2 = jnp.max(rest, axis=1, keepdims=True)
    i2 = jnp.min(jnp.where(rest == v2, lane, ne), axis=1, keepdims=True)
    e = jnp.exp(v2 - v1)
    g1 = 1.0 / (1.0 + e)
    g2 = e / (1.0 + e)
    oh1 = (lane == i1).astype(F32)
    oh2 = (lane == i2).astype(F32)
    both = oh1 + oh2
    before = jnp.dot(tril_ref[...], both.astype(BF16), preferred_element_type=F32) + run_ref[...]
    r1 = jnp.sum(before * oh1, axis=1, keepdims=True).astype(jnp.int32)
    r2 = jnp.sum(before * oh2, axis=1, keepdims=True).astype(jnp.int32)
    run_ref[...] += jnp.sum(both, axis=0, keepdims=True)
    ri_ref[...] = jnp.where(lane == 0, i1, jnp.where(lane == 1, i2, jnp.where(lane == 2, r1, r2)))
    rg_ref[...] = jnp.where(lane == 0, g1, g2)
    cnt_ref[...] = run_ref[...]


SC_WINDOW = 64


def _sc_workers():
    info = plsc.get_sparse_core_info()
    mesh = plsc.VectorSubcoreMesh(core_axis_name="core", subcore_axis_name="subcore")
    return mesh, info.num_cores, info.num_cores * info.num_subcores


def _dispatch(x, dest, cap):
    n, d = x.shape
    mesh, n_cores, n_workers = _sc_workers()
    per_worker = n // n_workers

    @functools.partial(
        pl.kernel, mesh=mesh, out_type=jax.ShapeDtypeStruct((cap, d), x.dtype),
        scratch_types=[pltpu.VMEM((SC_WINDOW,), jnp.int32)] * TOP_K
        + [pltpu.VMEM((SC_WINDOW, d), x.dtype), pltpu.SemaphoreType.DMA],
        name="moe_dispatch")
    def scatter(x_hbm, *rest):
        idx_hbm, (xs_hbm, *idx_vmem, rows, sem) = rest[:TOP_K], rest[TOP_K:]
        wid = lax.axis_index("subcore") * n_cores + lax.axis_index("core")

        @pl.loop(0, per_worker // SC_WINDOW)
        def _(c):
            base = wid * per_worker + c * SC_WINDOW
            for k in range(TOP_K):
                pltpu.sync_copy(idx_hbm[k].at[pl.ds(base, SC_WINDOW)], idx_vmem[k])
            pltpu.sync_copy(x_hbm.at[pl.ds(base, SC_WINDOW)], rows)
            for k in range(TOP_K):
                pltpu.async_copy(rows, xs_hbm.at[idx_vmem[k]], sem).wait()

    return scatter(x, *[dest[k] for k in range(TOP_K)])


def _gather_rows(table, idx):
    m = idx.shape[0]
    d = table.shape[1]
    mesh, n_cores, n_workers = _sc_workers()
    per_worker = m // n_workers

    @functools.partial(
        pl.kernel, mesh=mesh, out_type=jax.ShapeDtypeStruct((m, d), table.dtype),
        scratch_types=[pltpu.VMEM((SC_WINDOW,), jnp.int32), pltpu.VMEM((SC_WINDOW, d), table.dtype),
                       pltpu.SemaphoreType.DMA],
        name="moe_gather")
    def gather(t_hbm, i_hbm, o_hbm, idx_vmem, rows, sem):
        wid = lax.axis_index("subcore") * n_cores + lax.axis_index("core")

        @pl.loop(0, per_worker // SC_WINDOW)
        def _(c):
            base = wid * per_worker + c * SC_WINDOW
            pltpu.sync_copy(i_hbm.at[pl.ds(base, SC_WINDOW)], idx_vmem)
            pltpu.async_copy(t_hbm.at[idx_vmem], rows, sem).wait()
            pltpu.sync_copy(rows, o_hbm.at[pl.ds(base, SC_WINDOW)])

    return gather(table, idx)


def _expert_kernel(be_ref, nv_ref, xs_ref, wg_ref, wu_ref, wd_ref, ys_ref, xb_ref):
    del be_ref
    b = pl.program_id(0)
    j = pl.program_id(1)

    @pl.when(j == 0)
    def _():
        ys_ref[...] = jnp.zeros_like(ys_ref)

    @pl.when(b < nv_ref[0])
    def _():
        @pl.when(j == 0)
        def _():
            xb_ref[...] = _unpack_bf16_pairs(xs_ref[...])

        x = xb_ref[...]
        acts = []
        for c in range(wg_ref.shape[1] // MXU_DIM):
            cols = slice(c * MXU_DIM, (c + 1) * MXU_DIM)
            gate = jnp.dot(x, wg_ref[:, cols].astype(BF16), preferred_element_type=F32)
            up = jnp.dot(x, wu_ref[:, cols].astype(BF16), preferred_element_type=F32)
            acts.append((_silu(gate) * up).astype(BF16))
        a = jnp.concatenate(acts, axis=1)
        ys_ref[...] += jnp.dot(a, wd_ref[...].astype(BF16), preferred_element_type=F32)


def _experts(xs, block_e, n_valid, w_gate, w_up, w_down, layer, tm, tf):
    cap = xs.shape[0]
    d = 2 * xs.shape[1]
    ff = w_gate.shape[3]
    nj = ff // tf

    def blk(b, nv):
        return jnp.minimum(b, nv[0] - 1)

    def ffi(b, j, nv):
        return jnp.where(b < nv[0], j, nj - 1)

    return pl.pallas_call(
        _expert_kernel,
        grid_spec=pltpu.PrefetchScalarGridSpec(
            num_scalar_prefetch=2,
            grid=(cap // tm, nj),
            in_specs=[
                pl.BlockSpec((tm, d // 2), lambda b, j, be, nv: (blk(b, nv), 0)),
                pl.BlockSpec((None, None, d, tf), lambda b, j, be, nv: (layer, be[blk(b, nv)], 0, ffi(b, j, nv))),
                pl.BlockSpec((None, None, d, tf), lambda b, j, be, nv: (layer, be[blk(b, nv)], 0, ffi(b, j, nv))),
                pl.BlockSpec((None, None, tf, d), lambda b, j, be, nv: (layer, be[blk(b, nv)], ffi(b, j, nv), 0)),
            ],
            out_specs=pl.BlockSpec((tm, d), lambda b, j, be, nv: (b, 0)),
            scratch_shapes=[pltpu.VMEM((tm, d), BF16)],
        ),
        out_shape=jax.ShapeDtypeStruct((cap, d), F32),
        compiler_params=_params("arbitrary", "arbitrary"),
        name="moe_experts",
    )(block_e, n_valid, xs, w_gate, w_up, w_down)


def _combine_kernel(x_ref, rg_ref, gf_ref, y0_ref, y1_ref, o_ref, *, final_norm):
    rg = rg_ref[...]
    out = x_ref[...] + (rg[:, 0:1] * y0_ref[...] + rg[:, 1:2] * y1_ref[...])
    if final_norm:
        out = _rms(out, gf_ref[...])
    o_ref[...] = out


def _combine(x, route_g, yg, final_gain, final_norm):
    n, d = x.shape
    tm = 512
    return pl.pallas_call(
        functools.partial(_combine_kernel, final_norm=final_norm),
        grid=(n // tm,),
        in_specs=[
            pl.BlockSpec((tm, d), lambda i: (i, 0)),
            pl.BlockSpec((tm, ROUTE_COLS), lambda i: (i, 0)),
            pl.BlockSpec((1, d), lambda i: (0, 0)),
            pl.BlockSpec((None, tm, d), lambda i: (0, i, 0)),
            pl.BlockSpec((None, tm, d), lambda i: (1, i, 0)),
        ],
        out_specs=pl.BlockSpec((tm, d), lambda i: (i, 0)),
        out_shape=jax.ShapeDtypeStruct((n, d), F32),
        compiler_params=_params("parallel"),
        name="moe_combine",
    )(x, route_g, final_gain, yg, yg)


def _moe_layer(x, routing, w_gate, w_up, w_down, layer, final_gain, final_norm):
    n, d = x.shape
    tm = 1024 if n * TOP_K >= 8 * 1024 else 256
    tf = 2 * MXU_DIM
    route_i, route_g, counts, h_packed = routing

    counts = counts[0].astype(jnp.int32)
    padded = ((counts + tm - 1) // tm) * tm
    ends = jnp.cumsum(padded)
    starts = ends - padded
    dest = (starts[route_i[:, :TOP_K]] + route_i[:, TOP_K:2 * TOP_K]).T
    n_blocks = (n * TOP_K) // tm + N_EXPERTS
    block_start = jnp.arange(n_blocks, dtype=jnp.int32) * tm
    block_e = jnp.minimum(jnp.sum(ends[None, :] <= block_start[:, None], axis=1), N_EXPERTS - 1).astype(jnp.int32)
    n_valid = (ends[-1:] // tm).astype(jnp.int32)

    xs = _dispatch(h_packed, dest, n_blocks * tm)
    ys = _experts(xs, block_e, n_valid, w_gate, w_up, w_down, layer, tm, tf)
    yg = _gather_rows(ys, dest.reshape(TOP_K * n)).reshape(TOP_K, n, d)
    return _combine(x, route_g, yg, final_gain, final_norm)


def _rope_tables(seq):
    pos = jnp.arange(seq, dtype=F32)
    inv_freq = ROPE_THETA ** (-jnp.arange(0, HEAD_DIM, 2, dtype=F32) / HEAD_DIM)
    ang = pos[:, None] * inv_freq[None, :]
    reps = LANES // (HEAD_DIM // 2)
    cos = jnp.tile(jnp.cos(ang), (1, reps))
    sin = jnp.tile(jnp.sin(ang), (1, reps))
    first_half = (jnp.arange(LANES) % HEAD_DIM) < HEAD_DIM // 2
    tabs = (cos, jnp.where(first_half, -sin, sin))
    q_scale = HEAD_DIM ** -0.5 * 1.4426950408889634
    return tuple(jnp.stack([t * q_scale, t]) for t in tabs)


def kernel(x, mix_norm, ffn_norm, attn_w_qkv, attn_w_o, pool_w, pool_scale, dense_w_gate, dense_w_up,
           dense_w_down, moe_router, moe_w_gate, moe_w_up, moe_w_down, final_norm):
    batch, seq, d = x.shape
    depth = mix_norm.shape[0]
    assert depth % 2 == 0 and d == N_HEADS * HEAD_DIM
    rope_tables = _rope_tables(seq)
    xf = x.reshape(batch * seq, d)
    final_gain = final_norm.reshape(1, d)
    for i in range(depth):
        j = i // 2
        mix_gain = mix_norm[i].reshape(1, d)
        ffn_gain = ffn_norm[i].reshape(1, d)
        if i % 2 == 0:
            w_qkv = attn_w_qkv[j].astype(BF16)
            qkv = _qkv_project(xf, mix_gain, w_qkv, rope_tables, batch, seq)
            xf = _attention_and_dense_ffn(qkv, xf, attn_w_o[j].astype(BF16), ffn_gain, dense_w_gate[j].astype(BF16),
                                          dense_w_up[j].astype(BF16), dense_w_down[j].astype(BF16), batch, seq)
        else:
            xf, *routing = _pool_mixer_and_router(xf, mix_gain, pool_w[j].astype(BF16), pool_scale[j].reshape(1, d),
                                                  ffn_gain, moe_router[j], batch, seq)
            xf = _moe_layer(xf, routing, moe_w_gate, moe_w_up, moe_w_down, j, final_gain, i == depth - 1)
    return xf.reshape(batch, seq, d)
```

```python
import functools

import jax
import jax.numpy as jnp
from jax import lax
from jax.experimental import pallas as pl
from jax.experimental.pallas import tpu as pltpu
from jax.experimental.pallas import tpu_sc as plsc

WINDOWS = (128, 512, 2048)
DILATIONS = (1, 4, 16)
N_GROUPS = len(WINDOWS)
N_HEADS = 16
HEAD_DIM = 64
ROPE_THETA = 10000.0
POOL_SIZES = (2, 4, 8, 16)
N_EXPERTS = 8
TOP_K = 2
RMS_EPS = 1e-6
MASK_VALUE = -1e30
QKV_ROWS = 512

LANES = 128
MXU_DIM = 256
VMEM_LIMIT_BYTES = 56 * 1024 * 1024

F32 = jnp.float32
BF16 = jnp.bfloat16


def _params(*sem):
    return pltpu.CompilerParams(dimension_semantics=sem, vmem_limit_bytes=VMEM_LIMIT_BYTES)


def _rms(x, g):
    ms = jnp.mean(x * x, axis=-1, keepdims=True)
    return x * lax.rsqrt(ms + RMS_EPS) * g


def _silu(g):
    return g / (1.0 + jnp.exp(-g))


def _qkv_kernel(x_ref, g_ref, w_ref, cos_ref, sin_ref, o0_ref, o1_ref, o2_ref, acc_ref):
    tm, d = x_ref.shape
    n_chunks = acc_ref.shape[1]
    lane = lax.broadcasted_iota(jnp.int32, (1, LANES), 1)
    first_half = (lane % HEAD_DIM) < HEAD_DIM // 2
    h = _rms(x_ref[...], g_ref[...]).astype(BF16)
    for g, (o_ref, dil) in enumerate(zip((o0_ref, o1_ref, o2_ref), DILATIONS)):
        rows = tm // dil

        def strided(ref, r, dil=dil, rows=rows):
            if dil == 1:
                return ref[...]
            return ref[pl.ds(r, rows, stride=dil), :]

        for kind in range(3):
            col0 = (g * 3 + kind) * d
            acc = jnp.dot(h, w_ref[:, col0:col0 + d], preferred_element_type=F32)
            for c in range(n_chunks):
                acc_ref[kind, c] = acc[:, c * LANES:(c + 1) * LANES]
            for r in range(dil):
                if kind < 2:
                    cos = strided(cos_ref.at[kind], r)
                    sin = strided(sin_ref.at[kind], r)
                for c in range(n_chunks):
                    t = strided(acc_ref.at[kind, c], r)
                    if kind < 2:
                        partner = jnp.where(first_half, pltpu.roll(t, LANES - HEAD_DIM // 2, 1),
                                            pltpu.roll(t, HEAD_DIM // 2, 1))
                        t = t * cos + partner * sin
                    o_ref[kind, r, :, c * LANES:(c + 1) * LANES] = t.astype(BF16)


def _qkv_project(x, gain, w_qkv, tables, batch, seq):
    n, d = x.shape
    tm = QKV_ROWS
    tiles_per_seq = seq // tm
    tab = pl.BlockSpec((2, tm, LANES), lambda i: (0, i % tiles_per_seq, 0))

    def o_spec(dil):
        return pl.BlockSpec((3, None, dil, tm // dil, d),
                            lambda i: (0, i // tiles_per_seq, 0, i % tiles_per_seq, 0))

    outs = pl.pallas_call(
        _qkv_kernel,
        grid=(n // tm,),
        in_specs=[
            pl.BlockSpec((tm, d), lambda i: (i, 0)),
            pl.BlockSpec((1, d), lambda i: (0, 0)),
            pl.BlockSpec(w_qkv.shape, lambda i: (0, 0), pipeline_mode=pl.Buffered(1)),
            tab, tab,
        ],
        out_specs=[o_spec(dil) for dil in DILATIONS],
        out_shape=[jax.ShapeDtypeStruct((3, batch, dil, seq // dil, d), BF16) for dil in DILATIONS],
        scratch_shapes=[pltpu.VMEM((3, d // LANES, tm, LANES), F32)],
        compiler_params=_params("parallel"),
        name="qkv_rope",
    )(x, gain, w_qkv, *tables)
    return [o.reshape(3, batch, seq, d) for o in outs]


ATTN_TILES_PER_FFN_PIECE = 3


def _attn_dense_kernel(q0, k0, v0, q1, k1, v1, q2, k2, v2, band_ref, causal_ref,
                       x_ref, wo_ref, g_ref, wg_ref, wu_ref, wd_ref, y_ref,
                       o_scr, num_ref, max_ref, den_ref, kt_ref):
    b = pl.program_id(0)
    hp = pl.program_id(1)
    n_pairs, seq = o_scr.shape[1], o_scr.shape[2]
    w = WINDOWS[0] // DILATIONS[0]
    lane = lax.broadcasted_iota(jnp.int32, (1, LANES), 1)
    first_head = lane < HEAD_DIM
    o_out = o_scr.at[b % 2, hp]
    o_in = o_scr.at[(b + 1) % 2]

    @pl.when((b == 0) & (hp == 0))
    def _():
        o_scr[1] = jnp.zeros(o_scr.shape[1:], o_scr.dtype)

    def ffn_pieces():
        rows = pl.ds(pl.multiple_of(hp * x_ref.shape[0], x_ref.shape[0]), x_ref.shape[0])
        o = jnp.concatenate([o_in[p, rows, :] for p in range(n_pairs)], axis=1)
        x = x_ref[...] + jnp.dot(o, wo_ref[...], preferred_element_type=F32)
        h = _rms(x, g_ref[...]).astype(BF16)
        yield
        acts = []
        n_chunks = wg_ref.shape[1] // MXU_DIM
        for c in range(n_chunks):
            cols = slice(c * MXU_DIM, (c + 1) * MXU_DIM)
            gate = jnp.dot(h, wg_ref[:, cols], preferred_element_type=F32)
            up = jnp.dot(h, wu_ref[:, cols], preferred_element_type=F32)
            acts.append((_silu(gate) * up).astype(BF16))
            yield
        half = n_chunks // 2
        acc = jnp.dot(jnp.concatenate(acts[:half], axis=1), wd_ref[:half * MXU_DIM, :], preferred_element_type=F32)
        yield
        acc = acc + jnp.dot(jnp.concatenate(acts[half:], axis=1), wd_ref[half * MXU_DIM:, :],
                            preferred_element_type=F32)
        y_ref[...] = x + acc
        yield

    def tile(q_ref, v_ref, qoff, has_prev):
        q = q_ref[qoff:qoff + w, :]
        zero = jnp.zeros_like(q)
        qq = jnp.concatenate([jnp.where(first_head, q, zero), jnp.where(first_head, zero, q)], axis=0)
        k_lo = qoff - w if has_prev else qoff
        s = jnp.dot(qq, kt_ref[:, k_lo:qoff + w], preferred_element_type=F32)
        s = s + (band_ref[...] if has_prev else causal_ref[...])
        mx = jnp.max(s, axis=1, keepdims=True)
        p = jnp.exp2(s - mx)
        den = jnp.sum(p, axis=1, keepdims=True)
        pv = jnp.dot(p.astype(BF16), v_ref[k_lo:qoff + w, :], preferred_element_type=F32)
        return tuple(jnp.where(first_head, a[:w], a[w:]) for a in (pv, mx, den))

    pieces = ffn_pieces()
    tiles_done = 0
    order = sorted(range(N_GROUPS), key=lambda g: -DILATIONS[g])
    for g in order:
        q_ref, k_ref, v_ref = ((q0, k0, v0), (q1, k1, v1), (q2, k2, v2))[g]
        dil = DILATIONS[g]
        sub_len = seq // dil
        nb = sub_len // w
        kt_ref[...] = k_ref[...].T
        for t in range(dil * nb):
            r, n = divmod(t, nb)
            qoff = r * sub_len + n * w
            pv, mx, den = tile(q_ref, v_ref, qoff, n > 0)
            tiles_done += 1
            if tiles_done % ATTN_TILES_PER_FFN_PIECE == 0:
                next(pieces, None)
            rows = pl.ds(qoff, w) if dil == 1 else pl.ds(r + dil * n * w, w, stride=dil)
            if g == order[0]:
                num_ref[rows, :] = pv
                max_ref[rows, :] = mx
                den_ref[rows, :] = den
                continue
            old_max = max_ref[rows, :]
            new_max = jnp.maximum(old_max, mx)
            a = jnp.exp2(old_max - new_max)
            c = jnp.exp2(mx - new_max)
            num = a * num_ref[rows, :] + c * pv
            den = a * den_ref[rows, :] + c * den
            if g == order[-1]:
                o_out[rows, :] = (num / den).astype(BF16)
            else:
                num_ref[rows, :] = num
                den_ref[rows, :] = den
                max_ref[rows, :] = new_max
    for _ in pieces:
        pass


def _attention_biases(w):
    row = jnp.arange(2 * w)[:, None] % w
    col = jnp.arange(2 * w)[None, :]
    dist = row + w - col
    band = jnp.where((dist >= 0) & (dist <= w), 0.0, MASK_VALUE).astype(F32)
    causal = jnp.where(row >= col[:, :w], 0.0, MASK_VALUE).astype(F32)
    return band, causal


def _attention_and_dense_ffn(qkv_groups, x, w_o, gain, w_gate, w_up, w_down, batch, seq):
    n, d = x.shape
    ff = w_gate.shape[1]
    n_pairs = d // LANES
    rows = seq // n_pairs
    last = batch - 1
    in_specs, args = [], []
    for arr in qkv_groups:
        for kind in range(3):
            in_specs.append(pl.BlockSpec((None, None, seq, LANES),
                                         lambda b, hp, kind=kind: (kind, jnp.minimum(b, last), 0, hp)))
            args.append(arr)
    const = lambda shape: pl.BlockSpec(shape, lambda b, hp: (0, 0), pipeline_mode=pl.Buffered(1))
    ffn_in = pl.BlockSpec((rows, d), lambda b, hp: (jnp.maximum(b - 1, 0) * n_pairs + hp, 0))
    ffn_out = pl.BlockSpec((rows, d), lambda b, hp: (jnp.where(b == 0, batch, b - 1) * n_pairs + hp, 0))
    biases = _attention_biases(WINDOWS[0] // DILATIONS[0])
    in_specs += [const(t.shape) for t in biases]
    in_specs += [ffn_in, const((d, d)), const((1, d)), const((d, ff)), const((d, ff)), const((ff, d))]
    args += [*biases, x, w_o, gain, w_gate, w_up, w_down]
    return pl.pallas_call(
        _attn_dense_kernel,
        grid=(batch + 1, n_pairs),
        in_specs=in_specs,
        out_specs=ffn_out,
        out_shape=jax.ShapeDtypeStruct((n + seq, d), F32),
        scratch_shapes=[pltpu.VMEM((2, n_pairs, seq, LANES), BF16)]
        + [pltpu.VMEM((seq, LANES), F32)] * 3 + [pltpu.VMEM((LANES, seq), BF16)],
        compiler_params=_params("arbitrary", "arbitrary"),
        name="attention_ffn",
    )(*args)


POOL_HALO = max(POOL_SIZES)


def _pool_tile(i, x_ref, halo_ref, g_ref, w_ref, sc_ref, y_ref):
    ts = x_ref.shape[0]
    g = g_ref[...]
    x = x_ref[...]
    h = _rms(x, g)
    hh = jnp.where(i > 0, _rms(halo_ref[...], g), 0.0)
    ext = jnp.concatenate([hh, h], axis=0)
    pos = i * ts + lax.broadcasted_iota(jnp.int32, (ts, 1), 0)
    pc = w_ref.shape[1]
    for grp, size in enumerate(POOL_SIZES):
        sl = slice(grp * pc, (grp + 1) * pc)
        s = ext[:, sl]
        step = 1
        while step < size:
            s = s + pltpu.roll(s, step, 0)
            step *= 2
        cnt = jnp.minimum(pos + 1, size).astype(F32)
        y = s[POOL_HALO:, :] / cnt - h[:, sl]
        z = jnp.dot(y.astype(BF16), w_ref[grp], preferred_element_type=F32)
        y_ref[:, sl] = x[:, sl] + z * sc_ref[:, sl]


def _pool_router_kernel(x_ref, halo_ref, g_ref, w_ref, sc_ref, g2_ref, wr_ref, tril_ref,
                        y_ref, ri_ref, rg_ref, cnt_ref, hp_ref, run_ref):
    b = pl.program_id(0)
    i = pl.program_id(1)
    _pool_tile(i, x_ref, halo_ref, g_ref, w_ref, sc_ref, y_ref)
    _route_tile((b == 0) & (i == 0), y_ref[...], g2_ref, wr_ref, tril_ref, ri_ref, rg_ref, cnt_ref, hp_ref, run_ref)


def _pool_mixer_and_router(x, gain, pool_w, scale, ffn_gain, w_router, batch, seq):
    d = x.shape[1]
    n = batch * seq
    ne = w_router.shape[1]
    ts = 512
    tiles = seq // ts
    x3 = x.reshape(-1, seq, d)
    hb = ts // POOL_HALO
    tril = (jnp.arange(ts)[:, None] > jnp.arange(ts)[None, :]).astype(BF16)
    const = lambda shape: pl.BlockSpec(shape, lambda b, i: (0,) * len(shape))
    tok = lambda cols: pl.BlockSpec((ts, cols), lambda b, i: (b * tiles + i, 0))
    out, route_i, route_g, counts, h_packed = pl.pallas_call(
        _pool_router_kernel,
        grid=(batch, tiles),
        in_specs=[
            pl.BlockSpec((None, ts, d), lambda b, i: (b, i, 0)),
            pl.BlockSpec((None, POOL_HALO, d), lambda b, i: (b, jnp.maximum(i * hb - 1, 0), 0)),
            const((1, d)), const(pool_w.shape), const((1, d)), const((1, d)), const((d, ne)), const((ts, ts)),
        ],
        out_specs=[
            pl.BlockSpec((None, ts, d), lambda b, i: (b, i, 0)),
            tok(ROUTE_COLS), tok(ROUTE_COLS), const((1, ne)), tok(d // 2),
        ],
        out_shape=[
            jax.ShapeDtypeStruct((batch, seq, d), F32),
            jax.ShapeDtypeStruct((n, ROUTE_COLS), jnp.int32),
            jax.ShapeDtypeStruct((n, ROUTE_COLS), F32),
            jax.ShapeDtypeStruct((1, ne), F32),
            jax.ShapeDtypeStruct((n, d // 2), jnp.int32),
        ],
        scratch_shapes=[pltpu.VMEM((1, ne), F32)],
        compiler_params=_params("arbitrary", "arbitrary"),
        name="pool_router",
    )(x3, x3, gain, pool_w, scale, ffn_gain, w_router, tril)
    return out.reshape(n, d), route_i, route_g, counts, h_packed


ROUTE_COLS = 8


def _pack_bf16_pairs(hr):
    c = hr.shape[1] // 2
    bits = pltpu.bitcast(hr, jnp.int32)
    return (bits[:, c:] & jnp.int32(-65536)) | lax.shift_right_logical(bits[:, :c], 16)


def _unpack_bf16_pairs(packed):
    lo = pltpu.bitcast(lax.shift_left(packed, 16), F32)
    hi = pltpu.bitcast(packed & jnp.int32(-65536), F32)
    return jnp.concatenate([lo.astype(BF16), hi.astype(BF16)], axis=1)


def _route_tile(first, x, g_ref, wr_ref, tril_ref, ri_ref, rg_ref, cnt_ref, hp_ref, run_ref):
    @pl.when(first)
    def _():
        run_ref[...] = jnp.zeros_like(run_ref)

    h = _rms(x, g_ref[...])
    wr = wr_ref[...]
    ne = wr.shape[1]
    h_hi = h.astype(BF16)
    hp_ref[...] = _pack_bf16_pairs(h_hi.astype(F32))
    h_lo = (h - h_hi.astype(F32)).astype(BF16)
    w_hi = wr.astype(BF16)
    w_lo = (wr - w_hi.astype(F32)).astype(BF16)
    hi_part = jnp.dot(h_hi, jnp.concatenate([w_hi, w_lo], axis=1), preferred_element_type=F32)
    logits = hi_part[:, :ne] + hi_part[:, ne:] + jnp.dot(h_lo, w_hi, preferred_element_type=F32)
    tm = logits.shape[0]
    lane = lax.broadcasted_iota(jnp.int32, (tm, ne), 1)
    v1 = jnp.max(logits, axis=1, keepdims=True)
    i1 = jnp.min(jnp.where(logits == v1, lane, ne), axis=1, keepdims=True)
    rest = jnp.where(lane == i1, -jnp.inf, logits)
    v2 = jnp.max(rest, axis=1, keepdims=True)
    i2 = jnp.min(jnp.where(rest == v2, lane, ne), axis=1, keepdims=True)
    e = jnp.exp(v2 - v1)
    g1 = 1.0 / (1.0 + e)
    g2 = e / (1.0 + e)
    oh1 = (lane == i1).astype(F32)
    oh2 = (lane == i2).astype(F32)
    both = oh1 + oh2
    before = jnp.dot(tril_ref[...], both.astype(BF16), preferred_element_type=F32) + run_ref[...]
    r1 = jnp.sum(before * oh1, axis=1, keepdims=True).astype(jnp.int32)
    r2 = jnp.sum(before * oh2, axis=1, keepdims=True).astype(jnp.int32)
    run_ref[...] += jnp.sum(both, axis=0, keepdims=True)
    ri_ref[...] = jnp.where(lane == 0, i1, jnp.where(lane == 1, i2, jnp.where(lane == 2, r1, r2)))
    rg_ref[...] = jnp.where(lane == 0, g1, g2)
    cnt_ref[...] = run_ref[...]


SC_WINDOW = 64


def _sc_workers():
    info = plsc.get_sparse_core_info()
    mesh = plsc.VectorSubcoreMesh(core_axis_name="core", subcore_axis_name="subcore")
    return mesh, info.num_cores, info.num_cores * info.num_subcores


def _dispatch(x, dest, cap):
    n, d = x.shape
    mesh, n_cores, n_workers = _sc_workers()
    per_worker = n // n_workers

    @functools.partial(
        pl.kernel, mesh=mesh, out_type=jax.ShapeDtypeStruct((cap, d), x.dtype),
        scratch_types=[pltpu.VMEM((SC_WINDOW,), jnp.int32)] * TOP_K
        + [pltpu.VMEM((SC_WINDOW, d), x.dtype), pltpu.SemaphoreType.DMA],
        name="moe_dispatch")
    def scatter(x_hbm, *rest):
        idx_hbm, (xs_hbm, *idx_vmem, rows, sem) = rest[:TOP_K], rest[TOP_K:]
        wid = lax.axis_index("subcore") * n_cores + lax.axis_index("core")

        @pl.loop(0, per_worker // SC_WINDOW)
        def _(c):
            base = wid * per_worker + c * SC_WINDOW
            for k in range(TOP_K):
                pltpu.sync_copy(idx_hbm[k].at[pl.ds(base, SC_WINDOW)], idx_vmem[k])
            pltpu.sync_copy(x_hbm.at[pl.ds(base, SC_WINDOW)], rows)
            for k in range(TOP_K):
                pltpu.async_copy(rows, xs_hbm.at[idx_vmem[k]], sem).wait()

    return scatter(x, *[dest[k] for k in range(TOP_K)])


def _gather_rows(table, idx):
    m = idx.shape[0]
    d = table.shape[1]
    mesh, n_cores, n_workers = _sc_workers()
    per_worker = m // n_workers

    @functools.partial(
        pl.kernel, mesh=mesh, out_type=jax.ShapeDtypeStruct((m, d), table.dtype),
        scratch_types=[pltpu.VMEM((SC_WINDOW,), jnp.int32), pltpu.VMEM((SC_WINDOW, d), table.dtype),
                       pltpu.SemaphoreType.DMA],
        name="moe_gather")
    def gather(t_hbm, i_hbm, o_hbm, idx_vmem, rows, sem):
        wid = lax.axis_index("subcore") * n_cores + lax.axis_index("core")

        @pl.loop(0, per_worker // SC_WINDOW)
        def _(c):
            base = wid * per_worker + c * SC_WINDOW
            pltpu.sync_copy(i_hbm.at[pl.ds(base, SC_WINDOW)], idx_vmem)
            pltpu.async_copy(t_hbm.at[idx_vmem], rows, sem).wait()
            pltpu.sync_copy(rows, o_hbm.at[pl.ds(base, SC_WINDOW)])

    return gather(table, idx)


def _expert_kernel(be_ref, nv_ref, xs_ref, wg_ref, wu_ref, wd_ref, ys_ref, xb_ref, acc_ref):
    del be_ref
    b = pl.program_id(0)
    j = pl.program_id(1)

    @pl.when(j == 0)
    def _():
        acc_ref[...] = jnp.zeros_like(acc_ref)

    @pl.when(b < nv_ref[0])
    def _():
        @pl.when(j == 0)
        def _():
            xb_ref[...] = _unpack_bf16_pairs(xs_ref[...])

        x = xb_ref[...]
        acts = []
        for c in range(wg_ref.shape[1] // MXU_DIM):
            cols = slice(c * MXU_DIM, (c + 1) * MXU_DIM)
            gate = jnp.dot(x, wg_ref[:, cols].astype(BF16), preferred_element_type=F32)
            up = jnp.dot(x, wu_ref[:, cols].astype(BF16), preferred_element_type=F32)
            acts.append((_silu(gate) * up).astype(BF16))
        a = jnp.concatenate(acts, axis=1)
        acc_ref[...] += jnp.dot(a, wd_ref[...].astype(BF16), preferred_element_type=F32)

    @pl.when(j == pl.num_programs(1) - 1)
    def _():
        ys_ref[...] = _pack_bf16_pairs(acc_ref[...].astype(BF16).astype(F32))


def _experts(xs, block_e, n_valid, w_gate, w_up, w_down, layer, tm, tf):
    cap = xs.shape[0]
    d = 2 * xs.shape[1]
    ff = w_gate.shape[3]
    nj = ff // tf

    def blk(b, nv):
        return jnp.minimum(b, nv[0] - 1)

    def ffi(b, j, nv):
        return jnp.where(b < nv[0], j, nj - 1)

    return pl.pallas_call(
        _expert_kernel,
        grid_spec=pltpu.PrefetchScalarGridSpec(
            num_scalar_prefetch=2,
            grid=(cap // tm, nj),
            in_specs=[
                pl.BlockSpec((tm, d // 2), lambda b, j, be, nv: (blk(b, nv), 0)),
                pl.BlockSpec((None, None, d, tf), lambda b, j, be, nv: (layer, be[blk(b, nv)], 0, ffi(b, j, nv))),
                pl.BlockSpec((None, None, d, tf), lambda b, j, be, nv: (layer, be[blk(b, nv)], 0, ffi(b, j, nv))),
                pl.BlockSpec((None, None, tf, d), lambda b, j, be, nv: (layer, be[blk(b, nv)], ffi(b, j, nv), 0)),
            ],
            out_specs=pl.BlockSpec((tm, d // 2), lambda b, j, be, nv: (b, 0)),
            scratch_shapes=[pltpu.VMEM((tm, d), BF16), pltpu.VMEM((tm, d), F32)],
        ),
        out_shape=jax.ShapeDtypeStruct((cap, d // 2), jnp.int32),
        compiler_params=_params("arbitrary", "arbitrary"),
        name="moe_experts",
    )(block_e, n_valid, xs, w_gate, w_up, w_down)


def _combine_kernel(x_ref, rg_ref, gf_ref, y0_ref, y1_ref, o_ref, *, final_norm):
    rg = rg_ref[...]
    y0 = _unpack_bf16_pairs(y0_ref[...]).astype(F32)
    y1 = _unpack_bf16_pairs(y1_ref[...]).astype(F32)
    out = x_ref[...] + (rg[:, 0:1] * y0 + rg[:, 1:2] * y1)
    if final_norm:
        out = _rms(out, gf_ref[...])
    o_ref[...] = out


def _combine(x, route_g, yg, final_gain, final_norm):
    n, d = x.shape
    tm = 512
    return pl.pallas_call(
        functools.partial(_combine_kernel, final_norm=final_norm),
        grid=(n // tm,),
        in_specs=[
            pl.BlockSpec((tm, d), lambda i: (i, 0)),
            pl.BlockSpec((tm, ROUTE_COLS), lambda i: (i, 0)),
            pl.BlockSpec((1, d), lambda i: (0, 0)),
            pl.BlockSpec((None, tm, d // 2), lambda i: (0, i, 0)),
            pl.BlockSpec((None, tm, d // 2), lambda i: (1, i, 0)),
        ],
        out_specs=pl.BlockSpec((tm, d), lambda i: (i, 0)),
        out_shape=jax.ShapeDtypeStruct((n, d), F32),
        compiler_params=_params("parallel"),
        name="moe_combine",
    )(x, route_g, final_gain, yg, yg)


def _moe_layer(x, routing, w_gate, w_up, w_down, layer, final_gain, final_norm):
    n, d = x.shape
    tm = 1024 if n * TOP_K >= 8 * 1024 else 256
    tf = 2 * MXU_DIM
    route_i, route_g, counts, h_packed = routing

    counts = counts[0].astype(jnp.int32)
    padded = ((counts + tm - 1) // tm) * tm
    ends = jnp.cumsum(padded)
    starts = ends - padded
    dest = (starts[route_i[:, :TOP_K]] + route_i[:, TOP_K:2 * TOP_K]).T
    n_blocks = (n * TOP_K) // tm + N_EXPERTS
    block_start = jnp.arange(n_blocks, dtype=jnp.int32) * tm
    block_e = jnp.minimum(jnp.sum(ends[None, :] <= block_start[:, None], axis=1), N_EXPERTS - 1).astype(jnp.int32)
    n_valid = (ends[-1:] // tm).astype(jnp.int32)

    xs = _dispatch(h_packed, dest, n_blocks * tm)
    ys = _experts(xs, block_e, n_valid, w_gate, w_up, w_down, layer, tm, tf)
    yg = _gather_rows(ys, dest.reshape(TOP_K * n)).reshape(TOP_K, n, d // 2)
    return _combine(x, route_g, yg, final_gain, final_norm)


def _rope_tables(seq):
    pos = jnp.arange(seq, dtype=F32)
    inv_freq = ROPE_THETA ** (-jnp.arange(0, HEAD_DIM, 2, dtype=F32) / HEAD_DIM)
    ang = pos[:, None] * inv_freq[None, :]
    reps = LANES // (HEAD_DIM // 2)
    cos = jnp.tile(jnp.cos(ang), (1, reps))
    sin = jnp.tile(jnp.sin(ang), (1, reps))
    first_half = (jnp.arange(LANES) % HEAD_DIM) < HEAD_DIM // 2
    tabs = (cos, jnp.where(first_half, -sin, sin))
    q_scale = HEAD_DIM ** -0.5 * 1.4426950408889634
    return tuple(jnp.stack([t * q_scale, t]) for t in tabs)


def kernel(x, mix_norm, ffn_norm, attn_w_qkv, attn_w_o, pool_w, pool_scale, dense_w_gate, dense_w_up,
           dense_w_down, moe_router, moe_w_gate, moe_w_up, moe_w_down, final_norm):
    batch, seq, d = x.shape
    depth = mix_norm.shape[0]
    assert depth % 2 == 0 and d == N_HEADS * HEAD_DIM
    rope_tables = _rope_tables(seq)
    xf = x.reshape(batch * seq, d)
    final_gain = final_norm.reshape(1, d)
    for i in range(depth):
        j = i // 2
        mix_gain = mix_norm[i].reshape(1, d)
        ffn_gain = ffn_norm[i].reshape(1, d)
        if i % 2 == 0:
            w_qkv = attn_w_qkv[j].astype(BF16)
            qkv = _qkv_project(xf, mix_gain, w_qkv, rope_tables, batch, seq)
            xf = _attention_and_dense_ffn(qkv, xf, attn_w_o[j].astype(BF16), ffn_gain, dense_w_gate[j].astype(BF16),
                                          dense_w_up[j].astype(BF16), dense_w_down[j].astype(BF16), batch, seq)
        else:
            xf, *routing = _pool_mixer_and_router(xf, mix_gain, pool_w[j].astype(BF16), pool_scale[j].reshape(1, d),
                                                  ffn_gain, moe_router[j], batch, seq)
            xf = _moe_layer(xf, routing, moe_w_gate, moe_w_up, moe_w_down, j, final_gain, i == depth - 1)
    return xf.reshape(batch, seq, d)
```

```python
import functools

import jax
import jax.numpy as jnp
from jax import lax
from jax.experimental import pallas as pl
from jax.experimental.pallas import tpu as pltpu
from jax.experimental.pallas import tpu_sc as plsc

WINDOWS = (128, 512, 2048)
DILATIONS = (1, 4, 16)
N_GROUPS = len(WINDOWS)
N_HEADS = 16
HEAD_DIM = 64
ROPE_THETA = 10000.0
POOL_SIZES = (2, 4, 8, 16)
N_EXPERTS = 8
TOP_K = 2
RMS_EPS = 1e-6
MASK_VALUE = -1e30
QKV_ROWS = 512

LANES = 128
MXU_DIM = 256
VMEM_LIMIT_BYTES = 56 * 1024 * 1024

F32 = jnp.float32
BF16 = jnp.bfloat16


def _params(*sem):
    return pltpu.CompilerParams(dimension_semantics=sem, vmem_limit_bytes=VMEM_LIMIT_BYTES)


def _rms(x, g):
    ms = jnp.mean(x * x, axis=-1, keepdims=True)
    return x * lax.rsqrt(ms + RMS_EPS) * g


def _silu(g):
    return g / (1.0 + jnp.exp(-g))


def _qkv_kernel(x_ref, g_ref, w_ref, cos_ref, sin_ref, o0_ref, o1_ref, o2_ref, acc_ref):
    tm, d = x_ref.shape
    n_chunks = acc_ref.shape[1]
    lane = lax.broadcasted_iota(jnp.int32, (1, LANES), 1)
    first_half = (lane % HEAD_DIM) < HEAD_DIM // 2
    h = _rms(x_ref[...], g_ref[...]).astype(BF16)
    for g, (o_ref, dil) in enumerate(zip((o0_ref, o1_ref, o2_ref), DILATIONS)):
        rows = tm // dil

        def strided(ref, r, dil=dil, rows=rows):
            if dil == 1:
                return ref[...]
            return ref[pl.ds(r, rows, stride=dil), :]

        for kind in range(3):
            col0 = (g * 3 + kind) * d
            acc = jnp.dot(h, w_ref[:, col0:col0 + d], preferred_element_type=F32)
            for c in range(n_chunks):
                acc_ref[kind, c] = acc[:, c * LANES:(c + 1) * LANES]
            for r in range(dil):
                if kind < 2:
                    cos = strided(cos_ref.at[kind], r)
                    sin = strided(sin_ref.at[kind], r)
                for c in range(n_chunks):
                    t = strided(acc_ref.at[kind, c], r)
                    if kind < 2:
                        partner = jnp.where(first_half, pltpu.roll(t, LANES - HEAD_DIM // 2, 1),
                                            pltpu.roll(t, HEAD_DIM // 2, 1))
                        t = t * cos + partner * sin
                    o_ref[kind, r, :, c * LANES:(c + 1) * LANES] = t.astype(BF16)


def _qkv_project(x, gain, w_qkv, tables, batch, seq):
    n, d = x.shape
    tm = QKV_ROWS
    tiles_per_seq = seq // tm
    tab = pl.BlockSpec((2, tm, LANES), lambda i: (0, i % tiles_per_seq, 0))

    def o_spec(dil):
        return pl.BlockSpec((3, None, dil, tm // dil, d),
                            lambda i: (0, i // tiles_per_seq, 0, i % tiles_per_seq, 0))

    outs = pl.pallas_call(
        _qkv_kernel,
        grid=(n // tm,),
        in_specs=[
            pl.BlockSpec((tm, d), lambda i: (i, 0)),
            pl.BlockSpec((1, d), lambda i: (0, 0)),
            pl.BlockSpec(w_qkv.shape, lambda i: (0, 0), pipeline_mode=pl.Buffered(1)),
            tab, tab,
        ],
        out_specs=[o_spec(dil) for dil in DILATIONS],
        out_shape=[jax.ShapeDtypeStruct((3, batch, dil, seq // dil, d), BF16) for dil in DILATIONS],
        scratch_shapes=[pltpu.VMEM((3, d // LANES, tm, LANES), F32)],
        compiler_params=_params("parallel"),
        name="qkv_rope",
    )(x, gain, w_qkv, *tables)
    return [o.reshape(3, batch, seq, d) for o in outs]


ATTN_TILES_PER_FFN_PIECE = 3


def _attn_dense_kernel(q0, k0, v0, q1, k1, v1, q2, k2, v2, band_ref, causal_ref,
                       x_ref, wo_ref, g_ref, wg_ref, wu_ref, wd_ref, y_ref,
                       o_scr, num_ref, max_ref, den_ref, kt_ref):
    b = pl.program_id(0)
    hp = pl.program_id(1)
    n_pairs, seq = o_scr.shape[1], o_scr.shape[2]
    w = WINDOWS[0] // DILATIONS[0]
    lane = lax.broadcasted_iota(jnp.int32, (1, LANES), 1)
    first_head = lane < HEAD_DIM
    o_out = o_scr.at[b % 2, hp]
    o_in = o_scr.at[(b + 1) % 2]

    @pl.when((b == 0) & (hp == 0))
    def _():
        o_scr[1] = jnp.zeros(o_scr.shape[1:], o_scr.dtype)

    def ffn_pieces():
        rows = pl.ds(pl.multiple_of(hp * x_ref.shape[0], x_ref.shape[0]), x_ref.shape[0])
        o = jnp.concatenate([o_in[p, rows, :] for p in range(n_pairs)], axis=1)
        x = x_ref[...] + jnp.dot(o, wo_ref[...], preferred_element_type=F32)
        h = _rms(x, g_ref[...]).astype(BF16)
        yield
        acts = []
        n_chunks = wg_ref.shape[1] // MXU_DIM
        for c in range(n_chunks):
            cols = slice(c * MXU_DIM, (c + 1) * MXU_DIM)
            gate = jnp.dot(h, wg_ref[:, cols], preferred_element_type=F32)
            up = jnp.dot(h, wu_ref[:, cols], preferred_element_type=F32)
            acts.append((_silu(gate) * up).astype(BF16))
            yield
        half = n_chunks // 2
        acc = jnp.dot(jnp.concatenate(acts[:half], axis=1), wd_ref[:half * MXU_DIM, :], preferred_element_type=F32)
        yield
        acc = acc + jnp.dot(jnp.concatenate(acts[half:], axis=1), wd_ref[half * MXU_DIM:, :],
                            preferred_element_type=F32)
        y_ref[...] = x + acc
        yield

    def tile(q_ref, v_ref, qoff, has_prev):
        q = q_ref[qoff:qoff + w, :]
        zero = jnp.zeros_like(q)
        qq = jnp.concatenate([jnp.where(first_head, q, zero), jnp.where(first_head, zero, q)], axis=0)
        k_lo = qoff - w if has_prev else qoff
        s = jnp.dot(qq, kt_ref[:, k_lo:qoff + w], preferred_element_type=F32)
        s = s + (band_ref[...] if has_prev else causal_ref[...])
        mx = jnp.max(s, axis=1, keepdims=True)
        p = jnp.exp2(s - mx)
        den = jnp.sum(p, axis=1, keepdims=True)
        pv = jnp.dot(p.astype(BF16), v_ref[k_lo:qoff + w, :], preferred_element_type=F32)
        return tuple(jnp.where(first_head, a[:w], a[w:]) for a in (pv, mx, den))

    pieces = ffn_pieces()
    tiles_done = 0
    order = sorted(range(N_GROUPS), key=lambda g: -DILATIONS[g])
    for g in order:
        q_ref, k_ref, v_ref = ((q0, k0, v0), (q1, k1, v1), (q2, k2, v2))[g]
        dil = DILATIONS[g]
        sub_len = seq // dil
        nb = sub_len // w
        kt_ref[...] = k_ref[...].T
        for t in range(dil * nb):
            r, n = divmod(t, nb)
            qoff = r * sub_len + n * w
            pv, mx, den = tile(q_ref, v_ref, qoff, n > 0)
            tiles_done += 1
            if tiles_done % ATTN_TILES_PER_FFN_PIECE == 0:
                next(pieces, None)
            rows = pl.ds(qoff, w) if dil == 1 else pl.ds(r + dil * n * w, w, stride=dil)
            if g == order[0]:
                num_ref[rows, :] = pv
                max_ref[rows, :] = mx
                den_ref[rows, :] = den
                continue
            old_max = max_ref[rows, :]
            new_max = jnp.maximum(old_max, mx)
            a = jnp.exp2(old_max - new_max)
            c = jnp.exp2(mx - new_max)
            num = a * num_ref[rows, :] + c * pv
            den = a * den_ref[rows, :] + c * den
            if g == order[-1]:
                o_out[rows, :] = (num / den).astype(BF16)
            else:
                num_ref[rows, :] = num
                den_ref[rows, :] = den
                max_ref[rows, :] = new_max
    for _ in pieces:
        pass


def _attention_biases(w):
    row = jnp.arange(2 * w)[:, None] % w
    col = jnp.arange(2 * w)[None, :]
    dist = row + w - col
    band = jnp.where((dist >= 0) & (dist <= w), 0.0, MASK_VALUE).astype(F32)
    causal = jnp.where(row >= col[:, :w], 0.0, MASK_VALUE).astype(F32)
    return band, causal


def _attention_and_dense_ffn(qkv_groups, x, w_o, gain, w_gate, w_up, w_down, batch, seq):
    n, d = x.shape
    ff = w_gate.shape[1]
    n_pairs = d // LANES
    rows = seq // n_pairs
    last = batch - 1
    in_specs, args = [], []
    for arr in qkv_groups:
        for kind in range(3):
            in_specs.append(pl.BlockSpec((None, None, seq, LANES),
                                         lambda b, hp, kind=kind: (kind, jnp.minimum(b, last), 0, hp)))
            args.append(arr)
    const = lambda shape: pl.BlockSpec(shape, lambda b, hp: (0, 0), pipeline_mode=pl.Buffered(1))
    ffn_in = pl.BlockSpec((rows, d), lambda b, hp: (jnp.maximum(b - 1, 0) * n_pairs + hp, 0))
    ffn_out = pl.BlockSpec((rows, d), lambda b, hp: (jnp.where(b == 0, batch, b - 1) * n_pairs + hp, 0))
    biases = _attention_biases(WINDOWS[0] // DILATIONS[0])
    in_specs += [const(t.shape) for t in biases]
    in_specs += [ffn_in, const((d, d)), const((1, d)), const((d, ff)), const((d, ff)), const((ff, d))]
    args += [*biases, x, w_o, gain, w_gate, w_up, w_down]
    return pl.pallas_call(
        _attn_dense_kernel,
        grid=(batch + 1, n_pairs),
        in_specs=in_specs,
        out_specs=ffn_out,
        out_shape=jax.ShapeDtypeStruct((n + seq, d), F32),
        scratch_shapes=[pltpu.VMEM((2, n_pairs, seq, LANES), BF16)]
        + [pltpu.VMEM((seq, LANES), F32)] * 3 + [pltpu.VMEM((LANES, seq), BF16)],
        compiler_params=_params("arbitrary", "arbitrary"),
        name="attention_ffn",
    )(*args)


POOL_HALO = max(POOL_SIZES)


def _pool_tile(i, x_ref, halo_ref, g_ref, w_ref, sc_ref, y_ref):
    ts = x_ref.shape[0]
    g = g_ref[...]
    x = x_ref[...]
    h = _rms(x, g)
    hh = jnp.where(i > 0, _rms(halo_ref[...], g), 0.0)
    ext = jnp.concatenate([hh, h], axis=0)
    pos = i * ts + lax.broadcasted_iota(jnp.int32, (ts, 1), 0)
    pc = w_ref.shape[1]
    for grp, size in enumerate(POOL_SIZES):
        sl = slice(grp * pc, (grp + 1) * pc)
        s = ext[:, sl]
        step = 1
        while step < size:
            s = s + pltpu.roll(s, step, 0)
            step *= 2
        cnt = jnp.minimum(pos + 1, size).astype(F32)
        y = s[POOL_HALO:, :] / cnt - h[:, sl]
        z = jnp.dot(y.astype(BF16), w_ref[grp], preferred_element_type=F32)
        y_ref[:, sl] = x[:, sl] + z * sc_ref[:, sl]


def _pool_router_kernel(x_ref, halo_ref, g_ref, w_ref, sc_ref, g2_ref, wr_ref, tril_ref,
                        y_ref, ri_ref, rg_ref, cnt_ref, hp_ref, run_ref):
    b = pl.program_id(0)
    i = pl.program_id(1)
    _pool_tile(i, x_ref, halo_ref, g_ref, w_ref, sc_ref, y_ref)
    _route_tile((b == 0) & (i == 0), y_ref[...], g2_ref, wr_ref, tril_ref, ri_ref, rg_ref, cnt_ref, hp_ref, run_ref)


def _pool_mixer_and_router(x, gain, pool_w, scale, ffn_gain, w_router, batch, seq):
    d = x.shape[1]
    n = batch * seq
    ne = w_router.shape[1]
    ts = 512
    tiles = seq // ts
    x3 = x.reshape(-1, seq, d)
    hb = ts // POOL_HALO
    tril = (jnp.arange(ts)[:, None] > jnp.arange(ts)[None, :]).astype(BF16)
    const = lambda shape: pl.BlockSpec(shape, lambda b, i: (0,) * len(shape))
    tok = lambda cols: pl.BlockSpec((ts, cols), lambda b, i: (b * tiles + i, 0))
    out, route_i, route_g, counts, h_packed = pl.pallas_call(
        _pool_router_kernel,
        grid=(batch, tiles),
        in_specs=[
            pl.BlockSpec((None, ts, d), lambda b, i: (b, i, 0)),
            pl.BlockSpec((None, POOL_HALO, d), lambda b, i: (b, jnp.maximum(i * hb - 1, 0), 0)),
            const((1, d)), const(pool_w.shape), const((1, d)), const((1, d)), const((d, ne)), const((ts, ts)),
        ],
        out_specs=[
            pl.BlockSpec((None, ts, d), lambda b, i: (b, i, 0)),
            tok(ROUTE_COLS), tok(ROUTE_COLS), const((1, ne)), tok(d // 2),
        ],
        out_shape=[
            jax.ShapeDtypeStruct((batch, seq, d), F32),
            jax.ShapeDtypeStruct((n, ROUTE_COLS), jnp.int32),
            jax.ShapeDtypeStruct((n, ROUTE_COLS), F32),
            jax.ShapeDtypeStruct((1, ne), F32),
            jax.ShapeDtypeStruct((n, d // 2), jnp.int32),
        ],
        scratch_shapes=[pltpu.VMEM((1, ne), F32)],
        compiler_params=_params("arbitrary", "arbitrary"),
        name="pool_router",
    )(x3, x3, gain, pool_w, scale, ffn_gain, w_router, tril)
    return out.reshape(n, d), route_i, route_g, counts, h_packed


ROUTE_COLS = 8


def _pack_bf16_pairs(hr):
    c = hr.shape[1] // 2
    bits = pltpu.bitcast(hr, jnp.int32)
    return (bits[:, c:] & jnp.int32(-65536)) | lax.shift_right_logical(bits[:, :c], 16)


def _unpack_bf16_pairs(packed):
    lo = pltpu.bitcast(lax.shift_left(packed, 16), F32)
    hi = pltpu.bitcast(packed & jnp.int32(-65536), F32)
    return jnp.concatenate([lo.astype(BF16), hi.astype(BF16)], axis=1)


def _route_tile(first, x, g_ref, wr_ref, tril_ref, ri_ref, rg_ref, cnt_ref, hp_ref, run_ref):
    @pl.when(first)
    def _():
        run_ref[...] = jnp.zeros_like(run_ref)

    h = _rms(x, g_ref[...])
    wr = wr_ref[...]
    ne = wr.shape[1]
    h_hi = h.astype(BF16)
    hp_ref[...] = _pack_bf16_pairs(h_hi.astype(F32))
    h_lo = (h - h_hi.astype(F32)).astype(BF16)
    w_hi = wr.astype(BF16)
    w_lo = (wr - w_hi.astype(F32)).astype(BF16)
    hi_part = jnp.dot(h_hi, jnp.concatenate([w_hi, w_lo], axis=1), preferred_element_type=F32)
    logits = hi_part[:, :ne] + hi_part[:, ne:] + jnp.dot(h_lo, w_hi, preferred_element_type=F32)
    tm = logits.shape[0]
    lane = lax.broadcasted_iota(jnp.int32, (tm, ne), 1)
    v1 = jnp.max(logits, axis=1, keepdims=True)
    i1 = jnp.min(jnp.where(logits == v1, lane, ne), axis=1, keepdims=True)
    rest = jnp.where(lane == i1, -jnp.inf, logits)
    v2 = jnp.max(rest, axis=1, keepdims=True)
    i2 = jnp.min(jnp.where(rest == v2, lane, ne), axis=1, keepdims=True)
    e = jnp.exp(v2 - v1)
    g1 = 1.0 / (1.0 + e)
    g2 = e / (1.0 + e)
    oh1 = (lane == i1).astype(F32)
    oh2 = (lane == i2).astype(F32)
    both = oh1 + oh2
    before = jnp.dot(tril_ref[...], both.astype(BF16), preferred_element_type=F32) + run_ref[...]
    r1 = jnp.sum(before * oh1, axis=1, keepdims=True).astype(jnp.int32)
    r2 = jnp.sum(before * oh2, axis=1, keepdims=True).astype(jnp.int32)
    run_ref[...] += jnp.sum(both, axis=0, keepdims=True)
    ri_ref[...] = jnp.where(lane == 0, i1, jnp.where(lane == 1, i2, jnp.where(lane == 2, r1, r2)))
    rg_ref[...] = jnp.where(lane == 0, g1, g2)
    cnt_ref[...] = run_ref[...]


SC_WINDOW = 64


def _sc_workers():
    info = plsc.get_sparse_core_info()
    mesh = plsc.VectorSubcoreMesh(core_axis_name="core", subcore_axis_name="subcore")
    return mesh, info.num_cores, info.num_cores * info.num_subcores


def _dispatch(x, dest, cap):
    n, d = x.shape
    mesh, n_cores, n_workers = _sc_workers()
    per_worker = n // n_workers

    @functools.partial(
        pl.kernel, mesh=mesh, out_type=jax.ShapeDtypeStruct((cap, d), x.dtype),
        scratch_types=[pltpu.VMEM((SC_WINDOW,), jnp.int32)] * TOP_K
        + [pltpu.VMEM((SC_WINDOW, d), x.dtype), pltpu.SemaphoreType.DMA],
        name="moe_dispatch")
    def scatter(x_hbm, *rest):
        idx_hbm, (xs_hbm, *idx_vmem, rows, sem) = rest[:TOP_K], rest[TOP_K:]
        wid = lax.axis_index("subcore") * n_cores + lax.axis_index("core")

        @pl.loop(0, per_worker // SC_WINDOW)
        def _(c):
            base = wid * per_worker + c * SC_WINDOW
            for k in range(TOP_K):
                pltpu.sync_copy(idx_hbm[k].at[pl.ds(base, SC_WINDOW)], idx_vmem[k])
            pltpu.sync_copy(x_hbm.at[pl.ds(base, SC_WINDOW)], rows)
            copies = [pltpu.async_copy(rows, xs_hbm.at[idx_vmem[k]], sem) for k in range(TOP_K)]
            for cp in copies:
                cp.wait()

    return scatter(x, *[dest[k] for k in range(TOP_K)])


def _gather_rows(table, idx):
    m = idx.shape[0]
    d = table.shape[1]
    mesh, n_cores, n_workers = _sc_workers()
    per_worker = m // n_workers

    n_windows = per_worker // SC_WINDOW
    n_bufs = 2
    assert n_windows % n_bufs == 0

    @functools.partial(
        pl.kernel, mesh=mesh, out_type=jax.ShapeDtypeStruct((m, d), table.dtype),
        scratch_types=[pltpu.VMEM((SC_WINDOW,), jnp.int32)] * n_bufs
        + [pltpu.VMEM((SC_WINDOW, d), table.dtype)] * n_bufs + [pltpu.SemaphoreType.DMA] * n_bufs,
        name="moe_gather")
    def gather(t_hbm, i_hbm, o_hbm, *scratch):
        idx_vmem, rows, sems = scratch[:n_bufs], scratch[n_bufs:2 * n_bufs], scratch[2 * n_bufs:]
        wid = lax.axis_index("subcore") * n_cores + lax.axis_index("core")

        def window(c):
            return pl.ds(wid * per_worker + c * SC_WINDOW, SC_WINDOW)

        def fetch(c, buf):
            pltpu.sync_copy(i_hbm.at[window(c)], idx_vmem[buf])
            return pltpu.async_copy(t_hbm.at[idx_vmem[buf]], rows[buf], sems[buf])

        fetch(0, 0)

        @pl.loop(0, n_windows, step=n_bufs)
        def _(c):
            for buf in range(n_bufs):
                nxt = c + buf + 1

                @pl.when(nxt < n_windows)
                def _(nxt=nxt, buf=buf):
                    fetch(nxt, (buf + 1) % n_bufs)

                pltpu.make_async_copy(t_hbm.at[idx_vmem[buf]], rows[buf], sems[buf]).wait()
                pltpu.sync_copy(rows[buf], o_hbm.at[window(c + buf)])

    return gather(table, idx)


def _expert_kernel(be_ref, nv_ref, xs_ref, wg_ref, wu_ref, wd_ref, ys_ref, xb_ref, acc_ref):
    del be_ref
    b = pl.program_id(0)
    j = pl.program_id(1)

    @pl.when(j == 0)
    def _():
        acc_ref[...] = jnp.zeros_like(acc_ref)

    @pl.when(b < nv_ref[0])
    def _():
        @pl.when(j == 0)
        def _():
            xb_ref[...] = _unpack_bf16_pairs(xs_ref[...])

        x = xb_ref[...]
        acts = []
        for c in range(wg_ref.shape[1] // MXU_DIM):
            cols = slice(c * MXU_DIM, (c + 1) * MXU_DIM)
            gate = jnp.dot(x, wg_ref[:, cols].astype(BF16), preferred_element_type=F32)
            up = jnp.dot(x, wu_ref[:, cols].astype(BF16), preferred_element_type=F32)
            acts.append((_silu(gate) * up).astype(BF16))
        a = jnp.concatenate(acts, axis=1)
        acc_ref[...] += jnp.dot(a, wd_ref[...].astype(BF16), preferred_element_type=F32)

    @pl.when(j == pl.num_programs(1) - 1)
    def _():
        ys_ref[...] = _pack_bf16_pairs(acc_ref[...].astype(BF16).astype(F32))


def _experts(xs, block_e, n_valid, w_gate, w_up, w_down, layer, tm, tf):
    cap = xs.shape[0]
    d = 2 * xs.shape[1]
    ff = w_gate.shape[3]
    nj = ff // tf

    def blk(b, nv):
        return jnp.minimum(b, nv[0] - 1)

    def ffi(b, j, nv):
        return jnp.where(b < nv[0], j, nj - 1)

    return pl.pallas_call(
        _expert_kernel,
        grid_spec=pltpu.PrefetchScalarGridSpec(
            num_scalar_prefetch=2,
            grid=(cap // tm, nj),
            in_specs=[
                pl.BlockSpec((tm, d // 2), lambda b, j, be, nv: (blk(b, nv), 0)),
                pl.BlockSpec((None, None, d, tf), lambda b, j, be, nv: (layer, be[blk(b, nv)], 0, ffi(b, j, nv))),
                pl.BlockSpec((None, None, d, tf), lambda b, j, be, nv: (layer, be[blk(b, nv)], 0, ffi(b, j, nv))),
                pl.BlockSpec((None, None, tf, d), lambda b, j, be, nv: (layer, be[blk(b, nv)], ffi(b, j, nv), 0)),
            ],
            out_specs=pl.BlockSpec((tm, d // 2), lambda b, j, be, nv: (b, 0)),
            scratch_shapes=[pltpu.VMEM((tm, d), BF16), pltpu.VMEM((tm, d), F32)],
        ),
        out_shape=jax.ShapeDtypeStruct((cap, d // 2), jnp.int32),
        compiler_params=_params("arbitrary", "arbitrary"),
        name="moe_experts",
    )(block_e, n_valid, xs, w_gate, w_up, w_down)


def _combine_kernel(x_ref, rg_ref, gf_ref, y0_ref, y1_ref, o_ref, *, final_norm):
    rg = rg_ref[...]
    y0 = _unpack_bf16_pairs(y0_ref[...]).astype(F32)
    y1 = _unpack_bf16_pairs(y1_ref[...]).astype(F32)
    out = x_ref[...] + (rg[:, 0:1] * y0 + rg[:, 1:2] * y1)
    if final_norm:
        out = _rms(out, gf_ref[...])
    o_ref[...] = out


def _combine(x, route_g, yg, final_gain, final_norm):
    n, d = x.shape
    tm = 512
    return pl.pallas_call(
        functools.partial(_combine_kernel, final_norm=final_norm),
        grid=(n // tm,),
        in_specs=[
            pl.BlockSpec((tm, d), lambda i: (i, 0)),
            pl.BlockSpec((tm, ROUTE_COLS), lambda i: (i, 0)),
            pl.BlockSpec((1, d), lambda i: (0, 0)),
            pl.BlockSpec((None, tm, d // 2), lambda i: (0, i, 0)),
            pl.BlockSpec((None, tm, d // 2), lambda i: (1, i, 0)),
        ],
        out_specs=pl.BlockSpec((tm, d), lambda i: (i, 0)),
        out_shape=jax.ShapeDtypeStruct((n, d), F32),
        compiler_params=_params("parallel"),
        name="moe_combine",
    )(x, route_g, final_gain, yg, yg)


def _moe_layer(x, routing, w_gate, w_up, w_down, layer, final_gain, final_norm):
    n, d = x.shape
    tm = 1024 if n * TOP_K >= 8 * 1024 else 256
    tf = 2 * MXU_DIM
    route_i, route_g, counts, h_packed = routing

    counts = counts[0].astype(jnp.int32)
    padded = ((counts + tm - 1) // tm) * tm
    ends = jnp.cumsum(padded)
    starts = ends - padded
    dest = (starts[route_i[:, :TOP_K]] + route_i[:, TOP_K:2 * TOP_K]).T
    n_blocks = (n * TOP_K) // tm + N_EXPERTS
    block_start = jnp.arange(n_blocks, dtype=jnp.int32) * tm
    block_e = jnp.minimum(jnp.sum(ends[None, :] <= block_start[:, None], axis=1), N_EXPERTS - 1).astype(jnp.int32)
    n_valid = (ends[-1:] // tm).astype(jnp.int32)

    xs = _dispatch(h_packed, dest, n_blocks * tm)
    ys = _experts(xs, block_e, n_valid, w_gate, w_up, w_down, layer, tm, tf)
    yg = _gather_rows(ys, dest.reshape(TOP_K * n)).reshape(TOP_K, n, d // 2)
    return _combine(x, route_g, yg, final_gain, final_norm)


def _rope_tables(seq):
    pos = jnp.arange(seq, dtype=F32)
    inv_freq = ROPE_THETA ** (-jnp.arange(0, HEAD_DIM, 2, dtype=F32) / HEAD_DIM)
    ang = pos[:, None] * inv_freq[None, :]
    reps = LANES // (HEAD_DIM // 2)
    cos = jnp.tile(jnp.cos(ang), (1, reps))
    sin = jnp.tile(jnp.sin(ang), (1, reps))
    first_half = (jnp.arange(LANES) % HEAD_DIM) < HEAD_DIM // 2
    tabs = (cos, jnp.where(first_half, -sin, sin))
    q_scale = HEAD_DIM ** -0.5 * 1.4426950408889634
    return tuple(jnp.stack([t * q_scale, t]) for t in tabs)


def kernel(x, mix_norm, ffn_norm, attn_w_qkv, attn_w_o, pool_w, pool_scale, dense_w_gate, dense_w_up,
           dense_w_down, moe_router, moe_w_gate, moe_w_up, moe_w_down, final_norm):
    batch, seq, d = x.shape
    depth = mix_norm.shape[0]
    assert depth % 2 == 0 and d == N_HEADS * HEAD_DIM
    rope_tables = _rope_tables(seq)
    xf = x.reshape(batch * seq, d)
    final_gain = final_norm.reshape(1, d)
    for i in range(depth):
        j = i // 2
        mix_gain = mix_norm[i].reshape(1, d)
        ffn_gain = ffn_norm[i].reshape(1, d)
        if i % 2 == 0:
            w_qkv = attn_w_qkv[j].astype(BF16)
            qkv = _qkv_project(xf, mix_gain, w_qkv, rope_tables, batch, seq)
            xf = _attention_and_dense_ffn(qkv, xf, attn_w_o[j].astype(BF16), ffn_gain, dense_w_gate[j].astype(BF16),
                                          dense_w_up[j].astype(BF16), dense_w_down[j].astype(BF16), batch, seq)
        else:
            xf, *routing = _pool_mixer_and_router(xf, mix_gain, pool_w[j].astype(BF16), pool_scale[j].reshape(1, d),
                                                  ffn_gain, moe_router[j], batch, seq)
            xf = _moe_layer(xf, routing, moe_w_gate, moe_w_up, moe_w_down, j, final_gain, i == depth - 1)
    return xf.reshape(batch, seq, d)
```

```python
import functools

import jax
import jax.numpy as jnp
from jax import lax
from jax.experimental import pallas as pl
from jax.experimental.pallas import tpu as pltpu
from jax.experimental.pallas import tpu_sc as plsc

WINDOWS = (128, 512, 2048)
DILATIONS = (1, 4, 16)
N_GROUPS = len(WINDOWS)
N_HEADS = 16
HEAD_DIM = 64
ROPE_THETA = 10000.0
POOL_SIZES = (2, 4, 8, 16)
N_EXPERTS = 8
TOP_K = 2
RMS_EPS = 1e-6
MASK_VALUE = -1e30
QKV_ROWS = 512

LANES = 128
MXU_DIM = 256
VMEM_LIMIT_BYTES = 56 * 1024 * 1024

F32 = jnp.float32
BF16 = jnp.bfloat16


def _params(*sem):
    return pltpu.CompilerParams(dimension_semantics=sem, vmem_limit_bytes=VMEM_LIMIT_BYTES)


def _rms(x, g):
    ms = jnp.mean(x * x, axis=-1, keepdims=True)
    return x * lax.rsqrt(ms + RMS_EPS) * g


def _silu(g):
    return g / (1.0 + jnp.exp(-g))


def _qkv_kernel(x_ref, g_ref, w_ref, cos_ref, sin_ref, o0_ref, o1_ref, o2_ref, acc_ref):
    tm, d = x_ref.shape
    n_chunks = acc_ref.shape[1]
    lane = lax.broadcasted_iota(jnp.int32, (1, LANES), 1)
    first_half = (lane % HEAD_DIM) < HEAD_DIM // 2
    h = _rms(x_ref[...], g_ref[...]).astype(BF16)
    for g, (o_ref, dil) in enumerate(zip((o0_ref, o1_ref, o2_ref), DILATIONS)):
        rows = tm // dil

        def strided(ref, r, dil=dil, rows=rows):
            if dil == 1:
                return ref[...]
            return ref[pl.ds(r, rows, stride=dil), :]

        for kind in range(3):
            col0 = (g * 3 + kind) * d
            acc = jnp.dot(h, w_ref[:, col0:col0 + d], preferred_element_type=F32)
            for c in range(n_chunks):
                acc_ref[kind, c] = acc[:, c * LANES:(c + 1) * LANES]
            for r in range(dil):
                if kind < 2:
                    cos = strided(cos_ref.at[kind], r)
                    sin = strided(sin_ref.at[kind], r)
                for c in range(n_chunks):
                    t = strided(acc_ref.at[kind, c], r)
                    if kind < 2:
                        partner = jnp.where(first_half, pltpu.roll(t, LANES - HEAD_DIM // 2, 1),
                                            pltpu.roll(t, HEAD_DIM // 2, 1))
                        t = t * cos + partner * sin
                    o_ref[kind, r, :, c * LANES:(c + 1) * LANES] = t.astype(BF16)


def _qkv_project(x, gain, w_qkv, tables, batch, seq):
    n, d = x.shape
    tm = QKV_ROWS
    tiles_per_seq = seq // tm
    tab = pl.BlockSpec((2, tm, LANES), lambda i: (0, i % tiles_per_seq, 0))

    def o_spec(dil):
        return pl.BlockSpec((3, None, dil, tm // dil, d),
                            lambda i: (0, i // tiles_per_seq, 0, i % tiles_per_seq, 0))

    outs = pl.pallas_call(
        _qkv_kernel,
        grid=(n // tm,),
        in_specs=[
            pl.BlockSpec((tm, d), lambda i: (i, 0)),
            pl.BlockSpec((1, d), lambda i: (0, 0)),
            pl.BlockSpec(w_qkv.shape, lambda i: (0, 0), pipeline_mode=pl.Buffered(1)),
            tab, tab,
        ],
        out_specs=[o_spec(dil) for dil in DILATIONS],
        out_shape=[jax.ShapeDtypeStruct((3, batch, dil, seq // dil, d), BF16) for dil in DILATIONS],
        scratch_shapes=[pltpu.VMEM((3, d // LANES, tm, LANES), F32)],
        compiler_params=_params("parallel"),
        name="qkv_rope",
    )(x, gain, w_qkv, *tables)
    return [o.reshape(3, batch, seq, d) for o in outs]


ATTN_TILES_PER_FFN_PIECE = 3


def _attn_dense_kernel(q0, k0, v0, q1, k1, v1, q2, k2, v2, band_ref, causal_ref,
                       x_ref, wo_ref, g_ref, wg_ref, wu_ref, wd_ref, y_ref,
                       o_scr, num_ref, max_ref, den_ref, kt_ref):
    b = pl.program_id(0)
    hp = pl.program_id(1)
    n_pairs, seq = o_scr.shape[1], o_scr.shape[2]
    w = WINDOWS[0] // DILATIONS[0]
    lane = lax.broadcasted_iota(jnp.int32, (1, LANES), 1)
    first_head = lane < HEAD_DIM
    o_out = o_scr.at[b % 2, hp]
    o_in = o_scr.at[(b + 1) % 2]

    @pl.when((b == 0) & (hp == 0))
    def _():
        o_scr[1] = jnp.zeros(o_scr.shape[1:], o_scr.dtype)

    def ffn_pieces():
        rows = pl.ds(pl.multiple_of(hp * x_ref.shape[0], x_ref.shape[0]), x_ref.shape[0])
        o = jnp.concatenate([o_in[p, rows, :] for p in range(n_pairs)], axis=1)
        x = x_ref[...] + jnp.dot(o, wo_ref[...], preferred_element_type=F32)
        h = _rms(x, g_ref[...]).astype(BF16)
        yield
        acts = []
        n_chunks = wg_ref.shape[1] // MXU_DIM
        for c in range(n_chunks):
            cols = slice(c * MXU_DIM, (c + 1) * MXU_DIM)
            gate = jnp.dot(h, wg_ref[:, cols], preferred_element_type=F32)
            up = jnp.dot(h, wu_ref[:, cols], preferred_element_type=F32)
            acts.append((_silu(gate) * up).astype(BF16))
            yield
        half = n_chunks // 2
        acc = jnp.dot(jnp.concatenate(acts[:half], axis=1), wd_ref[:half * MXU_DIM, :], preferred_element_type=F32)
        yield
        acc = acc + jnp.dot(jnp.concatenate(acts[half:], axis=1), wd_ref[half * MXU_DIM:, :],
                            preferred_element_type=F32)
        y_ref[...] = x + acc
        yield

    def tile(q_ref, v_ref, qoff, has_prev):
        q = q_ref[qoff:qoff + w, :]
        zero = jnp.zeros_like(q)
        qq = jnp.concatenate([jnp.where(first_head, q, zero), jnp.where(first_head, zero, q)], axis=0)
        k_lo = qoff - w if has_prev else qoff
        s = jnp.dot(qq, kt_ref[:, k_lo:qoff + w], preferred_element_type=F32)
        s = s + (band_ref[...] if has_prev else causal_ref[...])
        mx = jnp.max(s, axis=1, keepdims=True)
        p = jnp.exp2(s - mx)
        den = jnp.sum(p, axis=1, keepdims=True)
        pv = jnp.dot(p.astype(BF16), v_ref[k_lo:qoff + w, :], preferred_element_type=F32)
        return tuple(jnp.where(first_head, a[:w], a[w:]) for a in (pv, mx, den))

    pieces = ffn_pieces()
    tiles_done = 0
    order = sorted(range(N_GROUPS), key=lambda g: -DILATIONS[g])
    for g in order:
        q_ref, k_ref, v_ref = ((q0, k0, v0), (q1, k1, v1), (q2, k2, v2))[g]
        dil = DILATIONS[g]
        sub_len = seq // dil
        nb = sub_len // w
        kt_ref[...] = k_ref[...].T
        for t in range(dil * nb):
            r, n = divmod(t, nb)
            qoff = r * sub_len + n * w
            pv, mx, den = tile(q_ref, v_ref, qoff, n > 0)
            tiles_done += 1
            if tiles_done % ATTN_TILES_PER_FFN_PIECE == 0:
                next(pieces, None)
            rows = pl.ds(qoff, w) if dil == 1 else pl.ds(r + dil * n * w, w, stride=dil)
            if g == order[0]:
                num_ref[rows, :] = pv
                max_ref[rows, :] = mx
                den_ref[rows, :] = den
                continue
            old_max = max_ref[rows, :]
            new_max = jnp.maximum(old_max, mx)
            a = jnp.exp2(old_max - new_max)
            c = jnp.exp2(mx - new_max)
            num = a * num_ref[rows, :] + c * pv
            den = a * den_ref[rows, :] + c * den
            if g == order[-1]:
                o_out[rows, :] = (num / den).astype(BF16)
            else:
                num_ref[rows, :] = num
                den_ref[rows, :] = den
                max_ref[rows, :] = new_max
    for _ in pieces:
        pass


def _attention_biases(w):
    row = jnp.arange(2 * w)[:, None] % w
    col = jnp.arange(2 * w)[None, :]
    dist = row + w - col
    band = jnp.where((dist >= 0) & (dist <= w), 0.0, MASK_VALUE).astype(F32)
    causal = jnp.where(row >= col[:, :w], 0.0, MASK_VALUE).astype(F32)
    return band, causal


def _attention_and_dense_ffn(qkv_groups, x, w_o, gain, w_gate, w_up, w_down, batch, seq):
    n, d = x.shape
    ff = w_gate.shape[1]
    n_pairs = d // LANES
    rows = seq // n_pairs
    last = batch - 1
    in_specs, args = [], []
    for arr in qkv_groups:
        for kind in range(3):
            in_specs.append(pl.BlockSpec((None, None, seq, LANES),
                                         lambda b, hp, kind=kind: (kind, jnp.minimum(b, last), 0, hp)))
            args.append(arr)
    const = lambda shape: pl.BlockSpec(shape, lambda b, hp: (0, 0), pipeline_mode=pl.Buffered(1))
    ffn_in = pl.BlockSpec((rows, d), lambda b, hp: (jnp.maximum(b - 1, 0) * n_pairs + hp, 0))
    ffn_out = pl.BlockSpec((rows, d), lambda b, hp: (jnp.where(b == 0, batch, b - 1) * n_pairs + hp, 0))
    biases = _attention_biases(WINDOWS[0] // DILATIONS[0])
    in_specs += [const(t.shape) for t in biases]
    in_specs += [ffn_in, const((d, d)), const((1, d)), const((d, ff)), const((d, ff)), const((ff, d))]
    args += [*biases, x, w_o, gain, w_gate, w_up, w_down]
    return pl.pallas_call(
        _attn_dense_kernel,
        grid=(batch + 1, n_pairs),
        in_specs=in_specs,
        out_specs=ffn_out,
        out_shape=jax.ShapeDtypeStruct((n + seq, d), F32),
        scratch_shapes=[pltpu.VMEM((2, n_pairs, seq, LANES), BF16)]
        + [pltpu.VMEM((seq, LANES), F32)] * 3 + [pltpu.VMEM((LANES, seq), BF16)],
        compiler_params=_params("arbitrary", "arbitrary"),
        name="attention_ffn",
    )(*args)


POOL_HALO = max(POOL_SIZES)


def _pool_tile(i, x_ref, halo_ref, g_ref, w_ref, sc_ref, y_ref):
    ts = x_ref.shape[0]
    g = g_ref[...]
    x = x_ref[...]
    h = _rms(x, g)
    hh = jnp.where(i > 0, _rms(halo_ref[...], g), 0.0)
    ext = jnp.concatenate([hh, h], axis=0)
    pos = i * ts + lax.broadcasted_iota(jnp.int32, (ts, 1), 0)
    pc = w_ref.shape[1]
    for grp, size in enumerate(POOL_SIZES):
        sl = slice(grp * pc, (grp + 1) * pc)
        s = ext[:, sl]
        step = 1
        while step < size:
            s = s + pltpu.roll(s, step, 0)
            step *= 2
        cnt = jnp.minimum(pos + 1, size).astype(F32)
        y = s[POOL_HALO:, :] / cnt - h[:, sl]
        z = jnp.dot(y.astype(BF16), w_ref[grp], preferred_element_type=F32)
        y_ref[:, sl] = x[:, sl] + z * sc_ref[:, sl]


def _pool_router_kernel(x_ref, halo_ref, g_ref, w_ref, sc_ref, g2_ref, wr_ref, tril_ref,
                        y_ref, ri_ref, rg_ref, cnt_ref, hp_ref, run_ref):
    b = pl.program_id(0)
    i = pl.program_id(1)
    _pool_tile(i, x_ref, halo_ref, g_ref, w_ref, sc_ref, y_ref)
    _route_tile((b == 0) & (i == 0), y_ref[...], g2_ref, wr_ref, tril_ref, ri_ref, rg_ref, cnt_ref, hp_ref, run_ref)


def _pool_mixer_and_router(x, gain, pool_w, scale, ffn_gain, w_router, batch, seq):
    d = x.shape[1]
    n = batch * seq
    ne = w_router.shape[1]
    ts = 512
    tiles = seq // ts
    x3 = x.reshape(-1, seq, d)
    hb = ts // POOL_HALO
    tril = (jnp.arange(ts)[:, None] > jnp.arange(ts)[None, :]).astype(BF16)
    const = lambda shape: pl.BlockSpec(shape, lambda b, i: (0,) * len(shape))
    tok = lambda cols: pl.BlockSpec((ts, cols), lambda b, i: (b * tiles + i, 0))
    out, route_i, route_g, counts, h_packed = pl.pallas_call(
        _pool_router_kernel,
        grid=(batch, tiles),
        in_specs=[
            pl.BlockSpec((None, ts, d), lambda b, i: (b, i, 0)),
            pl.BlockSpec((None, POOL_HALO, d), lambda b, i: (b, jnp.maximum(i * hb - 1, 0), 0)),
            const((1, d)), const(pool_w.shape), const((1, d)), const((1, d)), const((d, ne)), const((ts, ts)),
        ],
        out_specs=[
            pl.BlockSpec((None, ts, d), lambda b, i: (b, i, 0)),
            tok(ROUTE_COLS), tok(ROUTE_COLS), const((1, ne)), tok(d // 2),
        ],
        out_shape=[
            jax.ShapeDtypeStruct((batch, seq, d), F32),
            jax.ShapeDtypeStruct((n, ROUTE_COLS), jnp.int32),
            jax.ShapeDtypeStruct((n, ROUTE_COLS), F32),
            jax.ShapeDtypeStruct((1, ne), F32),
            jax.ShapeDtypeStruct((n, d // 2), jnp.int32),
        ],
        scratch_shapes=[pltpu.VMEM((1, ne), F32)],
        compiler_params=_params("arbitrary", "arbitrary"),
        name="pool_router",
    )(x3, x3, gain, pool_w, scale, ffn_gain, w_router, tril)
    return out.reshape(n, d), route_i, route_g, counts, h_packed


ROUTE_COLS = 8


def _pack_bf16_pairs(hr):
    c = hr.shape[1] // 2
    bits = pltpu.bitcast(hr, jnp.int32)
    return (bits[:, c:] & jnp.int32(-65536)) | lax.shift_right_logical(bits[:, :c], 16)


def _unpack_bf16_pairs(packed):
    lo = pltpu.bitcast(lax.shift_left(packed, 16), F32)
    hi = pltpu.bitcast(packed & jnp.int32(-65536), F32)
    return jnp.concatenate([lo.astype(BF16), hi.astype(BF16)], axis=1)


def _route_tile(first, x, g_ref, wr_ref, tril_ref, ri_ref, rg_ref, cnt_ref, hp_ref, run_ref):
    @pl.when(first)
    def _():
        run_ref[...] = jnp.zeros_like(run_ref)

    h = _rms(x, g_ref[...])
    wr = wr_ref[...]
    ne = wr.shape[1]
    h_hi = h.astype(BF16)
    hp_ref[...] = _pack_bf16_pairs(h_hi.astype(F32))
    h_lo = (h - h_hi.astype(F32)).astype(BF16)
    w_hi = wr.astype(BF16)
    w_lo = (wr - w_hi.astype(F32)).astype(BF16)
    hi_part = jnp.dot(h_hi, jnp.concatenate([w_hi, w_lo], axis=1), preferred_element_type=F32)
    logits = hi_part[:, :ne] + hi_part[:, ne:] + jnp.dot(h_lo, w_hi, preferred_element_type=F32)
    tm = logits.shape[0]
    lane = lax.broadcasted_iota(jnp.int32, (tm, ne), 1)
    v1 = jnp.max(logits, axis=1, keepdims=True)
    i1 = jnp.min(jnp.where(logits == v1, lane, ne), axis=1, keepdims=True)
    rest = jnp.where(lane == i1, -jnp.inf, logits)
    v2 = jnp.max(rest, axis=1, keepdims=True)
    i2 = jnp.min(jnp.where(rest == v2, lane, ne), axis=1, keepdims=True)
    e = jnp.exp(v2 - v1)
    g1 = 1.0 / (1.0 + e)
    g2 = e / (1.0 + e)
    oh1 = (lane == i1).astype(F32)
    oh2 = (lane == i2).astype(F32)
    both = oh1 + oh2
    before = jnp.dot(tril_ref[...], both.astype(BF16), preferred_element_type=F32) + run_ref[...]
    r1 = jnp.sum(before * oh1, axis=1, keepdims=True).astype(jnp.int32)
    r2 = jnp.sum(before * oh2, axis=1, keepdims=True).astype(jnp.int32)
    run_ref[...] += jnp.sum(both, axis=0, keepdims=True)
    ri_ref[...] = jnp.where(lane == 0, i1, jnp.where(lane == 1, i2, jnp.where(lane == 2, r1, r2)))
    rg_ref[...] = jnp.where(lane == 0, g1, g2)
    cnt_ref[...] = run_ref[...]


SC_WINDOW = 64


def _sc_workers():
    info = plsc.get_sparse_core_info()
    mesh = plsc.VectorSubcoreMesh(core_axis_name="core", subcore_axis_name="subcore")
    return mesh, info.num_cores, info.num_cores * info.num_subcores


def _dispatch(x, dest, cap):
    n, d = x.shape
    mesh, n_cores, n_workers = _sc_workers()
    per_worker = n // n_workers

    @functools.partial(
        pl.kernel, mesh=mesh, out_type=jax.ShapeDtypeStruct((cap, d), x.dtype),
        scratch_types=[pltpu.VMEM((SC_WINDOW,), jnp.int32)] * TOP_K
        + [pltpu.VMEM((SC_WINDOW, d), x.dtype), pltpu.SemaphoreType.DMA],
        name="moe_dispatch")
    def scatter(x_hbm, *rest):
        idx_hbm, (xs_hbm, *idx_vmem, rows, sem) = rest[:TOP_K], rest[TOP_K:]
        wid = lax.axis_index("subcore") * n_cores + lax.axis_index("core")

        @pl.loop(0, per_worker // SC_WINDOW)
        def _(c):
            base = wid * per_worker + c * SC_WINDOW
            for k in range(TOP_K):
                pltpu.sync_copy(idx_hbm[k].at[pl.ds(base, SC_WINDOW)], idx_vmem[k])
            pltpu.sync_copy(x_hbm.at[pl.ds(base, SC_WINDOW)], rows)
            copies = [pltpu.async_copy(rows, xs_hbm.at[idx_vmem[k]], sem) for k in range(TOP_K)]
            for cp in copies:
                cp.wait()

    return scatter(x, *[dest[k] for k in range(TOP_K)])


def _gather_rows(table, idx):
    m = idx.shape[0]
    d = table.shape[1]
    mesh, n_cores, n_workers = _sc_workers()
    per_worker = m // n_workers

    n_windows = per_worker // SC_WINDOW
    n_bufs = 2
    assert n_windows % n_bufs == 0

    @functools.partial(
        pl.kernel, mesh=mesh, out_type=jax.ShapeDtypeStruct((m, d), table.dtype),
        scratch_types=[pltpu.VMEM((SC_WINDOW,), jnp.int32)] * n_bufs
        + [pltpu.VMEM((SC_WINDOW, d), table.dtype)] * n_bufs + [pltpu.SemaphoreType.DMA] * n_bufs,
        name="moe_gather")
    def gather(t_hbm, i_hbm, o_hbm, *scratch):
        idx_vmem, rows, sems = scratch[:n_bufs], scratch[n_bufs:2 * n_bufs], scratch[2 * n_bufs:]
        wid = lax.axis_index("subcore") * n_cores + lax.axis_index("core")

        def window(c):
            return pl.ds(wid * per_worker + c * SC_WINDOW, SC_WINDOW)

        def fetch(c, buf):
            pltpu.sync_copy(i_hbm.at[window(c)], idx_vmem[buf])
            return pltpu.async_copy(t_hbm.at[idx_vmem[buf]], rows[buf], sems[buf])

        fetch(0, 0)

        @pl.loop(0, n_windows, step=n_bufs)
        def _(c):
            for buf in range(n_bufs):
                nxt = c + buf + 1

                @pl.when(nxt < n_windows)
                def _(nxt=nxt, buf=buf):
                    fetch(nxt, (buf + 1) % n_bufs)

                pltpu.make_async_copy(t_hbm.at[idx_vmem[buf]], rows[buf], sems[buf]).wait()
                pltpu.sync_copy(rows[buf], o_hbm.at[window(c + buf)])

    return gather(table, idx)


def _expert_kernel(be_ref, nv_ref, xs_ref, wg_ref, wu_ref, wd_ref, ys_ref, xb_ref, acc_ref):
    del be_ref
    b = pl.program_id(0)
    j = pl.program_id(1)

    def ff_tile(first):
        x = xb_ref[...]
        acts = []
        for c in range(wg_ref.shape[1] // MXU_DIM):
            cols = slice(c * MXU_DIM, (c + 1) * MXU_DIM)
            gate = jnp.dot(x, wg_ref[:, cols].astype(BF16), preferred_element_type=F32)
            up = jnp.dot(x, wu_ref[:, cols].astype(BF16), preferred_element_type=F32)
            acts.append((_silu(gate) * up).astype(BF16))
        part = jnp.dot(jnp.concatenate(acts, axis=1), wd_ref[...].astype(BF16), preferred_element_type=F32)
        if first:
            acc_ref[...] = part
        else:
            acc_ref[...] += part

    valid = b < nv_ref[0]

    @pl.when(valid & (j == 0))
    def _():
        xb_ref[...] = _unpack_bf16_pairs(xs_ref[...])
        ff_tile(True)

    @pl.when(valid & (j > 0))
    def _():
        ff_tile(False)

    @pl.when(jnp.logical_not(valid) & (j == 0))
    def _():
        acc_ref[...] = jnp.zeros_like(acc_ref)

    @pl.when(j == pl.num_programs(1) - 1)
    def _():
        ys_ref[...] = _pack_bf16_pairs(acc_ref[...].astype(BF16).astype(F32))


def _experts(xs, block_e, n_valid, w_gate, w_up, w_down, layer, tm, tf):
    cap = xs.shape[0]
    d = 2 * xs.shape[1]
    ff = w_gate.shape[3]
    nj = ff // tf

    def blk(b, nv):
        return jnp.minimum(b, nv[0] - 1)

    def ffi(b, j, nv):
        return jnp.where(b < nv[0], j, nj - 1)

    return pl.pallas_call(
        _expert_kernel,
        grid_spec=pltpu.PrefetchScalarGridSpec(
            num_scalar_prefetch=2,
            grid=(cap // tm, nj),
            in_specs=[
                pl.BlockSpec((tm, d // 2), lambda b, j, be, nv: (blk(b, nv), 0)),
                pl.BlockSpec((None, None, d, tf), lambda b, j, be, nv: (layer, be[blk(b, nv)], 0, ffi(b, j, nv))),
                pl.BlockSpec((None, None, d, tf), lambda b, j, be, nv: (layer, be[blk(b, nv)], 0, ffi(b, j, nv))),
                pl.BlockSpec((None, None, tf, d), lambda b, j, be, nv: (layer, be[blk(b, nv)], ffi(b, j, nv), 0)),
            ],
            out_specs=pl.BlockSpec((tm, d // 2), lambda b, j, be, nv: (b, 0)),
            scratch_shapes=[pltpu.VMEM((tm, d), BF16), pltpu.VMEM((tm, d), F32)],
        ),
        out_shape=jax.ShapeDtypeStruct((cap, d // 2), jnp.int32),
        compiler_params=_params("arbitrary", "arbitrary"),
        name="moe_experts",
    )(block_e, n_valid, xs, w_gate, w_up, w_down)


def _combine_kernel(x_ref, rg_ref, gf_ref, y0_ref, y1_ref, o_ref, *, final_norm):
    rg = rg_ref[...]
    y0 = _unpack_bf16_pairs(y0_ref[...]).astype(F32)
    y1 = _unpack_bf16_pairs(y1_ref[...]).astype(F32)
    out = x_ref[...] + (rg[:, 0:1] * y0 + rg[:, 1:2] * y1)
    if final_norm:
        out = _rms(out, gf_ref[...])
    o_ref[...] = out


def _combine(x, route_g, yg, final_gain, final_norm):
    n, d = x.shape
    tm = 512
    return pl.pallas_call(
        functools.partial(_combine_kernel, final_norm=final_norm),
        grid=(n // tm,),
        in_specs=[
            pl.BlockSpec((tm, d), lambda i: (i, 0)),
            pl.BlockSpec((tm, ROUTE_COLS), lambda i: (i, 0)),
            pl.BlockSpec((1, d), lambda i: (0, 0)),
            pl.BlockSpec((None, tm, d // 2), lambda i: (0, i, 0)),
            pl.BlockSpec((None, tm, d // 2), lambda i: (1, i, 0)),
        ],
        out_specs=pl.BlockSpec((tm, d), lambda i: (i, 0)),
        out_shape=jax.ShapeDtypeStruct((n, d), F32),
        compiler_params=_params("parallel"),
        name="moe_combine",
    )(x, route_g, final_gain, yg, yg)


def _moe_layer(x, routing, w_gate, w_up, w_down, layer, final_gain, final_norm):
    n, d = x.shape
    tm = 1024 if n * TOP_K >= 8 * 1024 else 256
    tf = 2 * MXU_DIM
    route_i, route_g, counts, h_packed = routing

    counts = counts[0].astype(jnp.int32)
    padded = ((counts + tm - 1) // tm) * tm
    ends = jnp.cumsum(padded)
    starts = ends - padded
    dest = (starts[route_i[:, :TOP_K]] + route_i[:, TOP_K:2 * TOP_K]).T
    n_blocks = (n * TOP_K) // tm + N_EXPERTS
    block_start = jnp.arange(n_blocks, dtype=jnp.int32) * tm
    block_e = jnp.minimum(jnp.sum(ends[None, :] <= block_start[:, None], axis=1), N_EXPERTS - 1).astype(jnp.int32)
    n_valid = (ends[-1:] // tm).astype(jnp.int32)

    xs = _dispatch(h_packed, dest, n_blocks * tm)
    ys = _experts(xs, block_e, n_valid, w_gate, w_up, w_down, layer, tm, tf)
    yg = _gather_rows(ys, dest.reshape(TOP_K * n)).reshape(TOP_K, n, d // 2)
    return _combine(x, route_g, yg, final_gain, final_norm)


def _rope_tables(seq):
    pos = jnp.arange(seq, dtype=F32)
    inv_freq = ROPE_THETA ** (-jnp.arange(0, HEAD_DIM, 2, dtype=F32) / HEAD_DIM)
    ang = pos[:, None] * inv_freq[None, :]
    reps = LANES // (HEAD_DIM // 2)
    cos = jnp.tile(jnp.cos(ang), (1, reps))
    sin = jnp.tile(jnp.sin(ang), (1, reps))
    first_half = (jnp.arange(LANES) % HEAD_DIM) < HEAD_DIM // 2
    tabs = (cos, jnp.where(first_half, -sin, sin))
    q_scale = HEAD_DIM ** -0.5 * 1.4426950408889634
    return tuple(jnp.stack([t * q_scale, t]) for t in tabs)


def kernel(x, mix_norm, ffn_norm, attn_w_qkv, attn_w_o, pool_w, pool_scale, dense_w_gate, dense_w_up,
           dense_w_down, moe_router, moe_w_gate, moe_w_up, moe_w_down, final_norm):
    batch, seq, d = x.shape
    depth = mix_norm.shape[0]
    assert depth % 2 == 0 and d == N_HEADS * HEAD_DIM
    rope_tables = _rope_tables(seq)
    xf = x.reshape(batch * seq, d)
    final_gain = final_norm.reshape(1, d)
    for i in range(depth):
        j = i // 2
        mix_gain = mix_norm[i].reshape(1, d)
        ffn_gain = ffn_norm[i].reshape(1, d)
        if i % 2 == 0:
            w_qkv = attn_w_qkv[j].astype(BF16)
            qkv = _qkv_project(xf, mix_gain, w_qkv, rope_tables, batch, seq)
            xf = _attention_and_dense_ffn(qkv, xf, attn_w_o[j].astype(BF16), ffn_gain, dense_w_gate[j].astype(BF16),
                                          dense_w_up[j].astype(BF16), dense_w_down[j].astype(BF16), batch, seq)
        else:
            xf, *routing = _pool_mixer_and_router(xf, mix_gain, pool_w[j].astype(BF16), pool_scale[j].reshape(1, d),
                                                  ffn_gain, moe_router[j], batch, seq)
            xf = _moe_layer(xf, routing, moe_w_gate, moe_w_up, moe_w_down, j, final_gain, i == depth - 1)
    return xf.reshape(batch, seq, d)
```

```python
import functools

import jax
import jax.numpy as jnp
from jax import lax
from jax.experimental import pallas as pl
from jax.experimental.pallas import tpu as pltpu
from jax.experimental.pallas import tpu_sc as plsc

WINDOWS = (128, 512, 2048)
DILATIONS = (1, 4, 16)
N_GROUPS = len(WINDOWS)
N_HEADS = 16
HEAD_DIM = 64
ROPE_THETA = 10000.0
POOL_SIZES = (2, 4, 8, 16)
N_EXPERTS = 8
TOP_K = 2
RMS_EPS = 1e-6
MASK_VALUE = -1e30
QKV_ROWS = 512

LANES = 128
MXU_DIM = 256
VMEM_LIMIT_BYTES = 56 * 1024 * 1024

F32 = jnp.float32
BF16 = jnp.bfloat16


def _params(*sem):
    return pltpu.CompilerParams(dimension_semantics=sem, vmem_limit_bytes=VMEM_LIMIT_BYTES)


def _rms(x, g):
    ms = jnp.mean(x * x, axis=-1, keepdims=True)
    return x * lax.rsqrt(ms + RMS_EPS) * g


def _silu(g):
    return g / (1.0 + jnp.exp(-g))


def _qkv_kernel(x_ref, g_ref, w_ref, cos_ref, sin_ref, o0_ref, o1_ref, o2_ref, acc_ref):
    tm, d = x_ref.shape
    n_chunks = acc_ref.shape[1]
    lane = lax.broadcasted_iota(jnp.int32, (1, LANES), 1)
    first_half = (lane % HEAD_DIM) < HEAD_DIM // 2
    h = _rms(x_ref[...], g_ref[...]).astype(BF16)
    for g, (o_ref, dil) in enumerate(zip((o0_ref, o1_ref, o2_ref), DILATIONS)):
        rows = tm // dil

        def strided(ref, r, dil=dil, rows=rows):
            if dil == 1:
                return ref[...]
            return ref[pl.ds(r, rows, stride=dil), :]

        for kind in range(3):
            col0 = (g * 3 + kind) * d
            acc = jnp.dot(h, w_ref[:, col0:col0 + d], preferred_element_type=F32)
            for c in range(n_chunks):
                acc_ref[kind, c] = acc[:, c * LANES:(c + 1) * LANES]
            for r in range(dil):
                if kind < 2:
                    cos = strided(cos_ref.at[kind], r)
                    sin = strided(sin_ref.at[kind], r)
                for c in range(n_chunks):
                    t = strided(acc_ref.at[kind, c], r)
                    if kind < 2:
                        partner = jnp.where(first_half, pltpu.roll(t, LANES - HEAD_DIM // 2, 1),
                                            pltpu.roll(t, HEAD_DIM // 2, 1))
                        t = t * cos + partner * sin
                    o_ref[kind, r, :, c * LANES:(c + 1) * LANES] = t.astype(BF16)


def _qkv_project(x, gain, w_qkv, tables, batch, seq):
    n, d = x.shape
    tm = QKV_ROWS
    tiles_per_seq = seq // tm
    tab = pl.BlockSpec((2, tm, LANES), lambda i: (0, i % tiles_per_seq, 0))

    def o_spec(dil):
        return pl.BlockSpec((3, None, dil, tm // dil, d),
                            lambda i: (0, i // tiles_per_seq, 0, i % tiles_per_seq, 0))

    outs = pl.pallas_call(
        _qkv_kernel,
        grid=(n // tm,),
        in_specs=[
            pl.BlockSpec((tm, d), lambda i: (i, 0)),
            pl.BlockSpec((1, d), lambda i: (0, 0)),
            pl.BlockSpec(w_qkv.shape, lambda i: (0, 0), pipeline_mode=pl.Buffered(1)),
            tab, tab,
        ],
        out_specs=[o_spec(dil) for dil in DILATIONS],
        out_shape=[jax.ShapeDtypeStruct((3, batch, dil, seq // dil, d), BF16) for dil in DILATIONS],
        scratch_shapes=[pltpu.VMEM((3, d // LANES, tm, LANES), F32)],
        compiler_params=_params("parallel"),
        name="qkv_rope",
    )(x, gain, w_qkv, *tables)
    return [o.reshape(3, batch, seq, d) for o in outs]


ATTN_TILES_PER_FFN_PIECE = 3


def _attn_dense_kernel(q0, k0, v0, q1, k1, v1, q2, k2, v2, band_ref, causal_ref,
                       x_ref, wo_ref, g_ref, wg_ref, wu_ref, wd_ref, y_ref,
                       o_scr, num_ref, max_ref, den_ref, kt_ref):
    b = pl.program_id(0)
    hp = pl.program_id(1)
    n_pairs, seq = o_scr.shape[1], o_scr.shape[2]
    w = WINDOWS[0] // DILATIONS[0]
    lane = lax.broadcasted_iota(jnp.int32, (1, LANES), 1)
    first_head = lane < HEAD_DIM
    o_out = o_scr.at[b % 2, hp]
    o_in = o_scr.at[(b + 1) % 2]

    @pl.when((b == 0) & (hp == 0))
    def _():
        o_scr[1] = jnp.zeros(o_scr.shape[1:], o_scr.dtype)

    def ffn_pieces():
        rows = pl.ds(pl.multiple_of(hp * x_ref.shape[0], x_ref.shape[0]), x_ref.shape[0])
        o = jnp.concatenate([o_in[p, rows, :] for p in range(n_pairs)], axis=1)
        x = x_ref[...] + jnp.dot(o, wo_ref[...], preferred_element_type=F32)
        h = _rms(x, g_ref[...]).astype(BF16)
        yield
        acts = []
        n_chunks = wg_ref.shape[1] // MXU_DIM
        for c in range(n_chunks):
            cols = slice(c * MXU_DIM, (c + 1) * MXU_DIM)
            gate = jnp.dot(h, wg_ref[:, cols], preferred_element_type=F32)
            up = jnp.dot(h, wu_ref[:, cols], preferred_element_type=F32)
            acts.append((_silu(gate) * up).astype(BF16))
            yield
        half = n_chunks // 2
        acc = jnp.dot(jnp.concatenate(acts[:half], axis=1), wd_ref[:half * MXU_DIM, :], preferred_element_type=F32)
        yield
        acc = acc + jnp.dot(jnp.concatenate(acts[half:], axis=1), wd_ref[half * MXU_DIM:, :],
                            preferred_element_type=F32)
        y_ref[...] = x + acc
        yield

    def tile(q_ref, v_ref, qoff, has_prev):
        q = q_ref[qoff:qoff + w, :]
        zero = jnp.zeros_like(q)
        qq = jnp.concatenate([jnp.where(first_head, q, zero), jnp.where(first_head, zero, q)], axis=0)
        k_lo = qoff - w if has_prev else qoff
        s = jnp.dot(qq, kt_ref[:, k_lo:qoff + w], preferred_element_type=F32)
        s = s + (band_ref[...] if has_prev else causal_ref[...])
        mx = jnp.max(s, axis=1, keepdims=True)
        p = jnp.exp2(s - mx)
        den = jnp.sum(p, axis=1, keepdims=True)
        pv = jnp.dot(p.astype(BF16), v_ref[k_lo:qoff + w, :], preferred_element_type=F32)
        return tuple(jnp.where(first_head, a[:w], a[w:]) for a in (pv, mx, den))

    pieces = ffn_pieces()
    next(pieces)
    tiles_done = 0
    order = sorted(range(N_GROUPS), key=lambda g: -DILATIONS[g])
    for g in order:
        q_ref, k_ref, v_ref = ((q0, k0, v0), (q1, k1, v1), (q2, k2, v2))[g]
        dil = DILATIONS[g]
        sub_len = seq // dil
        nb = sub_len // w
        kt_ref[...] = k_ref[...].T
        for t in range(dil * nb):
            r, n = divmod(t, nb)
            qoff = r * sub_len + n * w
            pv, mx, den = tile(q_ref, v_ref, qoff, n > 0)
            tiles_done += 1
            if tiles_done % ATTN_TILES_PER_FFN_PIECE == 0:
                next(pieces, None)
            rows = pl.ds(qoff, w) if dil == 1 else pl.ds(r + dil * n * w, w, stride=dil)
            if g == order[0]:
                num_ref[rows, :] = pv
                max_ref[rows, :] = mx
                den_ref[rows, :] = den
                continue
            old_max = max_ref[rows, :]
            new_max = jnp.maximum(old_max, mx)
            a = jnp.exp2(old_max - new_max)
            c = jnp.exp2(mx - new_max)
            num = a * num_ref[rows, :] + c * pv
            den = a * den_ref[rows, :] + c * den
            if g == order[-1]:
                o_out[rows, :] = (num / den).astype(BF16)
            else:
                num_ref[rows, :] = num
                den_ref[rows, :] = den
                max_ref[rows, :] = new_max
    for _ in pieces:
        pass


def _attention_biases(w):
    row = jnp.arange(2 * w)[:, None] % w
    col = jnp.arange(2 * w)[None, :]
    dist = row + w - col
    band = jnp.where((dist >= 0) & (dist <= w), 0.0, MASK_VALUE).astype(F32)
    causal = jnp.where(row >= col[:, :w], 0.0, MASK_VALUE).astype(F32)
    return band, causal


def _attention_and_dense_ffn(qkv_groups, x, w_o, gain, w_gate, w_up, w_down, batch, seq):
    n, d = x.shape
    ff = w_gate.shape[1]
    n_pairs = d // LANES
    rows = seq // n_pairs
    last = batch - 1
    in_specs, args = [], []
    for arr in qkv_groups:
        for kind in range(3):
            in_specs.append(pl.BlockSpec((None, None, seq, LANES),
                                         lambda b, hp, kind=kind: (kind, jnp.minimum(b, last), 0, hp)))
            args.append(arr)
    const = lambda shape: pl.BlockSpec(shape, lambda b, hp: (0, 0), pipeline_mode=pl.Buffered(1))
    ffn_in = pl.BlockSpec((rows, d), lambda b, hp: (jnp.maximum(b - 1, 0) * n_pairs + hp, 0))
    ffn_out = pl.BlockSpec((rows, d), lambda b, hp: (jnp.where(b == 0, batch, b - 1) * n_pairs + hp, 0))
    biases = _attention_biases(WINDOWS[0] // DILATIONS[0])
    in_specs += [const(t.shape) for t in biases]
    in_specs += [ffn_in, const((d, d)), const((1, d)), const((d, ff)), const((d, ff)), const((ff, d))]
    args += [*biases, x, w_o, gain, w_gate, w_up, w_down]
    return pl.pallas_call(
        _attn_dense_kernel,
        grid=(batch + 1, n_pairs),
        in_specs=in_specs,
        out_specs=ffn_out,
        out_shape=jax.ShapeDtypeStruct((n + seq, d), F32),
        scratch_shapes=[pltpu.VMEM((2, n_pairs, seq, LANES), BF16)]
        + [pltpu.VMEM((seq, LANES), F32)] * 3 + [pltpu.VMEM((LANES, seq), BF16)],
        compiler_params=_params("arbitrary", "arbitrary"),
        name="attention_ffn",
    )(*args)


POOL_HALO = max(POOL_SIZES)


def _pool_tile(i, x_ref, halo_ref, g_ref, w_ref, sc_ref, y_ref):
    ts = x_ref.shape[0]
    g = g_ref[...]
    x = x_ref[...]
    h = _rms(x, g)
    hh = jnp.where(i > 0, _rms(halo_ref[...], g), 0.0)
    ext = jnp.concatenate([hh, h], axis=0)
    pos = i * ts + lax.broadcasted_iota(jnp.int32, (ts, 1), 0)
    pc = w_ref.shape[1]
    for grp, size in enumerate(POOL_SIZES):
        sl = slice(grp * pc, (grp + 1) * pc)
        s = ext[:, sl]
        step = 1
        while step < size:
            s = s + pltpu.roll(s, step, 0)
            step *= 2
        cnt = jnp.minimum(pos + 1, size).astype(F32)
        y = s[POOL_HALO:, :] / cnt - h[:, sl]
        z = jnp.dot(y.astype(BF16), w_ref[grp], preferred_element_type=F32)
        y_ref[:, sl] = x[:, sl] + z * sc_ref[:, sl]


def _pool_router_kernel(x_ref, halo_ref, g_ref, w_ref, sc_ref, g2_ref, wr_ref, tril_ref,
                        y_ref, ri_ref, rg_ref, cnt_ref, hp_ref, run_ref):
    b = pl.program_id(0)
    i = pl.program_id(1)
    _pool_tile(i, x_ref, halo_ref, g_ref, w_ref, sc_ref, y_ref)
    _route_tile((b == 0) & (i == 0), y_ref[...], g2_ref, wr_ref, tril_ref, ri_ref, rg_ref, cnt_ref, hp_ref, run_ref)


def _pool_mixer_and_router(x, gain, pool_w, scale, ffn_gain, w_router, batch, seq):
    d = x.shape[1]
    n = batch * seq
    ne = w_router.shape[1]
    ts = 512
    tiles = seq // ts
    x3 = x.reshape(-1, seq, d)
    hb = ts // POOL_HALO
    tril = (jnp.arange(ts)[:, None] > jnp.arange(ts)[None, :]).astype(BF16)
    const = lambda shape: pl.BlockSpec(shape, lambda b, i: (0,) * len(shape))
    tok = lambda cols: pl.BlockSpec((ts, cols), lambda b, i: (b * tiles + i, 0))
    out, route_i, route_g, counts, h_packed = pl.pallas_call(
        _pool_router_kernel,
        grid=(batch, tiles),
        in_specs=[
            pl.BlockSpec((None, ts, d), lambda b, i: (b, i, 0)),
            pl.BlockSpec((None, POOL_HALO, d), lambda b, i: (b, jnp.maximum(i * hb - 1, 0), 0)),
            const((1, d)), const(pool_w.shape), const((1, d)), const((1, d)), const((d, ne)), const((ts, ts)),
        ],
        out_specs=[
            pl.BlockSpec((None, ts, d), lambda b, i: (b, i, 0)),
            tok(ROUTE_COLS), tok(ROUTE_COLS), const((1, ne)), tok(d // 2),
        ],
        out_shape=[
            jax.ShapeDtypeStruct((batch, seq, d), F32),
            jax.ShapeDtypeStruct((n, ROUTE_COLS), jnp.int32),
            jax.ShapeDtypeStruct((n, ROUTE_COLS), F32),
            jax.ShapeDtypeStruct((1, ne), F32),
            jax.ShapeDtypeStruct((n, d // 2), jnp.int32),
        ],
        scratch_shapes=[pltpu.VMEM((1, ne), F32)],
        compiler_params=_params("arbitrary", "arbitrary"),
        name="pool_router",
    )(x3, x3, gain, pool_w, scale, ffn_gain, w_router, tril)
    return out.reshape(n, d), route_i, route_g, counts, h_packed


ROUTE_COLS = 8


def _pack_bf16_pairs(hr):
    c = hr.shape[1] // 2
    bits = pltpu.bitcast(hr, jnp.int32)
    return (bits[:, c:] & jnp.int32(-65536)) | lax.shift_right_logical(bits[:, :c], 16)


def _unpack_bf16_pairs(packed):
    lo = pltpu.bitcast(lax.shift_left(packed, 16), F32)
    hi = pltpu.bitcast(packed & jnp.int32(-65536), F32)
    return jnp.concatenate([lo.astype(BF16), hi.astype(BF16)], axis=1)


def _route_tile(first, x, g_ref, wr_ref, tril_ref, ri_ref, rg_ref, cnt_ref, hp_ref, run_ref):
    @pl.when(first)
    def _():
        run_ref[...] = jnp.zeros_like(run_ref)

    h = _rms(x, g_ref[...])
    wr = wr_ref[...]
    ne = wr.shape[1]
    h_hi = h.astype(BF16)
    hp_ref[...] = _pack_bf16_pairs(h_hi.astype(F32))
    h_lo = (h - h_hi.astype(F32)).astype(BF16)
    w_hi = wr.astype(BF16)
    w_lo = (wr - w_hi.astype(F32)).astype(BF16)
    hi_part = jnp.dot(h_hi, jnp.concatenate([w_hi, w_lo], axis=1), preferred_element_type=F32)
    logits = hi_part[:, :ne] + hi_part[:, ne:] + jnp.dot(h_lo, w_hi, preferred_element_type=F32)
    tm = logits.shape[0]
    lane = lax.broadcasted_iota(jnp.int32, (tm, ne), 1)
    v1 = jnp.max(logits, axis=1, keepdims=True)
    i1 = jnp.min(jnp.where(logits == v1, lane, ne), axis=1, keepdims=True)
    rest = jnp.where(lane == i1, -jnp.inf, logits)
    v2 = jnp.max(rest, axis=1, keepdims=True)
    i2 = jnp.min(jnp.where(rest == v2, lane, ne), axis=1, keepdims=True)
    e = jnp.exp(v2 - v1)
    g1 = 1.0 / (1.0 + e)
    g2 = e / (1.0 + e)
    oh1 = (lane == i1).astype(F32)
    oh2 = (lane == i2).astype(F32)
    both = oh1 + oh2
    before = jnp.dot(tril_ref[...], both.astype(BF16), preferred_element_type=F32) + run_ref[...]
    r1 = jnp.sum(before * oh1, axis=1, keepdims=True).astype(jnp.int32)
    r2 = jnp.sum(before * oh2, axis=1, keepdims=True).astype(jnp.int32)
    run_ref[...] += jnp.sum(both, axis=0, keepdims=True)
    ri_ref[...] = jnp.where(lane == 0, i1, jnp.where(lane == 1, i2, jnp.where(lane == 2, r1, r2)))
    rg_ref[...] = jnp.where(lane == 0, g1, g2)
    cnt_ref[...] = run_ref[...]


SC_WINDOW = 64


def _sc_workers():
    info = plsc.get_sparse_core_info()
    mesh = plsc.VectorSubcoreMesh(core_axis_name="core", subcore_axis_name="subcore")
    return mesh, info.num_cores, info.num_cores * info.num_subcores


def _dispatch(x, dest, cap):
    n, d = x.shape
    mesh, n_cores, n_workers = _sc_workers()
    per_worker = n // n_workers

    n_windows = per_worker // SC_WINDOW
    n_bufs = 2
    assert n_windows % n_bufs == 0

    @functools.partial(
        pl.kernel, mesh=mesh, out_type=jax.ShapeDtypeStruct((cap, d), x.dtype),
        scratch_types=[pltpu.VMEM((SC_WINDOW,), jnp.int32)] * (TOP_K * n_bufs)
        + [pltpu.VMEM((SC_WINDOW, d), x.dtype)] * n_bufs + [pltpu.SemaphoreType.DMA] * n_bufs,
        name="moe_dispatch")
    def scatter(x_hbm, *rest):
        idx_hbm, (xs_hbm, *scratch) = rest[:TOP_K], rest[TOP_K:]
        idx_vmem = [scratch[b * TOP_K:(b + 1) * TOP_K] for b in range(n_bufs)]
        rows = scratch[TOP_K * n_bufs:TOP_K * n_bufs + n_bufs]
        sems = scratch[TOP_K * n_bufs + n_bufs:]
        wid = lax.axis_index("subcore") * n_cores + lax.axis_index("core")

        def drain(buf):
            for k in range(TOP_K):
                pltpu.make_async_copy(rows[buf], xs_hbm.at[idx_vmem[buf][k]], sems[buf]).wait()

        @pl.loop(0, n_windows, step=n_bufs)
        def _(c):
            for buf in range(n_bufs):
                @pl.when(c > 0)
                def _(buf=buf):
                    drain(buf)

                window = pl.ds(wid * per_worker + (c + buf) * SC_WINDOW, SC_WINDOW)
                for k in range(TOP_K):
                    pltpu.sync_copy(idx_hbm[k].at[window], idx_vmem[buf][k])
                pltpu.sync_copy(x_hbm.at[window], rows[buf])
                for k in range(TOP_K):
                    pltpu.async_copy(rows[buf], xs_hbm.at[idx_vmem[buf][k]], sems[buf])

        for buf in range(n_bufs):
            drain(buf)

    return scatter(x, *[dest[k] for k in range(TOP_K)])


def _gather_rows(table, idx):
    m = idx.shape[0]
    d = table.shape[1]
    mesh, n_cores, n_workers = _sc_workers()
    per_worker = m // n_workers

    n_windows = per_worker // SC_WINDOW
    n_bufs = 2
    assert n_windows % n_bufs == 0

    @functools.partial(
        pl.kernel, mesh=mesh, out_type=jax.ShapeDtypeStruct((m, d), table.dtype),
        scratch_types=[pltpu.VMEM((SC_WINDOW,), jnp.int32)] * n_bufs
        + [pltpu.VMEM((SC_WINDOW, d), table.dtype)] * n_bufs + [pltpu.SemaphoreType.DMA] * n_bufs,
        name="moe_gather")
    def gather(t_hbm, i_hbm, o_hbm, *scratch):
        idx_vmem, rows, sems = scratch[:n_bufs], scratch[n_bufs:2 * n_bufs], scratch[2 * n_bufs:]
        wid = lax.axis_index("subcore") * n_cores + lax.axis_index("core")

        def window(c):
            return pl.ds(wid * per_worker + c * SC_WINDOW, SC_WINDOW)

        def fetch(c, buf):
            pltpu.sync_copy(i_hbm.at[window(c)], idx_vmem[buf])
            return pltpu.async_copy(t_hbm.at[idx_vmem[buf]], rows[buf], sems[buf])

        fetch(0, 0)

        @pl.loop(0, n_windows, step=n_bufs)
        def _(c):
            for buf in range(n_bufs):
                nxt = c + buf + 1

                @pl.when(nxt < n_windows)
                def _(nxt=nxt, buf=buf):
                    fetch(nxt, (buf + 1) % n_bufs)

                pltpu.make_async_copy(t_hbm.at[idx_vmem[buf]], rows[buf], sems[buf]).wait()
                pltpu.sync_copy(rows[buf], o_hbm.at[window(c + buf)])

    return gather(table, idx)


def _expert_kernel(be_ref, nv_ref, xs_ref, wg_ref, wu_ref, wd_ref, ys_ref, xb_ref, acc_ref):
    del be_ref
    b = pl.program_id(0)
    j = pl.program_id(1)

    def ff_tile(first):
        x = xb_ref[...]
        acts = []
        for c in range(wg_ref.shape[1] // MXU_DIM):
            cols = slice(c * MXU_DIM, (c + 1) * MXU_DIM)
            gate = jnp.dot(x, wg_ref[:, cols].astype(BF16), preferred_element_type=F32)
            up = jnp.dot(x, wu_ref[:, cols].astype(BF16), preferred_element_type=F32)
            acts.append((_silu(gate) * up).astype(BF16))
        part = jnp.dot(jnp.concatenate(acts, axis=1), wd_ref[...].astype(BF16), preferred_element_type=F32)
        if first:
            acc_ref[...] = part
        else:
            acc_ref[...] += part

    valid = b < nv_ref[0]

    @pl.when(valid & (j == 0))
    def _():
        xb_ref[...] = _unpack_bf16_pairs(xs_ref[...])
        ff_tile(True)

    @pl.when(valid & (j > 0))
    def _():
        ff_tile(False)

    @pl.when(jnp.logical_not(valid) & (j == 0))
    def _():
        acc_ref[...] = jnp.zeros_like(acc_ref)

    @pl.when(j == pl.num_programs(1) - 1)
    def _():
        ys_ref[...] = _pack_bf16_pairs(acc_ref[...].astype(BF16).astype(F32))


def _experts(xs, block_e, n_valid, w_gate, w_up, w_down, layer, tm, tf):
    cap = xs.shape[0]
    d = 2 * xs.shape[1]
    ff = w_gate.shape[3]
    nj = ff // tf

    def blk(b, nv):
        return jnp.minimum(b, nv[0] - 1)

    def ffi(b, j, nv):
        return jnp.where(b < nv[0], j, nj - 1)

    return pl.pallas_call(
        _expert_kernel,
        grid_spec=pltpu.PrefetchScalarGridSpec(
            num_scalar_prefetch=2,
            grid=(cap // tm, nj),
            in_specs=[
                pl.BlockSpec((tm, d // 2), lambda b, j, be, nv: (blk(b, nv), 0)),
                pl.BlockSpec((None, None, d, tf), lambda b, j, be, nv: (layer, be[blk(b, nv)], 0, ffi(b, j, nv))),
                pl.BlockSpec((None, None, d, tf), lambda b, j, be, nv: (layer, be[blk(b, nv)], 0, ffi(b, j, nv))),
                pl.BlockSpec((None, None, tf, d), lambda b, j, be, nv: (layer, be[blk(b, nv)], ffi(b, j, nv), 0)),
            ],
            out_specs=pl.BlockSpec((tm, d // 2), lambda b, j, be, nv: (b, 0)),
            scratch_shapes=[pltpu.VMEM((tm, d), BF16), pltpu.VMEM((tm, d), F32)],
        ),
        out_shape=jax.ShapeDtypeStruct((cap, d // 2), jnp.int32),
        compiler_params=_params("arbitrary", "arbitrary"),
        name="moe_experts",
    )(block_e, n_valid, xs, w_gate, w_up, w_down)


def _combine_kernel(x_ref, rg_ref, gf_ref, y0_ref, y1_ref, o_ref, *, final_norm):
    rg = rg_ref[...]
    y0 = _unpack_bf16_pairs(y0_ref[...]).astype(F32)
    y1 = _unpack_bf16_pairs(y1_ref[...]).astype(F32)
    out = x_ref[...] + (rg[:, 0:1] * y0 + rg[:, 1:2] * y1)
    if final_norm:
        out = _rms(out, gf_ref[...])
    o_ref[...] = out


def _combine(x, route_g, yg, final_gain, final_norm):
    n, d = x.shape
    tm = 512
    return pl.pallas_call(
        functools.partial(_combine_kernel, final_norm=final_norm),
        grid=(n // tm,),
        in_specs=[
            pl.BlockSpec((tm, d), lambda i: (i, 0)),
            pl.BlockSpec((tm, ROUTE_COLS), lambda i: (i, 0)),
            pl.BlockSpec((1, d), lambda i: (0, 0)),
            pl.BlockSpec((None, tm, d // 2), lambda i: (0, i, 0)),
            pl.BlockSpec((None, tm, d // 2), lambda i: (1, i, 0)),
        ],
        out_specs=pl.BlockSpec((tm, d), lambda i: (i, 0)),
        out_shape=jax.ShapeDtypeStruct((n, d), F32),
        compiler_params=_params("parallel"),
        name="moe_combine",
    )(x, route_g, final_gain, yg, yg)


def _moe_layer(x, routing, w_gate, w_up, w_down, layer, final_gain, final_norm):
    n, d = x.shape
    tm = 1024 if n * TOP_K >= 8 * 1024 else 256
    tf = 2 * MXU_DIM
    route_i, route_g, counts, h_packed = routing

    counts = counts[0].astype(jnp.int32)
    padded = ((counts + tm - 1) // tm) * tm
    ends = jnp.cumsum(padded)
    starts = ends - padded
    dest = (starts[route_i[:, :TOP_K]] + route_i[:, TOP_K:2 * TOP_K]).T
    n_blocks = (n * TOP_K) // tm + N_EXPERTS
    block_start = jnp.arange(n_blocks, dtype=jnp.int32) * tm
    block_e = jnp.minimum(jnp.sum(ends[None, :] <= block_start[:, None], axis=1), N_EXPERTS - 1).astype(jnp.int32)
    n_valid = (ends[-1:] // tm).astype(jnp.int32)

    xs = _dispatch(h_packed, dest, n_blocks * tm)
    ys = _experts(xs, block_e, n_valid, w_gate, w_up, w_down, layer, tm, tf)
    yg = _gather_rows(ys, dest.reshape(TOP_K * n)).reshape(TOP_K, n, d // 2)
    return _combine(x, route_g, yg, final_gain, final_norm)


def _rope_tables(seq):
    pos = jnp.arange(seq, dtype=F32)
    inv_freq = ROPE_THETA ** (-jnp.arange(0, HEAD_DIM, 2, dtype=F32) / HEAD_DIM)
    ang = pos[:, None] * inv_freq[None, :]
    reps = LANES // (HEAD_DIM // 2)
    cos = jnp.tile(jnp.cos(ang), (1, reps))
    sin = jnp.tile(jnp.sin(ang), (1, reps))
    first_half = (jnp.arange(LANES) % HEAD_DIM) < HEAD_DIM // 2
    tabs = (cos, jnp.where(first_half, -sin, sin))
    q_scale = HEAD_DIM ** -0.5 * 1.4426950408889634
    return tuple(jnp.stack([t * q_scale, t]) for t in tabs)


def kernel(x, mix_norm, ffn_norm, attn_w_qkv, attn_w_o, pool_w, pool_scale, dense_w_gate, dense_w_up,
           dense_w_down, moe_router, moe_w_gate, moe_w_up, moe_w_down, final_norm):
    batch, seq, d = x.shape
    depth = mix_norm.shape[0]
    assert depth % 2 == 0 and d == N_HEADS * HEAD_DIM
    rope_tables = _rope_tables(seq)
    xf = x.reshape(batch * seq, d)
    final_gain = final_norm.reshape(1, d)
    for i in range(depth):
        j = i // 2
        mix_gain = mix_norm[i].reshape(1, d)
        ffn_gain = ffn_norm[i].reshape(1, d)
        if i % 2 == 0:
            w_qkv = attn_w_qkv[j].astype(BF16)
            qkv = _qkv_project(xf, mix_gain, w_qkv, rope_tables, batch, seq)
            xf = _attention_and_dense_ffn(qkv, xf, attn_w_o[j].astype(BF16), ffn_gain, dense_w_gate[j].astype(BF16),
                                          dense_w_up[j].astype(BF16), dense_w_down[j].astype(BF16), batch, seq)
        else:
            xf, *routing = _pool_mixer_and_router(xf, mix_gain, pool_w[j].astype(BF16), pool_scale[j].reshape(1, d),
                                                  ffn_gain, moe_router[j], batch, seq)
            xf = _moe_layer(xf, routing, moe_w_gate, moe_w_up, moe_w_down, j, final_gain, i == depth - 1)
    return xf.reshape(batch, seq, d)
```

```python
import functools

import jax
import jax.numpy as jnp
from jax import lax
from jax.experimental import pallas as pl
from jax.experimental.pallas import tpu as pltpu
from jax.experimental.pallas import tpu_sc as plsc

WINDOWS = (128, 512, 2048)
DILATIONS = (1, 4, 16)
N_GROUPS = len(WINDOWS)
N_HEADS = 16
HEAD_DIM = 64
ROPE_THETA = 10000.0
POOL_SIZES = (2, 4, 8, 16)
N_EXPERTS = 8
TOP_K = 2
RMS_EPS = 1e-6
MASK_VALUE = -1e30
QKV_ROWS = 512

LANES = 128
MXU_DIM = 256
VMEM_LIMIT_BYTES = 56 * 1024 * 1024

F32 = jnp.float32
BF16 = jnp.bfloat16


def _params(*sem):
    return pltpu.CompilerParams(dimension_semantics=sem, vmem_limit_bytes=VMEM_LIMIT_BYTES)


def _rms(x, g):
    ms = jnp.mean(x * x, axis=-1, keepdims=True)
    return x * lax.rsqrt(ms + RMS_EPS) * g


def _silu(g):
    return g / (1.0 + jnp.exp(-g))


def _qkv_kernel(x_ref, g_ref, w_ref, cos_ref, sin_ref, *rest, pending_moe):
    if pending_moe:
        rg_ref, y0_ref, y1_ref, o0_ref, o1_ref, o2_ref, xn_ref, acc_ref = rest
        rg = rg_ref[...]
        x = x_ref[...] + (rg[:, 0:1] * _unpack_bf16_pairs(y0_ref[...]).astype(F32)
                          + rg[:, 1:2] * _unpack_bf16_pairs(y1_ref[...]).astype(F32))
        xn_ref[...] = x
    else:
        o0_ref, o1_ref, o2_ref, acc_ref = rest
        x = x_ref[...]
    tm, d = x_ref.shape
    n_chunks = acc_ref.shape[1]
    lane = lax.broadcasted_iota(jnp.int32, (1, LANES), 1)
    first_half = (lane % HEAD_DIM) < HEAD_DIM // 2
    h = _rms(x, g_ref[...]).astype(BF16)
    for g, (o_ref, dil) in enumerate(zip((o0_ref, o1_ref, o2_ref), DILATIONS)):
        rows = tm // dil

        def strided(ref, r, dil=dil, rows=rows):
            if dil == 1:
                return ref[...]
            return ref[pl.ds(r, rows, stride=dil), :]

        for kind in range(3):
            col0 = (g * 3 + kind) * d
            acc = jnp.dot(h, w_ref[:, col0:col0 + d], preferred_element_type=F32)
            for c in range(n_chunks):
                acc_ref[kind, c] = acc[:, c * LANES:(c + 1) * LANES]
            for r in range(dil):
                if kind < 2:
                    cos = strided(cos_ref.at[kind], r)
                    sin = strided(sin_ref.at[kind], r)
                for c in range(n_chunks):
                    t = strided(acc_ref.at[kind, c], r)
                    if kind < 2:
                        partner = jnp.where(first_half, pltpu.roll(t, LANES - HEAD_DIM // 2, 1),
                                            pltpu.roll(t, HEAD_DIM // 2, 1))
                        t = t * cos + partner * sin
                    o_ref[kind, r, :, c * LANES:(c + 1) * LANES] = t.astype(BF16)


def _qkv_project(x, gain, w_qkv, tables, batch, seq, pending_moe=None):
    n, d = x.shape
    tm = QKV_ROWS // 2 if pending_moe else QKV_ROWS
    tiles_per_seq = seq // tm
    tab = pl.BlockSpec((2, tm, LANES), lambda i: (0, i % tiles_per_seq, 0))

    def o_spec(dil):
        return pl.BlockSpec((3, None, dil, tm // dil, d),
                            lambda i: (0, i // tiles_per_seq, 0, i % tiles_per_seq, 0))

    row = pl.BlockSpec((tm, d), lambda i: (i, 0))
    in_specs = [row, pl.BlockSpec((1, d), lambda i: (0, 0)),
                pl.BlockSpec(w_qkv.shape, lambda i: (0, 0), pipeline_mode=pl.Buffered(1)), tab, tab]
    args = [x, gain, w_qkv, *tables]
    out_specs = [o_spec(dil) for dil in DILATIONS]
    out_shape = [jax.ShapeDtypeStruct((3, batch, dil, seq // dil, d), BF16) for dil in DILATIONS]
    if pending_moe:
        route_g, yg = pending_moe
        in_specs += [pl.BlockSpec((tm, ROUTE_COLS), lambda i: (i, 0))]
        in_specs += [pl.BlockSpec((None, tm, d // 2), lambda i, k=k: (k, i, 0)) for k in range(TOP_K)]
        args += [route_g, yg, yg]
        out_specs.append(row)
        out_shape.append(jax.ShapeDtypeStruct((n, d), F32))
    outs = pl.pallas_call(
        functools.partial(_qkv_kernel, pending_moe=bool(pending_moe)),
        grid=(n // tm,),
        in_specs=in_specs,
        out_specs=out_specs,
        out_shape=out_shape,
        scratch_shapes=[pltpu.VMEM((3, d // LANES, tm, LANES), F32)],
        compiler_params=_params("parallel"),
        name="qkv_rope",
    )(*args)
    return [o.reshape(3, batch, seq, d) for o in outs[:N_GROUPS]], (outs[N_GROUPS] if pending_moe else x)


ATTN_TILES_PER_FFN_PIECE = 3


def _attn_dense_kernel(q0, k0, v0, q1, k1, v1, q2, k2, v2, band_ref, causal_ref,
                       x_ref, wo_ref, g_ref, wg_ref, wu_ref, wd_ref, y_ref,
                       o_scr, num_ref, max_ref, den_ref, kt_ref):
    b = pl.program_id(0)
    hp = pl.program_id(1)
    n_pairs, seq = o_scr.shape[1], o_scr.shape[2]
    w = WINDOWS[0] // DILATIONS[0]
    lane = lax.broadcasted_iota(jnp.int32, (1, LANES), 1)
    first_head = lane < HEAD_DIM
    o_out = o_scr.at[b % 2, hp]
    o_in = o_scr.at[(b + 1) % 2]

    @pl.when((b == 0) & (hp == 0))
    def _():
        o_scr[1] = jnp.zeros(o_scr.shape[1:], o_scr.dtype)

    def ffn_pieces():
        rows = pl.ds(pl.multiple_of(hp * x_ref.shape[0], x_ref.shape[0]), x_ref.shape[0])
        o = jnp.concatenate([o_in[p, rows, :] for p in range(n_pairs)], axis=1)
        x = x_ref[...] + jnp.dot(o, wo_ref[...], preferred_element_type=F32)
        h = _rms(x, g_ref[...]).astype(BF16)
        yield
        acts = []
        n_chunks = wg_ref.shape[1] // MXU_DIM
        for c in range(n_chunks):
            cols = slice(c * MXU_DIM, (c + 1) * MXU_DIM)
            gate = jnp.dot(h, wg_ref[:, cols], preferred_element_type=F32)
            up = jnp.dot(h, wu_ref[:, cols], preferred_element_type=F32)
            acts.append((_silu(gate) * up).astype(BF16))
            yield
        half = n_chunks // 2
        acc = jnp.dot(jnp.concatenate(acts[:half], axis=1), wd_ref[:half * MXU_DIM, :], preferred_element_type=F32)
        yield
        acc = acc + jnp.dot(jnp.concatenate(acts[half:], axis=1), wd_ref[half * MXU_DIM:, :],
                            preferred_element_type=F32)
        y_ref[...] = x + acc
        yield

    def tile(q_ref, v_ref, qoff, has_prev):
        q = q_ref[qoff:qoff + w, :]
        zero = jnp.zeros_like(q)
        qq = jnp.concatenate([jnp.where(first_head, q, zero), jnp.where(first_head, zero, q)], axis=0)
        k_lo = qoff - w if has_prev else qoff
        s = jnp.dot(qq, kt_ref[:, k_lo:qoff + w], preferred_element_type=F32)
        s = s + (band_ref[...] if has_prev else causal_ref[...])
        mx = jnp.max(s, axis=1, keepdims=True)
        p = jnp.exp2(s - mx)
        den = jnp.sum(p, axis=1, keepdims=True)
        pv = jnp.dot(p.astype(BF16), v_ref[k_lo:qoff + w, :], preferred_element_type=F32)
        return tuple(jnp.where(first_head, a[:w], a[w:]) for a in (pv, mx, den))

    pieces = ffn_pieces()
    next(pieces)
    tiles_done = 0
    order = sorted(range(N_GROUPS), key=lambda g: -DILATIONS[g])
    for g in order:
        q_ref, k_ref, v_ref = ((q0, k0, v0), (q1, k1, v1), (q2, k2, v2))[g]
        dil = DILATIONS[g]
        sub_len = seq // dil
        nb = sub_len // w
        kt_ref[...] = k_ref[...].T
        for t in range(dil * nb):
            r, n = divmod(t, nb)
            qoff = r * sub_len + n * w
            pv, mx, den = tile(q_ref, v_ref, qoff, n > 0)
            tiles_done += 1
            if tiles_done % ATTN_TILES_PER_FFN_PIECE == 0:
                next(pieces, None)
            rows = pl.ds(qoff, w) if dil == 1 else pl.ds(r + dil * n * w, w, stride=dil)
            if g == order[0]:
                num_ref[rows, :] = pv
                max_ref[rows, :] = mx
                den_ref[rows, :] = den
                continue
            old_max = max_ref[rows, :]
            new_max = jnp.maximum(old_max, mx)
            a = jnp.exp2(old_max - new_max)
            c = jnp.exp2(mx - new_max)
            num = a * num_ref[rows, :] + c * pv
            den = a * den_ref[rows, :] + c * den
            if g == order[-1]:
                o_out[rows, :] = (num / den).astype(BF16)
            else:
                num_ref[rows, :] = num
                den_ref[rows, :] = den
                max_ref[rows, :] = new_max
    for _ in pieces:
        pass


def _attention_biases(w):
    row = jnp.arange(2 * w)[:, None] % w
    col = jnp.arange(2 * w)[None, :]
    dist = row + w - col
    band = jnp.where((dist >= 0) & (dist <= w), 0.0, MASK_VALUE).astype(F32)
    causal = jnp.where(row >= col[:, :w], 0.0, MASK_VALUE).astype(F32)
    return band, causal


def _attention_and_dense_ffn(qkv_groups, x, w_o, gain, w_gate, w_up, w_down, batch, seq):
    n, d = x.shape
    ff = w_gate.shape[1]
    n_pairs = d // LANES
    rows = seq // n_pairs
    last = batch - 1
    in_specs, args = [], []
    for arr in qkv_groups:
        for kind in range(3):
            in_specs.append(pl.BlockSpec((None, None, seq, LANES),
                                         lambda b, hp, kind=kind: (kind, jnp.minimum(b, last), 0, hp)))
            args.append(arr)
    const = lambda shape: pl.BlockSpec(shape, lambda b, hp: (0, 0), pipeline_mode=pl.Buffered(1))
    ffn_in = pl.BlockSpec((rows, d), lambda b, hp: (jnp.maximum(b - 1, 0) * n_pairs + hp, 0))
    ffn_out = pl.BlockSpec((rows, d), lambda b, hp: (jnp.where(b == 0, batch, b - 1) * n_pairs + hp, 0))
    biases = _attention_biases(WINDOWS[0] // DILATIONS[0])
    in_specs += [const(t.shape) for t in biases]
    in_specs += [ffn_in, const((d, d)), const((1, d)), const((d, ff)), const((d, ff)), const((ff, d))]
    args += [*biases, x, w_o, gain, w_gate, w_up, w_down]
    return pl.pallas_call(
        _attn_dense_kernel,
        grid=(batch + 1, n_pairs),
        in_specs=in_specs,
        out_specs=ffn_out,
        out_shape=jax.ShapeDtypeStruct((n + seq, d), F32),
        scratch_shapes=[pltpu.VMEM((2, n_pairs, seq, LANES), BF16)]
        + [pltpu.VMEM((seq, LANES), F32)] * 3 + [pltpu.VMEM((LANES, seq), BF16)],
        compiler_params=_params("arbitrary", "arbitrary"),
        name="attention_ffn",
    )(*args)


POOL_HALO = max(POOL_SIZES)


def _pool_tile(i, x_ref, halo_ref, g_ref, w_ref, sc_ref, y_ref):
    ts = x_ref.shape[0]
    g = g_ref[...]
    x = x_ref[...]
    h = _rms(x, g)
    hh = jnp.where(i > 0, _rms(halo_ref[...], g), 0.0)
    ext = jnp.concatenate([hh, h], axis=0)
    pos = i * ts + lax.broadcasted_iota(jnp.int32, (ts, 1), 0)
    pc = w_ref.shape[1]
    for grp, size in enumerate(POOL_SIZES):
        sl = slice(grp * pc, (grp + 1) * pc)
        s = ext[:, sl]
        step = 1
        while step < size:
            s = s + pltpu.roll(s, step, 0)
            step *= 2
        cnt = jnp.minimum(pos + 1, size).astype(F32)
        y = s[POOL_HALO:, :] / cnt - h[:, sl]
        z = jnp.dot(y.astype(BF16), w_ref[grp], preferred_element_type=F32)
        y_ref[:, sl] = x[:, sl] + z * sc_ref[:, sl]


def _pool_router_kernel(x_ref, halo_ref, g_ref, w_ref, sc_ref, g2_ref, wr_ref, tril_ref,
                        y_ref, ri_ref, rg_ref, cnt_ref, hp_ref, run_ref):
    b = pl.program_id(0)
    i = pl.program_id(1)
    _pool_tile(i, x_ref, halo_ref, g_ref, w_ref, sc_ref, y_ref)
    _route_tile((b == 0) & (i == 0), y_ref[...], g2_ref, wr_ref, tril_ref, ri_ref, rg_ref, cnt_ref, hp_ref, run_ref)


def _pool_mixer_and_router(x, gain, pool_w, scale, ffn_gain, w_router, batch, seq):
    d = x.shape[1]
    n = batch * seq
    ne = w_router.shape[1]
    ts = 512
    tiles = seq // ts
    x3 = x.reshape(-1, seq, d)
    hb = ts // POOL_HALO
    tril = (jnp.arange(ts)[:, None] > jnp.arange(ts)[None, :]).astype(BF16)
    const = lambda shape: pl.BlockSpec(shape, lambda b, i: (0,) * len(shape))
    tok = lambda cols: pl.BlockSpec((ts, cols), lambda b, i: (b * tiles + i, 0))
    out, route_i, route_g, counts, h_packed = pl.pallas_call(
        _pool_router_kernel,
        grid=(batch, tiles),
        in_specs=[
            pl.BlockSpec((None, ts, d), lambda b, i: (b, i, 0)),
            pl.BlockSpec((None, POOL_HALO, d), lambda b, i: (b, jnp.maximum(i * hb - 1, 0), 0)),
            const((1, d)), const(pool_w.shape), const((1, d)), const((1, d)), const((d, ne)), const((ts, ts)),
        ],
        out_specs=[
            pl.BlockSpec((None, ts, d), lambda b, i: (b, i, 0)),
            tok(ROUTE_COLS), tok(ROUTE_COLS), const((1, ne)), tok(d // 2),
        ],
        out_shape=[
            jax.ShapeDtypeStruct((batch, seq, d), F32),
            jax.ShapeDtypeStruct((n, ROUTE_COLS), jnp.int32),
            jax.ShapeDtypeStruct((n, ROUTE_COLS), F32),
            jax.ShapeDtypeStruct((1, ne), F32),
            jax.ShapeDtypeStruct((n, d // 2), jnp.int32),
        ],
        scratch_shapes=[pltpu.VMEM((1, ne), F32)],
        compiler_params=_params("arbitrary", "arbitrary"),
        name="pool_router",
    )(x3, x3, gain, pool_w, scale, ffn_gain, w_router, tril)
    return out.reshape(n, d), route_i, route_g, counts, h_packed


ROUTE_COLS = 8


def _pack_bf16_pairs(hr):
    c = hr.shape[1] // 2
    bits = pltpu.bitcast(hr, jnp.int32)
    return (bits[:, c:] & jnp.int32(-65536)) | lax.shift_right_logical(bits[:, :c], 16)


def _unpack_bf16_pairs(packed):
    lo = pltpu.bitcast(lax.shift_left(packed, 16), F32)
    hi = pltpu.bitcast(packed & jnp.int32(-65536), F32)
    return jnp.concatenate([lo.astype(BF16), hi.astype(BF16)], axis=1)


def _route_tile(first, x, g_ref, wr_ref, tril_ref, ri_ref, rg_ref, cnt_ref, hp_ref, run_ref):
    @pl.when(first)
    def _():
        run_ref[...] = jnp.zeros_like(run_ref)

    h = _rms(x, g_ref[...])
    wr = wr_ref[...]
    ne = wr.shape[1]
    h_hi = h.astype(BF16)
    hp_ref[...] = _pack_bf16_pairs(h_hi.astype(F32))
    h_lo = (h - h_hi.astype(F32)).astype(BF16)
    w_hi = wr.astype(BF16)
    w_lo = (wr - w_hi.astype(F32)).astype(BF16)
    hi_part = jnp.dot(h_hi, jnp.concatenate([w_hi, w_lo], axis=1), preferred_element_type=F32)
    logits = hi_part[:, :ne] + hi_part[:, ne:] + jnp.dot(h_lo, w_hi, preferred_element_type=F32)
    tm = logits.shape[0]
    lane = lax.broadcasted_iota(jnp.int32, (tm, ne), 1)
    v1 = jnp.max(logits, axis=1, keepdims=True)
    i1 = jnp.min(jnp.where(logits == v1, lane, ne), axis=1, keepdims=True)
    rest = jnp.where(lane == i1, -jnp.inf, logits)
    v2 = jnp.max(rest, axis=1, keepdims=True)
    i2 = jnp.min(jnp.where(rest == v2, lane, ne), axis=1, keepdims=True)
    e = jnp.exp(v2 - v1)
    g1 = 1.0 / (1.0 + e)
    g2 = e / (1.0 + e)
    oh1 = (lane == i1).astype(F32)
    oh2 = (lane == i2).astype(F32)
    both = oh1 + oh2
    before = jnp.dot(tril_ref[...], both.astype(BF16), preferred_element_type=F32) + run_ref[...]
    r1 = jnp.sum(before * oh1, axis=1, keepdims=True).astype(jnp.int32)
    r2 = jnp.sum(before * oh2, axis=1, keepdims=True).astype(jnp.int32)
    run_ref[...] += jnp.sum(both, axis=0, keepdims=True)
    ri_ref[...] = jnp.where(lane == 0, i1, jnp.where(lane == 1, i2, jnp.where(lane == 2, r1, r2)))
    rg_ref[...] = jnp.where(lane == 0, g1, g2)
    cnt_ref[...] = run_ref[...]


SC_WINDOW = 64


def _sc_workers():
    info = plsc.get_sparse_core_info()
    mesh = plsc.VectorSubcoreMesh(core_axis_name="core", subcore_axis_name="subcore")
    return mesh, info.num_cores, info.num_cores * info.num_subcores


def _dispatch(x, dest, cap):
    n, d = x.shape
    mesh, n_cores, n_workers = _sc_workers()
    per_worker = n // n_workers

    n_windows = per_worker // SC_WINDOW
    n_bufs = 2
    assert n_windows % n_bufs == 0

    @functools.partial(
        pl.kernel, mesh=mesh, out_type=jax.ShapeDtypeStruct((cap, d), x.dtype),
        scratch_types=[pltpu.VMEM((SC_WINDOW,), jnp.int32)] * (TOP_K * n_bufs)
        + [pltpu.VMEM((SC_WINDOW, d), x.dtype)] * n_bufs + [pltpu.SemaphoreType.DMA] * n_bufs,
        name="moe_dispatch")
    def scatter(x_hbm, *rest):
        idx_hbm, (xs_hbm, *scratch) = rest[:TOP_K], rest[TOP_K:]
        idx_vmem = [scratch[b * TOP_K:(b + 1) * TOP_K] for b in range(n_bufs)]
        rows = scratch[TOP_K * n_bufs:TOP_K * n_bufs + n_bufs]
        sems = scratch[TOP_K * n_bufs + n_bufs:]
        wid = lax.axis_index("subcore") * n_cores + lax.axis_index("core")

        def drain(buf):
            for k in range(TOP_K):
                pltpu.make_async_copy(rows[buf], xs_hbm.at[idx_vmem[buf][k]], sems[buf]).wait()

        @pl.loop(0, n_windows, step=n_bufs)
        def _(c):
            for buf in range(n_bufs):
                @pl.when(c > 0)
                def _(buf=buf):
                    drain(buf)

                window = pl.ds(wid * per_worker + (c + buf) * SC_WINDOW, SC_WINDOW)
                for k in range(TOP_K):
                    pltpu.sync_copy(idx_hbm[k].at[window], idx_vmem[buf][k])
                pltpu.sync_copy(x_hbm.at[window], rows[buf])
                for k in range(TOP_K):
                    pltpu.async_copy(rows[buf], xs_hbm.at[idx_vmem[buf][k]], sems[buf])

        for buf in range(n_bufs):
            drain(buf)

    return scatter(x, *[dest[k] for k in range(TOP_K)])


def _gather_rows(table, idx):
    m = idx.shape[0]
    d = table.shape[1]
    mesh, n_cores, n_workers = _sc_workers()
    per_worker = m // n_workers

    n_windows = per_worker // SC_WINDOW
    n_bufs = 2
    assert n_windows % n_bufs == 0

    @functools.partial(
        pl.kernel, mesh=mesh, out_type=jax.ShapeDtypeStruct((m, d), table.dtype),
        scratch_types=[pltpu.VMEM((SC_WINDOW,), jnp.int32)] * n_bufs
        + [pltpu.VMEM((SC_WINDOW, d), table.dtype)] * n_bufs + [pltpu.SemaphoreType.DMA] * n_bufs,
        name="moe_gather")
    def gather(t_hbm, i_hbm, o_hbm, *scratch):
        idx_vmem, rows, sems = scratch[:n_bufs], scratch[n_bufs:2 * n_bufs], scratch[2 * n_bufs:]
        wid = lax.axis_index("subcore") * n_cores + lax.axis_index("core")

        def window(c):
            return pl.ds(wid * per_worker + c * SC_WINDOW, SC_WINDOW)

        def fetch(c, buf):
            pltpu.sync_copy(i_hbm.at[window(c)], idx_vmem[buf])
            return pltpu.async_copy(t_hbm.at[idx_vmem[buf]], rows[buf], sems[buf])

        fetch(0, 0)

        @pl.loop(0, n_windows, step=n_bufs)
        def _(c):
            for buf in range(n_bufs):
                nxt = c + buf + 1

                @pl.when(nxt < n_windows)
                def _(nxt=nxt, buf=buf):
                    fetch(nxt, (buf + 1) % n_bufs)

                pltpu.make_async_copy(t_hbm.at[idx_vmem[buf]], rows[buf], sems[buf]).wait()
                pltpu.sync_copy(rows[buf], o_hbm.at[window(c + buf)])

    return gather(table, idx)


def _expert_kernel(be_ref, nv_ref, xs_ref, wg_ref, wu_ref, wd_ref, ys_ref, xb_ref, acc_ref):
    del be_ref
    b = pl.program_id(0)
    j = pl.program_id(1)

    def ff_tile(first):
        x = xb_ref[...]
        acts = []
        for c in range(wg_ref.shape[1] // MXU_DIM):
            cols = slice(c * MXU_DIM, (c + 1) * MXU_DIM)
            gate = jnp.dot(x, wg_ref[:, cols].astype(BF16), preferred_element_type=F32)
            up = jnp.dot(x, wu_ref[:, cols].astype(BF16), preferred_element_type=F32)
            acts.append((_silu(gate) * up).astype(BF16))
        part = jnp.dot(jnp.concatenate(acts, axis=1), wd_ref[...].astype(BF16), preferred_element_type=F32)
        if first:
            acc_ref[...] = part
        else:
            acc_ref[...] += part

    valid = b < nv_ref[0]

    @pl.when(valid & (j == 0))
    def _():
        xb_ref[...] = _unpack_bf16_pairs(xs_ref[...])
        ff_tile(True)

    @pl.when(valid & (j > 0))
    def _():
        ff_tile(False)

    @pl.when(jnp.logical_not(valid) & (j == 0))
    def _():
        acc_ref[...] = jnp.zeros_like(acc_ref)

    @pl.when(j == pl.num_programs(1) - 1)
    def _():
        ys_ref[...] = _pack_bf16_pairs(acc_ref[...].astype(BF16).astype(F32))


def _experts(xs, block_e, n_valid, w_gate, w_up, w_down, layer, tm, tf):
    cap = xs.shape[0]
    d = 2 * xs.shape[1]
    ff = w_gate.shape[3]
    nj = ff // tf

    def blk(b, nv):
        return jnp.minimum(b, nv[0] - 1)

    def ffi(b, j, nv):
        return jnp.where(b < nv[0], j, nj - 1)

    return pl.pallas_call(
        _expert_kernel,
        grid_spec=pltpu.PrefetchScalarGridSpec(
            num_scalar_prefetch=2,
            grid=(cap // tm, nj),
            in_specs=[
                pl.BlockSpec((tm, d // 2), lambda b, j, be, nv: (blk(b, nv), 0)),
                pl.BlockSpec((None, None, d, tf), lambda b, j, be, nv: (layer, be[blk(b, nv)], 0, ffi(b, j, nv))),
                pl.BlockSpec((None, None, d, tf), lambda b, j, be, nv: (layer, be[blk(b, nv)], 0, ffi(b, j, nv))),
                pl.BlockSpec((None, None, tf, d), lambda b, j, be, nv: (layer, be[blk(b, nv)], ffi(b, j, nv), 0)),
            ],
            out_specs=pl.BlockSpec((tm, d // 2), lambda b, j, be, nv: (b, 0)),
            scratch_shapes=[pltpu.VMEM((tm, d), BF16), pltpu.VMEM((tm, d), F32)],
        ),
        out_shape=jax.ShapeDtypeStruct((cap, d // 2), jnp.int32),
        compiler_params=_params("arbitrary", "arbitrary"),
        name="moe_experts",
    )(block_e, n_valid, xs, w_gate, w_up, w_down)


def _combine_kernel(x_ref, rg_ref, gf_ref, y0_ref, y1_ref, o_ref, *, final_norm):
    rg = rg_ref[...]
    y0 = _unpack_bf16_pairs(y0_ref[...]).astype(F32)
    y1 = _unpack_bf16_pairs(y1_ref[...]).astype(F32)
    out = x_ref[...] + (rg[:, 0:1] * y0 + rg[:, 1:2] * y1)
    if final_norm:
        out = _rms(out, gf_ref[...])
    o_ref[...] = out


def _combine(x, route_g, yg, final_gain, final_norm):
    n, d = x.shape
    tm = 512
    return pl.pallas_call(
        functools.partial(_combine_kernel, final_norm=final_norm),
        grid=(n // tm,),
        in_specs=[
            pl.BlockSpec((tm, d), lambda i: (i, 0)),
            pl.BlockSpec((tm, ROUTE_COLS), lambda i: (i, 0)),
            pl.BlockSpec((1, d), lambda i: (0, 0)),
            pl.BlockSpec((None, tm, d // 2), lambda i: (0, i, 0)),
            pl.BlockSpec((None, tm, d // 2), lambda i: (1, i, 0)),
        ],
        out_specs=pl.BlockSpec((tm, d), lambda i: (i, 0)),
        out_shape=jax.ShapeDtypeStruct((n, d), F32),
        compiler_params=_params("parallel"),
        name="moe_combine",
    )(x, route_g, final_gain, yg, yg)


def _moe_layer(x, routing, w_gate, w_up, w_down, layer):
    n, d = x.shape
    tm = 1024 if n * TOP_K >= 8 * 1024 else 256
    tf = 2 * MXU_DIM
    route_i, route_g, counts, h_packed = routing

    counts = counts[0].astype(jnp.int32)
    padded = ((counts + tm - 1) // tm) * tm
    ends = jnp.cumsum(padded)
    starts = ends - padded
    dest = (starts[route_i[:, :TOP_K]] + route_i[:, TOP_K:2 * TOP_K]).T
    n_blocks = (n * TOP_K) // tm + N_EXPERTS
    block_start = jnp.arange(n_blocks, dtype=jnp.int32) * tm
    block_e = jnp.minimum(jnp.sum(ends[None, :] <= block_start[:, None], axis=1), N_EXPERTS - 1).astype(jnp.int32)
    n_valid = (ends[-1:] // tm).astype(jnp.int32)

    xs = _dispatch(h_packed, dest, n_blocks * tm)
    ys = _experts(xs, block_e, n_valid, w_gate, w_up, w_down, layer, tm, tf)
    return route_g, _gather_rows(ys, dest.reshape(TOP_K * n)).reshape(TOP_K, n, d // 2)


def _rope_tables(seq):
    pos = jnp.arange(seq, dtype=F32)
    inv_freq = ROPE_THETA ** (-jnp.arange(0, HEAD_DIM, 2, dtype=F32) / HEAD_DIM)
    ang = pos[:, None] * inv_freq[None, :]
    reps = LANES // (HEAD_DIM // 2)
    cos = jnp.tile(jnp.cos(ang), (1, reps))
    sin = jnp.tile(jnp.sin(ang), (1, reps))
    first_half = (jnp.arange(LANES) % HEAD_DIM) < HEAD_DIM // 2
    tabs = (cos, jnp.where(first_half, -sin, sin))
    q_scale = HEAD_DIM ** -0.5 * 1.4426950408889634
    return tuple(jnp.stack([t * q_scale, t]) for t in tabs)


def kernel(x, mix_norm, ffn_norm, attn_w_qkv, attn_w_o, pool_w, pool_scale, dense_w_gate, dense_w_up,
           dense_w_down, moe_router, moe_w_gate, moe_w_up, moe_w_down, final_norm):
    batch, seq, d = x.shape
    depth = mix_norm.shape[0]
    assert depth % 2 == 0 and d == N_HEADS * HEAD_DIM
    rope_tables = _rope_tables(seq)
    xf = x.reshape(batch * seq, d)
    final_gain = final_norm.reshape(1, d)
    pending_moe = None
    for i in range(depth):
        j = i // 2
        mix_gain = mix_norm[i].reshape(1, d)
        ffn_gain = ffn_norm[i].reshape(1, d)
        if i % 2 == 0:
            w_qkv = attn_w_qkv[j].astype(BF16)
            qkv, xf = _qkv_project(xf, mix_gain, w_qkv, rope_tables, batch, seq, pending_moe)
            pending_moe = None
            xf = _attention_and_dense_ffn(qkv, xf, attn_w_o[j].astype(BF16), ffn_gain, dense_w_gate[j].astype(BF16),
                                          dense_w_up[j].astype(BF16), dense_w_down[j].astype(BF16), batch, seq)
        else:
            xf, *routing = _pool_mixer_and_router(xf, mix_gain, pool_w[j].astype(BF16), pool_scale[j].reshape(1, d),
                                                  ffn_gain, moe_router[j], batch, seq)
            pending_moe = _moe_layer(xf, routing, moe_w_gate, moe_w_up, moe_w_down, j)
    return _combine(xf, *pending_moe, final_gain, final_norm=True).reshape(batch, seq, d)
```

```python
import functools

import jax
import jax.numpy as jnp
from jax import lax
from jax.experimental import pallas as pl
from jax.experimental.pallas import tpu as pltpu
from jax.experimental.pallas import tpu_sc as plsc

WINDOWS = (128, 512, 2048)
DILATIONS = (1, 4, 16)
N_GROUPS = len(WINDOWS)
N_HEADS = 16
HEAD_DIM = 64
ROPE_THETA = 10000.0
POOL_SIZES = (2, 4, 8, 16)
N_EXPERTS = 8
TOP_K = 2
RMS_EPS = 1e-6
MASK_VALUE = -1e30
QKV_ROWS = 512

LANES = 128
MXU_DIM = 256
VMEM_LIMIT_BYTES = 56 * 1024 * 1024

F32 = jnp.float32
BF16 = jnp.bfloat16


def _params(*sem):
    return pltpu.CompilerParams(dimension_semantics=sem, vmem_limit_bytes=VMEM_LIMIT_BYTES)


def _rms(x, g):
    ms = jnp.mean(x * x, axis=-1, keepdims=True)
    return x * lax.rsqrt(ms + RMS_EPS) * g


def _silu(g):
    return g / (1.0 + jnp.exp(-g))


def _qkv_kernel(x_ref, g_ref, w_ref, cos_ref, sin_ref, *rest, pending_moe):
    if pending_moe:
        rg_ref, y0_ref, y1_ref, o0_ref, o1_ref, o2_ref, xn_ref, acc_ref = rest
        rg = rg_ref[...]
        x = x_ref[...] + (rg[:, 0:1] * _unpack_bf16_pairs(y0_ref[...]).astype(F32)
                          + rg[:, 1:2] * _unpack_bf16_pairs(y1_ref[...]).astype(F32))
        xn_ref[...] = x
    else:
        o0_ref, o1_ref, o2_ref, acc_ref = rest
        x = x_ref[...]
    tm, d = x_ref.shape
    n_chunks = acc_ref.shape[1]
    lane = lax.broadcasted_iota(jnp.int32, (1, LANES), 1)
    first_half = (lane % HEAD_DIM) < HEAD_DIM // 2
    h = _rms(x, g_ref[...]).astype(BF16)
    for g, (o_ref, dil) in enumerate(zip((o0_ref, o1_ref, o2_ref), DILATIONS)):
        rows = tm // dil

        def strided(ref, r, dil=dil, rows=rows):
            if dil == 1:
                return ref[...]
            return ref[pl.ds(r, rows, stride=dil), :]

        for kind in range(3):
            col0 = (g * 3 + kind) * d
            acc = jnp.dot(h, w_ref[:, col0:col0 + d], preferred_element_type=F32)
            for c in range(n_chunks):
                acc_ref[kind, c] = acc[:, c * LANES:(c + 1) * LANES]
            for r in range(dil):
                if kind < 2:
                    cos = strided(cos_ref.at[kind], r)
                    sin = strided(sin_ref.at[kind], r)
                for c in range(n_chunks):
                    t = strided(acc_ref.at[kind, c], r)
                    if kind < 2:
                        partner = jnp.where(first_half, pltpu.roll(t, LANES - HEAD_DIM // 2, 1),
                                            pltpu.roll(t, HEAD_DIM // 2, 1))
                        t = t * cos + partner * sin
                    o_ref[kind, r, :, c * LANES:(c + 1) * LANES] = t.astype(BF16)


def _qkv_project(x, gain, w_qkv, tables, batch, seq, pending_moe=None):
    n, d = x.shape
    tm = QKV_ROWS // 2 if pending_moe else QKV_ROWS
    tiles_per_seq = seq // tm
    tab = pl.BlockSpec((2, tm, LANES), lambda i: (0, i % tiles_per_seq, 0))

    def o_spec(dil):
        return pl.BlockSpec((3, None, dil, tm // dil, d),
                            lambda i: (0, i // tiles_per_seq, 0, i % tiles_per_seq, 0))

    row = pl.BlockSpec((tm, d), lambda i: (i, 0))
    in_specs = [row, pl.BlockSpec((1, d), lambda i: (0, 0)),
                pl.BlockSpec(w_qkv.shape, lambda i: (0, 0), pipeline_mode=pl.Buffered(1)), tab, tab]
    args = [x, gain, w_qkv, *tables]
    out_specs = [o_spec(dil) for dil in DILATIONS]
    out_shape = [jax.ShapeDtypeStruct((3, batch, dil, seq // dil, d), BF16) for dil in DILATIONS]
    if pending_moe:
        route_g, yg = pending_moe
        in_specs += [pl.BlockSpec((tm, ROUTE_COLS), lambda i: (i, 0))]
        in_specs += [pl.BlockSpec((None, tm, d // 2), lambda i, k=k: (k, i, 0)) for k in range(TOP_K)]
        args += [route_g, yg, yg]
        out_specs.append(row)
        out_shape.append(jax.ShapeDtypeStruct((n, d), F32))
    outs = pl.pallas_call(
        functools.partial(_qkv_kernel, pending_moe=bool(pending_moe)),
        grid=(n // tm,),
        in_specs=in_specs,
        out_specs=out_specs,
        out_shape=out_shape,
        scratch_shapes=[pltpu.VMEM((3, d // LANES, tm, LANES), F32)],
        compiler_params=_params("parallel"),
        name="qkv_rope",
    )(*args)
    return [o.reshape(3, batch, seq, d) for o in outs[:N_GROUPS]], (outs[N_GROUPS] if pending_moe else x)


ATTN_TILES_PER_FFN_PIECE = 3


def _attn_dense_kernel(q0, k0, v0, q1, k1, v1, q2, k2, v2, band_ref, causal_ref,
                       x_ref, wo_ref, g_ref, wg_ref, wu_ref, wd_ref, y_ref,
                       o_scr, num_ref, max_ref, den_ref, kt_ref):
    b = pl.program_id(0)
    hp = pl.program_id(1)
    n_pairs, seq = o_scr.shape[1], o_scr.shape[2]
    w = WINDOWS[0] // DILATIONS[0]
    lane = lax.broadcasted_iota(jnp.int32, (1, LANES), 1)
    first_head = lane < HEAD_DIM
    o_out = o_scr.at[b % 2, hp]
    o_in = o_scr.at[(b + 1) % 2]

    @pl.when((b == 0) & (hp == 0))
    def _():
        o_scr[1] = jnp.zeros(o_scr.shape[1:], o_scr.dtype)

    def ffn_pieces():
        rows = pl.ds(pl.multiple_of(hp * x_ref.shape[0], x_ref.shape[0]), x_ref.shape[0])
        o = jnp.concatenate([o_in[p, rows, :] for p in range(n_pairs)], axis=1)
        x = x_ref[...] + jnp.dot(o, wo_ref[...], preferred_element_type=F32)
        h = _rms(x, g_ref[...]).astype(BF16)
        yield
        acts = []
        n_chunks = wg_ref.shape[1] // MXU_DIM
        for c in range(n_chunks):
            cols = slice(c * MXU_DIM, (c + 1) * MXU_DIM)
            gate = jnp.dot(h, wg_ref[:, cols], preferred_element_type=F32)
            up = jnp.dot(h, wu_ref[:, cols], preferred_element_type=F32)
            acts.append((_silu(gate) * up).astype(BF16))
            yield
        half = n_chunks // 2
        acc = jnp.dot(jnp.concatenate(acts[:half], axis=1), wd_ref[:half * MXU_DIM, :], preferred_element_type=F32)
        yield
        acc = acc + jnp.dot(jnp.concatenate(acts[half:], axis=1), wd_ref[half * MXU_DIM:, :],
                            preferred_element_type=F32)
        y_ref[...] = x + acc
        yield

    def tile(q_ref, v_ref, qoff, has_prev):
        q = q_ref[qoff:qoff + w, :]
        zero = jnp.zeros_like(q)
        qq = jnp.concatenate([jnp.where(first_head, q, zero), jnp.where(first_head, zero, q)], axis=0)
        k_lo = qoff - w if has_prev else qoff
        s = jnp.dot(qq, kt_ref[:, k_lo:qoff + w], preferred_element_type=F32)
        s = s + (band_ref[...] if has_prev else causal_ref[...])
        mx = jnp.max(s, axis=1, keepdims=True)
        p = jnp.exp2(s - mx)
        den = jnp.sum(p, axis=1, keepdims=True)
        pv = jnp.dot(p.astype(BF16), v_ref[k_lo:qoff + w, :], preferred_element_type=F32)
        return tuple(jnp.where(first_head, a[:w], a[w:]) for a in (pv, mx, den))

    def attention_and_ffn():
        pieces = ffn_pieces()
        next(pieces)
        tiles_done = 0
        order = sorted(range(N_GROUPS), key=lambda g: -DILATIONS[g])
        for g in order:
            q_ref, k_ref, v_ref = ((q0, k0, v0), (q1, k1, v1), (q2, k2, v2))[g]
            dil = DILATIONS[g]
            sub_len = seq // dil
            nb = sub_len // w
            kt_ref[...] = k_ref[...].T
            for t in range(dil * nb):
                r, n = divmod(t, nb)
                qoff = r * sub_len + n * w
                pv, mx, den = tile(q_ref, v_ref, qoff, n > 0)
                tiles_done += 1
                if tiles_done % ATTN_TILES_PER_FFN_PIECE == 0:
                    next(pieces, None)
                rows = pl.ds(qoff, w) if dil == 1 else pl.ds(r + dil * n * w, w, stride=dil)
                if g == order[0]:
                    num_ref[rows, :] = pv
                    max_ref[rows, :] = mx
                    den_ref[rows, :] = den
                    continue
                old_max = max_ref[rows, :]
                new_max = jnp.maximum(old_max, mx)
                a = jnp.exp2(old_max - new_max)
                c = jnp.exp2(mx - new_max)
                num = a * num_ref[rows, :] + c * pv
                den = a * den_ref[rows, :] + c * den
                if g == order[-1]:
                    o_out[rows, :] = (num / den).astype(BF16)
                else:
                    num_ref[rows, :] = num
                    den_ref[rows, :] = den
                    max_ref[rows, :] = new_max
        for _ in pieces:
            pass

    n_batches = pl.num_programs(0) - 1
    pl.when(b < n_batches)(attention_and_ffn)

    @pl.when(b == n_batches)
    def _():
        for _ in ffn_pieces():
            pass


def _attention_biases(w):
    row = jnp.arange(2 * w)[:, None] % w
    col = jnp.arange(2 * w)[None, :]
    dist = row + w - col
    band = jnp.where((dist >= 0) & (dist <= w), 0.0, MASK_VALUE).astype(F32)
    causal = jnp.where(row >= col[:, :w], 0.0, MASK_VALUE).astype(F32)
    return band, causal


def _attention_and_dense_ffn(qkv_groups, x, w_o, gain, w_gate, w_up, w_down, batch, seq):
    n, d = x.shape
    ff = w_gate.shape[1]
    n_pairs = d // LANES
    rows = seq // n_pairs
    last = batch - 1
    in_specs, args = [], []
    for arr in qkv_groups:
        for kind in range(3):
            in_specs.append(pl.BlockSpec(
                (None, None, seq, LANES),
                lambda b, hp, kind=kind: (kind, jnp.minimum(b, last), 0, jnp.where(b > last, n_pairs - 1, hp))))
            args.append(arr)
    const = lambda shape: pl.BlockSpec(shape, lambda b, hp: (0, 0), pipeline_mode=pl.Buffered(1))
    ffn_in = pl.BlockSpec((rows, d), lambda b, hp: (jnp.maximum(b - 1, 0) * n_pairs + hp, 0))
    ffn_out = pl.BlockSpec((rows, d), lambda b, hp: (jnp.where(b == 0, batch, b - 1) * n_pairs + hp, 0))
    biases = _attention_biases(WINDOWS[0] // DILATIONS[0])
    in_specs += [const(t.shape) for t in biases]
    in_specs += [ffn_in, const((d, d)), const((1, d)), const((d, ff)), const((d, ff)), const((ff, d))]
    args += [*biases, x, w_o, gain, w_gate, w_up, w_down]
    return pl.pallas_call(
        _attn_dense_kernel,
        grid=(batch + 1, n_pairs),
        in_specs=in_specs,
        out_specs=ffn_out,
        out_shape=jax.ShapeDtypeStruct((n + seq, d), F32),
        scratch_shapes=[pltpu.VMEM((2, n_pairs, seq, LANES), BF16)]
        + [pltpu.VMEM((seq, LANES), F32)] * 3 + [pltpu.VMEM((LANES, seq), BF16)],
        compiler_params=_params("arbitrary", "arbitrary"),
        name="attention_ffn",
    )(*args)


POOL_HALO = max(POOL_SIZES)


def _pool_tile(i, x_ref, halo_ref, g_ref, w_ref, sc_ref, y_ref):
    ts = x_ref.shape[0]
    g = g_ref[...]
    x = x_ref[...]
    h = _rms(x, g)
    hh = jnp.where(i > 0, _rms(halo_ref[...], g), 0.0)
    ext = jnp.concatenate([hh, h], axis=0)
    pos = i * ts + lax.broadcasted_iota(jnp.int32, (ts, 1), 0)
    pc = w_ref.shape[1]
    for grp, size in enumerate(POOL_SIZES):
        sl = slice(grp * pc, (grp + 1) * pc)
        s = ext[:, sl]
        step = 1
        while step < size:
            s = s + pltpu.roll(s, step, 0)
            step *= 2
        cnt = jnp.minimum(pos + 1, size).astype(F32)
        y = s[POOL_HALO:, :] / cnt - h[:, sl]
        z = jnp.dot(y.astype(BF16), w_ref[grp], preferred_element_type=F32)
        y_ref[:, sl] = x[:, sl] + z * sc_ref[:, sl]


def _pool_router_kernel(x_ref, halo_ref, g_ref, w_ref, sc_ref, g2_ref, wr_ref, tril_ref,
                        y_ref, ri_ref, rg_ref, cnt_ref, hp_ref, run_ref):
    b = pl.program_id(0)
    i = pl.program_id(1)
    _pool_tile(i, x_ref, halo_ref, g_ref, w_ref, sc_ref, y_ref)
    _route_tile((b == 0) & (i == 0), y_ref[...], g2_ref, wr_ref, tril_ref, ri_ref, rg_ref, cnt_ref, hp_ref, run_ref)


def _pool_mixer_and_router(x, gain, pool_w, scale, ffn_gain, w_router, batch, seq):
    d = x.shape[1]
    n = batch * seq
    ne = w_router.shape[1]
    ts = 512
    tiles = seq // ts
    x3 = x.reshape(-1, seq, d)
    hb = ts // POOL_HALO
    tril = (jnp.arange(ts)[:, None] > jnp.arange(ts)[None, :]).astype(BF16)
    const = lambda shape: pl.BlockSpec(shape, lambda b, i: (0,) * len(shape))
    tok = lambda cols: pl.BlockSpec((ts, cols), lambda b, i: (b * tiles + i, 0))
    out, route_i, route_g, counts, h_packed = pl.pallas_call(
        _pool_router_kernel,
        grid=(batch, tiles),
        in_specs=[
            pl.BlockSpec((None, ts, d), lambda b, i: (b, i, 0)),
            pl.BlockSpec((None, POOL_HALO, d), lambda b, i: (b, jnp.maximum(i * hb - 1, 0), 0)),
            const((1, d)), const(pool_w.shape), const((1, d)), const((1, d)), const((d, ne)), const((ts, ts)),
        ],
        out_specs=[
            pl.BlockSpec((None, ts, d), lambda b, i: (b, i, 0)),
            tok(ROUTE_COLS), tok(ROUTE_COLS), const((1, ne)), tok(d // 2),
        ],
        out_shape=[
            jax.ShapeDtypeStruct((batch, seq, d), F32),
            jax.ShapeDtypeStruct((n, ROUTE_COLS), jnp.int32),
            jax.ShapeDtypeStruct((n, ROUTE_COLS), F32),
            jax.ShapeDtypeStruct((1, ne), F32),
            jax.ShapeDtypeStruct((n, d // 2), jnp.int32),
        ],
        scratch_shapes=[pltpu.VMEM((1, ne), F32)],
        compiler_params=_params("arbitrary", "arbitrary"),
        name="pool_router",
    )(x3, x3, gain, pool_w, scale, ffn_gain, w_router, tril)
    return out.reshape(n, d), route_i, route_g, counts, h_packed


ROUTE_COLS = 8


def _pack_bf16_pairs(hr):
    c = hr.shape[1] // 2
    bits = pltpu.bitcast(hr, jnp.int32)
    return (bits[:, c:] & jnp.int32(-65536)) | lax.shift_right_logical(bits[:, :c], 16)


def _unpack_bf16_pairs(packed):
    lo = pltpu.bitcast(lax.shift_left(packed, 16), F32)
    hi = pltpu.bitcast(packed & jnp.int32(-65536), F32)
    return jnp.concatenate([lo.astype(BF16), hi.astype(BF16)], axis=1)


def _route_tile(first, x, g_ref, wr_ref, tril_ref, ri_ref, rg_ref, cnt_ref, hp_ref, run_ref):
    @pl.when(first)
    def _():
        run_ref[...] = jnp.zeros_like(run_ref)

    h = _rms(x, g_ref[...])
    wr = wr_ref[...]
    ne = wr.shape[1]
    h_hi = h.astype(BF16)
    hp_ref[...] = _pack_bf16_pairs(h_hi.astype(F32))
    h_lo = (h - h_hi.astype(F32)).astype(BF16)
    w_hi = wr.astype(BF16)
    w_lo = (wr - w_hi.astype(F32)).astype(BF16)
    hi_part = jnp.dot(h_hi, jnp.concatenate([w_hi, w_lo], axis=1), preferred_element_type=F32)
    logits = hi_part[:, :ne] + hi_part[:, ne:] + jnp.dot(h_lo, w_hi, preferred_element_type=F32)
    tm = logits.shape[0]
    lane = lax.broadcasted_iota(jnp.int32, (tm, ne), 1)
    v1 = jnp.max(logits, axis=1, keepdims=True)
    i1 = jnp.min(jnp.where(logits == v1, lane, ne), axis=1, keepdims=True)
    rest = jnp.where(lane == i1, -jnp.inf, logits)
    v2 = jnp.max(rest, axis=1, keepdims=True)
    i2 = jnp.min(jnp.where(rest == v2, lane, ne), axis=1, keepdims=True)
    e = jnp.exp(v2 - v1)
    g1 = 1.0 / (1.0 + e)
    g2 = e / (1.0 + e)
    oh1 = (lane == i1).astype(F32)
    oh2 = (lane == i2).astype(F32)
    both = oh1 + oh2
    before = jnp.dot(tril_ref[...], both.astype(BF16), preferred_element_type=F32) + run_ref[...]
    r1 = jnp.sum(before * oh1, axis=1, keepdims=True).astype(jnp.int32)
    r2 = jnp.sum(before * oh2, axis=1, keepdims=True).astype(jnp.int32)
    run_ref[...] += jnp.sum(both, axis=0, keepdims=True)
    ri_ref[...] = jnp.where(lane == 0, i1, jnp.where(lane == 1, i2, jnp.where(lane == 2, r1, r2)))
    rg_ref[...] = jnp.where(lane == 0, g1, g2)
    cnt_ref[...] = run_ref[...]


SC_WINDOW = 64


def _sc_workers():
    info = plsc.get_sparse_core_info()
    mesh = plsc.VectorSubcoreMesh(core_axis_name="core", subcore_axis_name="subcore")
    return mesh, info.num_cores, info.num_cores * info.num_subcores


def _dispatch(x, dest, cap):
    n, d = x.shape
    mesh, n_cores, n_workers = _sc_workers()
    per_worker = n // n_workers

    n_windows = per_worker // SC_WINDOW
    n_bufs = 2
    assert n_windows % n_bufs == 0

    @functools.partial(
        pl.kernel, mesh=mesh, out_type=jax.ShapeDtypeStruct((cap, d), x.dtype),
        scratch_types=[pltpu.VMEM((SC_WINDOW,), jnp.int32)] * (TOP_K * n_bufs)
        + [pltpu.VMEM((SC_WINDOW, d), x.dtype)] * n_bufs + [pltpu.SemaphoreType.DMA] * n_bufs,
        name="moe_dispatch")
    def scatter(x_hbm, *rest):
        idx_hbm, (xs_hbm, *scratch) = rest[:TOP_K], rest[TOP_K:]
        idx_vmem = [scratch[b * TOP_K:(b + 1) * TOP_K] for b in range(n_bufs)]
        rows = scratch[TOP_K * n_bufs:TOP_K * n_bufs + n_bufs]
        sems = scratch[TOP_K * n_bufs + n_bufs:]
        wid = lax.axis_index("subcore") * n_cores + lax.axis_index("core")

        def drain(buf):
            for k in range(TOP_K):
                pltpu.make_async_copy(rows[buf], xs_hbm.at[idx_vmem[buf][k]], sems[buf]).wait()

        @pl.loop(0, n_windows, step=n_bufs)
        def _(c):
            for buf in range(n_bufs):
                @pl.when(c > 0)
                def _(buf=buf):
                    drain(buf)

                window = pl.ds(wid * per_worker + (c + buf) * SC_WINDOW, SC_WINDOW)
                for k in range(TOP_K):
                    pltpu.sync_copy(idx_hbm[k].at[window], idx_vmem[buf][k])
                pltpu.sync_copy(x_hbm.at[window], rows[buf])
                for k in range(TOP_K):
                    pltpu.async_copy(rows[buf], xs_hbm.at[idx_vmem[buf][k]], sems[buf])

        for buf in range(n_bufs):
            drain(buf)

    return scatter(x, *[dest[k] for k in range(TOP_K)])


def _gather_rows(table, idx):
    m = idx.shape[0]
    d = table.shape[1]
    mesh, n_cores, n_workers = _sc_workers()
    per_worker = m // n_workers

    n_windows = per_worker // SC_WINDOW
    n_bufs = 2
    assert n_windows % n_bufs == 0

    @functools.partial(
        pl.kernel, mesh=mesh, out_type=jax.ShapeDtypeStruct((m, d), table.dtype),
        scratch_types=[pltpu.VMEM((SC_WINDOW,), jnp.int32)] * n_bufs
        + [pltpu.VMEM((SC_WINDOW, d), table.dtype)] * n_bufs + [pltpu.SemaphoreType.DMA] * n_bufs,
        name="moe_gather")
    def gather(t_hbm, i_hbm, o_hbm, *scratch):
        idx_vmem, rows, sems = scratch[:n_bufs], scratch[n_bufs:2 * n_bufs], scratch[2 * n_bufs:]
        wid = lax.axis_index("subcore") * n_cores + lax.axis_index("core")

        def window(c):
            return pl.ds(wid * per_worker + c * SC_WINDOW, SC_WINDOW)

        def fetch(c, buf):
            pltpu.sync_copy(i_hbm.at[window(c)], idx_vmem[buf])
            return pltpu.async_copy(t_hbm.at[idx_vmem[buf]], rows[buf], sems[buf])

        fetch(0, 0)

        @pl.loop(0, n_windows, step=n_bufs)
        def _(c):
            for buf in range(n_bufs):
                nxt = c + buf + 1

                @pl.when(nxt < n_windows)
                def _(nxt=nxt, buf=buf):
                    fetch(nxt, (buf + 1) % n_bufs)

                pltpu.make_async_copy(t_hbm.at[idx_vmem[buf]], rows[buf], sems[buf]).wait()
                pltpu.sync_copy(rows[buf], o_hbm.at[window(c + buf)])

    return gather(table, idx)


def _expert_kernel(be_ref, nv_ref, xs_ref, wg_ref, wu_ref, wd_ref, ys_ref, xb_ref, acc_ref):
    del be_ref
    b = pl.program_id(0)
    j = pl.program_id(1)

    def ff_tile(first):
        x = xb_ref[...]
        acts = []
        for c in range(wg_ref.shape[1] // MXU_DIM):
            cols = slice(c * MXU_DIM, (c + 1) * MXU_DIM)
            gate = jnp.dot(x, wg_ref[:, cols].astype(BF16), preferred_element_type=F32)
            up = jnp.dot(x, wu_ref[:, cols].astype(BF16), preferred_element_type=F32)
            acts.append((_silu(gate) * up).astype(BF16))
        part = jnp.dot(jnp.concatenate(acts, axis=1), wd_ref[...].astype(BF16), preferred_element_type=F32)
        if first:
            acc_ref[...] = part
        else:
            acc_ref[...] += part

    valid = b < nv_ref[0]

    @pl.when(valid & (j == 0))
    def _():
        xb_ref[...] = _unpack_bf16_pairs(xs_ref[...])
        ff_tile(True)

    @pl.when(valid & (j > 0))
    def _():
        ff_tile(False)

    @pl.when(jnp.logical_not(valid) & (j == 0))
    def _():
        acc_ref[...] = jnp.zeros_like(acc_ref)

    @pl.when(j == pl.num_programs(1) - 1)
    def _():
        ys_ref[...] = _pack_bf16_pairs(acc_ref[...].astype(BF16).astype(F32))


def _experts(xs, block_e, n_valid, w_gate, w_up, w_down, layer, tm, tf):
    cap = xs.shape[0]
    d = 2 * xs.shape[1]
    ff = w_gate.shape[3]
    nj = ff // tf

    def blk(b, nv):
        return jnp.minimum(b, nv[0] - 1)

    def ffi(b, j, nv):
        return jnp.where(b < nv[0], j, nj - 1)

    return pl.pallas_call(
        _expert_kernel,
        grid_spec=pltpu.PrefetchScalarGridSpec(
            num_scalar_prefetch=2,
            grid=(cap // tm, nj),
            in_specs=[
                pl.BlockSpec((tm, d // 2), lambda b, j, be, nv: (blk(b, nv), 0)),
                pl.BlockSpec((None, None, d, tf), lambda b, j, be, nv: (layer, be[blk(b, nv)], 0, ffi(b, j, nv))),
                pl.BlockSpec((None, None, d, tf), lambda b, j, be, nv: (layer, be[blk(b, nv)], 0, ffi(b, j, nv))),
                pl.BlockSpec((None, None, tf, d), lambda b, j, be, nv: (layer, be[blk(b, nv)], ffi(b, j, nv), 0)),
            ],
            out_specs=pl.BlockSpec((tm, d // 2), lambda b, j, be, nv: (b, 0)),
            scratch_shapes=[pltpu.VMEM((tm, d), BF16), pltpu.VMEM((tm, d), F32)],
        ),
        out_shape=jax.ShapeDtypeStruct((cap, d // 2), jnp.int32),
        compiler_params=_params("arbitrary", "arbitrary"),
        name="moe_experts",
    )(block_e, n_valid, xs, w_gate, w_up, w_down)


def _combine_kernel(x_ref, rg_ref, gf_ref, y0_ref, y1_ref, o_ref):
    rg = rg_ref[...]
    y0 = _unpack_bf16_pairs(y0_ref[...]).astype(F32)
    y1 = _unpack_bf16_pairs(y1_ref[...]).astype(F32)
    o_ref[...] = _rms(x_ref[...] + (rg[:, 0:1] * y0 + rg[:, 1:2] * y1), gf_ref[...])


def _combine_and_final_norm(x, route_g, yg, final_gain):
    n, d = x.shape
    tm = 512
    return pl.pallas_call(
        _combine_kernel,
        grid=(n // tm,),
        in_specs=[
            pl.BlockSpec((tm, d), lambda i: (i, 0)),
            pl.BlockSpec((tm, ROUTE_COLS), lambda i: (i, 0)),
            pl.BlockSpec((1, d), lambda i: (0, 0)),
            pl.BlockSpec((None, tm, d // 2), lambda i: (0, i, 0)),
            pl.BlockSpec((None, tm, d // 2), lambda i: (1, i, 0)),
        ],
        out_specs=pl.BlockSpec((tm, d), lambda i: (i, 0)),
        out_shape=jax.ShapeDtypeStruct((n, d), F32),
        compiler_params=_params("parallel"),
        name="moe_combine",
    )(x, route_g, final_gain, yg, yg)


def _moe_layer(x, routing, w_gate, w_up, w_down, layer):
    n, d = x.shape
    tm = 1024 if n * TOP_K >= 8 * 1024 else 256
    tf = 2 * MXU_DIM
    route_i, route_g, counts, h_packed = routing

    counts = counts[0].astype(jnp.int32)
    padded = ((counts + tm - 1) // tm) * tm
    ends = jnp.cumsum(padded)
    starts = ends - padded
    dest = (starts[route_i[:, :TOP_K]] + route_i[:, TOP_K:2 * TOP_K]).T
    n_blocks = (n * TOP_K) // tm + N_EXPERTS
    block_start = jnp.arange(n_blocks, dtype=jnp.int32) * tm
    block_e = jnp.minimum(jnp.sum(ends[None, :] <= block_start[:, None], axis=1), N_EXPERTS - 1).astype(jnp.int32)
    n_valid = (ends[-1:] // tm).astype(jnp.int32)

    xs = _dispatch(h_packed, dest, n_blocks * tm)
    ys = _experts(xs, block_e, n_valid, w_gate, w_up, w_down, layer, tm, tf)
    return route_g, _gather_rows(ys, dest.reshape(TOP_K * n)).reshape(TOP_K, n, d // 2)


def _rope_tables(seq):
    pos = jnp.arange(seq, dtype=F32)
    inv_freq = ROPE_THETA ** (-jnp.arange(0, HEAD_DIM, 2, dtype=F32) / HEAD_DIM)
    ang = pos[:, None] * inv_freq[None, :]
    reps = LANES // (HEAD_DIM // 2)
    cos = jnp.tile(jnp.cos(ang), (1, reps))
    sin = jnp.tile(jnp.sin(ang), (1, reps))
    first_half = (jnp.arange(LANES) % HEAD_DIM) < HEAD_DIM // 2
    tabs = (cos, jnp.where(first_half, -sin, sin))
    q_scale = HEAD_DIM ** -0.5 * 1.4426950408889634
    return tuple(jnp.stack([t * q_scale, t]) for t in tabs)


def kernel(x, mix_norm, ffn_norm, attn_w_qkv, attn_w_o, pool_w, pool_scale, dense_w_gate, dense_w_up,
           dense_w_down, moe_router, moe_w_gate, moe_w_up, moe_w_down, final_norm):
    batch, seq, d = x.shape
    depth = mix_norm.shape[0]
    assert depth % 2 == 0 and d == N_HEADS * HEAD_DIM
    rope_tables = _rope_tables(seq)
    xf = x.reshape(batch * seq, d)
    final_gain = final_norm.reshape(1, d)
    pending_moe = None
    for i in range(depth):
        j = i // 2
        mix_gain = mix_norm[i].reshape(1, d)
        ffn_gain = ffn_norm[i].reshape(1, d)
        if i % 2 == 0:
            w_qkv = attn_w_qkv[j].astype(BF16)
            qkv, xf = _qkv_project(xf, mix_gain, w_qkv, rope_tables, batch, seq, pending_moe)
            pending_moe = None
            xf = _attention_and_dense_ffn(qkv, xf, attn_w_o[j].astype(BF16), ffn_gain, dense_w_gate[j].astype(BF16),
                                          dense_w_up[j].astype(BF16), dense_w_down[j].astype(BF16), batch, seq)
        else:
            xf, *routing = _pool_mixer_and_router(xf, mix_gain, pool_w[j].astype(BF16), pool_scale[j].reshape(1, d),
                                                  ffn_gain, moe_router[j], batch, seq)
            pending_moe = _moe_layer(xf, routing, moe_w_gate, moe_w_up, moe_w_down, j)
    return _combine_and_final_norm(xf, *pending_moe, final_gain).reshape(batch, seq, d)
```

```python
import functools

import jax
import jax.numpy as jnp
from jax import lax
from jax.experimental import pallas as pl
from jax.experimental.pallas import tpu as pltpu
from jax.experimental.pallas import tpu_sc as plsc

WINDOWS = (128, 512, 2048)
DILATIONS = (1, 4, 16)
N_GROUPS = len(WINDOWS)
N_HEADS = 16
HEAD_DIM = 64
ROPE_THETA = 10000.0
POOL_SIZES = (2, 4, 8, 16)
N_EXPERTS = 8
TOP_K = 2
RMS_EPS = 1e-6
MASK_VALUE = -1e30
QKV_ROWS = 512

LANES = 128
MXU_DIM = 256
VMEM_LIMIT_BYTES = 56 * 1024 * 1024

F32 = jnp.float32
BF16 = jnp.bfloat16


def _params(*sem):
    return pltpu.CompilerParams(dimension_semantics=sem, vmem_limit_bytes=VMEM_LIMIT_BYTES)


def _rms(x, g):
    ms = jnp.mean(x * x, axis=-1, keepdims=True)
    return x * lax.rsqrt(ms + RMS_EPS) * g


def _silu(g):
    return g / (1.0 + jnp.exp(-g))


def _qkv_kernel(x_ref, g_ref, w_ref, cos_ref, sin_ref, *rest, pending_moe):
    if pending_moe:
        rg_ref, y0_ref, y1_ref, o0_ref, o1_ref, o2_ref, xn_ref, acc_ref = rest
        rg = rg_ref[...]
        x = x_ref[...] + (rg[:, 0:1] * _unpack_bf16_pairs(y0_ref[...]).astype(F32)
                          + rg[:, 1:2] * _unpack_bf16_pairs(y1_ref[...]).astype(F32))
        xn_ref[...] = x
    else:
        o0_ref, o1_ref, o2_ref, acc_ref = rest
        x = x_ref[...]
    tm, d = x_ref.shape
    n_chunks = acc_ref.shape[1]
    lane = lax.broadcasted_iota(jnp.int32, (1, LANES), 1)
    first_half = (lane % HEAD_DIM) < HEAD_DIM // 2
    h = _rms(x, g_ref[...]).astype(BF16)
    for g, (o_ref, dil) in enumerate(zip((o0_ref, o1_ref, o2_ref), DILATIONS)):
        rows = tm // dil

        def strided(ref, r, dil=dil, rows=rows):
            if dil == 1:
                return ref[...]
            return ref[pl.ds(r, rows, stride=dil), :]

        for kind in range(3):
            col0 = (g * 3 + kind) * d
            acc = jnp.dot(h, w_ref[:, col0:col0 + d], preferred_element_type=F32)
            for c in range(n_chunks):
                acc_ref[kind, c] = acc[:, c * LANES:(c + 1) * LANES]
            for r in range(dil):
                if kind < 2:
                    cos = strided(cos_ref.at[kind], r)
                    sin = strided(sin_ref.at[kind], r)
                for c in range(n_chunks):
                    t = strided(acc_ref.at[kind, c], r)
                    if kind < 2:
                        partner = jnp.where(first_half, pltpu.roll(t, LANES - HEAD_DIM // 2, 1),
                                            pltpu.roll(t, HEAD_DIM // 2, 1))
                        t = t * cos + partner * sin
                    o_ref[kind, r, :, c * LANES:(c + 1) * LANES] = t.astype(BF16)


def _qkv_project(x, gain, w_qkv, tables, batch, seq, pending_moe=None):
    n, d = x.shape
    tm = QKV_ROWS // 2 if pending_moe else QKV_ROWS
    tiles_per_seq = seq // tm
    tab = pl.BlockSpec((2, tm, LANES), lambda i: (0, i % tiles_per_seq, 0))

    def o_spec(dil):
        return pl.BlockSpec((3, None, dil, tm // dil, d),
                            lambda i: (0, i // tiles_per_seq, 0, i % tiles_per_seq, 0))

    row = pl.BlockSpec((tm, d), lambda i: (i, 0))
    in_specs = [row, pl.BlockSpec((1, d), lambda i: (0, 0)),
                pl.BlockSpec(w_qkv.shape, lambda i: (0, 0), pipeline_mode=pl.Buffered(1)), tab, tab]
    args = [x, gain, w_qkv, *tables]
    out_specs = [o_spec(dil) for dil in DILATIONS]
    out_shape = [jax.ShapeDtypeStruct((3, batch, dil, seq // dil, d), BF16) for dil in DILATIONS]
    if pending_moe:
        route_g, yg = pending_moe
        in_specs += [pl.BlockSpec((tm, ROUTE_COLS), lambda i: (i, 0))]
        in_specs += [pl.BlockSpec((None, tm, d // 2), lambda i, k=k: (k, i, 0)) for k in range(TOP_K)]
        args += [route_g, yg, yg]
        out_specs.append(row)
        out_shape.append(jax.ShapeDtypeStruct((n, d), F32))
    outs = pl.pallas_call(
        functools.partial(_qkv_kernel, pending_moe=bool(pending_moe)),
        grid=(n // tm,),
        in_specs=in_specs,
        out_specs=out_specs,
        out_shape=out_shape,
        scratch_shapes=[pltpu.VMEM((3, d // LANES, tm, LANES), F32)],
        compiler_params=_params("parallel"),
        name="qkv_rope",
    )(*args)
    return [o.reshape(3, batch, seq, d) for o in outs[:N_GROUPS]], (outs[N_GROUPS] if pending_moe else x)


ATTN_TILES_PER_FFN_PIECE = 3


def _attn_dense_kernel(q0, k0, v0, q1, k1, v1, q2, k2, v2, band_ref, causal_ref,
                       x_ref, wo_ref, g_ref, wg_ref, wu_ref, wd_ref, y_ref,
                       o_scr, num_ref, max_ref, den_ref, kt_ref):
    b = pl.program_id(0)
    hp = pl.program_id(1)
    n_pairs, seq = o_scr.shape[1], o_scr.shape[2]
    w = WINDOWS[0] // DILATIONS[0]
    lane = lax.broadcasted_iota(jnp.int32, (1, LANES), 1)
    first_head = lane < HEAD_DIM
    o_out = o_scr.at[b % 2, hp]
    o_in = o_scr.at[(b + 1) % 2]

    @pl.when((b == 0) & (hp == 0))
    def _():
        o_scr[1] = jnp.zeros(o_scr.shape[1:], o_scr.dtype)

    def ffn_pieces():
        rows = pl.ds(pl.multiple_of(hp * x_ref.shape[0], x_ref.shape[0]), x_ref.shape[0])
        o = jnp.concatenate([o_in[p, rows, :] for p in range(n_pairs)], axis=1)
        x = x_ref[...] + jnp.dot(o, wo_ref[...], preferred_element_type=F32)
        h = _rms(x, g_ref[...]).astype(BF16)
        yield
        acts = []
        n_chunks = wg_ref.shape[1] // MXU_DIM
        for c in range(n_chunks):
            cols = slice(c * MXU_DIM, (c + 1) * MXU_DIM)
            gate = jnp.dot(h, wg_ref[:, cols], preferred_element_type=F32)
            up = jnp.dot(h, wu_ref[:, cols], preferred_element_type=F32)
            acts.append((_silu(gate) * up).astype(BF16))
            yield
        half = n_chunks // 2
        acc = jnp.dot(jnp.concatenate(acts[:half], axis=1), wd_ref[:half * MXU_DIM, :], preferred_element_type=F32)
        yield
        acc = acc + jnp.dot(jnp.concatenate(acts[half:], axis=1), wd_ref[half * MXU_DIM:, :],
                            preferred_element_type=F32)
        y_ref[...] = x + acc
        yield

    def tile(q_ref, v_ref, qoff, has_prev):
        q = q_ref[qoff:qoff + w, :]
        zero = jnp.zeros_like(q)
        qq = jnp.concatenate([jnp.where(first_head, q, zero), jnp.where(first_head, zero, q)], axis=0)
        k_lo = qoff - w if has_prev else qoff
        s = jnp.dot(qq, kt_ref[:, k_lo:qoff + w], preferred_element_type=F32)
        s = s + (band_ref[...] if has_prev else causal_ref[...])
        mx = jnp.max(s, axis=1, keepdims=True)
        p = jnp.exp2(s - mx)
        den = jnp.sum(p, axis=1, keepdims=True)
        pv = jnp.dot(p.astype(BF16), v_ref[k_lo:qoff + w, :], preferred_element_type=F32)
        return tuple(jnp.where(first_head, a[:w], a[w:]) for a in (pv, mx, den))

    def attention_and_ffn():
        pieces = ffn_pieces()
        next(pieces)
        tiles_done = 0
        order = sorted(range(N_GROUPS), key=lambda g: -DILATIONS[g])
        for g in order:
            q_ref, k_ref, v_ref = ((q0, k0, v0), (q1, k1, v1), (q2, k2, v2))[g]
            dil = DILATIONS[g]
            sub_len = seq // dil
            nb = sub_len // w
            kt_ref[...] = k_ref[...].T
            for t in range(dil * nb):
                r, n = divmod(t, nb)
                qoff = r * sub_len + n * w
                pv, mx, den = tile(q_ref, v_ref, qoff, n > 0)
                tiles_done += 1
                if tiles_done % ATTN_TILES_PER_FFN_PIECE == 0:
                    next(pieces, None)
                rows = pl.ds(qoff, w) if dil == 1 else pl.ds(r + dil * n * w, w, stride=dil)
                if g == order[0]:
                    num_ref[rows, :] = pv
                    max_ref[rows, :] = mx
                    den_ref[rows, :] = den
                    continue
                old_max = max_ref[rows, :]
                new_max = jnp.maximum(old_max, mx)
                a = jnp.exp2(old_max - new_max)
                c = jnp.exp2(mx - new_max)
                num = a * num_ref[rows, :] + c * pv
                den = a * den_ref[rows, :] + c * den
                if g == order[-1]:
                    o_out[rows, :] = (num / den).astype(BF16)
                else:
                    num_ref[rows, :] = num
                    den_ref[rows, :] = den
                    max_ref[rows, :] = new_max
        for _ in pieces:
            pass

    n_batches = pl.num_programs(0) - 1
    pl.when(b < n_batches)(attention_and_ffn)

    @pl.when(b == n_batches)
    def _():
        for _ in ffn_pieces():
            pass


def _attention_biases(w):
    row = jnp.arange(2 * w)[:, None] % w
    col = jnp.arange(2 * w)[None, :]
    dist = row + w - col
    band = jnp.where((dist >= 0) & (dist <= w), 0.0, MASK_VALUE).astype(F32)
    causal = jnp.where(row >= col[:, :w], 0.0, MASK_VALUE).astype(F32)
    return band, causal


def _attention_and_dense_ffn(qkv_groups, x, w_o, gain, w_gate, w_up, w_down, batch, seq):
    n, d = x.shape
    ff = w_gate.shape[1]
    n_pairs = d // LANES
    rows = seq // n_pairs
    last = batch - 1
    in_specs, args = [], []
    for arr in qkv_groups:
        for kind in range(3):
            in_specs.append(pl.BlockSpec(
                (None, None, seq, LANES),
                lambda b, hp, kind=kind: (kind, jnp.minimum(b, last), 0, jnp.where(b > last, n_pairs - 1, hp))))
            args.append(arr)
    const = lambda shape: pl.BlockSpec(shape, lambda b, hp: (0, 0), pipeline_mode=pl.Buffered(1))
    ffn_in = pl.BlockSpec((rows, d), lambda b, hp: (jnp.maximum(b - 1, 0) * n_pairs + hp, 0))
    ffn_out = pl.BlockSpec((rows, d), lambda b, hp: (jnp.where(b == 0, batch, b - 1) * n_pairs + hp, 0))
    biases = _attention_biases(WINDOWS[0] // DILATIONS[0])
    in_specs += [const(t.shape) for t in biases]
    in_specs += [ffn_in, const((d, d)), const((1, d)), const((d, ff)), const((d, ff)), const((ff, d))]
    args += [*biases, x, w_o, gain, w_gate, w_up, w_down]
    return pl.pallas_call(
        _attn_dense_kernel,
        grid=(batch + 1, n_pairs),
        in_specs=in_specs,
        out_specs=ffn_out,
        out_shape=jax.ShapeDtypeStruct((n + seq, d), F32),
        scratch_shapes=[pltpu.VMEM((2, n_pairs, seq, LANES), BF16)]
        + [pltpu.VMEM((seq, LANES), F32)] * 3 + [pltpu.VMEM((LANES, seq), BF16)],
        compiler_params=_params("arbitrary", "arbitrary"),
        name="attention_ffn",
    )(*args)


POOL_HALO = max(POOL_SIZES)


def _pool_tile(i, x_ref, halo_ref, g_ref, w_ref, sc_ref, y_ref):
    ts = x_ref.shape[0]
    g = g_ref[...]
    x = x_ref[...]
    h = _rms(x, g)
    hh = jnp.where(i > 0, _rms(halo_ref[...], g), 0.0)
    ext = jnp.concatenate([hh, h], axis=0)
    pos = i * ts + lax.broadcasted_iota(jnp.int32, (ts, 1), 0)
    pc = w_ref.shape[1]
    for grp, size in enumerate(POOL_SIZES):
        sl = slice(grp * pc, (grp + 1) * pc)
        s = ext[:, sl]
        step = 1
        while step < size:
            s = s + pltpu.roll(s, step, 0)
            step *= 2
        cnt = jnp.minimum(pos + 1, size).astype(F32)
        y = s[POOL_HALO:, :] / cnt - h[:, sl]
        z = jnp.dot(y.astype(BF16), w_ref[grp], preferred_element_type=F32)
        y_ref[:, sl] = x[:, sl] + z * sc_ref[:, sl]


def _pool_router_kernel(x_ref, halo_ref, g_ref, w_ref, sc_ref, g2_ref, wr_ref, tril_ref,
                        y_ref, ri_ref, rg_ref, cnt_ref, hp_ref, run_ref):
    b = pl.program_id(0)
    i = pl.program_id(1)
    _pool_tile(i, x_ref, halo_ref, g_ref, w_ref, sc_ref, y_ref)
    _route_tile((b == 0) & (i == 0), y_ref[...], g2_ref, wr_ref, tril_ref, ri_ref, rg_ref, cnt_ref, hp_ref, run_ref)


def _pool_mixer_and_router(x, gain, pool_w, scale, ffn_gain, w_router, batch, seq):
    d = x.shape[1]
    n = batch * seq
    ne = w_router.shape[1]
    ts = 512
    tiles = seq // ts
    x3 = x.reshape(-1, seq, d)
    hb = ts // POOL_HALO
    tril = (jnp.arange(ts)[:, None] > jnp.arange(ts)[None, :]).astype(BF16)
    const = lambda shape: pl.BlockSpec(shape, lambda b, i: (0,) * len(shape))
    tok = lambda cols: pl.BlockSpec((ts, cols), lambda b, i: (b * tiles + i, 0))
    out, route_i, route_g, counts, h_packed = pl.pallas_call(
        _pool_router_kernel,
        grid=(batch, tiles),
        in_specs=[
            pl.BlockSpec((None, ts, d), lambda b, i: (b, i, 0)),
            pl.BlockSpec((None, POOL_HALO, d), lambda b, i: (b, jnp.maximum(i * hb - 1, 0), 0)),
            const((1, d)), const(pool_w.shape), const((1, d)), const((1, d)), const((d, ne)), const((ts, ts)),
        ],
        out_specs=[
            pl.BlockSpec((None, ts, d), lambda b, i: (b, i, 0)),
            tok(ROUTE_COLS), tok(ROUTE_COLS), const((1, ne)), tok(d // 2),
        ],
        out_shape=[
            jax.ShapeDtypeStruct((batch, seq, d), F32),
            jax.ShapeDtypeStruct((n, ROUTE_COLS), jnp.int32),
            jax.ShapeDtypeStruct((n, ROUTE_COLS), F32),
            jax.ShapeDtypeStruct((1, ne), F32),
            jax.ShapeDtypeStruct((n, d // 2), jnp.int32),
        ],
        scratch_shapes=[pltpu.VMEM((1, ne), F32)],
        compiler_params=_params("arbitrary", "arbitrary"),
        name="pool_router",
    )(x3, x3, gain, pool_w, scale, ffn_gain, w_router, tril)
    return out.reshape(n, d), route_i, route_g, counts, h_packed


ROUTE_COLS = 8


def _pack_bf16_pairs(hr):
    c = hr.shape[1] // 2
    bits = pltpu.bitcast(hr, jnp.int32)
    return (bits[:, c:] & jnp.int32(-65536)) | lax.shift_right_logical(bits[:, :c], 16)


def _unpack_bf16_pairs(packed):
    lo = pltpu.bitcast(lax.shift_left(packed, 16), F32)
    hi = pltpu.bitcast(packed & jnp.int32(-65536), F32)
    return jnp.concatenate([lo.astype(BF16), hi.astype(BF16)], axis=1)


def _route_tile(first, x, g_ref, wr_ref, tril_ref, ri_ref, rg_ref, cnt_ref, hp_ref, run_ref):
    @pl.when(first)
    def _():
        run_ref[...] = jnp.zeros_like(run_ref)

    h = _rms(x, g_ref[...])
    wr = wr_ref[...]
    ne = wr.shape[1]
    h_hi = h.astype(BF16)
    hp_ref[...] = _pack_bf16_pairs(h_hi.astype(F32))
    h_lo = (h - h_hi.astype(F32)).astype(BF16)
    w_hi = wr.astype(BF16)
    w_lo = (wr - w_hi.astype(F32)).astype(BF16)
    hi_part = jnp.dot(h_hi, jnp.concatenate([w_hi, w_lo], axis=1), preferred_element_type=F32)
    logits = hi_part[:, :ne] + hi_part[:, ne:] + jnp.dot(h_lo, w_hi, preferred_element_type=F32)
    tm = logits.shape[0]
    lane = lax.broadcasted_iota(jnp.int32, (tm, ne), 1)
    v1 = jnp.max(logits, axis=1, keepdims=True)
    i1 = jnp.min(jnp.where(logits == v1, lane, ne), axis=1, keepdims=True)
    rest = jnp.where(lane == i1, -jnp.inf, logits)
    v2 = jnp.max(rest, axis=1, keepdims=True)
    i2 = jnp.min(jnp.where(rest == v2, lane, ne), axis=1, keepdims=True)
    e = jnp.exp(v2 - v1)
    g1 = 1.0 / (1.0 + e)
    g2 = e / (1.0 + e)
    oh1 = (lane == i1).astype(F32)
    oh2 = (lane == i2).astype(F32)
    both = oh1 + oh2
    before = jnp.dot(tril_ref[...], both.astype(BF16), preferred_element_type=F32) + run_ref[...]
    r1 = jnp.sum(before * oh1, axis=1, keepdims=True).astype(jnp.int32)
    r2 = jnp.sum(before * oh2, axis=1, keepdims=True).astype(jnp.int32)
    run_ref[...] += jnp.sum(both, axis=0, keepdims=True)
    ri_ref[...] = jnp.where(lane == 0, i1, jnp.where(lane == 1, i2, jnp.where(lane == 2, r1, r2)))
    rg_ref[...] = jnp.where(lane == 0, g1, g2)
    cnt_ref[...] = run_ref[...]


SC_WINDOW = 64


def _sc_workers():
    info = plsc.get_sparse_core_info()
    mesh = plsc.VectorSubcoreMesh(core_axis_name="core", subcore_axis_name="subcore")
    return mesh, info.num_cores, info.num_cores * info.num_subcores


def _dispatch(x, dest, cap):
    n, d = x.shape
    mesh, n_cores, n_workers = _sc_workers()
    per_worker = n // n_workers

    n_windows = per_worker // SC_WINDOW
    n_bufs = 2
    assert n_windows % n_bufs == 0

    @functools.partial(
        pl.kernel, mesh=mesh, out_type=jax.ShapeDtypeStruct((cap, d), x.dtype),
        scratch_types=[pltpu.VMEM((SC_WINDOW,), jnp.int32)] * (TOP_K * n_bufs)
        + [pltpu.VMEM((SC_WINDOW, d), x.dtype)] * n_bufs + [pltpu.SemaphoreType.DMA] * n_bufs,
        name="moe_dispatch")
    def scatter(x_hbm, *rest):
        idx_hbm, (xs_hbm, *scratch) = rest[:TOP_K], rest[TOP_K:]
        idx_vmem = [scratch[b * TOP_K:(b + 1) * TOP_K] for b in range(n_bufs)]
        rows = scratch[TOP_K * n_bufs:TOP_K * n_bufs + n_bufs]
        sems = scratch[TOP_K * n_bufs + n_bufs:]
        wid = lax.axis_index("subcore") * n_cores + lax.axis_index("core")

        def drain(buf):
            for k in range(TOP_K):
                pltpu.make_async_copy(rows[buf], xs_hbm.at[idx_vmem[buf][k]], sems[buf]).wait()

        @pl.loop(0, n_windows, step=n_bufs)
        def _(c):
            for buf in range(n_bufs):
                @pl.when(c > 0)
                def _(buf=buf):
                    drain(buf)

                window = pl.ds(wid * per_worker + (c + buf) * SC_WINDOW, SC_WINDOW)
                for k in range(TOP_K):
                    pltpu.sync_copy(idx_hbm[k].at[window], idx_vmem[buf][k])
                pltpu.sync_copy(x_hbm.at[window], rows[buf])
                for k in range(TOP_K):
                    pltpu.async_copy(rows[buf], xs_hbm.at[idx_vmem[buf][k]], sems[buf])

        for buf in range(n_bufs):
            drain(buf)

    return scatter(x, *[dest[k] for k in range(TOP_K)])


def _gather_rows(table, idx):
    m = idx.shape[0]
    d = table.shape[1]
    mesh, n_cores, n_workers = _sc_workers()
    per_worker = m // n_workers

    n_windows = per_worker // SC_WINDOW
    n_bufs = 2
    assert n_windows % n_bufs == 0

    @functools.partial(
        pl.kernel, mesh=mesh, out_type=jax.ShapeDtypeStruct((m, d), table.dtype),
        scratch_types=[pltpu.VMEM((SC_WINDOW,), jnp.int32)] * n_bufs
        + [pltpu.VMEM((SC_WINDOW, d), table.dtype)] * n_bufs + [pltpu.SemaphoreType.DMA] * n_bufs,
        name="moe_gather")
    def gather(t_hbm, i_hbm, o_hbm, *scratch):
        idx_vmem, rows, sems = scratch[:n_bufs], scratch[n_bufs:2 * n_bufs], scratch[2 * n_bufs:]
        wid = lax.axis_index("subcore") * n_cores + lax.axis_index("core")

        def window(c):
            return pl.ds(wid * per_worker + c * SC_WINDOW, SC_WINDOW)

        def fetch(c, buf):
            pltpu.sync_copy(i_hbm.at[window(c)], idx_vmem[buf])
            return pltpu.async_copy(t_hbm.at[idx_vmem[buf]], rows[buf], sems[buf])

        fetch(0, 0)

        @pl.loop(0, n_windows, step=n_bufs)
        def _(c):
            for buf in range(n_bufs):
                nxt = c + buf + 1

                @pl.when(nxt < n_windows)
                def _(nxt=nxt, buf=buf):
                    fetch(nxt, (buf + 1) % n_bufs)

                pltpu.make_async_copy(t_hbm.at[idx_vmem[buf]], rows[buf], sems[buf]).wait()
                pltpu.sync_copy(rows[buf], o_hbm.at[window(c + buf)])

    return gather(table, idx)


def _expert_kernel(be_ref, nv_ref, xs_ref, wg_ref, wu_ref, wd_ref, ys_ref, xb_ref, acc_ref):
    del be_ref
    b = pl.program_id(0)
    j = pl.program_id(1)

    def ff_tile(first):
        x = xb_ref[...]
        acts = []
        for c in range(wg_ref.shape[1] // MXU_DIM):
            cols = slice(c * MXU_DIM, (c + 1) * MXU_DIM)
            gate = jnp.dot(x, wg_ref[:, cols].astype(BF16), preferred_element_type=F32)
            up = jnp.dot(x, wu_ref[:, cols].astype(BF16), preferred_element_type=F32)
            acts.append((_silu(gate) * up).astype(BF16))
        part = jnp.dot(jnp.concatenate(acts, axis=1), wd_ref[...].astype(BF16), preferred_element_type=F32)
        if first:
            acc_ref[...] = part
        else:
            acc_ref[...] += part

    valid = b < nv_ref[0]

    @pl.when(valid & (j == 0))
    def _():
        xb_ref[...] = _unpack_bf16_pairs(xs_ref[...])
        ff_tile(True)

    @pl.when(valid & (j > 0))
    def _():
        ff_tile(False)

    @pl.when(jnp.logical_not(valid) & (j == 0))
    def _():
        acc_ref[...] = jnp.zeros_like(acc_ref)

    @pl.when(j == pl.num_programs(1) - 1)
    def _():
        ys_ref[...] = _pack_bf16_pairs(acc_ref[...].astype(BF16).astype(F32))


def _experts(xs, block_e, n_valid, w_gate, w_up, w_down, layer, tm, tf):
    cap = xs.shape[0]
    d = 2 * xs.shape[1]
    ff = w_gate.shape[3]
    nj = ff // tf

    def blk(b, nv):
        return jnp.minimum(b, nv[0] - 1)

    def ffi(b, j, nv):
        return jnp.where(b < nv[0], j, nj - 1)

    return pl.pallas_call(
        _expert_kernel,
        grid_spec=pltpu.PrefetchScalarGridSpec(
            num_scalar_prefetch=2,
            grid=(cap // tm, nj),
            in_specs=[
                pl.BlockSpec((tm, d // 2), lambda b, j, be, nv: (blk(b, nv), 0)),
                pl.BlockSpec((None, None, d, tf), lambda b, j, be, nv: (layer, be[blk(b, nv)], 0, ffi(b, j, nv))),
                pl.BlockSpec((None, None, d, tf), lambda b, j, be, nv: (layer, be[blk(b, nv)], 0, ffi(b, j, nv))),
                pl.BlockSpec((None, None, tf, d), lambda b, j, be, nv: (layer, be[blk(b, nv)], ffi(b, j, nv), 0)),
            ],
            out_specs=pl.BlockSpec((tm, d // 2), lambda b, j, be, nv: (b, 0)),
            scratch_shapes=[pltpu.VMEM((tm, d), BF16), pltpu.VMEM((tm, d), F32)],
        ),
        out_shape=jax.ShapeDtypeStruct((cap, d // 2), jnp.int32),
        compiler_params=_params("arbitrary", "arbitrary"),
        name="moe_experts",
    )(block_e, n_valid, xs, w_gate, w_up, w_down)


def _combine_kernel(x_ref, rg_ref, gf_ref, y0_ref, y1_ref, o_ref):
    rg = rg_ref[...]
    y0 = _unpack_bf16_pairs(y0_ref[...]).astype(F32)
    y1 = _unpack_bf16_pairs(y1_ref[...]).astype(F32)
    o_ref[...] = _rms(x_ref[...] + (rg[:, 0:1] * y0 + rg[:, 1:2] * y1), gf_ref[...])


def _combine_and_final_norm(x, route_g, yg, final_gain):
    n, d = x.shape
    tm = 512
    return pl.pallas_call(
        _combine_kernel,
        grid=(n // tm,),
        in_specs=[
            pl.BlockSpec((tm, d), lambda i: (i, 0)),
            pl.BlockSpec((tm, ROUTE_COLS), lambda i: (i, 0)),
            pl.BlockSpec((1, d), lambda i: (0, 0)),
            pl.BlockSpec((None, tm, d // 2), lambda i: (0, i, 0)),
            pl.BlockSpec((None, tm, d // 2), lambda i: (1, i, 0)),
        ],
        out_specs=pl.BlockSpec((tm, d), lambda i: (i, 0)),
        out_shape=jax.ShapeDtypeStruct((n, d), F32),
        compiler_params=_params("parallel"),
        name="moe_combine",
    )(x, route_g, final_gain, yg, yg)


def _moe_layer(x, routing, w_gate, w_up, w_down, layer):
    n, d = x.shape
    tm = 1024 if n * TOP_K >= 8 * 1024 else 256
    tf = 2 * MXU_DIM
    route_i, route_g, counts, h_packed = routing

    counts = counts[0].astype(jnp.int32)
    padded = ((counts + tm - 1) // tm) * tm
    ends = jnp.cumsum(padded)
    starts = ends - padded
    dest = (starts[route_i[:, :TOP_K]] + route_i[:, TOP_K:2 * TOP_K]).T
    n_blocks = (n * TOP_K) // tm + N_EXPERTS
    block_start = jnp.arange(n_blocks, dtype=jnp.int32) * tm
    block_e = jnp.minimum(jnp.sum(ends[None, :] <= block_start[:, None], axis=1), N_EXPERTS - 1).astype(jnp.int32)
    n_valid = (ends[-1:] // tm).astype(jnp.int32)

    xs = _dispatch(h_packed, dest, n_blocks * tm)
    ys = _experts(xs, block_e, n_valid, w_gate, w_up, w_down, layer, tm, tf)
    return route_g, _gather_rows(ys, dest.reshape(TOP_K * n)).reshape(TOP_K, n, d // 2)


def _rope_tables(seq):
    pos = jnp.arange(seq, dtype=F32)
    inv_freq = ROPE_THETA ** (-jnp.arange(0, HEAD_DIM, 2, dtype=F32) / HEAD_DIM)
    ang = pos[:, None] * inv_freq[None, :]
    reps = LANES // (HEAD_DIM // 2)
    cos = jnp.tile(jnp.cos(ang), (1, reps))
    sin = jnp.tile(jnp.sin(ang), (1, reps))
    first_half = (jnp.arange(LANES) % HEAD_DIM) < HEAD_DIM // 2
    tabs = (cos, jnp.where(first_half, -sin, sin))
    q_scale = HEAD_DIM ** -0.5 * 1.4426950408889634
    return tuple(jnp.stack([t * q_scale, t]) for t in tabs)


def kernel(x, mix_norm, ffn_norm, attn_w_qkv, attn_w_o, pool_w, pool_scale, dense_w_gate, dense_w_up,
           dense_w_down, moe_router, moe_w_gate, moe_w_up, moe_w_down, final_norm):
    batch, seq, d = x.shape
    depth = mix_norm.shape[0]
    assert depth % 2 == 0 and d == N_HEADS * HEAD_DIM
    assert all(seq % (dil * (win // dil)) == 0 for win, dil in zip(WINDOWS, DILATIONS))
    assert seq % QKV_ROWS == 0 and seq % 512 == 0 and moe_router.shape[2] == N_EXPERTS
    rope_tables = _rope_tables(seq)
    xf = x.reshape(batch * seq, d)
    final_gain = final_norm.reshape(1, d)
    pending_moe = None
    for i in range(depth):
        j = i // 2
        mix_gain = mix_norm[i].reshape(1, d)
        ffn_gain = ffn_norm[i].reshape(1, d)
        if i % 2 == 0:
            w_qkv = attn_w_qkv[j].astype(BF16)
            qkv, xf = _qkv_project(xf, mix_gain, w_qkv, rope_tables, batch, seq, pending_moe)
            pending_moe = None
            xf = _attention_and_dense_ffn(qkv, xf, attn_w_o[j].astype(BF16), ffn_gain, dense_w_gate[j].astype(BF16),
                                          dense_w_up[j].astype(BF16), dense_w_down[j].astype(BF16), batch, seq)
        else:
            xf, *routing = _pool_mixer_and_router(xf, mix_gain, pool_w[j].astype(BF16), pool_scale[j].reshape(1, d),
                                                  ffn_gain, moe_router[j], batch, seq)
            pending_moe = _moe_layer(xf, routing, moe_w_gate, moe_w_up, moe_w_down, j)
    return _combine_and_final_norm(xf, *pending_moe, final_gain).reshape(batch, seq, d)
```

```python
import functools

import jax
import jax.numpy as jnp
from jax import lax
from jax.experimental import pallas as pl
from jax.experimental.pallas import tpu as pltpu
from jax.experimental.pallas import tpu_sc as plsc

WINDOWS = (128, 512, 2048)
DILATIONS = (1, 4, 16)
N_GROUPS = len(WINDOWS)
N_HEADS = 16
HEAD_DIM = 64
ROPE_THETA = 10000.0
POOL_SIZES = (2, 4, 8, 16)
N_EXPERTS = 8
TOP_K = 2
RMS_EPS = 1e-6
MASK_VALUE = -1e30
QKV_ROWS = 512

LANES = 128
MXU_DIM = 256
VMEM_LIMIT_BYTES = 56 * 1024 * 1024

F32 = jnp.float32
BF16 = jnp.bfloat16


def _params(*sem):
    return pltpu.CompilerParams(dimension_semantics=sem, vmem_limit_bytes=VMEM_LIMIT_BYTES)


def _rms(x, g):
    ms = jnp.mean(x * x, axis=-1, keepdims=True)
    return x * lax.rsqrt(ms + RMS_EPS) * g


def _silu(g):
    return g / (1.0 + jnp.exp(-g))


def _qkv_kernel(x_ref, g_ref, w_ref, cos_ref, sin_ref, *rest, pending_moe):
    if pending_moe:
        rg_ref, y0_ref, y1_ref, o0_ref, o1_ref, o2_ref, xn_ref, acc_ref = rest
        rg = rg_ref[...]
        x = x_ref[...] + (rg[:, 0:1] * _unpack_bf16_pairs(y0_ref[...]).astype(F32)
                          + rg[:, 1:2] * _unpack_bf16_pairs(y1_ref[...]).astype(F32))
        xn_ref[...] = x
    else:
        o0_ref, o1_ref, o2_ref, acc_ref = rest
        x = x_ref[...]
    tm, d = x_ref.shape
    n_chunks = acc_ref.shape[1]
    lane = lax.broadcasted_iota(jnp.int32, (1, LANES), 1)
    first_half = (lane % HEAD_DIM) < HEAD_DIM // 2
    h = _rms(x, g_ref[...]).astype(BF16)
    for g, (o_ref, dil) in enumerate(zip((o0_ref, o1_ref, o2_ref), DILATIONS)):
        rows = tm // dil

        def strided(ref, r, dil=dil, rows=rows):
            if dil == 1:
                return ref[...]
            return ref[pl.ds(r, rows, stride=dil), :]

        for kind in range(3):
            col0 = (g * 3 + kind) * d
            acc = jnp.dot(h, w_ref[:, col0:col0 + d], preferred_element_type=F32)
            for c in range(n_chunks):
                acc_ref[kind, c] = acc[:, c * LANES:(c + 1) * LANES]
            for r in range(dil):
                if kind < 2:
                    cos = strided(cos_ref.at[kind], r)
                    sin = strided(sin_ref.at[kind], r)
                for c in range(n_chunks):
                    t = strided(acc_ref.at[kind, c], r)
                    if kind < 2:
                        partner = jnp.where(first_half, pltpu.roll(t, LANES - HEAD_DIM // 2, 1),
                                            pltpu.roll(t, HEAD_DIM // 2, 1))
                        t = t * cos + partner * sin
                    o_ref[kind, r, :, c * LANES:(c + 1) * LANES] = t.astype(BF16)


def _qkv_project(x, gain, w_qkv, tables, batch, seq, pending_moe=None):
    n, d = x.shape
    tm = QKV_ROWS // 2 if pending_moe else QKV_ROWS
    tiles_per_seq = seq // tm
    tab = pl.BlockSpec((2, tm, LANES), lambda i: (0, i % tiles_per_seq, 0))

    def o_spec(dil):
        return pl.BlockSpec((3, None, dil, tm // dil, d),
                            lambda i: (0, i // tiles_per_seq, 0, i % tiles_per_seq, 0))

    row = pl.BlockSpec((tm, d), lambda i: (i, 0))
    in_specs = [row, pl.BlockSpec((1, d), lambda i: (0, 0)),
                pl.BlockSpec(w_qkv.shape, lambda i: (0, 0), pipeline_mode=pl.Buffered(1)), tab, tab]
    args = [x, gain, w_qkv, *tables]
    out_specs = [o_spec(dil) for dil in DILATIONS]
    out_shape = [jax.ShapeDtypeStruct((3, batch, dil, seq // dil, d), BF16) for dil in DILATIONS]
    if pending_moe:
        route_g, yg = pending_moe
        in_specs += [pl.BlockSpec((tm, ROUTE_COLS), lambda i: (i, 0))]
        in_specs += [pl.BlockSpec((None, tm, d // 2), lambda i, k=k: (k, i, 0)) for k in range(TOP_K)]
        args += [route_g, yg, yg]
        out_specs.append(row)
        out_shape.append(jax.ShapeDtypeStruct((n, d), F32))
    outs = pl.pallas_call(
        functools.partial(_qkv_kernel, pending_moe=bool(pending_moe)),
        grid=(n // tm,),
        in_specs=in_specs,
        out_specs=out_specs,
        out_shape=out_shape,
        scratch_shapes=[pltpu.VMEM((3, d // LANES, tm, LANES), F32)],
        compiler_params=_params("parallel"),
        name="qkv_rope",
    )(*args)
    return [o.reshape(3, batch, seq, d) for o in outs[:N_GROUPS]], (outs[N_GROUPS] if pending_moe else x)


ATTN_TILES_PER_FFN_PIECE = 3


def _attn_dense_kernel(q0, k0, v0, q1, k1, v1, q2, k2, v2, band_ref, causal_ref,
                       x_ref, wo_ref, g_ref, wg_ref, wu_ref, wd_ref, y_ref,
                       o_scr, num_ref, max_ref, den_ref, kt_ref):
    b = pl.program_id(0)
    hp = pl.program_id(1)
    n_pairs, seq = o_scr.shape[1], o_scr.shape[2]
    w = WINDOWS[0] // DILATIONS[0]
    lane = lax.broadcasted_iota(jnp.int32, (1, LANES), 1)
    first_head = lane < HEAD_DIM
    o_out = o_scr.at[b % 2, hp]
    o_in = o_scr.at[(b + 1) % 2]

    def ffn_pieces():
        rows = pl.ds(pl.multiple_of(hp * x_ref.shape[0], x_ref.shape[0]), x_ref.shape[0])
        o = jnp.concatenate([o_in[p, rows, :] for p in range(n_pairs)], axis=1)
        x = x_ref[...] + jnp.dot(o, wo_ref[...], preferred_element_type=F32)
        h = _rms(x, g_ref[...]).astype(BF16)
        yield
        acts = []
        n_chunks = wg_ref.shape[1] // MXU_DIM
        for c in range(n_chunks):
            cols = slice(c * MXU_DIM, (c + 1) * MXU_DIM)
            gate = jnp.dot(h, wg_ref[:, cols], preferred_element_type=F32)
            up = jnp.dot(h, wu_ref[:, cols], preferred_element_type=F32)
            acts.append((_silu(gate) * up).astype(BF16))
            yield
        half = n_chunks // 2
        acc = jnp.dot(jnp.concatenate(acts[:half], axis=1), wd_ref[:half * MXU_DIM, :], preferred_element_type=F32)
        yield
        acc = acc + jnp.dot(jnp.concatenate(acts[half:], axis=1), wd_ref[half * MXU_DIM:, :],
                            preferred_element_type=F32)
        y_ref[...] = x + acc
        yield

    def tile(q_ref, v_ref, qoff, has_prev):
        q = q_ref[qoff:qoff + w, :]
        zero = jnp.zeros_like(q)
        qq = jnp.concatenate([jnp.where(first_head, q, zero), jnp.where(first_head, zero, q)], axis=0)
        k_lo = qoff - w if has_prev else qoff
        s = jnp.dot(qq, kt_ref[:, k_lo:qoff + w], preferred_element_type=F32)
        s = s + (band_ref[...] if has_prev else causal_ref[...])
        mx = jnp.max(s, axis=1, keepdims=True)
        p = jnp.exp2(s - mx)
        den = jnp.sum(p, axis=1, keepdims=True)
        pv = jnp.dot(p.astype(BF16), v_ref[k_lo:qoff + w, :], preferred_element_type=F32)
        return tuple(jnp.where(first_head, a[:w], a[w:]) for a in (pv, mx, den))

    def attention(with_ffn):
        pieces = ffn_pieces() if with_ffn else iter(())
        next(pieces, None)
        tiles_done = 0
        order = sorted(range(N_GROUPS), key=lambda g: -DILATIONS[g])
        for g in order:
            q_ref, k_ref, v_ref = ((q0, k0, v0), (q1, k1, v1), (q2, k2, v2))[g]
            dil = DILATIONS[g]
            sub_len = seq // dil
            nb = sub_len // w
            kt_ref[...] = k_ref[...].T
            for t in range(dil * nb):
                r, n = divmod(t, nb)
                qoff = r * sub_len + n * w
                pv, mx, den = tile(q_ref, v_ref, qoff, n > 0)
                tiles_done += 1
                if tiles_done % ATTN_TILES_PER_FFN_PIECE == 0:
                    next(pieces, None)
                rows = pl.ds(qoff, w) if dil == 1 else pl.ds(r + dil * n * w, w, stride=dil)
                if g == order[0]:
                    num_ref[rows, :] = pv
                    max_ref[rows, :] = mx
                    den_ref[rows, :] = den
                    continue
                old_max = max_ref[rows, :]
                new_max = jnp.maximum(old_max, mx)
                a = jnp.exp2(old_max - new_max)
                c = jnp.exp2(mx - new_max)
                num = a * num_ref[rows, :] + c * pv
                den = a * den_ref[rows, :] + c * den
                if g == order[-1]:
                    o_out[rows, :] = (num / den).astype(BF16)
                else:
                    num_ref[rows, :] = num
                    den_ref[rows, :] = den
                    max_ref[rows, :] = new_max
        for _ in pieces:
            pass

    n_batches = pl.num_programs(0) - 1
    @pl.when(b == 0)
    def _():
        y_ref[...] = jnp.zeros_like(y_ref)
        attention(False)

    pl.when((b > 0) & (b < n_batches))(functools.partial(attention, True))

    @pl.when(b == n_batches)
    def _():
        for _ in ffn_pieces():
            pass


def _attention_biases(w):
    row = jnp.arange(2 * w)[:, None] % w
    col = jnp.arange(2 * w)[None, :]
    dist = row + w - col
    band = jnp.where((dist >= 0) & (dist <= w), 0.0, MASK_VALUE).astype(F32)
    causal = jnp.where(row >= col[:, :w], 0.0, MASK_VALUE).astype(F32)
    return band, causal


def _attention_and_dense_ffn(qkv_groups, x, w_o, gain, w_gate, w_up, w_down, batch, seq):
    n, d = x.shape
    ff = w_gate.shape[1]
    n_pairs = d // LANES
    rows = seq // n_pairs
    last = batch - 1
    in_specs, args = [], []
    for arr in qkv_groups:
        for kind in range(3):
            in_specs.append(pl.BlockSpec(
                (None, None, seq, LANES),
                lambda b, hp, kind=kind: (kind, jnp.minimum(b, last), 0, jnp.where(b > last, n_pairs - 1, hp))))
            args.append(arr)
    const = lambda shape: pl.BlockSpec(shape, lambda b, hp: (0, 0), pipeline_mode=pl.Buffered(1))
    ffn_in = pl.BlockSpec((rows, d), lambda b, hp: (jnp.maximum(b - 1, 0) * n_pairs + hp, 0))
    ffn_out = pl.BlockSpec((rows, d), lambda b, hp: (jnp.where(b == 0, batch, b - 1) * n_pairs + hp, 0))
    biases = _attention_biases(WINDOWS[0] // DILATIONS[0])
    in_specs += [const(t.shape) for t in biases]
    in_specs += [ffn_in, const((d, d)), const((1, d)), const((d, ff)), const((d, ff)), const((ff, d))]
    args += [*biases, x, w_o, gain, w_gate, w_up, w_down]
    return pl.pallas_call(
        _attn_dense_kernel,
        grid=(batch + 1, n_pairs),
        in_specs=in_specs,
        out_specs=ffn_out,
        out_shape=jax.ShapeDtypeStruct((n + seq, d), F32),
        scratch_shapes=[pltpu.VMEM((2, n_pairs, seq, LANES), BF16)]
        + [pltpu.VMEM((seq, LANES), F32)] * 3 + [pltpu.VMEM((LANES, seq), BF16)],
        compiler_params=_params("arbitrary", "arbitrary"),
        name="attention_ffn",
    )(*args)


POOL_HALO = max(POOL_SIZES)


def _pool_tile(i, x_ref, halo_ref, g_ref, w_ref, sc_ref, y_ref):
    ts = x_ref.shape[0]
    g = g_ref[...]
    x = x_ref[...]
    h = _rms(x, g)
    hh = jnp.where(i > 0, _rms(halo_ref[...], g), 0.0)
    ext = jnp.concatenate([hh, h], axis=0)
    pos = i * ts + lax.broadcasted_iota(jnp.int32, (ts, 1), 0)
    pc = w_ref.shape[1]
    for grp, size in enumerate(POOL_SIZES):
        sl = slice(grp * pc, (grp + 1) * pc)
        s = ext[:, sl]
        step = 1
        while step < size:
            s = s + pltpu.roll(s, step, 0)
            step *= 2
        cnt = jnp.minimum(pos + 1, size).astype(F32)
        y = s[POOL_HALO:, :] / cnt - h[:, sl]
        z = jnp.dot(y.astype(BF16), w_ref[grp], preferred_element_type=F32)
        y_ref[:, sl] = x[:, sl] + z * sc_ref[:, sl]


def _pool_router_kernel(x_ref, halo_ref, g_ref, w_ref, sc_ref, g2_ref, wr_ref, tril_ref,
                        y_ref, ri_ref, rg_ref, cnt_ref, hp_ref, run_ref):
    b = pl.program_id(0)
    i = pl.program_id(1)
    _pool_tile(i, x_ref, halo_ref, g_ref, w_ref, sc_ref, y_ref)
    _route_tile((b == 0) & (i == 0), y_ref[...], g2_ref, wr_ref, tril_ref, ri_ref, rg_ref, cnt_ref, hp_ref, run_ref)


def _pool_mixer_and_router(x, gain, pool_w, scale, ffn_gain, w_router, batch, seq):
    d = x.shape[1]
    n = batch * seq
    ne = w_router.shape[1]
    ts = 512
    tiles = seq // ts
    x3 = x.reshape(-1, seq, d)
    hb = ts // POOL_HALO
    tril = (jnp.arange(ts)[:, None] > jnp.arange(ts)[None, :]).astype(BF16)
    const = lambda shape: pl.BlockSpec(shape, lambda b, i: (0,) * len(shape))
    tok = lambda cols: pl.BlockSpec((ts, cols), lambda b, i: (b * tiles + i, 0))
    out, route_i, route_g, counts, h_packed = pl.pallas_call(
        _pool_router_kernel,
        grid=(batch, tiles),
        in_specs=[
            pl.BlockSpec((None, ts, d), lambda b, i: (b, i, 0)),
            pl.BlockSpec((None, POOL_HALO, d), lambda b, i: (b, jnp.maximum(i * hb - 1, 0), 0)),
            const((1, d)), const(pool_w.shape), const((1, d)), const((1, d)), const((d, ne)), const((ts, ts)),
        ],
        out_specs=[
            pl.BlockSpec((None, ts, d), lambda b, i: (b, i, 0)),
            tok(ROUTE_COLS), tok(ROUTE_COLS), const((1, ne)), tok(d // 2),
        ],
        out_shape=[
            jax.ShapeDtypeStruct((batch, seq, d), F32),
            jax.ShapeDtypeStruct((n, ROUTE_COLS), jnp.int32),
            jax.ShapeDtypeStruct((n, ROUTE_COLS), F32),
            jax.ShapeDtypeStruct((1, ne), F32),
            jax.ShapeDtypeStruct((n, d // 2), jnp.int32),
        ],
        scratch_shapes=[pltpu.VMEM((1, ne), F32)],
        compiler_params=_params("arbitrary", "arbitrary"),
        name="pool_router",
    )(x3, x3, gain, pool_w, scale, ffn_gain, w_router, tril)
    return out.reshape(n, d), route_i, route_g, counts, h_packed


ROUTE_COLS = 8


def _pack_bf16_pairs(hr):
    c = hr.shape[1] // 2
    bits = pltpu.bitcast(hr, jnp.int32)
    return (bits[:, c:] & jnp.int32(-65536)) | lax.shift_right_logical(bits[:, :c], 16)


def _unpack_bf16_pairs(packed):
    lo = pltpu.bitcast(lax.shift_left(packed, 16), F32)
    hi = pltpu.bitcast(packed & jnp.int32(-65536), F32)
    return jnp.concatenate([lo.astype(BF16), hi.astype(BF16)], axis=1)


def _route_tile(first, x, g_ref, wr_ref, tril_ref, ri_ref, rg_ref, cnt_ref, hp_ref, run_ref):
    @pl.when(first)
    def _():
        run_ref[...] = jnp.zeros_like(run_ref)

    h = _rms(x, g_ref[...])
    wr = wr_ref[...]
    ne = wr.shape[1]
    h_hi = h.astype(BF16)
    hp_ref[...] = _pack_bf16_pairs(h_hi.astype(F32))
    h_lo = (h - h_hi.astype(F32)).astype(BF16)
    w_hi = wr.astype(BF16)
    w_lo = (wr - w_hi.astype(F32)).astype(BF16)
    hi_part = jnp.dot(h_hi, jnp.concatenate([w_hi, w_lo], axis=1), preferred_element_type=F32)
    logits = hi_part[:, :ne] + hi_part[:, ne:] + jnp.dot(h_lo, w_hi, preferred_element_type=F32)
    tm = logits.shape[0]
    lane = lax.broadcasted_iota(jnp.int32, (tm, ne), 1)
    v1 = jnp.max(logits, axis=1, keepdims=True)
    i1 = jnp.min(jnp.where(logits == v1, lane, ne), axis=1, keepdims=True)
    rest = jnp.where(lane == i1, -jnp.inf, logits)
    v2 = jnp.max(rest, axis=1, keepdims=True)
    i2 = jnp.min(jnp.where(rest == v2, lane, ne), axis=1, keepdims=True)
    e = jnp.exp(v2 - v1)
    g1 = 1.0 / (1.0 + e)
    g2 = e / (1.0 + e)
    oh1 = (lane == i1).astype(F32)
    oh2 = (lane == i2).astype(F32)
    both = oh1 + oh2
    before = jnp.dot(tril_ref[...], both.astype(BF16), preferred_element_type=F32) + run_ref[...]
    r1 = jnp.sum(before * oh1, axis=1, keepdims=True).astype(jnp.int32)
    r2 = jnp.sum(before * oh2, axis=1, keepdims=True).astype(jnp.int32)
    run_ref[...] += jnp.sum(both, axis=0, keepdims=True)
    ri_ref[...] = jnp.where(lane == 0, i1, jnp.where(lane == 1, i2, jnp.where(lane == 2, r1, r2)))
    rg_ref[...] = jnp.where(lane == 0, g1, g2)
    cnt_ref[...] = run_ref[...]


SC_WINDOW = 64


def _sc_workers():
    info = plsc.get_sparse_core_info()
    mesh = plsc.VectorSubcoreMesh(core_axis_name="core", subcore_axis_name="subcore")
    return mesh, info.num_cores, info.num_cores * info.num_subcores


def _dispatch(x, dest, cap):
    n, d = x.shape
    mesh, n_cores, n_workers = _sc_workers()
    per_worker = n // n_workers

    n_windows = per_worker // SC_WINDOW
    n_bufs = 2
    assert n_windows % n_bufs == 0

    @functools.partial(
        pl.kernel, mesh=mesh, out_type=jax.ShapeDtypeStruct((cap, d), x.dtype),
        scratch_types=[pltpu.VMEM((SC_WINDOW,), jnp.int32)] * (TOP_K * n_bufs)
        + [pltpu.VMEM((SC_WINDOW, d), x.dtype)] * n_bufs + [pltpu.SemaphoreType.DMA] * n_bufs,
        name="moe_dispatch")
    def scatter(x_hbm, *rest):
        idx_hbm, (xs_hbm, *scratch) = rest[:TOP_K], rest[TOP_K:]
        idx_vmem = [scratch[b * TOP_K:(b + 1) * TOP_K] for b in range(n_bufs)]
        rows = scratch[TOP_K * n_bufs:TOP_K * n_bufs + n_bufs]
        sems = scratch[TOP_K * n_bufs + n_bufs:]
        wid = lax.axis_index("subcore") * n_cores + lax.axis_index("core")

        def drain(buf):
            for k in range(TOP_K):
                pltpu.make_async_copy(rows[buf], xs_hbm.at[idx_vmem[buf][k]], sems[buf]).wait()

        @pl.loop(0, n_windows, step=n_bufs)
        def _(c):
            for buf in range(n_bufs):
                @pl.when(c > 0)
                def _(buf=buf):
                    drain(buf)

                window = pl.ds(wid * per_worker + (c + buf) * SC_WINDOW, SC_WINDOW)
                for k in range(TOP_K):
                    pltpu.sync_copy(idx_hbm[k].at[window], idx_vmem[buf][k])
                pltpu.sync_copy(x_hbm.at[window], rows[buf])
                for k in range(TOP_K):
                    pltpu.async_copy(rows[buf], xs_hbm.at[idx_vmem[buf][k]], sems[buf])

        for buf in range(n_bufs):
            drain(buf)

    return scatter(x, *[dest[k] for k in range(TOP_K)])


def _gather_rows(table, idx):
    m = idx.shape[0]
    d = table.shape[1]
    mesh, n_cores, n_workers = _sc_workers()
    per_worker = m // n_workers

    n_windows = per_worker // SC_WINDOW
    n_bufs = 2
    assert n_windows % n_bufs == 0

    @functools.partial(
        pl.kernel, mesh=mesh, out_type=jax.ShapeDtypeStruct((m, d), table.dtype),
        scratch_types=[pltpu.VMEM((SC_WINDOW,), jnp.int32)] * n_bufs
        + [pltpu.VMEM((SC_WINDOW, d), table.dtype)] * n_bufs + [pltpu.SemaphoreType.DMA] * n_bufs,
        name="moe_gather")
    def gather(t_hbm, i_hbm, o_hbm, *scratch):
        idx_vmem, rows, sems = scratch[:n_bufs], scratch[n_bufs:2 * n_bufs], scratch[2 * n_bufs:]
        wid = lax.axis_index("subcore") * n_cores + lax.axis_index("core")

        def window(c):
            return pl.ds(wid * per_worker + c * SC_WINDOW, SC_WINDOW)

        def fetch(c, buf):
            pltpu.sync_copy(i_hbm.at[window(c)], idx_vmem[buf])
            return pltpu.async_copy(t_hbm.at[idx_vmem[buf]], rows[buf], sems[buf])

        fetch(0, 0)

        @pl.loop(0, n_windows, step=n_bufs)
        def _(c):
            for buf in range(n_bufs):
                nxt = c + buf + 1

                @pl.when(nxt < n_windows)
                def _(nxt=nxt, buf=buf):
                    fetch(nxt, (buf + 1) % n_bufs)

                pltpu.make_async_copy(t_hbm.at[idx_vmem[buf]], rows[buf], sems[buf]).wait()
                pltpu.sync_copy(rows[buf], o_hbm.at[window(c + buf)])

    return gather(table, idx)


def _expert_kernel(be_ref, nv_ref, xs_ref, wg_ref, wu_ref, wd_ref, ys_ref, xb_ref, acc_ref):
    del be_ref
    b = pl.program_id(0)
    j = pl.program_id(1)

    def ff_tile(first):
        x = xb_ref[...]
        acts = []
        for c in range(wg_ref.shape[1] // MXU_DIM):
            cols = slice(c * MXU_DIM, (c + 1) * MXU_DIM)
            gate = jnp.dot(x, wg_ref[:, cols].astype(BF16), preferred_element_type=F32)
            up = jnp.dot(x, wu_ref[:, cols].astype(BF16), preferred_element_type=F32)
            acts.append((_silu(gate) * up).astype(BF16))
        part = jnp.dot(jnp.concatenate(acts, axis=1), wd_ref[...].astype(BF16), preferred_element_type=F32)
        if first:
            acc_ref[...] = part
        else:
            acc_ref[...] += part

    valid = b < nv_ref[0]

    @pl.when(valid & (j == 0))
    def _():
        xb_ref[...] = _unpack_bf16_pairs(xs_ref[...])
        ff_tile(True)

    @pl.when(valid & (j > 0))
    def _():
        ff_tile(False)

    @pl.when(jnp.logical_not(valid) & (j == 0))
    def _():
        acc_ref[...] = jnp.zeros_like(acc_ref)

    @pl.when(j == pl.num_programs(1) - 1)
    def _():
        ys_ref[...] = _pack_bf16_pairs(acc_ref[...].astype(BF16).astype(F32))


def _experts(xs, block_e, n_valid, w_gate, w_up, w_down, layer, tm, tf):
    cap = xs.shape[0]
    d = 2 * xs.shape[1]
    ff = w_gate.shape[3]
    nj = ff // tf

    def blk(b, nv):
        return jnp.minimum(b, nv[0] - 1)

    def ffi(b, j, nv):
        return jnp.where(b < nv[0], j, nj - 1)

    return pl.pallas_call(
        _expert_kernel,
        grid_spec=pltpu.PrefetchScalarGridSpec(
            num_scalar_prefetch=2,
            grid=(cap // tm, nj),
            in_specs=[
                pl.BlockSpec((tm, d // 2), lambda b, j, be, nv: (blk(b, nv), 0)),
                pl.BlockSpec((None, None, d, tf), lambda b, j, be, nv: (layer, be[blk(b, nv)], 0, ffi(b, j, nv))),
                pl.BlockSpec((None, None, d, tf), lambda b, j, be, nv: (layer, be[blk(b, nv)], 0, ffi(b, j, nv))),
                pl.BlockSpec((None, None, tf, d), lambda b, j, be, nv: (layer, be[blk(b, nv)], ffi(b, j, nv), 0)),
            ],
            out_specs=pl.BlockSpec((tm, d // 2), lambda b, j, be, nv: (b, 0)),
            scratch_shapes=[pltpu.VMEM((tm, d), BF16), pltpu.VMEM((tm, d), F32)],
        ),
        out_shape=jax.ShapeDtypeStruct((cap, d // 2), jnp.int32),
        compiler_params=_params("arbitrary", "arbitrary"),
        name="moe_experts",
    )(block_e, n_valid, xs, w_gate, w_up, w_down)


def _combine_kernel(x_ref, rg_ref, gf_ref, y0_ref, y1_ref, o_ref):
    rg = rg_ref[...]
    y0 = _unpack_bf16_pairs(y0_ref[...]).astype(F32)
    y1 = _unpack_bf16_pairs(y1_ref[...]).astype(F32)
    o_ref[...] = _rms(x_ref[...] + (rg[:, 0:1] * y0 + rg[:, 1:2] * y1), gf_ref[...])


def _combine_and_final_norm(x, route_g, yg, final_gain):
    n, d = x.shape
    tm = 512
    return pl.pallas_call(
        _combine_kernel,
        grid=(n // tm,),
        in_specs=[
            pl.BlockSpec((tm, d), lambda i: (i, 0)),
            pl.BlockSpec((tm, ROUTE_COLS), lambda i: (i, 0)),
            pl.BlockSpec((1, d), lambda i: (0, 0)),
            pl.BlockSpec((None, tm, d // 2), lambda i: (0, i, 0)),
            pl.BlockSpec((None, tm, d // 2), lambda i: (1, i, 0)),
        ],
        out_specs=pl.BlockSpec((tm, d), lambda i: (i, 0)),
        out_shape=jax.ShapeDtypeStruct((n, d), F32),
        compiler_params=_params("parallel"),
        name="moe_combine",
    )(x, route_g, final_gain, yg, yg)


def _moe_layer(x, routing, w_gate, w_up, w_down, layer):
    n, d = x.shape
    tm = 1024 if n * TOP_K >= 8 * 1024 else 256
    tf = 2 * MXU_DIM
    route_i, route_g, counts, h_packed = routing

    counts = counts[0].astype(jnp.int32)
    padded = ((counts + tm - 1) // tm) * tm
    ends = jnp.cumsum(padded)
    starts = ends - padded
    dest = (starts[route_i[:, :TOP_K]] + route_i[:, TOP_K:2 * TOP_K]).T
    n_blocks = (n * TOP_K) // tm + N_EXPERTS
    block_start = jnp.arange(n_blocks, dtype=jnp.int32) * tm
    block_e = jnp.minimum(jnp.sum(ends[None, :] <= block_start[:, None], axis=1), N_EXPERTS - 1).astype(jnp.int32)
    n_valid = (ends[-1:] // tm).astype(jnp.int32)

    xs = _dispatch(h_packed, dest, n_blocks * tm)
    ys = _experts(xs, block_e, n_valid, w_gate, w_up, w_down, layer, tm, tf)
    return route_g, _gather_rows(ys, dest.reshape(TOP_K * n)).reshape(TOP_K, n, d // 2)


def _rope_tables(seq):
    pos = jnp.arange(seq, dtype=F32)
    inv_freq = ROPE_THETA ** (-jnp.arange(0, HEAD_DIM, 2, dtype=F32) / HEAD_DIM)
    ang = pos[:, None] * inv_freq[None, :]
    reps = LANES // (HEAD_DIM // 2)
    cos = jnp.tile(jnp.cos(ang), (1, reps))
    sin = jnp.tile(jnp.sin(ang), (1, reps))
    first_half = (jnp.arange(LANES) % HEAD_DIM) < HEAD_DIM // 2
    tabs = (cos, jnp.where(first_half, -sin, sin))
    q_scale = HEAD_DIM ** -0.5 * 1.4426950408889634
    return tuple(jnp.stack([t * q_scale, t]) for t in tabs)


def kernel(x, mix_norm, ffn_norm, attn_w_qkv, attn_w_o, pool_w, pool_scale, dense_w_gate, dense_w_up,
           dense_w_down, moe_router, moe_w_gate, moe_w_up, moe_w_down, final_norm):
    batch, seq, d = x.shape
    depth = mix_norm.shape[0]
    assert depth % 2 == 0 and d == N_HEADS * HEAD_DIM
    assert all(seq % (dil * (win // dil)) == 0 for win, dil in zip(WINDOWS, DILATIONS))
    assert seq % QKV_ROWS == 0 and seq % 512 == 0 and moe_router.shape[2] == N_EXPERTS
    rope_tables = _rope_tables(seq)
    xf = x.reshape(batch * seq, d)
    final_gain = final_norm.reshape(1, d)
    pending_moe = None
    for i in range(depth):
        j = i // 2
        mix_gain = mix_norm[i].reshape(1, d)
        ffn_gain = ffn_norm[i].reshape(1, d)
        if i % 2 == 0:
            w_qkv = attn_w_qkv[j].astype(BF16)
            qkv, xf = _qkv_project(xf, mix_gain, w_qkv, rope_tables, batch, seq, pending_moe)
            pending_moe = None
            xf = _attention_and_dense_ffn(qkv, xf, attn_w_o[j].astype(BF16), ffn_gain, dense_w_gate[j].astype(BF16),
                                          dense_w_up[j].astype(BF16), dense_w_down[j].astype(BF16), batch, seq)
        else:
            xf, *routing = _pool_mixer_and_router(xf, mix_gain, pool_w[j].astype(BF16), pool_scale[j].reshape(1, d),
                                                  ffn_gain, moe_router[j], batch, seq)
            pending_moe = _moe_layer(xf, routing, moe_w_gate, moe_w_up, moe_w_down, j)
    return _combine_and_final_norm(xf, *pending_moe, final_gain).reshape(batch, seq, d)
```

```python
import functools

import jax
import jax.numpy as jnp
from jax import lax
from jax.experimental import pallas as pl
from jax.experimental.pallas import tpu as pltpu
from jax.experimental.pallas import tpu_sc as plsc

WINDOWS = (128, 512, 2048)
DILATIONS = (1, 4, 16)
N_GROUPS = len(WINDOWS)
N_HEADS = 16
HEAD_DIM = 64
ROPE_THETA = 10000.0
POOL_SIZES = (2, 4, 8, 16)
N_EXPERTS = 8
TOP_K = 2
RMS_EPS = 1e-6
MASK_VALUE = -1e30
QKV_ROWS = 512
POOL_ROWS = 1024

LANES = 128
MXU_DIM = 256
VMEM_LIMIT_BYTES = 56 * 1024 * 1024

F32 = jnp.float32
BF16 = jnp.bfloat16


def _params(*sem):
    return pltpu.CompilerParams(dimension_semantics=sem, vmem_limit_bytes=VMEM_LIMIT_BYTES)


def _rms(x, g):
    ms = jnp.mean(x * x, axis=-1, keepdims=True)
    return x * lax.rsqrt(ms + RMS_EPS) * g


def _silu(g):
    return g / (1.0 + jnp.exp(-g))


def _qkv_kernel(x_ref, g_ref, w_ref, cos_ref, sin_ref, *rest, pending_moe):
    if pending_moe:
        rg_ref, y0_ref, y1_ref, o0_ref, o1_ref, o2_ref, xn_ref, acc_ref = rest
        rg = rg_ref[...]
        x = x_ref[...] + (rg[:, 0:1] * _unpack_bf16_pairs(y0_ref[...]).astype(F32)
                          + rg[:, 1:2] * _unpack_bf16_pairs(y1_ref[...]).astype(F32))
        xn_ref[...] = x
    else:
        o0_ref, o1_ref, o2_ref, acc_ref = rest
        x = x_ref[...]
    tm, d = x_ref.shape
    n_chunks = acc_ref.shape[1]
    lane = lax.broadcasted_iota(jnp.int32, (1, LANES), 1)
    first_half = (lane % HEAD_DIM) < HEAD_DIM // 2
    h = _rms(x, g_ref[...]).astype(BF16)
    for g, (o_ref, dil) in enumerate(zip((o0_ref, o1_ref, o2_ref), DILATIONS)):
        rows = tm // dil

        def strided(ref, r, dil=dil, rows=rows):
            if dil == 1:
                return ref[...]
            return ref[pl.ds(r, rows, stride=dil), :]

        for kind in range(3):
            col0 = (g * 3 + kind) * d
            acc = jnp.dot(h, w_ref[:, col0:col0 + d], preferred_element_type=F32)
            for c in range(n_chunks):
                acc_ref[kind, c] = acc[:, c * LANES:(c + 1) * LANES]
            for r in range(dil):
                if kind < 2:
                    cos = strided(cos_ref.at[kind], r)
                    sin = strided(sin_ref.at[kind], r)
                for c in range(n_chunks):
                    t = strided(acc_ref.at[kind, c], r)
                    if kind < 2:
                        partner = jnp.where(first_half, pltpu.roll(t, LANES - HEAD_DIM // 2, 1),
                                            pltpu.roll(t, HEAD_DIM // 2, 1))
                        t = t * cos + partner * sin
                    o_ref[kind, r, :, c * LANES:(c + 1) * LANES] = t.astype(BF16)


def _qkv_project(x, gain, w_qkv, tables, batch, seq, pending_moe=None):
    n, d = x.shape
    tm = QKV_ROWS // 2 if pending_moe else QKV_ROWS
    tiles_per_seq = seq // tm
    tab = pl.BlockSpec((2, tm, LANES), lambda i: (0, i % tiles_per_seq, 0))

    def o_spec(dil):
        return pl.BlockSpec((3, None, dil, tm // dil, d),
                            lambda i: (0, i // tiles_per_seq, 0, i % tiles_per_seq, 0))

    row = pl.BlockSpec((tm, d), lambda i: (i, 0))
    in_specs = [row, pl.BlockSpec((1, d), lambda i: (0, 0)),
                pl.BlockSpec(w_qkv.shape, lambda i: (0, 0), pipeline_mode=pl.Buffered(1)), tab, tab]
    args = [x, gain, w_qkv, *tables]
    out_specs = [o_spec(dil) for dil in DILATIONS]
    out_shape = [jax.ShapeDtypeStruct((3, batch, dil, seq // dil, d), BF16) for dil in DILATIONS]
    if pending_moe:
        route_g, yg = pending_moe
        in_specs += [pl.BlockSpec((tm, ROUTE_COLS), lambda i: (i, 0))]
        in_specs += [pl.BlockSpec((None, tm, d // 2), lambda i, k=k: (k, i, 0)) for k in range(TOP_K)]
        args += [route_g, yg, yg]
        out_specs.append(row)
        out_shape.append(jax.ShapeDtypeStruct((n, d), F32))
    outs = pl.pallas_call(
        functools.partial(_qkv_kernel, pending_moe=bool(pending_moe)),
        grid=(n // tm,),
        in_specs=in_specs,
        out_specs=out_specs,
        out_shape=out_shape,
        scratch_shapes=[pltpu.VMEM((3, d // LANES, tm, LANES), F32)],
        compiler_params=_params("parallel"),
        name="qkv_rope",
    )(*args)
    return [o.reshape(3, batch, seq, d) for o in outs[:N_GROUPS]], (outs[N_GROUPS] if pending_moe else x)


ATTN_TILES_PER_FFN_PIECE = 3


def _attn_dense_kernel(q0, k0, v0, q1, k1, v1, q2, k2, v2, band_ref, causal_ref,
                       x_ref, wo_ref, g_ref, wg_ref, wu_ref, wd_ref, y_ref,
                       o_scr, num_ref, max_ref, den_ref, kt_ref):
    b = pl.program_id(0)
    hp = pl.program_id(1)
    n_pairs, seq = o_scr.shape[1], o_scr.shape[2]
    w = WINDOWS[0] // DILATIONS[0]
    lane = lax.broadcasted_iota(jnp.int32, (1, LANES), 1)
    first_head = lane < HEAD_DIM
    o_out = o_scr.at[b % 2, hp]
    o_in = o_scr.at[(b + 1) % 2]

    def ffn_pieces():
        rows = pl.ds(pl.multiple_of(hp * x_ref.shape[0], x_ref.shape[0]), x_ref.shape[0])
        o = jnp.concatenate([o_in[p, rows, :] for p in range(n_pairs)], axis=1)
        x = x_ref[...] + jnp.dot(o, wo_ref[...], preferred_element_type=F32)
        h = _rms(x, g_ref[...]).astype(BF16)
        yield
        acts = []
        n_chunks = wg_ref.shape[1] // MXU_DIM
        for c in range(n_chunks):
            cols = slice(c * MXU_DIM, (c + 1) * MXU_DIM)
            gate = jnp.dot(h, wg_ref[:, cols], preferred_element_type=F32)
            up = jnp.dot(h, wu_ref[:, cols], preferred_element_type=F32)
            acts.append((_silu(gate) * up).astype(BF16))
            yield
        half = n_chunks // 2
        acc = jnp.dot(jnp.concatenate(acts[:half], axis=1), wd_ref[:half * MXU_DIM, :], preferred_element_type=F32)
        yield
        acc = acc + jnp.dot(jnp.concatenate(acts[half:], axis=1), wd_ref[half * MXU_DIM:, :],
                            preferred_element_type=F32)
        y_ref[...] = x + acc
        yield

    def tile(q_ref, v_ref, qoff, has_prev):
        q = q_ref[qoff:qoff + w, :]
        zero = jnp.zeros_like(q)
        qq = jnp.concatenate([jnp.where(first_head, q, zero), jnp.where(first_head, zero, q)], axis=0)
        k_lo = qoff - w if has_prev else qoff
        s = jnp.dot(qq, kt_ref[:, k_lo:qoff + w], preferred_element_type=F32)
        s = s + (band_ref[...] if has_prev else causal_ref[...])
        mx = jnp.max(s, axis=1, keepdims=True)
        p = jnp.exp2(s - mx)
        den = jnp.sum(p, axis=1, keepdims=True)
        pv = jnp.dot(p.astype(BF16), v_ref[k_lo:qoff + w, :], preferred_element_type=F32)
        return tuple(jnp.where(first_head, a[:w], a[w:]) for a in (pv, mx, den))

    def attention(with_ffn):
        pieces = ffn_pieces() if with_ffn else iter(())
        next(pieces, None)
        tiles_done = 0
        order = sorted(range(N_GROUPS), key=lambda g: -DILATIONS[g])
        for g in order:
            q_ref, k_ref, v_ref = ((q0, k0, v0), (q1, k1, v1), (q2, k2, v2))[g]
            dil = DILATIONS[g]
            sub_len = seq // dil
            nb = sub_len // w
            kt_ref[...] = k_ref[...].T
            for t in range(dil * nb):
                r, n = divmod(t, nb)
                qoff = r * sub_len + n * w
                pv, mx, den = tile(q_ref, v_ref, qoff, n > 0)
                tiles_done += 1
                if tiles_done % ATTN_TILES_PER_FFN_PIECE == 0:
                    next(pieces, None)
                rows = pl.ds(qoff, w) if dil == 1 else pl.ds(r + dil * n * w, w, stride=dil)
                if g == order[0]:
                    num_ref[rows, :] = pv
                    max_ref[rows, :] = mx
                    den_ref[rows, :] = den
                    continue
                old_max = max_ref[rows, :]
                new_max = jnp.maximum(old_max, mx)
                a = jnp.exp2(old_max - new_max)
                c = jnp.exp2(mx - new_max)
                num = a * num_ref[rows, :] + c * pv
                den = a * den_ref[rows, :] + c * den
                if g == order[-1]:
                    o_out[rows, :] = (num / den).astype(BF16)
                else:
                    num_ref[rows, :] = num
                    den_ref[rows, :] = den
                    max_ref[rows, :] = new_max
        for _ in pieces:
            pass

    n_batches = pl.num_programs(0) - 1
    @pl.when(b == 0)
    def _():
        y_ref[...] = jnp.zeros_like(y_ref)
        attention(False)

    pl.when((b > 0) & (b < n_batches))(functools.partial(attention, True))

    @pl.when(b == n_batches)
    def _():
        for _ in ffn_pieces():
            pass


def _attention_biases(w):
    row = jnp.arange(2 * w)[:, None] % w
    col = jnp.arange(2 * w)[None, :]
    dist = row + w - col
    band = jnp.where((dist >= 0) & (dist <= w), 0.0, MASK_VALUE).astype(F32)
    causal = jnp.where(row >= col[:, :w], 0.0, MASK_VALUE).astype(F32)
    return band, causal


def _attention_and_dense_ffn(qkv_groups, x, w_o, gain, w_gate, w_up, w_down, batch, seq):
    n, d = x.shape
    ff = w_gate.shape[1]
    n_pairs = d // LANES
    rows = seq // n_pairs
    last = batch - 1
    in_specs, args = [], []
    for arr in qkv_groups:
        for kind in range(3):
            in_specs.append(pl.BlockSpec(
                (None, None, seq, LANES),
                lambda b, hp, kind=kind: (kind, jnp.minimum(b, last), 0, jnp.where(b > last, n_pairs - 1, hp))))
            args.append(arr)
    const = lambda shape: pl.BlockSpec(shape, lambda b, hp: (0, 0), pipeline_mode=pl.Buffered(1))
    ffn_in = pl.BlockSpec((rows, d), lambda b, hp: (jnp.maximum(b - 1, 0) * n_pairs + hp, 0))
    ffn_out = pl.BlockSpec((rows, d), lambda b, hp: (jnp.where(b == 0, batch, b - 1) * n_pairs + hp, 0))
    biases = _attention_biases(WINDOWS[0] // DILATIONS[0])
    in_specs += [const(t.shape) for t in biases]
    in_specs += [ffn_in, const((d, d)), const((1, d)), const((d, ff)), const((d, ff)), const((ff, d))]
    args += [*biases, x, w_o, gain, w_gate, w_up, w_down]
    return pl.pallas_call(
        _attn_dense_kernel,
        grid=(batch + 1, n_pairs),
        in_specs=in_specs,
        out_specs=ffn_out,
        out_shape=jax.ShapeDtypeStruct((n + seq, d), F32),
        scratch_shapes=[pltpu.VMEM((2, n_pairs, seq, LANES), BF16)]
        + [pltpu.VMEM((seq, LANES), F32)] * 3 + [pltpu.VMEM((LANES, seq), BF16)],
        compiler_params=_params("arbitrary", "arbitrary"),
        name="attention_ffn",
    )(*args)


POOL_HALO = max(POOL_SIZES)


def _pool_tile(i, x_ref, halo_ref, g_ref, w_ref, sc_ref, y_ref):
    ts = x_ref.shape[0]
    g = g_ref[...]
    x = x_ref[...]
    h = _rms(x, g)
    hh = jnp.where(i > 0, _rms(halo_ref[...], g), 0.0)
    ext = jnp.concatenate([hh, h], axis=0)
    pos = i * ts + lax.broadcasted_iota(jnp.int32, (ts, 1), 0)
    pc = w_ref.shape[1]
    for grp, size in enumerate(POOL_SIZES):
        sl = slice(grp * pc, (grp + 1) * pc)
        s = ext[:, sl]
        step = 1
        while step < size:
            s = s + pltpu.roll(s, step, 0)
            step *= 2
        cnt = jnp.minimum(pos + 1, size).astype(F32)
        y = s[POOL_HALO:, :] / cnt - h[:, sl]
        z = jnp.dot(y.astype(BF16), w_ref[grp], preferred_element_type=F32)
        y_ref[:, sl] = x[:, sl] + z * sc_ref[:, sl]


def _pool_router_kernel(x_ref, halo_ref, g_ref, w_ref, sc_ref, g2_ref, wr_ref, tril_ref,
                        y_ref, ri_ref, rg_ref, cnt_ref, hp_ref, run_ref):
    b = pl.program_id(0)
    i = pl.program_id(1)
    _pool_tile(i, x_ref, halo_ref, g_ref, w_ref, sc_ref, y_ref)
    _route_tile((b == 0) & (i == 0), y_ref[...], g2_ref, wr_ref, tril_ref, ri_ref, rg_ref, cnt_ref, hp_ref, run_ref)


def _pool_mixer_and_router(x, gain, pool_w, scale, ffn_gain, w_router, batch, seq):
    d = x.shape[1]
    n = batch * seq
    ne = w_router.shape[1]
    ts = POOL_ROWS
    tiles = seq // ts
    x3 = x.reshape(-1, seq, d)
    hb = ts // POOL_HALO
    tril = (jnp.arange(ts)[:, None] > jnp.arange(ts)[None, :]).astype(BF16)
    const = lambda shape: pl.BlockSpec(shape, lambda b, i: (0,) * len(shape))
    tok = lambda cols: pl.BlockSpec((ts, cols), lambda b, i: (b * tiles + i, 0))
    out, route_i, route_g, counts, h_packed = pl.pallas_call(
        _pool_router_kernel,
        grid=(batch, tiles),
        in_specs=[
            pl.BlockSpec((None, ts, d), lambda b, i: (b, i, 0)),
            pl.BlockSpec((None, POOL_HALO, d), lambda b, i: (b, jnp.maximum(i * hb - 1, 0), 0)),
            const((1, d)), const(pool_w.shape), const((1, d)), const((1, d)), const((d, ne)), const((ts, ts)),
        ],
        out_specs=[
            pl.BlockSpec((None, ts, d), lambda b, i: (b, i, 0)),
            tok(ROUTE_COLS), tok(ROUTE_COLS), const((1, ne)), tok(d // 2),
        ],
        out_shape=[
            jax.ShapeDtypeStruct((batch, seq, d), F32),
            jax.ShapeDtypeStruct((n, ROUTE_COLS), jnp.int32),
            jax.ShapeDtypeStruct((n, ROUTE_COLS), F32),
            jax.ShapeDtypeStruct((1, ne), F32),
            jax.ShapeDtypeStruct((n, d // 2), jnp.int32),
        ],
        scratch_shapes=[pltpu.VMEM((1, ne), F32)],
        compiler_params=_params("arbitrary", "arbitrary"),
        name="pool_router",
    )(x3, x3, gain, pool_w, scale, ffn_gain, w_router, tril)
    return out.reshape(n, d), route_i, route_g, counts, h_packed


ROUTE_COLS = 8


def _pack_bf16_pairs(hr):
    c = hr.shape[1] // 2
    bits = pltpu.bitcast(hr, jnp.int32)
    return (bits[:, c:] & jnp.int32(-65536)) | lax.shift_right_logical(bits[:, :c], 16)


def _unpack_bf16_pairs(packed):
    lo = pltpu.bitcast(lax.shift_left(packed, 16), F32)
    hi = pltpu.bitcast(packed & jnp.int32(-65536), F32)
    return jnp.concatenate([lo.astype(BF16), hi.astype(BF16)], axis=1)


def _route_tile(first, x, g_ref, wr_ref, tril_ref, ri_ref, rg_ref, cnt_ref, hp_ref, run_ref):
    @pl.when(first)
    def _():
        run_ref[...] = jnp.zeros_like(run_ref)

    h = _rms(x, g_ref[...])
    wr = wr_ref[...]
    ne = wr.shape[1]
    h_hi = h.astype(BF16)
    hp_ref[...] = _pack_bf16_pairs(h_hi.astype(F32))
    h_lo = (h - h_hi.astype(F32)).astype(BF16)
    w_hi = wr.astype(BF16)
    w_lo = (wr - w_hi.astype(F32)).astype(BF16)
    hi_part = jnp.dot(h_hi, jnp.concatenate([w_hi, w_lo], axis=1), preferred_element_type=F32)
    logits = hi_part[:, :ne] + hi_part[:, ne:] + jnp.dot(h_lo, w_hi, preferred_element_type=F32)
    tm = logits.shape[0]
    lane = lax.broadcasted_iota(jnp.int32, (tm, ne), 1)
    v1 = jnp.max(logits, axis=1, keepdims=True)
    i1 = jnp.min(jnp.where(logits == v1, lane, ne), axis=1, keepdims=True)
    rest = jnp.where(lane == i1, -jnp.inf, logits)
    v2 = jnp.max(rest, axis=1, keepdims=True)
    i2 = jnp.min(jnp.where(rest == v2, lane, ne), axis=1, keepdims=True)
    e = jnp.exp(v2 - v1)
    g1 = 1.0 / (1.0 + e)
    g2 = e / (1.0 + e)
    oh1 = (lane == i1).astype(F32)
    oh2 = (lane == i2).astype(F32)
    both = oh1 + oh2
    before = jnp.dot(tril_ref[...], both.astype(BF16), preferred_element_type=F32) + run_ref[...]
    r1 = jnp.sum(before * oh1, axis=1, keepdims=True).astype(jnp.int32)
    r2 = jnp.sum(before * oh2, axis=1, keepdims=True).astype(jnp.int32)
    run_ref[...] += jnp.sum(both, axis=0, keepdims=True)
    ri_ref[...] = jnp.where(lane == 0, i1, jnp.where(lane == 1, i2, jnp.where(lane == 2, r1, r2)))
    rg_ref[...] = jnp.where(lane == 0, g1, g2)
    cnt_ref[...] = run_ref[...]


SC_WINDOW = 64


def _sc_workers():
    info = plsc.get_sparse_core_info()
    mesh = plsc.VectorSubcoreMesh(core_axis_name="core", subcore_axis_name="subcore")
    return mesh, info.num_cores, info.num_cores * info.num_subcores


def _dispatch(x, dest, cap):
    n, d = x.shape
    mesh, n_cores, n_workers = _sc_workers()
    per_worker = n // n_workers

    n_windows = per_worker // SC_WINDOW
    n_bufs = 2
    assert n_windows % n_bufs == 0

    @functools.partial(
        pl.kernel, mesh=mesh, out_type=jax.ShapeDtypeStruct((cap, d), x.dtype),
        scratch_types=[pltpu.VMEM((SC_WINDOW,), jnp.int32)] * (TOP_K * n_bufs)
        + [pltpu.VMEM((SC_WINDOW, d), x.dtype)] * n_bufs + [pltpu.SemaphoreType.DMA] * n_bufs,
        name="moe_dispatch")
    def scatter(x_hbm, *rest):
        idx_hbm, (xs_hbm, *scratch) = rest[:TOP_K], rest[TOP_K:]
        idx_vmem = [scratch[b * TOP_K:(b + 1) * TOP_K] for b in range(n_bufs)]
        rows = scratch[TOP_K * n_bufs:TOP_K * n_bufs + n_bufs]
        sems = scratch[TOP_K * n_bufs + n_bufs:]
        wid = lax.axis_index("subcore") * n_cores + lax.axis_index("core")

        def drain(buf):
            for k in range(TOP_K):
                pltpu.make_async_copy(rows[buf], xs_hbm.at[idx_vmem[buf][k]], sems[buf]).wait()

        @pl.loop(0, n_windows, step=n_bufs)
        def _(c):
            for buf in range(n_bufs):
                @pl.when(c > 0)
                def _(buf=buf):
                    drain(buf)

                window = pl.ds(wid * per_worker + (c + buf) * SC_WINDOW, SC_WINDOW)
                for k in range(TOP_K):
                    pltpu.sync_copy(idx_hbm[k].at[window], idx_vmem[buf][k])
                pltpu.sync_copy(x_hbm.at[window], rows[buf])
                for k in range(TOP_K):
                    pltpu.async_copy(rows[buf], xs_hbm.at[idx_vmem[buf][k]], sems[buf])

        for buf in range(n_bufs):
            drain(buf)

    return scatter(x, *[dest[k] for k in range(TOP_K)])


def _gather_rows(table, idx):
    m = idx.shape[0]
    d = table.shape[1]
    mesh, n_cores, n_workers = _sc_workers()
    per_worker = m // n_workers

    n_windows = per_worker // SC_WINDOW
    n_bufs = 2
    assert n_windows % n_bufs == 0

    @functools.partial(
        pl.kernel, mesh=mesh, out_type=jax.ShapeDtypeStruct((m, d), table.dtype),
        scratch_types=[pltpu.VMEM((SC_WINDOW,), jnp.int32)] * n_bufs
        + [pltpu.VMEM((SC_WINDOW, d), table.dtype)] * n_bufs + [pltpu.SemaphoreType.DMA] * n_bufs,
        name="moe_gather")
    def gather(t_hbm, i_hbm, o_hbm, *scratch):
        idx_vmem, rows, sems = scratch[:n_bufs], scratch[n_bufs:2 * n_bufs], scratch[2 * n_bufs:]
        wid = lax.axis_index("subcore") * n_cores + lax.axis_index("core")

        def window(c):
            return pl.ds(wid * per_worker + c * SC_WINDOW, SC_WINDOW)

        def fetch(c, buf):
            pltpu.sync_copy(i_hbm.at[window(c)], idx_vmem[buf])
            return pltpu.async_copy(t_hbm.at[idx_vmem[buf]], rows[buf], sems[buf])

        fetch(0, 0)

        @pl.loop(0, n_windows, step=n_bufs)
        def _(c):
            for buf in range(n_bufs):
                nxt = c + buf + 1

                @pl.when(nxt < n_windows)
                def _(nxt=nxt, buf=buf):
                    fetch(nxt, (buf + 1) % n_bufs)

                pltpu.make_async_copy(t_hbm.at[idx_vmem[buf]], rows[buf], sems[buf]).wait()
                pltpu.sync_copy(rows[buf], o_hbm.at[window(c + buf)])

    return gather(table, idx)


def _expert_kernel(be_ref, nv_ref, xs_ref, wg_ref, wu_ref, wd_ref, ys_ref, xb_ref, acc_ref):
    del be_ref
    b = pl.program_id(0)
    j = pl.program_id(1)

    def ff_tile(first):
        x = xb_ref[...]
        acts = []
        for c in range(wg_ref.shape[1] // MXU_DIM):
            cols = slice(c * MXU_DIM, (c + 1) * MXU_DIM)
            gate = jnp.dot(x, wg_ref[:, cols].astype(BF16), preferred_element_type=F32)
            up = jnp.dot(x, wu_ref[:, cols].astype(BF16), preferred_element_type=F32)
            acts.append((_silu(gate) * up).astype(BF16))
        part = jnp.dot(jnp.concatenate(acts, axis=1), wd_ref[...].astype(BF16), preferred_element_type=F32)
        if first:
            acc_ref[...] = part
        else:
            acc_ref[...] += part

    valid = b < nv_ref[0]

    @pl.when(valid & (j == 0))
    def _():
        xb_ref[...] = _unpack_bf16_pairs(xs_ref[...])
        ff_tile(True)

    @pl.when(valid & (j > 0))
    def _():
        ff_tile(False)

    @pl.when(jnp.logical_not(valid) & (j == 0))
    def _():
        acc_ref[...] = jnp.zeros_like(acc_ref)

    @pl.when(j == pl.num_programs(1) - 1)
    def _():
        ys_ref[...] = _pack_bf16_pairs(acc_ref[...].astype(BF16).astype(F32))


def _experts(xs, block_e, n_valid, w_gate, w_up, w_down, layer, tm, tf):
    cap = xs.shape[0]
    d = 2 * xs.shape[1]
    ff = w_gate.shape[3]
    nj = ff // tf

    def blk(b, nv):
        return jnp.minimum(b, nv[0] - 1)

    def ffi(b, j, nv):
        return jnp.where(b < nv[0], j, nj - 1)

    return pl.pallas_call(
        _expert_kernel,
        grid_spec=pltpu.PrefetchScalarGridSpec(
            num_scalar_prefetch=2,
            grid=(cap // tm, nj),
            in_specs=[
                pl.BlockSpec((tm, d // 2), lambda b, j, be, nv: (blk(b, nv), 0)),
                pl.BlockSpec((None, None, d, tf), lambda b, j, be, nv: (layer, be[blk(b, nv)], 0, ffi(b, j, nv))),
                pl.BlockSpec((None, None, d, tf), lambda b, j, be, nv: (layer, be[blk(b, nv)], 0, ffi(b, j, nv))),
                pl.BlockSpec((None, None, tf, d), lambda b, j, be, nv: (layer, be[blk(b, nv)], ffi(b, j, nv), 0)),
            ],
            out_specs=pl.BlockSpec((tm, d // 2), lambda b, j, be, nv: (b, 0)),
            scratch_shapes=[pltpu.VMEM((tm, d), BF16), pltpu.VMEM((tm, d), F32)],
        ),
        out_shape=jax.ShapeDtypeStruct((cap, d // 2), jnp.int32),
        compiler_params=_params("arbitrary", "arbitrary"),
        name="moe_experts",
    )(block_e, n_valid, xs, w_gate, w_up, w_down)


def _combine_kernel(x_ref, rg_ref, gf_ref, y0_ref, y1_ref, o_ref):
    rg = rg_ref[...]
    y0 = _unpack_bf16_pairs(y0_ref[...]).astype(F32)
    y1 = _unpack_bf16_pairs(y1_ref[...]).astype(F32)
    o_ref[...] = _rms(x_ref[...] + (rg[:, 0:1] * y0 + rg[:, 1:2] * y1), gf_ref[...])


def _combine_and_final_norm(x, route_g, yg, final_gain):
    n, d = x.shape
    tm = 1024
    return pl.pallas_call(
        _combine_kernel,
        grid=(n // tm,),
        in_specs=[
            pl.BlockSpec((tm, d), lambda i: (i, 0)),
            pl.BlockSpec((tm, ROUTE_COLS), lambda i: (i, 0)),
            pl.BlockSpec((1, d), lambda i: (0, 0)),
            pl.BlockSpec((None, tm, d // 2), lambda i: (0, i, 0)),
            pl.BlockSpec((None, tm, d // 2), lambda i: (1, i, 0)),
        ],
        out_specs=pl.BlockSpec((tm, d), lambda i: (i, 0)),
        out_shape=jax.ShapeDtypeStruct((n, d), F32),
        compiler_params=_params("parallel"),
        name="moe_combine",
    )(x, route_g, final_gain, yg, yg)


def _moe_layer(x, routing, w_gate, w_up, w_down, layer):
    n, d = x.shape
    tm = 1024 if n * TOP_K >= 8 * 1024 else 256
    tf = 2 * MXU_DIM
    route_i, route_g, counts, h_packed = routing

    counts = counts[0].astype(jnp.int32)
    padded = ((counts + tm - 1) // tm) * tm
    ends = jnp.cumsum(padded)
    starts = ends - padded
    dest = (starts[route_i[:, :TOP_K]] + route_i[:, TOP_K:2 * TOP_K]).T
    n_blocks = (n * TOP_K) // tm + N_EXPERTS
    block_start = jnp.arange(n_blocks, dtype=jnp.int32) * tm
    block_e = jnp.minimum(jnp.sum(ends[None, :] <= block_start[:, None], axis=1), N_EXPERTS - 1).astype(jnp.int32)
    n_valid = (ends[-1:] // tm).astype(jnp.int32)

    xs = _dispatch(h_packed, dest, n_blocks * tm)
    ys = _experts(xs, block_e, n_valid, w_gate, w_up, w_down, layer, tm, tf)
    return route_g, _gather_rows(ys, dest.reshape(TOP_K * n)).reshape(TOP_K, n, d // 2)


def _rope_tables(seq):
    pos = jnp.arange(seq, dtype=F32)
    inv_freq = ROPE_THETA ** (-jnp.arange(0, HEAD_DIM, 2, dtype=F32) / HEAD_DIM)
    ang = pos[:, None] * inv_freq[None, :]
    reps = LANES // (HEAD_DIM // 2)
    cos = jnp.tile(jnp.cos(ang), (1, reps))
    sin = jnp.tile(jnp.sin(ang), (1, reps))
    first_half = (jnp.arange(LANES) % HEAD_DIM) < HEAD_DIM // 2
    tabs = (cos, jnp.where(first_half, -sin, sin))
    q_scale = HEAD_DIM ** -0.5 * 1.4426950408889634
    return tuple(jnp.stack([t * q_scale, t]) for t in tabs)


def kernel(x, mix_norm, ffn_norm, attn_w_qkv, attn_w_o, pool_w, pool_scale, dense_w_gate, dense_w_up,
           dense_w_down, moe_router, moe_w_gate, moe_w_up, moe_w_down, final_norm):
    batch, seq, d = x.shape
    depth = mix_norm.shape[0]
    assert depth % 2 == 0 and d == N_HEADS * HEAD_DIM
    assert all(seq % (dil * (win // dil)) == 0 for win, dil in zip(WINDOWS, DILATIONS))
    assert seq % QKV_ROWS == 0 and seq % POOL_ROWS == 0 and moe_router.shape[2] == N_EXPERTS
    rope_tables = _rope_tables(seq)
    xf = x.reshape(batch * seq, d)
    final_gain = final_norm.reshape(1, d)
    pending_moe = None
    for i in range(depth):
        j = i // 2
        mix_gain = mix_norm[i].reshape(1, d)
        ffn_gain = ffn_norm[i].reshape(1, d)
        if i % 2 == 0:
            w_qkv = attn_w_qkv[j].astype(BF16)
            qkv, xf = _qkv_project(xf, mix_gain, w_qkv, rope_tables, batch, seq, pending_moe)
            pending_moe = None
            xf = _attention_and_dense_ffn(qkv, xf, attn_w_o[j].astype(BF16), ffn_gain, dense_w_gate[j].astype(BF16),
                                          dense_w_up[j].astype(BF16), dense_w_down[j].astype(BF16), batch, seq)
        else:
            xf, *routing = _pool_mixer_and_router(xf, mix_gain, pool_w[j].astype(BF16), pool_scale[j].reshape(1, d),
                                                  ffn_gain, moe_router[j], batch, seq)
            pending_moe = _moe_layer(xf, routing, moe_w_gate, moe_w_up, moe_w_down, j)
    return _combine_and_final_norm(xf, *pending_moe, final_gain).reshape(batch, seq, d)
```

```python
import functools

import jax
import jax.numpy as jnp
from jax import lax
from jax.experimental import pallas as pl
from jax.experimental.pallas import tpu as pltpu
from jax.experimental.pallas import tpu_sc as plsc

WINDOWS = (128, 512, 2048)
DILATIONS = (1, 4, 16)
N_GROUPS = len(WINDOWS)
N_HEADS = 16
HEAD_DIM = 64
ROPE_THETA = 10000.0
POOL_SIZES = (2, 4, 8, 16)
N_EXPERTS = 8
TOP_K = 2
RMS_EPS = 1e-6
MASK_VALUE = -1e30
QKV_ROWS = 512
POOL_ROWS = 1024

LANES = 128
MXU_DIM = 256
VMEM_LIMIT_BYTES = 56 * 1024 * 1024

F32 = jnp.float32
BF16 = jnp.bfloat16


def _params(*sem):
    return pltpu.CompilerParams(dimension_semantics=sem, vmem_limit_bytes=VMEM_LIMIT_BYTES)


def _rms(x, g):
    ms = jnp.mean(x * x, axis=-1, keepdims=True)
    return x * lax.rsqrt(ms + RMS_EPS) * g


def _silu(g):
    return g / (1.0 + jnp.exp(-g))


def _qkv_kernel(x_ref, g_ref, w_ref, cos_ref, sin_ref, *rest, pending_moe):
    if pending_moe:
        rg_ref, y0_ref, y1_ref, o0_ref, o1_ref, o2_ref, xn_ref, acc_ref = rest
        rg = rg_ref[...]
        x = x_ref[...] + (rg[:, 0:1] * _unpack_bf16_pairs(y0_ref[...]).astype(F32)
                          + rg[:, 1:2] * _unpack_bf16_pairs(y1_ref[...]).astype(F32))
        xn_ref[...] = x
    else:
        o0_ref, o1_ref, o2_ref, acc_ref = rest
        x = x_ref[...]
    tm, d = x_ref.shape
    n_chunks = acc_ref.shape[1]
    lane = lax.broadcasted_iota(jnp.int32, (1, LANES), 1)
    first_half = (lane % HEAD_DIM) < HEAD_DIM // 2
    h = _rms(x, g_ref[...]).astype(BF16)
    for g, (o_ref, dil) in enumerate(zip((o0_ref, o1_ref, o2_ref), DILATIONS)):
        rows = tm // dil

        def strided(ref, r, dil=dil, rows=rows):
            if dil == 1:
                return ref[...]
            return ref[pl.ds(r, rows, stride=dil), :]

        for kind in range(3):
            col0 = (g * 3 + kind) * d
            acc = jnp.dot(h, w_ref[:, col0:col0 + d], preferred_element_type=F32)
            for c in range(n_chunks):
                acc_ref[kind, c] = acc[:, c * LANES:(c + 1) * LANES]
            for r in range(dil):
                if kind < 2:
                    cos = strided(cos_ref.at[kind], r)
                    sin = strided(sin_ref.at[kind], r)
                for c in range(n_chunks):
                    t = strided(acc_ref.at[kind, c], r)
                    if kind < 2:
                        partner = jnp.where(first_half, pltpu.roll(t, LANES - HEAD_DIM // 2, 1),
                                            pltpu.roll(t, HEAD_DIM // 2, 1))
                        t = t * cos + partner * sin
                    o_ref[kind, r, :, c * LANES:(c + 1) * LANES] = t.astype(BF16)


def _qkv_project(x, gain, w_qkv, tables, batch, seq, pending_moe=None):
    n, d = x.shape
    tm = QKV_ROWS // 2 if pending_moe else QKV_ROWS
    tiles_per_seq = seq // tm
    tab = pl.BlockSpec((2, tm, LANES), lambda i: (0, i % tiles_per_seq, 0))

    def o_spec(dil):
        return pl.BlockSpec((3, None, dil, tm // dil, d),
                            lambda i: (0, i // tiles_per_seq, 0, i % tiles_per_seq, 0))

    row = pl.BlockSpec((tm, d), lambda i: (i, 0))
    in_specs = [row, pl.BlockSpec((1, d), lambda i: (0, 0)),
                pl.BlockSpec(w_qkv.shape, lambda i: (0, 0), pipeline_mode=pl.Buffered(1)), tab, tab]
    args = [x, gain, w_qkv, *tables]
    out_specs = [o_spec(dil) for dil in DILATIONS]
    out_shape = [jax.ShapeDtypeStruct((3, batch, dil, seq // dil, d), BF16) for dil in DILATIONS]
    if pending_moe:
        route_g, yg = pending_moe
        in_specs += [pl.BlockSpec((tm, ROUTE_COLS), lambda i: (i, 0))]
        in_specs += [pl.BlockSpec((None, tm, d // 2), lambda i, k=k: (k, i, 0)) for k in range(TOP_K)]
        args += [route_g, yg, yg]
        out_specs.append(row)
        out_shape.append(jax.ShapeDtypeStruct((n, d), F32))
    outs = pl.pallas_call(
        functools.partial(_qkv_kernel, pending_moe=bool(pending_moe)),
        grid=(n // tm,),
        in_specs=in_specs,
        out_specs=out_specs,
        out_shape=out_shape,
        scratch_shapes=[pltpu.VMEM((3, d // LANES, tm, LANES), F32)],
        compiler_params=_params("parallel"),
        name="qkv_rope",
    )(*args)
    return [o.reshape(3, batch, seq, d) for o in outs[:N_GROUPS]], (outs[N_GROUPS] if pending_moe else x)


ATTN_TILES_PER_FFN_PIECE = 3


def _attn_dense_kernel(q0, k0, v0, q1, k1, v1, q2, k2, v2, band_ref, causal_ref,
                       x_ref, wo_ref, g_ref, wg_ref, wu_ref, wd_ref, y_ref,
                       o_scr, num_ref, max_ref, den_ref, kt_ref):
    b = pl.program_id(0)
    hp = pl.program_id(1)
    n_pairs, seq = o_scr.shape[1], o_scr.shape[2]
    w = WINDOWS[0] // DILATIONS[0]
    lane = lax.broadcasted_iota(jnp.int32, (1, LANES), 1)
    first_head = lane < HEAD_DIM
    o_out = o_scr.at[b % 2, hp]
    o_in = o_scr.at[(b + 1) % 2]

    def ffn_pieces():
        rows = pl.ds(pl.multiple_of(hp * x_ref.shape[0], x_ref.shape[0]), x_ref.shape[0])
        o = jnp.concatenate([o_in[p, rows, :] for p in range(n_pairs)], axis=1)
        x = x_ref[...] + jnp.dot(o, wo_ref[...], preferred_element_type=F32)
        h = _rms(x, g_ref[...]).astype(BF16)
        yield
        acts = []
        n_chunks = wg_ref.shape[1] // MXU_DIM
        for c in range(n_chunks):
            cols = slice(c * MXU_DIM, (c + 1) * MXU_DIM)
            gate = jnp.dot(h, wg_ref[:, cols], preferred_element_type=F32)
            up = jnp.dot(h, wu_ref[:, cols], preferred_element_type=F32)
            acts.append((_silu(gate) * up).astype(BF16))
            yield
        half = n_chunks // 2
        acc = jnp.dot(jnp.concatenate(acts[:half], axis=1), wd_ref[:half * MXU_DIM, :], preferred_element_type=F32)
        yield
        acc = acc + jnp.dot(jnp.concatenate(acts[half:], axis=1), wd_ref[half * MXU_DIM:, :],
                            preferred_element_type=F32)
        y_ref[...] = x + acc
        yield

    def tile(q_ref, v_ref, qoff, has_prev):
        q = q_ref[qoff:qoff + w, :]
        zero = jnp.zeros_like(q)
        qq = jnp.concatenate([jnp.where(first_head, q, zero), jnp.where(first_head, zero, q)], axis=0)
        k_lo = qoff - w if has_prev else qoff
        s = jnp.dot(qq, kt_ref[:, k_lo:qoff + w], preferred_element_type=F32)
        s = s + (band_ref[...] if has_prev else causal_ref[...])
        mx = jnp.max(s, axis=1, keepdims=True)
        p = jnp.exp2(s - mx)
        den = jnp.sum(p, axis=1, keepdims=True)
        pv = jnp.dot(p.astype(BF16), v_ref[k_lo:qoff + w, :], preferred_element_type=F32)
        return tuple(jnp.where(first_head, a[:w], a[w:]) for a in (pv, mx, den))

    def attention(with_ffn):
        pieces = ffn_pieces() if with_ffn else iter(())
        next(pieces, None)
        tiles_done = 0
        order = sorted(range(N_GROUPS), key=lambda g: -DILATIONS[g])
        for g in order:
            q_ref, k_ref, v_ref = ((q0, k0, v0), (q1, k1, v1), (q2, k2, v2))[g]
            dil = DILATIONS[g]
            sub_len = seq // dil
            nb = sub_len // w
            kt_ref[...] = k_ref[...].T
            for t in range(dil * nb):
                r, n = divmod(t, nb)
                qoff = r * sub_len + n * w
                pv, mx, den = tile(q_ref, v_ref, qoff, n > 0)
                tiles_done += 1
                if tiles_done % ATTN_TILES_PER_FFN_PIECE == 0:
                    next(pieces, None)
                rows = pl.ds(qoff, w) if dil == 1 else pl.ds(r + dil * n * w, w, stride=dil)
                if g == order[0]:
                    num_ref[rows, :] = pv
                    max_ref[rows, :] = mx
                    den_ref[rows, :] = den
                    continue
                old_max = max_ref[rows, :]
                new_max = jnp.maximum(old_max, mx)
                a = jnp.exp2(old_max - new_max)
                c = jnp.exp2(mx - new_max)
                num = a * num_ref[rows, :] + c * pv
                den = a * den_ref[rows, :] + c * den
                if g == order[-1]:
                    o_out[rows, :] = (num / den).astype(BF16)
                else:
                    num_ref[rows, :] = num
                    den_ref[rows, :] = den
                    max_ref[rows, :] = new_max
        for _ in pieces:
            pass

    n_batches = pl.num_programs(0) - 1
    @pl.when(b == 0)
    def _():
        y_ref[...] = jnp.zeros_like(y_ref)
        attention(False)

    pl.when((b > 0) & (b < n_batches))(functools.partial(attention, True))

    @pl.when(b == n_batches)
    def _():
        for _ in ffn_pieces():
            pass


def _attention_biases(w):
    row = jnp.arange(2 * w)[:, None] % w
    col = jnp.arange(2 * w)[None, :]
    dist = row + w - col
    band = jnp.where((dist >= 0) & (dist <= w), 0.0, MASK_VALUE).astype(F32)
    causal = jnp.where(row >= col[:, :w], 0.0, MASK_VALUE).astype(F32)
    return band, causal


def _attention_and_dense_ffn(qkv_groups, x, w_o, gain, w_gate, w_up, w_down, batch, seq):
    n, d = x.shape
    ff = w_gate.shape[1]
    n_pairs = d // LANES
    rows = seq // n_pairs
    last = batch - 1
    in_specs, args = [], []
    for arr in qkv_groups:
        for kind in range(3):
            in_specs.append(pl.BlockSpec(
                (None, None, seq, LANES),
                lambda b, hp, kind=kind: (kind, jnp.minimum(b, last), 0, jnp.where(b > last, n_pairs - 1, hp))))
            args.append(arr)
    const = lambda shape: pl.BlockSpec(shape, lambda b, hp: (0, 0), pipeline_mode=pl.Buffered(1))
    ffn_in = pl.BlockSpec((rows, d), lambda b, hp: (jnp.maximum(b - 1, 0) * n_pairs + hp, 0))
    ffn_out = pl.BlockSpec((rows, d), lambda b, hp: (jnp.where(b == 0, batch, b - 1) * n_pairs + hp, 0))
    biases = _attention_biases(WINDOWS[0] // DILATIONS[0])
    in_specs += [const(t.shape) for t in biases]
    in_specs += [ffn_in, const((d, d)), const((1, d)), const((d, ff)), const((d, ff)), const((ff, d))]
    args += [*biases, x, w_o, gain, w_gate, w_up, w_down]
    return pl.pallas_call(
        _attn_dense_kernel,
        grid=(batch + 1, n_pairs),
        in_specs=in_specs,
        out_specs=ffn_out,
        out_shape=jax.ShapeDtypeStruct((n + seq, d), F32),
        scratch_shapes=[pltpu.VMEM((2, n_pairs, seq, LANES), BF16)]
        + [pltpu.VMEM((seq, LANES), F32)] * 3 + [pltpu.VMEM((LANES, seq), BF16)],
        compiler_params=_params("arbitrary", "arbitrary"),
        name="attention_ffn",
    )(*args)


POOL_HALO = max(POOL_SIZES)


def _pool_tile(i, x_ref, halo_ref, g_ref, w_ref, sc_ref, y_ref):
    ts = x_ref.shape[0]
    g = g_ref[...]
    x = x_ref[...]
    h = _rms(x, g)
    hh = jnp.where(i > 0, _rms(halo_ref[...], g), 0.0)
    ext = jnp.concatenate([hh, h], axis=0)
    pos = i * ts + lax.broadcasted_iota(jnp.int32, (ts, 1), 0)
    pc = w_ref.shape[1]
    for grp, size in enumerate(POOL_SIZES):
        sl = slice(grp * pc, (grp + 1) * pc)
        s = ext[:, sl]
        step = 1
        while step < size:
            s = s + pltpu.roll(s, step, 0)
            step *= 2
        cnt = jnp.minimum(pos + 1, size).astype(F32)
        y = s[POOL_HALO:, :] / cnt - h[:, sl]
        z = jnp.dot(y.astype(BF16), w_ref[grp], preferred_element_type=F32)
        y_ref[:, sl] = x[:, sl] + z * sc_ref[:, sl]


def _pool_router_kernel(x_ref, halo_ref, g_ref, w_ref, sc_ref, g2_ref, wr_ref, tril_ref,
                        y_ref, ri_ref, rg_ref, cnt_ref, hp_ref, run_ref):
    b = pl.program_id(0)
    i = pl.program_id(1)
    _pool_tile(i, x_ref, halo_ref, g_ref, w_ref, sc_ref, y_ref)
    _route_tile((b == 0) & (i == 0), y_ref[...], g2_ref, wr_ref, tril_ref, ri_ref, rg_ref, cnt_ref, hp_ref, run_ref)


def _pool_mixer_and_router(x, gain, pool_w, scale, ffn_gain, w_router, batch, seq):
    d = x.shape[1]
    n = batch * seq
    ne = w_router.shape[1]
    ts = POOL_ROWS
    tiles = seq // ts
    x3 = x.reshape(-1, seq, d)
    hb = ts // POOL_HALO
    tril = (jnp.arange(ts)[:, None] > jnp.arange(ts)[None, :]).astype(BF16)
    const = lambda shape: pl.BlockSpec(shape, lambda b, i: (0,) * len(shape))
    tok = lambda cols: pl.BlockSpec((ts, cols), lambda b, i: (b * tiles + i, 0))
    out, route_i, route_g, counts, h_packed = pl.pallas_call(
        _pool_router_kernel,
        grid=(batch, tiles),
        in_specs=[
            pl.BlockSpec((None, ts, d), lambda b, i: (b, i, 0)),
            pl.BlockSpec((None, POOL_HALO, d), lambda b, i: (b, jnp.maximum(i * hb - 1, 0), 0)),
            const((1, d)), const(pool_w.shape), const((1, d)), const((1, d)), const((d, ne)), const((ts, ts)),
        ],
        out_specs=[
            pl.BlockSpec((None, ts, d), lambda b, i: (b, i, 0)),
            tok(ROUTE_COLS), tok(ROUTE_COLS), const((1, ne)), tok(d // 2),
        ],
        out_shape=[
            jax.ShapeDtypeStruct((batch, seq, d), F32),
            jax.ShapeDtypeStruct((n, ROUTE_COLS), jnp.int32),
            jax.ShapeDtypeStruct((n, ROUTE_COLS), F32),
            jax.ShapeDtypeStruct((1, ne), F32),
            jax.ShapeDtypeStruct((n, d // 2), jnp.int32),
        ],
        scratch_shapes=[pltpu.VMEM((1, ne), F32)],
        compiler_params=_params("arbitrary", "arbitrary"),
        name="pool_router",
    )(x3, x3, gain, pool_w, scale, ffn_gain, w_router, tril)
    return out.reshape(n, d), route_i, route_g, counts, h_packed


ROUTE_COLS = 8


def _pack_bf16_pairs(hr):
    c = hr.shape[1] // 2
    bits = pltpu.bitcast(hr, jnp.int32)
    return (bits[:, c:] & jnp.int32(-65536)) | lax.shift_right_logical(bits[:, :c], 16)


def _unpack_bf16_pairs(packed):
    lo = pltpu.bitcast(lax.shift_left(packed, 16), F32)
    hi = pltpu.bitcast(packed & jnp.int32(-65536), F32)
    return jnp.concatenate([lo.astype(BF16), hi.astype(BF16)], axis=1)


def _route_tile(first, x, g_ref, wr_ref, tril_ref, ri_ref, rg_ref, cnt_ref, hp_ref, run_ref):
    @pl.when(first)
    def _():
        run_ref[...] = jnp.zeros_like(run_ref)

    h = _rms(x, g_ref[...])
    wr = wr_ref[...]
    ne = wr.shape[1]
    h_hi = h.astype(BF16)
    hp_ref[...] = _pack_bf16_pairs(h_hi.astype(F32))
    h_lo = (h - h_hi.astype(F32)).astype(BF16)
    w_hi = wr.astype(BF16)
    w_lo = (wr - w_hi.astype(F32)).astype(BF16)
    hi_part = jnp.dot(h_hi, jnp.concatenate([w_hi, w_lo], axis=1), preferred_element_type=F32)
    logits = hi_part[:, :ne] + hi_part[:, ne:] + jnp.dot(h_lo, w_hi, preferred_element_type=F32)
    tm = logits.shape[0]
    lane = lax.broadcasted_iota(jnp.int32, (tm, ne), 1)
    v1 = jnp.max(logits, axis=1, keepdims=True)
    i1 = jnp.min(jnp.where(logits == v1, lane, ne), axis=1, keepdims=True)
    rest = jnp.where(lane == i1, -jnp.inf, logits)
    v2 = jnp.max(rest, axis=1, keepdims=True)
    i2 = jnp.min(jnp.where(rest == v2, lane, ne), axis=1, keepdims=True)
    e = jnp.exp(v2 - v1)
    g1 = 1.0 / (1.0 + e)
    g2 = e / (1.0 + e)
    oh1 = (lane == i1).astype(F32)
    oh2 = (lane == i2).astype(F32)
    both = oh1 + oh2
    before = jnp.dot(tril_ref[...], both.astype(BF16), preferred_element_type=F32) + run_ref[...]
    r1 = jnp.sum(before * oh1, axis=1, keepdims=True).astype(jnp.int32)
    r2 = jnp.sum(before * oh2, axis=1, keepdims=True).astype(jnp.int32)
    run_ref[...] += jnp.sum(both, axis=0, keepdims=True)
    ri_ref[...] = jnp.where(lane == 0, i1, jnp.where(lane == 1, i2, jnp.where(lane == 2, r1, r2)))
    rg_ref[...] = jnp.where(lane == 0, g1, g2)
    cnt_ref[...] = run_ref[...]


SC_WINDOW = 64


def _sc_workers():
    info = plsc.get_sparse_core_info()
    mesh = plsc.VectorSubcoreMesh(core_axis_name="core", subcore_axis_name="subcore")
    return mesh, info.num_cores, info.num_cores * info.num_subcores


def _dispatch(x, dest, cap):
    n, d = x.shape
    mesh, n_cores, n_workers = _sc_workers()
    per_worker = n // n_workers

    n_windows = per_worker // SC_WINDOW
    n_bufs = 2
    assert n_windows % n_bufs == 0

    @functools.partial(
        pl.kernel, mesh=mesh, out_type=jax.ShapeDtypeStruct((cap, d), x.dtype),
        scratch_types=[pltpu.VMEM((SC_WINDOW,), jnp.int32)] * (TOP_K * n_bufs)
        + [pltpu.VMEM((SC_WINDOW, d), x.dtype)] * n_bufs + [pltpu.SemaphoreType.DMA] * n_bufs,
        name="moe_dispatch")
    def scatter(x_hbm, *rest):
        idx_hbm, (xs_hbm, *scratch) = rest[:TOP_K], rest[TOP_K:]
        idx_vmem = [scratch[b * TOP_K:(b + 1) * TOP_K] for b in range(n_bufs)]
        rows = scratch[TOP_K * n_bufs:TOP_K * n_bufs + n_bufs]
        sems = scratch[TOP_K * n_bufs + n_bufs:]
        wid = lax.axis_index("subcore") * n_cores + lax.axis_index("core")

        def drain(buf):
            for k in range(TOP_K):
                pltpu.make_async_copy(rows[buf], xs_hbm.at[idx_vmem[buf][k]], sems[buf]).wait()

        @pl.loop(0, n_windows, step=n_bufs)
        def _(c):
            for buf in range(n_bufs):
                @pl.when(c > 0)
                def _(buf=buf):
                    drain(buf)

                window = pl.ds(wid * per_worker + (c + buf) * SC_WINDOW, SC_WINDOW)
                for k in range(TOP_K):
                    pltpu.sync_copy(idx_hbm[k].at[window], idx_vmem[buf][k])
                pltpu.sync_copy(x_hbm.at[window], rows[buf])
                for k in range(TOP_K):
                    pltpu.async_copy(rows[buf], xs_hbm.at[idx_vmem[buf][k]], sems[buf])

        for buf in range(n_bufs):
            drain(buf)

    return scatter(x, *[dest[k] for k in range(TOP_K)])


def _gather_rows(table, idx):
    m = idx.shape[0]
    d = table.shape[1]
    mesh, n_cores, n_workers = _sc_workers()
    per_worker = m // n_workers

    n_windows = per_worker // SC_WINDOW
    n_bufs = 2
    assert n_windows % n_bufs == 0

    @functools.partial(
        pl.kernel, mesh=mesh, out_type=jax.ShapeDtypeStruct((m, d), table.dtype),
        scratch_types=[pltpu.VMEM((SC_WINDOW,), jnp.int32)] * n_bufs
        + [pltpu.VMEM((SC_WINDOW, d), table.dtype)] * n_bufs + [pltpu.SemaphoreType.DMA] * n_bufs,
        name="moe_gather")
    def gather(t_hbm, i_hbm, o_hbm, *scratch):
        idx_vmem, rows, sems = scratch[:n_bufs], scratch[n_bufs:2 * n_bufs], scratch[2 * n_bufs:]
        wid = lax.axis_index("subcore") * n_cores + lax.axis_index("core")

        def window(c):
            return pl.ds(wid * per_worker + c * SC_WINDOW, SC_WINDOW)

        def fetch(c, buf):
            pltpu.sync_copy(i_hbm.at[window(c)], idx_vmem[buf])
            return pltpu.async_copy(t_hbm.at[idx_vmem[buf]], rows[buf], sems[buf])

        fetch(0, 0)

        @pl.loop(0, n_windows, step=n_bufs)
        def _(c):
            for buf in range(n_bufs):
                nxt = c + buf + 1

                @pl.when(nxt < n_windows)
                def _(nxt=nxt, buf=buf):
                    fetch(nxt, (buf + 1) % n_bufs)

                pltpu.make_async_copy(t_hbm.at[idx_vmem[buf]], rows[buf], sems[buf]).wait()
                pltpu.sync_copy(rows[buf], o_hbm.at[window(c + buf)])

    return gather(table, idx)


def _expert_kernel(be_ref, nv_ref, xs_ref, *rest, n_tiles):
    del be_ref
    weights, (ys_ref, xb_ref, acc_ref) = (rest[0:3], rest[3:6]), rest[6:]
    b = pl.program_id(0)
    j = pl.program_id(1)
    n_steps = pl.num_programs(1)

    def ff_tile(first, wg_ref, wu_ref, wd_ref):
        x = xb_ref[...]
        acts = []
        for c in range(wg_ref.shape[1] // MXU_DIM):
            cols = slice(c * MXU_DIM, (c + 1) * MXU_DIM)
            gate = jnp.dot(x, wg_ref[:, cols].astype(BF16), preferred_element_type=F32)
            up = jnp.dot(x, wu_ref[:, cols].astype(BF16), preferred_element_type=F32)
            acts.append((_silu(gate) * up).astype(BF16))
        part = jnp.dot(jnp.concatenate(acts, axis=1), wd_ref[...].astype(BF16), preferred_element_type=F32)
        if first:
            acc_ref[...] = part
        else:
            acc_ref[...] += part

    valid = b < nv_ref[0]
    odd_tail = n_tiles % 2 == 1

    @pl.when(valid & (j == 0))
    def _():
        xb_ref[...] = _unpack_bf16_pairs(xs_ref[...])
        ff_tile(True, *weights[0])
        ff_tile(False, *weights[1])

    @pl.when(valid & (j > 0) & ((j < n_steps - 1) | (not odd_tail)))
    def _():
        ff_tile(False, *weights[0])
        ff_tile(False, *weights[1])

    if odd_tail:
        @pl.when(valid & (j == n_steps - 1))
        def _():
            ff_tile(False, *weights[0])

    @pl.when(jnp.logical_not(valid) & (j == 0))
    def _():
        acc_ref[...] = jnp.zeros_like(acc_ref)

    @pl.when(j == pl.num_programs(1) - 1)
    def _():
        ys_ref[...] = _pack_bf16_pairs(acc_ref[...].astype(BF16).astype(F32))


def _experts(xs, block_e, n_valid, w_gate, w_up, w_down, layer, tm, tf):
    cap = xs.shape[0]
    d = 2 * xs.shape[1]
    ff = w_gate.shape[3]
    n_tiles = ff // tf
    assert n_tiles >= 2
    n_steps = (n_tiles + 1) // 2

    def blk(b, nv):
        return jnp.minimum(b, nv[0] - 1)

    def tile(which):
        last_own = (n_tiles - 1 - which) // 2
        return lambda b, j, nv: 2 * jnp.where(b < nv[0], jnp.minimum(j, last_own), last_own) + which

    def w_specs(which):
        t = tile(which)
        return [
            pl.BlockSpec((None, None, d, tf), lambda b, j, be, nv: (layer, be[blk(b, nv)], 0, t(b, j, nv))),
            pl.BlockSpec((None, None, d, tf), lambda b, j, be, nv: (layer, be[blk(b, nv)], 0, t(b, j, nv))),
            pl.BlockSpec((None, None, tf, d), lambda b, j, be, nv: (layer, be[blk(b, nv)], t(b, j, nv), 0)),
        ]

    return pl.pallas_call(
        functools.partial(_expert_kernel, n_tiles=n_tiles),
        grid_spec=pltpu.PrefetchScalarGridSpec(
            num_scalar_prefetch=2,
            grid=(cap // tm, n_steps),
            in_specs=[pl.BlockSpec((tm, d // 2), lambda b, j, be, nv: (blk(b, nv), 0))] + w_specs(0) + w_specs(1),
            out_specs=pl.BlockSpec((tm, d // 2), lambda b, j, be, nv: (b, 0)),
            scratch_shapes=[pltpu.VMEM((tm, d), BF16), pltpu.VMEM((tm, d), F32)],
        ),
        out_shape=jax.ShapeDtypeStruct((cap, d // 2), jnp.int32),
        compiler_params=_params("arbitrary", "arbitrary"),
        name="moe_experts",
    )(block_e, n_valid, xs, w_gate, w_up, w_down, w_gate, w_up, w_down)


def _combine_kernel(x_ref, rg_ref, gf_ref, y0_ref, y1_ref, o_ref):
    rg = rg_ref[...]
    y0 = _unpack_bf16_pairs(y0_ref[...]).astype(F32)
    y1 = _unpack_bf16_pairs(y1_ref[...]).astype(F32)
    o_ref[...] = _rms(x_ref[...] + (rg[:, 0:1] * y0 + rg[:, 1:2] * y1), gf_ref[...])


def _combine_and_final_norm(x, route_g, yg, final_gain):
    n, d = x.shape
    tm = 1024
    return pl.pallas_call(
        _combine_kernel,
        grid=(n // tm,),
        in_specs=[
            pl.BlockSpec((tm, d), lambda i: (i, 0)),
            pl.BlockSpec((tm, ROUTE_COLS), lambda i: (i, 0)),
            pl.BlockSpec((1, d), lambda i: (0, 0)),
            pl.BlockSpec((None, tm, d // 2), lambda i: (0, i, 0)),
            pl.BlockSpec((None, tm, d // 2), lambda i: (1, i, 0)),
        ],
        out_specs=pl.BlockSpec((tm, d), lambda i: (i, 0)),
        out_shape=jax.ShapeDtypeStruct((n, d), F32),
        compiler_params=_params("parallel"),
        name="moe_combine",
    )(x, route_g, final_gain, yg, yg)


def _moe_layer(x, routing, w_gate, w_up, w_down, layer):
    n, d = x.shape
    tm = 1024 if n * TOP_K >= 8 * 1024 else 256
    tf = 2 * MXU_DIM
    route_i, route_g, counts, h_packed = routing

    counts = counts[0].astype(jnp.int32)
    padded = ((counts + tm - 1) // tm) * tm
    ends = jnp.cumsum(padded)
    starts = ends - padded
    dest = (starts[route_i[:, :TOP_K]] + route_i[:, TOP_K:2 * TOP_K]).T
    n_blocks = (n * TOP_K) // tm + N_EXPERTS
    block_start = jnp.arange(n_blocks, dtype=jnp.int32) * tm
    block_e = jnp.minimum(jnp.sum(ends[None, :] <= block_start[:, None], axis=1), N_EXPERTS - 1).astype(jnp.int32)
    n_valid = (ends[-1:] // tm).astype(jnp.int32)

    xs = _dispatch(h_packed, dest, n_blocks * tm)
    ys = _experts(xs, block_e, n_valid, w_gate, w_up, w_down, layer, tm, tf)
    return route_g, _gather_rows(ys, dest.reshape(TOP_K * n)).reshape(TOP_K, n, d // 2)


def _rope_tables(seq):
    pos = jnp.arange(seq, dtype=F32)
    inv_freq = ROPE_THETA ** (-jnp.arange(0, HEAD_DIM, 2, dtype=F32) / HEAD_DIM)
    ang = pos[:, None] * inv_freq[None, :]
    reps = LANES // (HEAD_DIM // 2)
    cos = jnp.tile(jnp.cos(ang), (1, reps))
    sin = jnp.tile(jnp.sin(ang), (1, reps))
    first_half = (jnp.arange(LANES) % HEAD_DIM) < HEAD_DIM // 2
    tabs = (cos, jnp.where(first_half, -sin, sin))
    q_scale = HEAD_DIM ** -0.5 * 1.4426950408889634
    return tuple(jnp.stack([t * q_scale, t]) for t in tabs)


def kernel(x, mix_norm, ffn_norm, attn_w_qkv, attn_w_o, pool_w, pool_scale, dense_w_gate, dense_w_up,
           dense_w_down, moe_router, moe_w_gate, moe_w_up, moe_w_down, final_norm):
    batch, seq, d = x.shape
    depth = mix_norm.shape[0]
    assert depth % 2 == 0 and d == N_HEADS * HEAD_DIM
    assert all(seq % (dil * (win // dil)) == 0 for win, dil in zip(WINDOWS, DILATIONS))
    assert seq % QKV_ROWS == 0 and seq % POOL_ROWS == 0 and moe_router.shape[2] == N_EXPERTS
    rope_tables = _rope_tables(seq)
    xf = x.reshape(batch * seq, d)
    final_gain = final_norm.reshape(1, d)
    pending_moe = None
    for i in range(depth):
        j = i // 2
        mix_gain = mix_norm[i].reshape(1, d)
        ffn_gain = ffn_norm[i].reshape(1, d)
        if i % 2 == 0:
            w_qkv = attn_w_qkv[j].astype(BF16)
            qkv, xf = _qkv_project(xf, mix_gain, w_qkv, rope_tables, batch, seq, pending_moe)
            pending_moe = None
            xf = _attention_and_dense_ffn(qkv, xf, attn_w_o[j].astype(BF16), ffn_gain, dense_w_gate[j].astype(BF16),
                                          dense_w_up[j].astype(BF16), dense_w_down[j].astype(BF16), batch, seq)
        else:
            xf, *routing = _pool_mixer_and_router(xf, mix_gain, pool_w[j].astype(BF16), pool_scale[j].reshape(1, d),
                                                  ffn_gain, moe_router[j], batch, seq)
            pending_moe = _moe_layer(xf, routing, moe_w_gate, moe_w_up, moe_w_down, j)
    return _combine_and_final_norm(xf, *pending_moe, final_gain).reshape(batch, seq, d)
```

```python
import functools

import jax
import jax.numpy as jnp
from jax import lax
from jax.experimental import pallas as pl
from jax.experimental.pallas import tpu as pltpu
from jax.experimental.pallas import tpu_sc as plsc

WINDOWS = (128, 512, 2048)
DILATIONS = (1, 4, 16)
N_GROUPS = len(WINDOWS)
N_HEADS = 16
HEAD_DIM = 64
ROPE_THETA = 10000.0
POOL_SIZES = (2, 4, 8, 16)
N_EXPERTS = 8
TOP_K = 2
RMS_EPS = 1e-6
MASK_VALUE = -1e30
QKV_ROWS = 512
POOL_ROWS = 1024

LANES = 128
MXU_DIM = 256
VMEM_LIMIT_BYTES = 56 * 1024 * 1024

F32 = jnp.float32
BF16 = jnp.bfloat16


def _params(*sem):
    return pltpu.CompilerParams(dimension_semantics=sem, vmem_limit_bytes=VMEM_LIMIT_BYTES)


def _rms(x, g):
    ms = jnp.mean(x * x, axis=-1, keepdims=True)
    return x * lax.rsqrt(ms + RMS_EPS) * g


def _silu(g):
    return g / (1.0 + jnp.exp(-g))


def _qkv_kernel(x_ref, g_ref, w_ref, cos_ref, sin_ref, *rest, pending_moe):
    if pending_moe:
        rg_ref, y0_ref, y1_ref, o0_ref, o1_ref, o2_ref, xn_ref, acc_ref = rest
        rg = rg_ref[...]
        x = x_ref[...] + (rg[:, 0:1] * _unpack_bf16_pairs(y0_ref[...]).astype(F32)
                          + rg[:, 1:2] * _unpack_bf16_pairs(y1_ref[...]).astype(F32))
        xn_ref[...] = x
    else:
        o0_ref, o1_ref, o2_ref, acc_ref = rest
        x = x_ref[...]
    tm, d = x_ref.shape
    n_chunks = acc_ref.shape[1]
    lane = lax.broadcasted_iota(jnp.int32, (1, LANES), 1)
    first_half = (lane % HEAD_DIM) < HEAD_DIM // 2
    h = _rms(x, g_ref[...]).astype(BF16)
    for g, (o_ref, dil) in enumerate(zip((o0_ref, o1_ref, o2_ref), DILATIONS)):
        rows = tm // dil

        def strided(ref, r, dil=dil, rows=rows):
            if dil == 1:
                return ref[...]
            return ref[pl.ds(r, rows, stride=dil), :]

        for kind in range(3):
            col0 = (g * 3 + kind) * d
            acc = jnp.dot(h, w_ref[:, col0:col0 + d], preferred_element_type=F32)
            for c in range(n_chunks):
                acc_ref[kind, c] = acc[:, c * LANES:(c + 1) * LANES]
            for r in range(dil):
                if kind < 2:
                    cos = strided(cos_ref.at[kind], r)
                    sin = strided(sin_ref.at[kind], r)
                for c in range(n_chunks):
                    t = strided(acc_ref.at[kind, c], r)
                    if kind < 2:
                        partner = jnp.where(first_half, pltpu.roll(t, LANES - HEAD_DIM // 2, 1),
                                            pltpu.roll(t, HEAD_DIM // 2, 1))
                        t = t * cos + partner * sin
                    o_ref[kind, r, :, c * LANES:(c + 1) * LANES] = t.astype(BF16)


def _qkv_project(x, gain, w_qkv, tables, batch, seq, pending_moe=None):
    n, d = x.shape
    tm = QKV_ROWS // 2 if pending_moe else QKV_ROWS
    tiles_per_seq = seq // tm
    tab = pl.BlockSpec((2, tm, LANES), lambda i: (0, i % tiles_per_seq, 0))

    def o_spec(dil):
        return pl.BlockSpec((3, None, dil, tm // dil, d),
                            lambda i: (0, i // tiles_per_seq, 0, i % tiles_per_seq, 0))

    row = pl.BlockSpec((tm, d), lambda i: (i, 0))
    in_specs = [row, pl.BlockSpec((1, d), lambda i: (0, 0)),
                pl.BlockSpec(w_qkv.shape, lambda i: (0, 0), pipeline_mode=pl.Buffered(1)), tab, tab]
    args = [x, gain, w_qkv, *tables]
    out_specs = [o_spec(dil) for dil in DILATIONS]
    out_shape = [jax.ShapeDtypeStruct((3, batch, dil, seq // dil, d), BF16) for dil in DILATIONS]
    if pending_moe:
        route_g, yg = pending_moe
        in_specs += [pl.BlockSpec((tm, ROUTE_COLS), lambda i: (i, 0))]
        in_specs += [pl.BlockSpec((None, tm, d // 2), lambda i, k=k: (k, i, 0)) for k in range(TOP_K)]
        args += [route_g, yg, yg]
        out_specs.append(row)
        out_shape.append(jax.ShapeDtypeStruct((n, d), F32))
    outs = pl.pallas_call(
        functools.partial(_qkv_kernel, pending_moe=bool(pending_moe)),
        grid=(n // tm,),
        in_specs=in_specs,
        out_specs=out_specs,
        out_shape=out_shape,
        scratch_shapes=[pltpu.VMEM((3, d // LANES, tm, LANES), F32)],
        compiler_params=_params("parallel"),
        name="qkv_rope",
    )(*args)
    return [o.reshape(3, batch, seq, d) for o in outs[:N_GROUPS]], (outs[N_GROUPS] if pending_moe else x)


ATTN_TILES_PER_FFN_PIECE = 3


def _attn_dense_kernel(q0, k0, v0, q1, k1, v1, q2, k2, v2, band_ref, causal_ref,
                       x_ref, wo_ref, g_ref, wg_ref, wu_ref, wd_ref, y_ref,
                       o_scr, num_ref, max_ref, den_ref, kt_ref):
    b = pl.program_id(0)
    hp = pl.program_id(1)
    n_pairs, seq = o_scr.shape[1], o_scr.shape[2]
    w = WINDOWS[0] // DILATIONS[0]
    lane = lax.broadcasted_iota(jnp.int32, (1, LANES), 1)
    first_head = lane < HEAD_DIM
    o_out = o_scr.at[b % 2, hp]
    o_in = o_scr.at[(b + 1) % 2]

    def ffn_pieces():
        rows = pl.ds(pl.multiple_of(hp * x_ref.shape[0], x_ref.shape[0]), x_ref.shape[0])
        o = jnp.concatenate([o_in[p, rows, :] for p in range(n_pairs)], axis=1)
        x = x_ref[...] + jnp.dot(o, wo_ref[...], preferred_element_type=F32)
        h = _rms(x, g_ref[...]).astype(BF16)
        yield
        acts = []
        n_chunks = wg_ref.shape[1] // MXU_DIM
        for c in range(n_chunks):
            cols = slice(c * MXU_DIM, (c + 1) * MXU_DIM)
            gate = jnp.dot(h, wg_ref[:, cols], preferred_element_type=F32)
            up = jnp.dot(h, wu_ref[:, cols], preferred_element_type=F32)
            acts.append((_silu(gate) * up).astype(BF16))
            yield
        half = n_chunks // 2
        acc = jnp.dot(jnp.concatenate(acts[:half], axis=1), wd_ref[:half * MXU_DIM, :], preferred_element_type=F32)
        yield
        acc = acc + jnp.dot(jnp.concatenate(acts[half:], axis=1), wd_ref[half * MXU_DIM:, :],
                            preferred_element_type=F32)
        y_ref[...] = x + acc
        yield

    def tile(q_ref, v_ref, qoff, has_prev):
        q = q_ref[qoff:qoff + w, :]
        zero = jnp.zeros_like(q)
        qq = jnp.concatenate([jnp.where(first_head, q, zero), jnp.where(first_head, zero, q)], axis=0)
        k_lo = qoff - w if has_prev else qoff
        s = jnp.dot(qq, kt_ref[:, k_lo:qoff + w], preferred_element_type=F32)
        s = s + (band_ref[...] if has_prev else causal_ref[...])
        mx = jnp.max(s, axis=1, keepdims=True)
        p = jnp.exp2(s - mx)
        den = jnp.sum(p, axis=1, keepdims=True)
        pv = jnp.dot(p.astype(BF16), v_ref[k_lo:qoff + w, :], preferred_element_type=F32)
        return tuple(jnp.where(first_head, a[:w], a[w:]) for a in (pv, mx, den))

    def attention(with_ffn):
        pieces = ffn_pieces() if with_ffn else iter(())
        next(pieces, None)
        tiles_done = 0
        order = sorted(range(N_GROUPS), key=lambda g: -DILATIONS[g])
        for g in order:
            q_ref, k_ref, v_ref = ((q0, k0, v0), (q1, k1, v1), (q2, k2, v2))[g]
            dil = DILATIONS[g]
            sub_len = seq // dil
            nb = sub_len // w
            kt_ref[...] = k_ref[...].T
            for t in range(dil * nb):
                r, n = divmod(t, nb)
                qoff = r * sub_len + n * w
                pv, mx, den = tile(q_ref, v_ref, qoff, n > 0)
                tiles_done += 1
                if tiles_done % ATTN_TILES_PER_FFN_PIECE == 0:
                    next(pieces, None)
                rows = pl.ds(qoff, w) if dil == 1 else pl.ds(r + dil * n * w, w, stride=dil)
                if g == order[0]:
                    num_ref[rows, :] = pv
                    max_ref[rows, :] = mx
                    den_ref[rows, :] = den
                    continue
                old_max = max_ref[rows, :]
                new_max = jnp.maximum(old_max, mx)
                a = jnp.exp2(old_max - new_max)
                c = jnp.exp2(mx - new_max)
                num = a * num_ref[rows, :] + c * pv
                den = a * den_ref[rows, :] + c * den
                if g == order[-1]:
                    o_out[rows, :] = (num / den).astype(BF16)
                else:
                    num_ref[rows, :] = num
                    den_ref[rows, :] = den
                    max_ref[rows, :] = new_max
        for _ in pieces:
            pass

    n_batches = pl.num_programs(0) - 1
    @pl.when(b == 0)
    def _():
        y_ref[...] = jnp.zeros_like(y_ref)
        attention(False)

    pl.when((b > 0) & (b < n_batches))(functools.partial(attention, True))

    @pl.when(b == n_batches)
    def _():
        for _ in ffn_pieces():
            pass


def _attention_biases(w):
    row = jnp.arange(2 * w)[:, None] % w
    col = jnp.arange(2 * w)[None, :]
    dist = row + w - col
    band = jnp.where((dist >= 0) & (dist <= w), 0.0, MASK_VALUE).astype(F32)
    causal = jnp.where(row >= col[:, :w], 0.0, MASK_VALUE).astype(F32)
    return band, causal


def _attention_and_dense_ffn(qkv_groups, x, w_o, gain, w_gate, w_up, w_down, batch, seq):
    n, d = x.shape
    ff = w_gate.shape[1]
    n_pairs = d // LANES
    rows = seq // n_pairs
    last = batch - 1
    in_specs, args = [], []
    for arr in qkv_groups:
        for kind in range(3):
            in_specs.append(pl.BlockSpec(
                (None, None, seq, LANES),
                lambda b, hp, kind=kind: (kind, jnp.minimum(b, last), 0, jnp.where(b > last, n_pairs - 1, hp))))
            args.append(arr)
    const = lambda shape: pl.BlockSpec(shape, lambda b, hp: (0, 0), pipeline_mode=pl.Buffered(1))
    ffn_in = pl.BlockSpec((rows, d), lambda b, hp: (jnp.maximum(b - 1, 0) * n_pairs + hp, 0))
    ffn_out = pl.BlockSpec((rows, d), lambda b, hp: (jnp.where(b == 0, batch, b - 1) * n_pairs + hp, 0))
    biases = _attention_biases(WINDOWS[0] // DILATIONS[0])
    in_specs += [const(t.shape) for t in biases]
    in_specs += [ffn_in, const((d, d)), const((1, d)), const((d, ff)), const((d, ff)), const((ff, d))]
    args += [*biases, x, w_o, gain, w_gate, w_up, w_down]
    return pl.pallas_call(
        _attn_dense_kernel,
        grid=(batch + 1, n_pairs),
        in_specs=in_specs,
        out_specs=ffn_out,
        out_shape=jax.ShapeDtypeStruct((n + seq, d), F32),
        scratch_shapes=[pltpu.VMEM((2, n_pairs, seq, LANES), BF16)]
        + [pltpu.VMEM((seq, LANES), F32)] * 3 + [pltpu.VMEM((LANES, seq), BF16)],
        compiler_params=_params("arbitrary", "arbitrary"),
        name="attention_ffn",
    )(*args)


POOL_HALO = max(POOL_SIZES)


def _pool_tile(i, x_ref, halo_ref, g_ref, w_ref, sc_ref, y_ref):
    ts = x_ref.shape[0]
    g = g_ref[...]
    x = x_ref[...]
    h = _rms(x, g)
    hh = jnp.where(i > 0, _rms(halo_ref[...], g), 0.0)
    ext = jnp.concatenate([hh, h], axis=0)
    pos = i * ts + lax.broadcasted_iota(jnp.int32, (ts, 1), 0)
    pc = w_ref.shape[1]
    for grp, size in enumerate(POOL_SIZES):
        sl = slice(grp * pc, (grp + 1) * pc)
        s = ext[:, sl]
        step = 1
        while step < size:
            s = s + pltpu.roll(s, step, 0)
            step *= 2
        cnt = jnp.minimum(pos + 1, size).astype(F32)
        y = s[POOL_HALO:, :] / cnt - h[:, sl]
        z = jnp.dot(y.astype(BF16), w_ref[grp], preferred_element_type=F32)
        y_ref[:, sl] = x[:, sl] + z * sc_ref[:, sl]


def _pool_router_kernel(x_ref, halo_ref, g_ref, w_ref, sc_ref, g2_ref, wr_ref, tril_ref,
                        y_ref, ri_ref, rg_ref, cnt_ref, hp_ref, run_ref):
    b = pl.program_id(0)
    i = pl.program_id(1)
    _pool_tile(i, x_ref, halo_ref, g_ref, w_ref, sc_ref, y_ref)
    _route_tile((b == 0) & (i == 0), y_ref[...], g2_ref, wr_ref, tril_ref, ri_ref, rg_ref, cnt_ref, hp_ref, run_ref)


def _pool_mixer_and_router(x, gain, pool_w, scale, ffn_gain, w_router, batch, seq):
    d = x.shape[1]
    n = batch * seq
    ne = w_router.shape[1]
    ts = POOL_ROWS
    tiles = seq // ts
    x3 = x.reshape(-1, seq, d)
    hb = ts // POOL_HALO
    tril = (jnp.arange(ts)[:, None] > jnp.arange(ts)[None, :]).astype(BF16)
    const = lambda shape: pl.BlockSpec(shape, lambda b, i: (0,) * len(shape))
    tok = lambda cols: pl.BlockSpec((ts, cols), lambda b, i: (b * tiles + i, 0))
    out, route_i, route_g, counts, h_packed = pl.pallas_call(
        _pool_router_kernel,
        grid=(batch, tiles),
        in_specs=[
            pl.BlockSpec((None, ts, d), lambda b, i: (b, i, 0)),
            pl.BlockSpec((None, POOL_HALO, d), lambda b, i: (b, jnp.maximum(i * hb - 1, 0), 0)),
            const((1, d)), const(pool_w.shape), const((1, d)), const((1, d)), const((d, ne)), const((ts, ts)),
        ],
        out_specs=[
            pl.BlockSpec((None, ts, d), lambda b, i: (b, i, 0)),
            tok(ROUTE_COLS), tok(ROUTE_COLS), const((1, ne)), tok(d // 2),
        ],
        out_shape=[
            jax.ShapeDtypeStruct((batch, seq, d), F32),
            jax.ShapeDtypeStruct((n, ROUTE_COLS), jnp.int32),
            jax.ShapeDtypeStruct((n, ROUTE_COLS), F32),
            jax.ShapeDtypeStruct((1, ne), F32),
            jax.ShapeDtypeStruct((n, d // 2), jnp.int32),
        ],
        scratch_shapes=[pltpu.VMEM((1, ne), F32)],
        compiler_params=_params("arbitrary", "arbitrary"),
        name="pool_router",
    )(x3, x3, gain, pool_w, scale, ffn_gain, w_router, tril)
    return out.reshape(n, d), route_i, route_g, counts, h_packed


ROUTE_COLS = 8


def _pack_bf16_pairs(hr):
    c = hr.shape[1] // 2
    bits = pltpu.bitcast(hr, jnp.int32)
    return (bits[:, c:] & jnp.int32(-65536)) | lax.shift_right_logical(bits[:, :c], 16)


def _unpack_bf16_pairs(packed):
    lo = pltpu.bitcast(lax.shift_left(packed, 16), F32)
    hi = pltpu.bitcast(packed & jnp.int32(-65536), F32)
    return jnp.concatenate([lo.astype(BF16), hi.astype(BF16)], axis=1)


def _route_tile(first, x, g_ref, wr_ref, tril_ref, ri_ref, rg_ref, cnt_ref, hp_ref, run_ref):
    @pl.when(first)
    def _():
        run_ref[...] = jnp.zeros_like(run_ref)

    h = _rms(x, g_ref[...])
    wr = wr_ref[...]
    ne = wr.shape[1]
    h_hi = h.astype(BF16)
    hp_ref[...] = _pack_bf16_pairs(h_hi.astype(F32))
    h_lo = (h - h_hi.astype(F32)).astype(BF16)
    w_hi = wr.astype(BF16)
    w_lo = (wr - w_hi.astype(F32)).astype(BF16)
    hi_part = jnp.dot(h_hi, jnp.concatenate([w_hi, w_lo], axis=1), preferred_element_type=F32)
    logits = hi_part[:, :ne] + hi_part[:, ne:] + jnp.dot(h_lo, w_hi, preferred_element_type=F32)
    tm = logits.shape[0]
    lane = lax.broadcasted_iota(jnp.int32, (tm, ne), 1)
    v1 = jnp.max(logits, axis=1, keepdims=True)
    i1 = jnp.min(jnp.where(logits == v1, lane, ne), axis=1, keepdims=True)
    rest = jnp.where(lane == i1, -jnp.inf, logits)
    v2 = jnp.max(rest, axis=1, keepdims=True)
    i2 = jnp.min(jnp.where(rest == v2, lane, ne), axis=1, keepdims=True)
    e = jnp.exp(v2 - v1)
    g1 = 1.0 / (1.0 + e)
    g2 = e / (1.0 + e)
    oh1 = (lane == i1).astype(F32)
    oh2 = (lane == i2).astype(F32)
    both = oh1 + oh2
    before = jnp.dot(tril_ref[...], both.astype(BF16), preferred_element_type=F32) + run_ref[...]
    r1 = jnp.sum(before * oh1, axis=1, keepdims=True).astype(jnp.int32)
    r2 = jnp.sum(before * oh2, axis=1, keepdims=True).astype(jnp.int32)
    run_ref[...] += jnp.sum(both, axis=0, keepdims=True)
    ri_ref[...] = jnp.where(lane == 0, i1, jnp.where(lane == 1, i2, jnp.where(lane == 2, r1, r2)))
    rg_ref[...] = jnp.where(lane == 0, g1, g2)
    cnt_ref[...] = run_ref[...]


SC_WINDOW = 64


def _sc_workers():
    info = plsc.get_sparse_core_info()
    mesh = plsc.VectorSubcoreMesh(core_axis_name="core", subcore_axis_name="subcore")
    return mesh, info.num_cores, info.num_cores * info.num_subcores


def _dispatch(x, dest, cap):
    n, d = x.shape
    mesh, n_cores, n_workers = _sc_workers()
    per_worker = n // n_workers

    n_windows = per_worker // SC_WINDOW
    n_bufs = 2
    assert n_windows % n_bufs == 0

    @functools.partial(
        pl.kernel, mesh=mesh, out_type=jax.ShapeDtypeStruct((cap, d), x.dtype),
        scratch_types=[pltpu.VMEM((SC_WINDOW,), jnp.int32)] * (TOP_K * n_bufs)
        + [pltpu.VMEM((SC_WINDOW, d), x.dtype)] * n_bufs + [pltpu.SemaphoreType.DMA] * n_bufs,
        name="moe_dispatch")
    def scatter(x_hbm, *rest):
        idx_hbm, (xs_hbm, *scratch) = rest[:TOP_K], rest[TOP_K:]
        idx_vmem = [scratch[b * TOP_K:(b + 1) * TOP_K] for b in range(n_bufs)]
        rows = scratch[TOP_K * n_bufs:TOP_K * n_bufs + n_bufs]
        sems = scratch[TOP_K * n_bufs + n_bufs:]
        wid = lax.axis_index("subcore") * n_cores + lax.axis_index("core")

        def drain(buf):
            for k in range(TOP_K):
                pltpu.make_async_copy(rows[buf], xs_hbm.at[idx_vmem[buf][k]], sems[buf]).wait()

        @pl.loop(0, n_windows, step=n_bufs)
        def _(c):
            for buf in range(n_bufs):
                @pl.when(c > 0)
                def _(buf=buf):
                    drain(buf)

                window = pl.ds(wid * per_worker + (c + buf) * SC_WINDOW, SC_WINDOW)
                for k in range(TOP_K):
                    pltpu.sync_copy(idx_hbm[k].at[window], idx_vmem[buf][k])
                pltpu.sync_copy(x_hbm.at[window], rows[buf])
                for k in range(TOP_K):
                    pltpu.async_copy(rows[buf], xs_hbm.at[idx_vmem[buf][k]], sems[buf])

        for buf in range(n_bufs):
            drain(buf)

    return scatter(x, *[dest[k] for k in range(TOP_K)])


def _gather_rows(table, idx):
    m = idx.shape[0]
    d = table.shape[1]
    mesh, n_cores, n_workers = _sc_workers()
    per_worker = m // n_workers

    n_windows = per_worker // SC_WINDOW
    n_bufs = 2
    assert n_windows % n_bufs == 0

    @functools.partial(
        pl.kernel, mesh=mesh, out_type=jax.ShapeDtypeStruct((m, d), table.dtype),
        scratch_types=[pltpu.VMEM((SC_WINDOW,), jnp.int32)] * n_bufs
        + [pltpu.VMEM((SC_WINDOW, d), table.dtype)] * n_bufs + [pltpu.SemaphoreType.DMA] * n_bufs,
        name="moe_gather")
    def gather(t_hbm, i_hbm, o_hbm, *scratch):
        idx_vmem, rows, sems = scratch[:n_bufs], scratch[n_bufs:2 * n_bufs], scratch[2 * n_bufs:]
        wid = lax.axis_index("subcore") * n_cores + lax.axis_index("core")

        def window(c):
            return pl.ds(wid * per_worker + c * SC_WINDOW, SC_WINDOW)

        def fetch(c, buf):
            pltpu.sync_copy(i_hbm.at[window(c)], idx_vmem[buf])
            return pltpu.async_copy(t_hbm.at[idx_vmem[buf]], rows[buf], sems[buf])

        fetch(0, 0)

        @pl.loop(0, n_windows, step=n_bufs)
        def _(c):
            for buf in range(n_bufs):
                nxt = c + buf + 1

                @pl.when(nxt < n_windows)
                def _(nxt=nxt, buf=buf):
                    fetch(nxt, (buf + 1) % n_bufs)

                pltpu.make_async_copy(t_hbm.at[idx_vmem[buf]], rows[buf], sems[buf]).wait()
                pltpu.sync_copy(rows[buf], o_hbm.at[window(c + buf)])

    return gather(table, idx)


def _expert_kernel(be_ref, nv_ref, xs_ref, wg_ref, wu_ref, wd_ref, ys_ref, xb_ref, acc_ref):
    del be_ref
    b = pl.program_id(0)
    j = pl.program_id(1)

    def ff_tile(first):
        x = xb_ref[...]
        acts = []
        for c in range(wg_ref.shape[1] // MXU_DIM):
            cols = slice(c * MXU_DIM, (c + 1) * MXU_DIM)
            gate = jnp.dot(x, wg_ref[:, cols].astype(BF16), preferred_element_type=F32)
            up = jnp.dot(x, wu_ref[:, cols].astype(BF16), preferred_element_type=F32)
            acts.append((_silu(gate) * up).astype(BF16))
        part = jnp.dot(jnp.concatenate(acts, axis=1), wd_ref[...].astype(BF16), preferred_element_type=F32)
        if first:
            acc_ref[...] = part
        else:
            acc_ref[...] += part

    valid = b < nv_ref[0]

    @pl.when(valid & (j == 0))
    def _():
        xb_ref[...] = _unpack_bf16_pairs(xs_ref[...])
        ff_tile(True)

    @pl.when(valid & (j > 0))
    def _():
        ff_tile(False)

    @pl.when(jnp.logical_not(valid) & (j == 0))
    def _():
        acc_ref[...] = jnp.zeros_like(acc_ref)

    @pl.when(j == pl.num_programs(1) - 1)
    def _():
        ys_ref[...] = _pack_bf16_pairs(acc_ref[...].astype(BF16).astype(F32))


def _experts(xs, block_e, n_valid, w_gate, w_up, w_down, layer, tm, tf):
    cap = xs.shape[0]
    d = 2 * xs.shape[1]
    ff = w_gate.shape[3]
    nj = ff // tf

    def blk(b, nv):
        return jnp.minimum(b, nv[0] - 1)

    def ffi(b, j, nv):
        return jnp.where(b < nv[0], j, nj - 1)

    return pl.pallas_call(
        _expert_kernel,
        grid_spec=pltpu.PrefetchScalarGridSpec(
            num_scalar_prefetch=2,
            grid=(cap // tm, nj),
            in_specs=[
                pl.BlockSpec((tm, d // 2), lambda b, j, be, nv: (blk(b, nv), 0)),
                pl.BlockSpec((None, None, d, tf), lambda b, j, be, nv: (layer, be[blk(b, nv)], 0, ffi(b, j, nv))),
                pl.BlockSpec((None, None, d, tf), lambda b, j, be, nv: (layer, be[blk(b, nv)], 0, ffi(b, j, nv))),
                pl.BlockSpec((None, None, tf, d), lambda b, j, be, nv: (layer, be[blk(b, nv)], ffi(b, j, nv), 0)),
            ],
            out_specs=pl.BlockSpec((tm, d // 2), lambda b, j, be, nv: (b, 0)),
            scratch_shapes=[pltpu.VMEM((tm, d), BF16), pltpu.VMEM((tm, d), F32)],
        ),
        out_shape=jax.ShapeDtypeStruct((cap, d // 2), jnp.int32),
        compiler_params=_params("arbitrary", "arbitrary"),
        name="moe_experts",
    )(block_e, n_valid, xs, w_gate, w_up, w_down)


def _combine_kernel(x_ref, rg_ref, gf_ref, y0_ref, y1_ref, o_ref):
    rg = rg_ref[...]
    y0 = _unpack_bf16_pairs(y0_ref[...]).astype(F32)
    y1 = _unpack_bf16_pairs(y1_ref[...]).astype(F32)
    o_ref[...] = _rms(x_ref[...] + (rg[:, 0:1] * y0 + rg[:, 1:2] * y1), gf_ref[...])


def _combine_and_final_norm(x, route_g, yg, final_gain):
    n, d = x.shape
    tm = 1024
    return pl.pallas_call(
        _combine_kernel,
        grid=(n // tm,),
        in_specs=[
            pl.BlockSpec((tm, d), lambda i: (i, 0)),
            pl.BlockSpec((tm, ROUTE_COLS), lambda i: (i, 0)),
            pl.BlockSpec((1, d), lambda i: (0, 0)),
            pl.BlockSpec((None, tm, d // 2), lambda i: (0, i, 0)),
            pl.BlockSpec((None, tm, d // 2), lambda i: (1, i, 0)),
        ],
        out_specs=pl.BlockSpec((tm, d), lambda i: (i, 0)),
        out_shape=jax.ShapeDtypeStruct((n, d), F32),
        compiler_params=_params("parallel"),
        name="moe_combine",
    )(x, route_g, final_gain, yg, yg)


def _moe_layer(x, routing, w_gate, w_up, w_down, layer):
    n, d = x.shape
    tm = 1024 if n * TOP_K >= 8 * 1024 else 256
    tf = 2 * MXU_DIM
    route_i, route_g, counts, h_packed = routing

    counts = counts[0].astype(jnp.int32)
    padded = ((counts + tm - 1) // tm) * tm
    ends = jnp.cumsum(padded)
    starts = ends - padded
    dest = (starts[route_i[:, :TOP_K]] + route_i[:, TOP_K:2 * TOP_K]).T
    n_blocks = (n * TOP_K) // tm + N_EXPERTS
    block_start = jnp.arange(n_blocks, dtype=jnp.int32) * tm
    block_e = jnp.minimum(jnp.sum(ends[None, :] <= block_start[:, None], axis=1), N_EXPERTS - 1).astype(jnp.int32)
    n_valid = (ends[-1:] // tm).astype(jnp.int32)

    xs = _dispatch(h_packed, dest, n_blocks * tm)
    ys = _experts(xs, block_e, n_valid, w_gate, w_up, w_down, layer, tm, tf)
    return route_g, _gather_rows(ys, dest.reshape(TOP_K * n)).reshape(TOP_K, n, d // 2)


def _rope_tables(seq):
    pos = jnp.arange(seq, dtype=F32)
    inv_freq = ROPE_THETA ** (-jnp.arange(0, HEAD_DIM, 2, dtype=F32) / HEAD_DIM)
    ang = pos[:, None] * inv_freq[None, :]
    reps = LANES // (HEAD_DIM // 2)
    cos = jnp.tile(jnp.cos(ang), (1, reps))
    sin = jnp.tile(jnp.sin(ang), (1, reps))
    first_half = (jnp.arange(LANES) % HEAD_DIM) < HEAD_DIM // 2
    tabs = (cos, jnp.where(first_half, -sin, sin))
    q_scale = HEAD_DIM ** -0.5 * 1.4426950408889634
    return tuple(jnp.stack([t * q_scale, t]) for t in tabs)


def kernel(x, mix_norm, ffn_norm, attn_w_qkv, attn_w_o, pool_w, pool_scale, dense_w_gate, dense_w_up,
           dense_w_down, moe_router, moe_w_gate, moe_w_up, moe_w_down, final_norm):
    batch, seq, d = x.shape
    depth = mix_norm.shape[0]
    assert depth % 2 == 0 and d == N_HEADS * HEAD_DIM
    assert all(seq % (dil * (win // dil)) == 0 for win, dil in zip(WINDOWS, DILATIONS))
    assert seq % QKV_ROWS == 0 and seq % POOL_ROWS == 0 and moe_router.shape[2] == N_EXPERTS
    rope_tables = _rope_tables(seq)
    xf = x.reshape(batch * seq, d)
    final_gain = final_norm.reshape(1, d)
    pending_moe = None
    for i in range(depth):
        j = i // 2
        mix_gain = mix_norm[i].reshape(1, d)
        ffn_gain = ffn_norm[i].reshape(1, d)
        if i % 2 == 0:
            w_qkv = attn_w_qkv[j].astype(BF16)
            qkv, xf = _qkv_project(xf, mix_gain, w_qkv, rope_tables, batch, seq, pending_moe)
            pending_moe = None
            xf = _attention_and_dense_ffn(qkv, xf, attn_w_o[j].astype(BF16), ffn_gain, dense_w_gate[j].astype(BF16),
                                          dense_w_up[j].astype(BF16), dense_w_down[j].astype(BF16), batch, seq)
        else:
            xf, *routing = _pool_mixer_and_router(xf, mix_gain, pool_w[j].astype(BF16), pool_scale[j].reshape(1, d),
                                                  ffn_gain, moe_router[j], batch, seq)
            pending_moe = _moe_layer(xf, routing, moe_w_gate, moe_w_up, moe_w_down, j)
    return _combine_and_final_norm(xf, *pending_moe, final_gain).reshape(batch, seq, d)
```

```python
import functools

import jax
import jax.numpy as jnp
from jax import lax
from jax.experimental import pallas as pl
from jax.experimental.pallas import tpu as pltpu
from jax.experimental.pallas import tpu_sc as plsc

WINDOWS = (128, 512, 2048)
DILATIONS = (1, 4, 16)
N_GROUPS = len(WINDOWS)
N_HEADS = 16
HEAD_DIM = 64
ROPE_THETA = 10000.0
POOL_SIZES = (2, 4, 8, 16)
N_EXPERTS = 8
TOP_K = 2
RMS_EPS = 1e-6
MASK_VALUE = -1e30
QKV_ROWS = 512
POOL_ROWS = 1024

LANES = 128
MXU_DIM = 256
VMEM_LIMIT_BYTES = 56 * 1024 * 1024

F32 = jnp.float32
BF16 = jnp.bfloat16


def _params(*sem):
    return pltpu.CompilerParams(dimension_semantics=sem, vmem_limit_bytes=VMEM_LIMIT_BYTES)


def _rms(x, g):
    ms = jnp.mean(x * x, axis=-1, keepdims=True)
    return x * lax.rsqrt(ms + RMS_EPS) * g


def _silu(g):
    return g / (1.0 + jnp.exp(-g))


def _qkv_kernel(x_ref, g_ref, w_ref, cos_ref, sin_ref, *rest, pending_moe):
    if pending_moe:
        rg_ref, y0_ref, y1_ref, o0_ref, o1_ref, o2_ref, xn_ref, acc_ref = rest
        rg = rg_ref[...]
        x = x_ref[...] + (rg[:, 0:1] * _unpack_bf16_pairs(y0_ref[...]).astype(F32)
                          + rg[:, 1:2] * _unpack_bf16_pairs(y1_ref[...]).astype(F32))
        xn_ref[...] = x
    else:
        o0_ref, o1_ref, o2_ref, acc_ref = rest
        x = x_ref[...]
    tm, d = x_ref.shape
    n_chunks = acc_ref.shape[1]
    lane = lax.broadcasted_iota(jnp.int32, (1, LANES), 1)
    first_half = (lane % HEAD_DIM) < HEAD_DIM // 2
    h = _rms(x, g_ref[...]).astype(BF16)
    for g, (o_ref, dil) in enumerate(zip((o0_ref, o1_ref, o2_ref), DILATIONS)):
        rows = tm // dil

        def strided(ref, r, dil=dil, rows=rows):
            if dil == 1:
                return ref[...]
            return ref[pl.ds(r, rows, stride=dil), :]

        for kind in range(3):
            col0 = (g * 3 + kind) * d
            acc = jnp.dot(h, w_ref[:, col0:col0 + d], preferred_element_type=F32)
            for c in range(n_chunks):
                acc_ref[kind, c] = acc[:, c * LANES:(c + 1) * LANES]
            for r in range(dil):
                if kind < 2:
                    cos = strided(cos_ref.at[kind], r)
                    sin = strided(sin_ref.at[kind], r)
                for c in range(n_chunks):
                    t = strided(acc_ref.at[kind, c], r)
                    if kind < 2:
                        partner = jnp.where(first_half, pltpu.roll(t, LANES - HEAD_DIM // 2, 1),
                                            pltpu.roll(t, HEAD_DIM // 2, 1))
                        t = t * cos + partner * sin
                    o_ref[kind, r, :, c * LANES:(c + 1) * LANES] = t.astype(BF16)


def _qkv_project(x, gain, w_qkv, tables, batch, seq, pending_moe=None):
    n, d = x.shape
    tm = QKV_ROWS // 2 if pending_moe else QKV_ROWS
    tiles_per_seq = seq // tm
    tab = pl.BlockSpec((2, tm, LANES), lambda i: (0, i % tiles_per_seq, 0))

    def o_spec(dil):
        return pl.BlockSpec((3, None, dil, tm // dil, d),
                            lambda i: (0, i // tiles_per_seq, 0, i % tiles_per_seq, 0))

    row = pl.BlockSpec((tm, d), lambda i: (i, 0))
    in_specs = [row, pl.BlockSpec((1, d), lambda i: (0, 0)),
                pl.BlockSpec(w_qkv.shape, lambda i: (0, 0), pipeline_mode=pl.Buffered(1)), tab, tab]
    args = [x, gain, w_qkv, *tables]
    out_specs = [o_spec(dil) for dil in DILATIONS]
    out_shape = [jax.ShapeDtypeStruct((3, batch, dil, seq // dil, d), BF16) for dil in DILATIONS]
    if pending_moe:
        route_g, yg = pending_moe
        in_specs += [pl.BlockSpec((tm, ROUTE_COLS), lambda i: (i, 0))]
        in_specs += [pl.BlockSpec((None, tm, d // 2), lambda i, k=k: (k, i, 0)) for k in range(TOP_K)]
        args += [route_g, yg, yg]
        out_specs.append(row)
        out_shape.append(jax.ShapeDtypeStruct((n, d), F32))
    outs = pl.pallas_call(
        functools.partial(_qkv_kernel, pending_moe=bool(pending_moe)),
        grid=(n // tm,),
        in_specs=in_specs,
        out_specs=out_specs,
        out_shape=out_shape,
        scratch_shapes=[pltpu.VMEM((3, d // LANES, tm, LANES), F32)],
        compiler_params=_params("parallel"),
        name="qkv_rope",
    )(*args)
    return [o.reshape(3, batch, seq, d) for o in outs[:N_GROUPS]], (outs[N_GROUPS] if pending_moe else x)


ATTN_TILES_PER_FFN_PIECE = 3


def _attn_dense_kernel(q0, k0, v0, q1, k1, v1, q2, k2, v2, band_ref, causal_ref,
                       x_ref, wo_ref, g_ref, wg_ref, wu_ref, wd_ref, y_ref,
                       o_scr, num_ref, max_ref, den_ref, kt_ref):
    b = pl.program_id(0)
    hp = pl.program_id(1)
    n_pairs, seq = o_scr.shape[1], o_scr.shape[2]
    w = WINDOWS[0] // DILATIONS[0]
    lane = lax.broadcasted_iota(jnp.int32, (1, LANES), 1)
    first_head = lane < HEAD_DIM
    o_out = o_scr.at[b % 2, hp]
    o_in = o_scr.at[(b + 1) % 2]

    def ffn_pieces():
        rows = pl.ds(pl.multiple_of(hp * x_ref.shape[0], x_ref.shape[0]), x_ref.shape[0])
        o = jnp.concatenate([o_in[p, rows, :] for p in range(n_pairs)], axis=1)
        x = x_ref[...] + jnp.dot(o, wo_ref[...], preferred_element_type=F32)
        h = _rms(x, g_ref[...]).astype(BF16)
        yield
        acts = []
        n_chunks = wg_ref.shape[1] // MXU_DIM
        for c in range(n_chunks):
            cols = slice(c * MXU_DIM, (c + 1) * MXU_DIM)
            gate = jnp.dot(h, wg_ref[:, cols], preferred_element_type=F32)
            up = jnp.dot(h, wu_ref[:, cols], preferred_element_type=F32)
            acts.append((_silu(gate) * up).astype(BF16))
            yield
        half = n_chunks // 2
        acc = jnp.dot(jnp.concatenate(acts[:half], axis=1), wd_ref[:half * MXU_DIM, :], preferred_element_type=F32)
        yield
        acc = acc + jnp.dot(jnp.concatenate(acts[half:], axis=1), wd_ref[half * MXU_DIM:, :],
                            preferred_element_type=F32)
        y_ref[...] = x + acc
        yield

    def tile(q_ref, v_ref, qoff, has_prev):
        q = q_ref[qoff:qoff + w, :]
        zero = jnp.zeros_like(q)
        qq = jnp.concatenate([jnp.where(first_head, q, zero), jnp.where(first_head, zero, q)], axis=0)
        k_lo = qoff - w if has_prev else qoff
        s = jnp.dot(qq, kt_ref[:, k_lo:qoff + w], preferred_element_type=F32)
        s = s + (band_ref[...] if has_prev else causal_ref[...])
        mx = jnp.max(s, axis=1, keepdims=True)
        p = jnp.exp2(s - mx)
        den = jnp.sum(p, axis=1, keepdims=True)
        pv = jnp.dot(p.astype(BF16), v_ref[k_lo:qoff + w, :], preferred_element_type=F32)
        return tuple(jnp.where(first_head, a[:w], a[w:]) for a in (pv, mx, den))

    def attention(with_ffn):
        pieces = ffn_pieces() if with_ffn else iter(())
        next(pieces, None)
        tiles_done = 0
        order = sorted(range(N_GROUPS), key=lambda g: -DILATIONS[g])
        for g in order:
            q_ref, k_ref, v_ref = ((q0, k0, v0), (q1, k1, v1), (q2, k2, v2))[g]
            dil = DILATIONS[g]
            sub_len = seq // dil
            nb = sub_len // w
            kt_ref[...] = k_ref[...].T
            for t in range(dil * nb):
                r, n = divmod(t, nb)
                qoff = r * sub_len + n * w
                pv, mx, den = tile(q_ref, v_ref, qoff, n > 0)
                tiles_done += 1
                if tiles_done % ATTN_TILES_PER_FFN_PIECE == 0:
                    next(pieces, None)
                rows = pl.ds(qoff, w) if dil == 1 else pl.ds(r + dil * n * w, w, stride=dil)
                if g == order[0]:
                    num_ref[rows, :] = pv
                    max_ref[rows, :] = mx
                    den_ref[rows, :] = den
                    continue
                old_max = max_ref[rows, :]
                new_max = jnp.maximum(old_max, mx)
                a = jnp.exp2(old_max - new_max)
                c = jnp.exp2(mx - new_max)
                num = a * num_ref[rows, :] + c * pv
                den = a * den_ref[rows, :] + c * den
                if g == order[-1]:
                    o_out[rows, :] = (num / den).astype(BF16)
                else:
                    num_ref[rows, :] = num
                    den_ref[rows, :] = den
                    max_ref[rows, :] = new_max
        for _ in pieces:
            pass

    n_batches = pl.num_programs(0) - 1
    @pl.when(b == 0)
    def _():
        y_ref[...] = jnp.zeros_like(y_ref)
        attention(False)

    pl.when((b > 0) & (b < n_batches))(functools.partial(attention, True))

    @pl.when(b == n_batches)
    def _():
        for _ in ffn_pieces():
            pass


def _attention_biases(w):
    row = jnp.arange(2 * w)[:, None] % w
    col = jnp.arange(2 * w)[None, :]
    dist = row + w - col
    band = jnp.where((dist >= 0) & (dist <= w), 0.0, MASK_VALUE).astype(F32)
    causal = jnp.where(row >= col[:, :w], 0.0, MASK_VALUE).astype(F32)
    return band, causal


def _attention_and_dense_ffn(qkv_groups, x, w_o, gain, w_gate, w_up, w_down, batch, seq):
    n, d = x.shape
    ff = w_gate.shape[1]
    n_pairs = d // LANES
    rows = seq // n_pairs
    last = batch - 1
    in_specs, args = [], []
    for arr in qkv_groups:
        for kind in range(3):
            in_specs.append(pl.BlockSpec(
                (None, None, seq, LANES),
                lambda b, hp, kind=kind: (kind, jnp.minimum(b, last), 0, jnp.where(b > last, n_pairs - 1, hp))))
            args.append(arr)
    const = lambda shape: pl.BlockSpec(shape, lambda b, hp: (0, 0), pipeline_mode=pl.Buffered(1))
    ffn_in = pl.BlockSpec((rows, d), lambda b, hp: (jnp.maximum(b - 1, 0) * n_pairs + hp, 0))
    ffn_out = pl.BlockSpec((rows, d), lambda b, hp: (jnp.where(b == 0, batch, b - 1) * n_pairs + hp, 0))
    biases = _attention_biases(WINDOWS[0] // DILATIONS[0])
    in_specs += [const(t.shape) for t in biases]
    in_specs += [ffn_in, const((d, d)), const((1, d)), const((d, ff)), const((d, ff)), const((ff, d))]
    args += [*biases, x, w_o, gain, w_gate, w_up, w_down]
    return pl.pallas_call(
        _attn_dense_kernel,
        grid=(batch + 1, n_pairs),
        in_specs=in_specs,
        out_specs=ffn_out,
        out_shape=jax.ShapeDtypeStruct((n + seq, d), F32),
        scratch_shapes=[pltpu.VMEM((2, n_pairs, seq, LANES), BF16)]
        + [pltpu.VMEM((seq, LANES), F32)] * 3 + [pltpu.VMEM((LANES, seq), BF16)],
        compiler_params=_params("arbitrary", "arbitrary"),
        name="attention_ffn",
    )(*args)


POOL_HALO = max(POOL_SIZES)


def _pool_tile(i, x_ref, halo_ref, g_ref, w_ref, sc_ref, y_ref):
    ts = x_ref.shape[0]
    g = g_ref[...]
    x = x_ref[...]
    h = _rms(x, g)
    hh = jnp.where(i > 0, _rms(halo_ref[...], g), 0.0)
    ext = jnp.concatenate([hh, h], axis=0)
    pos = i * ts + lax.broadcasted_iota(jnp.int32, (ts, 1), 0)
    pc = w_ref.shape[1]
    for grp, size in enumerate(POOL_SIZES):
        sl = slice(grp * pc, (grp + 1) * pc)
        s = ext[:, sl]
        step = 1
        while step < size:
            s = s + pltpu.roll(s, step, 0)
            step *= 2
        cnt = jnp.minimum(pos + 1, size).astype(F32)
        y = s[POOL_HALO:, :] / cnt - h[:, sl]
        z = jnp.dot(y.astype(BF16), w_ref[grp], preferred_element_type=F32)
        y_ref[:, sl] = x[:, sl] + z * sc_ref[:, sl]


def _pool_router_kernel(x_ref, halo_ref, g_ref, w_ref, sc_ref, g2_ref, wr_ref, tril_ref,
                        y_ref, ri_ref, rg_ref, cnt_ref, hp_ref, run_ref):
    b = pl.program_id(0)
    i = pl.program_id(1)
    _pool_tile(i, x_ref, halo_ref, g_ref, w_ref, sc_ref, y_ref)
    _route_tile((b == 0) & (i == 0), y_ref[...], g2_ref, wr_ref, tril_ref, ri_ref, rg_ref, cnt_ref, hp_ref, run_ref)


def _pool_mixer_and_router(x, gain, pool_w, scale, ffn_gain, w_router, batch, seq):
    d = x.shape[1]
    n = batch * seq
    ne = w_router.shape[1]
    ts = POOL_ROWS
    tiles = seq // ts
    x3 = x.reshape(-1, seq, d)
    hb = ts // POOL_HALO
    tril = (jnp.arange(ts)[:, None] > jnp.arange(ts)[None, :]).astype(BF16)
    const = lambda shape: pl.BlockSpec(shape, lambda b, i: (0,) * len(shape))
    tok = lambda cols: pl.BlockSpec((ts, cols), lambda b, i: (b * tiles + i, 0))
    out, route_i, route_g, counts, h_packed = pl.pallas_call(
        _pool_router_kernel,
        grid=(batch, tiles),
        in_specs=[
            pl.BlockSpec((None, ts, d), lambda b, i: (b, i, 0)),
            pl.BlockSpec((None, POOL_HALO, d), lambda b, i: (b, jnp.maximum(i * hb - 1, 0), 0)),
            const((1, d)), const(pool_w.shape), const((1, d)), const((1, d)), const((d, ne)), const((ts, ts)),
        ],
        out_specs=[
            pl.BlockSpec((None, ts, d), lambda b, i: (b, i, 0)),
            pl.BlockSpec((ROUTE_COLS, ts), lambda b, i: (0, b * tiles + i)),
            tok(ROUTE_COLS), const((1, ne)), tok(d // 2),
        ],
        out_shape=[
            jax.ShapeDtypeStruct((batch, seq, d), F32),
            jax.ShapeDtypeStruct((ROUTE_COLS, n), jnp.int32),
            jax.ShapeDtypeStruct((n, ROUTE_COLS), F32),
            jax.ShapeDtypeStruct((1, ne), F32),
            jax.ShapeDtypeStruct((n, d // 2), jnp.int32),
        ],
        scratch_shapes=[pltpu.VMEM((1, ne), F32)],
        compiler_params=_params("arbitrary", "arbitrary"),
        name="pool_router",
    )(x3, x3, gain, pool_w, scale, ffn_gain, w_router, tril)
    return out.reshape(n, d), route_i, route_g, counts, h_packed


ROUTE_COLS = 8


def _pack_bf16_pairs(hr):
    c = hr.shape[1] // 2
    bits = pltpu.bitcast(hr, jnp.int32)
    return (bits[:, c:] & jnp.int32(-65536)) | lax.shift_right_logical(bits[:, :c], 16)


def _unpack_bf16_pairs(packed):
    lo = pltpu.bitcast(lax.shift_left(packed, 16), F32)
    hi = pltpu.bitcast(packed & jnp.int32(-65536), F32)
    return jnp.concatenate([lo.astype(BF16), hi.astype(BF16)], axis=1)


def _route_tile(first, x, g_ref, wr_ref, tril_ref, ri_ref, rg_ref, cnt_ref, hp_ref, run_ref):
    @pl.when(first)
    def _():
        run_ref[...] = jnp.zeros_like(run_ref)

    h = _rms(x, g_ref[...])
    wr = wr_ref[...]
    ne = wr.shape[1]
    h_hi = h.astype(BF16)
    hp_ref[...] = _pack_bf16_pairs(h_hi.astype(F32))
    h_lo = (h - h_hi.astype(F32)).astype(BF16)
    w_hi = wr.astype(BF16)
    w_lo = (wr - w_hi.astype(F32)).astype(BF16)
    hi_part = jnp.dot(h_hi, jnp.concatenate([w_hi, w_lo], axis=1), preferred_element_type=F32)
    logits = hi_part[:, :ne] + hi_part[:, ne:] + jnp.dot(h_lo, w_hi, preferred_element_type=F32)
    tm = logits.shape[0]
    lane = lax.broadcasted_iota(jnp.int32, (tm, ne), 1)
    v1 = jnp.max(logits, axis=1, keepdims=True)
    i1 = jnp.min(jnp.where(logits == v1, lane, ne), axis=1, keepdims=True)
    rest = jnp.where(lane == i1, -jnp.inf, logits)
    v2 = jnp.max(rest, axis=1, keepdims=True)
    i2 = jnp.min(jnp.where(rest == v2, lane, ne), axis=1, keepdims=True)
    e = jnp.exp(v2 - v1)
    g1 = 1.0 / (1.0 + e)
    g2 = e / (1.0 + e)
    oh1 = (lane == i1).astype(F32)
    oh2 = (lane == i2).astype(F32)
    both = oh1 + oh2
    before = jnp.dot(tril_ref[...], both.astype(BF16), preferred_element_type=F32) + run_ref[...]
    r1 = jnp.sum(before * oh1, axis=1, keepdims=True).astype(jnp.int32)
    r2 = jnp.sum(before * oh2, axis=1, keepdims=True).astype(jnp.int32)
    run_ref[...] += jnp.sum(both, axis=0, keepdims=True)
    wide = lax.broadcasted_iota(jnp.int32, (tm, LANES), 1)
    fields = jnp.where(wide == 0, i1, jnp.where(wide == 1, i2, jnp.where(wide == 2, r1, r2)))
    ri_ref[...] = fields.T[:ri_ref.shape[0]]
    rg_ref[...] = jnp.where(lane == 0, g1, g2)
    cnt_ref[...] = run_ref[...]


SC_WINDOW = 64


def _sc_workers():
    info = plsc.get_sparse_core_info()
    mesh = plsc.VectorSubcoreMesh(core_axis_name="core", subcore_axis_name="subcore")
    return mesh, info.num_cores, info.num_cores * info.num_subcores


def _dispatch(x, dest, cap):
    n, d = x.shape
    mesh, n_cores, n_workers = _sc_workers()
    per_worker = n // n_workers

    n_windows = per_worker // SC_WINDOW
    n_bufs = 2
    assert n_windows % n_bufs == 0

    @functools.partial(
        pl.kernel, mesh=mesh, out_type=jax.ShapeDtypeStruct((cap, d), x.dtype),
        scratch_types=[pltpu.VMEM((SC_WINDOW,), jnp.int32)] * (TOP_K * n_bufs)
        + [pltpu.VMEM((SC_WINDOW, d), x.dtype)] * n_bufs + [pltpu.SemaphoreType.DMA] * n_bufs,
        name="moe_dispatch")
    def scatter(x_hbm, *rest):
        idx_hbm, (xs_hbm, *scratch) = rest[:TOP_K], rest[TOP_K:]
        idx_vmem = [scratch[b * TOP_K:(b + 1) * TOP_K] for b in range(n_bufs)]
        rows = scratch[TOP_K * n_bufs:TOP_K * n_bufs + n_bufs]
        sems = scratch[TOP_K * n_bufs + n_bufs:]
        wid = lax.axis_index("subcore") * n_cores + lax.axis_index("core")

        def drain(buf):
            for k in range(TOP_K):
                pltpu.make_async_copy(rows[buf], xs_hbm.at[idx_vmem[buf][k]], sems[buf]).wait()

        @pl.loop(0, n_windows, step=n_bufs)
        def _(c):
            for buf in range(n_bufs):
                @pl.when(c > 0)
                def _(buf=buf):
                    drain(buf)

                window = pl.ds(wid * per_worker + (c + buf) * SC_WINDOW, SC_WINDOW)
                for k in range(TOP_K):
                    pltpu.sync_copy(idx_hbm[k].at[window], idx_vmem[buf][k])
                pltpu.sync_copy(x_hbm.at[window], rows[buf])
                for k in range(TOP_K):
                    pltpu.async_copy(rows[buf], xs_hbm.at[idx_vmem[buf][k]], sems[buf])

        for buf in range(n_bufs):
            drain(buf)

    return scatter(x, *[dest[k] for k in range(TOP_K)])


def _gather_rows(table, idx):
    m = idx.shape[0]
    d = table.shape[1]
    mesh, n_cores, n_workers = _sc_workers()
    per_worker = m // n_workers

    n_windows = per_worker // SC_WINDOW
    n_bufs = 2
    assert n_windows % n_bufs == 0

    @functools.partial(
        pl.kernel, mesh=mesh, out_type=jax.ShapeDtypeStruct((m, d), table.dtype),
        scratch_types=[pltpu.VMEM((SC_WINDOW,), jnp.int32)] * n_bufs
        + [pltpu.VMEM((SC_WINDOW, d), table.dtype)] * n_bufs + [pltpu.SemaphoreType.DMA] * n_bufs,
        name="moe_gather")
    def gather(t_hbm, i_hbm, o_hbm, *scratch):
        idx_vmem, rows, sems = scratch[:n_bufs], scratch[n_bufs:2 * n_bufs], scratch[2 * n_bufs:]
        wid = lax.axis_index("subcore") * n_cores + lax.axis_index("core")

        def window(c):
            return pl.ds(wid * per_worker + c * SC_WINDOW, SC_WINDOW)

        def fetch(c, buf):
            pltpu.sync_copy(i_hbm.at[window(c)], idx_vmem[buf])
            return pltpu.async_copy(t_hbm.at[idx_vmem[buf]], rows[buf], sems[buf])

        fetch(0, 0)

        @pl.loop(0, n_windows, step=n_bufs)
        def _(c):
            for buf in range(n_bufs):
                nxt = c + buf + 1

                @pl.when(nxt < n_windows)
                def _(nxt=nxt, buf=buf):
                    fetch(nxt, (buf + 1) % n_bufs)

                pltpu.make_async_copy(t_hbm.at[idx_vmem[buf]], rows[buf], sems[buf]).wait()
                pltpu.sync_copy(rows[buf], o_hbm.at[window(c + buf)])

    return gather(table, idx)


def _expert_kernel(be_ref, nv_ref, xs_ref, wg_ref, wu_ref, wd_ref, ys_ref, xb_ref, acc_ref):
    del be_ref
    b = pl.program_id(0)
    j = pl.program_id(1)

    def ff_tile(first):
        x = xb_ref[...]
        acts = []
        for c in range(wg_ref.shape[1] // MXU_DIM):
            cols = slice(c * MXU_DIM, (c + 1) * MXU_DIM)
            gate = jnp.dot(x, wg_ref[:, cols].astype(BF16), preferred_element_type=F32)
            up = jnp.dot(x, wu_ref[:, cols].astype(BF16), preferred_element_type=F32)
            acts.append((_silu(gate) * up).astype(BF16))
        part = jnp.dot(jnp.concatenate(acts, axis=1), wd_ref[...].astype(BF16), preferred_element_type=F32)
        if first:
            acc_ref[...] = part
        else:
            acc_ref[...] += part

    valid = b < nv_ref[0]

    @pl.when(valid & (j == 0))
    def _():
        xb_ref[...] = _unpack_bf16_pairs(xs_ref[...])
        ff_tile(True)

    @pl.when(valid & (j > 0))
    def _():
        ff_tile(False)

    @pl.when(jnp.logical_not(valid) & (j == 0))
    def _():
        acc_ref[...] = jnp.zeros_like(acc_ref)

    @pl.when(j == pl.num_programs(1) - 1)
    def _():
        ys_ref[...] = _pack_bf16_pairs(acc_ref[...].astype(BF16).astype(F32))


def _experts(xs, block_e, n_valid, w_gate, w_up, w_down, layer, tm, tf):
    cap = xs.shape[0]
    d = 2 * xs.shape[1]
    ff = w_gate.shape[3]
    nj = ff // tf

    def blk(b, nv):
        return jnp.minimum(b, nv[0] - 1)

    def ffi(b, j, nv):
        return jnp.where(b < nv[0], j, nj - 1)

    return pl.pallas_call(
        _expert_kernel,
        grid_spec=pltpu.PrefetchScalarGridSpec(
            num_scalar_prefetch=2,
            grid=(cap // tm, nj),
            in_specs=[
                pl.BlockSpec((tm, d // 2), lambda b, j, be, nv: (blk(b, nv), 0)),
                pl.BlockSpec((None, None, d, tf), lambda b, j, be, nv: (layer, be[blk(b, nv)], 0, ffi(b, j, nv))),
                pl.BlockSpec((None, None, d, tf), lambda b, j, be, nv: (layer, be[blk(b, nv)], 0, ffi(b, j, nv))),
                pl.BlockSpec((None, None, tf, d), lambda b, j, be, nv: (layer, be[blk(b, nv)], ffi(b, j, nv), 0)),
            ],
            out_specs=pl.BlockSpec((tm, d // 2), lambda b, j, be, nv: (b, 0)),
            scratch_shapes=[pltpu.VMEM((tm, d), BF16), pltpu.VMEM((tm, d), F32)],
        ),
        out_shape=jax.ShapeDtypeStruct((cap, d // 2), jnp.int32),
        compiler_params=_params("arbitrary", "arbitrary"),
        name="moe_experts",
    )(block_e, n_valid, xs, w_gate, w_up, w_down)


def _combine_kernel(x_ref, rg_ref, gf_ref, y0_ref, y1_ref, o_ref):
    rg = rg_ref[...]
    y0 = _unpack_bf16_pairs(y0_ref[...]).astype(F32)
    y1 = _unpack_bf16_pairs(y1_ref[...]).astype(F32)
    o_ref[...] = _rms(x_ref[...] + (rg[:, 0:1] * y0 + rg[:, 1:2] * y1), gf_ref[...])


def _combine_and_final_norm(x, route_g, yg, final_gain):
    n, d = x.shape
    tm = 1024
    return pl.pallas_call(
        _combine_kernel,
        grid=(n // tm,),
        in_specs=[
            pl.BlockSpec((tm, d), lambda i: (i, 0)),
            pl.BlockSpec((tm, ROUTE_COLS), lambda i: (i, 0)),
            pl.BlockSpec((1, d), lambda i: (0, 0)),
            pl.BlockSpec((None, tm, d // 2), lambda i: (0, i, 0)),
            pl.BlockSpec((None, tm, d // 2), lambda i: (1, i, 0)),
        ],
        out_specs=pl.BlockSpec((tm, d), lambda i: (i, 0)),
        out_shape=jax.ShapeDtypeStruct((n, d), F32),
        compiler_params=_params("parallel"),
        name="moe_combine",
    )(x, route_g, final_gain, yg, yg)


def _moe_layer(x, routing, w_gate, w_up, w_down, layer):
    n, d = x.shape
    tm = 1024 if n * TOP_K >= 8 * 1024 else 256
    tf = 2 * MXU_DIM
    route_i, route_g, counts, h_packed = routing

    counts = counts[0].astype(jnp.int32)
    padded = ((counts + tm - 1) // tm) * tm
    ends = jnp.cumsum(padded)
    starts = ends - padded
    experts, ranks = route_i[:TOP_K], route_i[TOP_K:2 * TOP_K]
    dest = ranks + sum(jnp.where(experts == e, starts[e], 0) for e in range(N_EXPERTS))
    n_blocks = (n * TOP_K) // tm + N_EXPERTS
    block_start = jnp.arange(n_blocks, dtype=jnp.int32) * tm
    block_e = jnp.minimum(jnp.sum(ends[None, :] <= block_start[:, None], axis=1), N_EXPERTS - 1).astype(jnp.int32)
    n_valid = (ends[-1:] // tm).astype(jnp.int32)

    xs = _dispatch(h_packed, dest, n_blocks * tm)
    ys = _experts(xs, block_e, n_valid, w_gate, w_up, w_down, layer, tm, tf)
    return route_g, _gather_rows(ys, dest.reshape(TOP_K * n)).reshape(TOP_K, n, d // 2)


def _rope_tables(seq):
    pos = jnp.arange(seq, dtype=F32)
    inv_freq = ROPE_THETA ** (-jnp.arange(0, HEAD_DIM, 2, dtype=F32) / HEAD_DIM)
    ang = pos[:, None] * inv_freq[None, :]
    reps = LANES // (HEAD_DIM // 2)
    cos = jnp.tile(jnp.cos(ang), (1, reps))
    sin = jnp.tile(jnp.sin(ang), (1, reps))
    first_half = (jnp.arange(LANES) % HEAD_DIM) < HEAD_DIM // 2
    tabs = (cos, jnp.where(first_half, -sin, sin))
    q_scale = HEAD_DIM ** -0.5 * 1.4426950408889634
    return tuple(jnp.stack([t * q_scale, t]) for t in tabs)


def kernel(x, mix_norm, ffn_norm, attn_w_qkv, attn_w_o, pool_w, pool_scale, dense_w_gate, dense_w_up,
           dense_w_down, moe_router, moe_w_gate, moe_w_up, moe_w_down, final_norm):
    batch, seq, d = x.shape
    depth = mix_norm.shape[0]
    assert depth % 2 == 0 and d == N_HEADS * HEAD_DIM
    assert all(seq % (dil * (win // dil)) == 0 for win, dil in zip(WINDOWS, DILATIONS))
    assert seq % QKV_ROWS == 0 and seq % POOL_ROWS == 0 and moe_router.shape[2] == N_EXPERTS
    rope_tables = _rope_tables(seq)
    xf = x.reshape(batch * seq, d)
    final_gain = final_norm.reshape(1, d)
    pending_moe = None
    for i in range(depth):
        j = i // 2
        mix_gain = mix_norm[i].reshape(1, d)
        ffn_gain = ffn_norm[i].reshape(1, d)
        if i % 2 == 0:
            w_qkv = attn_w_qkv[j].astype(BF16)
            qkv, xf = _qkv_project(xf, mix_gain, w_qkv, rope_tables, batch, seq, pending_moe)
            pending_moe = None
            xf = _attention_and_dense_ffn(qkv, xf, attn_w_o[j].astype(BF16), ffn_gain, dense_w_gate[j].astype(BF16),
                                          dense_w_up[j].astype(BF16), dense_w_down[j].astype(BF16), batch, seq)
        else:
            xf, *routing = _pool_mixer_and_router(xf, mix_gain, pool_w[j].astype(BF16), pool_scale[j].reshape(1, d),
                                                  ffn_gain, moe_router[j], batch, seq)
            pending_moe = _moe_layer(xf, routing, moe_w_gate, moe_w_up, moe_w_down, j)
    return _combine_and_final_norm(xf, *pending_moe, final_gain).reshape(batch, seq, d)
```

```python
import functools

import jax
import jax.numpy as jnp
from jax import lax
from jax.experimental import pallas as pl
from jax.experimental.pallas import tpu as pltpu
from jax.experimental.pallas import tpu_sc as plsc

WINDOWS = (128, 512, 2048)
DILATIONS = (1, 4, 16)
N_GROUPS = len(WINDOWS)
N_HEADS = 16
HEAD_DIM = 64
ROPE_THETA = 10000.0
POOL_SIZES = (2, 4, 8, 16)
N_EXPERTS = 8
TOP_K = 2
RMS_EPS = 1e-6
MASK_VALUE = -1e30
QKV_ROWS = 512
POOL_ROWS = 1024

LANES = 128
MXU_DIM = 256
VMEM_LIMIT_BYTES = 56 * 1024 * 1024

F32 = jnp.float32
BF16 = jnp.bfloat16


def _params(*sem):
    return pltpu.CompilerParams(dimension_semantics=sem, vmem_limit_bytes=VMEM_LIMIT_BYTES)


def _rms(x, g):
    ms = jnp.mean(x * x, axis=-1, keepdims=True)
    return x * lax.rsqrt(ms + RMS_EPS) * g


def _silu(g):
    return g / (1.0 + jnp.exp(-g))


def _qkv_kernel(x_ref, g_ref, w_ref, cos_ref, sin_ref, *rest, pending_moe):
    if pending_moe:
        rg_ref, y0_ref, y1_ref, o0_ref, o1_ref, o2_ref, xn_ref, acc_ref = rest
        rg = rg_ref[...]
        x = x_ref[...] + (rg[:, 0:1] * _unpack_bf16_pairs(y0_ref[...]).astype(F32)
                          + rg[:, 1:2] * _unpack_bf16_pairs(y1_ref[...]).astype(F32))
        xn_ref[...] = x
    else:
        o0_ref, o1_ref, o2_ref, acc_ref = rest
        x = x_ref[...]
    tm, d = x_ref.shape
    n_chunks = acc_ref.shape[1]
    lane = lax.broadcasted_iota(jnp.int32, (1, LANES), 1)
    first_half = (lane % HEAD_DIM) < HEAD_DIM // 2
    h = _rms(x, g_ref[...]).astype(BF16)
    for g, (o_ref, dil) in enumerate(zip((o0_ref, o1_ref, o2_ref), DILATIONS)):
        rows = tm // dil

        def strided(ref, r, dil=dil, rows=rows):
            if dil == 1:
                return ref[...]
            return ref[pl.ds(r, rows, stride=dil), :]

        for kind in range(3):
            col0 = (g * 3 + kind) * d
            acc = jnp.dot(h, w_ref[:, col0:col0 + d], preferred_element_type=F32)
            for c in range(n_chunks):
                acc_ref[kind, c] = acc[:, c * LANES:(c + 1) * LANES]
            for r in range(dil):
                if kind < 2:
                    cos = strided(cos_ref.at[kind], r)
                    sin = strided(sin_ref.at[kind], r)
                for c in range(n_chunks):
                    t = strided(acc_ref.at[kind, c], r)
                    if kind < 2:
                        partner = jnp.where(first_half, pltpu.roll(t, LANES - HEAD_DIM // 2, 1),
                                            pltpu.roll(t, HEAD_DIM // 2, 1))
                        t = t * cos + partner * sin
                    o_ref[kind, r, :, c * LANES:(c + 1) * LANES] = t.astype(BF16)


def _qkv_project(x, gain, w_qkv, layer, tables, batch, seq, pending_moe=None):
    n, d = x.shape
    tm = QKV_ROWS // 2 if pending_moe else QKV_ROWS
    tiles_per_seq = seq // tm
    tab = pl.BlockSpec((2, tm, LANES), lambda i: (0, i % tiles_per_seq, 0))

    def o_spec(dil):
        return pl.BlockSpec((3, None, dil, tm // dil, d),
                            lambda i: (0, i // tiles_per_seq, 0, i % tiles_per_seq, 0))

    row = pl.BlockSpec((tm, d), lambda i: (i, 0))
    in_specs = [row, pl.BlockSpec((1, d), lambda i: (0, 0)),
                pl.BlockSpec((None,) + w_qkv.shape[1:], lambda i: (layer, 0, 0), pipeline_mode=pl.Buffered(1)),
                tab, tab]
    args = [x, gain, w_qkv, *tables]
    out_specs = [o_spec(dil) for dil in DILATIONS]
    out_shape = [jax.ShapeDtypeStruct((3, batch, dil, seq // dil, d), BF16) for dil in DILATIONS]
    if pending_moe:
        route_g, yg = pending_moe
        in_specs += [pl.BlockSpec((tm, ROUTE_COLS), lambda i: (i, 0))]
        in_specs += [pl.BlockSpec((None, tm, d // 2), lambda i, k=k: (k, i, 0)) for k in range(TOP_K)]
        args += [route_g, yg, yg]
        out_specs.append(row)
        out_shape.append(jax.ShapeDtypeStruct((n, d), F32))
    outs = pl.pallas_call(
        functools.partial(_qkv_kernel, pending_moe=bool(pending_moe)),
        grid=(n // tm,),
        in_specs=in_specs,
        out_specs=out_specs,
        out_shape=out_shape,
        scratch_shapes=[pltpu.VMEM((3, d // LANES, tm, LANES), F32)],
        compiler_params=_params("parallel"),
        name="qkv_rope",
    )(*args)
    return [o.reshape(3, batch, seq, d) for o in outs[:N_GROUPS]], (outs[N_GROUPS] if pending_moe else x)


ATTN_TILES_PER_FFN_PIECE = 3


def _attn_dense_kernel(q0, k0, v0, q1, k1, v1, q2, k2, v2, band_ref, causal_ref,
                       x_ref, wo_ref, g_ref, wg_ref, wu_ref, wd_ref, y_ref,
                       o_scr, num_ref, max_ref, den_ref, kt_ref):
    b = pl.program_id(0)
    hp = pl.program_id(1)
    n_pairs, seq = o_scr.shape[1], o_scr.shape[2]
    w = WINDOWS[0] // DILATIONS[0]
    lane = lax.broadcasted_iota(jnp.int32, (1, LANES), 1)
    first_head = lane < HEAD_DIM
    o_out = o_scr.at[b % 2, hp]
    o_in = o_scr.at[(b + 1) % 2]

    def ffn_pieces():
        rows = pl.ds(pl.multiple_of(hp * x_ref.shape[0], x_ref.shape[0]), x_ref.shape[0])
        o = jnp.concatenate([o_in[p, rows, :] for p in range(n_pairs)], axis=1)
        x = x_ref[...] + jnp.dot(o, wo_ref[...], preferred_element_type=F32)
        h = _rms(x, g_ref[...]).astype(BF16)
        yield
        acts = []
        n_chunks = wg_ref.shape[1] // MXU_DIM
        for c in range(n_chunks):
            cols = slice(c * MXU_DIM, (c + 1) * MXU_DIM)
            gate = jnp.dot(h, wg_ref[:, cols], preferred_element_type=F32)
            up = jnp.dot(h, wu_ref[:, cols], preferred_element_type=F32)
            acts.append((_silu(gate) * up).astype(BF16))
            yield
        half = n_chunks // 2
        acc = jnp.dot(jnp.concatenate(acts[:half], axis=1), wd_ref[:half * MXU_DIM, :], preferred_element_type=F32)
        yield
        acc = acc + jnp.dot(jnp.concatenate(acts[half:], axis=1), wd_ref[half * MXU_DIM:, :],
                            preferred_element_type=F32)
        y_ref[...] = x + acc
        yield

    def tile(q_ref, v_ref, qoff, has_prev):
        q = q_ref[qoff:qoff + w, :]
        zero = jnp.zeros_like(q)
        qq = jnp.concatenate([jnp.where(first_head, q, zero), jnp.where(first_head, zero, q)], axis=0)
        k_lo = qoff - w if has_prev else qoff
        s = jnp.dot(qq, kt_ref[:, k_lo:qoff + w], preferred_element_type=F32)
        s = s + (band_ref[...] if has_prev else causal_ref[...])
        mx = jnp.max(s, axis=1, keepdims=True)
        p = jnp.exp2(s - mx)
        den = jnp.sum(p, axis=1, keepdims=True)
        pv = jnp.dot(p.astype(BF16), v_ref[k_lo:qoff + w, :], preferred_element_type=F32)
        return tuple(jnp.where(first_head, a[:w], a[w:]) for a in (pv, mx, den))

    def attention(with_ffn):
        pieces = ffn_pieces() if with_ffn else iter(())
        next(pieces, None)
        tiles_done = 0
        order = sorted(range(N_GROUPS), key=lambda g: -DILATIONS[g])
        for g in order:
            q_ref, k_ref, v_ref = ((q0, k0, v0), (q1, k1, v1), (q2, k2, v2))[g]
            dil = DILATIONS[g]
            sub_len = seq // dil
            nb = sub_len // w
            kt_ref[...] = k_ref[...].T
            for t in range(dil * nb):
                r, n = divmod(t, nb)
                qoff = r * sub_len + n * w
                pv, mx, den = tile(q_ref, v_ref, qoff, n > 0)
                tiles_done += 1
                if tiles_done % ATTN_TILES_PER_FFN_PIECE == 0:
                    next(pieces, None)
                rows = pl.ds(qoff, w) if dil == 1 else pl.ds(r + dil * n * w, w, stride=dil)
                if g == order[0]:
                    num_ref[rows, :] = pv
                    max_ref[rows, :] = mx
                    den_ref[rows, :] = den
                    continue
                old_max = max_ref[rows, :]
                new_max = jnp.maximum(old_max, mx)
                a = jnp.exp2(old_max - new_max)
                c = jnp.exp2(mx - new_max)
                num = a * num_ref[rows, :] + c * pv
                den = a * den_ref[rows, :] + c * den
                if g == order[-1]:
                    o_out[rows, :] = (num / den).astype(BF16)
                else:
                    num_ref[rows, :] = num
                    den_ref[rows, :] = den
                    max_ref[rows, :] = new_max
        for _ in pieces:
            pass

    n_batches = pl.num_programs(0) - 1
    @pl.when(b == 0)
    def _():
        y_ref[...] = jnp.zeros_like(y_ref)
        attention(False)

    pl.when((b > 0) & (b < n_batches))(functools.partial(attention, True))

    @pl.when(b == n_batches)
    def _():
        for _ in ffn_pieces():
            pass


def _attention_biases(w):
    row = jnp.arange(2 * w)[:, None] % w
    col = jnp.arange(2 * w)[None, :]
    dist = row + w - col
    band = jnp.where((dist >= 0) & (dist <= w), 0.0, MASK_VALUE).astype(F32)
    causal = jnp.where(row >= col[:, :w], 0.0, MASK_VALUE).astype(F32)
    return band, causal


def _attention_and_dense_ffn(qkv_groups, x, w_o, gain, w_gate, w_up, w_down, layer, batch, seq):
    n, d = x.shape
    ff = w_gate.shape[2]
    n_pairs = d // LANES
    rows = seq // n_pairs
    last = batch - 1
    in_specs, args = [], []
    for arr in qkv_groups:
        for kind in range(3):
            in_specs.append(pl.BlockSpec(
                (None, None, seq, LANES),
                lambda b, hp, kind=kind: (kind, jnp.minimum(b, last), 0, jnp.where(b > last, n_pairs - 1, hp))))
            args.append(arr)
    const = lambda shape: pl.BlockSpec(shape, lambda b, hp: (0, 0), pipeline_mode=pl.Buffered(1))
    ffn_in = pl.BlockSpec((rows, d), lambda b, hp: (jnp.maximum(b - 1, 0) * n_pairs + hp, 0))
    ffn_out = pl.BlockSpec((rows, d), lambda b, hp: (jnp.where(b == 0, batch, b - 1) * n_pairs + hp, 0))
    biases = _attention_biases(WINDOWS[0] // DILATIONS[0])
    in_specs += [const(t.shape) for t in biases]
    per_layer = lambda shape: pl.BlockSpec((None,) + shape, lambda b, hp: (layer, 0, 0), pipeline_mode=pl.Buffered(1))
    in_specs += [ffn_in, per_layer((d, d)), const((1, d)), per_layer((d, ff)), per_layer((d, ff)), per_layer((ff, d))]
    args += [*biases, x, w_o, gain, w_gate, w_up, w_down]
    return pl.pallas_call(
        _attn_dense_kernel,
        grid=(batch + 1, n_pairs),
        in_specs=in_specs,
        out_specs=ffn_out,
        out_shape=jax.ShapeDtypeStruct((n + seq, d), F32),
        scratch_shapes=[pltpu.VMEM((2, n_pairs, seq, LANES), BF16)]
        + [pltpu.VMEM((seq, LANES), F32)] * 3 + [pltpu.VMEM((LANES, seq), BF16)],
        compiler_params=_params("arbitrary", "arbitrary"),
        name="attention_ffn",
    )(*args)


POOL_HALO = max(POOL_SIZES)


def _pool_tile(i, x_ref, halo_ref, g_ref, w_ref, sc_ref, y_ref):
    ts = x_ref.shape[0]
    g = g_ref[...]
    x = x_ref[...]
    h = _rms(x, g)
    hh = jnp.where(i > 0, _rms(halo_ref[...], g), 0.0)
    ext = jnp.concatenate([hh, h], axis=0)
    pos = i * ts + lax.broadcasted_iota(jnp.int32, (ts, 1), 0)
    pc = w_ref.shape[1]
    for grp, size in enumerate(POOL_SIZES):
        sl = slice(grp * pc, (grp + 1) * pc)
        s = ext[:, sl]
        step = 1
        while step < size:
            s = s + pltpu.roll(s, step, 0)
            step *= 2
        cnt = jnp.minimum(pos + 1, size).astype(F32)
        y = s[POOL_HALO:, :] / cnt - h[:, sl]
        z = jnp.dot(y.astype(BF16), w_ref[grp], preferred_element_type=F32)
        y_ref[:, sl] = x[:, sl] + z * sc_ref[:, sl]


def _pool_router_kernel(x_ref, halo_ref, g_ref, w_ref, sc_ref, g2_ref, wr_ref, tril_ref,
                        y_ref, ri_ref, rg_ref, cnt_ref, hp_ref, run_ref):
    b = pl.program_id(0)
    i = pl.program_id(1)
    _pool_tile(i, x_ref, halo_ref, g_ref, w_ref, sc_ref, y_ref)
    _route_tile((b == 0) & (i == 0), y_ref[...], g2_ref, wr_ref, tril_ref, ri_ref, rg_ref, cnt_ref, hp_ref, run_ref)


def _pool_mixer_and_router(x, gain, pool_w, layer, scale, ffn_gain, w_router, batch, seq):
    d = x.shape[1]
    n = batch * seq
    ne = w_router.shape[1]
    ts = POOL_ROWS
    tiles = seq // ts
    x3 = x.reshape(-1, seq, d)
    hb = ts // POOL_HALO
    tril = (jnp.arange(ts)[:, None] > jnp.arange(ts)[None, :]).astype(BF16)
    const = lambda shape: pl.BlockSpec(shape, lambda b, i: (0,) * len(shape))
    tok = lambda cols: pl.BlockSpec((ts, cols), lambda b, i: (b * tiles + i, 0))
    out, route_i, route_g, counts, h_packed = pl.pallas_call(
        _pool_router_kernel,
        grid=(batch, tiles),
        in_specs=[
            pl.BlockSpec((None, ts, d), lambda b, i: (b, i, 0)),
            pl.BlockSpec((None, POOL_HALO, d), lambda b, i: (b, jnp.maximum(i * hb - 1, 0), 0)),
            const((1, d)), pl.BlockSpec((None,) + pool_w.shape[1:], lambda b, i: (layer, 0, 0, 0)),
            const((1, d)), const((1, d)), const((d, ne)), const((ts, ts)),
        ],
        out_specs=[
            pl.BlockSpec((None, ts, d), lambda b, i: (b, i, 0)),
            pl.BlockSpec((ROUTE_COLS, ts), lambda b, i: (0, b * tiles + i)),
            tok(ROUTE_COLS), const((1, ne)), tok(d // 2),
        ],
        out_shape=[
            jax.ShapeDtypeStruct((batch, seq, d), F32),
            jax.ShapeDtypeStruct((ROUTE_COLS, n), jnp.int32),
            jax.ShapeDtypeStruct((n, ROUTE_COLS), F32),
            jax.ShapeDtypeStruct((1, ne), F32),
            jax.ShapeDtypeStruct((n, d // 2), jnp.int32),
        ],
        scratch_shapes=[pltpu.VMEM((1, ne), F32)],
        compiler_params=_params("arbitrary", "arbitrary"),
        name="pool_router",
    )(x3, x3, gain, pool_w, scale, ffn_gain, w_router, tril)
    return out.reshape(n, d), route_i, route_g, counts, h_packed


ROUTE_COLS = 8


def _pack_bf16_pairs(hr):
    c = hr.shape[1] // 2
    bits = pltpu.bitcast(hr, jnp.int32)
    return (bits[:, c:] & jnp.int32(-65536)) | lax.shift_right_logical(bits[:, :c], 16)


def _unpack_bf16_pairs(packed):
    lo = pltpu.bitcast(lax.shift_left(packed, 16), F32)
    hi = pltpu.bitcast(packed & jnp.int32(-65536), F32)
    return jnp.concatenate([lo.astype(BF16), hi.astype(BF16)], axis=1)


def _route_tile(first, x, g_ref, wr_ref, tril_ref, ri_ref, rg_ref, cnt_ref, hp_ref, run_ref):
    @pl.when(first)
    def _():
        run_ref[...] = jnp.zeros_like(run_ref)

    h = _rms(x, g_ref[...])
    wr = wr_ref[...]
    ne = wr.shape[1]
    h_hi = h.astype(BF16)
    hp_ref[...] = _pack_bf16_pairs(h_hi.astype(F32))
    h_lo = (h - h_hi.astype(F32)).astype(BF16)
    w_hi = wr.astype(BF16)
    w_lo = (wr - w_hi.astype(F32)).astype(BF16)
    hi_part = jnp.dot(h_hi, jnp.concatenate([w_hi, w_lo], axis=1), preferred_element_type=F32)
    logits = hi_part[:, :ne] + hi_part[:, ne:] + jnp.dot(h_lo, w_hi, preferred_element_type=F32)
    tm = logits.shape[0]
    lane = lax.broadcasted_iota(jnp.int32, (tm, ne), 1)
    v1 = jnp.max(logits, axis=1, keepdims=True)
    i1 = jnp.min(jnp.where(logits == v1, lane, ne), axis=1, keepdims=True)
    rest = jnp.where(lane == i1, -jnp.inf, logits)
    v2 = jnp.max(rest, axis=1, keepdims=True)
    i2 = jnp.min(jnp.where(rest == v2, lane, ne), axis=1, keepdims=True)
    e = jnp.exp(v2 - v1)
    g1 = 1.0 / (1.0 + e)
    g2 = e / (1.0 + e)
    oh1 = (lane == i1).astype(F32)
    oh2 = (lane == i2).astype(F32)
    both = oh1 + oh2
    before = jnp.dot(tril_ref[...], both.astype(BF16), preferred_element_type=F32) + run_ref[...]
    r1 = jnp.sum(before * oh1, axis=1, keepdims=True).astype(jnp.int32)
    r2 = jnp.sum(before * oh2, axis=1, keepdims=True).astype(jnp.int32)
    run_ref[...] += jnp.sum(both, axis=0, keepdims=True)
    wide = lax.broadcasted_iota(jnp.int32, (tm, LANES), 1)
    fields = jnp.where(wide == 0, i1, jnp.where(wide == 1, i2, jnp.where(wide == 2, r1, r2)))
    ri_ref[...] = fields.T[:ri_ref.shape[0]]
    rg_ref[...] = jnp.where(lane == 0, g1, g2)
    cnt_ref[...] = run_ref[...]


SC_WINDOW = 64


def _sc_workers():
    info = plsc.get_sparse_core_info()
    mesh = plsc.VectorSubcoreMesh(core_axis_name="core", subcore_axis_name="subcore")
    return mesh, info.num_cores, info.num_cores * info.num_subcores


def _dispatch(x, dest, cap):
    n, d = x.shape
    mesh, n_cores, n_workers = _sc_workers()
    per_worker = n // n_workers

    n_windows = per_worker // SC_WINDOW
    n_bufs = 2
    assert n_windows % n_bufs == 0

    @functools.partial(
        pl.kernel, mesh=mesh, out_type=jax.ShapeDtypeStruct((cap, d), x.dtype),
        scratch_types=[pltpu.VMEM((SC_WINDOW,), jnp.int32)] * (TOP_K * n_bufs)
        + [pltpu.VMEM((SC_WINDOW, d), x.dtype)] * n_bufs + [pltpu.SemaphoreType.DMA] * n_bufs,
        name="moe_dispatch")
    def scatter(x_hbm, *rest):
        idx_hbm, (xs_hbm, *scratch) = rest[:TOP_K], rest[TOP_K:]
        idx_vmem = [scratch[b * TOP_K:(b + 1) * TOP_K] for b in range(n_bufs)]
        rows = scratch[TOP_K * n_bufs:TOP_K * n_bufs + n_bufs]
        sems = scratch[TOP_K * n_bufs + n_bufs:]
        wid = lax.axis_index("subcore") * n_cores + lax.axis_index("core")

        def drain(buf):
            for k in range(TOP_K):
                pltpu.make_async_copy(rows[buf], xs_hbm.at[idx_vmem[buf][k]], sems[buf]).wait()

        @pl.loop(0, n_windows, step=n_bufs)
        def _(c):
            for buf in range(n_bufs):
                @pl.when(c > 0)
                def _(buf=buf):
                    drain(buf)

                window = pl.ds(wid * per_worker + (c + buf) * SC_WINDOW, SC_WINDOW)
                for k in range(TOP_K):
                    pltpu.sync_copy(idx_hbm[k].at[window], idx_vmem[buf][k])
                pltpu.sync_copy(x_hbm.at[window], rows[buf])
                for k in range(TOP_K):
                    pltpu.async_copy(rows[buf], xs_hbm.at[idx_vmem[buf][k]], sems[buf])

        for buf in range(n_bufs):
            drain(buf)

    return scatter(x, *[dest[k] for k in range(TOP_K)])


def _gather_rows(table, idx):
    m = idx.shape[0]
    d = table.shape[1]
    mesh, n_cores, n_workers = _sc_workers()
    per_worker = m // n_workers

    n_windows = per_worker // SC_WINDOW
    n_bufs = 2
    assert n_windows % n_bufs == 0

    @functools.partial(
        pl.kernel, mesh=mesh, out_type=jax.ShapeDtypeStruct((m, d), table.dtype),
        scratch_types=[pltpu.VMEM((SC_WINDOW,), jnp.int32)] * n_bufs
        + [pltpu.VMEM((SC_WINDOW, d), table.dtype)] * n_bufs + [pltpu.SemaphoreType.DMA] * n_bufs,
        name="moe_gather")
    def gather(t_hbm, i_hbm, o_hbm, *scratch):
        idx_vmem, rows, sems = scratch[:n_bufs], scratch[n_bufs:2 * n_bufs], scratch[2 * n_bufs:]
        wid = lax.axis_index("subcore") * n_cores + lax.axis_index("core")

        def window(c):
            return pl.ds(wid * per_worker + c * SC_WINDOW, SC_WINDOW)

        def fetch(c, buf):
            pltpu.sync_copy(i_hbm.at[window(c)], idx_vmem[buf])
            return pltpu.async_copy(t_hbm.at[idx_vmem[buf]], rows[buf], sems[buf])

        fetch(0, 0)

        @pl.loop(0, n_windows, step=n_bufs)
        def _(c):
            for buf in range(n_bufs):
                nxt = c + buf + 1

                @pl.when(nxt < n_windows)
                def _(nxt=nxt, buf=buf):
                    fetch(nxt, (buf + 1) % n_bufs)

                pltpu.make_async_copy(t_hbm.at[idx_vmem[buf]], rows[buf], sems[buf]).wait()
                pltpu.sync_copy(rows[buf], o_hbm.at[window(c + buf)])

    return gather(table, idx)


def _expert_kernel(be_ref, nv_ref, xs_ref, wg_ref, wu_ref, wd_ref, ys_ref, xb_ref, acc_ref):
    del be_ref
    b = pl.program_id(0)
    j = pl.program_id(1)

    def ff_tile(first):
        x = xb_ref[...]
        acts = []
        for c in range(wg_ref.shape[1] // MXU_DIM):
            cols = slice(c * MXU_DIM, (c + 1) * MXU_DIM)
            gate = jnp.dot(x, wg_ref[:, cols].astype(BF16), preferred_element_type=F32)
            up = jnp.dot(x, wu_ref[:, cols].astype(BF16), preferred_element_type=F32)
            acts.append((_silu(gate) * up).astype(BF16))
        part = jnp.dot(jnp.concatenate(acts, axis=1), wd_ref[...].astype(BF16), preferred_element_type=F32)
        if first:
            acc_ref[...] = part
        else:
            acc_ref[...] += part

    valid = b < nv_ref[0]

    @pl.when(valid & (j == 0))
    def _():
        xb_ref[...] = _unpack_bf16_pairs(xs_ref[...])
        ff_tile(True)

    @pl.when(valid & (j > 0))
    def _():
        ff_tile(False)

    @pl.when(jnp.logical_not(valid) & (j == 0))
    def _():
        acc_ref[...] = jnp.zeros_like(acc_ref)

    @pl.when(j == pl.num_programs(1) - 1)
    def _():
        ys_ref[...] = _pack_bf16_pairs(acc_ref[...].astype(BF16).astype(F32))


def _experts(xs, block_e, n_valid, w_gate, w_up, w_down, layer, tm, tf):
    cap = xs.shape[0]
    d = 2 * xs.shape[1]
    ff = w_gate.shape[3]
    nj = ff // tf

    def blk(b, nv):
        return jnp.minimum(b, nv[0] - 1)

    def ffi(b, j, nv):
        return jnp.where(b < nv[0], j, nj - 1)

    return pl.pallas_call(
        _expert_kernel,
        grid_spec=pltpu.PrefetchScalarGridSpec(
            num_scalar_prefetch=2,
            grid=(cap // tm, nj),
            in_specs=[
                pl.BlockSpec((tm, d // 2), lambda b, j, be, nv: (blk(b, nv), 0)),
                pl.BlockSpec((None, None, d, tf), lambda b, j, be, nv: (layer, be[blk(b, nv)], 0, ffi(b, j, nv))),
                pl.BlockSpec((None, None, d, tf), lambda b, j, be, nv: (layer, be[blk(b, nv)], 0, ffi(b, j, nv))),
                pl.BlockSpec((None, None, tf, d), lambda b, j, be, nv: (layer, be[blk(b, nv)], ffi(b, j, nv), 0)),
            ],
            out_specs=pl.BlockSpec((tm, d // 2), lambda b, j, be, nv: (b, 0)),
            scratch_shapes=[pltpu.VMEM((tm, d), BF16), pltpu.VMEM((tm, d), F32)],
        ),
        out_shape=jax.ShapeDtypeStruct((cap, d // 2), jnp.int32),
        compiler_params=_params("arbitrary", "arbitrary"),
        name="moe_experts",
    )(block_e, n_valid, xs, w_gate, w_up, w_down)


def _combine_kernel(x_ref, rg_ref, gf_ref, y0_ref, y1_ref, o_ref):
    rg = rg_ref[...]
    y0 = _unpack_bf16_pairs(y0_ref[...]).astype(F32)
    y1 = _unpack_bf16_pairs(y1_ref[...]).astype(F32)
    o_ref[...] = _rms(x_ref[...] + (rg[:, 0:1] * y0 + rg[:, 1:2] * y1), gf_ref[...])


def _combine_and_final_norm(x, route_g, yg, final_gain):
    n, d = x.shape
    tm = 1024
    return pl.pallas_call(
        _combine_kernel,
        grid=(n // tm,),
        in_specs=[
            pl.BlockSpec((tm, d), lambda i: (i, 0)),
            pl.BlockSpec((tm, ROUTE_COLS), lambda i: (i, 0)),
            pl.BlockSpec((1, d), lambda i: (0, 0)),
            pl.BlockSpec((None, tm, d // 2), lambda i: (0, i, 0)),
            pl.BlockSpec((None, tm, d // 2), lambda i: (1, i, 0)),
        ],
        out_specs=pl.BlockSpec((tm, d), lambda i: (i, 0)),
        out_shape=jax.ShapeDtypeStruct((n, d), F32),
        compiler_params=_params("parallel"),
        name="moe_combine",
    )(x, route_g, final_gain, yg, yg)


def _moe_layer(x, routing, w_gate, w_up, w_down, layer):
    n, d = x.shape
    tm = 1024 if n * TOP_K >= 8 * 1024 else 256
    tf = 2 * MXU_DIM
    route_i, route_g, counts, h_packed = routing

    counts = counts[0].astype(jnp.int32)
    padded = ((counts + tm - 1) // tm) * tm
    ends = jnp.cumsum(padded)
    starts = ends - padded
    experts, ranks = route_i[:TOP_K], route_i[TOP_K:2 * TOP_K]
    dest = ranks + sum(jnp.where(experts == e, starts[e], 0) for e in range(N_EXPERTS))
    n_blocks = (n * TOP_K) // tm + N_EXPERTS
    block_start = jnp.arange(n_blocks, dtype=jnp.int32) * tm
    block_e = jnp.minimum(jnp.sum(ends[None, :] <= block_start[:, None], axis=1), N_EXPERTS - 1).astype(jnp.int32)
    n_valid = (ends[-1:] // tm).astype(jnp.int32)

    xs = _dispatch(h_packed, dest, n_blocks * tm)
    ys = _experts(xs, block_e, n_valid, w_gate, w_up, w_down, layer, tm, tf)
    return route_g, _gather_rows(ys, dest.reshape(TOP_K * n)).reshape(TOP_K, n, d // 2)


def _rope_tables(seq):
    pos = jnp.arange(seq, dtype=F32)
    inv_freq = ROPE_THETA ** (-jnp.arange(0, HEAD_DIM, 2, dtype=F32) / HEAD_DIM)
    ang = pos[:, None] * inv_freq[None, :]
    reps = LANES // (HEAD_DIM // 2)
    cos = jnp.tile(jnp.cos(ang), (1, reps))
    sin = jnp.tile(jnp.sin(ang), (1, reps))
    first_half = (jnp.arange(LANES) % HEAD_DIM) < HEAD_DIM // 2
    tabs = (cos, jnp.where(first_half, -sin, sin))
    q_scale = HEAD_DIM ** -0.5 * 1.4426950408889634
    return tuple(jnp.stack([t * q_scale, t]) for t in tabs)


def kernel(x, mix_norm, ffn_norm, attn_w_qkv, attn_w_o, pool_w, pool_scale, dense_w_gate, dense_w_up,
           dense_w_down, moe_router, moe_w_gate, moe_w_up, moe_w_down, final_norm):
    batch, seq, d = x.shape
    depth = mix_norm.shape[0]
    assert depth % 2 == 0 and d == N_HEADS * HEAD_DIM
    assert all(seq % (dil * (win // dil)) == 0 for win, dil in zip(WINDOWS, DILATIONS))
    assert seq % QKV_ROWS == 0 and seq % POOL_ROWS == 0 and moe_router.shape[2] == N_EXPERTS
    rope_tables = _rope_tables(seq)
    xf = x.reshape(batch * seq, d)
    final_gain = final_norm.reshape(1, d)
    pending_moe = None
    w_qkv, w_o, w_gate, w_up, w_down, w_pool = (
        w.astype(BF16) for w in (attn_w_qkv, attn_w_o, dense_w_gate, dense_w_up, dense_w_down, pool_w))
    for i in range(depth):
        j = i // 2
        mix_gain = mix_norm[i].reshape(1, d)
        ffn_gain = ffn_norm[i].reshape(1, d)
        if i % 2 == 0:
            qkv, xf = _qkv_project(xf, mix_gain, w_qkv, j, rope_tables, batch, seq, pending_moe)
            pending_moe = None
            xf = _attention_and_dense_ffn(qkv, xf, w_o, ffn_gain, w_gate, w_up, w_down, j, batch, seq)
        else:
            xf, *routing = _pool_mixer_and_router(xf, mix_gain, w_pool, j, pool_scale[j].reshape(1, d),
                                                  ffn_gain, moe_router[j], batch, seq)
            pending_moe = _moe_layer(xf, routing, moe_w_gate, moe_w_up, moe_w_down, j)
    return _combine_and_final_norm(xf, *pending_moe, final_gain).reshape(batch, seq, d)
```

```python
import functools

import jax
import jax.numpy as jnp
from jax import lax
from jax.experimental import pallas as pl
from jax.experimental.pallas import tpu as pltpu
from jax.experimental.pallas import tpu_sc as plsc

WINDOWS = (128, 512, 2048)
DILATIONS = (1, 4, 16)
N_GROUPS = len(WINDOWS)
N_HEADS = 16
HEAD_DIM = 64
ROPE_THETA = 10000.0
POOL_SIZES = (2, 4, 8, 16)
N_EXPERTS = 8
TOP_K = 2
RMS_EPS = 1e-6
MASK_VALUE = -1e30
QKV_ROWS = 512
POOL_ROWS = 1024

LANES = 128
MXU_DIM = 256
VMEM_LIMIT_BYTES = 56 * 1024 * 1024
VMEM_LIMIT_HIGH_BYTES = 60 * 1024 * 1024

F32 = jnp.float32
BF16 = jnp.bfloat16


def _params(*sem, vmem_limit_bytes=VMEM_LIMIT_BYTES):
    return pltpu.CompilerParams(dimension_semantics=sem, vmem_limit_bytes=vmem_limit_bytes)


def _rms(x, g):
    ms = jnp.mean(x * x, axis=-1, keepdims=True)
    return x * lax.rsqrt(ms + RMS_EPS) * g


def _silu(g):
    return g / (1.0 + jnp.exp(-g))


def _qkv_kernel(x_ref, g_ref, w_ref, cos_ref, sin_ref, *rest, pending_moe):
    if pending_moe:
        rg_ref, y0_ref, y1_ref, o0_ref, o1_ref, o2_ref, xn_ref, acc_ref = rest
        rg = rg_ref[...]
        x = x_ref[...] + (rg[:, 0:1] * _unpack_bf16_pairs(y0_ref[...]).astype(F32)
                          + rg[:, 1:2] * _unpack_bf16_pairs(y1_ref[...]).astype(F32))
        xn_ref[...] = x
    else:
        o0_ref, o1_ref, o2_ref, acc_ref = rest
        x = x_ref[...]
    tm, d = x_ref.shape
    n_chunks = acc_ref.shape[1]
    lane = lax.broadcasted_iota(jnp.int32, (1, LANES), 1)
    first_half = (lane % HEAD_DIM) < HEAD_DIM // 2
    h = _rms(x, g_ref[...]).astype(BF16)
    for g, (o_ref, dil) in enumerate(zip((o0_ref, o1_ref, o2_ref), DILATIONS)):
        rows = tm // dil

        def strided(ref, r, dil=dil, rows=rows):
            if dil == 1:
                return ref[...]
            return ref[pl.ds(r, rows, stride=dil), :]

        for kind in range(3):
            col0 = (g * 3 + kind) * d
            acc = jnp.dot(h, w_ref[:, col0:col0 + d], preferred_element_type=F32)
            for c in range(n_chunks):
                acc_ref[kind, c] = acc[:, c * LANES:(c + 1) * LANES]
            for r in range(dil):
                if kind < 2:
                    cos = strided(cos_ref.at[kind], r)
                    sin = strided(sin_ref.at[kind], r)
                for c in range(n_chunks):
                    t = strided(acc_ref.at[kind, c], r)
                    if kind < 2:
                        partner = jnp.where(first_half, pltpu.roll(t, LANES - HEAD_DIM // 2, 1),
                                            pltpu.roll(t, HEAD_DIM // 2, 1))
                        t = t * cos + partner * sin
                    o_ref[kind, r, :, c * LANES:(c + 1) * LANES] = t.astype(BF16)


def _qkv_project(x, gain, w_qkv, layer, tables, batch, seq, pending_moe=None):
    n, d = x.shape
    tm = QKV_ROWS
    vmem_limit = VMEM_LIMIT_HIGH_BYTES if pending_moe else VMEM_LIMIT_BYTES
    tiles_per_seq = seq // tm
    tab = pl.BlockSpec((2, tm, LANES), lambda i: (0, i % tiles_per_seq, 0))

    def o_spec(dil):
        return pl.BlockSpec((3, None, dil, tm // dil, d),
                            lambda i: (0, i // tiles_per_seq, 0, i % tiles_per_seq, 0))

    row = pl.BlockSpec((tm, d), lambda i: (i, 0))
    in_specs = [row, pl.BlockSpec((1, d), lambda i: (0, 0)),
                pl.BlockSpec((None,) + w_qkv.shape[1:], lambda i: (layer, 0, 0), pipeline_mode=pl.Buffered(1)),
                tab, tab]
    args = [x, gain, w_qkv, *tables]
    out_specs = [o_spec(dil) for dil in DILATIONS]
    out_shape = [jax.ShapeDtypeStruct((3, batch, dil, seq // dil, d), BF16) for dil in DILATIONS]
    if pending_moe:
        route_g, yg = pending_moe
        in_specs += [pl.BlockSpec((tm, ROUTE_COLS), lambda i: (i, 0))]
        in_specs += [pl.BlockSpec((None, tm, d // 2), lambda i, k=k: (k, i, 0)) for k in range(TOP_K)]
        args += [route_g, yg, yg]
        out_specs.append(row)
        out_shape.append(jax.ShapeDtypeStruct((n, d), F32))
    outs = pl.pallas_call(
        functools.partial(_qkv_kernel, pending_moe=bool(pending_moe)),
        grid=(n // tm,),
        in_specs=in_specs,
        out_specs=out_specs,
        out_shape=out_shape,
        scratch_shapes=[pltpu.VMEM((3, d // LANES, tm, LANES), F32)],
        compiler_params=_params("parallel", vmem_limit_bytes=vmem_limit),
        name="qkv_rope",
    )(*args)
    return [o.reshape(3, batch, seq, d) for o in outs[:N_GROUPS]], (outs[N_GROUPS] if pending_moe else x)


ATTN_TILES_PER_FFN_PIECE = 3


def _attn_dense_kernel(q0, k0, v0, q1, k1, v1, q2, k2, v2, band_ref, causal_ref,
                       x_ref, wo_ref, g_ref, wg_ref, wu_ref, wd_ref, y_ref,
                       o_scr, num_ref, max_ref, den_ref, kt_ref):
    b = pl.program_id(0)
    hp = pl.program_id(1)
    n_pairs, seq = o_scr.shape[1], o_scr.shape[2]
    w = WINDOWS[0] // DILATIONS[0]
    lane = lax.broadcasted_iota(jnp.int32, (1, LANES), 1)
    first_head = lane < HEAD_DIM
    o_out = o_scr.at[b % 2, hp]
    o_in = o_scr.at[(b + 1) % 2]

    def ffn_pieces():
        rows = pl.ds(pl.multiple_of(hp * x_ref.shape[0], x_ref.shape[0]), x_ref.shape[0])
        o = jnp.concatenate([o_in[p, rows, :] for p in range(n_pairs)], axis=1)
        x = x_ref[...] + jnp.dot(o, wo_ref[...], preferred_element_type=F32)
        h = _rms(x, g_ref[...]).astype(BF16)
        yield
        acts = []
        n_chunks = wg_ref.shape[1] // MXU_DIM
        for c in range(n_chunks):
            cols = slice(c * MXU_DIM, (c + 1) * MXU_DIM)
            gate = jnp.dot(h, wg_ref[:, cols], preferred_element_type=F32)
            up = jnp.dot(h, wu_ref[:, cols], preferred_element_type=F32)
            acts.append((_silu(gate) * up).astype(BF16))
            yield
        half = n_chunks // 2
        acc = jnp.dot(jnp.concatenate(acts[:half], axis=1), wd_ref[:half * MXU_DIM, :], preferred_element_type=F32)
        yield
        acc = acc + jnp.dot(jnp.concatenate(acts[half:], axis=1), wd_ref[half * MXU_DIM:, :],
                            preferred_element_type=F32)
        y_ref[...] = x + acc
        yield

    def tile(q_ref, v_ref, qoff, has_prev):
        q = q_ref[qoff:qoff + w, :]
        zero = jnp.zeros_like(q)
        qq = jnp.concatenate([jnp.where(first_head, q, zero), jnp.where(first_head, zero, q)], axis=0)
        k_lo = qoff - w if has_prev else qoff
        s = jnp.dot(qq, kt_ref[:, k_lo:qoff + w], preferred_element_type=F32)
        s = s + (band_ref[...] if has_prev else causal_ref[...])
        mx = jnp.max(s, axis=1, keepdims=True)
        p = jnp.exp2(s - mx)
        den = jnp.sum(p, axis=1, keepdims=True)
        pv = jnp.dot(p.astype(BF16), v_ref[k_lo:qoff + w, :], preferred_element_type=F32)
        return tuple(jnp.where(first_head, a[:w], a[w:]) for a in (pv, mx, den))

    def attention(with_ffn):
        pieces = ffn_pieces() if with_ffn else iter(())
        next(pieces, None)
        tiles_done = 0
        order = sorted(range(N_GROUPS), key=lambda g: -DILATIONS[g])
        for g in order:
            q_ref, k_ref, v_ref = ((q0, k0, v0), (q1, k1, v1), (q2, k2, v2))[g]
            dil = DILATIONS[g]
            sub_len = seq // dil
            nb = sub_len // w
            kt_ref[...] = k_ref[...].T
            for t in range(dil * nb):
                r, n = divmod(t, nb)
                qoff = r * sub_len + n * w
                pv, mx, den = tile(q_ref, v_ref, qoff, n > 0)
                tiles_done += 1
                if tiles_done % ATTN_TILES_PER_FFN_PIECE == 0:
                    next(pieces, None)
                rows = pl.ds(qoff, w) if dil == 1 else pl.ds(r + dil * n * w, w, stride=dil)
                if g == order[0]:
                    num_ref[rows, :] = pv
                    max_ref[rows, :] = mx
                    den_ref[rows, :] = den
                    continue
                old_max = max_ref[rows, :]
                new_max = jnp.maximum(old_max, mx)
                a = jnp.exp2(old_max - new_max)
                c = jnp.exp2(mx - new_max)
                num = a * num_ref[rows, :] + c * pv
                den = a * den_ref[rows, :] + c * den
                if g == order[-1]:
                    o_out[rows, :] = (num / den).astype(BF16)
                else:
                    num_ref[rows, :] = num
                    den_ref[rows, :] = den
                    max_ref[rows, :] = new_max
        for _ in pieces:
            pass

    n_batches = pl.num_programs(0) - 1
    @pl.when(b == 0)
    def _():
        y_ref[...] = jnp.zeros_like(y_ref)
        attention(False)

    pl.when((b > 0) & (b < n_batches))(functools.partial(attention, True))

    @pl.when(b == n_batches)
    def _():
        for _ in ffn_pieces():
            pass


def _attention_biases(w):
    row = jnp.arange(2 * w)[:, None] % w
    col = jnp.arange(2 * w)[None, :]
    dist = row + w - col
    band = jnp.where((dist >= 0) & (dist <= w), 0.0, MASK_VALUE).astype(F32)
    causal = jnp.where(row >= col[:, :w], 0.0, MASK_VALUE).astype(F32)
    return band, causal


def _attention_and_dense_ffn(qkv_groups, x, w_o, gain, w_gate, w_up, w_down, layer, batch, seq):
    n, d = x.shape
    ff = w_gate.shape[2]
    n_pairs = d // LANES
    rows = seq // n_pairs
    last = batch - 1
    in_specs, args = [], []
    for arr in qkv_groups:
        for kind in range(3):
            in_specs.append(pl.BlockSpec(
                (None, None, seq, LANES),
                lambda b, hp, kind=kind: (kind, jnp.minimum(b, last), 0, jnp.where(b > last, n_pairs - 1, hp))))
            args.append(arr)
    const = lambda shape: pl.BlockSpec(shape, lambda b, hp: (0, 0), pipeline_mode=pl.Buffered(1))
    ffn_in = pl.BlockSpec((rows, d), lambda b, hp: (jnp.maximum(b - 1, 0) * n_pairs + hp, 0))
    ffn_out = pl.BlockSpec((rows, d), lambda b, hp: (jnp.where(b == 0, batch, b - 1) * n_pairs + hp, 0))
    biases = _attention_biases(WINDOWS[0] // DILATIONS[0])
    in_specs += [const(t.shape) for t in biases]
    per_layer = lambda shape: pl.BlockSpec((None,) + shape, lambda b, hp: (layer, 0, 0), pipeline_mode=pl.Buffered(1))
    in_specs += [ffn_in, per_layer((d, d)), const((1, d)), per_layer((d, ff)), per_layer((d, ff)), per_layer((ff, d))]
    args += [*biases, x, w_o, gain, w_gate, w_up, w_down]
    return pl.pallas_call(
        _attn_dense_kernel,
        grid=(batch + 1, n_pairs),
        in_specs=in_specs,
        out_specs=ffn_out,
        out_shape=jax.ShapeDtypeStruct((n + seq, d), F32),
        scratch_shapes=[pltpu.VMEM((2, n_pairs, seq, LANES), BF16)]
        + [pltpu.VMEM((seq, LANES), F32)] * 3 + [pltpu.VMEM((LANES, seq), BF16)],
        compiler_params=_params("arbitrary", "arbitrary"),
        name="attention_ffn",
    )(*args)


POOL_HALO = max(POOL_SIZES)


def _pool_tile(i, x_ref, halo_ref, g_ref, w_ref, sc_ref, y_ref):
    ts = x_ref.shape[0]
    g = g_ref[...]
    x = x_ref[...]
    h = _rms(x, g)
    hh = jnp.where(i > 0, _rms(halo_ref[...], g), 0.0)
    ext = jnp.concatenate([hh, h], axis=0)
    pos = i * ts + lax.broadcasted_iota(jnp.int32, (ts, 1), 0)
    pc = w_ref.shape[1]
    for grp, size in enumerate(POOL_SIZES):
        sl = slice(grp * pc, (grp + 1) * pc)
        s = ext[:, sl]
        step = 1
        while step < size:
            s = s + pltpu.roll(s, step, 0)
            step *= 2
        cnt = jnp.minimum(pos + 1, size).astype(F32)
        y = s[POOL_HALO:, :] / cnt - h[:, sl]
        z = jnp.dot(y.astype(BF16), w_ref[grp], preferred_element_type=F32)
        y_ref[:, sl] = x[:, sl] + z * sc_ref[:, sl]


def _pool_router_kernel(x_ref, halo_ref, g_ref, w_ref, sc_ref, g2_ref, wr_ref, tril_ref,
                        y_ref, ri_ref, rg_ref, cnt_ref, hp_ref, run_ref):
    b = pl.program_id(0)
    i = pl.program_id(1)
    _pool_tile(i, x_ref, halo_ref, g_ref, w_ref, sc_ref, y_ref)
    _route_tile((b == 0) & (i == 0), y_ref[...], g2_ref, wr_ref, tril_ref, ri_ref, rg_ref, cnt_ref, hp_ref, run_ref)


def _pool_mixer_and_router(x, gain, pool_w, layer, scale, ffn_gain, w_router, batch, seq):
    d = x.shape[1]
    n = batch * seq
    ne = w_router.shape[1]
    ts = POOL_ROWS
    tiles = seq // ts
    x3 = x.reshape(-1, seq, d)
    hb = ts // POOL_HALO
    tril = (jnp.arange(ts)[:, None] > jnp.arange(ts)[None, :]).astype(BF16)
    const = lambda shape: pl.BlockSpec(shape, lambda b, i: (0,) * len(shape))
    tok = lambda cols: pl.BlockSpec((ts, cols), lambda b, i: (b * tiles + i, 0))
    out, route_i, route_g, counts, h_packed = pl.pallas_call(
        _pool_router_kernel,
        grid=(batch, tiles),
        in_specs=[
            pl.BlockSpec((None, ts, d), lambda b, i: (b, i, 0)),
            pl.BlockSpec((None, POOL_HALO, d), lambda b, i: (b, jnp.maximum(i * hb - 1, 0), 0)),
            const((1, d)), pl.BlockSpec((None,) + pool_w.shape[1:], lambda b, i: (layer, 0, 0, 0)),
            const((1, d)), const((1, d)), const((d, ne)), const((ts, ts)),
        ],
        out_specs=[
            pl.BlockSpec((None, ts, d), lambda b, i: (b, i, 0)),
            pl.BlockSpec((ROUTE_COLS, ts), lambda b, i: (0, b * tiles + i)),
            tok(ROUTE_COLS), const((1, ne)), tok(d // 2),
        ],
        out_shape=[
            jax.ShapeDtypeStruct((batch, seq, d), F32),
            jax.ShapeDtypeStruct((ROUTE_COLS, n), jnp.int32),
            jax.ShapeDtypeStruct((n, ROUTE_COLS), F32),
            jax.ShapeDtypeStruct((1, ne), F32),
            jax.ShapeDtypeStruct((n, d // 2), jnp.int32),
        ],
        scratch_shapes=[pltpu.VMEM((1, ne), F32)],
        compiler_params=_params("arbitrary", "arbitrary"),
        name="pool_router",
    )(x3, x3, gain, pool_w, scale, ffn_gain, w_router, tril)
    return out.reshape(n, d), route_i, route_g, counts, h_packed


ROUTE_COLS = 8


def _pack_bf16_pairs(hr):
    c = hr.shape[1] // 2
    bits = pltpu.bitcast(hr, jnp.int32)
    return (bits[:, c:] & jnp.int32(-65536)) | lax.shift_right_logical(bits[:, :c], 16)


def _unpack_bf16_pairs(packed):
    lo = pltpu.bitcast(lax.shift_left(packed, 16), F32)
    hi = pltpu.bitcast(packed & jnp.int32(-65536), F32)
    return jnp.concatenate([lo.astype(BF16), hi.astype(BF16)], axis=1)


def _route_tile(first, x, g_ref, wr_ref, tril_ref, ri_ref, rg_ref, cnt_ref, hp_ref, run_ref):
    @pl.when(first)
    def _():
        run_ref[...] = jnp.zeros_like(run_ref)

    h = _rms(x, g_ref[...])
    wr = wr_ref[...]
    ne = wr.shape[1]
    h_hi = h.astype(BF16)
    hp_ref[...] = _pack_bf16_pairs(h_hi.astype(F32))
    h_lo = (h - h_hi.astype(F32)).astype(BF16)
    w_hi = wr.astype(BF16)
    w_lo = (wr - w_hi.astype(F32)).astype(BF16)
    hi_part = jnp.dot(h_hi, jnp.concatenate([w_hi, w_lo], axis=1), preferred_element_type=F32)
    logits = hi_part[:, :ne] + hi_part[:, ne:] + jnp.dot(h_lo, w_hi, preferred_element_type=F32)
    tm = logits.shape[0]
    lane = lax.broadcasted_iota(jnp.int32, (tm, ne), 1)
    v1 = jnp.max(logits, axis=1, keepdims=True)
    i1 = jnp.min(jnp.where(logits == v1, lane, ne), axis=1, keepdims=True)
    rest = jnp.where(lane == i1, -jnp.inf, logits)
    v2 = jnp.max(rest, axis=1, keepdims=True)
    i2 = jnp.min(jnp.where(rest == v2, lane, ne), axis=1, keepdims=True)
    e = jnp.exp(v2 - v1)
    g1 = 1.0 / (1.0 + e)
    g2 = e / (1.0 + e)
    oh1 = (lane == i1).astype(F32)
    oh2 = (lane == i2).astype(F32)
    both = oh1 + oh2
    before = jnp.dot(tril_ref[...], both.astype(BF16), preferred_element_type=F32) + run_ref[...]
    r1 = jnp.sum(before * oh1, axis=1, keepdims=True).astype(jnp.int32)
    r2 = jnp.sum(before * oh2, axis=1, keepdims=True).astype(jnp.int32)
    run_ref[...] += jnp.sum(both, axis=0, keepdims=True)
    wide = lax.broadcasted_iota(jnp.int32, (tm, LANES), 1)
    fields = jnp.where(wide == 0, i1, jnp.where(wide == 1, i2, jnp.where(wide == 2, r1, r2)))
    ri_ref[...] = fields.T[:ri_ref.shape[0]]
    rg_ref[...] = jnp.where(lane == 0, g1, g2)
    cnt_ref[...] = run_ref[...]


SC_WINDOW = 64


def _sc_workers():
    info = plsc.get_sparse_core_info()
    mesh = plsc.VectorSubcoreMesh(core_axis_name="core", subcore_axis_name="subcore")
    return mesh, info.num_cores, info.num_cores * info.num_subcores


def _dispatch(x, dest, cap):
    n, d = x.shape
    mesh, n_cores, n_workers = _sc_workers()
    per_worker = n // n_workers

    n_windows = per_worker // SC_WINDOW
    n_bufs = 2
    assert n_windows % n_bufs == 0

    @functools.partial(
        pl.kernel, mesh=mesh, out_type=jax.ShapeDtypeStruct((cap, d), x.dtype),
        scratch_types=[pltpu.VMEM((SC_WINDOW,), jnp.int32)] * (TOP_K * n_bufs)
        + [pltpu.VMEM((SC_WINDOW, d), x.dtype)] * n_bufs + [pltpu.SemaphoreType.DMA] * n_bufs,
        name="moe_dispatch")
    def scatter(x_hbm, *rest):
        idx_hbm, (xs_hbm, *scratch) = rest[:TOP_K], rest[TOP_K:]
        idx_vmem = [scratch[b * TOP_K:(b + 1) * TOP_K] for b in range(n_bufs)]
        rows = scratch[TOP_K * n_bufs:TOP_K * n_bufs + n_bufs]
        sems = scratch[TOP_K * n_bufs + n_bufs:]
        wid = lax.axis_index("subcore") * n_cores + lax.axis_index("core")

        def drain(buf):
            for k in range(TOP_K):
                pltpu.make_async_copy(rows[buf], xs_hbm.at[idx_vmem[buf][k]], sems[buf]).wait()

        @pl.loop(0, n_windows, step=n_bufs)
        def _(c):
            for buf in range(n_bufs):
                @pl.when(c > 0)
                def _(buf=buf):
                    drain(buf)

                window = pl.ds(wid * per_worker + (c + buf) * SC_WINDOW, SC_WINDOW)
                for k in range(TOP_K):
                    pltpu.sync_copy(idx_hbm[k].at[window], idx_vmem[buf][k])
                pltpu.sync_copy(x_hbm.at[window], rows[buf])
                for k in range(TOP_K):
                    pltpu.async_copy(rows[buf], xs_hbm.at[idx_vmem[buf][k]], sems[buf])

        for buf in range(n_bufs):
            drain(buf)

    return scatter(x, *[dest[k] for k in range(TOP_K)])


def _gather_rows(table, idx):
    m = idx.shape[0]
    d = table.shape[1]
    mesh, n_cores, n_workers = _sc_workers()
    per_worker = m // n_workers

    n_windows = per_worker // SC_WINDOW
    n_bufs = 2
    assert n_windows % n_bufs == 0

    @functools.partial(
        pl.kernel, mesh=mesh, out_type=jax.ShapeDtypeStruct((m, d), table.dtype),
        scratch_types=[pltpu.VMEM((SC_WINDOW,), jnp.int32)] * n_bufs
        + [pltpu.VMEM((SC_WINDOW, d), table.dtype)] * n_bufs + [pltpu.SemaphoreType.DMA] * n_bufs,
        name="moe_gather")
    def gather(t_hbm, i_hbm, o_hbm, *scratch):
        idx_vmem, rows, sems = scratch[:n_bufs], scratch[n_bufs:2 * n_bufs], scratch[2 * n_bufs:]
        wid = lax.axis_index("subcore") * n_cores + lax.axis_index("core")

        def window(c):
            return pl.ds(wid * per_worker + c * SC_WINDOW, SC_WINDOW)

        def fetch(c, buf):
            pltpu.sync_copy(i_hbm.at[window(c)], idx_vmem[buf])
            return pltpu.async_copy(t_hbm.at[idx_vmem[buf]], rows[buf], sems[buf])

        fetch(0, 0)

        @pl.loop(0, n_windows, step=n_bufs)
        def _(c):
            for buf in range(n_bufs):
                nxt = c + buf + 1

                @pl.when(nxt < n_windows)
                def _(nxt=nxt, buf=buf):
                    fetch(nxt, (buf + 1) % n_bufs)

                pltpu.make_async_copy(t_hbm.at[idx_vmem[buf]], rows[buf], sems[buf]).wait()
                pltpu.sync_copy(rows[buf], o_hbm.at[window(c + buf)])

    return gather(table, idx)


def _expert_kernel(be_ref, nv_ref, xs_ref, wg_ref, wu_ref, wd_ref, ys_ref, xb_ref, acc_ref):
    del be_ref
    b = pl.program_id(0)
    j = pl.program_id(1)

    def ff_tile(first):
        x = xb_ref[...]
        acts = []
        for c in range(wg_ref.shape[1] // MXU_DIM):
            cols = slice(c * MXU_DIM, (c + 1) * MXU_DIM)
            gate = jnp.dot(x, wg_ref[:, cols].astype(BF16), preferred_element_type=F32)
            up = jnp.dot(x, wu_ref[:, cols].astype(BF16), preferred_element_type=F32)
            acts.append((_silu(gate) * up).astype(BF16))
        part = jnp.dot(jnp.concatenate(acts, axis=1), wd_ref[...].astype(BF16), preferred_element_type=F32)
        if first:
            acc_ref[...] = part
        else:
            acc_ref[...] += part

    valid = b < nv_ref[0]

    @pl.when(valid & (j == 0))
    def _():
        xb_ref[...] = _unpack_bf16_pairs(xs_ref[...])
        ff_tile(True)

    @pl.when(valid & (j > 0))
    def _():
        ff_tile(False)

    @pl.when(jnp.logical_not(valid) & (j == 0))
    def _():
        acc_ref[...] = jnp.zeros_like(acc_ref)

    @pl.when(j == pl.num_programs(1) - 1)
    def _():
        ys_ref[...] = _pack_bf16_pairs(acc_ref[...].astype(BF16).astype(F32))


def _experts(xs, block_e, n_valid, w_gate, w_up, w_down, layer, tm, tf):
    cap = xs.shape[0]
    d = 2 * xs.shape[1]
    ff = w_gate.shape[3]
    nj = ff // tf

    def blk(b, nv):
        return jnp.minimum(b, nv[0] - 1)

    def ffi(b, j, nv):
        return jnp.where(b < nv[0], j, nj - 1)

    return pl.pallas_call(
        _expert_kernel,
        grid_spec=pltpu.PrefetchScalarGridSpec(
            num_scalar_prefetch=2,
            grid=(cap // tm, nj),
            in_specs=[
                pl.BlockSpec((tm, d // 2), lambda b, j, be, nv: (blk(b, nv), 0)),
                pl.BlockSpec((None, None, d, tf), lambda b, j, be, nv: (layer, be[blk(b, nv)], 0, ffi(b, j, nv))),
                pl.BlockSpec((None, None, d, tf), lambda b, j, be, nv: (layer, be[blk(b, nv)], 0, ffi(b, j, nv))),
                pl.BlockSpec((None, None, tf, d), lambda b, j, be, nv: (layer, be[blk(b, nv)], ffi(b, j, nv), 0)),
            ],
            out_specs=pl.BlockSpec((tm, d // 2), lambda b, j, be, nv: (b, 0)),
            scratch_shapes=[pltpu.VMEM((tm, d), BF16), pltpu.VMEM((tm, d), F32)],
        ),
        out_shape=jax.ShapeDtypeStruct((cap, d // 2), jnp.int32),
        compiler_params=_params("arbitrary", "arbitrary"),
        name="moe_experts",
    )(block_e, n_valid, xs, w_gate, w_up, w_down)


def _combine_kernel(x_ref, rg_ref, gf_ref, y0_ref, y1_ref, o_ref):
    rg = rg_ref[...]
    y0 = _unpack_bf16_pairs(y0_ref[...]).astype(F32)
    y1 = _unpack_bf16_pairs(y1_ref[...]).astype(F32)
    o_ref[...] = _rms(x_ref[...] + (rg[:, 0:1] * y0 + rg[:, 1:2] * y1), gf_ref[...])


def _combine_and_final_norm(x, route_g, yg, final_gain):
    n, d = x.shape
    tm = 1024
    return pl.pallas_call(
        _combine_kernel,
        grid=(n // tm,),
        in_specs=[
            pl.BlockSpec((tm, d), lambda i: (i, 0)),
            pl.BlockSpec((tm, ROUTE_COLS), lambda i: (i, 0)),
            pl.BlockSpec((1, d), lambda i: (0, 0)),
            pl.BlockSpec((None, tm, d // 2), lambda i: (0, i, 0)),
            pl.BlockSpec((None, tm, d // 2), lambda i: (1, i, 0)),
        ],
        out_specs=pl.BlockSpec((tm, d), lambda i: (i, 0)),
        out_shape=jax.ShapeDtypeStruct((n, d), F32),
        compiler_params=_params("parallel"),
        name="moe_combine",
    )(x, route_g, final_gain, yg, yg)


def _moe_layer(x, routing, w_gate, w_up, w_down, layer):
    n, d = x.shape
    tm = 1024 if n * TOP_K >= 8 * 1024 else 256
    tf = 2 * MXU_DIM
    route_i, route_g, counts, h_packed = routing

    counts = counts[0].astype(jnp.int32)
    padded = ((counts + tm - 1) // tm) * tm
    ends = jnp.cumsum(padded)
    starts = ends - padded
    experts, ranks = route_i[:TOP_K], route_i[TOP_K:2 * TOP_K]
    dest = ranks + sum(jnp.where(experts == e, starts[e], 0) for e in range(N_EXPERTS))
    n_blocks = (n * TOP_K) // tm + N_EXPERTS
    block_start = jnp.arange(n_blocks, dtype=jnp.int32) * tm
    block_e = jnp.minimum(jnp.sum(ends[None, :] <= block_start[:, None], axis=1), N_EXPERTS - 1).astype(jnp.int32)
    n_valid = (ends[-1:] // tm).astype(jnp.int32)

    xs = _dispatch(h_packed, dest, n_blocks * tm)
    ys = _experts(xs, block_e, n_valid, w_gate, w_up, w_down, layer, tm, tf)
    return route_g, _gather_rows(ys, dest.reshape(TOP_K * n)).reshape(TOP_K, n, d // 2)


def _rope_tables(seq):
    pos = jnp.arange(seq, dtype=F32)
    inv_freq = ROPE_THETA ** (-jnp.arange(0, HEAD_DIM, 2, dtype=F32) / HEAD_DIM)
    ang = pos[:, None] * inv_freq[None, :]
    reps = LANES // (HEAD_DIM // 2)
    cos = jnp.tile(jnp.cos(ang), (1, reps))
    sin = jnp.tile(jnp.sin(ang), (1, reps))
    first_half = (jnp.arange(LANES) % HEAD_DIM) < HEAD_DIM // 2
    tabs = (cos, jnp.where(first_half, -sin, sin))
    q_scale = HEAD_DIM ** -0.5 * 1.4426950408889634
    return tuple(jnp.stack([t * q_scale, t]) for t in tabs)


def kernel(x, mix_norm, ffn_norm, attn_w_qkv, attn_w_o, pool_w, pool_scale, dense_w_gate, dense_w_up,
           dense_w_down, moe_router, moe_w_gate, moe_w_up, moe_w_down, final_norm):
    batch, seq, d = x.shape
    depth = mix_norm.shape[0]
    assert depth % 2 == 0 and d == N_HEADS * HEAD_DIM
    assert all(seq % (dil * (win // dil)) == 0 for win, dil in zip(WINDOWS, DILATIONS))
    assert seq % QKV_ROWS == 0 and seq % POOL_ROWS == 0 and moe_router.shape[2] == N_EXPERTS
    rope_tables = _rope_tables(seq)
    xf = x.reshape(batch * seq, d)
    final_gain = final_norm.reshape(1, d)
    pending_moe = None
    w_qkv, w_o, w_gate, w_up, w_down, w_pool = (
        w.astype(BF16) for w in (attn_w_qkv, attn_w_o, dense_w_gate, dense_w_up, dense_w_down, pool_w))
    for i in range(depth):
        j = i // 2
        mix_gain = mix_norm[i].reshape(1, d)
        ffn_gain = ffn_norm[i].reshape(1, d)
        if i % 2 == 0:
            qkv, xf = _qkv_project(xf, mix_gain, w_qkv, j, rope_tables, batch, seq, pending_moe)
            pending_moe = None
            xf = _attention_and_dense_ffn(qkv, xf, w_o, ffn_gain, w_gate, w_up, w_down, j, batch, seq)
        else:
            xf, *routing = _pool_mixer_and_router(xf, mix_gain, w_pool, j, pool_scale[j].reshape(1, d),
                                                  ffn_gain, moe_router[j], batch, seq)
            pending_moe = _moe_layer(xf, routing, moe_w_gate, moe_w_up, moe_w_down, j)
    return _combine_and_final_norm(xf, *pending_moe, final_gain).reshape(batch, seq, d)
```

```python
import functools

import jax
import jax.numpy as jnp
from jax import lax
from jax.experimental import pallas as pl
from jax.experimental.pallas import tpu as pltpu
from jax.experimental.pallas import tpu_sc as plsc

WINDOWS = (128, 512, 2048)
DILATIONS = (1, 4, 16)
N_GROUPS = len(WINDOWS)
N_HEADS = 16
HEAD_DIM = 64
ROPE_THETA = 10000.0
POOL_SIZES = (2, 4, 8, 16)
N_EXPERTS = 8
TOP_K = 2
RMS_EPS = 1e-6
MASK_VALUE = -1e30
QKV_ROWS = 512
POOL_ROWS = 1024

LANES = 128
MXU_DIM = 256
VMEM_LIMIT_BYTES = 56 * 1024 * 1024
VMEM_LIMIT_HIGH_BYTES = 60 * 1024 * 1024

F32 = jnp.float32
BF16 = jnp.bfloat16


def _params(*sem, vmem_limit_bytes=VMEM_LIMIT_BYTES):
    return pltpu.CompilerParams(dimension_semantics=sem, vmem_limit_bytes=vmem_limit_bytes)


def _rms(x, g):
    ms = jnp.mean(x * x, axis=-1, keepdims=True)
    return x * lax.rsqrt(ms + RMS_EPS) * g


def _silu(g):
    return g / (1.0 + jnp.exp(-g))


def _qkv_kernel(x_ref, g_ref, w_ref, cos_ref, sin_ref, *rest, pending_moe):
    if pending_moe:
        rg_ref, y0_ref, y1_ref, o0_ref, o1_ref, o2_ref, xn_ref, acc_ref = rest
        rg = rg_ref[...]
        x = x_ref[...] + (rg[:, 0:1] * _unpack_bf16_pairs(y0_ref[...]).astype(F32)
                          + rg[:, 1:2] * _unpack_bf16_pairs(y1_ref[...]).astype(F32))
        xn_ref[...] = x
    else:
        o0_ref, o1_ref, o2_ref, acc_ref = rest
        x = x_ref[...]
    tm, d = x_ref.shape
    n_chunks = acc_ref.shape[1]
    lane = lax.broadcasted_iota(jnp.int32, (1, LANES), 1)
    first_half = (lane % HEAD_DIM) < HEAD_DIM // 2
    h = _rms(x, g_ref[...]).astype(BF16)
    for g, (o_ref, dil) in enumerate(zip((o0_ref, o1_ref, o2_ref), DILATIONS)):
        rows = tm // dil

        def strided(ref, r, dil=dil, rows=rows):
            if dil == 1:
                return ref[...]
            return ref[pl.ds(r, rows, stride=dil), :]

        for kind in range(3):
            col0 = (g * 3 + kind) * d
            acc = jnp.dot(h, w_ref[:, col0:col0 + d], preferred_element_type=F32)
            for c in range(n_chunks):
                acc_ref[kind, c] = acc[:, c * LANES:(c + 1) * LANES]
            for r in range(dil):
                if kind < 2:
                    cos = strided(cos_ref.at[kind], r)
                    sin = strided(sin_ref.at[kind], r)
                for c in range(n_chunks):
                    t = strided(acc_ref.at[kind, c], r)
                    if kind < 2:
                        partner = jnp.where(first_half, pltpu.roll(t, LANES - HEAD_DIM // 2, 1),
                                            pltpu.roll(t, HEAD_DIM // 2, 1))
                        t = t * cos + partner * sin
                    o_ref[kind, r, :, c * LANES:(c + 1) * LANES] = t.astype(BF16)


def _qkv_project(x, gain, w_qkv, layer, tables, batch, seq, pending_moe=None):
    n, d = x.shape
    tm = QKV_ROWS
    vmem_limit = VMEM_LIMIT_HIGH_BYTES if pending_moe else VMEM_LIMIT_BYTES
    tiles_per_seq = seq // tm
    tab = pl.BlockSpec((2, tm, LANES), lambda i: (0, i % tiles_per_seq, 0))

    def o_spec(dil):
        return pl.BlockSpec((3, None, dil, tm // dil, d),
                            lambda i: (0, i // tiles_per_seq, 0, i % tiles_per_seq, 0))

    row = pl.BlockSpec((tm, d), lambda i: (i, 0))
    in_specs = [row, pl.BlockSpec((1, d), lambda i: (0, 0)),
                pl.BlockSpec((None,) + w_qkv.shape[1:], lambda i: (layer, 0, 0), pipeline_mode=pl.Buffered(1)),
                tab, tab]
    args = [x, gain, w_qkv, *tables]
    out_specs = [o_spec(dil) for dil in DILATIONS]
    out_shape = [jax.ShapeDtypeStruct((3, batch, dil, seq // dil, d), BF16) for dil in DILATIONS]
    if pending_moe:
        route_g, (yg,) = pending_moe
        in_specs += [pl.BlockSpec((tm, ROUTE_COLS), lambda i: (i, 0))]
        in_specs += [pl.BlockSpec((None, tm, d // 2), lambda i, k=k: (k, i, 0)) for k in range(TOP_K)]
        args += [route_g, yg, yg]
        out_specs.append(row)
        out_shape.append(jax.ShapeDtypeStruct((n, d), F32))
    outs = pl.pallas_call(
        functools.partial(_qkv_kernel, pending_moe=bool(pending_moe)),
        grid=(n // tm,),
        in_specs=in_specs,
        out_specs=out_specs,
        out_shape=out_shape,
        scratch_shapes=[pltpu.VMEM((3, d // LANES, tm, LANES), F32)],
        compiler_params=_params("parallel", vmem_limit_bytes=vmem_limit),
        name="qkv_rope",
    )(*args)
    return [o.reshape(3, batch, seq, d) for o in outs[:N_GROUPS]], (outs[N_GROUPS] if pending_moe else x)


ATTN_TILES_PER_FFN_PIECE = 3


def _attn_dense_kernel(q0, k0, v0, q1, k1, v1, q2, k2, v2, band_ref, causal_ref,
                       x_ref, wo_ref, g_ref, wg_ref, wu_ref, wd_ref, y_ref,
                       o_scr, num_ref, max_ref, den_ref, kt_ref):
    b = pl.program_id(0)
    hp = pl.program_id(1)
    n_pairs, seq = o_scr.shape[1], o_scr.shape[2]
    w = WINDOWS[0] // DILATIONS[0]
    lane = lax.broadcasted_iota(jnp.int32, (1, LANES), 1)
    first_head = lane < HEAD_DIM
    o_out = o_scr.at[b % 2, hp]
    o_in = o_scr.at[(b + 1) % 2]

    def ffn_pieces():
        rows = pl.ds(pl.multiple_of(hp * x_ref.shape[0], x_ref.shape[0]), x_ref.shape[0])
        o = jnp.concatenate([o_in[p, rows, :] for p in range(n_pairs)], axis=1)
        x = x_ref[...] + jnp.dot(o, wo_ref[...], preferred_element_type=F32)
        h = _rms(x, g_ref[...]).astype(BF16)
        yield
        acts = []
        n_chunks = wg_ref.shape[1] // MXU_DIM
        for c in range(n_chunks):
            cols = slice(c * MXU_DIM, (c + 1) * MXU_DIM)
            gate = jnp.dot(h, wg_ref[:, cols], preferred_element_type=F32)
            up = jnp.dot(h, wu_ref[:, cols], preferred_element_type=F32)
            acts.append((_silu(gate) * up).astype(BF16))
            yield
        half = n_chunks // 2
        acc = jnp.dot(jnp.concatenate(acts[:half], axis=1), wd_ref[:half * MXU_DIM, :], preferred_element_type=F32)
        yield
        acc = acc + jnp.dot(jnp.concatenate(acts[half:], axis=1), wd_ref[half * MXU_DIM:, :],
                            preferred_element_type=F32)
        y_ref[...] = x + acc
        yield

    def tile(q_ref, v_ref, qoff, has_prev):
        q = q_ref[qoff:qoff + w, :]
        zero = jnp.zeros_like(q)
        qq = jnp.concatenate([jnp.where(first_head, q, zero), jnp.where(first_head, zero, q)], axis=0)
        k_lo = qoff - w if has_prev else qoff
        s = jnp.dot(qq, kt_ref[:, k_lo:qoff + w], preferred_element_type=F32)
        s = s + (band_ref[...] if has_prev else causal_ref[...])
        mx = jnp.max(s, axis=1, keepdims=True)
        p = jnp.exp2(s - mx)
        den = jnp.sum(p, axis=1, keepdims=True)
        pv = jnp.dot(p.astype(BF16), v_ref[k_lo:qoff + w, :], preferred_element_type=F32)
        return tuple(jnp.where(first_head, a[:w], a[w:]) for a in (pv, mx, den))

    def attention(with_ffn):
        pieces = ffn_pieces() if with_ffn else iter(())
        next(pieces, None)
        tiles_done = 0
        order = sorted(range(N_GROUPS), key=lambda g: -DILATIONS[g])
        for g in order:
            q_ref, k_ref, v_ref = ((q0, k0, v0), (q1, k1, v1), (q2, k2, v2))[g]
            dil = DILATIONS[g]
            sub_len = seq // dil
            nb = sub_len // w
            kt_ref[...] = k_ref[...].T
            for t in range(dil * nb):
                r, n = divmod(t, nb)
                qoff = r * sub_len + n * w
                pv, mx, den = tile(q_ref, v_ref, qoff, n > 0)
                tiles_done += 1
                if tiles_done % ATTN_TILES_PER_FFN_PIECE == 0:
                    next(pieces, None)
                rows = pl.ds(qoff, w) if dil == 1 else pl.ds(r + dil * n * w, w, stride=dil)
                if g == order[0]:
                    num_ref[rows, :] = pv
                    max_ref[rows, :] = mx
                    den_ref[rows, :] = den
                    continue
                old_max = max_ref[rows, :]
                new_max = jnp.maximum(old_max, mx)
                a = jnp.exp2(old_max - new_max)
                c = jnp.exp2(mx - new_max)
                num = a * num_ref[rows, :] + c * pv
                den = a * den_ref[rows, :] + c * den
                if g == order[-1]:
                    o_out[rows, :] = (num / den).astype(BF16)
                else:
                    num_ref[rows, :] = num
                    den_ref[rows, :] = den
                    max_ref[rows, :] = new_max
        for _ in pieces:
            pass

    n_batches = pl.num_programs(0) - 1
    @pl.when(b == 0)
    def _():
        y_ref[...] = jnp.zeros_like(y_ref)
        attention(False)

    pl.when((b > 0) & (b < n_batches))(functools.partial(attention, True))

    @pl.when(b == n_batches)
    def _():
        for _ in ffn_pieces():
            pass


def _attention_biases(w):
    row = jnp.arange(2 * w)[:, None] % w
    col = jnp.arange(2 * w)[None, :]
    dist = row + w - col
    band = jnp.where((dist >= 0) & (dist <= w), 0.0, MASK_VALUE).astype(F32)
    causal = jnp.where(row >= col[:, :w], 0.0, MASK_VALUE).astype(F32)
    return band, causal


def _attention_and_dense_ffn(qkv_groups, x, w_o, gain, w_gate, w_up, w_down, layer, batch, seq):
    n, d = x.shape
    ff = w_gate.shape[2]
    n_pairs = d // LANES
    rows = seq // n_pairs
    last = batch - 1
    in_specs, args = [], []
    for arr in qkv_groups:
        for kind in range(3):
            in_specs.append(pl.BlockSpec(
                (None, None, seq, LANES),
                lambda b, hp, kind=kind: (kind, jnp.minimum(b, last), 0, jnp.where(b > last, n_pairs - 1, hp))))
            args.append(arr)
    const = lambda shape: pl.BlockSpec(shape, lambda b, hp: (0, 0), pipeline_mode=pl.Buffered(1))
    ffn_in = pl.BlockSpec((rows, d), lambda b, hp: (jnp.maximum(b - 1, 0) * n_pairs + hp, 0))
    ffn_out = pl.BlockSpec((rows, d), lambda b, hp: (jnp.where(b == 0, batch, b - 1) * n_pairs + hp, 0))
    biases = _attention_biases(WINDOWS[0] // DILATIONS[0])
    in_specs += [const(t.shape) for t in biases]
    per_layer = lambda shape: pl.BlockSpec((None,) + shape, lambda b, hp: (layer, 0, 0), pipeline_mode=pl.Buffered(1))
    in_specs += [ffn_in, per_layer((d, d)), const((1, d)), per_layer((d, ff)), per_layer((d, ff)), per_layer((ff, d))]
    args += [*biases, x, w_o, gain, w_gate, w_up, w_down]
    return pl.pallas_call(
        _attn_dense_kernel,
        grid=(batch + 1, n_pairs),
        in_specs=in_specs,
        out_specs=ffn_out,
        out_shape=jax.ShapeDtypeStruct((n + seq, d), F32),
        scratch_shapes=[pltpu.VMEM((2, n_pairs, seq, LANES), BF16)]
        + [pltpu.VMEM((seq, LANES), F32)] * 3 + [pltpu.VMEM((LANES, seq), BF16)],
        compiler_params=_params("arbitrary", "arbitrary"),
        name="attention_ffn",
    )(*args)


POOL_HALO = max(POOL_SIZES)


def _pool_tile(i, x_ref, halo_ref, g_ref, w_ref, sc_ref, y_ref):
    ts = x_ref.shape[0]
    g = g_ref[...]
    x = x_ref[...]
    h = _rms(x, g)
    hh = jnp.where(i > 0, _rms(halo_ref[...], g), 0.0)
    ext = jnp.concatenate([hh, h], axis=0)
    pos = i * ts + lax.broadcasted_iota(jnp.int32, (ts, 1), 0)
    pc = w_ref.shape[1]
    for grp, size in enumerate(POOL_SIZES):
        sl = slice(grp * pc, (grp + 1) * pc)
        s = ext[:, sl]
        step = 1
        while step < size:
            s = s + pltpu.roll(s, step, 0)
            step *= 2
        cnt = jnp.minimum(pos + 1, size).astype(F32)
        y = s[POOL_HALO:, :] / cnt - h[:, sl]
        z = jnp.dot(y.astype(BF16), w_ref[grp], preferred_element_type=F32)
        y_ref[:, sl] = x[:, sl] + z * sc_ref[:, sl]


def _pool_router_kernel(x_ref, halo_ref, g_ref, w_ref, sc_ref, g2_ref, wr_ref, tril_ref,
                        y_ref, ri_ref, rg_ref, cnt_ref, hp_ref, run_ref):
    b = pl.program_id(0)
    i = pl.program_id(1)
    _pool_tile(i, x_ref, halo_ref, g_ref, w_ref, sc_ref, y_ref)
    _route_tile((b == 0) & (i == 0), y_ref[...], g2_ref, wr_ref, tril_ref, ri_ref, rg_ref, cnt_ref, hp_ref, run_ref)


def _pool_mixer_and_router(x, gain, pool_w, layer, scale, ffn_gain, w_router, batch, seq):
    d = x.shape[1]
    n = batch * seq
    ne = w_router.shape[1]
    ts = POOL_ROWS
    tiles = seq // ts
    x3 = x.reshape(-1, seq, d)
    hb = ts // POOL_HALO
    tril = (jnp.arange(ts)[:, None] > jnp.arange(ts)[None, :]).astype(BF16)
    const = lambda shape: pl.BlockSpec(shape, lambda b, i: (0,) * len(shape))
    tok = lambda cols: pl.BlockSpec((ts, cols), lambda b, i: (b * tiles + i, 0))
    out, route_i, route_g, counts, h_packed = pl.pallas_call(
        _pool_router_kernel,
        grid=(batch, tiles),
        in_specs=[
            pl.BlockSpec((None, ts, d), lambda b, i: (b, i, 0)),
            pl.BlockSpec((None, POOL_HALO, d), lambda b, i: (b, jnp.maximum(i * hb - 1, 0), 0)),
            const((1, d)), pl.BlockSpec((None,) + pool_w.shape[1:], lambda b, i: (layer, 0, 0, 0)),
            const((1, d)), const((1, d)), const((d, ne)), const((ts, ts)),
        ],
        out_specs=[
            pl.BlockSpec((None, ts, d), lambda b, i: (b, i, 0)),
            pl.BlockSpec((ROUTE_COLS, ts), lambda b, i: (0, b * tiles + i)),
            tok(ROUTE_COLS), const((1, ne)), tok(d // 2),
        ],
        out_shape=[
            jax.ShapeDtypeStruct((batch, seq, d), F32),
            jax.ShapeDtypeStruct((ROUTE_COLS, n), jnp.int32),
            jax.ShapeDtypeStruct((n, ROUTE_COLS), F32),
            jax.ShapeDtypeStruct((1, ne), F32),
            jax.ShapeDtypeStruct((n, d // 2), jnp.int32),
        ],
        scratch_shapes=[pltpu.VMEM((1, ne), F32)],
        compiler_params=_params("arbitrary", "arbitrary"),
        name="pool_router",
    )(x3, x3, gain, pool_w, scale, ffn_gain, w_router, tril)
    return out.reshape(n, d), route_i, route_g, counts, h_packed


ROUTE_COLS = 8


def _pack_bf16_pairs(hr):
    c = hr.shape[1] // 2
    bits = pltpu.bitcast(hr, jnp.int32)
    return (bits[:, c:] & jnp.int32(-65536)) | lax.shift_right_logical(bits[:, :c], 16)


def _unpack_bf16_pairs(packed):
    lo = pltpu.bitcast(lax.shift_left(packed, 16), F32)
    hi = pltpu.bitcast(packed & jnp.int32(-65536), F32)
    return jnp.concatenate([lo.astype(BF16), hi.astype(BF16)], axis=1)


def _route_tile(first, x, g_ref, wr_ref, tril_ref, ri_ref, rg_ref, cnt_ref, hp_ref, run_ref):
    @pl.when(first)
    def _():
        run_ref[...] = jnp.zeros_like(run_ref)

    h = _rms(x, g_ref[...])
    wr = wr_ref[...]
    ne = wr.shape[1]
    h_hi = h.astype(BF16)
    hp_ref[...] = _pack_bf16_pairs(h_hi.astype(F32))
    h_lo = (h - h_hi.astype(F32)).astype(BF16)
    w_hi = wr.astype(BF16)
    w_lo = (wr - w_hi.astype(F32)).astype(BF16)
    hi_part = jnp.dot(h_hi, jnp.concatenate([w_hi, w_lo], axis=1), preferred_element_type=F32)
    logits = hi_part[:, :ne] + hi_part[:, ne:] + jnp.dot(h_lo, w_hi, preferred_element_type=F32)
    tm = logits.shape[0]
    lane = lax.broadcasted_iota(jnp.int32, (tm, ne), 1)
    v1 = jnp.max(logits, axis=1, keepdims=True)
    i1 = jnp.min(jnp.where(logits == v1, lane, ne), axis=1, keepdims=True)
    rest = jnp.where(lane == i1, -jnp.inf, logits)
    v2 = jnp.max(rest, axis=1, keepdims=True)
    i2 = jnp.min(jnp.where(rest == v2, lane, ne), axis=1, keepdims=True)
    e = jnp.exp(v2 - v1)
    g1 = 1.0 / (1.0 + e)
    g2 = e / (1.0 + e)
    oh1 = (lane == i1).astype(F32)
    oh2 = (lane == i2).astype(F32)
    both = oh1 + oh2
    before = jnp.dot(tril_ref[...], both.astype(BF16), preferred_element_type=F32) + run_ref[...]
    r1 = jnp.sum(before * oh1, axis=1, keepdims=True).astype(jnp.int32)
    r2 = jnp.sum(before * oh2, axis=1, keepdims=True).astype(jnp.int32)
    run_ref[...] += jnp.sum(both, axis=0, keepdims=True)
    wide = lax.broadcasted_iota(jnp.int32, (tm, LANES), 1)
    fields = jnp.where(wide == 0, i1, jnp.where(wide == 1, i2, jnp.where(wide == 2, r1, r2)))
    ri_ref[...] = fields.T[:ri_ref.shape[0]]
    rg_ref[...] = jnp.where(lane == 0, g1, g2)
    cnt_ref[...] = run_ref[...]


SC_WINDOW = 64


def _sc_workers():
    info = plsc.get_sparse_core_info()
    mesh = plsc.VectorSubcoreMesh(core_axis_name="core", subcore_axis_name="subcore")
    return mesh, info.num_cores, info.num_cores * info.num_subcores


def _dispatch(x, dest, cap):
    n, d = x.shape
    mesh, n_cores, n_workers = _sc_workers()
    per_worker = n // n_workers

    n_windows = per_worker // SC_WINDOW
    n_bufs = 2
    assert n_windows % n_bufs == 0

    @functools.partial(
        pl.kernel, mesh=mesh, out_type=jax.ShapeDtypeStruct((cap, d), x.dtype),
        scratch_types=[pltpu.VMEM((SC_WINDOW,), jnp.int32)] * (TOP_K * n_bufs)
        + [pltpu.VMEM((SC_WINDOW, d), x.dtype)] * n_bufs + [pltpu.SemaphoreType.DMA] * n_bufs,
        name="moe_dispatch")
    def scatter(x_hbm, *rest):
        idx_hbm, (xs_hbm, *scratch) = rest[:TOP_K], rest[TOP_K:]
        idx_vmem = [scratch[b * TOP_K:(b + 1) * TOP_K] for b in range(n_bufs)]
        rows = scratch[TOP_K * n_bufs:TOP_K * n_bufs + n_bufs]
        sems = scratch[TOP_K * n_bufs + n_bufs:]
        wid = lax.axis_index("subcore") * n_cores + lax.axis_index("core")

        def drain(buf):
            for k in range(TOP_K):
                pltpu.make_async_copy(rows[buf], xs_hbm.at[idx_vmem[buf][k]], sems[buf]).wait()

        @pl.loop(0, n_windows, step=n_bufs)
        def _(c):
            for buf in range(n_bufs):
                @pl.when(c > 0)
                def _(buf=buf):
                    drain(buf)

                window = pl.ds(wid * per_worker + (c + buf) * SC_WINDOW, SC_WINDOW)
                for k in range(TOP_K):
                    pltpu.sync_copy(idx_hbm[k].at[window], idx_vmem[buf][k])
                pltpu.sync_copy(x_hbm.at[window], rows[buf])
                for k in range(TOP_K):
                    pltpu.async_copy(rows[buf], xs_hbm.at[idx_vmem[buf][k]], sems[buf])

        for buf in range(n_bufs):
            drain(buf)

    return scatter(x, *[dest[k] for k in range(TOP_K)])


def _gather_rows(table, idx):
    m = idx.shape[0]
    d = table.shape[1]
    mesh, n_cores, n_workers = _sc_workers()
    per_worker = m // n_workers

    n_windows = per_worker // SC_WINDOW
    n_bufs = 2
    assert n_windows % n_bufs == 0

    @functools.partial(
        pl.kernel, mesh=mesh, out_type=jax.ShapeDtypeStruct((m, d), table.dtype),
        scratch_types=[pltpu.VMEM((SC_WINDOW,), jnp.int32)] * n_bufs
        + [pltpu.VMEM((SC_WINDOW, d), table.dtype)] * n_bufs + [pltpu.SemaphoreType.DMA] * n_bufs,
        name="moe_gather")
    def gather(t_hbm, i_hbm, o_hbm, *scratch):
        idx_vmem, rows, sems = scratch[:n_bufs], scratch[n_bufs:2 * n_bufs], scratch[2 * n_bufs:]
        wid = lax.axis_index("subcore") * n_cores + lax.axis_index("core")

        def window(c):
            return pl.ds(wid * per_worker + c * SC_WINDOW, SC_WINDOW)

        def fetch(c, buf):
            pltpu.sync_copy(i_hbm.at[window(c)], idx_vmem[buf])
            return pltpu.async_copy(t_hbm.at[idx_vmem[buf]], rows[buf], sems[buf])

        fetch(0, 0)

        @pl.loop(0, n_windows, step=n_bufs)
        def _(c):
            for buf in range(n_bufs):
                nxt = c + buf + 1

                @pl.when(nxt < n_windows)
                def _(nxt=nxt, buf=buf):
                    fetch(nxt, (buf + 1) % n_bufs)

                pltpu.make_async_copy(t_hbm.at[idx_vmem[buf]], rows[buf], sems[buf]).wait()
                pltpu.sync_copy(rows[buf], o_hbm.at[window(c + buf)])

    return gather(table, idx)


def _expert_kernel(be_ref, nv_ref, xs_ref, wg_ref, wu_ref, wd_ref, ys_ref, xb_ref, acc_ref):
    del be_ref
    b = pl.program_id(0)
    j = pl.program_id(1)

    def ff_tile(first):
        x = xb_ref[...]
        acts = []
        for c in range(wg_ref.shape[1] // MXU_DIM):
            cols = slice(c * MXU_DIM, (c + 1) * MXU_DIM)
            gate = jnp.dot(x, wg_ref[:, cols].astype(BF16), preferred_element_type=F32)
            up = jnp.dot(x, wu_ref[:, cols].astype(BF16), preferred_element_type=F32)
            acts.append((_silu(gate) * up).astype(BF16))
        part = jnp.dot(jnp.concatenate(acts, axis=1), wd_ref[...].astype(BF16), preferred_element_type=F32)
        if first:
            acc_ref[...] = part
        else:
            acc_ref[...] += part

    valid = b < nv_ref[0]

    @pl.when(valid & (j == 0))
    def _():
        xb_ref[...] = _unpack_bf16_pairs(xs_ref[...])
        ff_tile(True)

    @pl.when(valid & (j > 0))
    def _():
        ff_tile(False)

    @pl.when(jnp.logical_not(valid) & (j == 0))
    def _():
        acc_ref[...] = jnp.zeros_like(acc_ref)

    @pl.when(j == pl.num_programs(1) - 1)
    def _():
        ys_ref[...] = _pack_bf16_pairs(acc_ref[...].astype(BF16).astype(F32))


def _experts(xs, block_e, n_valid, w_gate, w_up, w_down, layer, tm, tf):
    cap = xs.shape[0]
    d = 2 * xs.shape[1]
    ff = w_gate.shape[3]
    nj = ff // tf

    def blk(b, nv):
        return jnp.minimum(b, nv[0] - 1)

    def ffi(b, j, nv):
        return jnp.where(b < nv[0], j, nj - 1)

    return pl.pallas_call(
        _expert_kernel,
        grid_spec=pltpu.PrefetchScalarGridSpec(
            num_scalar_prefetch=2,
            grid=(cap // tm, nj),
            in_specs=[
                pl.BlockSpec((tm, d // 2), lambda b, j, be, nv: (blk(b, nv), 0)),
                pl.BlockSpec((None, None, d, tf), lambda b, j, be, nv: (layer, be[blk(b, nv)], 0, ffi(b, j, nv))),
                pl.BlockSpec((None, None, d, tf), lambda b, j, be, nv: (layer, be[blk(b, nv)], 0, ffi(b, j, nv))),
                pl.BlockSpec((None, None, tf, d), lambda b, j, be, nv: (layer, be[blk(b, nv)], ffi(b, j, nv), 0)),
            ],
            out_specs=pl.BlockSpec((tm, d // 2), lambda b, j, be, nv: (b, 0)),
            scratch_shapes=[pltpu.VMEM((tm, d), BF16), pltpu.VMEM((tm, d), F32)],
        ),
        out_shape=jax.ShapeDtypeStruct((cap, d // 2), jnp.int32),
        compiler_params=_params("arbitrary", "arbitrary"),
        name="moe_experts",
    )(block_e, n_valid, xs, w_gate, w_up, w_down)


def _combine_kernel(x_ref, rg_ref, gf_ref, y0_ref, y1_ref, o_ref):
    rg = rg_ref[...]
    y0 = _unpack_bf16_pairs(y0_ref[...]).astype(F32)
    y1 = _unpack_bf16_pairs(y1_ref[...]).astype(F32)
    o_ref[...] = _rms(x_ref[...] + (rg[:, 0:1] * y0 + rg[:, 1:2] * y1), gf_ref[...])


def _combine_part_kernel(x_ref, rg_ref, gf_ref, y0_ref, y1_ref, *rest):
    _combine_kernel(x_ref, rg_ref, gf_ref, y0_ref, y1_ref, rest[-1])


def _combine_and_final_norm(x, route_g, yg_parts, final_gain):
    n, d = x.shape
    tm = 1024
    part_tiles = n // len(yg_parts) // tm
    out = None
    for p, yg in enumerate(yg_parts):
        first = p * part_tiles
        tok = lambda cols: pl.BlockSpec((tm, cols), lambda i: (first + i, 0))
        in_specs = [tok(d), tok(ROUTE_COLS), pl.BlockSpec((1, d), lambda i: (0, 0)),
                    pl.BlockSpec((None, tm, d // 2), lambda i: (0, i, 0)),
                    pl.BlockSpec((None, tm, d // 2), lambda i: (1, i, 0))]
        args = [x, route_g, final_gain, yg, yg]
        if out is not None:
            in_specs.append(pl.BlockSpec(memory_space=pl.ANY))
            args.append(out)
        out = pl.pallas_call(
            _combine_part_kernel,
            grid=(part_tiles,),
            in_specs=in_specs,
            out_specs=tok(d),
            out_shape=jax.ShapeDtypeStruct((n, d), F32),
            input_output_aliases={} if len(args) == 5 else {5: 0},
            compiler_params=_params("parallel"),
            name="moe_combine",
        )(*args)
    return out


def _moe_layer(x, routing, w_gate, w_up, w_down, layer, gather_parts):
    n, d = x.shape
    tm = 1024 if n * TOP_K >= 8 * 1024 else 256
    tf = 2 * MXU_DIM
    route_i, route_g, counts, h_packed = routing

    counts = counts[0].astype(jnp.int32)
    padded = ((counts + tm - 1) // tm) * tm
    ends = jnp.cumsum(padded)
    starts = ends - padded
    experts, ranks = route_i[:TOP_K], route_i[TOP_K:2 * TOP_K]
    dest = ranks + sum(jnp.where(experts == e, starts[e], 0) for e in range(N_EXPERTS))
    n_blocks = (n * TOP_K) // tm + N_EXPERTS
    block_start = jnp.arange(n_blocks, dtype=jnp.int32) * tm
    block_e = jnp.minimum(jnp.sum(ends[None, :] <= block_start[:, None], axis=1), N_EXPERTS - 1).astype(jnp.int32)
    n_valid = (ends[-1:] // tm).astype(jnp.int32)

    xs = _dispatch(h_packed, dest, n_blocks * tm)
    ys = _experts(xs, block_e, n_valid, w_gate, w_up, w_down, layer, tm, tf)
    part = n // gather_parts
    return route_g, [
        _gather_rows(ys, dest[:, p * part:(p + 1) * part].reshape(TOP_K * part)).reshape(TOP_K, part, d // 2)
        for p in range(gather_parts)]


def _rope_tables(seq):
    pos = jnp.arange(seq, dtype=F32)
    inv_freq = ROPE_THETA ** (-jnp.arange(0, HEAD_DIM, 2, dtype=F32) / HEAD_DIM)
    ang = pos[:, None] * inv_freq[None, :]
    reps = LANES // (HEAD_DIM // 2)
    cos = jnp.tile(jnp.cos(ang), (1, reps))
    sin = jnp.tile(jnp.sin(ang), (1, reps))
    first_half = (jnp.arange(LANES) % HEAD_DIM) < HEAD_DIM // 2
    tabs = (cos, jnp.where(first_half, -sin, sin))
    q_scale = HEAD_DIM ** -0.5 * 1.4426950408889634
    return tuple(jnp.stack([t * q_scale, t]) for t in tabs)


def kernel(x, mix_norm, ffn_norm, attn_w_qkv, attn_w_o, pool_w, pool_scale, dense_w_gate, dense_w_up,
           dense_w_down, moe_router, moe_w_gate, moe_w_up, moe_w_down, final_norm):
    batch, seq, d = x.shape
    depth = mix_norm.shape[0]
    assert depth % 2 == 0 and d == N_HEADS * HEAD_DIM
    assert all(seq % (dil * (win // dil)) == 0 for win, dil in zip(WINDOWS, DILATIONS))
    assert seq % QKV_ROWS == 0 and seq % POOL_ROWS == 0 and moe_router.shape[2] == N_EXPERTS
    rope_tables = _rope_tables(seq)
    xf = x.reshape(batch * seq, d)
    final_gain = final_norm.reshape(1, d)
    pending_moe = None
    w_qkv, w_o, w_gate, w_up, w_down, w_pool = (
        w.astype(BF16) for w in (attn_w_qkv, attn_w_o, dense_w_gate, dense_w_up, dense_w_down, pool_w))
    for i in range(depth):
        j = i // 2
        mix_gain = mix_norm[i].reshape(1, d)
        ffn_gain = ffn_norm[i].reshape(1, d)
        if i % 2 == 0:
            qkv, xf = _qkv_project(xf, mix_gain, w_qkv, j, rope_tables, batch, seq, pending_moe)
            pending_moe = None
            xf = _attention_and_dense_ffn(qkv, xf, w_o, ffn_gain, w_gate, w_up, w_down, j, batch, seq)
        else:
            xf, *routing = _pool_mixer_and_router(xf, mix_gain, w_pool, j, pool_scale[j].reshape(1, d),
                                                  ffn_gain, moe_router[j], batch, seq)
            pending_moe = _moe_layer(xf, routing, moe_w_gate, moe_w_up, moe_w_down, j,
                                     gather_parts=2 if i == depth - 1 else 1)
    return _combine_and_final_norm(xf, *pending_moe, final_gain).reshape(batch, seq, d)
```

```python
import functools

import jax
import jax.numpy as jnp
from jax import lax
from jax.experimental import pallas as pl
from jax.experimental.pallas import tpu as pltpu
from jax.experimental.pallas import tpu_sc as plsc

WINDOWS = (128, 512, 2048)
DILATIONS = (1, 4, 16)
N_GROUPS = len(WINDOWS)
N_HEADS = 16
HEAD_DIM = 64
ROPE_THETA = 10000.0
POOL_SIZES = (2, 4, 8, 16)
N_EXPERTS = 8
TOP_K = 2
RMS_EPS = 1e-6
MASK_VALUE = -1e30
QKV_ROWS = 512
POOL_ROWS = 1024

LANES = 128
MXU_DIM = 256
VMEM_LIMIT_BYTES = 56 * 1024 * 1024
VMEM_LIMIT_HIGH_BYTES = 60 * 1024 * 1024

F32 = jnp.float32
BF16 = jnp.bfloat16


def _params(*sem, vmem_limit_bytes=VMEM_LIMIT_BYTES):
    return pltpu.CompilerParams(dimension_semantics=sem, vmem_limit_bytes=vmem_limit_bytes)


def _rms(x, g):
    ms = jnp.mean(x * x, axis=-1, keepdims=True)
    return x * lax.rsqrt(ms + RMS_EPS) * g


def _silu(g):
    return g / (1.0 + jnp.exp(-g))


def _qkv_kernel(x_ref, g_ref, w_ref, cos_ref, sin_ref, *rest, pending_moe):
    if pending_moe:
        rg_ref, y0_ref, y1_ref = rest[:3]
        o0_ref, o1_ref, o2_ref, xn_ref, acc_ref = rest[-5:]
        rg = rg_ref[...]
        x = x_ref[...] + (rg[:, 0:1] * _unpack_bf16_pairs(y0_ref[...]).astype(F32)
                          + rg[:, 1:2] * _unpack_bf16_pairs(y1_ref[...]).astype(F32))
        xn_ref[...] = x
    else:
        o0_ref, o1_ref, o2_ref, acc_ref = rest
        x = x_ref[...]
    tm, d = x_ref.shape
    n_chunks = acc_ref.shape[1]
    lane = lax.broadcasted_iota(jnp.int32, (1, LANES), 1)
    first_half = (lane % HEAD_DIM) < HEAD_DIM // 2
    h = _rms(x, g_ref[...]).astype(BF16)
    for g, (o_ref, dil) in enumerate(zip((o0_ref, o1_ref, o2_ref), DILATIONS)):
        rows = tm // dil

        def strided(ref, r, dil=dil, rows=rows):
            if dil == 1:
                return ref[...]
            return ref[pl.ds(r, rows, stride=dil), :]

        for kind in range(3):
            col0 = (g * 3 + kind) * d
            acc = jnp.dot(h, w_ref[:, col0:col0 + d], preferred_element_type=F32)
            for c in range(n_chunks):
                acc_ref[kind, c] = acc[:, c * LANES:(c + 1) * LANES]
            for r in range(dil):
                if kind < 2:
                    cos = strided(cos_ref.at[kind], r)
                    sin = strided(sin_ref.at[kind], r)
                for c in range(n_chunks):
                    t = strided(acc_ref.at[kind, c], r)
                    if kind < 2:
                        partner = jnp.where(first_half, pltpu.roll(t, LANES - HEAD_DIM // 2, 1),
                                            pltpu.roll(t, HEAD_DIM // 2, 1))
                        t = t * cos + partner * sin
                    o_ref[kind, r, :, c * LANES:(c + 1) * LANES] = t.astype(BF16)


def _qkv_project(x, gain, w_qkv, layer, tables, batch, seq, pending_moe=None):
    n, d = x.shape
    tm = QKV_ROWS
    vmem_limit = VMEM_LIMIT_HIGH_BYTES if pending_moe else VMEM_LIMIT_BYTES
    tiles_per_seq = seq // tm
    route_g, yg_parts = pending_moe if pending_moe else (None, [None])
    part_tiles = n // tm // len(yg_parts)
    outs = None
    for p, yg in enumerate(yg_parts):
        first = p * part_tiles
        tab = pl.BlockSpec((2, tm, LANES), lambda i: (0, (first + i) % tiles_per_seq, 0))

        def o_spec(dil):
            return pl.BlockSpec(
                (3, None, dil, tm // dil, d),
                lambda i: (0, (first + i) // tiles_per_seq, 0, (first + i) % tiles_per_seq, 0))

        row = pl.BlockSpec((tm, d), lambda i: (first + i, 0))
        in_specs = [row, pl.BlockSpec((1, d), lambda i: (0, 0)),
                    pl.BlockSpec((None,) + w_qkv.shape[1:], lambda i: (layer, 0, 0), pipeline_mode=pl.Buffered(1)),
                    tab, tab]
        args = [x, gain, w_qkv, *tables]
        out_specs = [o_spec(dil) for dil in DILATIONS]
        out_shape = [jax.ShapeDtypeStruct((3, batch, dil, seq // dil, d), BF16) for dil in DILATIONS]
        aliases = {}
        if pending_moe:
            in_specs += [pl.BlockSpec((tm, ROUTE_COLS), lambda i: (first + i, 0))]
            in_specs += [pl.BlockSpec((None, tm, d // 2), lambda i, k=k: (k, i, 0)) for k in range(TOP_K)]
            args += [route_g, yg, yg]
            out_specs.append(row)
            out_shape.append(jax.ShapeDtypeStruct((n, d), F32))
            if outs is not None:
                aliases = {len(args) + q: q for q in range(len(outs))}
                in_specs += [pl.BlockSpec(memory_space=pl.ANY)] * len(outs)
                args += list(outs)
        outs = pl.pallas_call(
            functools.partial(_qkv_kernel, pending_moe=bool(pending_moe)),
            grid=(part_tiles,),
            in_specs=in_specs,
            out_specs=out_specs,
            out_shape=out_shape,
            scratch_shapes=[pltpu.VMEM((3, d // LANES, tm, LANES), F32)],
            input_output_aliases=aliases,
            compiler_params=_params("parallel", vmem_limit_bytes=vmem_limit),
            name="qkv_rope",
        )(*args)
    return [o.reshape(3, batch, seq, d) for o in outs[:N_GROUPS]], (outs[N_GROUPS] if pending_moe else x)


ATTN_TILES_PER_FFN_PIECE = 3


def _attn_dense_kernel(q0, k0, v0, q1, k1, v1, q2, k2, v2, band_ref, causal_ref,
                       x_ref, wo_ref, g_ref, wg_ref, wu_ref, wd_ref, y_ref,
                       o_scr, num_ref, max_ref, den_ref, kt_ref):
    b = pl.program_id(0)
    hp = pl.program_id(1)
    n_pairs, seq = o_scr.shape[1], o_scr.shape[2]
    w = WINDOWS[0] // DILATIONS[0]
    lane = lax.broadcasted_iota(jnp.int32, (1, LANES), 1)
    first_head = lane < HEAD_DIM
    o_out = o_scr.at[b % 2, hp]
    o_in = o_scr.at[(b + 1) % 2]

    def ffn_pieces():
        rows = pl.ds(pl.multiple_of(hp * x_ref.shape[0], x_ref.shape[0]), x_ref.shape[0])
        o = jnp.concatenate([o_in[p, rows, :] for p in range(n_pairs)], axis=1)
        x = x_ref[...] + jnp.dot(o, wo_ref[...], preferred_element_type=F32)
        h = _rms(x, g_ref[...]).astype(BF16)
        yield
        acts = []
        n_chunks = wg_ref.shape[1] // MXU_DIM
        for c in range(n_chunks):
            cols = slice(c * MXU_DIM, (c + 1) * MXU_DIM)
            gate = jnp.dot(h, wg_ref[:, cols], preferred_element_type=F32)
            up = jnp.dot(h, wu_ref[:, cols], preferred_element_type=F32)
            acts.append((_silu(gate) * up).astype(BF16))
            yield
        half = n_chunks // 2
        acc = jnp.dot(jnp.concatenate(acts[:half], axis=1), wd_ref[:half * MXU_DIM, :], preferred_element_type=F32)
        yield
        acc = acc + jnp.dot(jnp.concatenate(acts[half:], axis=1), wd_ref[half * MXU_DIM:, :],
                            preferred_element_type=F32)
        y_ref[...] = x + acc
        yield

    def tile(q_ref, v_ref, qoff, has_prev):
        q = q_ref[qoff:qoff + w, :]
        zero = jnp.zeros_like(q)
        qq = jnp.concatenate([jnp.where(first_head, q, zero), jnp.where(first_head, zero, q)], axis=0)
        k_lo = qoff - w if has_prev else qoff
        s = jnp.dot(qq, kt_ref[:, k_lo:qoff + w], preferred_element_type=F32)
        s = s + (band_ref[...] if has_prev else causal_ref[...])
        mx = jnp.max(s, axis=1, keepdims=True)
        p = jnp.exp2(s - mx)
        den = jnp.sum(p, axis=1, keepdims=True)
        pv = jnp.dot(p.astype(BF16), v_ref[k_lo:qoff + w, :], preferred_element_type=F32)
        return tuple(jnp.where(first_head, a[:w], a[w:]) for a in (pv, mx, den))

    def attention(with_ffn):
        pieces = ffn_pieces() if with_ffn else iter(())
        next(pieces, None)
        tiles_done = 0
        order = sorted(range(N_GROUPS), key=lambda g: -DILATIONS[g])
        for g in order:
            q_ref, k_ref, v_ref = ((q0, k0, v0), (q1, k1, v1), (q2, k2, v2))[g]
            dil = DILATIONS[g]
            sub_len = seq // dil
            nb = sub_len // w
            kt_ref[...] = k_ref[...].T
            for t in range(dil * nb):
                r, n = divmod(t, nb)
                qoff = r * sub_len + n * w
                pv, mx, den = tile(q_ref, v_ref, qoff, n > 0)
                tiles_done += 1
                if tiles_done % ATTN_TILES_PER_FFN_PIECE == 0:
                    next(pieces, None)
                rows = pl.ds(qoff, w) if dil == 1 else pl.ds(r + dil * n * w, w, stride=dil)
                if g == order[0]:
                    num_ref[rows, :] = pv
                    max_ref[rows, :] = mx
                    den_ref[rows, :] = den
                    continue
                old_max = max_ref[rows, :]
                new_max = jnp.maximum(old_max, mx)
                a = jnp.exp2(old_max - new_max)
                c = jnp.exp2(mx - new_max)
                num = a * num_ref[rows, :] + c * pv
                den = a * den_ref[rows, :] + c * den
                if g == order[-1]:
                    o_out[rows, :] = (num / den).astype(BF16)
                else:
                    num_ref[rows, :] = num
                    den_ref[rows, :] = den
                    max_ref[rows, :] = new_max
        for _ in pieces:
            pass

    n_batches = pl.num_programs(0) - 1
    @pl.when(b == 0)
    def _():
        y_ref[...] = jnp.zeros_like(y_ref)
        attention(False)

    pl.when((b > 0) & (b < n_batches))(functools.partial(attention, True))

    @pl.when(b == n_batches)
    def _():
        for _ in ffn_pieces():
            pass


def _attention_biases(w):
    row = jnp.arange(2 * w)[:, None] % w
    col = jnp.arange(2 * w)[None, :]
    dist = row + w - col
    band = jnp.where((dist >= 0) & (dist <= w), 0.0, MASK_VALUE).astype(F32)
    causal = jnp.where(row >= col[:, :w], 0.0, MASK_VALUE).astype(F32)
    return band, causal


def _attention_and_dense_ffn(qkv_groups, x, w_o, gain, w_gate, w_up, w_down, layer, batch, seq):
    n, d = x.shape
    ff = w_gate.shape[2]
    n_pairs = d // LANES
    rows = seq // n_pairs
    last = batch - 1
    in_specs, args = [], []
    for arr in qkv_groups:
        for kind in range(3):
            in_specs.append(pl.BlockSpec(
                (None, None, seq, LANES),
                lambda b, hp, kind=kind: (kind, jnp.minimum(b, last), 0, jnp.where(b > last, n_pairs - 1, hp))))
            args.append(arr)
    const = lambda shape: pl.BlockSpec(shape, lambda b, hp: (0, 0), pipeline_mode=pl.Buffered(1))
    ffn_in = pl.BlockSpec((rows, d), lambda b, hp: (jnp.maximum(b - 1, 0) * n_pairs + hp, 0))
    ffn_out = pl.BlockSpec((rows, d), lambda b, hp: (jnp.where(b == 0, batch, b - 1) * n_pairs + hp, 0))
    biases = _attention_biases(WINDOWS[0] // DILATIONS[0])
    in_specs += [const(t.shape) for t in biases]
    per_layer = lambda shape: pl.BlockSpec((None,) + shape, lambda b, hp: (layer, 0, 0), pipeline_mode=pl.Buffered(1))
    in_specs += [ffn_in, per_layer((d, d)), const((1, d)), per_layer((d, ff)), per_layer((d, ff)), per_layer((ff, d))]
    args += [*biases, x, w_o, gain, w_gate, w_up, w_down]
    return pl.pallas_call(
        _attn_dense_kernel,
        grid=(batch + 1, n_pairs),
        in_specs=in_specs,
        out_specs=ffn_out,
        out_shape=jax.ShapeDtypeStruct((n + seq, d), F32),
        scratch_shapes=[pltpu.VMEM((2, n_pairs, seq, LANES), BF16)]
        + [pltpu.VMEM((seq, LANES), F32)] * 3 + [pltpu.VMEM((LANES, seq), BF16)],
        compiler_params=_params("arbitrary", "arbitrary"),
        name="attention_ffn",
    )(*args)


POOL_HALO = max(POOL_SIZES)


def _pool_tile(i, x_ref, halo_ref, g_ref, w_ref, sc_ref, y_ref):
    ts = x_ref.shape[0]
    g = g_ref[...]
    x = x_ref[...]
    h = _rms(x, g)
    hh = jnp.where(i > 0, _rms(halo_ref[...], g), 0.0)
    ext = jnp.concatenate([hh, h], axis=0)
    pos = i * ts + lax.broadcasted_iota(jnp.int32, (ts, 1), 0)
    pc = w_ref.shape[1]
    for grp, size in enumerate(POOL_SIZES):
        sl = slice(grp * pc, (grp + 1) * pc)
        s = ext[:, sl]
        step = 1
        while step < size:
            s = s + pltpu.roll(s, step, 0)
            step *= 2
        cnt = jnp.minimum(pos + 1, size).astype(F32)
        y = s[POOL_HALO:, :] / cnt - h[:, sl]
        z = jnp.dot(y.astype(BF16), w_ref[grp], preferred_element_type=F32)
        y_ref[:, sl] = x[:, sl] + z * sc_ref[:, sl]


def _pool_router_kernel(x_ref, halo_ref, g_ref, w_ref, sc_ref, g2_ref, wr_ref, tril_ref,
                        y_ref, ri_ref, rg_ref, cnt_ref, hp_ref, run_ref):
    b = pl.program_id(0)
    i = pl.program_id(1)
    _pool_tile(i, x_ref, halo_ref, g_ref, w_ref, sc_ref, y_ref)
    _route_tile((b == 0) & (i == 0), y_ref[...], g2_ref, wr_ref, tril_ref, ri_ref, rg_ref, cnt_ref, hp_ref, run_ref)


def _pool_mixer_and_router(x, gain, pool_w, layer, scale, ffn_gain, w_router, batch, seq):
    d = x.shape[1]
    n = batch * seq
    ne = w_router.shape[1]
    ts = POOL_ROWS
    tiles = seq // ts
    x3 = x.reshape(-1, seq, d)
    hb = ts // POOL_HALO
    tril = (jnp.arange(ts)[:, None] > jnp.arange(ts)[None, :]).astype(BF16)
    const = lambda shape: pl.BlockSpec(shape, lambda b, i: (0,) * len(shape))
    tok = lambda cols: pl.BlockSpec((ts, cols), lambda b, i: (b * tiles + i, 0))
    out, route_i, route_g, counts, h_packed = pl.pallas_call(
        _pool_router_kernel,
        grid=(batch, tiles),
        in_specs=[
            pl.BlockSpec((None, ts, d), lambda b, i: (b, i, 0)),
            pl.BlockSpec((None, POOL_HALO, d), lambda b, i: (b, jnp.maximum(i * hb - 1, 0), 0)),
            const((1, d)), pl.BlockSpec((None,) + pool_w.shape[1:], lambda b, i: (layer, 0, 0, 0)),
            const((1, d)), const((1, d)), const((d, ne)), const((ts, ts)),
        ],
        out_specs=[
            pl.BlockSpec((None, ts, d), lambda b, i: (b, i, 0)),
            pl.BlockSpec((ROUTE_COLS, ts), lambda b, i: (0, b * tiles + i)),
            tok(ROUTE_COLS), const((1, ne)), tok(d // 2),
        ],
        out_shape=[
            jax.ShapeDtypeStruct((batch, seq, d), F32),
            jax.ShapeDtypeStruct((ROUTE_COLS, n), jnp.int32),
            jax.ShapeDtypeStruct((n, ROUTE_COLS), F32),
            jax.ShapeDtypeStruct((1, ne), F32),
            jax.ShapeDtypeStruct((n, d // 2), jnp.int32),
        ],
        scratch_shapes=[pltpu.VMEM((1, ne), F32)],
        compiler_params=_params("arbitrary", "arbitrary"),
        name="pool_router",
    )(x3, x3, gain, pool_w, scale, ffn_gain, w_router, tril)
    return out.reshape(n, d), route_i, route_g, counts, h_packed


ROUTE_COLS = 8


def _pack_bf16_pairs(hr):
    c = hr.shape[1] // 2
    bits = pltpu.bitcast(hr, jnp.int32)
    return (bits[:, c:] & jnp.int32(-65536)) | lax.shift_right_logical(bits[:, :c], 16)


def _unpack_bf16_pairs(packed):
    lo = pltpu.bitcast(lax.shift_left(packed, 16), F32)
    hi = pltpu.bitcast(packed & jnp.int32(-65536), F32)
    return jnp.concatenate([lo.astype(BF16), hi.astype(BF16)], axis=1)


def _route_tile(first, x, g_ref, wr_ref, tril_ref, ri_ref, rg_ref, cnt_ref, hp_ref, run_ref):
    @pl.when(first)
    def _():
        run_ref[...] = jnp.zeros_like(run_ref)

    h = _rms(x, g_ref[...])
    wr = wr_ref[...]
    ne = wr.shape[1]
    h_hi = h.astype(BF16)
    hp_ref[...] = _pack_bf16_pairs(h_hi.astype(F32))
    h_lo = (h - h_hi.astype(F32)).astype(BF16)
    w_hi = wr.astype(BF16)
    w_lo = (wr - w_hi.astype(F32)).astype(BF16)
    hi_part = jnp.dot(h_hi, jnp.concatenate([w_hi, w_lo], axis=1), preferred_element_type=F32)
    logits = hi_part[:, :ne] + hi_part[:, ne:] + jnp.dot(h_lo, w_hi, preferred_element_type=F32)
    tm = logits.shape[0]
    lane = lax.broadcasted_iota(jnp.int32, (tm, ne), 1)
    v1 = jnp.max(logits, axis=1, keepdims=True)
    i1 = jnp.min(jnp.where(logits == v1, lane, ne), axis=1, keepdims=True)
    rest = jnp.where(lane == i1, -jnp.inf, logits)
    v2 = jnp.max(rest, axis=1, keepdims=True)
    i2 = jnp.min(jnp.where(rest == v2, lane, ne), axis=1, keepdims=True)
    e = jnp.exp(v2 - v1)
    g1 = 1.0 / (1.0 + e)
    g2 = e / (1.0 + e)
    oh1 = (lane == i1).astype(F32)
    oh2 = (lane == i2).astype(F32)
    both = oh1 + oh2
    before = jnp.dot(tril_ref[...], both.astype(BF16), preferred_element_type=F32) + run_ref[...]
    r1 = jnp.sum(before * oh1, axis=1, keepdims=True).astype(jnp.int32)
    r2 = jnp.sum(before * oh2, axis=1, keepdims=True).astype(jnp.int32)
    run_ref[...] += jnp.sum(both, axis=0, keepdims=True)
    wide = lax.broadcasted_iota(jnp.int32, (tm, LANES), 1)
    fields = jnp.where(wide == 0, i1, jnp.where(wide == 1, i2, jnp.where(wide == 2, r1, r2)))
    ri_ref[...] = fields.T[:ri_ref.shape[0]]
    rg_ref[...] = jnp.where(lane == 0, g1, g2)
    cnt_ref[...] = run_ref[...]


SC_WINDOW = 64


def _sc_workers():
    info = plsc.get_sparse_core_info()
    mesh = plsc.VectorSubcoreMesh(core_axis_name="core", subcore_axis_name="subcore")
    return mesh, info.num_cores, info.num_cores * info.num_subcores


def _dispatch(x, dest, cap):
    n, d = x.shape
    mesh, n_cores, n_workers = _sc_workers()
    per_worker = n // n_workers

    n_windows = per_worker // SC_WINDOW
    n_bufs = 2
    assert n_windows % n_bufs == 0

    @functools.partial(
        pl.kernel, mesh=mesh, out_type=jax.ShapeDtypeStruct((cap, d), x.dtype),
        scratch_types=[pltpu.VMEM((SC_WINDOW,), jnp.int32)] * (TOP_K * n_bufs)
        + [pltpu.VMEM((SC_WINDOW, d), x.dtype)] * n_bufs + [pltpu.SemaphoreType.DMA] * n_bufs,
        name="moe_dispatch")
    def scatter(x_hbm, *rest):
        idx_hbm, (xs_hbm, *scratch) = rest[:TOP_K], rest[TOP_K:]
        idx_vmem = [scratch[b * TOP_K:(b + 1) * TOP_K] for b in range(n_bufs)]
        rows = scratch[TOP_K * n_bufs:TOP_K * n_bufs + n_bufs]
        sems = scratch[TOP_K * n_bufs + n_bufs:]
        wid = lax.axis_index("subcore") * n_cores + lax.axis_index("core")

        def drain(buf):
            for k in range(TOP_K):
                pltpu.make_async_copy(rows[buf], xs_hbm.at[idx_vmem[buf][k]], sems[buf]).wait()

        @pl.loop(0, n_windows, step=n_bufs)
        def _(c):
            for buf in range(n_bufs):
                @pl.when(c > 0)
                def _(buf=buf):
                    drain(buf)

                window = pl.ds(wid * per_worker + (c + buf) * SC_WINDOW, SC_WINDOW)
                for k in range(TOP_K):
                    pltpu.sync_copy(idx_hbm[k].at[window], idx_vmem[buf][k])
                pltpu.sync_copy(x_hbm.at[window], rows[buf])
                for k in range(TOP_K):
                    pltpu.async_copy(rows[buf], xs_hbm.at[idx_vmem[buf][k]], sems[buf])

        for buf in range(n_bufs):
            drain(buf)

    return scatter(x, *[dest[k] for k in range(TOP_K)])


def _gather_rows(table, idx):
    m = idx.shape[0]
    d = table.shape[1]
    mesh, n_cores, n_workers = _sc_workers()
    per_worker = m // n_workers

    n_windows = per_worker // SC_WINDOW
    n_bufs = 2
    assert n_windows % n_bufs == 0

    @functools.partial(
        pl.kernel, mesh=mesh, out_type=jax.ShapeDtypeStruct((m, d), table.dtype),
        scratch_types=[pltpu.VMEM((SC_WINDOW,), jnp.int32)] * n_bufs
        + [pltpu.VMEM((SC_WINDOW, d), table.dtype)] * n_bufs + [pltpu.SemaphoreType.DMA] * n_bufs,
        name="moe_gather")
    def gather(t_hbm, i_hbm, o_hbm, *scratch):
        idx_vmem, rows, sems = scratch[:n_bufs], scratch[n_bufs:2 * n_bufs], scratch[2 * n_bufs:]
        wid = lax.axis_index("subcore") * n_cores + lax.axis_index("core")

        def window(c):
            return pl.ds(wid * per_worker + c * SC_WINDOW, SC_WINDOW)

        def fetch(c, buf):
            pltpu.sync_copy(i_hbm.at[window(c)], idx_vmem[buf])
            return pltpu.async_copy(t_hbm.at[idx_vmem[buf]], rows[buf], sems[buf])

        fetch(0, 0)

        @pl.loop(0, n_windows, step=n_bufs)
        def _(c):
            for buf in range(n_bufs):
                nxt = c + buf + 1

                @pl.when(nxt < n_windows)
                def _(nxt=nxt, buf=buf):
                    fetch(nxt, (buf + 1) % n_bufs)

                pltpu.make_async_copy(t_hbm.at[idx_vmem[buf]], rows[buf], sems[buf]).wait()
                pltpu.sync_copy(rows[buf], o_hbm.at[window(c + buf)])

    return gather(table, idx)


def _expert_kernel(be_ref, nv_ref, xs_ref, wg_ref, wu_ref, wd_ref, ys_ref, xb_ref, acc_ref):
    del be_ref
    b = pl.program_id(0)
    j = pl.program_id(1)

    def ff_tile(first):
        x = xb_ref[...]
        acts = []
        for c in range(wg_ref.shape[1] // MXU_DIM):
            cols = slice(c * MXU_DIM, (c + 1) * MXU_DIM)
            gate = jnp.dot(x, wg_ref[:, cols].astype(BF16), preferred_element_type=F32)
            up = jnp.dot(x, wu_ref[:, cols].astype(BF16), preferred_element_type=F32)
            acts.append((_silu(gate) * up).astype(BF16))
        part = jnp.dot(jnp.concatenate(acts, axis=1), wd_ref[...].astype(BF16), preferred_element_type=F32)
        if first:
            acc_ref[...] = part
        else:
            acc_ref[...] += part

    valid = b < nv_ref[0]

    @pl.when(valid & (j == 0))
    def _():
        xb_ref[...] = _unpack_bf16_pairs(xs_ref[...])
        ff_tile(True)

    @pl.when(valid & (j > 0))
    def _():
        ff_tile(False)

    @pl.when(jnp.logical_not(valid) & (j == 0))
    def _():
        acc_ref[...] = jnp.zeros_like(acc_ref)

    @pl.when(j == pl.num_programs(1) - 1)
    def _():
        ys_ref[...] = _pack_bf16_pairs(acc_ref[...].astype(BF16).astype(F32))


def _experts(xs, block_e, n_valid, w_gate, w_up, w_down, layer, tm, tf):
    cap = xs.shape[0]
    d = 2 * xs.shape[1]
    ff = w_gate.shape[3]
    nj = ff // tf

    def blk(b, nv):
        return jnp.minimum(b, nv[0] - 1)

    def ffi(b, j, nv):
        return jnp.where(b < nv[0], j, nj - 1)

    return pl.pallas_call(
        _expert_kernel,
        grid_spec=pltpu.PrefetchScalarGridSpec(
            num_scalar_prefetch=2,
            grid=(cap // tm, nj),
            in_specs=[
                pl.BlockSpec((tm, d // 2), lambda b, j, be, nv: (blk(b, nv), 0)),
                pl.BlockSpec((None, None, d, tf), lambda b, j, be, nv: (layer, be[blk(b, nv)], 0, ffi(b, j, nv))),
                pl.BlockSpec((None, None, d, tf), lambda b, j, be, nv: (layer, be[blk(b, nv)], 0, ffi(b, j, nv))),
                pl.BlockSpec((None, None, tf, d), lambda b, j, be, nv: (layer, be[blk(b, nv)], ffi(b, j, nv), 0)),
            ],
            out_specs=pl.BlockSpec((tm, d // 2), lambda b, j, be, nv: (b, 0)),
            scratch_shapes=[pltpu.VMEM((tm, d), BF16), pltpu.VMEM((tm, d), F32)],
        ),
        out_shape=jax.ShapeDtypeStruct((cap, d // 2), jnp.int32),
        compiler_params=_params("arbitrary", "arbitrary"),
        name="moe_experts",
    )(block_e, n_valid, xs, w_gate, w_up, w_down)


def _combine_kernel(x_ref, rg_ref, gf_ref, y0_ref, y1_ref, o_ref):
    rg = rg_ref[...]
    y0 = _unpack_bf16_pairs(y0_ref[...]).astype(F32)
    y1 = _unpack_bf16_pairs(y1_ref[...]).astype(F32)
    o_ref[...] = _rms(x_ref[...] + (rg[:, 0:1] * y0 + rg[:, 1:2] * y1), gf_ref[...])


def _combine_part_kernel(x_ref, rg_ref, gf_ref, y0_ref, y1_ref, *rest):
    _combine_kernel(x_ref, rg_ref, gf_ref, y0_ref, y1_ref, rest[-1])


def _combine_and_final_norm(x, route_g, yg_parts, final_gain):
    n, d = x.shape
    tm = 1024
    part_tiles = n // len(yg_parts) // tm
    out = None
    for p, yg in enumerate(yg_parts):
        first = p * part_tiles
        tok = lambda cols: pl.BlockSpec((tm, cols), lambda i: (first + i, 0))
        in_specs = [tok(d), tok(ROUTE_COLS), pl.BlockSpec((1, d), lambda i: (0, 0)),
                    pl.BlockSpec((None, tm, d // 2), lambda i: (0, i, 0)),
                    pl.BlockSpec((None, tm, d // 2), lambda i: (1, i, 0))]
        args = [x, route_g, final_gain, yg, yg]
        if out is not None:
            in_specs.append(pl.BlockSpec(memory_space=pl.ANY))
            args.append(out)
        out = pl.pallas_call(
            _combine_part_kernel,
            grid=(part_tiles,),
            in_specs=in_specs,
            out_specs=tok(d),
            out_shape=jax.ShapeDtypeStruct((n, d), F32),
            input_output_aliases={} if len(args) == 5 else {5: 0},
            compiler_params=_params("parallel"),
            name="moe_combine",
        )(*args)
    return out


def _moe_layer(x, routing, w_gate, w_up, w_down, layer, gather_parts):
    n, d = x.shape
    tm = 1024 if n * TOP_K >= 8 * 1024 else 256
    tf = 2 * MXU_DIM
    route_i, route_g, counts, h_packed = routing

    counts = counts[0].astype(jnp.int32)
    padded = ((counts + tm - 1) // tm) * tm
    ends = jnp.cumsum(padded)
    starts = ends - padded
    experts, ranks = route_i[:TOP_K], route_i[TOP_K:2 * TOP_K]
    dest = ranks + sum(jnp.where(experts == e, starts[e], 0) for e in range(N_EXPERTS))
    n_blocks = (n * TOP_K) // tm + N_EXPERTS
    block_start = jnp.arange(n_blocks, dtype=jnp.int32) * tm
    block_e = jnp.minimum(jnp.sum(ends[None, :] <= block_start[:, None], axis=1), N_EXPERTS - 1).astype(jnp.int32)
    n_valid = (ends[-1:] // tm).astype(jnp.int32)

    xs = _dispatch(h_packed, dest, n_blocks * tm)
    ys = _experts(xs, block_e, n_valid, w_gate, w_up, w_down, layer, tm, tf)
    part = n // gather_parts
    return route_g, [
        _gather_rows(ys, dest[:, p * part:(p + 1) * part].reshape(TOP_K * part)).reshape(TOP_K, part, d // 2)
        for p in range(gather_parts)]


def _rope_tables(seq):
    pos = jnp.arange(seq, dtype=F32)
    inv_freq = ROPE_THETA ** (-jnp.arange(0, HEAD_DIM, 2, dtype=F32) / HEAD_DIM)
    ang = pos[:, None] * inv_freq[None, :]
    reps = LANES // (HEAD_DIM // 2)
    cos = jnp.tile(jnp.cos(ang), (1, reps))
    sin = jnp.tile(jnp.sin(ang), (1, reps))
    first_half = (jnp.arange(LANES) % HEAD_DIM) < HEAD_DIM // 2
    tabs = (cos, jnp.where(first_half, -sin, sin))
    q_scale = HEAD_DIM ** -0.5 * 1.4426950408889634
    return tuple(jnp.stack([t * q_scale, t]) for t in tabs)


def kernel(x, mix_norm, ffn_norm, attn_w_qkv, attn_w_o, pool_w, pool_scale, dense_w_gate, dense_w_up,
           dense_w_down, moe_router, moe_w_gate, moe_w_up, moe_w_down, final_norm):
    batch, seq, d = x.shape
    depth = mix_norm.shape[0]
    assert depth % 2 == 0 and d == N_HEADS * HEAD_DIM
    assert all(seq % (dil * (win // dil)) == 0 for win, dil in zip(WINDOWS, DILATIONS))
    assert seq % QKV_ROWS == 0 and seq % POOL_ROWS == 0 and moe_router.shape[2] == N_EXPERTS
    rope_tables = _rope_tables(seq)
    xf = x.reshape(batch * seq, d)
    final_gain = final_norm.reshape(1, d)
    pending_moe = None
    w_qkv, w_o, w_gate, w_up, w_down, w_pool = (
        w.astype(BF16) for w in (attn_w_qkv, attn_w_o, dense_w_gate, dense_w_up, dense_w_down, pool_w))
    for i in range(depth):
        j = i // 2
        mix_gain = mix_norm[i].reshape(1, d)
        ffn_gain = ffn_norm[i].reshape(1, d)
        if i % 2 == 0:
            qkv, xf = _qkv_project(xf, mix_gain, w_qkv, j, rope_tables, batch, seq, pending_moe)
            pending_moe = None
            xf = _attention_and_dense_ffn(qkv, xf, w_o, ffn_gain, w_gate, w_up, w_down, j, batch, seq)
        else:
            xf, *routing = _pool_mixer_and_router(xf, mix_gain, w_pool, j, pool_scale[j].reshape(1, d),
                                                  ffn_gain, moe_router[j], batch, seq)
            pending_moe = _moe_layer(xf, routing, moe_w_gate, moe_w_up, moe_w_down, j, gather_parts=2)
    return _combine_and_final_norm(xf, *pending_moe, final_gain).reshape(batch, seq, d)
```
